```python
import math
import jax
import jax.numpy as jnp
from jax import lax
import numpy as np

D_MODEL = 2048
BATCH = 8
SEQ = 2048
DEPTH = 1

GRID_W = 64
CTX_LEN = 256
HEAD_DIM = 128
N_Q_HEADS = D_MODEL // HEAD_DIM
N_KV_HEADS = N_Q_HEADS // 4
Q_PER_KV = N_Q_HEADS // N_KV_HEADS
ATTN_W = N_Q_HEADS * HEAD_DIM
KV_W = N_KV_HEADS * HEAD_DIM
ROPE_AXIS_DIM = HEAD_DIM // 2
ROPE_THETA = 10000.0
Q_BLOCK = 128
ATTN_SCALE = HEAD_DIM ** -0.5
SSM_W = D_MODEL // 2
SSM_GROUP = 16
SSM_GROUPS = SSM_W // SSM_GROUP
SSM_STATE = 64
DT_MIN = 1e-3
DT_MAX = 1e-1
D_FF = ((8 * D_MODEL // 3 + 255) // 256) * 256
N_MOD = 9
N_MOD_CTX_LAST = 5
NORM_EPS = 1e-6
CTX_IN_W = 2 * KV_W + SSM_W
IN_W = CTX_IN_W + ATTN_W + 2 * D_MODEL
SPLITS = [KV_W, 2 * KV_W, CTX_IN_W, CTX_IN_W + ATTN_W]

kernel_name = 'hybrid_s5_gqa_macaron_dit_layer'


def _rms_norm(x, g):
    xf = x.astype(jnp.float32)
    xf = xf * lax.rsqrt(jnp.mean(xf * xf, axis=-1, keepdims=True) + NORM_EPS)
    return xf.astype(x.dtype) * g


def _modulate(h, shift, scale):
    return h * (1 + scale) + shift


def _swiglu(h, w_gate, w_up, w_down):
    return (jax.nn.silu(h @ w_gate) * (h @ w_up)) @ w_down


def _axial_rope_tables(L):
    rows = L // GRID_W
    row_ids = jnp.broadcast_to(jnp.arange(rows)[:, None], (rows, GRID_W)).reshape(-1)
    col_ids = jnp.broadcast_to(jnp.arange(GRID_W)[None, :], (rows, GRID_W)).reshape(-1)
    half = ROPE_AXIS_DIM // 2
    inv_freq = ROPE_THETA ** (-jnp.arange(half, dtype=jnp.float32) / half)
    ang_r = row_ids.astype(jnp.float32)[:, None, None] * inv_freq
    ang_c = col_ids.astype(jnp.float32)[:, None, None] * inv_freq
    return (jnp.cos(ang_r), jnp.sin(ang_r), jnp.cos(ang_c), jnp.sin(ang_c))


def _rope_half(x, cos, sin):
    cos = cos.astype(x.dtype)
    sin = sin.astype(x.dtype)
    x1, x2 = jnp.split(x, 2, axis=-1)
    return jnp.concatenate([x1 * cos - x2 * sin, x2 * cos + x1 * sin], axis=-1)


def _axial_rope(x, tables):
    cos_r, sin_r, cos_c, sin_c = tables
    return jnp.concatenate([_rope_half(x[..., :ROPE_AXIS_DIM], cos_r, sin_r),
                            _rope_half(x[..., ROPE_AXIS_DIM:], cos_c, sin_c)], axis=-1)


def _attend_block(qb, k, v):
    B, T = qb.shape[0], qb.shape[1]
    qg = qb.reshape(B, T, N_KV_HEADS, Q_PER_KV, HEAD_DIM)
    s = jnp.einsum('bqkrd,bskd->bkrqs', qg, k).astype(jnp.float32) * ATTN_SCALE
    p = jax.nn.softmax(s, axis=-1).astype(v.dtype)
    o = jnp.einsum('bkrqs,bskd->bqkrd', p, v)
    return o.reshape(B, T, ATTN_W)


def _blocked_attention(q, k, v):
    B, L = q.shape[0], q.shape[1]
    nb = L // Q_BLOCK
    qb = q.reshape(B, nb, Q_BLOCK, N_Q_HEADS, HEAD_DIM).swapaxes(0, 1)
    o = lax.map(lambda qi: _attend_block(qi, k, v), qb)
    return o.swapaxes(0, 1).reshape(B, L, ATTN_W)


def _zoh(a_re, a_im, log_dt):
    a_re = a_re.astype(jnp.float32)
    a_im = a_im.astype(jnp.float32)
    dt = jnp.exp(log_dt.astype(jnp.float32))[:, None]
    mag = jnp.exp(a_re * dt)
    lb_re = mag * jnp.cos(a_im * dt)
    lb_im = mag * jnp.sin(a_im * dt)
    den = a_re * a_re + a_im * a_im
    coef_re = ((lb_re - 1.0) * a_re + lb_im * a_im) / den
    coef_im = (lb_im * a_re - (lb_re - 1.0) * a_im) / den
    return lb_re, lb_im, coef_re, coef_im


def _drive(u, b_re, b_im, coef_re, coef_im):
    bu_re = jnp.einsum('blgc,gpc->blgp', u, b_re)
    bu_im = jnp.einsum('blgc,gpc->blgp', u, b_im)
    return coef_re * bu_re - coef_im * bu_im, coef_re * bu_im + coef_im * bu_re


def _combine(e1, e2):
    a1r, a1i, b1r, b1i = e1
    a2r, a2i, b2r, b2i = e2
    return (a2r * a1r - a2i * a1i,
            a2r * a1i + a2i * a1r,
            a2r * b1r - a2i * b1i + b2r,
            a2r * b1i + a2i * b1r + b2i)


def _scan(lb_re, lb_im, bu_re, bu_im, reverse, h0=None):
    L = bu_re.shape[1]
    a_re = jnp.broadcast_to(lb_re, (1, L) + lb_re.shape)
    a_im = jnp.broadcast_to(lb_im, (1, L) + lb_im.shape)
    A_re, A_im, s_re, s_im = lax.associative_scan(_combine, (a_re, a_im, bu_re, bu_im),
                                                  reverse=reverse, axis=1)
    if h0 is None:
        return s_re, s_im
    h0_re, h0_im = h0
    return (s_re + A_re * h0_re - A_im * h0_im, s_im + A_re * h0_im + A_im * h0_re)


def _readout(h_re, h_im, c_re, c_im):
    return (jnp.einsum('blgp,gcp->blgc', h_re, c_re)
            - jnp.einsum('blgp,gcp->blgc', h_im, c_im))


def _s5_mixer(u, uc, a_re, a_im, log_dt, b_re, b_im, c_re, c_im, d, ctx_out):
    B, L = u.shape[0], u.shape[1]
    Lc = uc.shape[1]
    uf = u.astype(jnp.float32).reshape(B, L, SSM_GROUPS, SSM_GROUP)
    ucf = uc.astype(jnp.float32).reshape(B, Lc, SSM_GROUPS, SSM_GROUP)
    d_g = d.astype(jnp.float32).reshape(SSM_GROUPS, SSM_GROUP)
    y = d_g * uf
    yc = d_g * ucf if ctx_out else None
    for direction, reverse in ((0, False), (1, True)):
        lb_re, lb_im, coef_re, coef_im = _zoh(a_re[direction], a_im[direction], log_dt[direction])
        br = b_re[direction].astype(jnp.float32)
        bi = b_im[direction].astype(jnp.float32)
        cr = c_re[direction].astype(jnp.float32)
        ci = c_im[direction].astype(jnp.float32)
        dc_re, dc_im = _drive(ucf, br, bi, coef_re, coef_im)
        hc_re, hc_im = _scan(lb_re, lb_im, dc_re, dc_im, reverse)
        edge = slice(0, 1) if reverse else slice(Lc - 1, Lc)
        h0 = (hc_re[:, edge], hc_im[:, edge])
        dl_re, dl_im = _drive(uf, br, bi, coef_re, coef_im)
        h_re, h_im = _scan(lb_re, lb_im, dl_re, dl_im, reverse, h0)
        y = y + _readout(h_re, h_im, cr, ci)
        if ctx_out:
            yc = yc + _readout(hc_re, hc_im, cr, ci)
    y = y.reshape(B, L, SSM_W).astype(u.dtype)
    if ctx_out:
        yc = yc.reshape(B, Lc, SSM_W).astype(u.dtype)
    return y, yc


def _merge(attn, ssm, gate, w_glu, b_glu, w_br_attn, w_br_ssm, w_out):
    y = jax.nn.gelu(ssm)
    y = y * jax.nn.sigmoid(y @ w_glu + b_glu)
    g_attn, g_ssm = jnp.split(jax.nn.sigmoid(gate), 2, axis=-1)
    merged = g_attn * (attn @ w_br_attn) + g_ssm * (y @ w_br_ssm)
    return merged @ w_out


def _token_mixer(h, hc, rope, w_in, q_g, k_g, a_re, a_im, log_dt, b_re, b_im, c_re, c_im, d,
                 w_glu, b_glu, w_br_attn, w_br_ssm, w_out, ctx_out):
    B, L = h.shape[0], h.shape[1]
    Lc = hc.shape[1]
    k, v, u, q, gate = jnp.split(h @ w_in, SPLITS, axis=-1)
    pc = hc @ (w_in if ctx_out else w_in[:, :CTX_IN_W])
    kc, vc, uc = pc[..., :KV_W], pc[..., KV_W:2 * KV_W], pc[..., 2 * KV_W:CTX_IN_W]
    q = _axial_rope(_rms_norm(q.reshape(B, L, N_Q_HEADS, HEAD_DIM), q_g), rope)
    k = _axial_rope(_rms_norm(k.reshape(B, L, N_KV_HEADS, HEAD_DIM), k_g), rope)
    kc = _rms_norm(kc.reshape(B, Lc, N_KV_HEADS, HEAD_DIM), k_g)
    vc = vc.reshape(B, Lc, N_KV_HEADS, HEAD_DIM)
    v = v.reshape(B, L, N_KV_HEADS, HEAD_DIM)
    k_all = jnp.concatenate([kc, k], axis=1)
    v_all = jnp.concatenate([vc, v], axis=1)
    attn = _blocked_attention(q, k_all, v_all)
    ssm, ssm_c = _s5_mixer(u, uc, a_re, a_im, log_dt, b_re, b_im, c_re, c_im, d, ctx_out)
    out = _merge(attn, ssm, gate, w_glu, b_glu, w_br_attn, w_br_ssm, w_out)
    out_c = None
    if ctx_out:
        qc = _rms_norm(pc[..., CTX_IN_W:CTX_IN_W + ATTN_W].reshape(B, Lc, N_Q_HEADS, HEAD_DIM), q_g)
        attn_c = _attend_block(qc, kc, vc)
        out_c = _merge(attn_c, ssm_c, pc[..., CTX_IN_W + ATTN_W:], w_glu, b_glu,
                       w_br_attn, w_br_ssm, w_out)
    return out, out_c


def _fwd_setup_inputs(seed: int = 0) -> dict:
    key = jax.random.key(seed)
    ks = jax.random.split(key, 32)
    f32 = jnp.float32

    def nrm(k, shape, scale):
        return jax.random.normal(k, shape, f32) * scale

    G, P, E = SSM_GROUPS, SSM_STATE, SSM_GROUP
    n_idx = jnp.arange(P, dtype=f32)
    return {
        'x': nrm(ks[0], (BATCH, SEQ, D_MODEL), 1.0),
        'c': nrm(ks[1], (BATCH, D_MODEL), 1.0),
        'ctx': nrm(ks[2], (BATCH, CTX_LEN, D_MODEL), 1.0),
        'c_ctx': nrm(ks[3], (D_MODEL,), 1.0),
        'w_mod': nrm(ks[4], (DEPTH, D_MODEL, N_MOD * D_MODEL), 0.5 * D_MODEL ** -0.5),
        'b_mod': nrm(ks[5], (DEPTH, N_MOD * D_MODEL), 0.01),
        'norm_g': 1.0 + nrm(ks[6], (DEPTH, 3, D_MODEL), 0.02),
        'w_ffn1_gate': nrm(ks[7], (DEPTH, D_MODEL, D_FF), D_MODEL ** -0.5),
        'w_ffn1_up': nrm(ks[8], (DEPTH, D_MODEL, D_FF), D_MODEL ** -0.5),
        'w_ffn1_down': nrm(ks[9], (DEPTH, D_FF, D_MODEL), D_FF ** -0.5),
        'w_in': nrm(ks[10], (DEPTH, D_MODEL, IN_W), D_MODEL ** -0.5),
        'q_norm_g': 1.0 + nrm(ks[11], (DEPTH, HEAD_DIM), 0.02),
        'k_norm_g': 1.0 + nrm(ks[12], (DEPTH, HEAD_DIM), 0.02),
        'ssm_a_re': -0.5 + nrm(ks[13], (DEPTH, 2, G, P), 0.01),
        'ssm_a_im': math.pi * n_idx + nrm(ks[14], (DEPTH, 2, G, P), 0.01),
        'ssm_log_dt': jax.random.uniform(ks[15], (DEPTH, 2, G), f32,
                                         math.log(DT_MIN), math.log(DT_MAX)),
        'ssm_b_re': nrm(ks[16], (DEPTH, 2, G, P, E), (2 * E) ** -0.5),
        'ssm_b_im': nrm(ks[17], (DEPTH, 2, G, P, E), (2 * E) ** -0.5),
        'ssm_c_re': nrm(ks[18], (DEPTH, 2, G, E, P), P ** -0.5),
        'ssm_c_im': nrm(ks[19], (DEPTH, 2, G, E, P), P ** -0.5),
        'ssm_d': nrm(ks[20], (DEPTH, SSM_W), 1.0),
        'w_glu': nrm(ks[21], (DEPTH, SSM_W, SSM_W), SSM_W ** -0.5),
        'b_glu': nrm(ks[22], (DEPTH, SSM_W), 0.01),
        'w_br_attn': nrm(ks[23], (DEPTH, ATTN_W, D_MODEL), ATTN_W ** -0.5),
        'w_br_ssm': nrm(ks[24], (DEPTH, SSM_W, D_MODEL), SSM_W ** -0.5),
        'w_out': nrm(ks[25], (DEPTH, D_MODEL, D_MODEL), D_MODEL ** -0.5),
        'w_ffn2_gate': nrm(ks[26], (DEPTH, D_MODEL, D_FF), D_MODEL ** -0.5),
        'w_ffn2_up': nrm(ks[27], (DEPTH, D_MODEL, D_FF), D_MODEL ** -0.5),
        'w_ffn2_down': nrm(ks[28], (DEPTH, D_FF, D_MODEL), D_FF ** -0.5),
    }


def _fwd_reference(x, c, ctx, c_ctx, w_mod, b_mod, norm_g, w_ffn1_gate, w_ffn1_up, w_ffn1_down,
              w_in, q_norm_g, k_norm_g, ssm_a_re, ssm_a_im, ssm_log_dt, ssm_b_re, ssm_b_im,
              ssm_c_re, ssm_c_im, ssm_d, w_glu, b_glu, w_br_attn, w_br_ssm, w_out,
              w_ffn2_gate, w_ffn2_up, w_ffn2_down):
    L = x.shape[1]
    rope = _axial_rope_tables(L)
    silu_c = jax.nn.silu(c)
    silu_cc = jax.nn.silu(c_ctx)
    for l in range(DEPTH):
        last = l == DEPTH - 1
        n_ctx_mod = N_MOD_CTX_LAST if last else N_MOD
        mod = (silu_c @ w_mod[l] + b_mod[l])[:, None, :]
        sh1, sc1, g1, sh2, sc2, g2, sh3, sc3, g3 = jnp.split(mod, N_MOD, axis=-1)
        mod_c = silu_cc @ w_mod[l][:, :n_ctx_mod * D_MODEL] + b_mod[l][:n_ctx_mod * D_MODEL]
        mc = jnp.split(mod_c, n_ctx_mod, axis=-1)
        ffn1 = (w_ffn1_gate[l], w_ffn1_up[l], w_ffn1_down[l])
        ffn2 = (w_ffn2_gate[l], w_ffn2_up[l], w_ffn2_down[l])
        x = x + 0.5 * g1 * _swiglu(_modulate(_rms_norm(x, norm_g[l, 0]), sh1, sc1), *ffn1)
        ctx = ctx + 0.5 * mc[2] * _swiglu(_modulate(_rms_norm(ctx, norm_g[l, 0]), mc[0], mc[1]), *ffn1)
        h = _modulate(_rms_norm(x, norm_g[l, 1]), sh2, sc2)
        hc = _modulate(_rms_norm(ctx, norm_g[l, 1]), mc[3], mc[4])
        mix, mix_c = _token_mixer(h, hc, rope, w_in[l], q_norm_g[l], k_norm_g[l],
                                  ssm_a_re[l], ssm_a_im[l], ssm_log_dt[l], ssm_b_re[l], ssm_b_im[l],
                                  ssm_c_re[l], ssm_c_im[l], ssm_d[l], w_glu[l], b_glu[l],
                                  w_br_attn[l], w_br_ssm[l], w_out[l], not last)
        x = x + g2 * mix
        x = x + 0.5 * g3 * _swiglu(_modulate(_rms_norm(x, norm_g[l, 2]), sh3, sc3), *ffn2)
        if not last:
            ctx = ctx + mc[5] * mix_c
            ctx = ctx + 0.5 * mc[8] * _swiglu(_modulate(_rms_norm(ctx, norm_g[l, 2]), mc[6], mc[7]), *ffn2)
    return x


import jax as _jax
import jax.numpy as _jnp

TWIN_FORMAT = 'train_step'
FWD_PARAMS = ['x', 'c', 'ctx', 'c_ctx', 'w_mod', 'b_mod', 'norm_g', 'w_ffn1_gate', 'w_ffn1_up', 'w_ffn1_down', 'w_in', 'q_norm_g', 'k_norm_g', 'ssm_a_re', 'ssm_a_im', 'ssm_log_dt', 'ssm_b_re', 'ssm_b_im', 'ssm_c_re', 'ssm_c_im', 'ssm_d', 'w_glu', 'b_glu', 'w_br_attn', 'w_br_ssm', 'w_out', 'w_ffn2_gate', 'w_ffn2_up', 'w_ffn2_down']
TWIN_WEIGHTS = ['c_ctx', 'w_mod', 'b_mod', 'norm_g', 'w_ffn1_gate', 'w_ffn1_up', 'w_ffn1_down', 'w_in', 'q_norm_g', 'k_norm_g', 'ssm_a_re', 'ssm_a_im', 'ssm_log_dt', 'ssm_b_re', 'ssm_b_im', 'ssm_c_re', 'ssm_c_im', 'ssm_d', 'w_glu', 'b_glu', 'w_br_attn', 'w_br_ssm', 'w_out', 'w_ffn2_gate', 'w_ffn2_up', 'w_ffn2_down']
TWIN_DIFF_INPUT = 'x'
TWIN_INPUTS = ['x', 'c', 'ctx', 'c_ctx', 'w_mod', 'b_mod', 'norm_g', 'w_ffn1_gate', 'w_ffn1_up', 'w_ffn1_down', 'w_in', 'q_norm_g', 'k_norm_g', 'ssm_a_re', 'ssm_a_im', 'ssm_log_dt', 'ssm_b_re', 'ssm_b_im', 'ssm_c_re', 'ssm_c_im', 'ssm_d', 'w_glu', 'b_glu', 'w_br_attn', 'w_br_ssm', 'w_out', 'w_ffn2_gate', 'w_ffn2_up', 'w_ffn2_down', 'loss_target', 'm_c_ctx', 'm_w_mod', 'm_b_mod', 'm_norm_g', 'm_w_ffn1_gate', 'm_w_ffn1_up', 'm_w_ffn1_down', 'm_w_in', 'm_q_norm_g', 'm_k_norm_g', 'm_ssm_a_re', 'm_ssm_a_im', 'm_ssm_log_dt', 'm_ssm_b_re', 'm_ssm_b_im', 'm_ssm_c_re', 'm_ssm_c_im', 'm_ssm_d', 'm_w_glu', 'm_b_glu', 'm_w_br_attn', 'm_w_br_ssm', 'm_w_out', 'm_w_ffn2_gate', 'm_w_ffn2_up', 'm_w_ffn2_down', 'v_c_ctx', 'v_w_mod', 'v_b_mod', 'v_norm_g', 'v_w_ffn1_gate', 'v_w_ffn1_up', 'v_w_ffn1_down', 'v_w_in', 'v_q_norm_g', 'v_k_norm_g', 'v_ssm_a_re', 'v_ssm_a_im', 'v_ssm_log_dt', 'v_ssm_b_re', 'v_ssm_b_im', 'v_ssm_c_re', 'v_ssm_c_im', 'v_ssm_d', 'v_w_glu', 'v_b_glu', 'v_w_br_attn', 'v_w_br_ssm', 'v_w_out', 'v_w_ffn2_gate', 'v_w_ffn2_up', 'v_w_ffn2_down']
TWIN_OUTPUTS = ['loss', 'grad_x', 'grad_c_ctx', 'grad_w_mod', 'grad_b_mod', 'grad_norm_g', 'grad_w_ffn1_gate', 'grad_w_ffn1_up', 'grad_w_ffn1_down', 'grad_w_in', 'grad_q_norm_g', 'grad_k_norm_g', 'grad_ssm_a_re', 'grad_ssm_a_im', 'grad_ssm_log_dt', 'grad_ssm_b_re', 'grad_ssm_b_im', 'grad_ssm_c_re', 'grad_ssm_c_im', 'grad_ssm_d', 'grad_w_glu', 'grad_b_glu', 'grad_w_br_attn', 'grad_w_br_ssm', 'grad_w_out', 'grad_w_ffn2_gate', 'grad_w_ffn2_up', 'grad_w_ffn2_down', 'delta_c_ctx', 'delta_w_mod', 'delta_b_mod', 'delta_norm_g', 'delta_w_ffn1_gate', 'delta_w_ffn1_up', 'delta_w_ffn1_down', 'delta_w_in', 'delta_q_norm_g', 'delta_k_norm_g', 'delta_ssm_a_re', 'delta_ssm_a_im', 'delta_ssm_log_dt', 'delta_ssm_b_re', 'delta_ssm_b_im', 'delta_ssm_c_re', 'delta_ssm_c_im', 'delta_ssm_d', 'delta_w_glu', 'delta_b_glu', 'delta_w_br_attn', 'delta_w_br_ssm', 'delta_w_out', 'delta_w_ffn2_gate', 'delta_w_ffn2_up', 'delta_w_ffn2_down', 'new_m_c_ctx', 'new_m_w_mod', 'new_m_b_mod', 'new_m_norm_g', 'new_m_w_ffn1_gate', 'new_m_w_ffn1_up', 'new_m_w_ffn1_down', 'new_m_w_in', 'new_m_q_norm_g', 'new_m_k_norm_g', 'new_m_ssm_a_re', 'new_m_ssm_a_im', 'new_m_ssm_log_dt', 'new_m_ssm_b_re', 'new_m_ssm_b_im', 'new_m_ssm_c_re', 'new_m_ssm_c_im', 'new_m_ssm_d', 'new_m_w_glu', 'new_m_b_glu', 'new_m_w_br_attn', 'new_m_w_br_ssm', 'new_m_w_out', 'new_m_w_ffn2_gate', 'new_m_w_ffn2_up', 'new_m_w_ffn2_down', 'new_v_c_ctx', 'new_v_w_mod', 'new_v_b_mod', 'new_v_norm_g', 'new_v_w_ffn1_gate', 'new_v_w_ffn1_up', 'new_v_w_ffn1_down', 'new_v_w_in', 'new_v_q_norm_g', 'new_v_k_norm_g', 'new_v_ssm_a_re', 'new_v_ssm_a_im', 'new_v_ssm_log_dt', 'new_v_ssm_b_re', 'new_v_ssm_b_im', 'new_v_ssm_c_re', 'new_v_ssm_c_im', 'new_v_ssm_d', 'new_v_w_glu', 'new_v_b_glu', 'new_v_w_br_attn', 'new_v_w_br_ssm', 'new_v_w_out', 'new_v_w_ffn2_gate', 'new_v_w_ffn2_up', 'new_v_w_ffn2_down']
TWIN_LEAF_KINDS = {'loss': 'loss', 'grad_x': 'grad_x', 'grad_c_ctx': 'grad_w', 'grad_w_mod': 'grad_w', 'grad_b_mod': 'grad_w', 'grad_norm_g': 'grad_w', 'grad_w_ffn1_gate': 'grad_w', 'grad_w_ffn1_up': 'grad_w', 'grad_w_ffn1_down': 'grad_w', 'grad_w_in': 'grad_w', 'grad_q_norm_g': 'grad_w', 'grad_k_norm_g': 'grad_w', 'grad_ssm_a_re': 'grad_w', 'grad_ssm_a_im': 'grad_w', 'grad_ssm_log_dt': 'grad_w', 'grad_ssm_b_re': 'grad_w', 'grad_ssm_b_im': 'grad_w', 'grad_ssm_c_re': 'grad_w', 'grad_ssm_c_im': 'grad_w', 'grad_ssm_d': 'grad_w', 'grad_w_glu': 'grad_w', 'grad_b_glu': 'grad_w', 'grad_w_br_attn': 'grad_w', 'grad_w_br_ssm': 'grad_w', 'grad_w_out': 'grad_w', 'grad_w_ffn2_gate': 'grad_w', 'grad_w_ffn2_up': 'grad_w', 'grad_w_ffn2_down': 'grad_w', 'delta_c_ctx': 'delta_w', 'delta_w_mod': 'delta_w', 'delta_b_mod': 'delta_w', 'delta_norm_g': 'delta_w', 'delta_w_ffn1_gate': 'delta_w', 'delta_w_ffn1_up': 'delta_w', 'delta_w_ffn1_down': 'delta_w', 'delta_w_in': 'delta_w', 'delta_q_norm_g': 'delta_w', 'delta_k_norm_g': 'delta_w', 'delta_ssm_a_re': 'delta_w', 'delta_ssm_a_im': 'delta_w', 'delta_ssm_log_dt': 'delta_w', 'delta_ssm_b_re': 'delta_w', 'delta_ssm_b_im': 'delta_w', 'delta_ssm_c_re': 'delta_w', 'delta_ssm_c_im': 'delta_w', 'delta_ssm_d': 'delta_w', 'delta_w_glu': 'delta_w', 'delta_b_glu': 'delta_w', 'delta_w_br_attn': 'delta_w', 'delta_w_br_ssm': 'delta_w', 'delta_w_out': 'delta_w', 'delta_w_ffn2_gate': 'delta_w', 'delta_w_ffn2_up': 'delta_w', 'delta_w_ffn2_down': 'delta_w', 'new_m_c_ctx': 'new_m', 'new_m_w_mod': 'new_m', 'new_m_b_mod': 'new_m', 'new_m_norm_g': 'new_m', 'new_m_w_ffn1_gate': 'new_m', 'new_m_w_ffn1_up': 'new_m', 'new_m_w_ffn1_down': 'new_m', 'new_m_w_in': 'new_m', 'new_m_q_norm_g': 'new_m', 'new_m_k_norm_g': 'new_m', 'new_m_ssm_a_re': 'new_m', 'new_m_ssm_a_im': 'new_m', 'new_m_ssm_log_dt': 'new_m', 'new_m_ssm_b_re': 'new_m', 'new_m_ssm_b_im': 'new_m', 'new_m_ssm_c_re': 'new_m', 'new_m_ssm_c_im': 'new_m', 'new_m_ssm_d': 'new_m', 'new_m_w_glu': 'new_m', 'new_m_b_glu': 'new_m', 'new_m_w_br_attn': 'new_m', 'new_m_w_br_ssm': 'new_m', 'new_m_w_out': 'new_m', 'new_m_w_ffn2_gate': 'new_m', 'new_m_w_ffn2_up': 'new_m', 'new_m_w_ffn2_down': 'new_m', 'new_v_c_ctx': 'new_v', 'new_v_w_mod': 'new_v', 'new_v_b_mod': 'new_v', 'new_v_norm_g': 'new_v', 'new_v_w_ffn1_gate': 'new_v', 'new_v_w_ffn1_up': 'new_v', 'new_v_w_ffn1_down': 'new_v', 'new_v_w_in': 'new_v', 'new_v_q_norm_g': 'new_v', 'new_v_k_norm_g': 'new_v', 'new_v_ssm_a_re': 'new_v', 'new_v_ssm_a_im': 'new_v', 'new_v_ssm_log_dt': 'new_v', 'new_v_ssm_b_re': 'new_v', 'new_v_ssm_b_im': 'new_v', 'new_v_ssm_c_re': 'new_v', 'new_v_ssm_c_im': 'new_v', 'new_v_ssm_d': 'new_v', 'new_v_w_glu': 'new_v', 'new_v_b_glu': 'new_v', 'new_v_w_br_attn': 'new_v', 'new_v_w_br_ssm': 'new_v', 'new_v_w_out': 'new_v', 'new_v_w_ffn2_gate': 'new_v', 'new_v_w_ffn2_up': 'new_v', 'new_v_w_ffn2_down': 'new_v'}


def _forward(args):
    return _fwd_reference(*[args[k] for k in FWD_PARAMS])


def _output_shape():
    out = _jax.eval_shape(lambda: _forward(_fwd_setup_inputs(0)))
    return out.shape, out.dtype

N_MICROBATCH = 1
ADAM_LR = 0.001
ADAM_B1 = 0.9
ADAM_B2 = 0.999
ADAM_EPS = 1e-08
ADAM_WD = 0.01
ADAM_STEP = 10
PER_EXAMPLE_BATCH_AXIS = {'x': 0, 'c': 0, 'ctx': 0, 'loss_target': 0}
SHARED_INPUTS = []
_WEIGHT_DTYPES = {'c_ctx': _jnp.float32, 'w_mod': _jnp.float32, 'b_mod': _jnp.float32, 'norm_g': _jnp.float32, 'w_ffn1_gate': _jnp.float32, 'w_ffn1_up': _jnp.float32, 'w_ffn1_down': _jnp.float32, 'w_in': _jnp.float32, 'q_norm_g': _jnp.float32, 'k_norm_g': _jnp.float32, 'ssm_a_re': _jnp.float32, 'ssm_a_im': _jnp.float32, 'ssm_log_dt': _jnp.float32, 'ssm_b_re': _jnp.float32, 'ssm_b_im': _jnp.float32, 'ssm_c_re': _jnp.float32, 'ssm_c_im': _jnp.float32, 'ssm_d': _jnp.float32, 'w_glu': _jnp.float32, 'b_glu': _jnp.float32, 'w_br_attn': _jnp.float32, 'w_br_ssm': _jnp.float32, 'w_out': _jnp.float32, 'w_ffn2_gate': _jnp.float32, 'w_ffn2_up': _jnp.float32, 'w_ffn2_down': _jnp.float32}
MOMENT_SCALE = {'c_ctx': 5.476764e-03, 'w_mod': 5.751832e-02, 'b_mod': 1.283509e-01, 'norm_g': 1.619169e-01, 'w_ffn1_gate': 7.192950e-03, 'w_ffn1_up': 6.589574e-03, 'w_ffn1_down': 1.073737e-02, 'w_in': 1.286561e-02, 'q_norm_g': 7.523533e-03, 'k_norm_g': 7.404608e-03, 'ssm_a_re': 2.926013e-03, 'ssm_a_im': 1.742963e-03, 'ssm_log_dt': 3.077803e-01, 'ssm_b_re': 1.180361e-03, 'ssm_b_im': 1.471652e-03, 'ssm_c_re': 1.963006e-03, 'ssm_c_im': 1.696714e-03, 'ssm_d': 7.990879e-02, 'w_glu': 1.687987e-02, 'b_glu': 4.670305e-02, 'w_br_attn': 1.776386e-02, 'w_br_ssm': 1.384758e-02, 'w_out': 1.813784e-02, 'w_ffn2_gate': 7.176553e-03, 'w_ffn2_up': 6.548371e-03, 'w_ffn2_down': 1.062282e-02}


def _to_microbatches(a, axis):
    t = _jnp.moveaxis(a, axis, 0)
    t = t.reshape((N_MICROBATCH, t.shape[0] // N_MICROBATCH) + t.shape[1:])
    return _jnp.moveaxis(t, 1, axis + 1)


def setup_inputs(seed: int = 0) -> dict:
    inp = _fwd_setup_inputs(seed)
    key = _jax.random.fold_in(_jax.random.key(seed), 7919)
    shape, _ = _output_shape()
    out = dict(inp)
    out["loss_target"] = _jax.random.normal(_jax.random.fold_in(key, 0), shape, _jnp.float32)
    for i, name in enumerate(TWIN_WEIGHTS):
        w = inp[name].astype(_jnp.float32)
        if MOMENT_SCALE is None:
            s = _jnp.sqrt(_jnp.mean(_jnp.square(w)) + 1e-30)
        else:
            s = MOMENT_SCALE[name]
        km, kv = _jax.random.split(_jax.random.fold_in(key, i + 1))
        out[name] = w
        out["m_" + name] = s * _jax.random.normal(km, w.shape, _jnp.float32)
        out["v_" + name] = (s * s) * _jax.random.uniform(kv, w.shape, _jnp.float32, 0.5, 1.5)
    if N_MICROBATCH > 1:
        for name, axis in PER_EXAMPLE_BATCH_AXIS.items():
            out[name] = _to_microbatches(out[name], axis)
    return {'x': out['x'], 'c': out['c'], 'ctx': out['ctx'], 'c_ctx': out['c_ctx'], 'w_mod': out['w_mod'], 'b_mod': out['b_mod'], 'norm_g': out['norm_g'], 'w_ffn1_gate': out['w_ffn1_gate'], 'w_ffn1_up': out['w_ffn1_up'], 'w_ffn1_down': out['w_ffn1_down'], 'w_in': out['w_in'], 'q_norm_g': out['q_norm_g'], 'k_norm_g': out['k_norm_g'], 'ssm_a_re': out['ssm_a_re'], 'ssm_a_im': out['ssm_a_im'], 'ssm_log_dt': out['ssm_log_dt'], 'ssm_b_re': out['ssm_b_re'], 'ssm_b_im': out['ssm_b_im'], 'ssm_c_re': out['ssm_c_re'], 'ssm_c_im': out['ssm_c_im'], 'ssm_d': out['ssm_d'], 'w_glu': out['w_glu'], 'b_glu': out['b_glu'], 'w_br_attn': out['w_br_attn'], 'w_br_ssm': out['w_br_ssm'], 'w_out': out['w_out'], 'w_ffn2_gate': out['w_ffn2_gate'], 'w_ffn2_up': out['w_ffn2_up'], 'w_ffn2_down': out['w_ffn2_down'], 'loss_target': out['loss_target'], 'm_c_ctx': out['m_c_ctx'], 'm_w_mod': out['m_w_mod'], 'm_b_mod': out['m_b_mod'], 'm_norm_g': out['m_norm_g'], 'm_w_ffn1_gate': out['m_w_ffn1_gate'], 'm_w_ffn1_up': out['m_w_ffn1_up'], 'm_w_ffn1_down': out['m_w_ffn1_down'], 'm_w_in': out['m_w_in'], 'm_q_norm_g': out['m_q_norm_g'], 'm_k_norm_g': out['m_k_norm_g'], 'm_ssm_a_re': out['m_ssm_a_re'], 'm_ssm_a_im': out['m_ssm_a_im'], 'm_ssm_log_dt': out['m_ssm_log_dt'], 'm_ssm_b_re': out['m_ssm_b_re'], 'm_ssm_b_im': out['m_ssm_b_im'], 'm_ssm_c_re': out['m_ssm_c_re'], 'm_ssm_c_im': out['m_ssm_c_im'], 'm_ssm_d': out['m_ssm_d'], 'm_w_glu': out['m_w_glu'], 'm_b_glu': out['m_b_glu'], 'm_w_br_attn': out['m_w_br_attn'], 'm_w_br_ssm': out['m_w_br_ssm'], 'm_w_out': out['m_w_out'], 'm_w_ffn2_gate': out['m_w_ffn2_gate'], 'm_w_ffn2_up': out['m_w_ffn2_up'], 'm_w_ffn2_down': out['m_w_ffn2_down'], 'v_c_ctx': out['v_c_ctx'], 'v_w_mod': out['v_w_mod'], 'v_b_mod': out['v_b_mod'], 'v_norm_g': out['v_norm_g'], 'v_w_ffn1_gate': out['v_w_ffn1_gate'], 'v_w_ffn1_up': out['v_w_ffn1_up'], 'v_w_ffn1_down': out['v_w_ffn1_down'], 'v_w_in': out['v_w_in'], 'v_q_norm_g': out['v_q_norm_g'], 'v_k_norm_g': out['v_k_norm_g'], 'v_ssm_a_re': out['v_ssm_a_re'], 'v_ssm_a_im': out['v_ssm_a_im'], 'v_ssm_log_dt': out['v_ssm_log_dt'], 'v_ssm_b_re': out['v_ssm_b_re'], 'v_ssm_b_im': out['v_ssm_b_im'], 'v_ssm_c_re': out['v_ssm_c_re'], 'v_ssm_c_im': out['v_ssm_c_im'], 'v_ssm_d': out['v_ssm_d'], 'v_w_glu': out['v_w_glu'], 'v_b_glu': out['v_b_glu'], 'v_w_br_attn': out['v_w_br_attn'], 'v_w_br_ssm': out['v_w_br_ssm'], 'v_w_out': out['v_w_out'], 'v_w_ffn2_gate': out['v_w_ffn2_gate'], 'v_w_ffn2_up': out['v_w_ffn2_up'], 'v_w_ffn2_down': out['v_w_ffn2_down']}


def _loss(weights, diff, rest, loss_target):
    with _jax.named_scope("forward"):
        args = {**rest, TWIN_DIFF_INPUT: diff, **{k: w.astype(_WEIGHT_DTYPES[k]) for k, w in weights.items()}}
        y = _forward(args)
    with _jax.named_scope("loss_head"):
        err = _jnp.square(y.astype(_jnp.float32) - loss_target)
        return 0.5 * _jnp.sum(_jnp.mean(err, axis=-1)) if err.ndim else 0.5 * err


def _adamw(w, g, m, v):
    m = ADAM_B1 * m + (1.0 - ADAM_B1) * g
    v = ADAM_B2 * v + (1.0 - ADAM_B2) * _jnp.square(g)
    m_hat = m / (1.0 - ADAM_B1 ** ADAM_STEP)
    v_hat = v / (1.0 - ADAM_B2 ** ADAM_STEP)
    delta = -ADAM_LR * (m_hat / (_jnp.sqrt(v_hat) + ADAM_EPS) + ADAM_WD * w)
    return delta, m, v


def reference(x, c, ctx, c_ctx, w_mod, b_mod, norm_g, w_ffn1_gate, w_ffn1_up, w_ffn1_down, w_in, q_norm_g, k_norm_g, ssm_a_re, ssm_a_im, ssm_log_dt, ssm_b_re, ssm_b_im, ssm_c_re, ssm_c_im, ssm_d, w_glu, b_glu, w_br_attn, w_br_ssm, w_out, w_ffn2_gate, w_ffn2_up, w_ffn2_down, loss_target, m_c_ctx, m_w_mod, m_b_mod, m_norm_g, m_w_ffn1_gate, m_w_ffn1_up, m_w_ffn1_down, m_w_in, m_q_norm_g, m_k_norm_g, m_ssm_a_re, m_ssm_a_im, m_ssm_log_dt, m_ssm_b_re, m_ssm_b_im, m_ssm_c_re, m_ssm_c_im, m_ssm_d, m_w_glu, m_b_glu, m_w_br_attn, m_w_br_ssm, m_w_out, m_w_ffn2_gate, m_w_ffn2_up, m_w_ffn2_down, v_c_ctx, v_w_mod, v_b_mod, v_norm_g, v_w_ffn1_gate, v_w_ffn1_up, v_w_ffn1_down, v_w_in, v_q_norm_g, v_k_norm_g, v_ssm_a_re, v_ssm_a_im, v_ssm_log_dt, v_ssm_b_re, v_ssm_b_im, v_ssm_c_re, v_ssm_c_im, v_ssm_d, v_w_glu, v_b_glu, v_w_br_attn, v_w_br_ssm, v_w_out, v_w_ffn2_gate, v_w_ffn2_up, v_w_ffn2_down):
    given = dict(x=x, c=c, ctx=ctx, c_ctx=c_ctx, w_mod=w_mod, b_mod=b_mod, norm_g=norm_g, w_ffn1_gate=w_ffn1_gate, w_ffn1_up=w_ffn1_up, w_ffn1_down=w_ffn1_down, w_in=w_in, q_norm_g=q_norm_g, k_norm_g=k_norm_g, ssm_a_re=ssm_a_re, ssm_a_im=ssm_a_im, ssm_log_dt=ssm_log_dt, ssm_b_re=ssm_b_re, ssm_b_im=ssm_b_im, ssm_c_re=ssm_c_re, ssm_c_im=ssm_c_im, ssm_d=ssm_d, w_glu=w_glu, b_glu=b_glu, w_br_attn=w_br_attn, w_br_ssm=w_br_ssm, w_out=w_out, w_ffn2_gate=w_ffn2_gate, w_ffn2_up=w_ffn2_up, w_ffn2_down=w_ffn2_down, loss_target=loss_target, m_c_ctx=m_c_ctx, m_w_mod=m_w_mod, m_b_mod=m_b_mod, m_norm_g=m_norm_g, m_w_ffn1_gate=m_w_ffn1_gate, m_w_ffn1_up=m_w_ffn1_up, m_w_ffn1_down=m_w_ffn1_down, m_w_in=m_w_in, m_q_norm_g=m_q_norm_g, m_k_norm_g=m_k_norm_g, m_ssm_a_re=m_ssm_a_re, m_ssm_a_im=m_ssm_a_im, m_ssm_log_dt=m_ssm_log_dt, m_ssm_b_re=m_ssm_b_re, m_ssm_b_im=m_ssm_b_im, m_ssm_c_re=m_ssm_c_re, m_ssm_c_im=m_ssm_c_im, m_ssm_d=m_ssm_d, m_w_glu=m_w_glu, m_b_glu=m_b_glu, m_w_br_attn=m_w_br_attn, m_w_br_ssm=m_w_br_ssm, m_w_out=m_w_out, m_w_ffn2_gate=m_w_ffn2_gate, m_w_ffn2_up=m_w_ffn2_up, m_w_ffn2_down=m_w_ffn2_down, v_c_ctx=v_c_ctx, v_w_mod=v_w_mod, v_b_mod=v_b_mod, v_norm_g=v_norm_g, v_w_ffn1_gate=v_w_ffn1_gate, v_w_ffn1_up=v_w_ffn1_up, v_w_ffn1_down=v_w_ffn1_down, v_w_in=v_w_in, v_q_norm_g=v_q_norm_g, v_k_norm_g=v_k_norm_g, v_ssm_a_re=v_ssm_a_re, v_ssm_a_im=v_ssm_a_im, v_ssm_log_dt=v_ssm_log_dt, v_ssm_b_re=v_ssm_b_re, v_ssm_b_im=v_ssm_b_im, v_ssm_c_re=v_ssm_c_re, v_ssm_c_im=v_ssm_c_im, v_ssm_d=v_ssm_d, v_w_glu=v_w_glu, v_b_glu=v_b_glu, v_w_br_attn=v_w_br_attn, v_w_br_ssm=v_w_br_ssm, v_w_out=v_w_out, v_w_ffn2_gate=v_w_ffn2_gate, v_w_ffn2_up=v_w_ffn2_up, v_w_ffn2_down=v_w_ffn2_down)
    weights = {n: given[n] for n in TWIN_WEIGHTS}
    shared = {n: given[n] for n in SHARED_INPUTS}
    per_example = {n: given[n] for n in ['x', 'c', 'ctx']}
    grad_fn = _jax.value_and_grad(_loss, argnums=(0, 1))

    def one_microbatch(ex, loss_target):
        ex = dict(ex)
        diff = ex.pop(TWIN_DIFF_INPUT)
        return grad_fn(weights, diff, {**shared, **ex}, loss_target)

    if N_MICROBATCH == 1:
        loss, (grad_w, grad_x) = one_microbatch(per_example, given["loss_target"])
    else:
        def body(carry, xs):
            loss_sum, grad_sum = carry
            l_k, (gw_k, gx_k) = one_microbatch(xs[0], xs[1])
            with _jax.named_scope("update"):
                return (loss_sum + l_k, _jax.tree.map(_jnp.add, grad_sum, gw_k)), gx_k

        init = (_jnp.zeros((), _jnp.float32), _jax.tree.map(_jnp.zeros_like, weights))
        (loss, grad_w), grad_x = _jax.lax.scan(body, init, (per_example, given["loss_target"]))
    with _jax.named_scope("update"):
        delta_w, new_m, new_v = {}, {}, {}
        for n in TWIN_WEIGHTS:
            delta_w[n], new_m[n], new_v[n] = _adamw(weights[n], grad_w[n], given["m_" + n], given["v_" + n])
    return (loss, grad_x, *[grad_w[n] for n in TWIN_WEIGHTS], *[delta_w[n] for n in TWIN_WEIGHTS],
            *[new_m[n] for n in TWIN_WEIGHTS], *[new_v[n] for n in TWIN_WEIGHTS])
```

```python
import math

import jax
import jax.numpy as jnp
from jax import lax
from jax.experimental import pallas as pl
from jax.experimental.pallas import tpu as pltpu

F32 = jnp.float32
BF16 = jnp.bfloat16
MESH = pl.DeviceIdType.MESH

NORM_EPS = 1e-6
ROPE_THETA = 10000.0
GRID_W = 64
HEAD_DIM = 128
Q_PER_KV = 4
SSM_GROUP = 16
SSM_STATE = 64
ADAM_LR = 0.001
ADAM_B1 = 0.9
ADAM_B2 = 0.999
ADAM_EPS = 1e-08
ADAM_WD = 0.01
ADAM_STEP = 10

N_CHIPS = 4
N_DEV = 8
LANES = 128
SLAB_CH = 128
SLAB_GROUPS = SLAB_CH // SSM_GROUP
SLAB_ST = SLAB_GROUPS * SSM_STATE
VMEM_LIMIT_BYTES = 56 * 1024 * 1024
PACK_W = 1024


def _cparams(**kw):
    return pltpu.CompilerParams(vmem_limit_bytes=VMEM_LIMIT_BYTES, **kw)


def _div(n, pref, mult=LANES):
    t = (min(pref, n) // mult) * mult
    while t >= mult:
        if n % t == 0:
            return t
        t -= mult
    return n


def _sigmoid(x):
    return jax.nn.sigmoid(x)


def _gelu(x):
    return x * (0.5 * (1.0 + jnp.tanh(math.sqrt(2.0 / math.pi) * (x + 0.044715 * (x * x * x)))))


def _rowk(name, fn, nrows, tr, ins, outs, nc=0):
    nt = nrows // tr
    in_specs, arrays = [], []
    for arr, kind in ins:
        arrays.append(arr)
        if kind == 'row':
            in_specs.append(pl.BlockSpec((tr, arr.shape[1]), lambda i: (i, 0)))
        elif kind == 'xrow':
            in_specs.append(pl.BlockSpec((tr, arr.shape[1]), lambda i: (jnp.maximum(i - nc, 0), 0)))
        elif kind == 'orow':
            in_specs.append(pl.BlockSpec((tr, arr.shape[1]), lambda i: (i + nc, 0)))
        elif kind == 'vec':
            in_specs.append(pl.BlockSpec(arr.shape, lambda i, nd=arr.ndim: (0,) * nd))
        elif kind == 'row3':
            in_specs.append(pl.BlockSpec((arr.shape[0], tr, arr.shape[2]), lambda i: (0, i, 0)))
        elif kind[0] == 'ocol':
            _, width, blk = kind
            in_specs.append(pl.BlockSpec((tr, width), lambda i, blk=blk: (i + nc, blk)))
        else:
            _, width, blk = kind
            in_specs.append(pl.BlockSpec((tr, width), lambda i, blk=blk: (i, blk)))
    out_shape, out_specs = [], []
    for shape, dtype, kind in outs:
        out_shape.append(jax.ShapeDtypeStruct(shape, dtype))
        if kind == 'row':
            out_specs.append(pl.BlockSpec((tr, shape[1]), lambda i: (i, 0)))
        else:
            out_specs.append(pl.BlockSpec(shape, lambda i, nd=len(shape): (0,) * nd))
    nin = len(ins)

    def body(*refs):
        i = pl.program_id(0)
        res = fn(i, *[r[...] for r in refs[:nin]])
        for (shape, dtype, kind), ref, val in zip(outs, refs[nin:], res):
            if kind == 'row':
                ref[...] = val.astype(dtype)
            else:
                @pl.when(i == 0)
                def _():
                    ref[...] = val.astype(dtype)

                @pl.when(i > 0)
                def _():
                    ref[...] += val.astype(dtype)

    return pl.pallas_call(body, name=name, grid=(nt,), in_specs=in_specs, out_specs=out_specs,
                          out_shape=out_shape, compiler_params=_cparams())(*arrays)


def _mm(name, pairs, M, N, *, tm, tn, nk=1, epi, outs, ta=False, tb=False, rows=(), vecs=(),
        a_pro=None, b_pro=None, n_outer=True):
    nm, nn = M // tm, N // tn
    npair = len(pairs)

    def idx(f):
        if n_outer:
            return lambda j, i, k: f(i, j, k)
        return lambda i, j, k: f(i, j, k)

    in_specs, args = [], []
    for a, b, K in pairs:
        tk = K // nk
        if ta:
            in_specs.append(pl.BlockSpec((tk, tm), idx(lambda i, j, k: (k, i))))
        else:
            in_specs.append(pl.BlockSpec((tm, tk), idx(lambda i, j, k: (i, k))))
        args.append(a)
        if b.ndim == 3:
            if tb:
                per = b.shape[2] // tk
                in_specs.append(pl.BlockSpec((None, tn, tk), idx(lambda i, j, k, per=per: (k // per, j, k % per))))
            else:
                per = b.shape[2] // tn
                in_specs.append(pl.BlockSpec((None, tk, tn), idx(lambda i, j, k, per=per: (j // per, k, j % per))))
        elif tb:
            in_specs.append(pl.BlockSpec((tn, tk), idx(lambda i, j, k: (j, k))))
        else:
            in_specs.append(pl.BlockSpec((tk, tn), idx(lambda i, j, k: (k, j))))
        args.append(b)
    for arr, ro, co in rows:
        in_specs.append(pl.BlockSpec((tm, tn), idx(lambda i, j, k, ro=ro, co=co: (i + ro, j + co))))
        args.append(arr)
    for arr in vecs:
        in_specs.append(pl.BlockSpec((arr.shape[0], tn), idx(lambda i, j, k: (0, j))))
        args.append(arr)
    out_shape, out_specs = [], []
    for dtype, chunked in outs:
        if chunked:
            per = (N // N_CHIPS) // tn
            out_shape.append(jax.ShapeDtypeStruct((N_CHIPS, M, N // N_CHIPS), dtype))
            out_specs.append(pl.BlockSpec((None, tm, tn), idx(lambda i, j, k, per=per: (j // per, i, j % per))))
        else:
            out_shape.append(jax.ShapeDtypeStruct((M, N), dtype))
            out_specs.append(pl.BlockSpec((tm, tn), idx(lambda i, j, k: (i, j))))
    scratch = [pltpu.VMEM((tm, tn), F32) for _ in range(npair)] if nk > 1 else []
    nrow, nvec, nout = len(rows), len(vecs), len(outs)
    dims = (((0 if ta else 1,), (1 if tb else 0,)), ((), ()))

    def body(*refs):
        ab = refs[:2 * npair]
        row_refs = refs[2 * npair:2 * npair + nrow]
        vec_refs = refs[2 * npair + nrow:2 * npair + nrow + nvec]
        out_refs = refs[2 * npair + nrow + nvec:2 * npair + nrow + nvec + nout]
        acc_refs = refs[2 * npair + nrow + nvec + nout:]
        if n_outer:
            j, i, k = pl.program_id(0), pl.program_id(1), pl.program_id(2)
        else:
            i, j, k = pl.program_id(0), pl.program_id(1), pl.program_id(2)

        def part(p):
            av, bv = ab[2 * p][...], ab[2 * p + 1][...]
            if a_pro is not None:
                av = a_pro(av)
            if b_pro is not None:
                bv = b_pro(bv)
            return lax.dot_general(av, bv, dims, preferred_element_type=F32)

        def finish(accs):
            row_index = i * tm + lax.broadcasted_iota(jnp.int32, (tm, 1), 0)
            res = epi(accs, [r[...] for r in row_refs], [v[...] for v in vec_refs], row_index)
            for ref, val in zip(out_refs, res):
                ref[...] = val.astype(ref.dtype)

        if nk == 1:
            finish([part(p) for p in range(npair)])
        else:
            parts = [part(p) for p in range(npair)]

            @pl.when(k == 0)
            def _():
                for p in range(npair):
                    acc_refs[p][...] = parts[p]

            @pl.when(k > 0)
            def _():
                for p in range(npair):
                    acc_refs[p][...] += parts[p]

            @pl.when(k == nk - 1)
            def _():
                finish([acc_refs[p][...] for p in range(npair)])

    grid = (nn, nm, nk) if n_outer else (nm, nn, nk)
    return pl.pallas_call(body, name=name, grid=grid, in_specs=in_specs, out_specs=out_specs,
                          out_shape=out_shape, scratch_shapes=scratch, compiler_params=_cparams())(*args)


def _split3(v):
    v0 = v.astype(BF16)
    r1 = v - v0.astype(F32)
    v1 = r1.astype(BF16)
    v2 = (r1 - v1.astype(F32)).astype(BF16)
    return v0, v1, v2


def _mesh_pos():
    return lax.axis_index("x"), lax.axis_index("y"), lax.axis_index("c")


def _allgather_small(name, x):
    m, n = x.shape

    def body(x_ref, out_ref, send_sems, recv_sems, local_sem):
        xi, yi, ci = _mesh_pos()
        me, sibling = (xi, yi, ci), (xi, yi, 1 - ci)
        chips = [(1 - xi, yi), (xi, 1 - yi), (1 - xi, 1 - yi)]

        def rows(px, py, pc):
            return out_ref.at[pl.ds((4 * px + 2 * py + pc) * m, m), :]

        def copy(k, block, to, src=None):
            return pltpu.make_async_remote_copy(
                src_ref=rows(*block) if src is None else src, dst_ref=rows(*block),
                send_sem=send_sems.at[k], recv_sem=recv_sems.at[k], device_id=to, device_id_type=MESH)

        mine = pltpu.make_async_copy(x_ref, rows(*me), local_sem)
        mine.start()
        first = [copy(0, me, sibling, src=x_ref)]
        first += [copy(1 + j, me, (*chip, ci), src=x_ref) for j, chip in enumerate(chips)]
        for cp in first:
            cp.start()
        passed = [copy(4 + j, (*chip, ci), sibling) for j, chip in enumerate(chips)]
        for j, chip in enumerate(chips):
            copy(1 + j, (*chip, ci), me).wait_recv()
            passed[j].start()
        copy(0, sibling, me).wait_recv()
        for j, chip in enumerate(chips):
            copy(4 + j, (*chip, 1 - ci), me).wait_recv()
        for cp in first + passed:
            cp.wait_send()
        mine.wait()

    return pl.pallas_call(
        body, name=name, out_shape=jax.ShapeDtypeStruct((N_DEV * m, n), x.dtype),
        in_specs=[pl.BlockSpec(memory_space=pltpu.VMEM)], out_specs=pl.BlockSpec(memory_space=pltpu.VMEM),
        scratch_shapes=[pltpu.SemaphoreType.DMA((7,)), pltpu.SemaphoreType.DMA((7,)), pltpu.SemaphoreType.DMA],
        compiler_params=_cparams())(x)


def _gather_weights(shards):
    na = len(shards)

    def body(*refs):
        ins, outs = refs[:na], refs[na:2 * na]
        send_sems, recv_sems, local_sems = refs[2 * na:]
        xi, yi, ci = _mesh_pos()
        me, sibling = (xi, yi, ci), (xi, yi, 1 - ci)
        chip = 2 * xi + yi
        chips = [(1 - xi, yi), (xi, 1 - yi), (1 - xi, 1 - yi)]

        def half(a, ch, h):
            rh = shards[a].shape[0] // 2
            return outs[a].at[ch, pl.ds(h * rh, rh), :]

        def copy(a, k, ch, h, to, src=None):
            return pltpu.make_async_remote_copy(
                src_ref=half(a, ch, h) if src is None else src, dst_ref=half(a, ch, h),
                send_sem=send_sems.at[a, k], recv_sem=recv_sems.at[a, k], device_id=to, device_id_type=MESH)

        local = [pltpu.make_async_copy(ins[a], outs[a].at[chip], local_sems.at[a]) for a in range(na)]
        for cp in local:
            cp.start()
        first = []
        for a in range(na):
            rh = shards[a].shape[0] // 2
            src = ins[a].at[pl.ds(ci * rh, rh), :]
            for j, (px, py) in enumerate(chips):
                first.append(copy(a, j, chip, ci, (px, py, ci), src=src))
        for cp in first:
            cp.start()
        passed = []
        for a in range(na):
            for j, (px, py) in enumerate(chips):
                copy(a, j, 2 * px + py, ci, me).wait_recv()
                fwd = copy(a, 3 + j, 2 * px + py, ci, sibling)
                fwd.start()
                passed.append(fwd)
        for a in range(na):
            for j, (px, py) in enumerate(chips):
                copy(a, 3 + j, 2 * px + py, 1 - ci, me).wait_recv()
        for cp in first + passed:
            cp.wait_send()
        for cp in local:
            cp.wait()

    any_spec = pl.BlockSpec(memory_space=pl.ANY)
    return pl.pallas_call(
        body, name="gather_weights",
        out_shape=[jax.ShapeDtypeStruct((N_CHIPS,) + s.shape, s.dtype) for s in shards],
        in_specs=[any_spec] * na, out_specs=[any_spec] * na,
        scratch_shapes=[pltpu.SemaphoreType.DMA((na, 6)), pltpu.SemaphoreType.DMA((na, 6)),
                        pltpu.SemaphoreType.DMA((na,))],
        compiler_params=_cparams())(*shards)


def _scatter_grads(grads):
    na = len(grads)

    def body(*refs):
        ins, outs = refs[:na], refs[na:2 * na]
        send_sems, recv_sems, local_sems = refs[2 * na:]
        xi, yi, ci = _mesh_pos()
        chip = 2 * xi + yi
        chips = [(1 - xi, yi), (xi, 1 - yi), (1 - xi, 1 - yi)]

        def copy(a, j, px, py, slot):
            return pltpu.make_async_remote_copy(
                src_ref=ins[a].at[2 * px + py], dst_ref=outs[a].at[slot],
                send_sem=send_sems.at[a, j], recv_sem=recv_sems.at[a, j],
                device_id=(px, py, ci), device_id_type=MESH)

        local = [pltpu.make_async_copy(ins[a].at[chip], outs[a].at[chip], local_sems.at[a]) for a in range(na)]
        for cp in local:
            cp.start()
        sends = [copy(a, j, px, py, chip) for a in range(na) for j, (px, py) in enumerate(chips)]
        for cp in sends:
            cp.start()
        for a in range(na):
            for j, (px, py) in enumerate(chips):
                copy(a, j, px, py, 2 * px + py).wait_recv()
        for cp in sends:
            cp.wait_send()
        for cp in local:
            cp.wait()

    any_spec = pl.BlockSpec(memory_space=pl.ANY)
    return pl.pallas_call(
        body, name="scatter_grads",
        out_shape=[jax.ShapeDtypeStruct(g.shape, g.dtype) for g in grads],
        in_specs=[any_spec] * na, out_specs=[any_spec] * na,
        scratch_shapes=[pltpu.SemaphoreType.DMA((na, 3)), pltpu.SemaphoreType.DMA((na, 3)),
                        pltpu.SemaphoreType.DMA((na,))],
        compiler_params=_cparams())(*grads)


def _swap_sibling(arrs):
    na = len(arrs)

    def body(*refs):
        ins, outs = refs[:na], refs[na:2 * na]
        send_sems, recv_sems = refs[2 * na:]
        xi, yi, ci = _mesh_pos()
        copies = [pltpu.make_async_remote_copy(
            src_ref=ins[a], dst_ref=outs[a], send_sem=send_sems.at[a], recv_sem=recv_sems.at[a],
            device_id=(xi, yi, 1 - ci), device_id_type=MESH) for a in range(na)]
        for cp in copies:
            cp.start()
        for cp in copies:
            cp.wait()

    any_spec = pl.BlockSpec(memory_space=pl.ANY)
    return pl.pallas_call(
        body, name="swap_sibling",
        out_shape=[jax.ShapeDtypeStruct(g.shape, g.dtype) for g in arrs],
        in_specs=[any_spec] * na, out_specs=[any_spec] * na,
        scratch_shapes=[pltpu.SemaphoreType.DMA((na,)), pltpu.SemaphoreType.DMA((na,))],
        compiler_params=_cparams())(*arrs)


def _attn_tiles(L, Lc, D):
    tq = min(256, Lc)
    return tq, L // tq, Lc // tq, D // HEAD_DIM // Q_PER_KV


def _attn_probs(q, k):
    s = lax.dot_general(q, k, (((1,), (1,)), ((), ())), preferred_element_type=F32) * (HEAD_DIM ** -0.5)
    e = jnp.exp(s - jnp.max(s, axis=-1, keepdims=True))
    return e / jnp.sum(e, axis=-1, keepdims=True)


def _attn_fwd(qr, kr, v, L, Lc, D):
    T = L + Lc
    tq, nq, qoff, nkv = _attn_tiles(L, Lc, D)

    def body(q_ref, k_ref, v_ref, o_ref):
        p = _attn_probs(q_ref[...], k_ref[...])
        o_ref[...] = jnp.dot(p.astype(BF16), v_ref[...], preferred_element_type=F32).astype(o_ref.dtype)

    kv_spec = pl.BlockSpec((T, HEAD_DIM), lambda h, r, q: (0, h))
    return pl.pallas_call(
        body, name="attn_fwd", grid=(nkv, Q_PER_KV, nq),
        in_specs=[pl.BlockSpec((tq, HEAD_DIM), lambda h, r, q: (q + qoff, h * Q_PER_KV + r)), kv_spec, kv_spec],
        out_specs=pl.BlockSpec((tq, HEAD_DIM), lambda h, r, q: (q, h * Q_PER_KV + r)),
        out_shape=jax.ShapeDtypeStruct((L, D), BF16), compiler_params=_cparams())(qr, kr, v)


def _attn_bwd(qr, kr, v, do, L, Lc, D):
    T = L + Lc
    tq, nq, qoff, nkv = _attn_tiles(L, Lc, D)
    scale = HEAD_DIM ** -0.5

    def body(q_ref, k_ref, v_ref, do_ref, dq_ref, dk_ref, dv_ref):
        first = jnp.logical_and(pl.program_id(1) == 0, pl.program_id(2) == 0)
        q, k, dout = q_ref[...], k_ref[...], do_ref[...]
        p = _attn_probs(q, k)
        dp = lax.dot_general(dout, v_ref[...], (((1,), (1,)), ((), ())), preferred_element_type=F32)
        ds = (p * (dp - jnp.sum(p * dp, axis=-1, keepdims=True)) * scale).astype(BF16)
        dq_ref[...] = jnp.dot(ds, k, preferred_element_type=F32)
        dk = lax.dot_general(ds, q, (((0,), (0,)), ((), ())), preferred_element_type=F32)
        dv = lax.dot_general(p.astype(BF16), dout, (((0,), (0,)), ((), ())), preferred_element_type=F32)

        @pl.when(first)
        def _():
            dk_ref[...] = dk
            dv_ref[...] = dv

        @pl.when(jnp.logical_not(first))
        def _():
            dk_ref[...] += dk
            dv_ref[...] += dv

    kv_spec = pl.BlockSpec((T, HEAD_DIM), lambda h, r, q: (0, h))
    q_spec = pl.BlockSpec((tq, HEAD_DIM), lambda h, r, q: (q + qoff, h * Q_PER_KV + r))
    o_spec = pl.BlockSpec((tq, HEAD_DIM), lambda h, r, q: (q, h * Q_PER_KV + r))
    return pl.pallas_call(
        body, name="attn_bwd", grid=(nkv, Q_PER_KV, nq),
        in_specs=[q_spec, kv_spec, kv_spec, o_spec], out_specs=[o_spec, kv_spec, kv_spec],
        out_shape=[jax.ShapeDtypeStruct((L, D), F32), jax.ShapeDtypeStruct((T, D // Q_PER_KV), F32),
                   jax.ShapeDtypeStruct((T, D // Q_PER_KV), F32)],
        compiler_params=_cparams())(qr, kr, v, do)


def _scan_tile(xr, xi, pw_re, pw_im, lanes, reverse):
    tt = xr.shape[0]
    rows = lax.broadcasted_iota(jnp.int32, (tt, 1), 0)
    for k in range(tt.bit_length() - 1):
        d = 1 << k
        shift = tt - d if reverse else d
        keep = rows < tt - d if reverse else rows >= d
        sr = jnp.where(keep, pltpu.roll(xr, shift, 0), 0.0)
        si = jnp.where(keep, pltpu.roll(xi, shift, 0), 0.0)
        pr, pi = pw_re[k:k + 1, lanes], pw_im[k:k + 1, lanes]
        xr, xi = xr + (pr * sr - pi * si), xi + (pr * si + pi * sr)
    return xr, xi


def _scan_init(lr, li, pw_re, pw_im, w_re, w_im, carry_re, carry_im, nslab, reverse):
    tt = w_re.shape[0]
    carry_re[...] = jnp.zeros_like(carry_re)
    carry_im[...] = jnp.zeros_like(carry_im)
    pr, pi = lr, li
    for k in range(tt.bit_length() - 1):
        pw_re[k:k + 1, :] = pr
        pw_im[k:k + 1, :] = pi
        pr, pi = pr * pr - pi * pi, 2.0 * pr * pi
    rows = lax.broadcasted_iota(jnp.int32, (tt, 1), 0)
    edge = rows == (tt - 1 if reverse else 0)
    for j in range(nslab):
        lanes = slice(j * SLAB_ST, (j + 1) * SLAB_ST)
        wr, wi = _scan_tile(jnp.where(edge, lr[:, lanes], 0.0), jnp.where(edge, li[:, lanes], 0.0),
                            pw_re, pw_im, lanes, reverse)
        w_re[:, lanes] = wr
        w_im[:, lanes] = wi


def _ssm_tiles(T, Lc):
    tt = min(128, Lc)
    return tt, T // tt, Lc // tt


def _ssm_fwd(name, u, bbd, cbd_re, cbd_im, lam_re, lam_im, coef_re, coef_im, Lc, reverse):
    T, W = u.shape
    nslab = W // SLAB_CH
    NS = nslab * SLAB_ST
    tt, nt, nc = _ssm_tiles(T, Lc)
    if reverse:
        tile = lambda s: jnp.where(s < nc, nc - 1 - s, nt - 1 - (s - nc))
    else:
        tile = lambda s: s

    def body(u_ref, b_ref, cr_ref, ci_ref, lr_ref, li_ref, kr_ref, ki_ref, hr_ref, hi_ref, y_ref,
             pw_re, pw_im, w_re, w_im, carry_re, carry_im):
        @pl.when(pl.program_id(0) == 0)
        def _():
            _scan_init(lr_ref[...], li_ref[...], pw_re, pw_im, w_re, w_im, carry_re, carry_im, nslab, reverse)

        edge_row = 0 if reverse else tt - 1
        for j in range(nslab):
            lanes = slice(j * SLAB_ST, (j + 1) * SLAB_ST)
            bu = jnp.dot(u_ref[:, j * SLAB_CH:(j + 1) * SLAB_CH], b_ref[j], preferred_element_type=F32)
            br, bi = bu[:, :SLAB_ST], bu[:, SLAB_ST:]
            kr, ki = kr_ref[:, lanes], ki_ref[:, lanes]
            hr, hi = _scan_tile(kr * br - ki * bi, kr * bi + ki * br, pw_re, pw_im, lanes, reverse)
            car, cai = carry_re[:, lanes], carry_im[:, lanes]
            wr, wi = w_re[:, lanes], w_im[:, lanes]
            hr = hr + (wr * car - wi * cai)
            hi = hi + (wr * cai + wi * car)
            carry_re[:, lanes] = hr[edge_row:edge_row + 1, :]
            carry_im[:, lanes] = hi[edge_row:edge_row + 1, :]
            hrb, hib = hr.astype(BF16), hi.astype(BF16)
            hr_ref[:, lanes] = hrb
            hi_ref[:, lanes] = hib
            y_ref[:, j * SLAB_CH:(j + 1) * SLAB_CH] = (
                jnp.dot(hrb, cr_ref[j], preferred_element_type=F32)
                - jnp.dot(hib, ci_ref[j], preferred_element_type=F32))

    whole3 = lambda arr: pl.BlockSpec(arr.shape, lambda s: (0, 0, 0))
    vec = pl.BlockSpec((1, NS), lambda s: (0, 0))
    return pl.pallas_call(
        body, name=name, grid=(nt,),
        in_specs=[pl.BlockSpec((tt, W), lambda s: (tile(s), 0)), whole3(bbd), whole3(cbd_re), whole3(cbd_im),
                  vec, vec, vec, vec],
        out_specs=[pl.BlockSpec((tt, NS), lambda s: (tile(s), 0)), pl.BlockSpec((tt, NS), lambda s: (tile(s), 0)),
                   pl.BlockSpec((tt, W), lambda s: (tile(s), 0))],
        out_shape=[jax.ShapeDtypeStruct((T, NS), BF16), jax.ShapeDtypeStruct((T, NS), BF16),
                   jax.ShapeDtypeStruct((T, W), F32)],
        scratch_shapes=[pltpu.VMEM((8, NS), F32), pltpu.VMEM((8, NS), F32), pltpu.VMEM((tt, NS), F32),
                        pltpu.VMEM((tt, NS), F32), pltpu.VMEM((1, NS), F32), pltpu.VMEM((1, NS), F32)],
        compiler_params=_cparams())(u, bbd, cbd_re, cbd_im, lam_re, lam_im, coef_re, coef_im)


def _ssm_bwd(name, dy, h_re, h_im, u, bbd, bbdt_re, bbdt_im, cbdt_re, cbdt_im, lam_re, lam_im,
             coef_re, coef_im, Lc, reverse):
    T, W = u.shape
    nslab = W // SLAB_CH
    NS = nslab * SLAB_ST
    tt, nt, nc = _ssm_tiles(T, Lc)
    adj_reverse = not reverse
    if reverse:
        tile = lambda s: jnp.where(s < nt - nc, nc + s, s - (nt - nc))
    else:
        tile = lambda s: nt - 1 - s

    def body(dy_ref, hr_ref, hi_ref, u_ref, b_ref, btr_ref, bti_ref, ctr_ref, cti_ref, lr_ref, li_ref,
             kr_ref, ki_ref, du_ref, dlr_ref, dli_ref, dkr_ref, dki_ref, db_ref, dcr_ref, dci_ref,
             pw_re, pw_im, w_re, w_im, carry_re, carry_im):
        @pl.when(pl.program_id(0) == 0)
        def _():
            _scan_init(lr_ref[...], -li_ref[...], pw_re, pw_im, w_re, w_im, carry_re, carry_im, nslab, adj_reverse)
            for ref in (dlr_ref, dli_ref, dkr_ref, dki_ref, db_ref, dcr_ref, dci_ref):
                ref[...] = jnp.zeros_like(ref)

        rows = lax.broadcasted_iota(jnp.int32, (tt, 1), 0)
        edge_row = 0 if adj_reverse else tt - 1
        far_row = tt - 1 if adj_reverse else 0
        tn_dims = (((0,), (0,)), ((), ()))
        for j in range(nslab):
            lanes = slice(j * SLAB_ST, (j + 1) * SLAB_ST)
            chans = slice(j * SLAB_CH, (j + 1) * SLAB_CH)
            dys, us = dy_ref[:, chans], u_ref[:, chans]
            er = jnp.dot(dys, ctr_ref[j], preferred_element_type=F32)
            ei = -jnp.dot(dys, cti_ref[j], preferred_element_type=F32)
            ar, ai = _scan_tile(er, ei, pw_re, pw_im, lanes, adj_reverse)
            car, cai = carry_re[:, lanes], carry_im[:, lanes]
            wr, wi = w_re[:, lanes], w_im[:, lanes]
            ar = ar + (wr * car - wi * cai)
            ai = ai + (wr * cai + wi * car)
            shift = tt - 1 if adj_reverse else 1
            nr = jnp.where(rows == far_row, car, pltpu.roll(ar, shift, 0))
            ni = jnp.where(rows == far_row, cai, pltpu.roll(ai, shift, 0))
            carry_re[:, lanes] = ar[edge_row:edge_row + 1, :]
            carry_im[:, lanes] = ai[edge_row:edge_row + 1, :]
            hrb, hib = hr_ref[:, lanes], hi_ref[:, lanes]
            hr, hi = hrb.astype(F32), hib.astype(F32)
            dlr_ref[:, lanes] += jnp.sum(nr * hr + ni * hi, axis=0, keepdims=True)
            dli_ref[:, lanes] += jnp.sum(ni * hr - nr * hi, axis=0, keepdims=True)
            bu = jnp.dot(us, b_ref[j], preferred_element_type=F32)
            br, bi = bu[:, :SLAB_ST], bu[:, SLAB_ST:]
            dkr_ref[:, lanes] += jnp.sum(ar * br + ai * bi, axis=0, keepdims=True)
            dki_ref[:, lanes] += jnp.sum(ai * br - ar * bi, axis=0, keepdims=True)
            kr, ki = kr_ref[:, lanes], ki_ref[:, lanes]
            dbr = (ar * kr + ai * ki).astype(BF16)
            dbi = (ai * kr - ar * ki).astype(BF16)
            du_ref[:, chans] = (jnp.dot(dbr, btr_ref[j], preferred_element_type=F32)
                                + jnp.dot(dbi, bti_ref[j], preferred_element_type=F32))
            db_ref[j, :, :SLAB_ST] += lax.dot_general(us, dbr, tn_dims, preferred_element_type=F32)
            db_ref[j, :, SLAB_ST:] += lax.dot_general(us, dbi, tn_dims, preferred_element_type=F32)
            dcr_ref[j] += lax.dot_general(hrb, dys, tn_dims, preferred_element_type=F32)
            dci_ref[j] -= lax.dot_general(hib, dys, tn_dims, preferred_element_type=F32)

    whole3 = lambda arr: pl.BlockSpec(arr.shape, lambda s: (0, 0, 0))
    vec = pl.BlockSpec((1, NS), lambda s: (0, 0))
    row_w = pl.BlockSpec((tt, W), lambda s: (tile(s), 0))
    row_s = pl.BlockSpec((tt, NS), lambda s: (tile(s), 0))
    return pl.pallas_call(
        body, name=name, grid=(nt,),
        in_specs=[row_w, row_s, row_s, row_w, whole3(bbd), whole3(bbdt_re), whole3(bbdt_im), whole3(cbdt_re),
                  whole3(cbdt_im), vec, vec, vec, vec],
        out_specs=[row_w, vec, vec, vec, vec, whole3(bbd), whole3(bbdt_re), whole3(bbdt_re)],
        out_shape=[jax.ShapeDtypeStruct((T, W), F32)] + [jax.ShapeDtypeStruct((1, NS), F32)] * 4
        + [jax.ShapeDtypeStruct(bbd.shape, F32), jax.ShapeDtypeStruct(bbdt_re.shape, F32),
           jax.ShapeDtypeStruct(bbdt_re.shape, F32)],
        scratch_shapes=[pltpu.VMEM((8, NS), F32), pltpu.VMEM((8, NS), F32), pltpu.VMEM((tt, NS), F32),
                        pltpu.VMEM((tt, NS), F32), pltpu.VMEM((1, NS), F32), pltpu.VMEM((1, NS), F32)],
        compiler_params=_cparams())(dy, h_re, h_im, u, bbd, bbdt_re, bbdt_im, cbdt_re, cbdt_im,
                                    lam_re, lam_im, coef_re, coef_im)


def _zoh_math(a_re, a_im, log_dt):
    dt = jnp.exp(log_dt)
    mag = jnp.exp(a_re * dt)
    lb_re = mag * jnp.cos(a_im * dt)
    lb_im = mag * jnp.sin(a_im * dt)
    den = a_re * a_re + a_im * a_im
    coef_re = ((lb_re - 1.0) * a_re + lb_im * a_im) / den
    coef_im = (lb_im * a_re - (lb_re - 1.0) * a_im) / den
    return lb_re, lb_im, coef_re, coef_im


def _zoh_fwd(a_re, a_im, log_dt):
    def body(ar, ai, ld, o0, o1, o2, o3):
        for ref, val in zip((o0, o1, o2, o3), _zoh_math(ar[...], ai[...], ld[...])):
            ref[...] = val

    return pl.pallas_call(body, name="zoh_fwd", out_shape=[jax.ShapeDtypeStruct(a_re.shape, F32)] * 4,
                          compiler_params=_cparams())(a_re, a_im, log_dt)


def _zoh_bwd(a_re, a_im, log_dt, cots):
    def body(ar, ai, ld, c0, c1, c2, c3, o0, o1, o2):
        _, vjp = jax.vjp(_zoh_math, ar[...], ai[...], ld[...])
        for ref, val in zip((o0, o1, o2), vjp((c0[...], c1[...], c2[...], c3[...]))):
            ref[...] = val

    return pl.pallas_call(
        body, name="zoh_bwd",
        out_shape=[jax.ShapeDtypeStruct(a_re.shape, F32), jax.ShapeDtypeStruct(a_re.shape, F32),
                   jax.ShapeDtypeStruct(log_dt.shape, F32)],
        compiler_params=_cparams())(a_re, a_im, log_dt, *cots)


def _outer_sum(acts, cots):
    D, N = acts.shape[1], cots.shape[1]
    tm, tn = _div(D, 512), _div(N, 1152)
    dims = (((0,), (0,)), ((), ()))

    def body(a_ref, b_ref, o_ref):
        a = a_ref[...]
        aa = _split3(a * _sigmoid(a))
        bb = _split3(b_ref[...])
        acc = None
        for ia in range(3):
            for ib in range(3 - ia):
                t = lax.dot_general(aa[ia], bb[ib], dims, preferred_element_type=F32)
                acc = t if acc is None else acc + t
        o_ref[...] = acc

    return pl.pallas_call(
        body, name="mod_dw", grid=(D // tm, N // tn),
        in_specs=[pl.BlockSpec((16, tm), lambda i, j: (0, i)), pl.BlockSpec((16, tn), lambda i, j: (0, j))],
        out_specs=pl.BlockSpec((tm, tn), lambda i, j: (i, j)),
        out_shape=jax.ShapeDtypeStruct((D, N), F32), compiler_params=_cparams())(acts, cots)


def _adamw_math(w, g, m, v):
    m = ADAM_B1 * m + (1.0 - ADAM_B1) * g
    v = ADAM_B2 * v + (1.0 - ADAM_B2) * (g * g)
    m_hat = m / (1.0 - ADAM_B1 ** ADAM_STEP)
    v_hat = v / (1.0 - ADAM_B2 ** ADAM_STEP)
    delta = -ADAM_LR * (m_hat / (jnp.sqrt(v_hat) + ADAM_EPS) + ADAM_WD * w)
    return delta, m, v


def _adamw(name, w, m, v, gparts):
    R, C = w.shape
    tr = _div(R, max(8, 131072 // C), mult=8)

    def fn(i, wv, mv, vv, *gs):
        g = gs[0]
        for extra in gs[1:]:
            g = g + extra
        return (g,) + _adamw_math(wv, g, mv, vv)

    return _rowk(name, fn, R, tr, [(w, 'row'), (m, 'row'), (v, 'row')] + [(g, 'row') for g in gparts],
                 [((R, C), F32, 'row')] * 4)


def _pack(pieces, rows_mult=8):
    flat = jnp.concatenate([p.reshape(-1).astype(F32) for p in pieces])
    unit = rows_mult * PACK_W
    total = -(-flat.shape[0] // unit) * unit
    return jnp.pad(flat, (0, total - flat.shape[0])).reshape(total // PACK_W, PACK_W)


def _unpack(buf, shapes):
    flat = buf.reshape(-1)
    out, off = [], 0
    for s in shapes:
        n = math.prod(s)
        out.append(flat[off:off + n].reshape(s))
        off += n
    return out


def _bd_expand(t):
    S, g, a, b = t.shape
    eye = jnp.eye(g, dtype=t.dtype)
    return (t[:, :, :, None, :] * eye[None, :, None, :, None]).reshape(S, g * a, g * b)


def _bd_extract(t, a, b):
    S = t.shape[0]
    g = t.shape[1] // a
    eye = jnp.eye(g, dtype=t.dtype)
    return jnp.sum(t.reshape(S, g, a, g, b) * eye[None, :, None, :, None], axis=3)


def _rope_tables(L, Lc):
    rows = L // GRID_W
    row_ids = jnp.broadcast_to(jnp.arange(rows)[:, None], (rows, GRID_W)).reshape(-1).astype(F32)
    col_ids = jnp.broadcast_to(jnp.arange(GRID_W)[None, :], (rows, GRID_W)).reshape(-1).astype(F32)
    quarter = HEAD_DIM // 4
    inv_freq = ROPE_THETA ** (-jnp.arange(quarter, dtype=F32) / quarter)
    ang_r = row_ids[:, None] * inv_freq
    ang_c = col_ids[:, None] * inv_freq
    cos = jnp.concatenate([jnp.cos(ang_r), jnp.cos(ang_r), jnp.cos(ang_c), jnp.cos(ang_c)], axis=1)
    sin = jnp.concatenate([-jnp.sin(ang_r), jnp.sin(ang_r), -jnp.sin(ang_c), jnp.sin(ang_c)], axis=1)
    cos = jnp.concatenate([jnp.ones((Lc, HEAD_DIM), F32), cos], axis=0)
    sin = jnp.concatenate([jnp.zeros((Lc, HEAD_DIM), F32), sin], axis=0)
    return cos, sin


def _rot(v):
    lane = lax.broadcasted_iota(jnp.int32, (1, HEAD_DIM), 1)
    first = (lane % (HEAD_DIM // 2)) < (HEAD_DIM // 4)
    return jnp.where(first, pltpu.roll(v, HEAD_DIM - HEAD_DIM // 4, 1), pltpu.roll(v, HEAD_DIM // 4, 1))


def _head_norm(xh, g):
    return xh * lax.rsqrt(jnp.mean(xh * xh, axis=-1, keepdims=True) + NORM_EPS) * g


def _norm_mod(xv, g, sh, sc):
    r = lax.rsqrt(jnp.mean(xv * xv, axis=-1, keepdims=True) + NORM_EPS)
    return (xv * r) * g * (1.0 + sc) + sh


def kernel(x, c, ctx, c_ctx, w_mod, b_mod, norm_g, w_ffn1_gate, w_ffn1_up, w_ffn1_down, w_in, q_norm_g, k_norm_g, ssm_a_re, ssm_a_im, ssm_log_dt, ssm_b_re, ssm_b_im, ssm_c_re, ssm_c_im, ssm_d, w_glu, b_glu, w_br_attn, w_br_ssm, w_out, w_ffn2_gate, w_ffn2_up, w_ffn2_down, loss_target, m_c_ctx, m_w_mod, m_b_mod, m_norm_g, m_w_ffn1_gate, m_w_ffn1_up, m_w_ffn1_down, m_w_in, m_q_norm_g, m_k_norm_g, m_ssm_a_re, m_ssm_a_im, m_ssm_log_dt, m_ssm_b_re, m_ssm_b_im, m_ssm_c_re, m_ssm_c_im, m_ssm_d, m_w_glu, m_b_glu, m_w_br_attn, m_w_br_ssm, m_w_out, m_w_ffn2_gate, m_w_ffn2_up, m_w_ffn2_down, v_c_ctx, v_w_mod, v_b_mod, v_norm_g, v_w_ffn1_gate, v_w_ffn1_up, v_w_ffn1_down, v_w_in, v_q_norm_g, v_k_norm_g, v_ssm_a_re, v_ssm_a_im, v_ssm_log_dt, v_ssm_b_re, v_ssm_b_im, v_ssm_c_re, v_ssm_c_im, v_ssm_d, v_w_glu, v_b_glu, v_w_br_attn, v_w_br_ssm, v_w_out, v_w_ffn2_gate, v_w_ffn2_up, v_w_ffn2_down):
    A = dict(locals())
    xi, yi, ci = _mesh_pos()
    chip = 2 * xi + yi
    me = 4 * xi + 2 * yi + ci
    L, D = x.shape[1], x.shape[2]
    Lc = ctx.shape[1]
    T = L + Lc
    F4 = w_ffn1_gate.shape[2]
    F = N_CHIPS * F4
    W, KV, Dq = D // 2, D // 4, D // 4
    G = W // SSM_GROUP
    P, E = SSM_STATE, SSM_GROUP
    NS = G * P
    nslab = W // SLAB_CH
    tr = min(256, Lc)
    ncr = Lc // tr
    assert L % tr == 0 and Lc % tr == 0 and W % SLAB_CH == 0 and D % (4 * LANES) == 0

    def sel(i, v):
        return v if v.shape[0] == 1 else jnp.where(i < ncr, v[0:1], v[1:2])

    def put(i, v, nrow):
        if nrow == 1:
            return v
        which = (i >= ncr).astype(jnp.int32)
        r2 = lax.broadcasted_iota(jnp.int32, (nrow, 1), 0)
        return jnp.where(r2 == which, jnp.broadcast_to(v, (nrow, v.shape[1])), 0.0)

    ident = lambda accs, rows, vecs, ri: [accs[0]]

    NM = w_mod.shape[2]
    first = jnp.zeros((8, D), F32).at[0].set(c[0]).at[1:4, :Dq].set(norm_g[0])
    g0 = _allgather_small("gather_c", first).reshape(N_CHIPS, 2, 8, D)
    c_all = g0[:, :, 0].reshape(N_DEV, D)
    ng = jnp.transpose(g0[:, 0, 1:4, :Dq], (1, 0, 2)).reshape(3, D)
    acts = jnp.concatenate([c_all, c_ctx[None], jnp.zeros((7, D), F32)], axis=0)
    wm = w_mod[0]
    b_shard = lax.dynamic_slice(b_mod[0], (chip * NM,), (NM,))[None]
    silu_bf = lambda a: (a * _sigmoid(a)).astype(BF16)
    to_bf = lambda b: b.astype(BF16)
    mod_part = _mm("mod_fwd", [(acts, wm, D)], 16, NM, tm=16, tn=_div(NM, 1152),
                   epi=lambda accs, rows, vecs, ri: [accs[0] + vecs[0]], outs=[(F32, False)],
                   vecs=[b_shard], a_pro=silu_bf, b_pro=to_bf)[0]
    mg = _allgather_small("gather_mod", mod_part).reshape(N_CHIPS, 2, 16, NM)[:, 0]
    mod_all = jnp.transpose(mg, (1, 0, 2)).reshape(16, N_CHIPS * NM)
    mod_x = lax.dynamic_slice(mod_all, (me, 0), (1, 9 * D))
    mod_c = jnp.where(jnp.arange(9 * D)[None] < 5 * D, mod_all[8:9], 0.0)
    modv = jnp.concatenate([mod_c, mod_x], axis=0)
    mv = lambda k: modv[:, k * D:(k + 1) * D]
    sh1, sc1, g1, sh2, sc2 = mv(0), mv(1), mv(2), mv(3), mv(4)
    g2, sh3, sc3, g3 = mv(5)[1:2], mv(6)[1:2], mv(7)[1:2], mv(8)[1:2]

    big = ['w_ffn1_gate', 'w_ffn1_up', 'w_ffn1_down', 'w_ffn2_gate', 'w_ffn2_up', 'w_ffn2_down',
           'w_in', 'w_glu', 'w_br_attn', 'w_br_ssm', 'w_out']
    row_sharded = {'w_ffn1_down', 'w_ffn2_down', 'w_glu', 'w_br_attn', 'w_out'}
    gathered = _gather_weights([A[n][0].astype(BF16) for n in big])
    Wt = {}
    for n, gw in zip(big, gathered):
        Wt[n] = gw.reshape(N_CHIPS * gw.shape[1], gw.shape[2]) if n in row_sharded else gw

    a_re2, a_im2 = ssm_a_re[0].reshape(2 * G, P), ssm_a_im[0].reshape(2 * G, P)
    ldt2 = ssm_log_dt[0].reshape(2 * G, 1)
    zoh = _zoh_fwd(a_re2, a_im2, ldt2)
    lam_re, lam_im, coef_re, coef_im = [[z[d * G:(d + 1) * G].reshape(1, NS) for d in range(2)] for z in zoh]
    bd_b = lambda b: _bd_expand(jnp.transpose(b, (0, 2, 1)).reshape(nslab, SLAB_GROUPS, E, P))
    bd_c = lambda cc: _bd_expand(jnp.transpose(cc, (0, 2, 1)).reshape(nslab, SLAB_GROUPS, P, E))
    bbd, bbdt_re, bbdt_im, cbd_re, cbd_im, cbdt_re, cbdt_im = [], [], [], [], [], [], []
    for d in range(2):
        br_, bi_ = bd_b(ssm_b_re[0, d]).astype(BF16), bd_b(ssm_b_im[0, d]).astype(BF16)
        cr_, ci_ = bd_c(ssm_c_re[0, d]).astype(BF16), bd_c(ssm_c_im[0, d]).astype(BF16)
        bbd.append(jnp.concatenate([br_, bi_], axis=2))
        bbdt_re.append(jnp.transpose(br_, (0, 2, 1)))
        bbdt_im.append(jnp.transpose(bi_, (0, 2, 1)))
        cbd_re.append(cr_)
        cbd_im.append(ci_)
        cbdt_re.append(jnp.transpose(cr_, (0, 2, 1)))
        cbdt_im.append(jnp.transpose(ci_, (0, 2, 1)))
    cos_t, sin_t = _rope_tables(L, Lc)
    qg, kg = q_norm_g, k_norm_g

    def norm_mod(name, xv, g, sh, sc):
        rows = xv.shape[0]
        return _rowk(name, lambda i, xt, gt, sht, sct: [_norm_mod(xt, gt, sel(i, sht), sel(i, sct))],
                     rows, tr, [(xv, 'row'), (g, 'vec'), (sh, 'vec'), (sc, 'vec')], [((rows, D), BF16, 'row')])[0]

    def swiglu_epi(accs, rows, vecs, ri):
        a_, b_ = accs
        return [a_, b_, a_ * _sigmoid(a_) * b_]

    def res_epi(coef):
        def epi(accs, rows, vecs, ri):
            gate = vecs[0]
            if gate.shape[0] == 2:
                gate = jnp.where(ri < Lc, gate[0:1], gate[1:2])
            return [accs[0], rows[0] + (coef * gate) * accs[0]]
        return epi

    def ffn_fwd(tag, h, xres, gate, wg, wu, wd):
        rows = h.shape[0]
        a_, b_, s_ = _mm(tag + "_up", [(h, wg, D), (h, wu, D)], rows, F, tm=_div(rows, 256), tn=F4, epi=swiglu_epi,
                         outs=[(F32, False), (F32, False), (BF16, False)])
        f_, xo = _mm(tag + "_down", [(s_, wd, F)], rows, D, tm=_div(rows, 512), tn=_div(D, 1024), nk=N_CHIPS,
                     epi=res_epi(0.5), outs=[(F32, False), (F32, False)], rows=[(xres, 0, 0)], vecs=[gate])
        return a_, b_, s_, f_, xo

    xc = jnp.concatenate([ctx[0], x[0]], axis=0)
    h1 = norm_mod("norm1", xc, ng[0:1], sh1, sc1)
    a1, b1, s1, f1, x1 = ffn_fwd("ffn1", h1, xc, g1, Wt['w_ffn1_gate'], Wt['w_ffn1_up'], Wt['w_ffn1_down'])
    h2 = norm_mod("norm2", x1, ng[1:2], sh2, sc2)
    proj = _mm("in_proj", [(h2, Wt['w_in'], D)], T, 4 * D, tm=_div(T, 768), tn=_div(D, 1024), epi=ident,
               outs=[(F32, False)])[0]
    nh, nkvh = D // HEAD_DIM, KV // HEAD_DIM

    def prep_fn(i, kt, vt, ut, qt, qgt, kgt, ct, st):
        qs = [_head_norm(qt[:, h * HEAD_DIM:(h + 1) * HEAD_DIM], qgt) for h in range(nh)]
        ks = [_head_norm(kt[:, h * HEAD_DIM:(h + 1) * HEAD_DIM], kgt) for h in range(nkvh)]
        qs = [v * ct + _rot(v) * st for v in qs]
        ks = [v * ct + _rot(v) * st for v in ks]
        return [jnp.concatenate(qs, axis=1), jnp.concatenate(ks, axis=1), vt, ut]

    qr, kr, vb, ub = _rowk(
        "qk_prep", prep_fn, T, tr,
        [(proj, ('col', KV, 0)), (proj, ('col', KV, 1)), (proj, ('col', W, 1)), (proj, ('col', D, 1)),
         (qg, 'vec'), (kg, 'vec'), (cos_t, 'row'), (sin_t, 'row')],
        [((T, D), BF16, 'row'), ((T, KV), BF16, 'row'), ((T, KV), BF16, 'row'), ((T, W), BF16, 'row')])
    attn = _attn_fwd(qr, kr, vb, L, Lc, D)
    hs_re, hs_im, ys = [], [], []
    for d in range(2):
        hr_, hi_, y_ = _ssm_fwd("ssm_fwd%d" % d, ub, bbd[d], cbd_re[d], cbd_im[d], lam_re[d], lam_im[d],
                                coef_re[d], coef_im[d], Lc, reverse=bool(d))
        hs_re.append(hr_)
        hs_im.append(hi_)
        ys.append(y_)

    def ssm_out_fn(i, y0, y1, ut, dt):
        pre = dt * ut + y0 + y1
        yg_ = _gelu(pre)
        return [pre, yg_, yg_]

    ssm_pre, yg, ygb = _rowk(
        "ssm_out", ssm_out_fn, L, tr,
        [(ys[0], 'orow'), (ys[1], 'orow'), (proj, ('ocol', W, 1)), (ssm_d, 'vec')],
        [((L, W), F32, 'row'), ((L, W), F32, 'row'), ((L, W), BF16, 'row')], nc=ncr)

    def glu_epi(accs, rows, vecs, ri):
        z_ = accs[0] + vecs[0]
        return [z_, rows[0] * _sigmoid(z_)]

    zglu, y2 = _mm("glu", [(ygb, Wt['w_glu'], W)], L, W, tm=_div(L, 512), tn=_div(W, 512), epi=glu_epi,
                   outs=[(F32, False), (BF16, False)], rows=[(yg, 0, 0)], vecs=[b_glu])
    tnm = _div(Dq, 512)

    def merge_epi(accs, rows, vecs, ri):
        ga, gs = _sigmoid(rows[0]), _sigmoid(rows[1])
        return [accs[0], accs[1], ga * accs[0] + gs * accs[1]]

    ba, bs, merged = _mm("merge", [(attn, Wt['w_br_attn'], D), (y2, Wt['w_br_ssm'], W)], L, D, tm=tr, tn=tnm,
                         epi=merge_epi, outs=[(F32, False), (F32, False), (BF16, False)],
                         rows=[(proj, ncr, 2 * D // tnm), (proj, ncr, 3 * D // tnm)])
    mix, x2 = _mm("out_proj", [(merged, Wt['w_out'], D)], L, D, tm=tr, tn=_div(D, 1024), epi=res_epi(1.0),
                  outs=[(F32, False), (F32, False)], rows=[(x1, ncr, 0)], vecs=[g2])
    h3 = norm_mod("norm3", x2, ng[2:3], sh3, sc3)
    a3, b3, s3, f3, x3 = ffn_fwd("ffn2", h3, x2, g3, Wt['w_ffn2_gate'], Wt['w_ffn2_up'], Wt['w_ffn2_down'])

    def loss_fn(i, yt, tt_):
        diff = yt - tt_
        return [diff * (1.0 / D), jnp.sum(diff * diff, axis=0, keepdims=True)]

    dy, sq = _rowk("loss", loss_fn, L, tr, [(x3, 'row'), (loss_target[0], 'row')],
                   [((L, D), F32, 'row'), ((1, D), F32, 'acc')])
    loss = lax.psum(0.5 * jnp.sum(sq) / D, ("x", "y", "c"))

    def res_bwd(name, dxo, f_, gate, coef):
        rows, nrow = dxo.shape[0], gate.shape[0]

        def fn(i, dt, ft, gt):
            return [(coef * sel(i, gt)) * dt, put(i, jnp.sum(dt * ft, axis=0, keepdims=True) * coef, nrow)]

        return _rowk(name, fn, rows, tr, [(dxo, 'row'), (f_, 'row'), (gate, 'vec')],
                     [((rows, D), BF16, 'row'), ((nrow, D), F32, 'acc')])

    def swiglu_bwd_epi(accs, rows, vecs, ri):
        ds_, a_, b_ = accs[0], rows[0], rows[1]
        sg = _sigmoid(a_)
        return [ds_ * b_ * (sg * (1.0 + a_ * (1.0 - sg))), ds_ * (a_ * sg)]

    sum2 = lambda accs, rows, vecs, ri: [accs[0] + accs[1]]

    def norm_mod_bwd(name, xv, g, sh, sc, dh, dres, dres_kind):
        rows, nrow = xv.shape[0], sh.shape[0]

        def fn(i, xt, gt, sht, sct, dht, rest):
            _, vjp = jax.vjp(_norm_mod, xt, gt, sel(i, sht), sel(i, sct))
            dx_, dg_, dsh_, dsc_ = vjp(dht)
            dx_ = dx_ + (jnp.where(i >= ncr, rest, 0.0) if dres_kind == 'xrow' else rest)
            return [dx_, dg_, put(i, dsh_, nrow), put(i, dsc_, nrow)]

        return _rowk(name, fn, rows, tr,
                     [(xv, 'row'), (g, 'vec'), (sh, 'vec'), (sc, 'vec'), (dh, 'row'), (dres, dres_kind)],
                     [((rows, D), F32, 'row'), ((1, D), F32, 'acc'), ((nrow, D), F32, 'acc'), ((nrow, D), F32, 'acc')],
                     nc=ncr)

    def ffn_bwd(tag, dxo, h, a_, b_, s_, f_, gate, wg, wu, wd):
        rows = dxo.shape[0]
        df, dgate = res_bwd(tag + "_dres", dxo, f_, gate, 0.5)
        da, db = _mm(tag + "_dact", [(df, wd, D)], rows, F, tm=_div(rows, 384), tn=F4, tb=True, epi=swiglu_bwd_epi,
                     outs=[(BF16, False), (BF16, False)], rows=[(a_, 0, 0), (b_, 0, 0)])
        dwd = _mm(tag + "_dwd", [(s_, df, rows)], F, D, tm=_div(F, 512), tn=_div(D, 1024), ta=True, epi=ident,
                  outs=[(BF16, False)])[0].reshape(N_CHIPS, F4, D)
        dwg = _mm(tag + "_dwg", [(h, da, rows)], D, F, tm=_div(D, 512), tn=F4, ta=True, epi=ident,
                  outs=[(BF16, True)])[0]
        dwu = _mm(tag + "_dwu", [(h, db, rows)], D, F, tm=_div(D, 512), tn=F4, ta=True, epi=ident,
                  outs=[(BF16, True)])[0]
        dh = _mm(tag + "_dh", [(da, wg, F), (db, wu, F)], rows, D, tm=_div(rows, 768), tn=_div(D, 1024), nk=N_CHIPS,
                 tb=True, epi=sum2, outs=[(F32, False)])[0]
        return dh, dgate, dwg, dwu, dwd

    dh3, dg3, dwg2, dwu2, dwd2 = ffn_bwd("ffn2", dy, h3, a3, b3, s3, f3, g3, Wt['w_ffn2_gate'], Wt['w_ffn2_up'],
                                         Wt['w_ffn2_down'])
    dx2, dng3, dsh3, dsc3 = norm_mod_bwd("norm3_bwd", x2, ng[2:3], sh3, sc3, dh3, dy, 'row')
    dmix, dg2 = res_bwd("mix_dres", dx2, mix, g2, 1.0)

    def dmerge_epi(accs, rows, vecs, ri):
        dm_, ba_, bs_ = accs[0], rows[0], rows[1]
        ga, gs = _sigmoid(rows[2]), _sigmoid(rows[3])
        return [dm_ * ga, dm_ * gs, dm_ * ba_ * ga * (1.0 - ga), dm_ * bs_ * gs * (1.0 - gs)]

    dba, dbs, dga, dgs = _mm("dmerge", [(dmix, Wt['w_out'], D)], L, D, tm=tr, tn=tnm, tb=True, epi=dmerge_epi,
                             outs=[(BF16, False)] * 4,
                             rows=[(ba, 0, 0), (bs, 0, 0), (proj, ncr, 2 * D // tnm), (proj, ncr, 3 * D // tnm)])
    dwout = _mm("dw_out", [(merged, dmix, L)], D, D, tm=_div(D, 512), tn=_div(D, 1024), ta=True, epi=ident,
                outs=[(BF16, False)])[0].reshape(N_CHIPS, Dq, D)
    dattn = _mm("dattn", [(dba, Wt['w_br_attn'], D)], L, D, tm=_div(L, 512), tn=_div(D, 1024), tb=True, epi=ident,
                outs=[(BF16, False)])[0]
    dwba = _mm("dw_br_attn", [(attn, dba, L)], D, D, tm=_div(D, 512), tn=_div(D, 1024), ta=True, epi=ident,
               outs=[(BF16, False)])[0].reshape(N_CHIPS, Dq, D)
    dy2 = _mm("dy2", [(dbs, Wt['w_br_ssm'], D)], L, W, tm=_div(L, 512), tn=_div(W, 1024), nk=N_CHIPS, tb=True,
              epi=ident, outs=[(F32, False)])[0]
    dwbs = _mm("dw_br_ssm", [(y2, dbs, L)], W, D, tm=_div(W, 512), tn=_div(Dq, 512), ta=True, epi=ident,
               outs=[(BF16, True)])[0]

    def glu_bwd_fn(i, d2, ygt, zt):
        sz = _sigmoid(zt)
        dz_ = d2 * ygt * sz * (1.0 - sz)
        return [dz_, d2 * sz, jnp.sum(dz_, axis=0, keepdims=True)]

    dz, dyd, dbglu = _rowk("glu_bwd", glu_bwd_fn, L, tr, [(dy2, 'row'), (yg, 'row'), (zglu, 'row')],
                           [((L, W), BF16, 'row'), ((L, W), F32, 'row'), ((1, W), F32, 'acc')])

    def dssm_epi(accs, rows, vecs, ri):
        _, vjp = jax.vjp(_gelu, rows[1])
        ds_ = vjp(accs[0] + rows[0])[0]
        return [ds_, ds_]

    dssm, dssm_b = _mm("dssm", [(dz, Wt['w_glu'], W)], L, W, tm=_div(L, 512), tn=_div(W, 512), tb=True, epi=dssm_epi,
                       outs=[(F32, False), (BF16, False)], rows=[(dyd, 0, 0), (ssm_pre, 0, 0)])
    dwglu = _mm("dw_glu", [(ygb, dz, L)], W, W, tm=_div(W, 512), tn=_div(W, 1024), ta=True, epi=ident,
                outs=[(BF16, False)])[0].reshape(N_CHIPS, W // N_CHIPS, W)
    dssm_full = jnp.concatenate([jnp.zeros((Lc, W), BF16), dssm_b], axis=0)
    dus, dlam_re, dlam_im, dcoef_re, dcoef_im, dbbd, dcbd_re, dcbd_im = [], [], [], [], [], [], [], []
    for d in range(2):
        r = _ssm_bwd("ssm_bwd%d" % d, dssm_full, hs_re[d], hs_im[d], ub, bbd[d], bbdt_re[d], bbdt_im[d],
                     cbdt_re[d], cbdt_im[d], lam_re[d], lam_im[d], coef_re[d], coef_im[d], Lc, reverse=bool(d))
        for lst, val in zip((dus, dlam_re, dlam_im, dcoef_re, dcoef_im, dbbd, dcbd_re, dcbd_im), r):
            lst.append(val)
    dqr, dkr, dvf = _attn_bwd(qr, kr, vb, dattn, L, Lc, D)

    def prep_bwd_fn(i, qt, kt, ut, dqt, dkt, dvt, du0, du1, dst, dt, qgt, kgt, ct, st):
        live = i >= ncr
        dqt = jnp.where(live, dqt, 0.0)
        dst = jnp.where(live, dst, 0.0)
        dqs, dks = [], []
        dqg_ = jnp.zeros((1, HEAD_DIM), F32)
        dkg_ = jnp.zeros((1, HEAD_DIM), F32)
        for h in range(nh):
            hl = slice(h * HEAD_DIM, (h + 1) * HEAD_DIM)
            dn = dqt[:, hl] * ct + _rot(dqt[:, hl] * st)
            _, vjp = jax.vjp(_head_norm, qt[:, hl], qgt)
            dxh, dgh = vjp(dn)
            dqs.append(dxh)
            dqg_ = dqg_ + dgh
        for h in range(nkvh):
            hl = slice(h * HEAD_DIM, (h + 1) * HEAD_DIM)
            dn = dkt[:, hl] * ct + _rot(dkt[:, hl] * st)
            _, vjp = jax.vjp(_head_norm, kt[:, hl], kgt)
            dxh, dgh = vjp(dn)
            dks.append(dxh)
            dkg_ = dkg_ + dgh
        du_ = du0 + du1 + dst * dt
        return [jnp.concatenate(dqs, axis=1), jnp.concatenate(dks, axis=1), dvt, du_, dqg_, dkg_,
                jnp.sum(dst * ut, axis=0, keepdims=True)]

    dq_b, dk_b, dv_b, du_b, dqg, dkg, dssd = _rowk(
        "qk_prep_bwd", prep_bwd_fn, T, tr,
        [(proj, ('col', D, 1)), (proj, ('col', KV, 0)), (proj, ('col', W, 1)), (dqr, 'xrow'), (dkr, 'row'),
         (dvf, 'row'), (dus[0], 'row'), (dus[1], 'row'), (dssm, 'xrow'), (ssm_d, 'vec'), (qg, 'vec'), (kg, 'vec'),
         (cos_t, 'row'), (sin_t, 'row')],
        [((T, D), BF16, 'row'), ((T, KV), BF16, 'row'), ((T, KV), BF16, 'row'), ((T, W), BF16, 'row'),
         ((1, HEAD_DIM), F32, 'acc'), ((1, HEAD_DIM), F32, 'acc'), ((1, W), F32, 'acc')], nc=ncr)
    dgate = jnp.concatenate([jnp.zeros((Lc, 2 * D), BF16), jnp.concatenate([dga, dgs], axis=1)], axis=0)
    dproj = jnp.concatenate([dk_b, dv_b, du_b, dq_b, dgate], axis=1)
    dh2 = _mm("in_proj_dx", [(dproj, Wt['w_in'], 4 * D)], T, D, tm=_div(T, 768), tn=_div(D, 1024), nk=N_CHIPS, tb=True,
              epi=ident, outs=[(F32, False)])[0]
    dwin = _mm("in_proj_dw", [(h2, dproj, T)], D, 4 * D, tm=_div(D, 512), tn=_div(D, 1024), ta=True, epi=ident,
               outs=[(BF16, True)])[0]
    dx1, dng2, dsh2, dsc2 = norm_mod_bwd("norm2_bwd", x1, ng[1:2], sh2, sc2, dh2, dx2, 'xrow')
    dh1, dg1, dwg1, dwu1, dwd1 = ffn_bwd("ffn1", dx1, h1, a1, b1, s1, f1, g1, Wt['w_ffn1_gate'], Wt['w_ffn1_up'],
                                         Wt['w_ffn1_down'])
    dx0, dng1, dsh1, dsc1 = norm_mod_bwd("norm1_bwd", xc, ng[0:1], sh1, sc1, dh1, dx1, 'row')
    grad_x = dx0[Lc:][None]

    zD = jnp.zeros((1, D), F32)
    dmod_x = jnp.concatenate([dsh1[1:2], dsc1[1:2], dg1[1:2], dsh2[1:2], dsc2[1:2], dg2, dsh3, dsc3, dg3], axis=1)
    dmod_c = jnp.concatenate([dsh1[0:1], dsc1[0:1], dg1[0:1], dsh2[0:1], dsc2[0:1], zD, zD, zD, zD], axis=1)
    db_parts, dc_parts = [], []
    for d in range(2):
        db_parts.append(jnp.transpose(_bd_extract(dbbd[d][:, :, :SLAB_ST], E, P).reshape(G, E, P), (0, 2, 1)))
        db_parts.append(jnp.transpose(_bd_extract(dbbd[d][:, :, SLAB_ST:], E, P).reshape(G, E, P), (0, 2, 1)))
        dc_parts.append(jnp.transpose(_bd_extract(dcbd_re[d], P, E).reshape(G, P, E), (0, 2, 1)))
        dc_parts.append(jnp.transpose(_bd_extract(dcbd_im[d], P, E).reshape(G, P, E), (0, 2, 1)))
    pieces = [dmod_x, dmod_c, dng1, dng2, dng3, dqg, dkg] + dlam_re + dlam_im + dcoef_re + dcoef_im \
        + db_parts + dc_parts + [dssd, dbglu]
    shapes = [p_.shape for p_ in pieces]
    pack = _pack(pieces)
    RP = pack.shape[0]
    allp = _allgather_small("gather_small", pack).reshape(N_DEV, RP, PACK_W)

    def sum_dev_fn(i, t):
        s_ = t[0]
        for k in range(1, N_DEV):
            s_ = s_ + t[k]
        return [s_]

    tot = _rowk("sum_small", sum_dev_fn, RP, 8, [(allp, 'row3')], [((RP, PACK_W), F32, 'row')])[0]
    (t_dmod_x, t_dmod_c, t_ng1, t_ng2, t_ng3, t_qg, t_kg, t_lr0, t_lr1, t_li0, t_li1, t_kr0, t_kr1, t_ki0, t_ki1,
     t_bre0, t_bim0, t_bre1, t_bim1, t_cre0, t_cim0, t_cre1, t_cim1, t_d, t_bglu) = _unpack(tot, shapes)
    cat2 = lambda u0, u1: jnp.concatenate([u0.reshape(G, P), u1.reshape(G, P)], axis=0)
    g_are, g_aim, g_ldt = _zoh_bwd(a_re2, a_im2, ldt2, [cat2(t_lr0, t_lr1), cat2(t_li0, t_li1),
                                                         cat2(t_kr0, t_kr1), cat2(t_ki0, t_ki1)])
    g_bmod = _rowk("bmod_grad", lambda i, u0, u1: [u0 + u1], 1, 1, [(t_dmod_x, 'row'), (t_dmod_c, 'row')],
                   [((1, 9 * D), F32, 'row')])[0]
    dmx_all = allp.reshape(N_DEV, RP * PACK_W)[:, :9 * D]
    cots = jnp.concatenate([dmx_all, t_dmod_c, jnp.zeros((7, 9 * D), F32)], axis=0)
    cots_sh = lax.dynamic_slice(cots, (0, chip * NM), (16, NM))
    g_wmod = _outer_sum(acts, cots_sh)
    part = _mm("cctx_part", [(cots_sh[8:16], wm, NM)], 8, D, tm=8, tn=_div(D, 1024), nk=NM // _div(NM, 1152), tb=True,
               epi=ident, outs=[(F32, False)], a_pro=to_bf, b_pro=to_bf)[0]
    parts = _allgather_small("gather_cctx", part).reshape(N_CHIPS, 2, 8, D)[:, 0, 0]

    def cctx_fn(i, pt, ct):
        ds_ = ((pt[0:1] + pt[1:2]) + pt[2:3]) + pt[3:4]
        _, vjp = jax.vjp(lambda v: v * _sigmoid(v), ct)
        return [vjp(ds_)[0]]

    g_cctx = _rowk("cctx_grad", cctx_fn, 1, 1, [(parts, 'vec'), (c_ctx[None], 'row')], [((1, D), F32, 'row')])[0]

    gbig = {'w_ffn1_gate': dwg1, 'w_ffn1_up': dwu1, 'w_ffn1_down': dwd1, 'w_ffn2_gate': dwg2, 'w_ffn2_up': dwu2,
            'w_ffn2_down': dwd2, 'w_in': dwin, 'w_glu': dwglu, 'w_br_attn': dwba, 'w_br_ssm': dwbs, 'w_out': dwout}
    landed = _scatter_grads([gbig[n] for n in big])

    def sum_chip_fn(i, t):
        return [((t[0].astype(F32) + t[1].astype(F32)) + t[2].astype(F32)) + t[3].astype(F32)]

    plane = []
    for n, rb in zip(big, landed):
        R_, C_ = rb.shape[1], rb.shape[2]
        plane.append(_rowk("sum_" + n, sum_chip_fn, R_, _div(R_, max(16, 262144 // C_), mult=16), [(rb, 'row3')],
                           [((R_, C_), F32, 'row')])[0])
    other = _swap_sibling(plane)
    results = {}
    for n, mine, theirs in zip(big, plane, other):
        results[n] = _adamw("adamw_" + n, A[n][0], A['m_' + n][0], A['v_' + n][0], [mine, theirs])
    results['w_mod'] = _adamw("adamw_w_mod", wm, m_w_mod[0], v_w_mod[0], [g_wmod])

    small = ['c_ctx', 'b_mod', 'norm_g', 'q_norm_g', 'k_norm_g', 'ssm_a_re', 'ssm_a_im', 'ssm_log_dt', 'ssm_b_re',
             'ssm_b_im', 'ssm_c_re', 'ssm_c_im', 'ssm_d', 'b_glu']
    ng_full = jnp.concatenate([t_ng1, t_ng2, t_ng3], axis=0)
    gsmall = {
        'c_ctx': g_cctx, 'b_mod': g_bmod, 'norm_g': lax.dynamic_slice(ng_full, (0, chip * Dq), (3, Dq)),
        'q_norm_g': t_qg, 'k_norm_g': t_kg, 'ssm_a_re': g_are, 'ssm_a_im': g_aim, 'ssm_log_dt': g_ldt,
        'ssm_b_re': jnp.stack([t_bre0, t_bre1]), 'ssm_b_im': jnp.stack([t_bim0, t_bim1]),
        'ssm_c_re': jnp.stack([t_cre0, t_cre1]), 'ssm_c_im': jnp.stack([t_cim0, t_cim1]),
        'ssm_d': t_d, 'b_glu': t_bglu}
    sshapes = [A[n].shape for n in small]
    packs = [_pack([A[pre + n] for n in small]) for pre in ('', 'm_', 'v_')] + [_pack([gsmall[n] for n in small])]
    sres = _adamw("adamw_small", packs[0], packs[1], packs[2], [packs[3]])
    sres = [_unpack(b_, sshapes) for b_ in sres]
    for k, n in enumerate(small):
        results[n] = tuple(sres[q][k] for q in range(4))

    order = ['c_ctx', 'w_mod', 'b_mod', 'norm_g', 'w_ffn1_gate', 'w_ffn1_up', 'w_ffn1_down', 'w_in', 'q_norm_g',
             'k_norm_g', 'ssm_a_re', 'ssm_a_im', 'ssm_log_dt', 'ssm_b_re', 'ssm_b_im', 'ssm_c_re', 'ssm_c_im',
             'ssm_d', 'w_glu', 'b_glu', 'w_br_attn', 'w_br_ssm', 'w_out', 'w_ffn2_gate', 'w_ffn2_up', 'w_ffn2_down']
    outs = [loss, grad_x]
    for q in range(4):
        outs += [results[n][q].reshape(A[n].shape) for n in order]
    return tuple(outs)
```

```python
import math

import jax
import jax.numpy as jnp
from jax import lax
from jax.experimental import pallas as pl
from jax.experimental.pallas import tpu as pltpu

F32 = jnp.float32
BF16 = jnp.bfloat16
MESH = pl.DeviceIdType.MESH

NORM_EPS = 1e-6
ROPE_THETA = 10000.0
GRID_W = 64
HEAD_DIM = 128
Q_PER_KV = 4
SSM_GROUP = 16
SSM_STATE = 64
ADAM_LR = 0.001
ADAM_B1 = 0.9
ADAM_B2 = 0.999
ADAM_EPS = 1e-08
ADAM_WD = 0.01
ADAM_STEP = 10

N_CHIPS = 4
N_DEV = 8
LANES = 128
SLAB_CH = 128
SLAB_GROUPS = SLAB_CH // SSM_GROUP
SLAB_ST = SLAB_GROUPS * SSM_STATE
VMEM_LIMIT_BYTES = 56 * 1024 * 1024
PACK_W = 1024


def _cparams(**kw):
    return pltpu.CompilerParams(vmem_limit_bytes=VMEM_LIMIT_BYTES, **kw)


def _div(n, pref, mult=LANES):
    t = (min(pref, n) // mult) * mult
    while t >= mult:
        if n % t == 0:
            return t
        t -= mult
    return n


def _sigmoid(x):
    return jax.nn.sigmoid(x)


def _gelu(x):
    return x * (0.5 * (1.0 + jnp.tanh(math.sqrt(2.0 / math.pi) * (x + 0.044715 * (x * x * x)))))


def _rowk(name, fn, nrows, tr, ins, outs, nc=0):
    nt = nrows // tr
    in_specs, arrays = [], []
    for arr, kind in ins:
        arrays.append(arr)
        if kind == 'row':
            in_specs.append(pl.BlockSpec((tr, arr.shape[1]), lambda i: (i, 0)))
        elif kind == 'xrow':
            in_specs.append(pl.BlockSpec((tr, arr.shape[1]), lambda i: (jnp.maximum(i - nc, 0), 0)))
        elif kind == 'orow':
            in_specs.append(pl.BlockSpec((tr, arr.shape[1]), lambda i: (i + nc, 0)))
        elif kind == 'vec':
            in_specs.append(pl.BlockSpec(arr.shape, lambda i, nd=arr.ndim: (0,) * nd))
        elif kind == 'row3':
            in_specs.append(pl.BlockSpec((arr.shape[0], tr, arr.shape[2]), lambda i: (0, i, 0)))
        elif kind == 'row1':
            in_specs.append(pl.BlockSpec((None, tr, arr.shape[2]), lambda i: (0, i, 0)))
        elif kind[0] == 'ocol':
            _, width, blk = kind
            in_specs.append(pl.BlockSpec((tr, width), lambda i, blk=blk: (i + nc, blk)))
        else:
            _, width, blk = kind
            in_specs.append(pl.BlockSpec((tr, width), lambda i, blk=blk: (i, blk)))
    out_shape, out_specs = [], []
    for shape, dtype, kind in outs:
        out_shape.append(jax.ShapeDtypeStruct(shape, dtype))
        if kind == 'row':
            out_specs.append(pl.BlockSpec((tr, shape[1]), lambda i: (i, 0)))
        elif kind == 'row1':
            out_specs.append(pl.BlockSpec((None, tr, shape[2]), lambda i: (0, i, 0)))
        else:
            out_specs.append(pl.BlockSpec(shape, lambda i, nd=len(shape): (0,) * nd))
    nin = len(ins)

    def body(*refs):
        i = pl.program_id(0)
        res = fn(i, *[r[...] for r in refs[:nin]])
        for (shape, dtype, kind), ref, val in zip(outs, refs[nin:], res):
            if kind in ('row', 'row1'):
                ref[...] = val.astype(dtype)
            else:
                @pl.when(i == 0)
                def _():
                    ref[...] = val.astype(dtype)

                @pl.when(i > 0)
                def _():
                    ref[...] += val.astype(dtype)

    return pl.pallas_call(body, name=name, grid=(nt,), in_specs=in_specs, out_specs=out_specs,
                          out_shape=out_shape, compiler_params=_cparams())(*arrays)


def _mm(name, pairs, M, N, *, tm, tn, nk=1, epi, outs, ta=False, tb=False, rows=(), vecs=(),
        a_pro=None, b_pro=None, n_outer=True):
    nm, nn = M // tm, N // tn
    npair = len(pairs)

    def idx(f):
        if n_outer:
            return lambda j, i, k: f(i, j, k)
        return lambda i, j, k: f(i, j, k)

    in_specs, args = [], []
    for a, b, K in pairs:
        tk = K // nk
        if ta:
            in_specs.append(pl.BlockSpec((tk, tm), idx(lambda i, j, k: (k, i))))
        else:
            in_specs.append(pl.BlockSpec((tm, tk), idx(lambda i, j, k: (i, k))))
        args.append(a)
        if b.ndim == 3:
            if tb:
                per = b.shape[2] // tk
                in_specs.append(pl.BlockSpec((None, tn, tk), idx(lambda i, j, k, per=per: (k // per, j, k % per))))
            else:
                per = b.shape[2] // tn
                in_specs.append(pl.BlockSpec((None, tk, tn), idx(lambda i, j, k, per=per: (j // per, k, j % per))))
        elif tb:
            in_specs.append(pl.BlockSpec((tn, tk), idx(lambda i, j, k: (j, k))))
        else:
            in_specs.append(pl.BlockSpec((tk, tn), idx(lambda i, j, k: (k, j))))
        args.append(b)
    for arr, ro, co in rows:
        in_specs.append(pl.BlockSpec((tm, tn), idx(lambda i, j, k, ro=ro, co=co: (i + ro, j + co))))
        args.append(arr)
    for arr in vecs:
        in_specs.append(pl.BlockSpec((arr.shape[0], tn), idx(lambda i, j, k: (0, j))))
        args.append(arr)
    out_shape, out_specs = [], []
    for dtype, chunked in outs:
        if chunked:
            per = (N // N_CHIPS) // tn
            out_shape.append(jax.ShapeDtypeStruct((N_CHIPS, M, N // N_CHIPS), dtype))
            out_specs.append(pl.BlockSpec((None, tm, tn), idx(lambda i, j, k, per=per: (j // per, i, j % per))))
        else:
            out_shape.append(jax.ShapeDtypeStruct((M, N), dtype))
            out_specs.append(pl.BlockSpec((tm, tn), idx(lambda i, j, k: (i, j))))
    scratch = [pltpu.VMEM((tm, tn), F32) for _ in range(npair)] if nk > 1 else []
    nrow, nvec, nout = len(rows), len(vecs), len(outs)
    dims = (((0 if ta else 1,), (1 if tb else 0,)), ((), ()))

    def body(*refs):
        ab = refs[:2 * npair]
        row_refs = refs[2 * npair:2 * npair + nrow]
        vec_refs = refs[2 * npair + nrow:2 * npair + nrow + nvec]
        out_refs = refs[2 * npair + nrow + nvec:2 * npair + nrow + nvec + nout]
        acc_refs = refs[2 * npair + nrow + nvec + nout:]
        if n_outer:
            j, i, k = pl.program_id(0), pl.program_id(1), pl.program_id(2)
        else:
            i, j, k = pl.program_id(0), pl.program_id(1), pl.program_id(2)

        def part(p):
            av, bv = ab[2 * p][...], ab[2 * p + 1][...]
            if a_pro is not None:
                av = a_pro(av)
            if b_pro is not None:
                bv = b_pro(bv)
            return lax.dot_general(av, bv, dims, preferred_element_type=F32)

        def finish(accs):
            row_index = i * tm + lax.broadcasted_iota(jnp.int32, (tm, 1), 0)
            res = epi(accs, [r[...] for r in row_refs], [v[...] for v in vec_refs], row_index)
            for ref, val in zip(out_refs, res):
                ref[...] = val.astype(ref.dtype)

        if nk == 1:
            finish([part(p) for p in range(npair)])
        else:
            parts = [part(p) for p in range(npair)]

            @pl.when(k == 0)
            def _():
                for p in range(npair):
                    acc_refs[p][...] = parts[p]

            @pl.when(k > 0)
            def _():
                for p in range(npair):
                    acc_refs[p][...] += parts[p]

            @pl.when(k == nk - 1)
            def _():
                finish([acc_refs[p][...] for p in range(npair)])

    grid = (nn, nm, nk) if n_outer else (nm, nn, nk)
    return pl.pallas_call(body, name=name, grid=grid, in_specs=in_specs, out_specs=out_specs,
                          out_shape=out_shape, scratch_shapes=scratch, compiler_params=_cparams())(*args)


def _split3(v):
    v0 = v.astype(BF16)
    r1 = v - v0.astype(F32)
    v1 = r1.astype(BF16)
    v2 = (r1 - v1.astype(F32)).astype(BF16)
    return v0, v1, v2


def _mesh_pos():
    return lax.axis_index("x"), lax.axis_index("y"), lax.axis_index("c")


def _allgather_small(name, x):
    m, n = x.shape

    def body(x_ref, out_ref, send_sems, recv_sems, local_sem):
        xi, yi, ci = _mesh_pos()
        me, sibling = (xi, yi, ci), (xi, yi, 1 - ci)
        chips = [(1 - xi, yi), (xi, 1 - yi), (1 - xi, 1 - yi)]

        def rows(px, py, pc):
            return out_ref.at[pl.ds((4 * px + 2 * py + pc) * m, m), :]

        def copy(k, block, to, src=None):
            return pltpu.make_async_remote_copy(
                src_ref=rows(*block) if src is None else src, dst_ref=rows(*block),
                send_sem=send_sems.at[k], recv_sem=recv_sems.at[k], device_id=to, device_id_type=MESH)

        mine = pltpu.make_async_copy(x_ref, rows(*me), local_sem)
        mine.start()
        first = [copy(0, me, sibling, src=x_ref)]
        first += [copy(1 + j, me, (*chip, ci), src=x_ref) for j, chip in enumerate(chips)]
        for cp in first:
            cp.start()
        passed = [copy(4 + j, (*chip, ci), sibling) for j, chip in enumerate(chips)]
        for j, chip in enumerate(chips):
            copy(1 + j, (*chip, ci), me).wait_recv()
            passed[j].start()
        copy(0, sibling, me).wait_recv()
        for j, chip in enumerate(chips):
            copy(4 + j, (*chip, 1 - ci), me).wait_recv()
        for cp in first + passed:
            cp.wait_send()
        mine.wait()

    return pl.pallas_call(
        body, name=name, out_shape=jax.ShapeDtypeStruct((N_DEV * m, n), x.dtype),
        in_specs=[pl.BlockSpec(memory_space=pltpu.VMEM)], out_specs=pl.BlockSpec(memory_space=pltpu.VMEM),
        scratch_shapes=[pltpu.SemaphoreType.DMA((7,)), pltpu.SemaphoreType.DMA((7,)), pltpu.SemaphoreType.DMA],
        compiler_params=_cparams())(x)


_HBM = pl.BlockSpec(memory_space=pltpu.HBM)
_SEM = pl.BlockSpec(memory_space=pltpu.SEMAPHORE)
_ANY = pl.BlockSpec(memory_space=pl.ANY)
_EFFECT = pltpu.SideEffectType.DATAFLOW_SIDE_EFFECTING


def _in_hbm(v):
    return pltpu.with_memory_space_constraint(v, pltpu.HBM)


def _other_chips(xi, yi):
    return [(1 - xi, yi), (xi, 1 - yi), (1 - xi, 1 - yi)]


def _guarded(north_only, fn):
    if north_only:
        pl.when(lax.axis_index("c") == 1)(fn)
    else:
        fn()


def _split_copies(name, srcs, lands, after, pairs, north_only):
    ns, nl = len(srcs), len(lands)
    dma = pltpu.SemaphoreType.DMA((3 * ns,))
    thru = [pltpu.HBM(v.shape, v.dtype) for v in list(srcs) + list(lands)]

    def start_body(*refs):
        src_refs, land_refs = refs[:ns], refs[ns:ns + nl]
        descs = pairs(src_refs, land_refs, refs[ns + nl + 1], refs[ns + nl + 2])

        def go():
            for send, _ in descs:
                send.start()

        _guarded(north_only, go)
        refs[-1][...] = jnp.zeros_like(refs[-1])

    res = pl.pallas_call(
        start_body, name=name + "_start",
        out_shape=(dma, dma, *thru, jax.ShapeDtypeStruct((8, LANES), F32)),
        in_specs=[_HBM] * (ns + nl) + [_ANY],
        out_specs=(_SEM, _SEM, *([_HBM] * (ns + nl)), pl.BlockSpec(memory_space=pltpu.VMEM)),
        input_output_aliases={k: 2 + k for k in range(ns + nl)},
        compiler_params=_cparams(has_side_effects=_EFFECT),
    )(*[_in_hbm(v) for v in srcs], *[_in_hbm(v) for v in lands], after)
    send_sems, recv_sems, token = res[0], res[1], res[-1]
    carried = res[2:2 + ns + nl]

    def finish(after_work):
        def wait_body(*refs):
            src_refs, land_refs = refs[:ns], refs[ns:ns + nl]
            descs = pairs(src_refs, land_refs, refs[ns + nl], refs[ns + nl + 1])

            def go():
                for send, recv in descs:
                    send.wait_send()
                    recv.wait_recv()

            _guarded(north_only, go)

        out = pl.pallas_call(
            wait_body, name=name + "_wait", out_shape=tuple(thru),
            in_specs=[_HBM] * (ns + nl) + [_SEM, _SEM, _ANY], out_specs=tuple([_HBM] * (ns + nl)),
            input_output_aliases={k: k for k in range(ns + nl)},
            compiler_params=_cparams(has_side_effects=_EFFECT),
        )(*carried, send_sems, recv_sems, after_work)
        return list(out[ns:])

    return token, finish


def _gather_split(name, shards, after):
    lands = [lax.empty((N_CHIPS,) + s.shape, s.dtype) for s in shards]

    def pairs(src_refs, land_refs, send_sems, recv_sems):
        xi, yi, _ = _mesh_pos()
        out = []
        for a in range(len(shards)):
            for j, (px, py) in enumerate(_other_chips(xi, yi)):
                def to_slot(slot, a=a, j=j, px=px, py=py):
                    return pltpu.make_async_remote_copy(
                        src_ref=src_refs[a], dst_ref=land_refs[a].at[slot], send_sem=send_sems.at[3 * a + j],
                        recv_sem=recv_sems.at[3 * a + j], device_id=(px, py, 1), device_id_type=MESH)
                out.append((to_slot(2 * xi + yi), to_slot(2 * px + py)))
        return out

    return _split_copies(name, shards, lands, after, pairs, north_only=True)


def _scatter_split(name, grads, after):
    lands = [lax.empty((3,) + g.shape[1:], g.dtype) for g in grads]

    def pairs(src_refs, land_refs, send_sems, recv_sems):
        xi, yi, ci = _mesh_pos()
        out = []
        for a in range(len(grads)):
            for j, (px, py) in enumerate(_other_chips(xi, yi)):
                cp = pltpu.make_async_remote_copy(
                    src_ref=src_refs[a].at[2 * px + py], dst_ref=land_refs[a].at[j], send_sem=send_sems.at[3 * a + j],
                    recv_sem=recv_sems.at[3 * a + j], device_id=(px, py, ci), device_id_type=MESH)
                out.append((cp, cp))
        return out

    return _split_copies(name, grads, lands, after, pairs, north_only=False)


def _gather_finish(name, shards, lands):
    na = len(shards)

    def body(*refs):
        ins, outs = refs[:na], refs[2 * na:3 * na]
        send_sems, recv_sems, local_sems = refs[3 * na:]
        xi, yi, ci = _mesh_pos()
        chip = 2 * xi + yi
        local = [pltpu.make_async_copy(ins[a], outs[a].at[chip], local_sems.at[a]) for a in range(na)]
        for cp in local:
            cp.start()
        passes = [pltpu.make_async_remote_copy(
            src_ref=outs[a].at[2 * px + py], dst_ref=outs[a].at[2 * px + py],
            send_sem=send_sems.at[a, j], recv_sem=recv_sems.at[a, j], device_id=(xi, yi, 0), device_id_type=MESH)
            for a in range(na) for j, (px, py) in enumerate(_other_chips(xi, yi))]

        @pl.when(ci == 1)
        def _():
            for cp in passes:
                cp.start()
            for cp in passes:
                cp.wait_send()

        @pl.when(ci == 0)
        def _():
            for cp in passes:
                cp.wait_recv()

        for cp in local:
            cp.wait()

    return pl.pallas_call(
        body, name=name, out_shape=[jax.ShapeDtypeStruct(v.shape, v.dtype) for v in lands],
        in_specs=[_ANY] * (2 * na), out_specs=[_ANY] * na,
        input_output_aliases={na + a: a for a in range(na)},
        scratch_shapes=[pltpu.SemaphoreType.DMA((na, 3)), pltpu.SemaphoreType.DMA((na, 3)),
                        pltpu.SemaphoreType.DMA((na,))],
        compiler_params=_cparams())(*shards, *lands)


def _swap_sibling(name, arrs):
    na = len(arrs)

    def body(*refs):
        ins, outs = refs[:na], refs[na:2 * na]
        send_sems, recv_sems = refs[2 * na:]
        xi, yi, ci = _mesh_pos()
        copies = [pltpu.make_async_remote_copy(
            src_ref=ins[a], dst_ref=outs[a], send_sem=send_sems.at[a], recv_sem=recv_sems.at[a],
            device_id=(xi, yi, 1 - ci), device_id_type=MESH) for a in range(na)]
        for cp in copies:
            cp.start()
        for cp in copies:
            cp.wait()

    return pl.pallas_call(
        body, name=name,
        out_shape=[jax.ShapeDtypeStruct(g.shape, g.dtype) for g in arrs],
        in_specs=[_ANY] * na, out_specs=[_ANY] * na,
        scratch_shapes=[pltpu.SemaphoreType.DMA((na,)), pltpu.SemaphoreType.DMA((na,))],
        compiler_params=_cparams())(*arrs)


def _attn_tiles(L, Lc, D):
    tq = min(256, Lc)
    return tq, L // tq, Lc // tq, D // HEAD_DIM // Q_PER_KV


def _attn_probs(q, k):
    s = lax.dot_general(q, k, (((1,), (1,)), ((), ())), preferred_element_type=F32) * (HEAD_DIM ** -0.5)
    e = jnp.exp(s - jnp.max(s, axis=-1, keepdims=True))
    return e * (1.0 / jnp.sum(e, axis=-1, keepdims=True))


def _attn_fwd(qr, kr, v, L, Lc, D):
    T = L + Lc
    tq, nq, qoff, nkv = _attn_tiles(L, Lc, D)

    def body(q_ref, k_ref, v_ref, o_ref):
        p = _attn_probs(q_ref[...], k_ref[...])
        o_ref[...] = jnp.dot(p.astype(BF16), v_ref[...], preferred_element_type=F32).astype(o_ref.dtype)

    kv_spec = pl.BlockSpec((T, HEAD_DIM), lambda h, r, q: (0, h))
    return pl.pallas_call(
        body, name="attn_fwd", grid=(nkv, Q_PER_KV, nq),
        in_specs=[pl.BlockSpec((tq, HEAD_DIM), lambda h, r, q: (q + qoff, h * Q_PER_KV + r)), kv_spec, kv_spec],
        out_specs=pl.BlockSpec((tq, HEAD_DIM), lambda h, r, q: (q, h * Q_PER_KV + r)),
        out_shape=jax.ShapeDtypeStruct((L, D), BF16), compiler_params=_cparams())(qr, kr, v)


def _attn_bwd(qr, kr, v, do, L, Lc, D):
    T = L + Lc
    tq, nq, qoff, nkv = _attn_tiles(L, Lc, D)
    scale = HEAD_DIM ** -0.5

    def body(q_ref, k_ref, v_ref, do_ref, dq_ref, dk_ref, dv_ref):
        first = jnp.logical_and(pl.program_id(1) == 0, pl.program_id(2) == 0)
        q, k, dout = q_ref[...], k_ref[...], do_ref[...]
        p = _attn_probs(q, k)
        dp = lax.dot_general(dout, v_ref[...], (((1,), (1,)), ((), ())), preferred_element_type=F32)
        ds = (p * (dp - jnp.sum(p * dp, axis=-1, keepdims=True)) * scale).astype(BF16)
        dq_ref[...] = jnp.dot(ds, k, preferred_element_type=F32)
        dk = lax.dot_general(ds, q, (((0,), (0,)), ((), ())), preferred_element_type=F32)
        dv = lax.dot_general(p.astype(BF16), dout, (((0,), (0,)), ((), ())), preferred_element_type=F32)

        @pl.when(first)
        def _():
            dk_ref[...] = dk
            dv_ref[...] = dv

        @pl.when(jnp.logical_not(first))
        def _():
            dk_ref[...] += dk
            dv_ref[...] += dv

    kv_spec = pl.BlockSpec((T, HEAD_DIM), lambda h, r, q: (0, h))
    q_spec = pl.BlockSpec((tq, HEAD_DIM), lambda h, r, q: (q + qoff, h * Q_PER_KV + r))
    o_spec = pl.BlockSpec((tq, HEAD_DIM), lambda h, r, q: (q, h * Q_PER_KV + r))
    return pl.pallas_call(
        body, name="attn_bwd", grid=(nkv, Q_PER_KV, nq),
        in_specs=[q_spec, kv_spec, kv_spec, o_spec], out_specs=[o_spec, kv_spec, kv_spec],
        out_shape=[jax.ShapeDtypeStruct((L, D), F32), jax.ShapeDtypeStruct((T, D // Q_PER_KV), F32),
                   jax.ShapeDtypeStruct((T, D // Q_PER_KV), F32)],
        compiler_params=_cparams())(qr, kr, v, do)


def _scan_tile(xr, xi, pw_re, pw_im, lanes, reverse):
    tt = xr.shape[0]
    rows = lax.broadcasted_iota(jnp.int32, (tt, 1), 0)
    for k in range(tt.bit_length() - 1):
        d = 1 << k
        shift = tt - d if reverse else d
        keep = rows < tt - d if reverse else rows >= d
        sr = jnp.where(keep, pltpu.roll(xr, shift, 0), 0.0)
        si = jnp.where(keep, pltpu.roll(xi, shift, 0), 0.0)
        pr, pi = pw_re[k:k + 1, lanes], pw_im[k:k + 1, lanes]
        xr, xi = xr + (pr * sr - pi * si), xi + (pr * si + pi * sr)
    return xr, xi


def _scan_init(lr, li, pw_re, pw_im, w_re, w_im, carry_re, carry_im, nslab, reverse):
    tt = w_re.shape[0]
    carry_re[...] = jnp.zeros_like(carry_re)
    carry_im[...] = jnp.zeros_like(carry_im)
    pr, pi = lr, li
    for k in range(tt.bit_length() - 1):
        pw_re[k:k + 1, :] = pr
        pw_im[k:k + 1, :] = pi
        pr, pi = pr * pr - pi * pi, 2.0 * pr * pi
    rows = lax.broadcasted_iota(jnp.int32, (tt, 1), 0)
    edge = rows == (tt - 1 if reverse else 0)
    for j in range(nslab):
        lanes = slice(j * SLAB_ST, (j + 1) * SLAB_ST)
        wr, wi = _scan_tile(jnp.where(edge, lr[:, lanes], 0.0), jnp.where(edge, li[:, lanes], 0.0),
                            pw_re, pw_im, lanes, reverse)
        w_re[:, lanes] = wr
        w_im[:, lanes] = wi


def _ssm_tiles(T, Lc):
    tt = min(128, Lc)
    return tt, T // tt, Lc // tt


def _ssm_fwd(name, u, bbd, cbd_re, cbd_im, lam_re, lam_im, coef_re, coef_im, Lc, reverse):
    T, W = u.shape
    nslab = W // SLAB_CH
    NS = nslab * SLAB_ST
    tt, nt, nc = _ssm_tiles(T, Lc)
    if reverse:
        tile = lambda s: jnp.where(s < nc, nc - 1 - s, nt - 1 - (s - nc))
    else:
        tile = lambda s: s

    def body(u_ref, b_ref, cr_ref, ci_ref, lr_ref, li_ref, kr_ref, ki_ref, hr_ref, hi_ref, y_ref,
             pw_re, pw_im, w_re, w_im, carry_re, carry_im):
        @pl.when(pl.program_id(0) == 0)
        def _():
            _scan_init(lr_ref[...], li_ref[...], pw_re, pw_im, w_re, w_im, carry_re, carry_im, nslab, reverse)

        edge_row = 0 if reverse else tt - 1
        for j in range(nslab):
            lanes = slice(j * SLAB_ST, (j + 1) * SLAB_ST)
            bu = jnp.dot(u_ref[:, j * SLAB_CH:(j + 1) * SLAB_CH], b_ref[j], preferred_element_type=F32)
            br, bi = bu[:, :SLAB_ST], bu[:, SLAB_ST:]
            kr, ki = kr_ref[:, lanes], ki_ref[:, lanes]
            hr, hi = _scan_tile(kr * br - ki * bi, kr * bi + ki * br, pw_re, pw_im, lanes, reverse)
            car, cai = carry_re[:, lanes], carry_im[:, lanes]
            wr, wi = w_re[:, lanes], w_im[:, lanes]
            hr = hr + (wr * car - wi * cai)
            hi = hi + (wr * cai + wi * car)
            carry_re[:, lanes] = hr[edge_row:edge_row + 1, :]
            carry_im[:, lanes] = hi[edge_row:edge_row + 1, :]
            hrb, hib = hr.astype(BF16), hi.astype(BF16)
            hr_ref[:, lanes] = hrb
            hi_ref[:, lanes] = hib
            y_ref[:, j * SLAB_CH:(j + 1) * SLAB_CH] = (
                jnp.dot(hrb, cr_ref[j], preferred_element_type=F32)
                - jnp.dot(hib, ci_ref[j], preferred_element_type=F32))

    whole3 = lambda arr: pl.BlockSpec(arr.shape, lambda s: (0, 0, 0))
    vec = pl.BlockSpec((1, NS), lambda s: (0, 0))
    return pl.pallas_call(
        body, name=name, grid=(nt,),
        in_specs=[pl.BlockSpec((tt, W), lambda s: (tile(s), 0)), whole3(bbd), whole3(cbd_re), whole3(cbd_im),
                  vec, vec, vec, vec],
        out_specs=[pl.BlockSpec((tt, NS), lambda s: (tile(s), 0)), pl.BlockSpec((tt, NS), lambda s: (tile(s), 0)),
                   pl.BlockSpec((tt, W), lambda s: (tile(s), 0))],
        out_shape=[jax.ShapeDtypeStruct((T, NS), BF16), jax.ShapeDtypeStruct((T, NS), BF16),
                   jax.ShapeDtypeStruct((T, W), F32)],
        scratch_shapes=[pltpu.VMEM((8, NS), F32), pltpu.VMEM((8, NS), F32), pltpu.VMEM((tt, NS), F32),
                        pltpu.VMEM((tt, NS), F32), pltpu.VMEM((1, NS), F32), pltpu.VMEM((1, NS), F32)],
        compiler_params=_cparams())(u, bbd, cbd_re, cbd_im, lam_re, lam_im, coef_re, coef_im)


def _ssm_bwd(name, dy, h_re, h_im, u, bbd, bbdt_re, bbdt_im, cbdt_re, cbdt_im, lam_re, lam_im,
             coef_re, coef_im, Lc, reverse):
    T, W = u.shape
    nslab = W // SLAB_CH
    NS = nslab * SLAB_ST
    tt, nt, nc = _ssm_tiles(T, Lc)
    adj_reverse = not reverse
    if reverse:
        tile = lambda s: jnp.where(s < nt - nc, nc + s, s - (nt - nc))
    else:
        tile = lambda s: nt - 1 - s

    def body(dy_ref, hr_ref, hi_ref, u_ref, b_ref, btr_ref, bti_ref, ctr_ref, cti_ref, lr_ref, li_ref,
             kr_ref, ki_ref, du_ref, dlr_ref, dli_ref, dkr_ref, dki_ref, db_ref, dcr_ref, dci_ref,
             pw_re, pw_im, w_re, w_im, carry_re, carry_im):
        @pl.when(pl.program_id(0) == 0)
        def _():
            _scan_init(lr_ref[...], -li_ref[...], pw_re, pw_im, w_re, w_im, carry_re, carry_im, nslab, adj_reverse)
            for ref in (dlr_ref, dli_ref, dkr_ref, dki_ref, db_ref, dcr_ref, dci_ref):
                ref[...] = jnp.zeros_like(ref)

        rows = lax.broadcasted_iota(jnp.int32, (tt, 1), 0)
        edge_row = 0 if adj_reverse else tt - 1
        far_row = tt - 1 if adj_reverse else 0
        tn_dims = (((0,), (0,)), ((), ()))
        for j in range(nslab):
            lanes = slice(j * SLAB_ST, (j + 1) * SLAB_ST)
            chans = slice(j * SLAB_CH, (j + 1) * SLAB_CH)
            dys, us = dy_ref[:, chans], u_ref[:, chans]
            er = jnp.dot(dys, ctr_ref[j], preferred_element_type=F32)
            ei = -jnp.dot(dys, cti_ref[j], preferred_element_type=F32)
            ar, ai = _scan_tile(er, ei, pw_re, pw_im, lanes, adj_reverse)
            car, cai = carry_re[:, lanes], carry_im[:, lanes]
            wr, wi = w_re[:, lanes], w_im[:, lanes]
            ar = ar + (wr * car - wi * cai)
            ai = ai + (wr * cai + wi * car)
            shift = tt - 1 if adj_reverse else 1
            nr = jnp.where(rows == far_row, car, pltpu.roll(ar, shift, 0))
            ni = jnp.where(rows == far_row, cai, pltpu.roll(ai, shift, 0))
            carry_re[:, lanes] = ar[edge_row:edge_row + 1, :]
            carry_im[:, lanes] = ai[edge_row:edge_row + 1, :]
            hrb, hib = hr_ref[:, lanes], hi_ref[:, lanes]
            hr, hi = hrb.astype(F32), hib.astype(F32)
            dlr_ref[:, lanes] += jnp.sum(nr * hr + ni * hi, axis=0, keepdims=True)
            dli_ref[:, lanes] += jnp.sum(ni * hr - nr * hi, axis=0, keepdims=True)
            bu = jnp.dot(us, b_ref[j], preferred_element_type=F32)
            br, bi = bu[:, :SLAB_ST], bu[:, SLAB_ST:]
            dkr_ref[:, lanes] += jnp.sum(ar * br + ai * bi, axis=0, keepdims=True)
            dki_ref[:, lanes] += jnp.sum(ai * br - ar * bi, axis=0, keepdims=True)
            kr, ki = kr_ref[:, lanes], ki_ref[:, lanes]
            dbr = (ar * kr + ai * ki).astype(BF16)
            dbi = (ai * kr - ar * ki).astype(BF16)
            du_ref[:, chans] = (jnp.dot(dbr, btr_ref[j], preferred_element_type=F32)
                                + jnp.dot(dbi, bti_ref[j], preferred_element_type=F32))
            db_ref[j, :, :SLAB_ST] += lax.dot_general(us, dbr, tn_dims, preferred_element_type=F32)
            db_ref[j, :, SLAB_ST:] += lax.dot_general(us, dbi, tn_dims, preferred_element_type=F32)
            dcr_ref[j] += lax.dot_general(hrb, dys, tn_dims, preferred_element_type=F32)
            dci_ref[j] -= lax.dot_general(hib, dys, tn_dims, preferred_element_type=F32)

    whole3 = lambda arr: pl.BlockSpec(arr.shape, lambda s: (0, 0, 0))
    vec = pl.BlockSpec((1, NS), lambda s: (0, 0))
    row_w = pl.BlockSpec((tt, W), lambda s: (tile(s), 0))
    row_s = pl.BlockSpec((tt, NS), lambda s: (tile(s), 0))
    return pl.pallas_call(
        body, name=name, grid=(nt,),
        in_specs=[row_w, row_s, row_s, row_w, whole3(bbd), whole3(bbdt_re), whole3(bbdt_im), whole3(cbdt_re),
                  whole3(cbdt_im), vec, vec, vec, vec],
        out_specs=[row_w, vec, vec, vec, vec, whole3(bbd), whole3(bbdt_re), whole3(bbdt_re)],
        out_shape=[jax.ShapeDtypeStruct((T, W), F32)] + [jax.ShapeDtypeStruct((1, NS), F32)] * 4
        + [jax.ShapeDtypeStruct(bbd.shape, F32), jax.ShapeDtypeStruct(bbdt_re.shape, F32),
           jax.ShapeDtypeStruct(bbdt_re.shape, F32)],
        scratch_shapes=[pltpu.VMEM((8, NS), F32), pltpu.VMEM((8, NS), F32), pltpu.VMEM((tt, NS), F32),
                        pltpu.VMEM((tt, NS), F32), pltpu.VMEM((1, NS), F32), pltpu.VMEM((1, NS), F32)],
        compiler_params=_cparams())(dy, h_re, h_im, u, bbd, bbdt_re, bbdt_im, cbdt_re, cbdt_im,
                                    lam_re, lam_im, coef_re, coef_im)


def _zoh_math(a_re, a_im, log_dt):
    dt = jnp.exp(log_dt)
    mag = jnp.exp(a_re * dt)
    lb_re = mag * jnp.cos(a_im * dt)
    lb_im = mag * jnp.sin(a_im * dt)
    den = a_re * a_re + a_im * a_im
    coef_re = ((lb_re - 1.0) * a_re + lb_im * a_im) / den
    coef_im = (lb_im * a_re - (lb_re - 1.0) * a_im) / den
    return lb_re, lb_im, coef_re, coef_im


def _zoh_fwd(a_re, a_im, log_dt):
    def body(ar, ai, ld, o0, o1, o2, o3):
        for ref, val in zip((o0, o1, o2, o3), _zoh_math(ar[...], ai[...], ld[...])):
            ref[...] = val

    return pl.pallas_call(body, name="zoh_fwd", out_shape=[jax.ShapeDtypeStruct(a_re.shape, F32)] * 4,
                          compiler_params=_cparams())(a_re, a_im, log_dt)


def _zoh_bwd(a_re, a_im, log_dt, cots):
    def body(ar, ai, ld, c0, c1, c2, c3, o0, o1, o2):
        _, vjp = jax.vjp(_zoh_math, ar[...], ai[...], ld[...])
        for ref, val in zip((o0, o1, o2), vjp((c0[...], c1[...], c2[...], c3[...]))):
            ref[...] = val

    return pl.pallas_call(
        body, name="zoh_bwd",
        out_shape=[jax.ShapeDtypeStruct(a_re.shape, F32), jax.ShapeDtypeStruct(a_re.shape, F32),
                   jax.ShapeDtypeStruct(log_dt.shape, F32)],
        compiler_params=_cparams())(a_re, a_im, log_dt, *cots)


def _outer_sum(acts, cots):
    D, N = acts.shape[1], cots.shape[1]
    tm, tn = _div(D, 512), _div(N, 1152)
    dims = (((0,), (0,)), ((), ()))

    def body(a_ref, b_ref, o_ref):
        a = a_ref[...]
        aa = _split3(a * _sigmoid(a))
        bb = _split3(b_ref[...])
        acc = None
        for ia in range(3):
            for ib in range(3 - ia):
                t = lax.dot_general(aa[ia], bb[ib], dims, preferred_element_type=F32)
                acc = t if acc is None else acc + t
        o_ref[...] = acc

    return pl.pallas_call(
        body, name="mod_dw", grid=(D // tm, N // tn),
        in_specs=[pl.BlockSpec((16, tm), lambda i, j: (0, i)), pl.BlockSpec((16, tn), lambda i, j: (0, j))],
        out_specs=pl.BlockSpec((tm, tn), lambda i, j: (i, j)),
        out_shape=jax.ShapeDtypeStruct((D, N), F32), compiler_params=_cparams())(acts, cots)


def _adamw_math(w, g, m, v):
    m = ADAM_B1 * m + (1.0 - ADAM_B1) * g
    v = ADAM_B2 * v + (1.0 - ADAM_B2) * (g * g)
    m_hat = m / (1.0 - ADAM_B1 ** ADAM_STEP)
    v_hat = v / (1.0 - ADAM_B2 ** ADAM_STEP)
    delta = -ADAM_LR * (m_hat / (jnp.sqrt(v_hat) + ADAM_EPS) + ADAM_WD * w)
    return delta, m, v


def _adamw(name, w, m, v, gparts):
    R, C = w.shape[-2:]
    kind = 'row1' if w.ndim == 3 else 'row'
    tr = _div(R, max(8, 262144 // C), mult=8)

    def fn(i, wv, mv, vv, *gs):
        g = gs[0]
        for extra in gs[1:]:
            g = g + extra
        return (g,) + _adamw_math(wv, g, mv, vv)

    return _rowk(name, fn, R, tr, [(w, kind), (m, kind), (v, kind)] + [(g, 'row') for g in gparts],
                 [(w.shape, F32, kind)] * 4)


def _pack(pieces, rows_mult=8):
    flat = jnp.concatenate([p.reshape(-1).astype(F32) for p in pieces])
    unit = rows_mult * PACK_W
    total = -(-flat.shape[0] // unit) * unit
    return jnp.pad(flat, (0, total - flat.shape[0])).reshape(total // PACK_W, PACK_W)


def _unpack(buf, shapes):
    flat = buf.reshape(-1)
    out, off = [], 0
    for s in shapes:
        n = math.prod(s)
        out.append(flat[off:off + n].reshape(s))
        off += n
    return out


def _bd_expand(t):
    S, g, a, b = t.shape
    eye = jnp.eye(g, dtype=t.dtype)
    return (t[:, :, :, None, :] * eye[None, :, None, :, None]).reshape(S, g * a, g * b)


def _bd_extract(t, a, b):
    S = t.shape[0]
    g = t.shape[1] // a
    eye = jnp.eye(g, dtype=t.dtype)
    return jnp.sum(t.reshape(S, g, a, g, b) * eye[None, :, None, :, None], axis=3)


def _rope_tables(L, Lc):
    rows = L // GRID_W
    row_ids = jnp.broadcast_to(jnp.arange(rows)[:, None], (rows, GRID_W)).reshape(-1).astype(F32)
    col_ids = jnp.broadcast_to(jnp.arange(GRID_W)[None, :], (rows, GRID_W)).reshape(-1).astype(F32)
    quarter = HEAD_DIM // 4
    inv_freq = ROPE_THETA ** (-jnp.arange(quarter, dtype=F32) / quarter)
    ang_r = row_ids[:, None] * inv_freq
    ang_c = col_ids[:, None] * inv_freq
    cos = jnp.concatenate([jnp.cos(ang_r), jnp.cos(ang_r), jnp.cos(ang_c), jnp.cos(ang_c)], axis=1)
    sin = jnp.concatenate([-jnp.sin(ang_r), jnp.sin(ang_r), -jnp.sin(ang_c), jnp.sin(ang_c)], axis=1)
    cos = jnp.concatenate([jnp.ones((Lc, HEAD_DIM), F32), cos], axis=0)
    sin = jnp.concatenate([jnp.zeros((Lc, HEAD_DIM), F32), sin], axis=0)
    return cos, sin


def _rot(v):
    lane = lax.broadcasted_iota(jnp.int32, (1, HEAD_DIM), 1)
    first = (lane % (HEAD_DIM // 2)) < (HEAD_DIM // 4)
    return jnp.where(first, pltpu.roll(v, HEAD_DIM - HEAD_DIM // 4, 1), pltpu.roll(v, HEAD_DIM // 4, 1))


def _head_norm(xh, g):
    return xh * lax.rsqrt(jnp.mean(xh * xh, axis=-1, keepdims=True) + NORM_EPS) * g


def _norm_mod(xv, g, sh, sc):
    r = lax.rsqrt(jnp.mean(xv * xv, axis=-1, keepdims=True) + NORM_EPS)
    return (xv * r) * g * (1.0 + sc) + sh


def kernel(x, c, ctx, c_ctx, w_mod, b_mod, norm_g, w_ffn1_gate, w_ffn1_up, w_ffn1_down, w_in, q_norm_g, k_norm_g, ssm_a_re, ssm_a_im, ssm_log_dt, ssm_b_re, ssm_b_im, ssm_c_re, ssm_c_im, ssm_d, w_glu, b_glu, w_br_attn, w_br_ssm, w_out, w_ffn2_gate, w_ffn2_up, w_ffn2_down, loss_target, m_c_ctx, m_w_mod, m_b_mod, m_norm_g, m_w_ffn1_gate, m_w_ffn1_up, m_w_ffn1_down, m_w_in, m_q_norm_g, m_k_norm_g, m_ssm_a_re, m_ssm_a_im, m_ssm_log_dt, m_ssm_b_re, m_ssm_b_im, m_ssm_c_re, m_ssm_c_im, m_ssm_d, m_w_glu, m_b_glu, m_w_br_attn, m_w_br_ssm, m_w_out, m_w_ffn2_gate, m_w_ffn2_up, m_w_ffn2_down, v_c_ctx, v_w_mod, v_b_mod, v_norm_g, v_w_ffn1_gate, v_w_ffn1_up, v_w_ffn1_down, v_w_in, v_q_norm_g, v_k_norm_g, v_ssm_a_re, v_ssm_a_im, v_ssm_log_dt, v_ssm_b_re, v_ssm_b_im, v_ssm_c_re, v_ssm_c_im, v_ssm_d, v_w_glu, v_b_glu, v_w_br_attn, v_w_br_ssm, v_w_out, v_w_ffn2_gate, v_w_ffn2_up, v_w_ffn2_down):
    A = dict(locals())
    xi, yi, ci = _mesh_pos()
    chip = 2 * xi + yi
    me = 4 * xi + 2 * yi + ci
    L, D = x.shape[1], x.shape[2]
    Lc = ctx.shape[1]
    T = L + Lc
    F4 = w_ffn1_gate.shape[2]
    F = N_CHIPS * F4
    W, KV, Dq = D // 2, D // 4, D // 4
    G = W // SSM_GROUP
    P, E = SSM_STATE, SSM_GROUP
    NS = G * P
    nslab = W // SLAB_CH
    tr = min(256, Lc)
    ncr = Lc // tr
    assert L % tr == 0 and Lc % tr == 0 and W % SLAB_CH == 0 and D % (4 * LANES) == 0

    def sel(i, v):
        return v if v.shape[0] == 1 else jnp.where(i < ncr, v[0:1], v[1:2])

    def put(i, v, nrow):
        if nrow == 1:
            return v
        which = (i >= ncr).astype(jnp.int32)
        r2 = lax.broadcasted_iota(jnp.int32, (nrow, 1), 0)
        return jnp.where(r2 == which, jnp.broadcast_to(v, (nrow, v.shape[1])), 0.0)

    ident = lambda accs, rows, vecs, ri: [accs[0]]

    NM = w_mod.shape[2]
    first = jnp.zeros((8, D), F32).at[0].set(c[0]).at[1:4, :Dq].set(norm_g[0])
    g0 = _allgather_small("gather_c", first).reshape(N_CHIPS, 2, 8, D)
    c_all = g0[:, :, 0].reshape(N_DEV, D)
    ng = jnp.transpose(g0[:, 0, 1:4, :Dq], (1, 0, 2)).reshape(3, D)
    acts = jnp.concatenate([c_all, c_ctx[None], jnp.zeros((7, D), F32)], axis=0)
    wm = w_mod[0]
    b_shard = lax.dynamic_slice(b_mod[0], (chip * NM,), (NM,))[None]
    silu_bf = lambda a: (a * _sigmoid(a)).astype(BF16)
    to_bf = lambda b: b.astype(BF16)
    mod_part = _mm("mod_fwd", [(acts, wm, D)], 16, NM, tm=16, tn=_div(NM, 1152),
                   epi=lambda accs, rows, vecs, ri: [accs[0] + vecs[0]], outs=[(F32, False)],
                   vecs=[b_shard], a_pro=silu_bf, b_pro=to_bf)[0]
    mg = _allgather_small("gather_mod", mod_part).reshape(N_CHIPS, 2, 16, NM)[:, 0]
    mod_all = jnp.transpose(mg, (1, 0, 2)).reshape(16, N_CHIPS * NM)
    mod_x = lax.dynamic_slice(mod_all, (me, 0), (1, 9 * D))
    mod_c = jnp.where(jnp.arange(9 * D)[None] < 5 * D, mod_all[8:9], 0.0)
    modv = jnp.concatenate([mod_c, mod_x], axis=0)
    mv = lambda k: modv[:, k * D:(k + 1) * D]
    sh1, sc1, g1, sh2, sc2 = mv(0), mv(1), mv(2), mv(3), mv(4)
    g2, sh3, sc3, g3 = mv(5)[1:2], mv(6)[1:2], mv(7)[1:2], mv(8)[1:2]

    big = ['w_ffn1_gate', 'w_ffn1_up', 'w_ffn1_down', 'w_ffn2_gate', 'w_ffn2_up', 'w_ffn2_down',
           'w_in', 'w_glu', 'w_br_attn', 'w_br_ssm', 'w_out']
    row_sharded = {'w_ffn1_down', 'w_ffn2_down', 'w_glu', 'w_br_attn', 'w_out'}
    groups = [big[0:3], big[6:11], big[3:6]]
    shards = {n: A[n][0].astype(BF16) for n in big}
    tok, gather_finish = modv, []
    for gi, names in enumerate(groups):
        tok, fin = _gather_split("gather_w%d" % gi, [shards[n] for n in names], tok)
        gather_finish.append(fin)
    ng = ng + tok[0:1, 0:1]
    Wt = {}

    def weights_ready(gi, after_work):
        names = groups[gi]
        lands = gather_finish[gi](after_work)
        full = _gather_finish("gather_w%d_pass" % gi, [shards[n] for n in names], lands)
        for n, gw in zip(names, full):
            Wt[n] = gw.reshape(N_CHIPS * gw.shape[1], gw.shape[2]) if n in row_sharded else gw

    weights_ready(0, tok)

    a_re2, a_im2 = ssm_a_re[0].reshape(2 * G, P), ssm_a_im[0].reshape(2 * G, P)
    ldt2 = ssm_log_dt[0].reshape(2 * G, 1)
    zoh = _zoh_fwd(a_re2, a_im2, ldt2)
    lam_re, lam_im, coef_re, coef_im = [[z[d * G:(d + 1) * G].reshape(1, NS) for d in range(2)] for z in zoh]
    bd_b = lambda b: _bd_expand(jnp.transpose(b, (0, 2, 1)).reshape(nslab, SLAB_GROUPS, E, P))
    bd_c = lambda cc: _bd_expand(jnp.transpose(cc, (0, 2, 1)).reshape(nslab, SLAB_GROUPS, P, E))
    bbd, bbdt_re, bbdt_im, cbd_re, cbd_im, cbdt_re, cbdt_im = [], [], [], [], [], [], []
    for d in range(2):
        br_, bi_ = bd_b(ssm_b_re[0, d]).astype(BF16), bd_b(ssm_b_im[0, d]).astype(BF16)
        cr_, ci_ = bd_c(ssm_c_re[0, d]).astype(BF16), bd_c(ssm_c_im[0, d]).astype(BF16)
        bbd.append(jnp.concatenate([br_, bi_], axis=2))
        bbdt_re.append(jnp.transpose(br_, (0, 2, 1)))
        bbdt_im.append(jnp.transpose(bi_, (0, 2, 1)))
        cbd_re.append(cr_)
        cbd_im.append(ci_)
        cbdt_re.append(jnp.transpose(cr_, (0, 2, 1)))
        cbdt_im.append(jnp.transpose(ci_, (0, 2, 1)))
    cos_t, sin_t = _rope_tables(L, Lc)
    qg, kg = q_norm_g, k_norm_g

    def norm_mod(name, xv, g, sh, sc):
        rows = xv.shape[0]
        return _rowk(name, lambda i, xt, gt, sht, sct: [_norm_mod(xt, gt, sel(i, sht), sel(i, sct))],
                     rows, tr, [(xv, 'row'), (g, 'vec'), (sh, 'vec'), (sc, 'vec')], [((rows, D), BF16, 'row')])[0]

    def swiglu_epi(accs, rows, vecs, ri):
        a_, b_ = accs
        return [a_, b_, a_ * _sigmoid(a_) * b_]

    def res_epi(coef):
        def epi(accs, rows, vecs, ri):
            gate = vecs[0]
            if gate.shape[0] == 2:
                gate = jnp.where(ri < Lc, gate[0:1], gate[1:2])
            return [accs[0], rows[0] + (coef * gate) * accs[0]]
        return epi

    def ffn_fwd(tag, h, xres, gate, wg, wu, wd):
        rows = h.shape[0]
        a_, b_, s_ = _mm(tag + "_up", [(h, wg, D), (h, wu, D)], rows, F, tm=_div(rows, 256), tn=F4, epi=swiglu_epi,
                         outs=[(F32, False), (F32, False), (BF16, False)])
        f_, xo = _mm(tag + "_down", [(s_, wd, F)], rows, D, tm=_div(rows, 512), tn=_div(D, 1024), nk=N_CHIPS,
                     epi=res_epi(0.5), outs=[(F32, False), (F32, False)], rows=[(xres, 0, 0)], vecs=[gate])
        return a_, b_, s_, f_, xo

    xc = jnp.concatenate([ctx[0], x[0]], axis=0)
    h1 = norm_mod("norm1", xc, ng[0:1], sh1, sc1)
    a1, b1, s1, f1, x1 = ffn_fwd("ffn1", h1, xc, g1, Wt['w_ffn1_gate'], Wt['w_ffn1_up'], Wt['w_ffn1_down'])
    weights_ready(1, x1)
    h2 = norm_mod("norm2", x1, ng[1:2], sh2, sc2)
    proj = _mm("in_proj", [(h2, Wt['w_in'], D)], T, 4 * D, tm=_div(T, 768), tn=_div(D, 1024), epi=ident,
               outs=[(F32, False)])[0]
    nh, nkvh = D // HEAD_DIM, KV // HEAD_DIM

    def prep_fn(i, kt, vt, ut, qt, qgt, kgt, ct, st):
        qs = [_head_norm(qt[:, h * HEAD_DIM:(h + 1) * HEAD_DIM], qgt) for h in range(nh)]
        ks = [_head_norm(kt[:, h * HEAD_DIM:(h + 1) * HEAD_DIM], kgt) for h in range(nkvh)]
        qs = [v * ct + _rot(v) * st for v in qs]
        ks = [v * ct + _rot(v) * st for v in ks]
        return [jnp.concatenate(qs, axis=1), jnp.concatenate(ks, axis=1), vt, ut]

    qr, kr, vb, ub = _rowk(
        "qk_prep", prep_fn, T, tr,
        [(proj, ('col', KV, 0)), (proj, ('col', KV, 1)), (proj, ('col', W, 1)), (proj, ('col', D, 1)),
         (qg, 'vec'), (kg, 'vec'), (cos_t, 'row'), (sin_t, 'row')],
        [((T, D), BF16, 'row'), ((T, KV), BF16, 'row'), ((T, KV), BF16, 'row'), ((T, W), BF16, 'row')])
    attn = _attn_fwd(qr, kr, vb, L, Lc, D)
    hs_re, hs_im, ys = [], [], []
    for d in range(2):
        hr_, hi_, y_ = _ssm_fwd("ssm_fwd%d" % d, ub, bbd[d], cbd_re[d], cbd_im[d], lam_re[d], lam_im[d],
                                coef_re[d], coef_im[d], Lc, reverse=bool(d))
        hs_re.append(hr_)
        hs_im.append(hi_)
        ys.append(y_)

    def ssm_out_fn(i, y0, y1, ut, dt):
        pre = dt * ut + y0 + y1
        yg_ = _gelu(pre)
        return [pre, yg_, yg_]

    ssm_pre, yg, ygb = _rowk(
        "ssm_out", ssm_out_fn, L, tr,
        [(ys[0], 'orow'), (ys[1], 'orow'), (proj, ('ocol', W, 1)), (ssm_d, 'vec')],
        [((L, W), F32, 'row'), ((L, W), F32, 'row'), ((L, W), BF16, 'row')], nc=ncr)

    def glu_epi(accs, rows, vecs, ri):
        z_ = accs[0] + vecs[0]
        return [z_, rows[0] * _sigmoid(z_)]

    zglu, y2 = _mm("glu", [(ygb, Wt['w_glu'], W)], L, W, tm=_div(L, 512), tn=_div(W, 512), epi=glu_epi,
                   outs=[(F32, False), (BF16, False)], rows=[(yg, 0, 0)], vecs=[b_glu])
    tnm = _div(Dq, 512)

    def merge_epi(accs, rows, vecs, ri):
        ga, gs = _sigmoid(rows[0]), _sigmoid(rows[1])
        return [accs[0], accs[1], ga * accs[0] + gs * accs[1]]

    ba, bs, merged = _mm("merge", [(attn, Wt['w_br_attn'], D), (y2, Wt['w_br_ssm'], W)], L, D, tm=tr, tn=tnm,
                         epi=merge_epi, outs=[(F32, False), (F32, False), (BF16, False)],
                         rows=[(proj, ncr, 2 * D // tnm), (proj, ncr, 3 * D // tnm)])
    mix, x2 = _mm("out_proj", [(merged, Wt['w_out'], D)], L, D, tm=tr, tn=_div(D, 1024), epi=res_epi(1.0),
                  outs=[(F32, False), (F32, False)], rows=[(x1, ncr, 0)], vecs=[g2])
    weights_ready(2, x2)
    h3 = norm_mod("norm3", x2, ng[2:3], sh3, sc3)
    a3, b3, s3, f3, x3 = ffn_fwd("ffn2", h3, x2, g3, Wt['w_ffn2_gate'], Wt['w_ffn2_up'], Wt['w_ffn2_down'])

    def loss_fn(i, yt, tt_):
        diff = yt - tt_
        return [diff * (1.0 / D), jnp.sum(diff * diff, axis=0, keepdims=True)]

    dy, sq = _rowk("loss", loss_fn, L, tr, [(x3, 'row'), (loss_target[0], 'row')],
                   [((L, D), F32, 'row'), ((1, D), F32, 'acc')])
    loss = lax.psum(0.5 * jnp.sum(sq) / D, ("x", "y", "c"))

    def res_bwd(name, dxo, f_, gate, coef):
        rows, nrow = dxo.shape[0], gate.shape[0]

        def fn(i, dt, ft, gt):
            return [(coef * sel(i, gt)) * dt, put(i, jnp.sum(dt * ft, axis=0, keepdims=True) * coef, nrow)]

        return _rowk(name, fn, rows, tr, [(dxo, 'row'), (f_, 'row'), (gate, 'vec')],
                     [((rows, D), BF16, 'row'), ((nrow, D), F32, 'acc')])

    def swiglu_bwd_epi(accs, rows, vecs, ri):
        ds_, a_, b_ = accs[0], rows[0], rows[1]
        sg = _sigmoid(a_)
        return [ds_ * b_ * (sg * (1.0 + a_ * (1.0 - sg))), ds_ * (a_ * sg)]

    sum2 = lambda accs, rows, vecs, ri: [accs[0] + accs[1]]

    def norm_mod_bwd(name, xv, g, sh, sc, dh, dres, dres_kind):
        rows, nrow = xv.shape[0], sh.shape[0]

        def fn(i, xt, gt, sht, sct, dht, rest):
            _, vjp = jax.vjp(_norm_mod, xt, gt, sel(i, sht), sel(i, sct))
            dx_, dg_, dsh_, dsc_ = vjp(dht)
            dx_ = dx_ + (jnp.where(i >= ncr, rest, 0.0) if dres_kind == 'xrow' else rest)
            return [dx_, dg_, put(i, dsh_, nrow), put(i, dsc_, nrow)]

        return _rowk(name, fn, rows, tr,
                     [(xv, 'row'), (g, 'vec'), (sh, 'vec'), (sc, 'vec'), (dh, 'row'), (dres, dres_kind)],
                     [((rows, D), F32, 'row'), ((1, D), F32, 'acc'), ((nrow, D), F32, 'acc'), ((nrow, D), F32, 'acc')],
                     nc=ncr)

    def ffn_bwd(tag, dxo, h, a_, b_, s_, f_, gate, wg, wu, wd):
        rows = dxo.shape[0]
        df, dgate = res_bwd(tag + "_dres", dxo, f_, gate, 0.5)
        da, db = _mm(tag + "_dact", [(df, wd, D)], rows, F, tm=_div(rows, 384), tn=F4, tb=True, epi=swiglu_bwd_epi,
                     outs=[(BF16, False), (BF16, False)], rows=[(a_, 0, 0), (b_, 0, 0)])
        dwd = _mm(tag + "_dwd", [(s_, df, rows)], F, D, tm=_div(F, 512), tn=_div(D, 1024), ta=True, epi=ident,
                  outs=[(BF16, False)])[0].reshape(N_CHIPS, F4, D)
        dwg = _mm(tag + "_dwg", [(h, da, rows)], D, F, tm=_div(D, 512), tn=F4, ta=True, epi=ident,
                  outs=[(BF16, True)])[0]
        dwu = _mm(tag + "_dwu", [(h, db, rows)], D, F, tm=_div(D, 512), tn=F4, ta=True, epi=ident,
                  outs=[(BF16, True)])[0]
        dh = _mm(tag + "_dh", [(da, wg, F), (db, wu, F)], rows, D, tm=_div(rows, 768), tn=_div(D, 1024), nk=N_CHIPS,
                 tb=True, epi=sum2, outs=[(F32, False)])[0]
        return dh, dgate, dwg, dwu, dwd

    dh3, dg3, dwg2, dwu2, dwd2 = ffn_bwd("ffn2", dy, h3, a3, b3, s3, f3, g3, Wt['w_ffn2_gate'], Wt['w_ffn2_up'],
                                         Wt['w_ffn2_down'])
    tok_r1, scatter_fin1 = _scatter_split("scatter_ffn2", [dwg2, dwu2, dwd2], dg3)
    dx2, dng3, dsh3, dsc3 = norm_mod_bwd("norm3_bwd", x2, ng[2:3], sh3, sc3, dh3, dy, 'row')
    dmix, dg2 = res_bwd("mix_dres", dx2, mix, g2 + tok_r1[0:1, 0:1], 1.0)

    def dmerge_epi(accs, rows, vecs, ri):
        dm_, ba_, bs_ = accs[0], rows[0], rows[1]
        ga, gs = _sigmoid(rows[2]), _sigmoid(rows[3])
        return [dm_ * ga, dm_ * gs, dm_ * ba_ * ga * (1.0 - ga), dm_ * bs_ * gs * (1.0 - gs)]

    dba, dbs, dga, dgs = _mm("dmerge", [(dmix, Wt['w_out'], D)], L, D, tm=tr, tn=tnm, tb=True, epi=dmerge_epi,
                             outs=[(BF16, False)] * 4,
                             rows=[(ba, 0, 0), (bs, 0, 0), (proj, ncr, 2 * D // tnm), (proj, ncr, 3 * D // tnm)])
    dwout = _mm("dw_out", [(merged, dmix, L)], D, D, tm=_div(D, 512), tn=_div(D, 1024), ta=True, epi=ident,
                outs=[(BF16, False)])[0].reshape(N_CHIPS, Dq, D)
    dattn = _mm("dattn", [(dba, Wt['w_br_attn'], D)], L, D, tm=_div(L, 512), tn=_div(D, 1024), tb=True, epi=ident,
                outs=[(BF16, False)])[0]
    dwba = _mm("dw_br_attn", [(attn, dba, L)], D, D, tm=_div(D, 512), tn=_div(D, 1024), ta=True, epi=ident,
               outs=[(BF16, False)])[0].reshape(N_CHIPS, Dq, D)
    dy2 = _mm("dy2", [(dbs, Wt['w_br_ssm'], D)], L, W, tm=_div(L, 512), tn=_div(W, 1024), nk=N_CHIPS, tb=True,
              epi=ident, outs=[(F32, False)])[0]
    dwbs = _mm("dw_br_ssm", [(y2, dbs, L)], W, D, tm=_div(W, 512), tn=_div(Dq, 512), ta=True, epi=ident,
               outs=[(BF16, True)])[0]

    def glu_bwd_fn(i, d2, ygt, zt):
        sz = _sigmoid(zt)
        dz_ = d2 * ygt * sz * (1.0 - sz)
        return [dz_, d2 * sz, jnp.sum(dz_, axis=0, keepdims=True)]

    dz, dyd, dbglu = _rowk("glu_bwd", glu_bwd_fn, L, tr, [(dy2, 'row'), (yg, 'row'), (zglu, 'row')],
                           [((L, W), BF16, 'row'), ((L, W), F32, 'row'), ((1, W), F32, 'acc')])

    def dssm_epi(accs, rows, vecs, ri):
        _, vjp = jax.vjp(_gelu, rows[1])
        ds_ = vjp(accs[0] + rows[0])[0]
        return [ds_, ds_]

    dssm, dssm_b = _mm("dssm", [(dz, Wt['w_glu'], W)], L, W, tm=_div(L, 512), tn=_div(W, 512), tb=True, epi=dssm_epi,
                       outs=[(F32, False), (BF16, False)], rows=[(dyd, 0, 0), (ssm_pre, 0, 0)])
    dwglu = _mm("dw_glu", [(ygb, dz, L)], W, W, tm=_div(W, 512), tn=_div(W, 1024), ta=True, epi=ident,
                outs=[(BF16, False)])[0].reshape(N_CHIPS, W // N_CHIPS, W)
    dssm_full = jnp.concatenate([jnp.zeros((Lc, W), BF16), dssm_b], axis=0)
    dus, dlam_re, dlam_im, dcoef_re, dcoef_im, dbbd, dcbd_re, dcbd_im = [], [], [], [], [], [], [], []
    for d in range(2):
        r = _ssm_bwd("ssm_bwd%d" % d, dssm_full, hs_re[d], hs_im[d], ub, bbd[d], bbdt_re[d], bbdt_im[d],
                     cbdt_re[d], cbdt_im[d], lam_re[d], lam_im[d], coef_re[d], coef_im[d], Lc, reverse=bool(d))
        for lst, val in zip((dus, dlam_re, dlam_im, dcoef_re, dcoef_im, dbbd, dcbd_re, dcbd_im), r):
            lst.append(val)
    dqr, dkr, dvf = _attn_bwd(qr, kr, vb, dattn, L, Lc, D)

    def prep_bwd_fn(i, qt, kt, ut, dqt, dkt, dvt, du0, du1, dst, dt, qgt, kgt, ct, st):
        live = i >= ncr
        dqt = jnp.where(live, dqt, 0.0)
        dst = jnp.where(live, dst, 0.0)
        dqs, dks = [], []
        dqg_ = jnp.zeros((1, HEAD_DIM), F32)
        dkg_ = jnp.zeros((1, HEAD_DIM), F32)
        for h in range(nh):
            hl = slice(h * HEAD_DIM, (h + 1) * HEAD_DIM)
            dn = dqt[:, hl] * ct + _rot(dqt[:, hl] * st)
            _, vjp = jax.vjp(_head_norm, qt[:, hl], qgt)
            dxh, dgh = vjp(dn)
            dqs.append(dxh)
            dqg_ = dqg_ + dgh
        for h in range(nkvh):
            hl = slice(h * HEAD_DIM, (h + 1) * HEAD_DIM)
            dn = dkt[:, hl] * ct + _rot(dkt[:, hl] * st)
            _, vjp = jax.vjp(_head_norm, kt[:, hl], kgt)
            dxh, dgh = vjp(dn)
            dks.append(dxh)
            dkg_ = dkg_ + dgh
        du_ = du0 + du1 + dst * dt
        return [jnp.concatenate(dqs, axis=1), jnp.concatenate(dks, axis=1), dvt, du_, dqg_, dkg_,
                jnp.sum(dst * ut, axis=0, keepdims=True)]

    dq_b, dk_b, dv_b, du_b, dqg, dkg, dssd = _rowk(
        "qk_prep_bwd", prep_bwd_fn, T, tr,
        [(proj, ('col', D, 1)), (proj, ('col', KV, 0)), (proj, ('col', W, 1)), (dqr, 'xrow'), (dkr, 'row'),
         (dvf, 'row'), (dus[0], 'row'), (dus[1], 'row'), (dssm, 'xrow'), (ssm_d, 'vec'), (qg, 'vec'), (kg, 'vec'),
         (cos_t, 'row'), (sin_t, 'row')],
        [((T, D), BF16, 'row'), ((T, KV), BF16, 'row'), ((T, KV), BF16, 'row'), ((T, W), BF16, 'row'),
         ((1, HEAD_DIM), F32, 'acc'), ((1, HEAD_DIM), F32, 'acc'), ((1, W), F32, 'acc')], nc=ncr)
    dgate = jnp.concatenate([jnp.zeros((Lc, 2 * D), BF16), jnp.concatenate([dga, dgs], axis=1)], axis=0)
    dproj = jnp.concatenate([dk_b, dv_b, du_b, dq_b, dgate], axis=1)
    dh2 = _mm("in_proj_dx", [(dproj, Wt['w_in'], 4 * D)], T, D, tm=_div(T, 768), tn=_div(D, 1024), nk=N_CHIPS, tb=True,
              epi=ident, outs=[(F32, False)])[0]
    dwin = _mm("in_proj_dw", [(h2, dproj, T)], D, 4 * D, tm=_div(D, 512), tn=_div(D, 1024), ta=True, epi=ident,
               outs=[(BF16, True)])[0]
    tok_r2, scatter_fin2 = _scatter_split("scatter_mix", [dwin, dwglu, dwba, dwbs, dwout], dqg)
    dx1, dng2, dsh2, dsc2 = norm_mod_bwd("norm2_bwd", x1, ng[1:2] + tok_r2[0:1, 0:1], sh2, sc2, dh2, dx2, 'xrow')
    dh1, dg1, dwg1, dwu1, dwd1 = ffn_bwd("ffn1", dx1, h1, a1, b1, s1, f1, g1, Wt['w_ffn1_gate'], Wt['w_ffn1_up'],
                                         Wt['w_ffn1_down'])
    dx0, dng1, dsh1, dsc1 = norm_mod_bwd("norm1_bwd", xc, ng[0:1], sh1, sc1, dh1, dx1, 'row')
    grad_x = dx0[Lc:][None]

    zD = jnp.zeros((1, D), F32)
    dmod_x = jnp.concatenate([dsh1[1:2], dsc1[1:2], dg1[1:2], dsh2[1:2], dsc2[1:2], dg2, dsh3, dsc3, dg3], axis=1)
    dmod_c = jnp.concatenate([dsh1[0:1], dsc1[0:1], dg1[0:1], dsh2[0:1], dsc2[0:1], zD, zD, zD, zD], axis=1)
    db_parts, dc_parts = [], []
    for d in range(2):
        db_parts.append(jnp.transpose(_bd_extract(dbbd[d][:, :, :SLAB_ST], E, P).reshape(G, E, P), (0, 2, 1)))
        db_parts.append(jnp.transpose(_bd_extract(dbbd[d][:, :, SLAB_ST:], E, P).reshape(G, E, P), (0, 2, 1)))
        dc_parts.append(jnp.transpose(_bd_extract(dcbd_re[d], P, E).reshape(G, P, E), (0, 2, 1)))
        dc_parts.append(jnp.transpose(_bd_extract(dcbd_im[d], P, E).reshape(G, P, E), (0, 2, 1)))
    pieces = [dmod_x, dmod_c, dng1, dng2, dng3, dqg, dkg] + dlam_re + dlam_im + dcoef_re + dcoef_im \
        + db_parts + dc_parts + [dssd, dbglu]
    shapes = [p_.shape for p_ in pieces]
    pack = _pack(pieces)
    RP = pack.shape[0]
    allp = _allgather_small("gather_small", pack).reshape(N_DEV, RP, PACK_W)

    def sum_dev_fn(i, t):
        s_ = t[0]
        for k in range(1, N_DEV):
            s_ = s_ + t[k]
        return [s_]

    tot = _rowk("sum_small", sum_dev_fn, RP, 8, [(allp, 'row3')], [((RP, PACK_W), F32, 'row')])[0]
    (t_dmod_x, t_dmod_c, t_ng1, t_ng2, t_ng3, t_qg, t_kg, t_lr0, t_lr1, t_li0, t_li1, t_kr0, t_kr1, t_ki0, t_ki1,
     t_bre0, t_bim0, t_bre1, t_bim1, t_cre0, t_cim0, t_cre1, t_cim1, t_d, t_bglu) = _unpack(tot, shapes)
    cat2 = lambda u0, u1: jnp.concatenate([u0.reshape(G, P), u1.reshape(G, P)], axis=0)
    g_are, g_aim, g_ldt = _zoh_bwd(a_re2, a_im2, ldt2, [cat2(t_lr0, t_lr1), cat2(t_li0, t_li1),
                                                         cat2(t_kr0, t_kr1), cat2(t_ki0, t_ki1)])
    g_bmod = _rowk("bmod_grad", lambda i, u0, u1: [u0 + u1], 1, 1, [(t_dmod_x, 'row'), (t_dmod_c, 'row')],
                   [((1, 9 * D), F32, 'row')])[0]
    mod_rows = -(-9 * D // PACK_W)
    dmx_all = allp[:, :mod_rows].reshape(N_DEV, mod_rows * PACK_W)[:, :9 * D]
    cots = jnp.concatenate([dmx_all, t_dmod_c, jnp.zeros((7, 9 * D), F32)], axis=0)
    cots_sh = lax.dynamic_slice(cots, (0, chip * NM), (16, NM))
    g_wmod = _outer_sum(acts, cots_sh)
    part = _mm("cctx_part", [(cots_sh[8:16], wm, NM)], 8, D, tm=8, tn=_div(D, 1024), nk=NM // _div(NM, 1152), tb=True,
               epi=ident, outs=[(F32, False)], a_pro=to_bf, b_pro=to_bf)[0]
    parts = _allgather_small("gather_cctx", part).reshape(N_CHIPS, 2, 8, D)[:, 0, 0]

    def cctx_fn(i, pt, ct):
        ds_ = ((pt[0:1] + pt[1:2]) + pt[2:3]) + pt[3:4]
        _, vjp = jax.vjp(lambda v: v * _sigmoid(v), ct)
        return [vjp(ds_)[0]]

    g_cctx = _rowk("cctx_grad", cctx_fn, 1, 1, [(parts, 'vec'), (c_ctx[None], 'row')], [((1, D), F32, 'row')])[0]

    tok_r3, scatter_fin3 = _scatter_split("scatter_ffn1", [dwg1, dwu1, dwd1], g_cctx)
    results = {}
    results['w_mod'] = _adamw("adamw_w_mod", w_mod, m_w_mod, v_w_mod, [g_wmod])

    def sum_chip_fn(i, own, t):
        return [((own.astype(F32) + t[0].astype(F32)) + t[1].astype(F32)) + t[2].astype(F32)]

    def reduce_group(tag, names, grads, fin, after_work):
        landed = fin(after_work)
        plane = []
        for n, g_, rb in zip(names, grads, landed):
            R_, C_ = rb.shape[1], rb.shape[2]
            own = lax.dynamic_index_in_dim(g_, chip, 0, keepdims=False)
            plane.append(_rowk("sum_" + n, sum_chip_fn, R_, _div(R_, max(16, 262144 // C_), mult=16),
                               [(own, 'row'), (rb, 'row3')], [((R_, C_), F32, 'row')])[0])
        other = _swap_sibling("swap_" + tag, plane)
        for n, mine, theirs in zip(names, plane, other):
            results[n] = _adamw("adamw_" + n, A[n], A['m_' + n], A['v_' + n], [mine, theirs])

    reduce_group("ffn2", big[3:6], [dwg2, dwu2, dwd2], scatter_fin1, tok_r3)
    reduce_group("mix", big[6:11], [dwin, dwglu, dwba, dwbs, dwout], scatter_fin2, results['w_ffn2_down'][0])
    reduce_group("ffn1", big[0:3], [dwg1, dwu1, dwd1], scatter_fin3, results['w_out'][0])

    small = ['c_ctx', 'b_mod', 'norm_g', 'q_norm_g', 'k_norm_g', 'ssm_a_re', 'ssm_a_im', 'ssm_log_dt', 'ssm_b_re',
             'ssm_b_im', 'ssm_c_re', 'ssm_c_im', 'ssm_d', 'b_glu']
    ng_full = jnp.concatenate([t_ng1, t_ng2, t_ng3], axis=0)
    gsmall = {
        'c_ctx': g_cctx, 'b_mod': g_bmod, 'norm_g': lax.dynamic_slice(ng_full, (0, chip * Dq), (3, Dq)),
        'q_norm_g': t_qg, 'k_norm_g': t_kg, 'ssm_a_re': g_are, 'ssm_a_im': g_aim, 'ssm_log_dt': g_ldt,
        'ssm_b_re': jnp.stack([t_bre0, t_bre1]), 'ssm_b_im': jnp.stack([t_bim0, t_bim1]),
        'ssm_c_re': jnp.stack([t_cre0, t_cre1]), 'ssm_c_im': jnp.stack([t_cim0, t_cim1]),
        'ssm_d': t_d, 'b_glu': t_bglu}
    sshapes = [A[n].shape for n in small]
    packs = [_pack([A[pre + n] for n in small]) for pre in ('', 'm_', 'v_')] + [_pack([gsmall[n] for n in small])]
    sres = _adamw("adamw_small", packs[0], packs[1], packs[2], [packs[3]])
    sres = [_unpack(b_, sshapes) for b_ in sres]
    for k, n in enumerate(small):
        results[n] = tuple(sres[q][k] for q in range(4))

    order = ['c_ctx', 'w_mod', 'b_mod', 'norm_g', 'w_ffn1_gate', 'w_ffn1_up', 'w_ffn1_down', 'w_in', 'q_norm_g',
             'k_norm_g', 'ssm_a_re', 'ssm_a_im', 'ssm_log_dt', 'ssm_b_re', 'ssm_b_im', 'ssm_c_re', 'ssm_c_im',
             'ssm_d', 'w_glu', 'b_glu', 'w_br_attn', 'w_br_ssm', 'w_out', 'w_ffn2_gate', 'w_ffn2_up', 'w_ffn2_down']
    outs = [loss, grad_x]
    for q in range(4):
        outs += [results[n][q].reshape(A[n].shape) for n in order]
    return tuple(outs)
```

```python
import math

import jax
import jax.numpy as jnp
from jax import lax
from jax.experimental import pallas as pl
from jax.experimental.pallas import tpu as pltpu

F32 = jnp.float32
BF16 = jnp.bfloat16
MESH = pl.DeviceIdType.MESH

NORM_EPS = 1e-6
ROPE_THETA = 10000.0
GRID_W = 64
HEAD_DIM = 128
Q_PER_KV = 4
SSM_GROUP = 16
SSM_STATE = 64
ADAM_LR = 0.001
ADAM_B1 = 0.9
ADAM_B2 = 0.999
ADAM_EPS = 1e-08
ADAM_WD = 0.01
ADAM_STEP = 10

N_CHIPS = 4
N_DEV = 8
LANES = 128
SLAB_CH = 128
SLAB_GROUPS = SLAB_CH // SSM_GROUP
SLAB_ST = SLAB_GROUPS * SSM_STATE
VMEM_LIMIT_BYTES = 56 * 1024 * 1024
PACK_W = 1024


def _cparams(**kw):
    return pltpu.CompilerParams(vmem_limit_bytes=VMEM_LIMIT_BYTES, **kw)


def _div(n, pref, mult=LANES):
    t = (min(pref, n) // mult) * mult
    while t >= mult:
        if n % t == 0:
            return t
        t -= mult
    return n


def _sigmoid(x):
    return jax.nn.sigmoid(x)


def _gelu(x):
    return x * (0.5 * (1.0 + jnp.tanh(math.sqrt(2.0 / math.pi) * (x + 0.044715 * (x * x * x)))))


def _rowk(name, fn, nrows, tr, ins, outs, nc=0):
    nt = nrows // tr
    in_specs, arrays = [], []
    for arr, kind in ins:
        arrays.append(arr)
        if kind == 'row':
            in_specs.append(pl.BlockSpec((tr, arr.shape[1]), lambda i: (i, 0)))
        elif kind == 'xrow':
            in_specs.append(pl.BlockSpec((tr, arr.shape[1]), lambda i: (jnp.maximum(i - nc, 0), 0)))
        elif kind == 'orow':
            in_specs.append(pl.BlockSpec((tr, arr.shape[1]), lambda i: (i + nc, 0)))
        elif kind == 'vec':
            in_specs.append(pl.BlockSpec(arr.shape, lambda i, nd=arr.ndim: (0,) * nd))
        elif kind == 'row3':
            in_specs.append(pl.BlockSpec((arr.shape[0], tr, arr.shape[2]), lambda i: (0, i, 0)))
        elif kind == 'row1':
            in_specs.append(pl.BlockSpec((None, tr, arr.shape[2]), lambda i: (0, i, 0)))
        elif kind[0] == 'ocol':
            _, width, blk = kind
            in_specs.append(pl.BlockSpec((tr, width), lambda i, blk=blk: (i + nc, blk)))
        else:
            _, width, blk = kind
            in_specs.append(pl.BlockSpec((tr, width), lambda i, blk=blk: (i, blk)))
    out_shape, out_specs = [], []
    for shape, dtype, kind in outs:
        out_shape.append(jax.ShapeDtypeStruct(shape, dtype))
        if kind == 'row':
            out_specs.append(pl.BlockSpec((tr, shape[1]), lambda i: (i, 0)))
        elif kind == 'row1':
            out_specs.append(pl.BlockSpec((None, tr, shape[2]), lambda i: (0, i, 0)))
        else:
            out_specs.append(pl.BlockSpec(shape, lambda i, nd=len(shape): (0,) * nd))
    nin = len(ins)

    def body(*refs):
        i = pl.program_id(0)
        res = fn(i, *[r[...] for r in refs[:nin]])
        for (shape, dtype, kind), ref, val in zip(outs, refs[nin:], res):
            if kind in ('row', 'row1'):
                ref[...] = val.astype(dtype)
            else:
                @pl.when(i == 0)
                def _():
                    ref[...] = val.astype(dtype)

                @pl.when(i > 0)
                def _():
                    ref[...] += val.astype(dtype)

    return pl.pallas_call(body, name=name, grid=(nt,), in_specs=in_specs, out_specs=out_specs,
                          out_shape=out_shape, compiler_params=_cparams())(*arrays)


def _mm(name, pairs, M, N, *, tm, tn, nk=1, epi, outs, ta=False, tb=False, rows=(), vecs=(),
        a_pro=None, b_pro=None, n_outer=True):
    nm, nn = M // tm, N // tn
    npair = len(pairs)

    def idx(f):
        if n_outer:
            return lambda j, i, k: f(i, j, k)
        return lambda i, j, k: f(i, j, k)

    in_specs, args = [], []
    for a, b, K in pairs:
        tk = K // nk
        if ta:
            in_specs.append(pl.BlockSpec((tk, tm), idx(lambda i, j, k: (k, i))))
        else:
            in_specs.append(pl.BlockSpec((tm, tk), idx(lambda i, j, k: (i, k))))
        args.append(a)
        if b.ndim == 3:
            if tb:
                per = b.shape[2] // tk
                in_specs.append(pl.BlockSpec((None, tn, tk), idx(lambda i, j, k, per=per: (k // per, j, k % per))))
            else:
                per = b.shape[2] // tn
                in_specs.append(pl.BlockSpec((None, tk, tn), idx(lambda i, j, k, per=per: (j // per, k, j % per))))
        elif tb:
            in_specs.append(pl.BlockSpec((tn, tk), idx(lambda i, j, k: (j, k))))
        else:
            in_specs.append(pl.BlockSpec((tk, tn), idx(lambda i, j, k: (k, j))))
        args.append(b)
    for arr, ro, co in rows:
        in_specs.append(pl.BlockSpec((tm, tn), idx(lambda i, j, k, ro=ro, co=co: (i + ro, j + co))))
        args.append(arr)
    for arr in vecs:
        in_specs.append(pl.BlockSpec((arr.shape[0], tn), idx(lambda i, j, k: (0, j))))
        args.append(arr)
    out_shape, out_specs = [], []
    for dtype, chunked in outs:
        if chunked:
            per = (N // N_CHIPS) // tn
            out_shape.append(jax.ShapeDtypeStruct((N_CHIPS, M, N // N_CHIPS), dtype))
            out_specs.append(pl.BlockSpec((None, tm, tn), idx(lambda i, j, k, per=per: (j // per, i, j % per))))
        else:
            out_shape.append(jax.ShapeDtypeStruct((M, N), dtype))
            out_specs.append(pl.BlockSpec((tm, tn), idx(lambda i, j, k: (i, j))))
    scratch = [pltpu.VMEM((tm, tn), F32) for _ in range(npair)] if nk > 1 else []
    nrow, nvec, nout = len(rows), len(vecs), len(outs)
    dims = (((0 if ta else 1,), (1 if tb else 0,)), ((), ()))

    def body(*refs):
        ab = refs[:2 * npair]
        row_refs = refs[2 * npair:2 * npair + nrow]
        vec_refs = refs[2 * npair + nrow:2 * npair + nrow + nvec]
        out_refs = refs[2 * npair + nrow + nvec:2 * npair + nrow + nvec + nout]
        acc_refs = refs[2 * npair + nrow + nvec + nout:]
        if n_outer:
            j, i, k = pl.program_id(0), pl.program_id(1), pl.program_id(2)
        else:
            i, j, k = pl.program_id(0), pl.program_id(1), pl.program_id(2)

        def part(p):
            av, bv = ab[2 * p][...], ab[2 * p + 1][...]
            if a_pro is not None:
                av = a_pro(av)
            if b_pro is not None:
                bv = b_pro(bv)
            return lax.dot_general(av, bv, dims, preferred_element_type=F32)

        def finish(accs):
            row_index = i * tm + lax.broadcasted_iota(jnp.int32, (tm, 1), 0)
            res = epi(accs, [r[...] for r in row_refs], [v[...] for v in vec_refs], row_index)
            for ref, val in zip(out_refs, res):
                ref[...] = val.astype(ref.dtype)

        if nk == 1:
            finish([part(p) for p in range(npair)])
        else:
            parts = [part(p) for p in range(npair)]

            @pl.when(k == 0)
            def _():
                for p in range(npair):
                    acc_refs[p][...] = parts[p]

            @pl.when(k > 0)
            def _():
                for p in range(npair):
                    acc_refs[p][...] += parts[p]

            @pl.when(k == nk - 1)
            def _():
                finish([acc_refs[p][...] for p in range(npair)])

    grid = (nn, nm, nk) if n_outer else (nm, nn, nk)
    return pl.pallas_call(body, name=name, grid=grid, in_specs=in_specs, out_specs=out_specs,
                          out_shape=out_shape, scratch_shapes=scratch, compiler_params=_cparams())(*args)


def _split3(v):
    v0 = v.astype(BF16)
    r1 = v - v0.astype(F32)
    v1 = r1.astype(BF16)
    v2 = (r1 - v1.astype(F32)).astype(BF16)
    return v0, v1, v2


def _mesh_pos():
    return lax.axis_index("x"), lax.axis_index("y"), lax.axis_index("c")


def _allgather_small(name, x):
    m, n = x.shape

    def body(x_ref, out_ref, send_sems, recv_sems, local_sem):
        xi, yi, ci = _mesh_pos()
        me, sibling = (xi, yi, ci), (xi, yi, 1 - ci)
        chips = [(1 - xi, yi), (xi, 1 - yi), (1 - xi, 1 - yi)]

        def rows(px, py, pc):
            return out_ref.at[pl.ds((4 * px + 2 * py + pc) * m, m), :]

        def copy(k, block, to, src=None):
            return pltpu.make_async_remote_copy(
                src_ref=rows(*block) if src is None else src, dst_ref=rows(*block),
                send_sem=send_sems.at[k], recv_sem=recv_sems.at[k], device_id=to, device_id_type=MESH)

        mine = pltpu.make_async_copy(x_ref, rows(*me), local_sem)
        mine.start()
        first = [copy(0, me, sibling, src=x_ref)]
        first += [copy(1 + j, me, (*chip, ci), src=x_ref) for j, chip in enumerate(chips)]
        for cp in first:
            cp.start()
        passed = [copy(4 + j, (*chip, ci), sibling) for j, chip in enumerate(chips)]
        for j, chip in enumerate(chips):
            copy(1 + j, (*chip, ci), me).wait_recv()
            passed[j].start()
        copy(0, sibling, me).wait_recv()
        for j, chip in enumerate(chips):
            copy(4 + j, (*chip, 1 - ci), me).wait_recv()
        for cp in first + passed:
            cp.wait_send()
        mine.wait()

    return pl.pallas_call(
        body, name=name, out_shape=jax.ShapeDtypeStruct((N_DEV * m, n), x.dtype),
        in_specs=[pl.BlockSpec(memory_space=pltpu.VMEM)], out_specs=pl.BlockSpec(memory_space=pltpu.VMEM),
        scratch_shapes=[pltpu.SemaphoreType.DMA((7,)), pltpu.SemaphoreType.DMA((7,)), pltpu.SemaphoreType.DMA],
        compiler_params=_cparams())(x)


_HBM = pl.BlockSpec(memory_space=pltpu.HBM)
_SEM = pl.BlockSpec(memory_space=pltpu.SEMAPHORE)
_ANY = pl.BlockSpec(memory_space=pl.ANY)
_EFFECT = pltpu.SideEffectType.DATAFLOW_SIDE_EFFECTING


def _in_hbm(v):
    return pltpu.with_memory_space_constraint(v, pltpu.HBM)


def _other_chips(xi, yi):
    return [(1 - xi, yi), (xi, 1 - yi), (1 - xi, 1 - yi)]


def _guarded(north_only, fn):
    if north_only:
        pl.when(lax.axis_index("c") == 1)(fn)
    else:
        fn()


def _split_copies(name, srcs, lands, after, pairs, north_only, ncopy):
    ns, nl = len(srcs), len(lands)
    dma = pltpu.SemaphoreType.DMA((ncopy,))
    thru = [pltpu.HBM(v.shape, v.dtype) for v in list(srcs) + list(lands)]

    def start_body(*refs):
        src_refs, land_refs = refs[:ns], refs[ns:ns + nl]
        descs = pairs(src_refs, land_refs, refs[ns + nl + 1], refs[ns + nl + 2])

        def go():
            for send, _ in descs:
                send.start()

        _guarded(north_only, go)
        refs[-1][...] = jnp.zeros_like(refs[-1])

    res = pl.pallas_call(
        start_body, name=name + "_start",
        out_shape=(dma, dma, *thru, jax.ShapeDtypeStruct((8, LANES), F32)),
        in_specs=[_HBM] * (ns + nl) + [_ANY],
        out_specs=(_SEM, _SEM, *([_HBM] * (ns + nl)), pl.BlockSpec(memory_space=pltpu.VMEM)),
        input_output_aliases={k: 2 + k for k in range(ns + nl)},
        compiler_params=_cparams(has_side_effects=_EFFECT),
    )(*[_in_hbm(v) for v in srcs], *[_in_hbm(v) for v in lands], after)
    send_sems, recv_sems, token = res[0], res[1], res[-1]
    carried = res[2:2 + ns + nl]

    def finish(after_work):
        def wait_body(*refs):
            src_refs, land_refs = refs[:ns], refs[ns:ns + nl]
            descs = pairs(src_refs, land_refs, refs[ns + nl], refs[ns + nl + 1])

            def go():
                for send, recv in descs:
                    send.wait_send()
                    recv.wait_recv()

            _guarded(north_only, go)

        out = pl.pallas_call(
            wait_body, name=name + "_wait", out_shape=tuple(thru),
            in_specs=[_HBM] * (ns + nl) + [_SEM, _SEM, _ANY], out_specs=tuple([_HBM] * (ns + nl)),
            input_output_aliases={k: k for k in range(ns + nl)},
            compiler_params=_cparams(has_side_effects=_EFFECT),
        )(*carried, send_sems, recv_sems, after_work)
        return list(out[ns:])

    return token, finish


def _cast_slot(name, w, chip_index):
    R, C = w.shape[1:]
    tr = _div(R, max(16, 524288 // C), mult=16)

    def body(chip_ref, w_ref, o_ref):
        o_ref[...] = w_ref[...].astype(BF16)

    return pl.pallas_call(
        body, name=name, out_shape=jax.ShapeDtypeStruct((N_CHIPS, R, C), BF16),
        grid_spec=pltpu.PrefetchScalarGridSpec(
            num_scalar_prefetch=1, grid=(R // tr,),
            in_specs=[pl.BlockSpec((None, tr, C), lambda i, chip_ref: (0, i, 0))],
            out_specs=pl.BlockSpec((None, tr, C), lambda i, chip_ref: (chip_ref[0], i, 0))),
        compiler_params=_cparams())(chip_index, w)


def _gather_split(name, lands, after):
    def pairs(src_refs, land_refs, send_sems, recv_sems):
        xi, yi, _ = _mesh_pos()
        mine = 2 * xi + yi
        out = []
        for a in range(len(lands)):
            for j, (px, py) in enumerate(_other_chips(xi, yi)):
                def to_slot(slot, a=a, j=j, px=px, py=py):
                    return pltpu.make_async_remote_copy(
                        src_ref=land_refs[a].at[mine], dst_ref=land_refs[a].at[slot], send_sem=send_sems.at[3 * a + j],
                        recv_sem=recv_sems.at[3 * a + j], device_id=(px, py, 1), device_id_type=MESH)
                out.append((to_slot(mine), to_slot(2 * px + py)))
        return out

    return _split_copies(name, [], lands, after, pairs, north_only=True, ncopy=3 * len(lands))


def _scatter_split(name, grads, after):
    lands = [lax.empty((3,) + g.shape[1:], g.dtype) for g in grads]

    def pairs(src_refs, land_refs, send_sems, recv_sems):
        xi, yi, ci = _mesh_pos()
        out = []
        for a in range(len(grads)):
            for j, (px, py) in enumerate(_other_chips(xi, yi)):
                cp = pltpu.make_async_remote_copy(
                    src_ref=src_refs[a].at[2 * px + py], dst_ref=land_refs[a].at[j], send_sem=send_sems.at[3 * a + j],
                    recv_sem=recv_sems.at[3 * a + j], device_id=(px, py, ci), device_id_type=MESH)
                out.append((cp, cp))
        return out

    return _split_copies(name, grads, lands, after, pairs, north_only=False, ncopy=3 * len(grads))


def _gather_finish(name, lands):
    na = len(lands)

    def body(*refs):
        outs = refs[na:2 * na]
        send_sems, recv_sems = refs[2 * na:]
        xi, yi, ci = _mesh_pos()
        passes = [pltpu.make_async_remote_copy(
            src_ref=outs[a].at[2 * px + py], dst_ref=outs[a].at[2 * px + py],
            send_sem=send_sems.at[a, j], recv_sem=recv_sems.at[a, j], device_id=(xi, yi, 0), device_id_type=MESH)
            for a in range(na) for j, (px, py) in enumerate(_other_chips(xi, yi))]

        @pl.when(ci == 1)
        def _():
            for cp in passes:
                cp.start()
            for cp in passes:
                cp.wait_send()

        @pl.when(ci == 0)
        def _():
            for cp in passes:
                cp.wait_recv()

    return pl.pallas_call(
        body, name=name, out_shape=[jax.ShapeDtypeStruct(v.shape, v.dtype) for v in lands],
        in_specs=[_ANY] * na, out_specs=[_ANY] * na,
        input_output_aliases={a: a for a in range(na)},
        scratch_shapes=[pltpu.SemaphoreType.DMA((na, 3)), pltpu.SemaphoreType.DMA((na, 3))],
        compiler_params=_cparams())(*lands)


def _swap_sibling(name, arrs):
    na = len(arrs)

    def body(*refs):
        ins, outs = refs[:na], refs[na:2 * na]
        send_sems, recv_sems = refs[2 * na:]
        xi, yi, ci = _mesh_pos()
        copies = [pltpu.make_async_remote_copy(
            src_ref=ins[a], dst_ref=outs[a], send_sem=send_sems.at[a], recv_sem=recv_sems.at[a],
            device_id=(xi, yi, 1 - ci), device_id_type=MESH) for a in range(na)]
        for cp in copies:
            cp.start()
        for cp in copies:
            cp.wait()

    return pl.pallas_call(
        body, name=name,
        out_shape=[jax.ShapeDtypeStruct(g.shape, g.dtype) for g in arrs],
        in_specs=[_ANY] * na, out_specs=[_ANY] * na,
        scratch_shapes=[pltpu.SemaphoreType.DMA((na,)), pltpu.SemaphoreType.DMA((na,))],
        compiler_params=_cparams())(*arrs)


def _attn_tiles(L, Lc, D):
    tq = min(256, Lc)
    return tq, L // tq, Lc // tq, D // HEAD_DIM // Q_PER_KV


def _attn_probs(q, k):
    s = lax.dot_general(q, k, (((1,), (1,)), ((), ())), preferred_element_type=F32) * (HEAD_DIM ** -0.5)
    e = jnp.exp(s - jnp.max(s, axis=-1, keepdims=True))
    return e * (1.0 / jnp.sum(e, axis=-1, keepdims=True))


def _attn_fwd(qr, kr, v, L, Lc, D):
    T = L + Lc
    tq, nq, qoff, nkv = _attn_tiles(L, Lc, D)

    def body(q_ref, k_ref, v_ref, o_ref):
        p = _attn_probs(q_ref[...], k_ref[...])
        o_ref[...] = jnp.dot(p.astype(BF16), v_ref[...], preferred_element_type=F32).astype(o_ref.dtype)

    kv_spec = pl.BlockSpec((T, HEAD_DIM), lambda h, r, q: (0, h))
    return pl.pallas_call(
        body, name="attn_fwd", grid=(nkv, Q_PER_KV, nq),
        in_specs=[pl.BlockSpec((tq, HEAD_DIM), lambda h, r, q: (q + qoff, h * Q_PER_KV + r)), kv_spec, kv_spec],
        out_specs=pl.BlockSpec((tq, HEAD_DIM), lambda h, r, q: (q, h * Q_PER_KV + r)),
        out_shape=jax.ShapeDtypeStruct((L, D), BF16), compiler_params=_cparams())(qr, kr, v)


def _attn_bwd(qr, kr, v, do, L, Lc, D):
    T = L + Lc
    tq, nq, qoff, nkv = _attn_tiles(L, Lc, D)
    scale = HEAD_DIM ** -0.5

    def body(q_ref, k_ref, v_ref, do_ref, dq_ref, dk_ref, dv_ref):
        first = jnp.logical_and(pl.program_id(1) == 0, pl.program_id(2) == 0)
        q, k, dout = q_ref[...], k_ref[...], do_ref[...]
        p = _attn_probs(q, k)
        dp = lax.dot_general(dout, v_ref[...], (((1,), (1,)), ((), ())), preferred_element_type=F32)
        ds = (p * (dp - jnp.sum(p * dp, axis=-1, keepdims=True)) * scale).astype(BF16)
        dq_ref[...] = jnp.dot(ds, k, preferred_element_type=F32)
        dk = lax.dot_general(ds, q, (((0,), (0,)), ((), ())), preferred_element_type=F32)
        dv = lax.dot_general(p.astype(BF16), dout, (((0,), (0,)), ((), ())), preferred_element_type=F32)

        @pl.when(first)
        def _():
            dk_ref[...] = dk
            dv_ref[...] = dv

        @pl.when(jnp.logical_not(first))
        def _():
            dk_ref[...] += dk
            dv_ref[...] += dv

    kv_spec = pl.BlockSpec((T, HEAD_DIM), lambda h, r, q: (0, h))
    q_spec = pl.BlockSpec((tq, HEAD_DIM), lambda h, r, q: (q + qoff, h * Q_PER_KV + r))
    o_spec = pl.BlockSpec((tq, HEAD_DIM), lambda h, r, q: (q, h * Q_PER_KV + r))
    return pl.pallas_call(
        body, name="attn_bwd", grid=(nkv, Q_PER_KV, nq),
        in_specs=[q_spec, kv_spec, kv_spec, o_spec], out_specs=[o_spec, kv_spec, kv_spec],
        out_shape=[jax.ShapeDtypeStruct((L, D), F32), jax.ShapeDtypeStruct((T, D // Q_PER_KV), F32),
                   jax.ShapeDtypeStruct((T, D // Q_PER_KV), F32)],
        compiler_params=_cparams())(qr, kr, v, do)


def _scan_tile(xr, xi, pw_re, pw_im, lanes, reverse):
    tt = xr.shape[0]
    rows = lax.broadcasted_iota(jnp.int32, (tt, 1), 0)
    for k in range(tt.bit_length() - 1):
        d = 1 << k
        shift = tt - d if reverse else d
        keep = rows < tt - d if reverse else rows >= d
        sr = jnp.where(keep, pltpu.roll(xr, shift, 0), 0.0)
        si = jnp.where(keep, pltpu.roll(xi, shift, 0), 0.0)
        pr, pi = pw_re[k:k + 1, lanes], pw_im[k:k + 1, lanes]
        xr, xi = xr + (pr * sr - pi * si), xi + (pr * si + pi * sr)
    return xr, xi


def _scan_init(lr, li, pw_re, pw_im, w_re, w_im, carry_re, carry_im, nslab, reverse):
    tt = w_re.shape[0]
    carry_re[...] = jnp.zeros_like(carry_re)
    carry_im[...] = jnp.zeros_like(carry_im)
    pr, pi = lr, li
    for k in range(tt.bit_length() - 1):
        pw_re[k:k + 1, :] = pr
        pw_im[k:k + 1, :] = pi
        pr, pi = pr * pr - pi * pi, 2.0 * pr * pi
    rows = lax.broadcasted_iota(jnp.int32, (tt, 1), 0)
    edge = rows == (tt - 1 if reverse else 0)
    for j in range(nslab):
        lanes = slice(j * SLAB_ST, (j + 1) * SLAB_ST)
        wr, wi = _scan_tile(jnp.where(edge, lr[:, lanes], 0.0), jnp.where(edge, li[:, lanes], 0.0),
                            pw_re, pw_im, lanes, reverse)
        w_re[:, lanes] = wr
        w_im[:, lanes] = wi


def _ssm_tiles(T, Lc):
    tt = min(128, Lc)
    return tt, T // tt, Lc // tt


def _ssm_fwd(name, u, bbd, cbd_re, cbd_im, lam_re, lam_im, coef_re, coef_im, Lc, reverse):
    T, W = u.shape
    nslab = W // SLAB_CH
    NS = nslab * SLAB_ST
    tt, nt, nc = _ssm_tiles(T, Lc)
    if reverse:
        tile = lambda s: jnp.where(s < nc, nc - 1 - s, nt - 1 - (s - nc))
    else:
        tile = lambda s: s

    def body(u_ref, b_ref, cr_ref, ci_ref, lr_ref, li_ref, kr_ref, ki_ref, hr_ref, hi_ref, y_ref,
             pw_re, pw_im, w_re, w_im, carry_re, carry_im):
        @pl.when(pl.program_id(0) == 0)
        def _():
            _scan_init(lr_ref[...], li_ref[...], pw_re, pw_im, w_re, w_im, carry_re, carry_im, nslab, reverse)

        edge_row = 0 if reverse else tt - 1
        for j in range(nslab):
            lanes = slice(j * SLAB_ST, (j + 1) * SLAB_ST)
            bu = jnp.dot(u_ref[:, j * SLAB_CH:(j + 1) * SLAB_CH], b_ref[j], preferred_element_type=F32)
            br, bi = bu[:, :SLAB_ST], bu[:, SLAB_ST:]
            kr, ki = kr_ref[:, lanes], ki_ref[:, lanes]
            hr, hi = _scan_tile(kr * br - ki * bi, kr * bi + ki * br, pw_re, pw_im, lanes, reverse)
            car, cai = carry_re[:, lanes], carry_im[:, lanes]
            wr, wi = w_re[:, lanes], w_im[:, lanes]
            hr = hr + (wr * car - wi * cai)
            hi = hi + (wr * cai + wi * car)
            carry_re[:, lanes] = hr[edge_row:edge_row + 1, :]
            carry_im[:, lanes] = hi[edge_row:edge_row + 1, :]
            hrb, hib = hr.astype(BF16), hi.astype(BF16)
            hr_ref[:, lanes] = hrb
            hi_ref[:, lanes] = hib
            y_ref[:, j * SLAB_CH:(j + 1) * SLAB_CH] = (
                jnp.dot(hrb, cr_ref[j], preferred_element_type=F32)
                - jnp.dot(hib, ci_ref[j], preferred_element_type=F32))

    whole3 = lambda arr: pl.BlockSpec(arr.shape, lambda s: (0, 0, 0))
    vec = pl.BlockSpec((1, NS), lambda s: (0, 0))
    return pl.pallas_call(
        body, name=name, grid=(nt,),
        in_specs=[pl.BlockSpec((tt, W), lambda s: (tile(s), 0)), whole3(bbd), whole3(cbd_re), whole3(cbd_im),
                  vec, vec, vec, vec],
        out_specs=[pl.BlockSpec((tt, NS), lambda s: (tile(s), 0)), pl.BlockSpec((tt, NS), lambda s: (tile(s), 0)),
                   pl.BlockSpec((tt, W), lambda s: (tile(s), 0))],
        out_shape=[jax.ShapeDtypeStruct((T, NS), BF16), jax.ShapeDtypeStruct((T, NS), BF16),
                   jax.ShapeDtypeStruct((T, W), F32)],
        scratch_shapes=[pltpu.VMEM((8, NS), F32), pltpu.VMEM((8, NS), F32), pltpu.VMEM((tt, NS), F32),
                        pltpu.VMEM((tt, NS), F32), pltpu.VMEM((1, NS), F32), pltpu.VMEM((1, NS), F32)],
        compiler_params=_cparams())(u, bbd, cbd_re, cbd_im, lam_re, lam_im, coef_re, coef_im)


def _ssm_bwd(name, dy, h_re, h_im, u, bbd, bbdt_re, bbdt_im, cbdt_re, cbdt_im, lam_re, lam_im,
             coef_re, coef_im, Lc, reverse):
    T, W = u.shape
    nslab = W // SLAB_CH
    NS = nslab * SLAB_ST
    tt, nt, nc = _ssm_tiles(T, Lc)
    adj_reverse = not reverse
    if reverse:
        tile = lambda s: jnp.where(s < nt - nc, nc + s, s - (nt - nc))
    else:
        tile = lambda s: nt - 1 - s

    def body(dy_ref, hr_ref, hi_ref, u_ref, b_ref, btr_ref, bti_ref, ctr_ref, cti_ref, lr_ref, li_ref,
             kr_ref, ki_ref, du_ref, dlr_ref, dli_ref, dkr_ref, dki_ref, db_ref, dcr_ref, dci_ref,
             pw_re, pw_im, w_re, w_im, carry_re, carry_im):
        @pl.when(pl.program_id(0) == 0)
        def _():
            _scan_init(lr_ref[...], -li_ref[...], pw_re, pw_im, w_re, w_im, carry_re, carry_im, nslab, adj_reverse)
            for ref in (dlr_ref, dli_ref, dkr_ref, dki_ref, db_ref, dcr_ref, dci_ref):
                ref[...] = jnp.zeros_like(ref)

        rows = lax.broadcasted_iota(jnp.int32, (tt, 1), 0)
        edge_row = 0 if adj_reverse else tt - 1
        far_row = tt - 1 if adj_reverse else 0
        tn_dims = (((0,), (0,)), ((), ()))
        for j in range(nslab):
            lanes = slice(j * SLAB_ST, (j + 1) * SLAB_ST)
            chans = slice(j * SLAB_CH, (j + 1) * SLAB_CH)
            dys, us = dy_ref[:, chans], u_ref[:, chans]
            er = jnp.dot(dys, ctr_ref[j], preferred_element_type=F32)
            ei = -jnp.dot(dys, cti_ref[j], preferred_element_type=F32)
            ar, ai = _scan_tile(er, ei, pw_re, pw_im, lanes, adj_reverse)
            car, cai = carry_re[:, lanes], carry_im[:, lanes]
            wr, wi = w_re[:, lanes], w_im[:, lanes]
            ar = ar + (wr * car - wi * cai)
            ai = ai + (wr * cai + wi * car)
            shift = tt - 1 if adj_reverse else 1
            nr = jnp.where(rows == far_row, car, pltpu.roll(ar, shift, 0))
            ni = jnp.where(rows == far_row, cai, pltpu.roll(ai, shift, 0))
            carry_re[:, lanes] = ar[edge_row:edge_row + 1, :]
            carry_im[:, lanes] = ai[edge_row:edge_row + 1, :]
            hrb, hib = hr_ref[:, lanes], hi_ref[:, lanes]
            hr, hi = hrb.astype(F32), hib.astype(F32)
            dlr_ref[:, lanes] += jnp.sum(nr * hr + ni * hi, axis=0, keepdims=True)
            dli_ref[:, lanes] += jnp.sum(ni * hr - nr * hi, axis=0, keepdims=True)
            bu = jnp.dot(us, b_ref[j], preferred_element_type=F32)
            br, bi = bu[:, :SLAB_ST], bu[:, SLAB_ST:]
            dkr_ref[:, lanes] += jnp.sum(ar * br + ai * bi, axis=0, keepdims=True)
            dki_ref[:, lanes] += jnp.sum(ai * br - ar * bi, axis=0, keepdims=True)
            kr, ki = kr_ref[:, lanes], ki_ref[:, lanes]
            dbr = (ar * kr + ai * ki).astype(BF16)
            dbi = (ai * kr - ar * ki).astype(BF16)
            du_ref[:, chans] = (jnp.dot(dbr, btr_ref[j], preferred_element_type=F32)
                                + jnp.dot(dbi, bti_ref[j], preferred_element_type=F32))
            db_ref[j, :, :SLAB_ST] += lax.dot_general(us, dbr, tn_dims, preferred_element_type=F32)
            db_ref[j, :, SLAB_ST:] += lax.dot_general(us, dbi, tn_dims, preferred_element_type=F32)
            dcr_ref[j] += lax.dot_general(hrb, dys, tn_dims, preferred_element_type=F32)
            dci_ref[j] -= lax.dot_general(hib, dys, tn_dims, preferred_element_type=F32)

    whole3 = lambda arr: pl.BlockSpec(arr.shape, lambda s: (0, 0, 0))
    vec = pl.BlockSpec((1, NS), lambda s: (0, 0))
    row_w = pl.BlockSpec((tt, W), lambda s: (tile(s), 0))
    row_s = pl.BlockSpec((tt, NS), lambda s: (tile(s), 0))
    return pl.pallas_call(
        body, name=name, grid=(nt,),
        in_specs=[row_w, row_s, row_s, row_w, whole3(bbd), whole3(bbdt_re), whole3(bbdt_im), whole3(cbdt_re),
                  whole3(cbdt_im), vec, vec, vec, vec],
        out_specs=[row_w, vec, vec, vec, vec, whole3(bbd), whole3(bbdt_re), whole3(bbdt_re)],
        out_shape=[jax.ShapeDtypeStruct((T, W), F32)] + [jax.ShapeDtypeStruct((1, NS), F32)] * 4
        + [jax.ShapeDtypeStruct(bbd.shape, F32), jax.ShapeDtypeStruct(bbdt_re.shape, F32),
           jax.ShapeDtypeStruct(bbdt_re.shape, F32)],
        scratch_shapes=[pltpu.VMEM((8, NS), F32), pltpu.VMEM((8, NS), F32), pltpu.VMEM((tt, NS), F32),
                        pltpu.VMEM((tt, NS), F32), pltpu.VMEM((1, NS), F32), pltpu.VMEM((1, NS), F32)],
        compiler_params=_cparams())(dy, h_re, h_im, u, bbd, bbdt_re, bbdt_im, cbdt_re, cbdt_im,
                                    lam_re, lam_im, coef_re, coef_im)


def _zoh_math(a_re, a_im, log_dt):
    dt = jnp.exp(log_dt)
    mag = jnp.exp(a_re * dt)
    lb_re = mag * jnp.cos(a_im * dt)
    lb_im = mag * jnp.sin(a_im * dt)
    den = a_re * a_re + a_im * a_im
    coef_re = ((lb_re - 1.0) * a_re + lb_im * a_im) / den
    coef_im = (lb_im * a_re - (lb_re - 1.0) * a_im) / den
    return lb_re, lb_im, coef_re, coef_im


def _zoh_fwd(a_re, a_im, log_dt):
    def body(ar, ai, ld, o0, o1, o2, o3):
        for ref, val in zip((o0, o1, o2, o3), _zoh_math(ar[...], ai[...], ld[...])):
            ref[...] = val

    return pl.pallas_call(body, name="zoh_fwd", out_shape=[jax.ShapeDtypeStruct(a_re.shape, F32)] * 4,
                          compiler_params=_cparams())(a_re, a_im, log_dt)


def _zoh_bwd(a_re, a_im, log_dt, cots):
    def body(ar, ai, ld, c0, c1, c2, c3, o0, o1, o2):
        _, vjp = jax.vjp(_zoh_math, ar[...], ai[...], ld[...])
        for ref, val in zip((o0, o1, o2), vjp((c0[...], c1[...], c2[...], c3[...]))):
            ref[...] = val

    return pl.pallas_call(
        body, name="zoh_bwd",
        out_shape=[jax.ShapeDtypeStruct(a_re.shape, F32), jax.ShapeDtypeStruct(a_re.shape, F32),
                   jax.ShapeDtypeStruct(log_dt.shape, F32)],
        compiler_params=_cparams())(a_re, a_im, log_dt, *cots)


def _outer_sum(acts, cots):
    D, N = acts.shape[1], cots.shape[1]
    tm, tn = _div(D, 512), _div(N, 1152)
    dims = (((0,), (0,)), ((), ()))

    def body(a_ref, b_ref, o_ref):
        a = a_ref[...]
        aa = _split3(a * _sigmoid(a))
        bb = _split3(b_ref[...])
        acc = None
        for ia in range(3):
            for ib in range(3 - ia):
                t = lax.dot_general(aa[ia], bb[ib], dims, preferred_element_type=F32)
                acc = t if acc is None else acc + t
        o_ref[...] = acc

    return pl.pallas_call(
        body, name="mod_dw", grid=(D // tm, N // tn),
        in_specs=[pl.BlockSpec((16, tm), lambda i, j: (0, i)), pl.BlockSpec((16, tn), lambda i, j: (0, j))],
        out_specs=pl.BlockSpec((tm, tn), lambda i, j: (i, j)),
        out_shape=jax.ShapeDtypeStruct((D, N), F32), compiler_params=_cparams())(acts, cots)


def _adamw_math(w, g, m, v):
    m = ADAM_B1 * m + (1.0 - ADAM_B1) * g
    v = ADAM_B2 * v + (1.0 - ADAM_B2) * (g * g)
    m_hat = m / (1.0 - ADAM_B1 ** ADAM_STEP)
    v_hat = v / (1.0 - ADAM_B2 ** ADAM_STEP)
    delta = -ADAM_LR * (m_hat / (jnp.sqrt(v_hat) + ADAM_EPS) + ADAM_WD * w)
    return delta, m, v


def _adamw(name, w, m, v, gparts):
    R, C = w.shape[-2:]
    kind = 'row1' if w.ndim == 3 else 'row'
    tr = _div(R, max(8, 262144 // C), mult=8)

    def fn(i, wv, mv, vv, *gs):
        g = gs[0]
        for extra in gs[1:]:
            g = g + extra
        return (g,) + _adamw_math(wv, g, mv, vv)

    return _rowk(name, fn, R, tr, [(w, kind), (m, kind), (v, kind)] + [(g, 'row') for g in gparts],
                 [(w.shape, F32, kind)] * 4)


def _pack(pieces, rows_mult=8):
    flat = jnp.concatenate([p.reshape(-1).astype(F32) for p in pieces])
    unit = rows_mult * PACK_W
    total = -(-flat.shape[0] // unit) * unit
    return jnp.pad(flat, (0, total - flat.shape[0])).reshape(total // PACK_W, PACK_W)


def _unpack(buf, shapes):
    flat = buf.reshape(-1)
    out, off = [], 0
    for s in shapes:
        n = math.prod(s)
        out.append(flat[off:off + n].reshape(s))
        off += n
    return out


def _bd_expand(t):
    S, g, a, b = t.shape
    eye = jnp.eye(g, dtype=t.dtype)
    return (t[:, :, :, None, :] * eye[None, :, None, :, None]).reshape(S, g * a, g * b)


def _bd_extract(t, a, b):
    S = t.shape[0]
    g = t.shape[1] // a
    eye = jnp.eye(g, dtype=t.dtype)
    return jnp.sum(t.reshape(S, g, a, g, b) * eye[None, :, None, :, None], axis=3)


def _rope_tables(L, Lc):
    rows = L // GRID_W
    row_ids = jnp.broadcast_to(jnp.arange(rows)[:, None], (rows, GRID_W)).reshape(-1).astype(F32)
    col_ids = jnp.broadcast_to(jnp.arange(GRID_W)[None, :], (rows, GRID_W)).reshape(-1).astype(F32)
    quarter = HEAD_DIM // 4
    inv_freq = ROPE_THETA ** (-jnp.arange(quarter, dtype=F32) / quarter)
    ang_r = row_ids[:, None] * inv_freq
    ang_c = col_ids[:, None] * inv_freq
    cos = jnp.concatenate([jnp.cos(ang_r), jnp.cos(ang_r), jnp.cos(ang_c), jnp.cos(ang_c)], axis=1)
    sin = jnp.concatenate([-jnp.sin(ang_r), jnp.sin(ang_r), -jnp.sin(ang_c), jnp.sin(ang_c)], axis=1)
    cos = jnp.concatenate([jnp.ones((Lc, HEAD_DIM), F32), cos], axis=0)
    sin = jnp.concatenate([jnp.zeros((Lc, HEAD_DIM), F32), sin], axis=0)
    return cos, sin


def _rot(v):
    lane = lax.broadcasted_iota(jnp.int32, (1, HEAD_DIM), 1)
    first = (lane % (HEAD_DIM // 2)) < (HEAD_DIM // 4)
    return jnp.where(first, pltpu.roll(v, HEAD_DIM - HEAD_DIM // 4, 1), pltpu.roll(v, HEAD_DIM // 4, 1))


def _head_norm(xh, g):
    return xh * lax.rsqrt(jnp.mean(xh * xh, axis=-1, keepdims=True) + NORM_EPS) * g


def _norm_mod(xv, g, sh, sc):
    r = lax.rsqrt(jnp.mean(xv * xv, axis=-1, keepdims=True) + NORM_EPS)
    return (xv * r) * g * (1.0 + sc) + sh


def kernel(x, c, ctx, c_ctx, w_mod, b_mod, norm_g, w_ffn1_gate, w_ffn1_up, w_ffn1_down, w_in, q_norm_g, k_norm_g, ssm_a_re, ssm_a_im, ssm_log_dt, ssm_b_re, ssm_b_im, ssm_c_re, ssm_c_im, ssm_d, w_glu, b_glu, w_br_attn, w_br_ssm, w_out, w_ffn2_gate, w_ffn2_up, w_ffn2_down, loss_target, m_c_ctx, m_w_mod, m_b_mod, m_norm_g, m_w_ffn1_gate, m_w_ffn1_up, m_w_ffn1_down, m_w_in, m_q_norm_g, m_k_norm_g, m_ssm_a_re, m_ssm_a_im, m_ssm_log_dt, m_ssm_b_re, m_ssm_b_im, m_ssm_c_re, m_ssm_c_im, m_ssm_d, m_w_glu, m_b_glu, m_w_br_attn, m_w_br_ssm, m_w_out, m_w_ffn2_gate, m_w_ffn2_up, m_w_ffn2_down, v_c_ctx, v_w_mod, v_b_mod, v_norm_g, v_w_ffn1_gate, v_w_ffn1_up, v_w_ffn1_down, v_w_in, v_q_norm_g, v_k_norm_g, v_ssm_a_re, v_ssm_a_im, v_ssm_log_dt, v_ssm_b_re, v_ssm_b_im, v_ssm_c_re, v_ssm_c_im, v_ssm_d, v_w_glu, v_b_glu, v_w_br_attn, v_w_br_ssm, v_w_out, v_w_ffn2_gate, v_w_ffn2_up, v_w_ffn2_down):
    A = dict(locals())
    xi, yi, ci = _mesh_pos()
    chip = 2 * xi + yi
    me = 4 * xi + 2 * yi + ci
    L, D = x.shape[1], x.shape[2]
    Lc = ctx.shape[1]
    T = L + Lc
    F4 = w_ffn1_gate.shape[2]
    F = N_CHIPS * F4
    W, KV, Dq = D // 2, D // 4, D // 4
    G = W // SSM_GROUP
    P, E = SSM_STATE, SSM_GROUP
    NS = G * P
    nslab = W // SLAB_CH
    tr = min(256, Lc)
    ncr = Lc // tr
    assert L % tr == 0 and Lc % tr == 0 and W % SLAB_CH == 0 and D % (4 * LANES) == 0

    def sel(i, v):
        return v if v.shape[0] == 1 else jnp.where(i < ncr, v[0:1], v[1:2])

    def put(i, v, nrow):
        if nrow == 1:
            return v
        which = (i >= ncr).astype(jnp.int32)
        r2 = lax.broadcasted_iota(jnp.int32, (nrow, 1), 0)
        return jnp.where(r2 == which, jnp.broadcast_to(v, (nrow, v.shape[1])), 0.0)

    ident = lambda accs, rows, vecs, ri: [accs[0]]

    NM = w_mod.shape[2]
    first = jnp.zeros((8, D), F32).at[0].set(c[0]).at[1:4, :Dq].set(norm_g[0])
    g0 = _allgather_small("gather_c", first).reshape(N_CHIPS, 2, 8, D)
    c_all = g0[:, :, 0].reshape(N_DEV, D)
    ng = jnp.transpose(g0[:, 0, 1:4, :Dq], (1, 0, 2)).reshape(3, D)
    acts = jnp.concatenate([c_all, c_ctx[None], jnp.zeros((7, D), F32)], axis=0)
    wm = w_mod[0]
    b_shard = lax.dynamic_slice(b_mod[0], (chip * NM,), (NM,))[None]
    silu_bf = lambda a: (a * _sigmoid(a)).astype(BF16)
    to_bf = lambda b: b.astype(BF16)
    mod_part = _mm("mod_fwd", [(acts, wm, D)], 16, NM, tm=16, tn=_div(NM, 1152),
                   epi=lambda accs, rows, vecs, ri: [accs[0] + vecs[0]], outs=[(F32, False)],
                   vecs=[b_shard], a_pro=silu_bf, b_pro=to_bf)[0]
    mg = _allgather_small("gather_mod", mod_part).reshape(N_CHIPS, 2, 16, NM)[:, 0]
    mod_all = jnp.transpose(mg, (1, 0, 2)).reshape(16, N_CHIPS * NM)
    mod_x = lax.dynamic_slice(mod_all, (me, 0), (1, 9 * D))
    mod_c = jnp.where(jnp.arange(9 * D)[None] < 5 * D, mod_all[8:9], 0.0)
    modv = jnp.concatenate([mod_c, mod_x], axis=0)
    mv = lambda k: modv[:, k * D:(k + 1) * D]
    sh1, sc1, g1, sh2, sc2 = mv(0), mv(1), mv(2), mv(3), mv(4)
    g2, sh3, sc3, g3 = mv(5)[1:2], mv(6)[1:2], mv(7)[1:2], mv(8)[1:2]

    big = ['w_ffn1_gate', 'w_ffn1_up', 'w_ffn1_down', 'w_ffn2_gate', 'w_ffn2_up', 'w_ffn2_down',
           'w_in', 'w_glu', 'w_br_attn', 'w_br_ssm', 'w_out']
    row_sharded = {'w_ffn1_down', 'w_ffn2_down', 'w_glu', 'w_br_attn', 'w_out'}
    groups = [big[0:2], big[2:3], big[6:11], big[3:6]]
    chip_index = jnp.reshape(chip, (1,)).astype(jnp.int32)
    slots = {n: _cast_slot("cast_" + n, A[n], chip_index) for n in big}
    tok, gather_finish = modv, []
    for gi, names in enumerate(groups):
        tok, fin = _gather_split("gather_w%d" % gi, [slots[n] for n in names], tok)
        gather_finish.append(fin)
    ng = ng + tok[0:1, 0:1]
    Wt = {}

    def weights_ready(gi, after_work):
        names = groups[gi]
        lands = gather_finish[gi](after_work)
        full = _gather_finish("gather_w%d_pass" % gi, lands)
        for n, gw in zip(names, full):
            Wt[n] = gw.reshape(N_CHIPS * gw.shape[1], gw.shape[2]) if n in row_sharded else gw

    weights_ready(0, tok)

    a_re2, a_im2 = ssm_a_re[0].reshape(2 * G, P), ssm_a_im[0].reshape(2 * G, P)
    ldt2 = ssm_log_dt[0].reshape(2 * G, 1)
    zoh = _zoh_fwd(a_re2, a_im2, ldt2)
    lam_re, lam_im, coef_re, coef_im = [[z[d * G:(d + 1) * G].reshape(1, NS) for d in range(2)] for z in zoh]
    bd_b = lambda b: _bd_expand(jnp.transpose(b, (0, 2, 1)).reshape(nslab, SLAB_GROUPS, E, P))
    bd_c = lambda cc: _bd_expand(jnp.transpose(cc, (0, 2, 1)).reshape(nslab, SLAB_GROUPS, P, E))
    bbd, bbdt_re, bbdt_im, cbd_re, cbd_im, cbdt_re, cbdt_im = [], [], [], [], [], [], []
    for d in range(2):
        br_, bi_ = bd_b(ssm_b_re[0, d]).astype(BF16), bd_b(ssm_b_im[0, d]).astype(BF16)
        cr_, ci_ = bd_c(ssm_c_re[0, d]).astype(BF16), bd_c(ssm_c_im[0, d]).astype(BF16)
        bbd.append(jnp.concatenate([br_, bi_], axis=2))
        bbdt_re.append(jnp.transpose(br_, (0, 2, 1)))
        bbdt_im.append(jnp.transpose(bi_, (0, 2, 1)))
        cbd_re.append(cr_)
        cbd_im.append(ci_)
        cbdt_re.append(jnp.transpose(cr_, (0, 2, 1)))
        cbdt_im.append(jnp.transpose(ci_, (0, 2, 1)))
    cos_t, sin_t = _rope_tables(L, Lc)
    qg, kg = q_norm_g, k_norm_g

    def norm_mod(name, xv, g, sh, sc):
        rows = xv.shape[0]
        return _rowk(name, lambda i, xt, gt, sht, sct: [_norm_mod(xt, gt, sel(i, sht), sel(i, sct))],
                     rows, tr, [(xv, 'row'), (g, 'vec'), (sh, 'vec'), (sc, 'vec')], [((rows, D), BF16, 'row')])[0]

    def swiglu_epi(accs, rows, vecs, ri):
        a_, b_ = accs
        return [a_, b_, a_ * _sigmoid(a_) * b_]

    def res_epi(coef):
        def epi(accs, rows, vecs, ri):
            gate = vecs[0]
            if gate.shape[0] == 2:
                gate = jnp.where(ri < Lc, gate[0:1], gate[1:2])
            return [accs[0], rows[0] + (coef * gate) * accs[0]]
        return epi

    def ffn_fwd(tag, h, xres, gate, down_ready=None):
        rows = h.shape[0]
        a_, b_, s_ = _mm(tag + "_up", [(h, Wt['w_' + tag + '_gate'], D), (h, Wt['w_' + tag + '_up'], D)], rows, F,
                         tm=_div(rows, 256), tn=F4, epi=swiglu_epi, outs=[(F32, False), (F32, False), (BF16, False)])
        if down_ready is not None:
            down_ready(s_)
        f_, xo = _mm(tag + "_down", [(s_, Wt['w_' + tag + '_down'], F)], rows, D, tm=_div(rows, 512),
                     tn=_div(D, 1024), nk=N_CHIPS, epi=res_epi(0.5), outs=[(F32, False), (F32, False)],
                     rows=[(xres, 0, 0)], vecs=[gate])
        return a_, b_, s_, f_, xo

    xc = jnp.concatenate([ctx[0], x[0]], axis=0)
    h1 = norm_mod("norm1", xc, ng[0:1], sh1, sc1)
    a1, b1, s1, f1, x1 = ffn_fwd("ffn1", h1, xc, g1, down_ready=lambda s_: weights_ready(1, s_))
    weights_ready(2, x1)
    h2 = norm_mod("norm2", x1, ng[1:2], sh2, sc2)
    proj = _mm("in_proj", [(h2, Wt['w_in'], D)], T, 4 * D, tm=_div(T, 768), tn=_div(D, 1024), epi=ident,
               outs=[(F32, False)])[0]
    nh, nkvh = D // HEAD_DIM, KV // HEAD_DIM

    def prep_fn(i, kt, vt, ut, qt, qgt, kgt, ct, st):
        qs = [_head_norm(qt[:, h * HEAD_DIM:(h + 1) * HEAD_DIM], qgt) for h in range(nh)]
        ks = [_head_norm(kt[:, h * HEAD_DIM:(h + 1) * HEAD_DIM], kgt) for h in range(nkvh)]
        qs = [v * ct + _rot(v) * st for v in qs]
        ks = [v * ct + _rot(v) * st for v in ks]
        return [jnp.concatenate(qs, axis=1), jnp.concatenate(ks, axis=1), vt, ut]

    qr, kr, vb, ub = _rowk(
        "qk_prep", prep_fn, T, tr,
        [(proj, ('col', KV, 0)), (proj, ('col', KV, 1)), (proj, ('col', W, 1)), (proj, ('col', D, 1)),
         (qg, 'vec'), (kg, 'vec'), (cos_t, 'row'), (sin_t, 'row')],
        [((T, D), BF16, 'row'), ((T, KV), BF16, 'row'), ((T, KV), BF16, 'row'), ((T, W), BF16, 'row')])
    attn = _attn_fwd(qr, kr, vb, L, Lc, D)
    hs_re, hs_im, ys = [], [], []
    for d in range(2):
        hr_, hi_, y_ = _ssm_fwd("ssm_fwd%d" % d, ub, bbd[d], cbd_re[d], cbd_im[d], lam_re[d], lam_im[d],
                                coef_re[d], coef_im[d], Lc, reverse=bool(d))
        hs_re.append(hr_)
        hs_im.append(hi_)
        ys.append(y_)

    def ssm_out_fn(i, y0, y1, ut, dt):
        pre = dt * ut + y0 + y1
        yg_ = _gelu(pre)
        return [pre, yg_, yg_]

    ssm_pre, yg, ygb = _rowk(
        "ssm_out", ssm_out_fn, L, tr,
        [(ys[0], 'orow'), (ys[1], 'orow'), (proj, ('ocol', W, 1)), (ssm_d, 'vec')],
        [((L, W), F32, 'row'), ((L, W), F32, 'row'), ((L, W), BF16, 'row')], nc=ncr)

    def glu_epi(accs, rows, vecs, ri):
        z_ = accs[0] + vecs[0]
        return [z_, rows[0] * _sigmoid(z_)]

    zglu, y2 = _mm("glu", [(ygb, Wt['w_glu'], W)], L, W, tm=_div(L, 512), tn=_div(W, 512), epi=glu_epi,
                   outs=[(F32, False), (BF16, False)], rows=[(yg, 0, 0)], vecs=[b_glu])
    tnm = _div(Dq, 512)

    def merge_epi(accs, rows, vecs, ri):
        ga, gs = _sigmoid(rows[0]), _sigmoid(rows[1])
        return [accs[0], accs[1], ga * accs[0] + gs * accs[1]]

    ba, bs, merged = _mm("merge", [(attn, Wt['w_br_attn'], D), (y2, Wt['w_br_ssm'], W)], L, D, tm=tr, tn=tnm,
                         epi=merge_epi, outs=[(F32, False), (F32, False), (BF16, False)],
                         rows=[(proj, ncr, 2 * D // tnm), (proj, ncr, 3 * D // tnm)])
    mix, x2 = _mm("out_proj", [(merged, Wt['w_out'], D)], L, D, tm=tr, tn=_div(D, 1024), epi=res_epi(1.0),
                  outs=[(F32, False), (F32, False)], rows=[(x1, ncr, 0)], vecs=[g2])
    weights_ready(3, x2)
    h3 = norm_mod("norm3", x2, ng[2:3], sh3, sc3)
    a3, b3, s3, f3, x3 = ffn_fwd("ffn2", h3, x2, g3)

    def loss_fn(i, yt, tt_):
        diff = yt - tt_
        return [diff * (1.0 / D), jnp.sum(diff * diff, axis=0, keepdims=True)]

    dy, sq = _rowk("loss", loss_fn, L, tr, [(x3, 'row'), (loss_target[0], 'row')],
                   [((L, D), F32, 'row'), ((1, D), F32, 'acc')])
    loss = lax.psum(0.5 * jnp.sum(sq) / D, ("x", "y", "c"))

    def res_bwd(name, dxo, f_, gate, coef):
        rows, nrow = dxo.shape[0], gate.shape[0]

        def fn(i, dt, ft, gt):
            return [(coef * sel(i, gt)) * dt, put(i, jnp.sum(dt * ft, axis=0, keepdims=True) * coef, nrow)]

        return _rowk(name, fn, rows, tr, [(dxo, 'row'), (f_, 'row'), (gate, 'vec')],
                     [((rows, D), BF16, 'row'), ((nrow, D), F32, 'acc')])

    def swiglu_bwd_epi(accs, rows, vecs, ri):
        ds_, a_, b_ = accs[0], rows[0], rows[1]
        sg = _sigmoid(a_)
        return [ds_ * b_ * (sg * (1.0 + a_ * (1.0 - sg))), ds_ * (a_ * sg)]

    sum2 = lambda accs, rows, vecs, ri: [accs[0] + accs[1]]

    def norm_mod_bwd(name, xv, g, sh, sc, dh, dres, dres_kind):
        rows, nrow = xv.shape[0], sh.shape[0]

        def fn(i, xt, gt, sht, sct, dht, rest):
            _, vjp = jax.vjp(_norm_mod, xt, gt, sel(i, sht), sel(i, sct))
            dx_, dg_, dsh_, dsc_ = vjp(dht)
            dx_ = dx_ + (jnp.where(i >= ncr, rest, 0.0) if dres_kind == 'xrow' else rest)
            return [dx_, dg_, put(i, dsh_, nrow), put(i, dsc_, nrow)]

        return _rowk(name, fn, rows, tr,
                     [(xv, 'row'), (g, 'vec'), (sh, 'vec'), (sc, 'vec'), (dh, 'row'), (dres, dres_kind)],
                     [((rows, D), F32, 'row'), ((1, D), F32, 'acc'), ((nrow, D), F32, 'acc'), ((nrow, D), F32, 'acc')],
                     nc=ncr)

    def ffn_bwd(tag, dxo, h, a_, b_, s_, f_, gate, wg, wu, wd, on_dwd=None):
        rows = dxo.shape[0]
        df, dgate = res_bwd(tag + "_dres", dxo, f_, gate, 0.5)
        dwd = _mm(tag + "_dwd", [(s_, df, rows)], F, D, tm=_div(F, 512), tn=_div(D, 1024), ta=True, epi=ident,
                  outs=[(BF16, False)])[0].reshape(N_CHIPS, F4, D)
        if on_dwd is not None:
            on_dwd(dwd)
        da, db = _mm(tag + "_dact", [(df, wd, D)], rows, F, tm=_div(rows, 384), tn=F4, tb=True, epi=swiglu_bwd_epi,
                     outs=[(BF16, False), (BF16, False)], rows=[(a_, 0, 0), (b_, 0, 0)])
        dwg = _mm(tag + "_dwg", [(h, da, rows)], D, F, tm=_div(D, 512), tn=F4, ta=True, epi=ident,
                  outs=[(BF16, True)])[0]
        dwu = _mm(tag + "_dwu", [(h, db, rows)], D, F, tm=_div(D, 512), tn=F4, ta=True, epi=ident,
                  outs=[(BF16, True)])[0]
        dh = _mm(tag + "_dh", [(da, wg, F), (db, wu, F)], rows, D, tm=_div(rows, 768), tn=_div(D, 1024), nk=N_CHIPS,
                 tb=True, epi=sum2, outs=[(F32, False)])[0]
        return dh, dgate, dwg, dwu, dwd

    dh3, dg3, dwg2, dwu2, dwd2 = ffn_bwd("ffn2", dy, h3, a3, b3, s3, f3, g3, Wt['w_ffn2_gate'], Wt['w_ffn2_up'],
                                         Wt['w_ffn2_down'])
    tok_r1, scatter_fin1 = _scatter_split("scatter_ffn2", [dwg2, dwu2, dwd2], dg3)
    dx2, dng3, dsh3, dsc3 = norm_mod_bwd("norm3_bwd", x2, ng[2:3], sh3, sc3, dh3, dy, 'row')
    dmix, dg2 = res_bwd("mix_dres", dx2, mix, g2 + tok_r1[0:1, 0:1], 1.0)

    def dmerge_epi(accs, rows, vecs, ri):
        dm_, ba_, bs_ = accs[0], rows[0], rows[1]
        ga, gs = _sigmoid(rows[2]), _sigmoid(rows[3])
        return [dm_ * ga, dm_ * gs, dm_ * ba_ * ga * (1.0 - ga), dm_ * bs_ * gs * (1.0 - gs)]

    dba, dbs, dga, dgs = _mm("dmerge", [(dmix, Wt['w_out'], D)], L, D, tm=tr, tn=tnm, tb=True, epi=dmerge_epi,
                             outs=[(BF16, False)] * 4,
                             rows=[(ba, 0, 0), (bs, 0, 0), (proj, ncr, 2 * D // tnm), (proj, ncr, 3 * D // tnm)])
    dwout = _mm("dw_out", [(merged, dmix, L)], D, D, tm=_div(D, 512), tn=_div(D, 1024), ta=True, epi=ident,
                outs=[(BF16, False)])[0].reshape(N_CHIPS, Dq, D)
    dattn = _mm("dattn", [(dba, Wt['w_br_attn'], D)], L, D, tm=_div(L, 512), tn=_div(D, 1024), tb=True, epi=ident,
                outs=[(BF16, False)])[0]
    dwba = _mm("dw_br_attn", [(attn, dba, L)], D, D, tm=_div(D, 512), tn=_div(D, 1024), ta=True, epi=ident,
               outs=[(BF16, False)])[0].reshape(N_CHIPS, Dq, D)
    dy2 = _mm("dy2", [(dbs, Wt['w_br_ssm'], D)], L, W, tm=_div(L, 512), tn=_div(W, 1024), nk=N_CHIPS, tb=True,
              epi=ident, outs=[(F32, False)])[0]
    dwbs = _mm("dw_br_ssm", [(y2, dbs, L)], W, D, tm=_div(W, 512), tn=_div(Dq, 512), ta=True, epi=ident,
               outs=[(BF16, True)])[0]

    def glu_bwd_fn(i, d2, ygt, zt):
        sz = _sigmoid(zt)
        dz_ = d2 * ygt * sz * (1.0 - sz)
        return [dz_, d2 * sz, jnp.sum(dz_, axis=0, keepdims=True)]

    dz, dyd, dbglu = _rowk("glu_bwd", glu_bwd_fn, L, tr, [(dy2, 'row'), (yg, 'row'), (zglu, 'row')],
                           [((L, W), BF16, 'row'), ((L, W), F32, 'row'), ((1, W), F32, 'acc')])

    def dssm_epi(accs, rows, vecs, ri):
        _, vjp = jax.vjp(_gelu, rows[1])
        ds_ = vjp(accs[0] + rows[0])[0]
        return [ds_, ds_]

    dssm, dssm_b = _mm("dssm", [(dz, Wt['w_glu'], W)], L, W, tm=_div(L, 512), tn=_div(W, 512), tb=True, epi=dssm_epi,
                       outs=[(F32, False), (BF16, False)], rows=[(dyd, 0, 0), (ssm_pre, 0, 0)])
    dwglu = _mm("dw_glu", [(ygb, dz, L)], W, W, tm=_div(W, 512), tn=_div(W, 1024), ta=True, epi=ident,
                outs=[(BF16, False)])[0].reshape(N_CHIPS, W // N_CHIPS, W)
    dssm_full = jnp.concatenate([jnp.zeros((Lc, W), BF16), dssm_b], axis=0)
    dus, dlam_re, dlam_im, dcoef_re, dcoef_im, dbbd, dcbd_re, dcbd_im = [], [], [], [], [], [], [], []
    for d in range(2):
        r = _ssm_bwd("ssm_bwd%d" % d, dssm_full, hs_re[d], hs_im[d], ub, bbd[d], bbdt_re[d], bbdt_im[d],
                     cbdt_re[d], cbdt_im[d], lam_re[d], lam_im[d], coef_re[d], coef_im[d], Lc, reverse=bool(d))
        for lst, val in zip((dus, dlam_re, dlam_im, dcoef_re, dcoef_im, dbbd, dcbd_re, dcbd_im), r):
            lst.append(val)
    dqr, dkr, dvf = _attn_bwd(qr, kr, vb, dattn, L, Lc, D)

    def prep_bwd_fn(i, qt, kt, ut, dqt, dkt, dvt, du0, du1, dst, dt, qgt, kgt, ct, st):
        live = i >= ncr
        dqt = jnp.where(live, dqt, 0.0)
        dst = jnp.where(live, dst, 0.0)
        dqs, dks = [], []
        dqg_ = jnp.zeros((1, HEAD_DIM), F32)
        dkg_ = jnp.zeros((1, HEAD_DIM), F32)
        for h in range(nh):
            hl = slice(h * HEAD_DIM, (h + 1) * HEAD_DIM)
            dn = dqt[:, hl] * ct + _rot(dqt[:, hl] * st)
            _, vjp = jax.vjp(_head_norm, qt[:, hl], qgt)
            dxh, dgh = vjp(dn)
            dqs.append(dxh)
            dqg_ = dqg_ + dgh
        for h in range(nkvh):
            hl = slice(h * HEAD_DIM, (h + 1) * HEAD_DIM)
            dn = dkt[:, hl] * ct + _rot(dkt[:, hl] * st)
            _, vjp = jax.vjp(_head_norm, kt[:, hl], kgt)
            dxh, dgh = vjp(dn)
            dks.append(dxh)
            dkg_ = dkg_ + dgh
        du_ = du0 + du1 + dst * dt
        return [jnp.concatenate(dqs, axis=1), jnp.concatenate(dks, axis=1), dvt, du_, dqg_, dkg_,
                jnp.sum(dst * ut, axis=0, keepdims=True)]

    dq_b, dk_b, dv_b, du_b, dqg, dkg, dssd = _rowk(
        "qk_prep_bwd", prep_bwd_fn, T, tr,
        [(proj, ('col', D, 1)), (proj, ('col', KV, 0)), (proj, ('col', W, 1)), (dqr, 'xrow'), (dkr, 'row'),
         (dvf, 'row'), (dus[0], 'row'), (dus[1], 'row'), (dssm, 'xrow'), (ssm_d, 'vec'), (qg, 'vec'), (kg, 'vec'),
         (cos_t, 'row'), (sin_t, 'row')],
        [((T, D), BF16, 'row'), ((T, KV), BF16, 'row'), ((T, KV), BF16, 'row'), ((T, W), BF16, 'row'),
         ((1, HEAD_DIM), F32, 'acc'), ((1, HEAD_DIM), F32, 'acc'), ((1, W), F32, 'acc')], nc=ncr)
    dgate = jnp.concatenate([jnp.zeros((Lc, 2 * D), BF16), jnp.concatenate([dga, dgs], axis=1)], axis=0)
    dproj = jnp.concatenate([dk_b, dv_b, du_b, dq_b, dgate], axis=1)
    dh2 = _mm("in_proj_dx", [(dproj, Wt['w_in'], 4 * D)], T, D, tm=_div(T, 768), tn=_div(D, 1024), nk=N_CHIPS, tb=True,
              epi=ident, outs=[(F32, False)])[0]
    dwin = _mm("in_proj_dw", [(h2, dproj, T)], D, 4 * D, tm=_div(D, 512), tn=_div(D, 1024), ta=True, epi=ident,
               outs=[(BF16, True)])[0]
    tok_r2, scatter_fin2 = _scatter_split("scatter_mix", [dwin, dwglu, dwba, dwbs, dwout], dqg)
    dx1, dng2, dsh2, dsc2 = norm_mod_bwd("norm2_bwd", x1, ng[1:2] + tok_r2[0:1, 0:1], sh2, sc2, dh2, dx2, 'xrow')
    early = {}

    def start_down(dwd):
        early['tok'], early['fin'] = _scatter_split("scatter_ffn1_down", [dwd], dg2)

    dh1, dg1, dwg1, dwu1, dwd1 = ffn_bwd("ffn1", dx1, h1, a1, b1, s1, f1, g1, Wt['w_ffn1_gate'], Wt['w_ffn1_up'],
                                         Wt['w_ffn1_down'], on_dwd=start_down)
    dx0, dng1, dsh1, dsc1 = norm_mod_bwd("norm1_bwd", xc, ng[0:1] + early['tok'][0:1, 0:1], sh1, sc1, dh1, dx1, 'row')
    grad_x = dx0[Lc:][None]

    zD = jnp.zeros((1, D), F32)
    dmod_x = jnp.concatenate([dsh1[1:2], dsc1[1:2], dg1[1:2], dsh2[1:2], dsc2[1:2], dg2, dsh3, dsc3, dg3], axis=1)
    dmod_c = jnp.concatenate([dsh1[0:1], dsc1[0:1], dg1[0:1], dsh2[0:1], dsc2[0:1], zD, zD, zD, zD], axis=1)
    db_parts, dc_parts = [], []
    for d in range(2):
        db_parts.append(jnp.transpose(_bd_extract(dbbd[d][:, :, :SLAB_ST], E, P).reshape(G, E, P), (0, 2, 1)))
        db_parts.append(jnp.transpose(_bd_extract(dbbd[d][:, :, SLAB_ST:], E, P).reshape(G, E, P), (0, 2, 1)))
        dc_parts.append(jnp.transpose(_bd_extract(dcbd_re[d], P, E).reshape(G, P, E), (0, 2, 1)))
        dc_parts.append(jnp.transpose(_bd_extract(dcbd_im[d], P, E).reshape(G, P, E), (0, 2, 1)))
    pieces = [dmod_x, dmod_c, dng1, dng2, dng3, dqg, dkg] + dlam_re + dlam_im + dcoef_re + dcoef_im \
        + db_parts + dc_parts + [dssd, dbglu]
    shapes = [p_.shape for p_ in pieces]
    pack = _pack(pieces)
    RP = pack.shape[0]
    allp = _allgather_small("gather_small", pack).reshape(N_DEV, RP, PACK_W)

    head_rows = -(-18 * D // PACK_W)
    head = allp[:, :head_rows].reshape(N_DEV, head_rows * PACK_W)
    dmx_all = head[:, :9 * D]

    def sum_rows_fn(i, t):
        s_ = t[0:1]
        for k in range(1, N_DEV):
            s_ = s_ + t[k:k + 1]
        return [s_]

    dmc_sum = _rowk("sum_dmod_c", sum_rows_fn, 1, 1, [(head[:, 9 * D:18 * D], 'vec')], [((1, 9 * D), F32, 'row')])[0]
    cots = jnp.concatenate([dmx_all, dmc_sum, jnp.zeros((7, 9 * D), F32)], axis=0)
    cots_sh = lax.dynamic_slice(cots, (0, chip * NM), (16, NM))
    part = _mm("cctx_part", [(cots_sh[8:16], wm, NM)], 8, D, tm=8, tn=_div(D, 1024), nk=NM // _div(NM, 1152), tb=True,
               epi=ident, outs=[(F32, False)], a_pro=to_bf, b_pro=to_bf)[0]
    parts = _allgather_small("gather_cctx", part).reshape(N_CHIPS, 2, 8, D)[:, 0, 0]

    def cctx_fn(i, pt, ct):
        ds_ = ((pt[0:1] + pt[1:2]) + pt[2:3]) + pt[3:4]
        _, vjp = jax.vjp(lambda v: v * _sigmoid(v), ct)
        return [vjp(ds_)[0]]

    g_cctx = _rowk("cctx_grad", cctx_fn, 1, 1, [(parts, 'vec'), (c_ctx[None], 'row')], [((1, D), F32, 'row')])[0]

    tok_r3, scatter_fin3 = _scatter_split("scatter_ffn1_up", [dwg1, dwu1], g_cctx)
    zero = tok_r3[0:1, 0:1]

    def sum_dev_fn(i, t):
        s_ = t[0]
        for k in range(1, N_DEV):
            s_ = s_ + t[k]
        return [s_]

    tot = _rowk("sum_small", sum_dev_fn, RP, 8, [(allp, 'row3')], [((RP, PACK_W), F32, 'row')])[0]
    (t_dmod_x, t_dmod_c, t_ng1, t_ng2, t_ng3, t_qg, t_kg, t_lr0, t_lr1, t_li0, t_li1, t_kr0, t_kr1, t_ki0, t_ki1,
     t_bre0, t_bim0, t_bre1, t_bim1, t_cre0, t_cim0, t_cre1, t_cim1, t_d, t_bglu) = _unpack(tot, shapes)
    cat2 = lambda u0, u1: jnp.concatenate([u0.reshape(G, P), u1.reshape(G, P)], axis=0)
    g_are, g_aim, g_ldt = _zoh_bwd(a_re2 + zero, a_im2, ldt2, [cat2(t_lr0, t_lr1), cat2(t_li0, t_li1),
                                                                cat2(t_kr0, t_kr1), cat2(t_ki0, t_ki1)])
    g_bmod = _rowk("bmod_grad", lambda i, u0, u1: [u0 + u1], 1, 1, [(t_dmod_x, 'row'), (t_dmod_c, 'row')],
                   [((1, 9 * D), F32, 'row')])[0]
    g_wmod = _outer_sum(acts + zero, cots_sh)
    results = {}
    results['w_mod'] = _adamw("adamw_w_mod", w_mod, m_w_mod, v_w_mod, [g_wmod])

    def sum_chip_fn(i, own, t):
        return [((own.astype(F32) + t[0].astype(F32)) + t[1].astype(F32)) + t[2].astype(F32)]

    def reduce_group(tag, names, grads, fin, after_work):
        landed = fin(after_work)
        plane = []
        for n, g_, rb in zip(names, grads, landed):
            R_, C_ = rb.shape[1], rb.shape[2]
            own = lax.dynamic_index_in_dim(g_, chip, 0, keepdims=False)
            plane.append(_rowk("sum_" + n, sum_chip_fn, R_, _div(R_, max(16, 262144 // C_), mult=16),
                               [(own, 'row'), (rb, 'row3')], [((R_, C_), F32, 'row')])[0])
        other = _swap_sibling("swap_" + tag, plane)
        for n, mine, theirs in zip(names, plane, other):
            results[n] = _adamw("adamw_" + n, A[n], A['m_' + n], A['v_' + n], [mine, theirs])

    reduce_group("ffn2", big[3:6], [dwg2, dwu2, dwd2], scatter_fin1, tok_r3)
    reduce_group("mix", big[6:11], [dwin, dwglu, dwba, dwbs, dwout], scatter_fin2, results['w_ffn2_down'][0])
    reduce_group("ffn1_down", big[2:3], [dwd1], early['fin'], results['w_out'][0])
    reduce_group("ffn1_up", big[0:2], [dwg1, dwu1], scatter_fin3, results['w_ffn1_down'][0])

    small = ['c_ctx', 'b_mod', 'norm_g', 'q_norm_g', 'k_norm_g', 'ssm_a_re', 'ssm_a_im', 'ssm_log_dt', 'ssm_b_re',
             'ssm_b_im', 'ssm_c_re', 'ssm_c_im', 'ssm_d', 'b_glu']
    ng_full = jnp.concatenate([t_ng1, t_ng2, t_ng3], axis=0)
    gsmall = {
        'c_ctx': g_cctx, 'b_mod': g_bmod, 'norm_g': lax.dynamic_slice(ng_full, (0, chip * Dq), (3, Dq)),
        'q_norm_g': t_qg, 'k_norm_g': t_kg, 'ssm_a_re': g_are, 'ssm_a_im': g_aim, 'ssm_log_dt': g_ldt,
        'ssm_b_re': jnp.stack([t_bre0, t_bre1]), 'ssm_b_im': jnp.stack([t_bim0, t_bim1]),
        'ssm_c_re': jnp.stack([t_cre0, t_cre1]), 'ssm_c_im': jnp.stack([t_cim0, t_cim1]),
        'ssm_d': t_d, 'b_glu': t_bglu}
    sshapes = [A[n].shape for n in small]
    packs = [_pack([A[pre + n] for n in small]) for pre in ('', 'm_', 'v_')] + [_pack([gsmall[n] for n in small])]
    sres = _adamw("adamw_small", packs[0], packs[1], packs[2], [packs[3]])
    sres = [_unpack(b_, sshapes) for b_ in sres]
    for k, n in enumerate(small):
        results[n] = tuple(sres[q][k] for q in range(4))

    order = ['c_ctx', 'w_mod', 'b_mod', 'norm_g', 'w_ffn1_gate', 'w_ffn1_up', 'w_ffn1_down', 'w_in', 'q_norm_g',
             'k_norm_g', 'ssm_a_re', 'ssm_a_im', 'ssm_log_dt', 'ssm_b_re', 'ssm_b_im', 'ssm_c_re', 'ssm_c_im',
             'ssm_d', 'w_glu', 'b_glu', 'w_br_attn', 'w_br_ssm', 'w_out', 'w_ffn2_gate', 'w_ffn2_up', 'w_ffn2_down']
    outs = [loss, grad_x]
    for q in range(4):
        outs += [results[n][q].reshape(A[n].shape) for n in order]
    return tuple(outs)
```

```python
import math

import jax
import jax.numpy as jnp
from jax import lax
from jax.experimental import pallas as pl
from jax.experimental.pallas import tpu as pltpu

F32 = jnp.float32
BF16 = jnp.bfloat16
MESH = pl.DeviceIdType.MESH

NORM_EPS = 1e-6
ROPE_THETA = 10000.0
GRID_W = 64
HEAD_DIM = 128
Q_PER_KV = 4
SSM_GROUP = 16
SSM_STATE = 64
ADAM_LR = 0.001
ADAM_B1 = 0.9
ADAM_B2 = 0.999
ADAM_EPS = 1e-08
ADAM_WD = 0.01
ADAM_STEP = 10

N_CHIPS = 4
N_DEV = 8
LANES = 128
SLAB_CH = 128
SLAB_GROUPS = SLAB_CH // SSM_GROUP
SLAB_ST = SLAB_GROUPS * SSM_STATE
VMEM_LIMIT_BYTES = 56 * 1024 * 1024
PACK_W = 1024


def _cparams(**kw):
    return pltpu.CompilerParams(vmem_limit_bytes=VMEM_LIMIT_BYTES, **kw)


def _div(n, pref, mult=LANES):
    t = (min(pref, n) // mult) * mult
    while t >= mult:
        if n % t == 0:
            return t
        t -= mult
    return n


def _sigmoid(x):
    return jax.nn.sigmoid(x)


def _gelu(x):
    return x * (0.5 * (1.0 + jnp.tanh(math.sqrt(2.0 / math.pi) * (x + 0.044715 * (x * x * x)))))


def _rowk(name, fn, nrows, tr, ins, outs, nc=0):
    nt = nrows // tr
    in_specs, arrays = [], []
    for arr, kind in ins:
        arrays.append(arr)
        if kind == 'row':
            in_specs.append(pl.BlockSpec((tr, arr.shape[1]), lambda i: (i, 0)))
        elif kind == 'xrow':
            in_specs.append(pl.BlockSpec((tr, arr.shape[1]), lambda i: (jnp.maximum(i - nc, 0), 0)))
        elif kind == 'orow':
            in_specs.append(pl.BlockSpec((tr, arr.shape[1]), lambda i: (i + nc, 0)))
        elif kind == 'vec':
            in_specs.append(pl.BlockSpec(arr.shape, lambda i, nd=arr.ndim: (0,) * nd))
        elif kind == 'row3':
            in_specs.append(pl.BlockSpec((arr.shape[0], tr, arr.shape[2]), lambda i: (0, i, 0)))
        elif kind == 'row1':
            in_specs.append(pl.BlockSpec((None, tr, arr.shape[2]), lambda i: (0, i, 0)))
        elif kind[0] == 'ocol':
            _, width, blk = kind
            in_specs.append(pl.BlockSpec((tr, width), lambda i, blk=blk: (i + nc, blk)))
        else:
            _, width, blk = kind
            in_specs.append(pl.BlockSpec((tr, width), lambda i, blk=blk: (i, blk)))
    out_shape, out_specs = [], []
    for shape, dtype, kind in outs:
        out_shape.append(jax.ShapeDtypeStruct(shape, dtype))
        if kind == 'row':
            out_specs.append(pl.BlockSpec((tr, shape[1]), lambda i: (i, 0)))
        elif kind == 'row1':
            out_specs.append(pl.BlockSpec((None, tr, shape[2]), lambda i: (0, i, 0)))
        else:
            out_specs.append(pl.BlockSpec(shape, lambda i, nd=len(shape): (0,) * nd))
    nin = len(ins)

    def body(*refs):
        i = pl.program_id(0)
        res = fn(i, *[r[...] for r in refs[:nin]])
        for (shape, dtype, kind), ref, val in zip(outs, refs[nin:], res):
            if kind in ('row', 'row1'):
                ref[...] = val.astype(dtype)
            else:
                @pl.when(i == 0)
                def _():
                    ref[...] = val.astype(dtype)

                @pl.when(i > 0)
                def _():
                    ref[...] += val.astype(dtype)

    return pl.pallas_call(body, name=name, grid=(nt,), in_specs=in_specs, out_specs=out_specs,
                          out_shape=out_shape, compiler_params=_cparams())(*arrays)


def _mm(name, pairs, M, N, *, tm, tn, nk=1, epi, outs, ta=False, tb=False, rows=(), vecs=(),
        a_pro=None, b_pro=None, n_outer=True, summed=False):
    nm, nn = M // tm, N // tn
    npair = len(pairs)

    def idx(f):
        if n_outer:
            return lambda j, i, k: f(i, j, k)
        return lambda i, j, k: f(i, j, k)

    in_specs, args = [], []
    for a, b, K in pairs:
        tk = K // nk
        if ta:
            in_specs.append(pl.BlockSpec((tk, tm), idx(lambda i, j, k: (k, i))))
        else:
            in_specs.append(pl.BlockSpec((tm, tk), idx(lambda i, j, k: (i, k))))
        args.append(a)
        if b.ndim == 3:
            if tb:
                per = b.shape[2] // tk
                in_specs.append(pl.BlockSpec((None, tn, tk), idx(lambda i, j, k, per=per: (k // per, j, k % per))))
            else:
                per = b.shape[2] // tn
                in_specs.append(pl.BlockSpec((None, tk, tn), idx(lambda i, j, k, per=per: (j // per, k, j % per))))
        elif tb:
            in_specs.append(pl.BlockSpec((tn, tk), idx(lambda i, j, k: (j, k))))
        else:
            in_specs.append(pl.BlockSpec((tk, tn), idx(lambda i, j, k: (k, j))))
        args.append(b)
    for arr, ro, co in rows:
        in_specs.append(pl.BlockSpec((tm, tn), idx(lambda i, j, k, ro=ro, co=co: (i + ro, j + co))))
        args.append(arr)
    for arr in vecs:
        in_specs.append(pl.BlockSpec((arr.shape[0], tn), idx(lambda i, j, k: (0, j))))
        args.append(arr)
    out_shape, out_specs = [], []
    for dtype, chunked in outs:
        if chunked:
            per = (N // N_CHIPS) // tn
            out_shape.append(jax.ShapeDtypeStruct((N_CHIPS, M, N // N_CHIPS), dtype))
            out_specs.append(pl.BlockSpec((None, tm, tn), idx(lambda i, j, k, per=per: (j // per, i, j % per))))
        else:
            out_shape.append(jax.ShapeDtypeStruct((M, N), dtype))
            out_specs.append(pl.BlockSpec((tm, tn), idx(lambda i, j, k: (i, j))))
    nacc = 1 if summed else npair
    scratch = [pltpu.VMEM((tm, tn), F32) for _ in range(nacc)] if nk > 1 else []
    nrow, nvec, nout = len(rows), len(vecs), len(outs)
    dims = (((0 if ta else 1,), (1 if tb else 0,)), ((), ()))

    def body(*refs):
        ab = refs[:2 * npair]
        row_refs = refs[2 * npair:2 * npair + nrow]
        vec_refs = refs[2 * npair + nrow:2 * npair + nrow + nvec]
        out_refs = refs[2 * npair + nrow + nvec:2 * npair + nrow + nvec + nout]
        acc_refs = refs[2 * npair + nrow + nvec + nout:]
        if n_outer:
            j, i, k = pl.program_id(0), pl.program_id(1), pl.program_id(2)
        else:
            i, j, k = pl.program_id(0), pl.program_id(1), pl.program_id(2)

        def part(p):
            av, bv = ab[2 * p][...], ab[2 * p + 1][...]
            if a_pro is not None:
                av = a_pro(av)
            if b_pro is not None:
                bv = b_pro(bv)
            return lax.dot_general(av, bv, dims, preferred_element_type=F32)

        def finish(accs):
            row_index = i * tm + lax.broadcasted_iota(jnp.int32, (tm, 1), 0)
            res = epi(accs, [r[...] for r in row_refs], [v[...] for v in vec_refs], row_index)
            for ref, val in zip(out_refs, res):
                ref[...] = val.astype(ref.dtype)

        parts = [part(p) for p in range(npair)]
        if summed:
            total = parts[0]
            for extra in parts[1:]:
                total = total + extra
            parts = [total]
        if nk == 1:
            finish(parts)
        else:
            @pl.when(k == 0)
            def _():
                for q in range(nacc):
                    acc_refs[q][...] = parts[q]

            @pl.when(jnp.logical_and(k > 0, k < nk - 1))
            def _():
                for q in range(nacc):
                    acc_refs[q][...] += parts[q]

            @pl.when(k == nk - 1)
            def _():
                finish([acc_refs[q][...] + parts[q] for q in range(nacc)])

    grid = (nn, nm, nk) if n_outer else (nm, nn, nk)
    return pl.pallas_call(body, name=name, grid=grid, in_specs=in_specs, out_specs=out_specs,
                          out_shape=out_shape, scratch_shapes=scratch, compiler_params=_cparams())(*args)


def _split3(v):
    v0 = v.astype(BF16)
    r1 = v - v0.astype(F32)
    v1 = r1.astype(BF16)
    v2 = (r1 - v1.astype(F32)).astype(BF16)
    return v0, v1, v2


def _mesh_pos():
    return lax.axis_index("x"), lax.axis_index("y"), lax.axis_index("c")


def _allgather_small(name, x):
    m, n = x.shape

    def body(x_ref, out_ref, send_sems, recv_sems, local_sem):
        xi, yi, ci = _mesh_pos()
        me, sibling = (xi, yi, ci), (xi, yi, 1 - ci)
        chips = [(1 - xi, yi), (xi, 1 - yi), (1 - xi, 1 - yi)]

        def rows(px, py, pc):
            return out_ref.at[pl.ds((4 * px + 2 * py + pc) * m, m), :]

        def copy(k, block, to, src=None):
            return pltpu.make_async_remote_copy(
                src_ref=rows(*block) if src is None else src, dst_ref=rows(*block),
                send_sem=send_sems.at[k], recv_sem=recv_sems.at[k], device_id=to, device_id_type=MESH)

        mine = pltpu.make_async_copy(x_ref, rows(*me), local_sem)
        mine.start()
        first = [copy(0, me, sibling, src=x_ref)]
        first += [copy(1 + j, me, (*chip, ci), src=x_ref) for j, chip in enumerate(chips)]
        for cp in first:
            cp.start()
        passed = [copy(4 + j, (*chip, ci), sibling) for j, chip in enumerate(chips)]
        for j, chip in enumerate(chips):
            copy(1 + j, (*chip, ci), me).wait_recv()
            passed[j].start()
        copy(0, sibling, me).wait_recv()
        for j, chip in enumerate(chips):
            copy(4 + j, (*chip, 1 - ci), me).wait_recv()
        for cp in first + passed:
            cp.wait_send()
        mine.wait()

    return pl.pallas_call(
        body, name=name, out_shape=jax.ShapeDtypeStruct((N_DEV * m, n), x.dtype),
        in_specs=[pl.BlockSpec(memory_space=pltpu.VMEM)], out_specs=pl.BlockSpec(memory_space=pltpu.VMEM),
        scratch_shapes=[pltpu.SemaphoreType.DMA((7,)), pltpu.SemaphoreType.DMA((7,)), pltpu.SemaphoreType.DMA],
        compiler_params=_cparams())(x)


_HBM = pl.BlockSpec(memory_space=pltpu.HBM)
_SEM = pl.BlockSpec(memory_space=pltpu.SEMAPHORE)
_ANY = pl.BlockSpec(memory_space=pl.ANY)
_EFFECT = pltpu.SideEffectType.DATAFLOW_SIDE_EFFECTING


def _in_hbm(v):
    return pltpu.with_memory_space_constraint(v, pltpu.HBM)


def _other_chips(xi, yi):
    return [(1 - xi, yi), (xi, 1 - yi), (1 - xi, 1 - yi)]


def _guarded(north_only, fn):
    if north_only:
        pl.when(lax.axis_index("c") == 1)(fn)
    else:
        fn()


def _split_copies(name, srcs, lands, after, pairs, north_only, ncopy):
    ns, nl = len(srcs), len(lands)
    dma = pltpu.SemaphoreType.DMA((ncopy,))
    thru = [pltpu.HBM(v.shape, v.dtype) for v in list(srcs) + list(lands)]

    def start_body(*refs):
        src_refs, land_refs = refs[:ns], refs[ns:ns + nl]
        descs = pairs(src_refs, land_refs, refs[ns + nl + 1], refs[ns + nl + 2])

        def go():
            for send, _ in descs:
                send.start()

        _guarded(north_only, go)
        refs[-1][...] = jnp.zeros_like(refs[-1])

    res = pl.pallas_call(
        start_body, name=name + "_start",
        out_shape=(dma, dma, *thru, jax.ShapeDtypeStruct((8, LANES), F32)),
        in_specs=[_HBM] * (ns + nl) + [_ANY],
        out_specs=(_SEM, _SEM, *([_HBM] * (ns + nl)), pl.BlockSpec(memory_space=pltpu.VMEM)),
        input_output_aliases={k: 2 + k for k in range(ns + nl)},
        compiler_params=_cparams(has_side_effects=_EFFECT),
    )(*[_in_hbm(v) for v in srcs], *[_in_hbm(v) for v in lands], after)
    send_sems, recv_sems, token = res[0], res[1], res[-1]
    carried = res[2:2 + ns + nl]

    def finish(after_work):
        def wait_body(*refs):
            src_refs, land_refs = refs[:ns], refs[ns:ns + nl]
            descs = pairs(src_refs, land_refs, refs[ns + nl], refs[ns + nl + 1])

            def go():
                for send, recv in descs:
                    send.wait_send()
                    recv.wait_recv()

            _guarded(north_only, go)

        out = pl.pallas_call(
            wait_body, name=name + "_wait", out_shape=tuple(thru),
            in_specs=[_HBM] * (ns + nl) + [_SEM, _SEM, _ANY], out_specs=tuple([_HBM] * (ns + nl)),
            input_output_aliases={k: k for k in range(ns + nl)},
            compiler_params=_cparams(has_side_effects=_EFFECT),
        )(*carried, send_sems, recv_sems, after_work)
        return list(out[ns:])

    return token, finish


def _cast_slot(name, w, chip_index):
    R, C = w.shape[1:]
    tr = _div(R, max(16, 524288 // C), mult=16)

    def body(chip_ref, w_ref, o_ref):
        o_ref[...] = w_ref[...].astype(BF16)

    return pl.pallas_call(
        body, name=name, out_shape=jax.ShapeDtypeStruct((N_CHIPS, R, C), BF16),
        grid_spec=pltpu.PrefetchScalarGridSpec(
            num_scalar_prefetch=1, grid=(R // tr,),
            in_specs=[pl.BlockSpec((None, tr, C), lambda i, chip_ref: (0, i, 0))],
            out_specs=pl.BlockSpec((None, tr, C), lambda i, chip_ref: (chip_ref[0], i, 0))),
        compiler_params=_cparams())(chip_index, w)


def _gather_split(name, lands, after):
    def pairs(src_refs, land_refs, send_sems, recv_sems):
        xi, yi, _ = _mesh_pos()
        mine = 2 * xi + yi
        out = []
        for a in range(len(lands)):
            for j, (px, py) in enumerate(_other_chips(xi, yi)):
                def to_slot(slot, a=a, j=j, px=px, py=py):
                    return pltpu.make_async_remote_copy(
                        src_ref=land_refs[a].at[mine], dst_ref=land_refs[a].at[slot], send_sem=send_sems.at[3 * a + j],
                        recv_sem=recv_sems.at[3 * a + j], device_id=(px, py, 1), device_id_type=MESH)
                out.append((to_slot(mine), to_slot(2 * px + py)))
        return out

    return _split_copies(name, [], lands, after, pairs, north_only=True, ncopy=3 * len(lands))


def _scatter_split(name, grads, after):
    lands = [lax.empty((3,) + g.shape[1:], g.dtype) for g in grads]

    def pairs(src_refs, land_refs, send_sems, recv_sems):
        xi, yi, ci = _mesh_pos()
        out = []
        for a in range(len(grads)):
            for j, (px, py) in enumerate(_other_chips(xi, yi)):
                cp = pltpu.make_async_remote_copy(
                    src_ref=src_refs[a].at[2 * px + py], dst_ref=land_refs[a].at[j], send_sem=send_sems.at[3 * a + j],
                    recv_sem=recv_sems.at[3 * a + j], device_id=(px, py, ci), device_id_type=MESH)
                out.append((cp, cp))
        return out

    return _split_copies(name, grads, lands, after, pairs, north_only=False, ncopy=3 * len(grads))


def _gather_finish(name, lands):
    na = len(lands)

    def body(*refs):
        outs = refs[na:2 * na]
        send_sems, recv_sems = refs[2 * na:]
        xi, yi, ci = _mesh_pos()
        passes = [pltpu.make_async_remote_copy(
            src_ref=outs[a].at[2 * px + py], dst_ref=outs[a].at[2 * px + py],
            send_sem=send_sems.at[a, j], recv_sem=recv_sems.at[a, j], device_id=(xi, yi, 0), device_id_type=MESH)
            for a in range(na) for j, (px, py) in enumerate(_other_chips(xi, yi))]

        @pl.when(ci == 1)
        def _():
            for cp in passes:
                cp.start()
            for cp in passes:
                cp.wait_send()

        @pl.when(ci == 0)
        def _():
            for cp in passes:
                cp.wait_recv()

    return pl.pallas_call(
        body, name=name, out_shape=[jax.ShapeDtypeStruct(v.shape, v.dtype) for v in lands],
        in_specs=[_ANY] * na, out_specs=[_ANY] * na,
        input_output_aliases={a: a for a in range(na)},
        scratch_shapes=[pltpu.SemaphoreType.DMA((na, 3)), pltpu.SemaphoreType.DMA((na, 3))],
        compiler_params=_cparams())(*lands)


def _swap_sibling(name, arrs):
    na = len(arrs)

    def body(*refs):
        ins, outs = refs[:na], refs[na:2 * na]
        send_sems, recv_sems = refs[2 * na:]
        xi, yi, ci = _mesh_pos()
        copies = [pltpu.make_async_remote_copy(
            src_ref=ins[a], dst_ref=outs[a], send_sem=send_sems.at[a], recv_sem=recv_sems.at[a],
            device_id=(xi, yi, 1 - ci), device_id_type=MESH) for a in range(na)]
        for cp in copies:
            cp.start()
        for cp in copies:
            cp.wait()

    return pl.pallas_call(
        body, name=name,
        out_shape=[jax.ShapeDtypeStruct(g.shape, g.dtype) for g in arrs],
        in_specs=[_ANY] * na, out_specs=[_ANY] * na,
        scratch_shapes=[pltpu.SemaphoreType.DMA((na,)), pltpu.SemaphoreType.DMA((na,))],
        compiler_params=_cparams())(*arrs)


def _attn_tiles(L, Lc, D):
    tq = min(256, Lc)
    return tq, L // tq, Lc // tq, D // HEAD_DIM // Q_PER_KV


def _attn_probs(q, k):
    s = lax.dot_general(q, k, (((1,), (1,)), ((), ())), preferred_element_type=F32) * (HEAD_DIM ** -0.5)
    e = jnp.exp(s - jnp.max(s, axis=-1, keepdims=True))
    return e * (1.0 / jnp.sum(e, axis=-1, keepdims=True))


def _attn_fwd(qr, kr, v, L, Lc, D):
    T = L + Lc
    tq, nq, qoff, nkv = _attn_tiles(L, Lc, D)

    def body(q_ref, k_ref, v_ref, o_ref):
        p = _attn_probs(q_ref[...], k_ref[...])
        o_ref[...] = jnp.dot(p.astype(BF16), v_ref[...], preferred_element_type=F32).astype(o_ref.dtype)

    kv_spec = pl.BlockSpec((T, HEAD_DIM), lambda h, r, q: (0, h))
    return pl.pallas_call(
        body, name="attn_fwd", grid=(nkv, Q_PER_KV, nq),
        in_specs=[pl.BlockSpec((tq, HEAD_DIM), lambda h, r, q: (q + qoff, h * Q_PER_KV + r)), kv_spec, kv_spec],
        out_specs=pl.BlockSpec((tq, HEAD_DIM), lambda h, r, q: (q, h * Q_PER_KV + r)),
        out_shape=jax.ShapeDtypeStruct((L, D), BF16), compiler_params=_cparams())(qr, kr, v)


def _attn_bwd(qr, kr, v, do, L, Lc, D):
    T = L + Lc
    tq, nq, qoff, nkv = _attn_tiles(L, Lc, D)
    scale = HEAD_DIM ** -0.5

    def body(q_ref, k_ref, v_ref, do_ref, dq_ref, dk_ref, dv_ref):
        first = jnp.logical_and(pl.program_id(1) == 0, pl.program_id(2) == 0)
        q, k, dout = q_ref[...], k_ref[...], do_ref[...]
        p = _attn_probs(q, k)
        dp = lax.dot_general(dout, v_ref[...], (((1,), (1,)), ((), ())), preferred_element_type=F32)
        ds = (p * (dp - jnp.sum(p * dp, axis=-1, keepdims=True)) * scale).astype(BF16)
        dq_ref[...] = jnp.dot(ds, k, preferred_element_type=F32)
        dk = lax.dot_general(ds, q, (((0,), (0,)), ((), ())), preferred_element_type=F32)
        dv = lax.dot_general(p.astype(BF16), dout, (((0,), (0,)), ((), ())), preferred_element_type=F32)

        @pl.when(first)
        def _():
            dk_ref[...] = dk
            dv_ref[...] = dv

        @pl.when(jnp.logical_not(first))
        def _():
            dk_ref[...] += dk
            dv_ref[...] += dv

    kv_spec = pl.BlockSpec((T, HEAD_DIM), lambda h, r, q: (0, h))
    q_spec = pl.BlockSpec((tq, HEAD_DIM), lambda h, r, q: (q + qoff, h * Q_PER_KV + r))
    o_spec = pl.BlockSpec((tq, HEAD_DIM), lambda h, r, q: (q, h * Q_PER_KV + r))
    return pl.pallas_call(
        body, name="attn_bwd", grid=(nkv, Q_PER_KV, nq),
        in_specs=[q_spec, kv_spec, kv_spec, o_spec], out_specs=[o_spec, kv_spec, kv_spec],
        out_shape=[jax.ShapeDtypeStruct((L, D), F32), jax.ShapeDtypeStruct((T, D // Q_PER_KV), F32),
                   jax.ShapeDtypeStruct((T, D // Q_PER_KV), F32)],
        compiler_params=_cparams())(qr, kr, v, do)


def _scan_tile(xr, xi, pw_re, pw_im, lanes, reverse):
    tt = xr.shape[0]
    rows = lax.broadcasted_iota(jnp.int32, (tt, 1), 0)
    for k in range(tt.bit_length() - 1):
        d = 1 << k
        shift = tt - d if reverse else d
        keep = rows < tt - d if reverse else rows >= d
        sr = jnp.where(keep, pltpu.roll(xr, shift, 0), 0.0)
        si = jnp.where(keep, pltpu.roll(xi, shift, 0), 0.0)
        pr, pi = pw_re[k:k + 1, lanes], pw_im[k:k + 1, lanes]
        xr, xi = xr + (pr * sr - pi * si), xi + (pr * si + pi * sr)
    return xr, xi


def _scan_init(lr, li, pw_re, pw_im, w_re, w_im, carry_re, carry_im, nslab, reverse):
    tt = w_re.shape[0]
    carry_re[...] = jnp.zeros_like(carry_re)
    carry_im[...] = jnp.zeros_like(carry_im)
    pr, pi = lr, li
    for k in range(tt.bit_length() - 1):
        pw_re[k:k + 1, :] = pr
        pw_im[k:k + 1, :] = pi
        pr, pi = pr * pr - pi * pi, 2.0 * pr * pi
    rows = lax.broadcasted_iota(jnp.int32, (tt, 1), 0)
    edge = rows == (tt - 1 if reverse else 0)
    for j in range(nslab):
        lanes = slice(j * SLAB_ST, (j + 1) * SLAB_ST)
        wr, wi = _scan_tile(jnp.where(edge, lr[:, lanes], 0.0), jnp.where(edge, li[:, lanes], 0.0),
                            pw_re, pw_im, lanes, reverse)
        w_re[:, lanes] = wr
        w_im[:, lanes] = wi


def _ssm_tiles(T, Lc):
    tt = min(128, Lc)
    return tt, T // tt, Lc // tt


def _ssm_fwd(name, u, bbd, cbd_re, cbd_im, lam_re, lam_im, coef_re, coef_im, Lc, reverse):
    T, W = u.shape
    nslab = W // SLAB_CH
    NS = nslab * SLAB_ST
    tt, nt, nc = _ssm_tiles(T, Lc)
    if reverse:
        tile = lambda s: jnp.where(s < nc, nc - 1 - s, nt - 1 - (s - nc))
    else:
        tile = lambda s: s

    def body(u_ref, b_ref, cr_ref, ci_ref, lr_ref, li_ref, kr_ref, ki_ref, hr_ref, hi_ref, y_ref,
             pw_re, pw_im, w_re, w_im, carry_re, carry_im):
        @pl.when(pl.program_id(0) == 0)
        def _():
            _scan_init(lr_ref[...], li_ref[...], pw_re, pw_im, w_re, w_im, carry_re, carry_im, nslab, reverse)

        edge_row = 0 if reverse else tt - 1
        for j in range(nslab):
            lanes = slice(j * SLAB_ST, (j + 1) * SLAB_ST)
            bu = jnp.dot(u_ref[:, j * SLAB_CH:(j + 1) * SLAB_CH], b_ref[j], preferred_element_type=F32)
            br, bi = bu[:, :SLAB_ST], bu[:, SLAB_ST:]
            kr, ki = kr_ref[:, lanes], ki_ref[:, lanes]
            hr, hi = _scan_tile(kr * br - ki * bi, kr * bi + ki * br, pw_re, pw_im, lanes, reverse)
            car, cai = carry_re[:, lanes], carry_im[:, lanes]
            wr, wi = w_re[:, lanes], w_im[:, lanes]
            hr = hr + (wr * car - wi * cai)
            hi = hi + (wr * cai + wi * car)
            carry_re[:, lanes] = hr[edge_row:edge_row + 1, :]
            carry_im[:, lanes] = hi[edge_row:edge_row + 1, :]
            hrb, hib = hr.astype(BF16), hi.astype(BF16)
            hr_ref[:, lanes] = hrb
            hi_ref[:, lanes] = hib
            y_ref[:, j * SLAB_CH:(j + 1) * SLAB_CH] = (
                jnp.dot(hrb, cr_ref[j], preferred_element_type=F32)
                - jnp.dot(hib, ci_ref[j], preferred_element_type=F32))

    whole3 = lambda arr: pl.BlockSpec(arr.shape, lambda s: (0, 0, 0))
    vec = pl.BlockSpec((1, NS), lambda s: (0, 0))
    return pl.pallas_call(
        body, name=name, grid=(nt,),
        in_specs=[pl.BlockSpec((tt, W), lambda s: (tile(s), 0)), whole3(bbd), whole3(cbd_re), whole3(cbd_im),
                  vec, vec, vec, vec],
        out_specs=[pl.BlockSpec((tt, NS), lambda s: (tile(s), 0)), pl.BlockSpec((tt, NS), lambda s: (tile(s), 0)),
                   pl.BlockSpec((tt, W), lambda s: (tile(s), 0))],
        out_shape=[jax.ShapeDtypeStruct((T, NS), BF16), jax.ShapeDtypeStruct((T, NS), BF16),
                   jax.ShapeDtypeStruct((T, W), F32)],
        scratch_shapes=[pltpu.VMEM((8, NS), F32), pltpu.VMEM((8, NS), F32), pltpu.VMEM((tt, NS), F32),
                        pltpu.VMEM((tt, NS), F32), pltpu.VMEM((1, NS), F32), pltpu.VMEM((1, NS), F32)],
        compiler_params=_cparams())(u, bbd, cbd_re, cbd_im, lam_re, lam_im, coef_re, coef_im)


def _ssm_bwd(name, dy, h_re, h_im, u, bbd, bbdt_re, bbdt_im, cbdt_re, cbdt_im, lam_re, lam_im,
             coef_re, coef_im, Lc, reverse):
    T, W = u.shape
    nslab = W // SLAB_CH
    NS = nslab * SLAB_ST
    tt, nt, nc = _ssm_tiles(T, Lc)
    adj_reverse = not reverse
    if reverse:
        tile = lambda s: jnp.where(s < nt - nc, nc + s, s - (nt - nc))
    else:
        tile = lambda s: nt - 1 - s

    def body(dy_ref, hr_ref, hi_ref, u_ref, b_ref, btr_ref, bti_ref, ctr_ref, cti_ref, lr_ref, li_ref,
             kr_ref, ki_ref, du_ref, dlr_ref, dli_ref, dkr_ref, dki_ref, db_ref, dcr_ref, dci_ref,
             pw_re, pw_im, w_re, w_im, carry_re, carry_im):
        @pl.when(pl.program_id(0) == 0)
        def _():
            _scan_init(lr_ref[...], -li_ref[...], pw_re, pw_im, w_re, w_im, carry_re, carry_im, nslab, adj_reverse)
            for ref in (dlr_ref, dli_ref, dkr_ref, dki_ref, db_ref, dcr_ref, dci_ref):
                ref[...] = jnp.zeros_like(ref)

        rows = lax.broadcasted_iota(jnp.int32, (tt, 1), 0)
        edge_row = 0 if adj_reverse else tt - 1
        far_row = tt - 1 if adj_reverse else 0
        tn_dims = (((0,), (0,)), ((), ()))
        for j in range(nslab):
            lanes = slice(j * SLAB_ST, (j + 1) * SLAB_ST)
            chans = slice(j * SLAB_CH, (j + 1) * SLAB_CH)
            dys, us = dy_ref[:, chans], u_ref[:, chans]
            er = jnp.dot(dys, ctr_ref[j], preferred_element_type=F32)
            ei = -jnp.dot(dys, cti_ref[j], preferred_element_type=F32)
            ar, ai = _scan_tile(er, ei, pw_re, pw_im, lanes, adj_reverse)
            car, cai = carry_re[:, lanes], carry_im[:, lanes]
            wr, wi = w_re[:, lanes], w_im[:, lanes]
            ar = ar + (wr * car - wi * cai)
            ai = ai + (wr * cai + wi * car)
            shift = tt - 1 if adj_reverse else 1
            nr = jnp.where(rows == far_row, car, pltpu.roll(ar, shift, 0))
            ni = jnp.where(rows == far_row, cai, pltpu.roll(ai, shift, 0))
            carry_re[:, lanes] = ar[edge_row:edge_row + 1, :]
            carry_im[:, lanes] = ai[edge_row:edge_row + 1, :]
            hrb, hib = hr_ref[:, lanes], hi_ref[:, lanes]
            hr, hi = hrb.astype(F32), hib.astype(F32)
            dlr_ref[:, lanes] += jnp.sum(nr * hr + ni * hi, axis=0, keepdims=True)
            dli_ref[:, lanes] += jnp.sum(ni * hr - nr * hi, axis=0, keepdims=True)
            bu = jnp.dot(us, b_ref[j], preferred_element_type=F32)
            br, bi = bu[:, :SLAB_ST], bu[:, SLAB_ST:]
            dkr_ref[:, lanes] += jnp.sum(ar * br + ai * bi, axis=0, keepdims=True)
            dki_ref[:, lanes] += jnp.sum(ai * br - ar * bi, axis=0, keepdims=True)
            kr, ki = kr_ref[:, lanes], ki_ref[:, lanes]
            dbr = (ar * kr + ai * ki).astype(BF16)
            dbi = (ai * kr - ar * ki).astype(BF16)
            du_ref[:, chans] = (jnp.dot(dbr, btr_ref[j], preferred_element_type=F32)
                                + jnp.dot(dbi, bti_ref[j], preferred_element_type=F32))
            db_ref[j, :, :SLAB_ST] += lax.dot_general(us, dbr, tn_dims, preferred_element_type=F32)
            db_ref[j, :, SLAB_ST:] += lax.dot_general(us, dbi, tn_dims, preferred_element_type=F32)
            dcr_ref[j] += lax.dot_general(hrb, dys, tn_dims, preferred_element_type=F32)
            dci_ref[j] -= lax.dot_general(hib, dys, tn_dims, preferred_element_type=F32)

    whole3 = lambda arr: pl.BlockSpec(arr.shape, lambda s: (0, 0, 0))
    vec = pl.BlockSpec((1, NS), lambda s: (0, 0))
    row_w = pl.BlockSpec((tt, W), lambda s: (tile(s), 0))
    row_s = pl.BlockSpec((tt, NS), lambda s: (tile(s), 0))
    return pl.pallas_call(
        body, name=name, grid=(nt,),
        in_specs=[row_w, row_s, row_s, row_w, whole3(bbd), whole3(bbdt_re), whole3(bbdt_im), whole3(cbdt_re),
                  whole3(cbdt_im), vec, vec, vec, vec],
        out_specs=[row_w, vec, vec, vec, vec, whole3(bbd), whole3(bbdt_re), whole3(bbdt_re)],
        out_shape=[jax.ShapeDtypeStruct((T, W), F32)] + [jax.ShapeDtypeStruct((1, NS), F32)] * 4
        + [jax.ShapeDtypeStruct(bbd.shape, F32), jax.ShapeDtypeStruct(bbdt_re.shape, F32),
           jax.ShapeDtypeStruct(bbdt_re.shape, F32)],
        scratch_shapes=[pltpu.VMEM((8, NS), F32), pltpu.VMEM((8, NS), F32), pltpu.VMEM((tt, NS), F32),
                        pltpu.VMEM((tt, NS), F32), pltpu.VMEM((1, NS), F32), pltpu.VMEM((1, NS), F32)],
        compiler_params=_cparams())(dy, h_re, h_im, u, bbd, bbdt_re, bbdt_im, cbdt_re, cbdt_im,
                                    lam_re, lam_im, coef_re, coef_im)


def _zoh_math(a_re, a_im, log_dt):
    dt = jnp.exp(log_dt)
    mag = jnp.exp(a_re * dt)
    lb_re = mag * jnp.cos(a_im * dt)
    lb_im = mag * jnp.sin(a_im * dt)
    den = a_re * a_re + a_im * a_im
    coef_re = ((lb_re - 1.0) * a_re + lb_im * a_im) / den
    coef_im = (lb_im * a_re - (lb_re - 1.0) * a_im) / den
    return lb_re, lb_im, coef_re, coef_im


def _zoh_fwd(a_re, a_im, log_dt):
    def body(ar, ai, ld, o0, o1, o2, o3):
        for ref, val in zip((o0, o1, o2, o3), _zoh_math(ar[...], ai[...], ld[...])):
            ref[...] = val

    return pl.pallas_call(body, name="zoh_fwd", out_shape=[jax.ShapeDtypeStruct(a_re.shape, F32)] * 4,
                          compiler_params=_cparams())(a_re, a_im, log_dt)


def _zoh_bwd(a_re, a_im, log_dt, cots):
    def body(ar, ai, ld, c0, c1, c2, c3, o0, o1, o2):
        _, vjp = jax.vjp(_zoh_math, ar[...], ai[...], ld[...])
        for ref, val in zip((o0, o1, o2), vjp((c0[...], c1[...], c2[...], c3[...]))):
            ref[...] = val

    return pl.pallas_call(
        body, name="zoh_bwd",
        out_shape=[jax.ShapeDtypeStruct(a_re.shape, F32), jax.ShapeDtypeStruct(a_re.shape, F32),
                   jax.ShapeDtypeStruct(log_dt.shape, F32)],
        compiler_params=_cparams())(a_re, a_im, log_dt, *cots)


def _outer_sum(acts, cots):
    D, N = acts.shape[1], cots.shape[1]
    tm, tn = _div(D, 512), _div(N, 1152)
    dims = (((0,), (0,)), ((), ()))

    def body(a_ref, b_ref, o_ref):
        a = a_ref[...]
        aa = _split3(a * _sigmoid(a))
        bb = _split3(b_ref[...])
        acc = None
        for ia in range(3):
            for ib in range(3 - ia):
                t = lax.dot_general(aa[ia], bb[ib], dims, preferred_element_type=F32)
                acc = t if acc is None else acc + t
        o_ref[...] = acc

    return pl.pallas_call(
        body, name="mod_dw", grid=(D // tm, N // tn),
        in_specs=[pl.BlockSpec((16, tm), lambda i, j: (0, i)), pl.BlockSpec((16, tn), lambda i, j: (0, j))],
        out_specs=pl.BlockSpec((tm, tn), lambda i, j: (i, j)),
        out_shape=jax.ShapeDtypeStruct((D, N), F32), compiler_params=_cparams())(acts, cots)


def _adamw_math(w, g, m, v):
    m = ADAM_B1 * m + (1.0 - ADAM_B1) * g
    v = ADAM_B2 * v + (1.0 - ADAM_B2) * (g * g)
    m_hat = m / (1.0 - ADAM_B1 ** ADAM_STEP)
    v_hat = v / (1.0 - ADAM_B2 ** ADAM_STEP)
    delta = -ADAM_LR * (m_hat / (jnp.sqrt(v_hat) + ADAM_EPS) + ADAM_WD * w)
    return delta, m, v


def _adamw(name, w, m, v, gparts):
    R, C = w.shape[-2:]
    kind = 'row1' if w.ndim == 3 else 'row'
    tr = _div(R, max(8, 262144 // C), mult=8)

    def fn(i, wv, mv, vv, *gs):
        g = gs[0]
        for extra in gs[1:]:
            g = g + extra
        return (g,) + _adamw_math(wv, g, mv, vv)

    return _rowk(name, fn, R, tr, [(w, kind), (m, kind), (v, kind)] + [(g, 'row') for g in gparts],
                 [(w.shape, F32, kind)] * 4)


def _pack(pieces, rows_mult=8):
    flat = jnp.concatenate([p.reshape(-1).astype(F32) for p in pieces])
    unit = rows_mult * PACK_W
    total = -(-flat.shape[0] // unit) * unit
    return jnp.pad(flat, (0, total - flat.shape[0])).reshape(total // PACK_W, PACK_W)


def _unpack(buf, shapes):
    flat = buf.reshape(-1)
    out, off = [], 0
    for s in shapes:
        n = math.prod(s)
        out.append(flat[off:off + n].reshape(s))
        off += n
    return out


def _bd_expand(t):
    S, g, a, b = t.shape
    eye = jnp.eye(g, dtype=t.dtype)
    return (t[:, :, :, None, :] * eye[None, :, None, :, None]).reshape(S, g * a, g * b)


def _bd_extract(t, a, b):
    S = t.shape[0]
    g = t.shape[1] // a
    eye = jnp.eye(g, dtype=t.dtype)
    return jnp.sum(t.reshape(S, g, a, g, b) * eye[None, :, None, :, None], axis=3)


def _rope_tables(L, Lc):
    rows = L // GRID_W
    row_ids = jnp.broadcast_to(jnp.arange(rows)[:, None], (rows, GRID_W)).reshape(-1).astype(F32)
    col_ids = jnp.broadcast_to(jnp.arange(GRID_W)[None, :], (rows, GRID_W)).reshape(-1).astype(F32)
    quarter = HEAD_DIM // 4
    inv_freq = ROPE_THETA ** (-jnp.arange(quarter, dtype=F32) / quarter)
    ang_r = row_ids[:, None] * inv_freq
    ang_c = col_ids[:, None] * inv_freq
    cos = jnp.concatenate([jnp.cos(ang_r), jnp.cos(ang_r), jnp.cos(ang_c), jnp.cos(ang_c)], axis=1)
    sin = jnp.concatenate([-jnp.sin(ang_r), jnp.sin(ang_r), -jnp.sin(ang_c), jnp.sin(ang_c)], axis=1)
    cos = jnp.concatenate([jnp.ones((Lc, HEAD_DIM), F32), cos], axis=0)
    sin = jnp.concatenate([jnp.zeros((Lc, HEAD_DIM), F32), sin], axis=0)
    return cos, sin


def _rot(v):
    lane = lax.broadcasted_iota(jnp.int32, (1, HEAD_DIM), 1)
    first = (lane % (HEAD_DIM // 2)) < (HEAD_DIM // 4)
    return jnp.where(first, pltpu.roll(v, HEAD_DIM - HEAD_DIM // 4, 1), pltpu.roll(v, HEAD_DIM // 4, 1))


def _head_norm(xh, g):
    return xh * lax.rsqrt(jnp.mean(xh * xh, axis=-1, keepdims=True) + NORM_EPS) * g


def _norm_mod(xv, g, sh, sc):
    r = lax.rsqrt(jnp.mean(xv * xv, axis=-1, keepdims=True) + NORM_EPS)
    return (xv * r) * g * (1.0 + sc) + sh


def kernel(x, c, ctx, c_ctx, w_mod, b_mod, norm_g, w_ffn1_gate, w_ffn1_up, w_ffn1_down, w_in, q_norm_g, k_norm_g, ssm_a_re, ssm_a_im, ssm_log_dt, ssm_b_re, ssm_b_im, ssm_c_re, ssm_c_im, ssm_d, w_glu, b_glu, w_br_attn, w_br_ssm, w_out, w_ffn2_gate, w_ffn2_up, w_ffn2_down, loss_target, m_c_ctx, m_w_mod, m_b_mod, m_norm_g, m_w_ffn1_gate, m_w_ffn1_up, m_w_ffn1_down, m_w_in, m_q_norm_g, m_k_norm_g, m_ssm_a_re, m_ssm_a_im, m_ssm_log_dt, m_ssm_b_re, m_ssm_b_im, m_ssm_c_re, m_ssm_c_im, m_ssm_d, m_w_glu, m_b_glu, m_w_br_attn, m_w_br_ssm, m_w_out, m_w_ffn2_gate, m_w_ffn2_up, m_w_ffn2_down, v_c_ctx, v_w_mod, v_b_mod, v_norm_g, v_w_ffn1_gate, v_w_ffn1_up, v_w_ffn1_down, v_w_in, v_q_norm_g, v_k_norm_g, v_ssm_a_re, v_ssm_a_im, v_ssm_log_dt, v_ssm_b_re, v_ssm_b_im, v_ssm_c_re, v_ssm_c_im, v_ssm_d, v_w_glu, v_b_glu, v_w_br_attn, v_w_br_ssm, v_w_out, v_w_ffn2_gate, v_w_ffn2_up, v_w_ffn2_down):
    A = dict(locals())
    xi, yi, ci = _mesh_pos()
    chip = 2 * xi + yi
    me = 4 * xi + 2 * yi + ci
    L, D = x.shape[1], x.shape[2]
    Lc = ctx.shape[1]
    T = L + Lc
    F4 = w_ffn1_gate.shape[2]
    F = N_CHIPS * F4
    W, KV, Dq = D // 2, D // 4, D // 4
    G = W // SSM_GROUP
    P, E = SSM_STATE, SSM_GROUP
    NS = G * P
    nslab = W // SLAB_CH
    tr = min(256, Lc)
    ncr = Lc // tr
    assert L % tr == 0 and Lc % tr == 0 and W % SLAB_CH == 0 and D % (4 * LANES) == 0

    def sel(i, v):
        return v if v.shape[0] == 1 else jnp.where(i < ncr, v[0:1], v[1:2])

    def put(i, v, nrow):
        if nrow == 1:
            return v
        which = (i >= ncr).astype(jnp.int32)
        r2 = lax.broadcasted_iota(jnp.int32, (nrow, 1), 0)
        return jnp.where(r2 == which, jnp.broadcast_to(v, (nrow, v.shape[1])), 0.0)

    ident = lambda accs, rows, vecs, ri: [accs[0]]

    NM = w_mod.shape[2]
    first = jnp.zeros((8, D), F32).at[0].set(c[0]).at[1:4, :Dq].set(norm_g[0])
    g0 = _allgather_small("gather_c", first).reshape(N_CHIPS, 2, 8, D)
    c_all = g0[:, :, 0].reshape(N_DEV, D)
    ng = jnp.transpose(g0[:, 0, 1:4, :Dq], (1, 0, 2)).reshape(3, D)
    acts = jnp.concatenate([c_all, c_ctx[None], jnp.zeros((7, D), F32)], axis=0)
    wm = w_mod[0]
    b_shard = lax.dynamic_slice(b_mod[0], (chip * NM,), (NM,))[None]
    silu_bf = lambda a: (a * _sigmoid(a)).astype(BF16)
    to_bf = lambda b: b.astype(BF16)
    mod_part = _mm("mod_fwd", [(acts, wm, D)], 16, NM, tm=16, tn=_div(NM, 1152),
                   epi=lambda accs, rows, vecs, ri: [accs[0] + vecs[0]], outs=[(F32, False)],
                   vecs=[b_shard], a_pro=silu_bf, b_pro=to_bf)[0]
    mg = _allgather_small("gather_mod", mod_part).reshape(N_CHIPS, 2, 16, NM)[:, 0]
    mod_all = jnp.transpose(mg, (1, 0, 2)).reshape(16, N_CHIPS * NM)
    mod_x = lax.dynamic_slice(mod_all, (me, 0), (1, 9 * D))
    mod_c = jnp.where(jnp.arange(9 * D)[None] < 5 * D, mod_all[8:9], 0.0)
    modv = jnp.concatenate([mod_c, mod_x], axis=0)
    mv = lambda k: modv[:, k * D:(k + 1) * D]
    sh1, sc1, g1, sh2, sc2 = mv(0), mv(1), mv(2), mv(3), mv(4)
    g2, sh3, sc3, g3 = mv(5)[1:2], mv(6)[1:2], mv(7)[1:2], mv(8)[1:2]

    big = ['w_ffn1_gate', 'w_ffn1_up', 'w_ffn1_down', 'w_ffn2_gate', 'w_ffn2_up', 'w_ffn2_down',
           'w_in', 'w_glu', 'w_br_attn', 'w_br_ssm', 'w_out']
    row_sharded = {'w_ffn1_down', 'w_ffn2_down', 'w_glu', 'w_br_attn', 'w_out'}
    groups = [big[0:2], big[2:3], big[6:7], big[7:11], big[3:6]]
    chip_index = jnp.reshape(chip, (1,)).astype(jnp.int32)
    slots = {n: _cast_slot("cast_" + n, A[n], chip_index) for n in big}
    tok, gather_finish = modv, []
    for gi, names in enumerate(groups):
        tok, fin = _gather_split("gather_w%d" % gi, [slots[n] for n in names], tok)
        gather_finish.append(fin)
    ng = ng + tok[0:1, 0:1]
    Wt = {}

    def weights_ready(gi, after_work):
        names = groups[gi]
        lands = gather_finish[gi](after_work)
        full = _gather_finish("gather_w%d_pass" % gi, lands)
        for n, gw in zip(names, full):
            Wt[n] = gw.reshape(N_CHIPS * gw.shape[1], gw.shape[2]) if n in row_sharded else gw

    a_re2, a_im2 = ssm_a_re[0].reshape(2 * G, P), ssm_a_im[0].reshape(2 * G, P)
    ldt2 = ssm_log_dt[0].reshape(2 * G, 1)
    zoh = _zoh_fwd(a_re2, a_im2, ldt2)
    lam_re, lam_im, coef_re, coef_im = [[z[d * G:(d + 1) * G].reshape(1, NS) for d in range(2)] for z in zoh]
    bd_b = lambda b: _bd_expand(jnp.transpose(b, (0, 2, 1)).reshape(nslab, SLAB_GROUPS, E, P))
    bd_c = lambda cc: _bd_expand(jnp.transpose(cc, (0, 2, 1)).reshape(nslab, SLAB_GROUPS, P, E))
    bbd, bbdt_re, bbdt_im, cbd_re, cbd_im, cbdt_re, cbdt_im = [], [], [], [], [], [], []
    for d in range(2):
        br_, bi_ = bd_b(ssm_b_re[0, d]).astype(BF16), bd_b(ssm_b_im[0, d]).astype(BF16)
        cr_, ci_ = bd_c(ssm_c_re[0, d]).astype(BF16), bd_c(ssm_c_im[0, d]).astype(BF16)
        bbd.append(jnp.concatenate([br_, bi_], axis=2))
        bbdt_re.append(jnp.transpose(br_, (0, 2, 1)))
        bbdt_im.append(jnp.transpose(bi_, (0, 2, 1)))
        cbd_re.append(cr_)
        cbd_im.append(ci_)
        cbdt_re.append(jnp.transpose(cr_, (0, 2, 1)))
        cbdt_im.append(jnp.transpose(ci_, (0, 2, 1)))
    cos_t, sin_t = _rope_tables(L, Lc)
    qg, kg = q_norm_g, k_norm_g
    small = ['c_ctx', 'b_mod', 'norm_g', 'q_norm_g', 'k_norm_g', 'ssm_a_re', 'ssm_a_im', 'ssm_log_dt', 'ssm_b_re',
             'ssm_b_im', 'ssm_c_re', 'ssm_c_im', 'ssm_d', 'b_glu']
    packs_wmv = [_pack([A[pre + n] for n in small]) for pre in ('', 'm_', 'v_')]
    prepared = packs_wmv + [cos_t, sin_t, coef_im[0], coef_im[1]] + [
        t[d][0] for t in (bbd, bbdt_re, bbdt_im, cbd_re, cbd_im, cbdt_re, cbdt_im) for d in range(2)]
    weights_ready(0, tok + sum(t[0:1, 0:1].astype(F32) for t in prepared))

    def norm_mod(name, xv, g, sh, sc):
        rows = xv.shape[0]
        return _rowk(name, lambda i, xt, gt, sht, sct: [_norm_mod(xt, gt, sel(i, sht), sel(i, sct))],
                     rows, tr, [(xv, 'row'), (g, 'vec'), (sh, 'vec'), (sc, 'vec')], [((rows, D), BF16, 'row')])[0]

    def swiglu_epi(accs, rows, vecs, ri):
        a_, b_ = accs
        return [a_, b_, a_ * _sigmoid(a_) * b_]

    def res_epi(coef):
        def epi(accs, rows, vecs, ri):
            gate = vecs[0]
            if gate.shape[0] == 2:
                gate = jnp.where(ri < Lc, gate[0:1], gate[1:2])
            return [accs[0], rows[0] + (coef * gate) * accs[0]]
        return epi

    def ffn_fwd(tag, h, xres, gate, down_ready=None):
        rows = h.shape[0]
        a_, b_, s_ = _mm(tag + "_up", [(h, Wt['w_' + tag + '_gate'], D), (h, Wt['w_' + tag + '_up'], D)], rows, F,
                         tm=_div(rows, 256), tn=F4, epi=swiglu_epi, outs=[(F32, False), (F32, False), (BF16, False)])
        if down_ready is not None:
            down_ready(s_)
        f_, xo = _mm(tag + "_down", [(s_, Wt['w_' + tag + '_down'], F)], rows, D, tm=_div(rows, 512),
                     tn=_div(D, 1024), nk=N_CHIPS, epi=res_epi(0.5), outs=[(F32, False), (F32, False)],
                     rows=[(xres, 0, 0)], vecs=[gate])
        return a_, b_, s_, f_, xo

    xc = jnp.concatenate([ctx[0], x[0]], axis=0)
    h1 = norm_mod("norm1", xc, ng[0:1], sh1, sc1)
    a1, b1, s1, f1, x1 = ffn_fwd("ffn1", h1, xc, g1, down_ready=lambda s_: weights_ready(1, s_))
    weights_ready(2, x1)
    h2 = norm_mod("norm2", x1, ng[1:2], sh2, sc2)
    proj = _mm("in_proj", [(h2, Wt['w_in'], D)], T, 4 * D, tm=_div(T, 768), tn=_div(D, 1024), epi=ident,
               outs=[(F32, False)])[0]
    nh, nkvh = D // HEAD_DIM, KV // HEAD_DIM

    def prep_fn(i, kt, vt, ut, qt, qgt, kgt, ct, st):
        qs = [_head_norm(qt[:, h * HEAD_DIM:(h + 1) * HEAD_DIM], qgt) for h in range(nh)]
        ks = [_head_norm(kt[:, h * HEAD_DIM:(h + 1) * HEAD_DIM], kgt) for h in range(nkvh)]
        qs = [v * ct + _rot(v) * st for v in qs]
        ks = [v * ct + _rot(v) * st for v in ks]
        return [jnp.concatenate(qs, axis=1), jnp.concatenate(ks, axis=1), vt, ut]

    qr, kr, vb, ub = _rowk(
        "qk_prep", prep_fn, T, tr,
        [(proj, ('col', KV, 0)), (proj, ('col', KV, 1)), (proj, ('col', W, 1)), (proj, ('col', D, 1)),
         (qg, 'vec'), (kg, 'vec'), (cos_t, 'row'), (sin_t, 'row')],
        [((T, D), BF16, 'row'), ((T, KV), BF16, 'row'), ((T, KV), BF16, 'row'), ((T, W), BF16, 'row')])
    attn = _attn_fwd(qr, kr, vb, L, Lc, D)
    weights_ready(3, attn)
    hs_re, hs_im, ys = [], [], []
    for d in range(2):
        hr_, hi_, y_ = _ssm_fwd("ssm_fwd%d" % d, ub, bbd[d], cbd_re[d], cbd_im[d], lam_re[d], lam_im[d],
                                coef_re[d], coef_im[d], Lc, reverse=bool(d))
        hs_re.append(hr_)
        hs_im.append(hi_)
        ys.append(y_)

    def ssm_out_fn(i, y0, y1, ut, dt):
        pre = dt * ut + y0 + y1
        yg_ = _gelu(pre)
        return [pre, yg_, yg_]

    ssm_pre, yg, ygb = _rowk(
        "ssm_out", ssm_out_fn, L, tr,
        [(ys[0], 'orow'), (ys[1], 'orow'), (proj, ('ocol', W, 1)), (ssm_d, 'vec')],
        [((L, W), F32, 'row'), ((L, W), F32, 'row'), ((L, W), BF16, 'row')], nc=ncr)

    def glu_epi(accs, rows, vecs, ri):
        z_ = accs[0] + vecs[0]
        return [z_, rows[0] * _sigmoid(z_)]

    zglu, y2 = _mm("glu", [(ygb, Wt['w_glu'], W)], L, W, tm=_div(L, 512), tn=_div(W, 512), epi=glu_epi,
                   outs=[(F32, False), (BF16, False)], rows=[(yg, 0, 0)], vecs=[b_glu])
    tnm = _div(Dq, 512)

    def merge_epi(accs, rows, vecs, ri):
        ga, gs = _sigmoid(rows[0]), _sigmoid(rows[1])
        return [accs[0], accs[1], ga * accs[0] + gs * accs[1]]

    ba, bs, merged = _mm("merge", [(attn, Wt['w_br_attn'], D), (y2, Wt['w_br_ssm'], W)], L, D, tm=tr, tn=tnm,
                         epi=merge_epi, outs=[(F32, False), (F32, False), (BF16, False)],
                         rows=[(proj, ncr, 2 * D // tnm), (proj, ncr, 3 * D // tnm)])
    mix, x2 = _mm("out_proj", [(merged, Wt['w_out'], D)], L, D, tm=tr, tn=_div(D, 1024), epi=res_epi(1.0),
                  outs=[(F32, False), (F32, False)], rows=[(x1, ncr, 0)], vecs=[g2])
    weights_ready(4, x2)
    h3 = norm_mod("norm3", x2, ng[2:3], sh3, sc3)
    a3, b3, s3, f3, x3 = ffn_fwd("ffn2", h3, x2, g3)

    def loss_fn(i, yt, tt_):
        diff = yt - tt_
        return [diff * (1.0 / D), jnp.sum(diff * diff, axis=0, keepdims=True)]

    dy, sq = _rowk("loss", loss_fn, L, tr, [(x3, 'row'), (loss_target[0], 'row')],
                   [((L, D), F32, 'row'), ((1, D), F32, 'acc')])
    loss = lax.psum(0.5 * jnp.sum(sq) / D, ("x", "y", "c"))

    def res_bwd(name, dxo, f_, gate, coef):
        rows, nrow = dxo.shape[0], gate.shape[0]

        def fn(i, dt, ft, gt):
            return [(coef * sel(i, gt)) * dt, put(i, jnp.sum(dt * ft, axis=0, keepdims=True) * coef, nrow)]

        return _rowk(name, fn, rows, tr, [(dxo, 'row'), (f_, 'row'), (gate, 'vec')],
                     [((rows, D), BF16, 'row'), ((nrow, D), F32, 'acc')])

    def swiglu_bwd_epi(accs, rows, vecs, ri):
        ds_, a_, b_ = accs[0], rows[0], rows[1]
        sg = _sigmoid(a_)
        return [ds_ * b_ * (sg * (1.0 + a_ * (1.0 - sg))), ds_ * (a_ * sg)]

    def norm_mod_bwd(name, xv, g, sh, sc, dh, dres, dres_kind):
        rows, nrow = xv.shape[0], sh.shape[0]

        def fn(i, xt, gt, sht, sct, dht, rest):
            _, vjp = jax.vjp(_norm_mod, xt, gt, sel(i, sht), sel(i, sct))
            dx_, dg_, dsh_, dsc_ = vjp(dht)
            dx_ = dx_ + (jnp.where(i >= ncr, rest, 0.0) if dres_kind == 'xrow' else rest)
            return [dx_, dg_, put(i, dsh_, nrow), put(i, dsc_, nrow)]

        return _rowk(name, fn, rows, tr,
                     [(xv, 'row'), (g, 'vec'), (sh, 'vec'), (sc, 'vec'), (dh, 'row'), (dres, dres_kind)],
                     [((rows, D), F32, 'row'), ((1, D), F32, 'acc'), ((nrow, D), F32, 'acc'), ((nrow, D), F32, 'acc')],
                     nc=ncr)

    def ffn_bwd(tag, dxo, h, a_, b_, s_, f_, gate, wg, wu, wd, on_dwd=None):
        rows = dxo.shape[0]
        df, dgate = res_bwd(tag + "_dres", dxo, f_, gate, 0.5)
        dwd = _mm(tag + "_dwd", [(s_, df, rows)], F, D, tm=_div(F, 512), tn=_div(D, 1024), ta=True, epi=ident,
                  outs=[(BF16, False)])[0].reshape(N_CHIPS, F4, D)
        if on_dwd is not None:
            on_dwd(dwd)
        da, db = _mm(tag + "_dact", [(df, wd, D)], rows, F, tm=_div(rows, 384), tn=F4, tb=True, epi=swiglu_bwd_epi,
                     outs=[(BF16, False), (BF16, False)], rows=[(a_, 0, 0), (b_, 0, 0)])
        dwg = _mm(tag + "_dwg", [(h, da, rows)], D, F, tm=_div(D, 512), tn=F4, ta=True, epi=ident,
                  outs=[(BF16, True)])[0]
        dwu = _mm(tag + "_dwu", [(h, db, rows)], D, F, tm=_div(D, 512), tn=F4, ta=True, epi=ident,
                  outs=[(BF16, True)])[0]
        dh = _mm(tag + "_dh", [(da, wg, F), (db, wu, F)], rows, D, tm=_div(rows, 768), tn=_div(D, 1024), nk=N_CHIPS,
                 tb=True, epi=ident, outs=[(F32, False)], summed=True)[0]
        return dh, dgate, dwg, dwu, dwd

    dh3, dg3, dwg2, dwu2, dwd2 = ffn_bwd("ffn2", dy, h3, a3, b3, s3, f3, g3, Wt['w_ffn2_gate'], Wt['w_ffn2_up'],
                                         Wt['w_ffn2_down'])
    tok_r1, scatter_fin1 = _scatter_split("scatter_ffn2", [dwg2, dwu2, dwd2], dg3)
    dx2, dng3, dsh3, dsc3 = norm_mod_bwd("norm3_bwd", x2, ng[2:3], sh3, sc3, dh3, dy, 'row')
    dmix, dg2 = res_bwd("mix_dres", dx2, mix, g2 + tok_r1[0:1, 0:1], 1.0)

    def dmerge_epi(accs, rows, vecs, ri):
        dm_, ba_, bs_ = accs[0], rows[0], rows[1]
        ga, gs = _sigmoid(rows[2]), _sigmoid(rows[3])
        return [dm_ * ga, dm_ * gs, dm_ * ba_ * ga * (1.0 - ga), dm_ * bs_ * gs * (1.0 - gs)]

    dba, dbs, dga, dgs = _mm("dmerge", [(dmix, Wt['w_out'], D)], L, D, tm=tr, tn=tnm, tb=True, epi=dmerge_epi,
                             outs=[(BF16, False)] * 4,
                             rows=[(ba, 0, 0), (bs, 0, 0), (proj, ncr, 2 * D // tnm), (proj, ncr, 3 * D // tnm)])
    dwout = _mm("dw_out", [(merged, dmix, L)], D, D, tm=_div(D, 512), tn=_div(D, 1024), ta=True, epi=ident,
                outs=[(BF16, False)])[0].reshape(N_CHIPS, Dq, D)
    dattn = _mm("dattn", [(dba, Wt['w_br_attn'], D)], L, D, tm=_div(L, 512), tn=_div(D, 1024), tb=True, epi=ident,
                outs=[(BF16, False)])[0]
    dwba = _mm("dw_br_attn", [(attn, dba, L)], D, D, tm=_div(D, 512), tn=_div(D, 1024), ta=True, epi=ident,
               outs=[(BF16, False)])[0].reshape(N_CHIPS, Dq, D)
    dy2 = _mm("dy2", [(dbs, Wt['w_br_ssm'], D)], L, W, tm=_div(L, 512), tn=_div(W, 1024), nk=N_CHIPS, tb=True,
              epi=ident, outs=[(F32, False)])[0]
    dwbs = _mm("dw_br_ssm", [(y2, dbs, L)], W, D, tm=_div(W, 512), tn=_div(Dq, 512), ta=True, epi=ident,
               outs=[(BF16, True)])[0]

    def glu_bwd_fn(i, d2, ygt, zt):
        sz = _sigmoid(zt)
        dz_ = d2 * ygt * sz * (1.0 - sz)
        return [dz_, d2 * sz, jnp.sum(dz_, axis=0, keepdims=True)]

    dz, dyd, dbglu = _rowk("glu_bwd", glu_bwd_fn, L, tr, [(dy2, 'row'), (yg, 'row'), (zglu, 'row')],
                           [((L, W), BF16, 'row'), ((L, W), F32, 'row'), ((1, W), F32, 'acc')])

    def dssm_epi(accs, rows, vecs, ri):
        _, vjp = jax.vjp(_gelu, rows[1])
        ds_ = vjp(accs[0] + rows[0])[0]
        return [ds_, ds_]

    dssm, dssm_b = _mm("dssm", [(dz, Wt['w_glu'], W)], L, W, tm=_div(L, 512), tn=_div(W, 512), tb=True, epi=dssm_epi,
                       outs=[(F32, False), (BF16, False)], rows=[(dyd, 0, 0), (ssm_pre, 0, 0)])
    dwglu = _mm("dw_glu", [(ygb, dz, L)], W, W, tm=_div(W, 512), tn=_div(W, 1024), ta=True, epi=ident,
                outs=[(BF16, False)])[0].reshape(N_CHIPS, W // N_CHIPS, W)
    tok_r2a, scatter_fin2a = _scatter_split("scatter_mix", [dwglu, dwba, dwbs, dwout], dbglu)
    dssm_full = jnp.concatenate([jnp.zeros((Lc, W), BF16), dssm_b], axis=0)
    dus, dlam_re, dlam_im, dcoef_re, dcoef_im, dbbd, dcbd_re, dcbd_im = [], [], [], [], [], [], [], []
    for d in range(2):
        r = _ssm_bwd("ssm_bwd%d" % d, dssm_full, hs_re[d], hs_im[d], ub, bbd[d], bbdt_re[d], bbdt_im[d],
                     cbdt_re[d], cbdt_im[d], lam_re[d] + tok_r2a[0:1, 0:1], lam_im[d], coef_re[d], coef_im[d], Lc,
                     reverse=bool(d))
        for lst, val in zip((dus, dlam_re, dlam_im, dcoef_re, dcoef_im, dbbd, dcbd_re, dcbd_im), r):
            lst.append(val)
    dqr, dkr, dvf = _attn_bwd(qr, kr, vb, dattn, L, Lc, D)

    def prep_bwd_fn(i, qt, kt, ut, dqt, dkt, dvt, du0, du1, dst, dt, qgt, kgt, ct, st):
        live = i >= ncr
        dqt = jnp.where(live, dqt, 0.0)
        dst = jnp.where(live, dst, 0.0)
        dqs, dks = [], []
        dqg_ = jnp.zeros((1, HEAD_DIM), F32)
        dkg_ = jnp.zeros((1, HEAD_DIM), F32)
        for h in range(nh):
            hl = slice(h * HEAD_DIM, (h + 1) * HEAD_DIM)
            dn = dqt[:, hl] * ct + _rot(dqt[:, hl] * st)
            _, vjp = jax.vjp(_head_norm, qt[:, hl], qgt)
            dxh, dgh = vjp(dn)
            dqs.append(dxh)
            dqg_ = dqg_ + dgh
        for h in range(nkvh):
            hl = slice(h * HEAD_DIM, (h + 1) * HEAD_DIM)
            dn = dkt[:, hl] * ct + _rot(dkt[:, hl] * st)
            _, vjp = jax.vjp(_head_norm, kt[:, hl], kgt)
            dxh, dgh = vjp(dn)
            dks.append(dxh)
            dkg_ = dkg_ + dgh
        du_ = du0 + du1 + dst * dt
        return [jnp.concatenate(dqs, axis=1), jnp.concatenate(dks, axis=1), dvt, du_, dqg_, dkg_,
                jnp.sum(dst * ut, axis=0, keepdims=True)]

    dq_b, dk_b, dv_b, du_b, dqg, dkg, dssd = _rowk(
        "qk_prep_bwd", prep_bwd_fn, T, tr,
        [(proj, ('col', D, 1)), (proj, ('col', KV, 0)), (proj, ('col', W, 1)), (dqr, 'xrow'), (dkr, 'row'),
         (dvf, 'row'), (dus[0], 'row'), (dus[1], 'row'), (dssm, 'xrow'), (ssm_d, 'vec'), (qg, 'vec'), (kg, 'vec'),
         (cos_t, 'row'), (sin_t, 'row')],
        [((T, D), BF16, 'row'), ((T, KV), BF16, 'row'), ((T, KV), BF16, 'row'), ((T, W), BF16, 'row'),
         ((1, HEAD_DIM), F32, 'acc'), ((1, HEAD_DIM), F32, 'acc'), ((1, W), F32, 'acc')], nc=ncr)
    dgate = jnp.concatenate([jnp.zeros((Lc, 2 * D), BF16), jnp.concatenate([dga, dgs], axis=1)], axis=0)
    dproj = jnp.concatenate([dk_b, dv_b, du_b, dq_b, dgate], axis=1)
    dh2 = _mm("in_proj_dx", [(dproj, Wt['w_in'], 4 * D)], T, D, tm=_div(T, 768), tn=_div(D, 1024), nk=N_CHIPS, tb=True,
              epi=ident, outs=[(F32, False)])[0]
    dwin = _mm("in_proj_dw", [(h2, dproj, T)], D, 4 * D, tm=_div(D, 512), tn=_div(D, 1024), ta=True, epi=ident,
               outs=[(BF16, True)])[0]
    tok_r2, scatter_fin2 = _scatter_split("scatter_w_in", [dwin], dqg)
    dx1, dng2, dsh2, dsc2 = norm_mod_bwd("norm2_bwd", x1, ng[1:2] + tok_r2[0:1, 0:1], sh2, sc2, dh2, dx2, 'xrow')
    early = {}

    def start_down(dwd):
        early['tok'], early['fin'] = _scatter_split("scatter_ffn1_down", [dwd], dg2)

    dh1, dg1, dwg1, dwu1, dwd1 = ffn_bwd("ffn1", dx1, h1, a1, b1, s1, f1, g1, Wt['w_ffn1_gate'], Wt['w_ffn1_up'],
                                         Wt['w_ffn1_down'], on_dwd=start_down)
    dx0, dng1, dsh1, dsc1 = norm_mod_bwd("norm1_bwd", xc, ng[0:1] + early['tok'][0:1, 0:1], sh1, sc1, dh1, dx1, 'row')
    grad_x = dx0[Lc:][None]

    zD = jnp.zeros((1, D), F32)
    dmod_x = jnp.concatenate([dsh1[1:2], dsc1[1:2], dg1[1:2], dsh2[1:2], dsc2[1:2], dg2, dsh3, dsc3, dg3], axis=1)
    dmod_c = jnp.concatenate([dsh1[0:1], dsc1[0:1], dg1[0:1], dsh2[0:1], dsc2[0:1], zD, zD, zD, zD], axis=1)
    db_parts, dc_parts = [], []
    for d in range(2):
        db_parts.append(jnp.transpose(_bd_extract(dbbd[d][:, :, :SLAB_ST], E, P).reshape(G, E, P), (0, 2, 1)))
        db_parts.append(jnp.transpose(_bd_extract(dbbd[d][:, :, SLAB_ST:], E, P).reshape(G, E, P), (0, 2, 1)))
        dc_parts.append(jnp.transpose(_bd_extract(dcbd_re[d], P, E).reshape(G, P, E), (0, 2, 1)))
        dc_parts.append(jnp.transpose(_bd_extract(dcbd_im[d], P, E).reshape(G, P, E), (0, 2, 1)))
    pieces = [dmod_x, dmod_c, dng1, dng2, dng3, dqg, dkg] + dlam_re + dlam_im + dcoef_re + dcoef_im \
        + db_parts + dc_parts + [dssd, dbglu]
    shapes = [p_.shape for p_ in pieces]
    pack = _pack(pieces)
    RP = pack.shape[0]
    allp = _allgather_small("gather_small", pack).reshape(N_DEV, RP, PACK_W)

    head_rows = -(-18 * D // PACK_W)
    head = allp[:, :head_rows].reshape(N_DEV, head_rows * PACK_W)
    dmx_all = head[:, :9 * D]

    def sum_rows_fn(i, t):
        s_ = t[0:1]
        for k in range(1, N_DEV):
            s_ = s_ + t[k:k + 1]
        return [s_]

    dmc_sum = _rowk("sum_dmod_c", sum_rows_fn, 1, 1, [(head[:, 9 * D:18 * D], 'vec')], [((1, 9 * D), F32, 'row')])[0]
    cots = jnp.concatenate([dmx_all, dmc_sum, jnp.zeros((7, 9 * D), F32)], axis=0)
    cots_sh = lax.dynamic_slice(cots, (0, chip * NM), (16, NM))
    part = _mm("cctx_part", [(cots_sh[8:16], wm, NM)], 8, D, tm=8, tn=_div(D, 1024), nk=NM // _div(NM, 1152), tb=True,
               epi=ident, outs=[(F32, False)], a_pro=to_bf, b_pro=to_bf)[0]
    parts = _allgather_small("gather_cctx", part).reshape(N_CHIPS, 2, 8, D)[:, 0, 0]

    def cctx_fn(i, pt, ct):
        ds_ = ((pt[0:1] + pt[1:2]) + pt[2:3]) + pt[3:4]
        _, vjp = jax.vjp(lambda v: v * _sigmoid(v), ct)
        return [vjp(ds_)[0]]

    g_cctx = _rowk("cctx_grad", cctx_fn, 1, 1, [(parts, 'vec'), (c_ctx[None], 'row')], [((1, D), F32, 'row')])[0]

    tok_r3, scatter_fin3 = _scatter_split("scatter_ffn1_up", [dwg1, dwu1], g_cctx)
    zero = tok_r3[0:1, 0:1]

    def sum_dev_fn(i, t):
        s_ = t[0]
        for k in range(1, N_DEV):
            s_ = s_ + t[k]
        return [s_]

    tot = _rowk("sum_small", sum_dev_fn, RP, 8, [(allp, 'row3')], [((RP, PACK_W), F32, 'row')])[0]
    (t_dmod_x, t_dmod_c, t_ng1, t_ng2, t_ng3, t_qg, t_kg, t_lr0, t_lr1, t_li0, t_li1, t_kr0, t_kr1, t_ki0, t_ki1,
     t_bre0, t_bim0, t_bre1, t_bim1, t_cre0, t_cim0, t_cre1, t_cim1, t_d, t_bglu) = _unpack(tot, shapes)
    cat2 = lambda u0, u1: jnp.concatenate([u0.reshape(G, P), u1.reshape(G, P)], axis=0)
    g_are, g_aim, g_ldt = _zoh_bwd(a_re2 + zero, a_im2, ldt2, [cat2(t_lr0, t_lr1), cat2(t_li0, t_li1),
                                                                cat2(t_kr0, t_kr1), cat2(t_ki0, t_ki1)])
    g_bmod = _rowk("bmod_grad", lambda i, u0, u1: [u0 + u1], 1, 1, [(t_dmod_x, 'row'), (t_dmod_c, 'row')],
                   [((1, 9 * D), F32, 'row')])[0]
    g_wmod = _outer_sum(acts + zero, cots_sh)
    results = {}
    results['w_mod'] = _adamw("adamw_w_mod", w_mod, m_w_mod, v_w_mod, [g_wmod])

    def sum_chip_fn(i, own, t):
        return [((own.astype(F32) + t[0].astype(F32)) + t[1].astype(F32)) + t[2].astype(F32)]

    def reduce_group(tag, names, grads, fin, after_work):
        landed = fin(after_work)
        plane = []
        for n, g_, rb in zip(names, grads, landed):
            R_, C_ = rb.shape[1], rb.shape[2]
            own = lax.dynamic_index_in_dim(g_, chip, 0, keepdims=False)
            plane.append(_rowk("sum_" + n, sum_chip_fn, R_, _div(R_, max(16, 262144 // C_), mult=16),
                               [(own, 'row'), (rb, 'row3')], [((R_, C_), F32, 'row')])[0])
        other = _swap_sibling("swap_" + tag, plane)
        for n, mine, theirs in zip(names, plane, other):
            results[n] = _adamw("adamw_" + n, A[n], A['m_' + n], A['v_' + n], [mine, theirs])

    reduce_group("ffn2", big[3:6], [dwg2, dwu2, dwd2], scatter_fin1, tok_r3)
    reduce_group("mix", big[7:11], [dwglu, dwba, dwbs, dwout], scatter_fin2a, results['w_ffn2_down'][0])
    reduce_group("w_in", big[6:7], [dwin], scatter_fin2, results['w_out'][0])
    reduce_group("ffn1_down", big[2:3], [dwd1], early['fin'], results['w_out'][0])
    reduce_group("ffn1_up", big[0:2], [dwg1, dwu1], scatter_fin3, results['w_ffn1_down'][0])

    ng_full =jnp.concatenate([t_ng1, t_ng2, t_ng3], axis=0)
    gsmall = {
        'c_ctx': g_cctx, 'b_mod': g_bmod, 'norm_g': lax.dynamic_slice(ng_full, (0, chip * Dq), (3, Dq)),
        'q_norm_g': t_qg, 'k_norm_g': t_kg, 'ssm_a_re': g_are, 'ssm_a_im': g_aim, 'ssm_log_dt': g_ldt,
        'ssm_b_re': jnp.stack([t_bre0, t_bre1]), 'ssm_b_im': jnp.stack([t_bim0, t_bim1]),
        'ssm_c_re': jnp.stack([t_cre0, t_cre1]), 'ssm_c_im': jnp.stack([t_cim0, t_cim1]),
        'ssm_d': t_d, 'b_glu': t_bglu}
    sshapes = [A[n].shape for n in small]
    sres = _adamw("adamw_small", packs_wmv[0], packs_wmv[1], packs_wmv[2], [_pack([gsmall[n] for n in small])])
    sres = [_unpack(b_, sshapes) for b_ in sres]
    for k, n in enumerate(small):
        results[n] = tuple(sres[q][k] for q in range(4))

    order = ['c_ctx', 'w_mod', 'b_mod', 'norm_g', 'w_ffn1_gate', 'w_ffn1_up', 'w_ffn1_down', 'w_in', 'q_norm_g',
             'k_norm_g', 'ssm_a_re', 'ssm_a_im', 'ssm_log_dt', 'ssm_b_re', 'ssm_b_im', 'ssm_c_re', 'ssm_c_im',
             'ssm_d', 'w_glu', 'b_glu', 'w_br_attn', 'w_br_ssm', 'w_out', 'w_ffn2_gate', 'w_ffn2_up', 'w_ffn2_down']
    outs = [loss, grad_x]
    for q in range(4):
        outs += [results[n][q].reshape(A[n].shape) for n in order]
    return tuple(outs)
```

```python
import math

import jax
import jax.numpy as jnp
from jax import lax
from jax.experimental import pallas as pl
from jax.experimental.pallas import tpu as pltpu

F32 = jnp.float32
BF16 = jnp.bfloat16
MESH = pl.DeviceIdType.MESH

NORM_EPS = 1e-6
ROPE_THETA = 10000.0
GRID_W = 64
HEAD_DIM = 128
Q_PER_KV = 4
SSM_GROUP = 16
SSM_STATE = 64
ADAM_LR = 0.001
ADAM_B1 = 0.9
ADAM_B2 = 0.999
ADAM_EPS = 1e-08
ADAM_WD = 0.01
ADAM_STEP = 10

N_CHIPS = 4
N_DEV = 8
LANES = 128
SLAB_CH = 128
SLAB_GROUPS = SLAB_CH // SSM_GROUP
SLAB_ST = SLAB_GROUPS * SSM_STATE
VMEM_LIMIT_BYTES = 56 * 1024 * 1024
PACK_W = 1024


def _cparams(**kw):
    return pltpu.CompilerParams(vmem_limit_bytes=VMEM_LIMIT_BYTES, **kw)


def _div(n, pref, mult=LANES):
    t = (min(pref, n) // mult) * mult
    while t >= mult:
        if n % t == 0:
            return t
        t -= mult
    return n


def _sigmoid(x):
    return jax.nn.sigmoid(x)


def _gelu(x):
    return x * (0.5 * (1.0 + jnp.tanh(math.sqrt(2.0 / math.pi) * (x + 0.044715 * (x * x * x)))))


def _rowk(name, fn, nrows, tr, ins, outs, nc=0):
    nt = nrows // tr
    in_specs, arrays = [], []
    for arr, kind in ins:
        arrays.append(arr)
        if kind == 'row':
            in_specs.append(pl.BlockSpec((tr, arr.shape[1]), lambda i: (i, 0)))
        elif kind == 'xrow':
            in_specs.append(pl.BlockSpec((tr, arr.shape[1]), lambda i: (jnp.maximum(i - nc, 0), 0)))
        elif kind == 'orow':
            in_specs.append(pl.BlockSpec((tr, arr.shape[1]), lambda i: (i + nc, 0)))
        elif kind == 'vec':
            in_specs.append(pl.BlockSpec(arr.shape, lambda i, nd=arr.ndim: (0,) * nd))
        elif kind == 'row3':
            in_specs.append(pl.BlockSpec((arr.shape[0], tr, arr.shape[2]), lambda i: (0, i, 0)))
        elif kind == 'row1':
            in_specs.append(pl.BlockSpec((None, tr, arr.shape[2]), lambda i: (0, i, 0)))
        elif kind[0] == 'ocol':
            _, width, blk = kind
            in_specs.append(pl.BlockSpec((tr, width), lambda i, blk=blk: (i + nc, blk)))
        else:
            _, width, blk = kind
            in_specs.append(pl.BlockSpec((tr, width), lambda i, blk=blk: (i, blk)))
    out_shape, out_specs = [], []
    for shape, dtype, kind in outs:
        out_shape.append(jax.ShapeDtypeStruct(shape, dtype))
        if kind == 'row':
            out_specs.append(pl.BlockSpec((tr, shape[1]), lambda i: (i, 0)))
        elif kind == 'row1':
            out_specs.append(pl.BlockSpec((None, tr, shape[2]), lambda i: (0, i, 0)))
        else:
            out_specs.append(pl.BlockSpec(shape, lambda i, nd=len(shape): (0,) * nd))
    nin = len(ins)

    def body(*refs):
        i = pl.program_id(0)
        res = fn(i, *[r[...] for r in refs[:nin]])
        for (shape, dtype, kind), ref, val in zip(outs, refs[nin:], res):
            if kind in ('row', 'row1'):
                ref[...] = val.astype(dtype)
            else:
                @pl.when(i == 0)
                def _():
                    ref[...] = val.astype(dtype)

                @pl.when(i > 0)
                def _():
                    ref[...] += val.astype(dtype)

    return pl.pallas_call(body, name=name, grid=(nt,), in_specs=in_specs, out_specs=out_specs,
                          out_shape=out_shape, compiler_params=_cparams())(*arrays)


def _mm(name, pairs, M, N, *, tm, tn, nk=1, epi, outs, ta=False, tb=False, rows=(), vecs=(),
        a_pro=None, b_pro=None, n_outer=True, summed=False):
    nm, nn = M // tm, N // tn
    npair = len(pairs)

    def idx(f):
        if n_outer:
            return lambda j, i, k: f(i, j, k)
        return lambda i, j, k: f(i, j, k)

    in_specs, args = [], []
    for a, b, K in pairs:
        tk = K // nk
        if ta:
            in_specs.append(pl.BlockSpec((tk, tm), idx(lambda i, j, k: (k, i))))
        else:
            in_specs.append(pl.BlockSpec((tm, tk), idx(lambda i, j, k: (i, k))))
        args.append(a)
        if b.ndim == 3:
            if tb:
                per = b.shape[2] // tk
                in_specs.append(pl.BlockSpec((None, tn, tk), idx(lambda i, j, k, per=per: (k // per, j, k % per))))
            else:
                per = b.shape[2] // tn
                in_specs.append(pl.BlockSpec((None, tk, tn), idx(lambda i, j, k, per=per: (j // per, k, j % per))))
        elif tb:
            in_specs.append(pl.BlockSpec((tn, tk), idx(lambda i, j, k: (j, k))))
        else:
            in_specs.append(pl.BlockSpec((tk, tn), idx(lambda i, j, k: (k, j))))
        args.append(b)
    for arr, ro, co in rows:
        in_specs.append(pl.BlockSpec((tm, tn), idx(lambda i, j, k, ro=ro, co=co: (i + ro, j + co))))
        args.append(arr)
    for arr in vecs:
        in_specs.append(pl.BlockSpec((arr.shape[0], tn), idx(lambda i, j, k: (0, j))))
        args.append(arr)
    out_shape, out_specs = [], []
    for dtype, chunked in outs:
        if chunked:
            per = (N // N_CHIPS) // tn
            out_shape.append(jax.ShapeDtypeStruct((N_CHIPS, M, N // N_CHIPS), dtype))
            out_specs.append(pl.BlockSpec((None, tm, tn), idx(lambda i, j, k, per=per: (j // per, i, j % per))))
        else:
            out_shape.append(jax.ShapeDtypeStruct((M, N), dtype))
            out_specs.append(pl.BlockSpec((tm, tn), idx(lambda i, j, k: (i, j))))
    nacc = 1 if summed else npair
    scratch = [pltpu.VMEM((tm, tn), F32) for _ in range(nacc)] if nk > 1 else []
    nrow, nvec, nout = len(rows), len(vecs), len(outs)
    dims = (((0 if ta else 1,), (1 if tb else 0,)), ((), ()))

    def body(*refs):
        ab = refs[:2 * npair]
        row_refs = refs[2 * npair:2 * npair + nrow]
        vec_refs = refs[2 * npair + nrow:2 * npair + nrow + nvec]
        out_refs = refs[2 * npair + nrow + nvec:2 * npair + nrow + nvec + nout]
        acc_refs = refs[2 * npair + nrow + nvec + nout:]
        if n_outer:
            j, i, k = pl.program_id(0), pl.program_id(1), pl.program_id(2)
        else:
            i, j, k = pl.program_id(0), pl.program_id(1), pl.program_id(2)

        def part(p):
            av, bv = ab[2 * p][...], ab[2 * p + 1][...]
            if a_pro is not None:
                av = a_pro(av)
            if b_pro is not None:
                bv = b_pro(bv)
            return lax.dot_general(av, bv, dims, preferred_element_type=F32)

        def finish(accs):
            row_index = i * tm + lax.broadcasted_iota(jnp.int32, (tm, 1), 0)
            res = epi(accs, [r[...] for r in row_refs], [v[...] for v in vec_refs], row_index)
            for ref, val in zip(out_refs, res):
                ref[...] = val.astype(ref.dtype)

        parts = [part(p) for p in range(npair)]
        if summed:
            total = parts[0]
            for extra in parts[1:]:
                total = total + extra
            parts = [total]
        if nk == 1:
            finish(parts)
        else:
            @pl.when(k == 0)
            def _():
                for q in range(nacc):
                    acc_refs[q][...] = parts[q]

            @pl.when(jnp.logical_and(k > 0, k < nk - 1))
            def _():
                for q in range(nacc):
                    acc_refs[q][...] += parts[q]

            @pl.when(k == nk - 1)
            def _():
                finish([acc_refs[q][...] + parts[q] for q in range(nacc)])

    grid = (nn, nm, nk) if n_outer else (nm, nn, nk)
    return pl.pallas_call(body, name=name, grid=grid, in_specs=in_specs, out_specs=out_specs,
                          out_shape=out_shape, scratch_shapes=scratch, compiler_params=_cparams())(*args)


def _split3(v):
    v0 = v.astype(BF16)
    r1 = v - v0.astype(F32)
    v1 = r1.astype(BF16)
    v2 = (r1 - v1.astype(F32)).astype(BF16)
    return v0, v1, v2


def _mesh_pos():
    return lax.axis_index("x"), lax.axis_index("y"), lax.axis_index("c")


def _allgather_small(name, x):
    m, n = x.shape

    def body(x_ref, out_ref, send_sems, recv_sems, local_sem):
        xi, yi, ci = _mesh_pos()
        me, sibling = (xi, yi, ci), (xi, yi, 1 - ci)
        chips = [(1 - xi, yi), (xi, 1 - yi), (1 - xi, 1 - yi)]

        def rows(px, py, pc):
            return out_ref.at[pl.ds((4 * px + 2 * py + pc) * m, m), :]

        def copy(k, block, to, src=None):
            return pltpu.make_async_remote_copy(
                src_ref=rows(*block) if src is None else src, dst_ref=rows(*block),
                send_sem=send_sems.at[k], recv_sem=recv_sems.at[k], device_id=to, device_id_type=MESH)

        mine = pltpu.make_async_copy(x_ref, rows(*me), local_sem)
        mine.start()
        first = [copy(0, me, sibling, src=x_ref)]
        first += [copy(1 + j, me, (*chip, ci), src=x_ref) for j, chip in enumerate(chips)]
        for cp in first:
            cp.start()
        passed = [copy(4 + j, (*chip, ci), sibling) for j, chip in enumerate(chips)]
        for j, chip in enumerate(chips):
            copy(1 + j, (*chip, ci), me).wait_recv()
            passed[j].start()
        copy(0, sibling, me).wait_recv()
        for j, chip in enumerate(chips):
            copy(4 + j, (*chip, 1 - ci), me).wait_recv()
        for cp in first + passed:
            cp.wait_send()
        mine.wait()

    return pl.pallas_call(
        body, name=name, out_shape=jax.ShapeDtypeStruct((N_DEV * m, n), x.dtype),
        in_specs=[pl.BlockSpec(memory_space=pltpu.VMEM)], out_specs=pl.BlockSpec(memory_space=pltpu.VMEM),
        scratch_shapes=[pltpu.SemaphoreType.DMA((7,)), pltpu.SemaphoreType.DMA((7,)), pltpu.SemaphoreType.DMA],
        compiler_params=_cparams())(x)


_HBM = pl.BlockSpec(memory_space=pltpu.HBM)
_SEM = pl.BlockSpec(memory_space=pltpu.SEMAPHORE)
_ANY = pl.BlockSpec(memory_space=pl.ANY)
_EFFECT = pltpu.SideEffectType.DATAFLOW_SIDE_EFFECTING


def _in_hbm(v):
    return pltpu.with_memory_space_constraint(v, pltpu.HBM)


def _other_chips(xi, yi):
    return [(1 - xi, yi), (xi, 1 - yi), (1 - xi, 1 - yi)]


def _guarded(north_only, fn):
    if north_only:
        pl.when(lax.axis_index("c") == 1)(fn)
    else:
        fn()


def _split_copies(name, srcs, lands, after, pairs, north_only, ncopy):
    ns, nl = len(srcs), len(lands)
    dma = pltpu.SemaphoreType.DMA((ncopy,))
    thru = [pltpu.HBM(v.shape, v.dtype) for v in list(srcs) + list(lands)]

    def start_body(*refs):
        src_refs, land_refs = refs[:ns], refs[ns:ns + nl]
        descs = pairs(src_refs, land_refs, refs[ns + nl + 1], refs[ns + nl + 2])

        def go():
            for send, _ in descs:
                send.start()

        _guarded(north_only, go)
        refs[-1][...] = jnp.zeros_like(refs[-1])

    res = pl.pallas_call(
        start_body, name=name + "_start",
        out_shape=(dma, dma, *thru, jax.ShapeDtypeStruct((8, LANES), F32)),
        in_specs=[_HBM] * (ns + nl) + [_ANY],
        out_specs=(_SEM, _SEM, *([_HBM] * (ns + nl)), pl.BlockSpec(memory_space=pltpu.VMEM)),
        input_output_aliases={k: 2 + k for k in range(ns + nl)},
        compiler_params=_cparams(has_side_effects=_EFFECT),
    )(*[_in_hbm(v) for v in srcs], *[_in_hbm(v) for v in lands], after)
    send_sems, recv_sems, token = res[0], res[1], res[-1]
    carried = res[2:2 + ns + nl]

    def finish(after_work):
        def wait_body(*refs):
            src_refs, land_refs = refs[:ns], refs[ns:ns + nl]
            descs = pairs(src_refs, land_refs, refs[ns + nl], refs[ns + nl + 1])

            def go():
                for send, recv in descs:
                    send.wait_send()
                    recv.wait_recv()

            _guarded(north_only, go)

        out = pl.pallas_call(
            wait_body, name=name + "_wait", out_shape=tuple(thru),
            in_specs=[_HBM] * (ns + nl) + [_SEM, _SEM, _ANY], out_specs=tuple([_HBM] * (ns + nl)),
            input_output_aliases={k: k for k in range(ns + nl)},
            compiler_params=_cparams(has_side_effects=_EFFECT),
        )(*carried, send_sems, recv_sems, after_work)
        return list(out[ns:])

    return token, finish


def _cast_slot(name, w, chip_index):
    R, C = w.shape[1:]
    tr = _div(R, max(16, 524288 // C), mult=16)

    def body(chip_ref, w_ref, o_ref):
        o_ref[...] = w_ref[...].astype(BF16)

    return pl.pallas_call(
        body, name=name, out_shape=jax.ShapeDtypeStruct((N_CHIPS, R, C), BF16),
        grid_spec=pltpu.PrefetchScalarGridSpec(
            num_scalar_prefetch=1, grid=(R // tr,),
            in_specs=[pl.BlockSpec((None, tr, C), lambda i, chip_ref: (0, i, 0))],
            out_specs=pl.BlockSpec((None, tr, C), lambda i, chip_ref: (chip_ref[0], i, 0))),
        compiler_params=_cparams())(chip_index, w)


def _sum_plane(name, grads, landed, chip_index):
    R, C = grads.shape[1:]
    tr = _div(R, max(16, 262144 // C), mult=16)

    def body(chip_ref, own_ref, land_ref, o_ref):
        o_ref[...] = ((own_ref[...].astype(F32) + land_ref[0].astype(F32)) + land_ref[1].astype(F32)) \
            + land_ref[2].astype(F32)

    return pl.pallas_call(
        body, name=name, out_shape=jax.ShapeDtypeStruct((R, C), F32),
        grid_spec=pltpu.PrefetchScalarGridSpec(
            num_scalar_prefetch=1, grid=(R // tr,),
            in_specs=[pl.BlockSpec((None, tr, C), lambda i, chip_ref: (chip_ref[0], i, 0)),
                      pl.BlockSpec((3, tr, C), lambda i, chip_ref: (0, i, 0))],
            out_specs=pl.BlockSpec((tr, C), lambda i, chip_ref: (i, 0))),
        compiler_params=_cparams())(chip_index, grads, landed)


def _gather_split(name, lands, after):
    def pairs(src_refs, land_refs, send_sems, recv_sems):
        xi, yi, _ = _mesh_pos()
        mine = 2 * xi + yi
        out = []
        for a in range(len(lands)):
            for j, (px, py) in enumerate(_other_chips(xi, yi)):
                def to_slot(slot, a=a, j=j, px=px, py=py):
                    return pltpu.make_async_remote_copy(
                        src_ref=land_refs[a].at[mine], dst_ref=land_refs[a].at[slot], send_sem=send_sems.at[3 * a + j],
                        recv_sem=recv_sems.at[3 * a + j], device_id=(px, py, 1), device_id_type=MESH)
                out.append((to_slot(mine), to_slot(2 * px + py)))
        return out

    return _split_copies(name, [], lands, after, pairs, north_only=True, ncopy=3 * len(lands))


def _scatter_split(name, grads, after):
    lands = [lax.empty((3,) + g.shape[1:], g.dtype) for g in grads]

    def pairs(src_refs, land_refs, send_sems, recv_sems):
        xi, yi, ci = _mesh_pos()
        out = []
        for a in range(len(grads)):
            for j, (px, py) in enumerate(_other_chips(xi, yi)):
                cp = pltpu.make_async_remote_copy(
                    src_ref=src_refs[a].at[2 * px + py], dst_ref=land_refs[a].at[j], send_sem=send_sems.at[3 * a + j],
                    recv_sem=recv_sems.at[3 * a + j], device_id=(px, py, ci), device_id_type=MESH)
                out.append((cp, cp))
        return out

    return _split_copies(name, grads, lands, after, pairs, north_only=False, ncopy=3 * len(grads))


def _gather_finish(name, lands):
    na = len(lands)

    def body(*refs):
        outs = refs[na:2 * na]
        send_sems, recv_sems = refs[2 * na:]
        xi, yi, ci = _mesh_pos()
        passes = [pltpu.make_async_remote_copy(
            src_ref=outs[a].at[2 * px + py], dst_ref=outs[a].at[2 * px + py],
            send_sem=send_sems.at[a, j], recv_sem=recv_sems.at[a, j], device_id=(xi, yi, 0), device_id_type=MESH)
            for a in range(na) for j, (px, py) in enumerate(_other_chips(xi, yi))]

        @pl.when(ci == 1)
        def _():
            for cp in passes:
                cp.start()
            for cp in passes:
                cp.wait_send()

        @pl.when(ci == 0)
        def _():
            for cp in passes:
                cp.wait_recv()

    return pl.pallas_call(
        body, name=name, out_shape=[jax.ShapeDtypeStruct(v.shape, v.dtype) for v in lands],
        in_specs=[_ANY] * na, out_specs=[_ANY] * na,
        input_output_aliases={a: a for a in range(na)},
        scratch_shapes=[pltpu.SemaphoreType.DMA((na, 3)), pltpu.SemaphoreType.DMA((na, 3))],
        compiler_params=_cparams())(*lands)


def _swap_sibling(name, arrs):
    na = len(arrs)

    def body(*refs):
        ins, outs = refs[:na], refs[na:2 * na]
        send_sems, recv_sems = refs[2 * na:]
        xi, yi, ci = _mesh_pos()
        copies = [pltpu.make_async_remote_copy(
            src_ref=ins[a], dst_ref=outs[a], send_sem=send_sems.at[a], recv_sem=recv_sems.at[a],
            device_id=(xi, yi, 1 - ci), device_id_type=MESH) for a in range(na)]
        for cp in copies:
            cp.start()
        for cp in copies:
            cp.wait()

    return pl.pallas_call(
        body, name=name,
        out_shape=[jax.ShapeDtypeStruct(g.shape, g.dtype) for g in arrs],
        in_specs=[_ANY] * na, out_specs=[_ANY] * na,
        scratch_shapes=[pltpu.SemaphoreType.DMA((na,)), pltpu.SemaphoreType.DMA((na,))],
        compiler_params=_cparams())(*arrs)


def _attn_tiles(L, Lc, D):
    tq = min(256, Lc)
    return tq, L // tq, Lc // tq, D // HEAD_DIM // Q_PER_KV


def _attn_probs(q, k):
    s = lax.dot_general(q, k, (((1,), (1,)), ((), ())), preferred_element_type=F32) * (HEAD_DIM ** -0.5)
    e = jnp.exp(s - jnp.max(s, axis=-1, keepdims=True))
    return e * (1.0 / jnp.sum(e, axis=-1, keepdims=True))


def _attn_fwd(qr, kr, v, L, Lc, D):
    T = L + Lc
    tq, nq, qoff, nkv = _attn_tiles(L, Lc, D)

    def body(q_ref, k_ref, v_ref, o_ref):
        p = _attn_probs(q_ref[...], k_ref[...])
        o_ref[...] = jnp.dot(p.astype(BF16), v_ref[...], preferred_element_type=F32).astype(o_ref.dtype)

    kv_spec = pl.BlockSpec((T, HEAD_DIM), lambda h, r, q: (0, h))
    return pl.pallas_call(
        body, name="attn_fwd", grid=(nkv, Q_PER_KV, nq),
        in_specs=[pl.BlockSpec((tq, HEAD_DIM), lambda h, r, q: (q + qoff, h * Q_PER_KV + r)), kv_spec, kv_spec],
        out_specs=pl.BlockSpec((tq, HEAD_DIM), lambda h, r, q: (q, h * Q_PER_KV + r)),
        out_shape=jax.ShapeDtypeStruct((L, D), BF16), compiler_params=_cparams())(qr, kr, v)


def _attn_bwd(qr, kr, v, do, L, Lc, D):
    T = L + Lc
    tq, nq, qoff, nkv = _attn_tiles(L, Lc, D)
    scale = HEAD_DIM ** -0.5

    def body(q_ref, k_ref, v_ref, do_ref, dq_ref, dk_ref, dv_ref):
        first = jnp.logical_and(pl.program_id(1) == 0, pl.program_id(2) == 0)
        q, k, dout = q_ref[...], k_ref[...], do_ref[...]
        p = _attn_probs(q, k)
        dp = lax.dot_general(dout, v_ref[...], (((1,), (1,)), ((), ())), preferred_element_type=F32)
        ds = (p * (dp - jnp.sum(p * dp, axis=-1, keepdims=True)) * scale).astype(BF16)
        dq_ref[...] = jnp.dot(ds, k, preferred_element_type=F32)
        dk = lax.dot_general(ds, q, (((0,), (0,)), ((), ())), preferred_element_type=F32)
        dv = lax.dot_general(p.astype(BF16), dout, (((0,), (0,)), ((), ())), preferred_element_type=F32)

        @pl.when(first)
        def _():
            dk_ref[...] = dk
            dv_ref[...] = dv

        @pl.when(jnp.logical_not(first))
        def _():
            dk_ref[...] += dk
            dv_ref[...] += dv

    kv_spec = pl.BlockSpec((T, HEAD_DIM), lambda h, r, q: (0, h))
    q_spec = pl.BlockSpec((tq, HEAD_DIM), lambda h, r, q: (q + qoff, h * Q_PER_KV + r))
    o_spec = pl.BlockSpec((tq, HEAD_DIM), lambda h, r, q: (q, h * Q_PER_KV + r))
    return pl.pallas_call(
        body, name="attn_bwd", grid=(nkv, Q_PER_KV, nq),
        in_specs=[q_spec, kv_spec, kv_spec, o_spec], out_specs=[o_spec, kv_spec, kv_spec],
        out_shape=[jax.ShapeDtypeStruct((L, D), F32), jax.ShapeDtypeStruct((T, D // Q_PER_KV), F32),
                   jax.ShapeDtypeStruct((T, D // Q_PER_KV), F32)],
        compiler_params=_cparams())(qr, kr, v, do)


SUB = 8


def _doubling(xr, xi, pw_re, pw_im, lanes, first_power, period, reverse):
    n = xr.shape[0]
    rows = lax.broadcasted_iota(jnp.int32, (n, 1), 0) & (period - 1)
    for k in range(period.bit_length() - 1):
        d = 1 << k
        keep = rows < period - d if reverse else rows >= d
        sr = jnp.where(keep, pltpu.roll(xr, n - d if reverse else d, 0), 0.0)
        si = jnp.where(keep, pltpu.roll(xi, n - d if reverse else d, 0), 0.0)
        pr, pi = pw_re[first_power + k:first_power + k + 1, lanes], pw_im[first_power + k:first_power + k + 1, lanes]
        xr, xi = xr + (pr * sr - pi * si), xi + (pr * si + pi * sr)
    return xr, xi


def _scan_tile(xr, xi, tb, lanes, reverse):
    pw_re, pw_im, w8_re, w8_im, wb_re, wb_im, carry_re, carry_im, sr, si = tb
    tt = xr.shape[0]
    nb = tt // SUB
    xr, xi = _doubling(xr, xi, pw_re, pw_im, lanes, 0, SUB, reverse)
    nq = sr.shape[0]
    cols = [slice(q * LANES, (q + 1) * LANES) for q in range(nq)]
    for q in range(nq):
        sr[q] = xr[:, cols[q]]
        si[q] = xi[:, cols[q]]
    last = 0 if reverse else SUB - 1
    er = jnp.concatenate([sr[q, pl.ds(last, nb, stride=SUB), :] for q in range(nq)], axis=1)
    ei = jnp.concatenate([si[q, pl.ds(last, nb, stride=SUB), :] for q in range(nq)], axis=1)
    er, ei = _doubling(er, ei, pw_re, pw_im, lanes, 3, nb, reverse)
    car, cai = carry_re[:, lanes], carry_im[:, lanes]
    wbr, wbi = wb_re[:, lanes], wb_im[:, lanes]
    er = er + (wbr * car - wbi * cai)
    ei = ei + (wbr * cai + wbi * car)
    out_block = 0 if reverse else nb - 1
    carry_re[:, lanes] = er[out_block:out_block + 1, :]
    carry_im[:, lanes] = ei[out_block:out_block + 1, :]
    blocks = lax.broadcasted_iota(jnp.int32, (nb, 1), 0)
    first = blocks == (nb - 1 if reverse else 0)
    cr = jnp.where(first, car, pltpu.roll(er, nb - 1 if reverse else 1, 0))
    ci = jnp.where(first, cai, pltpu.roll(ei, nb - 1 if reverse else 1, 0))
    for r in range(SUB):
        wr, wi = w8_re[r:r + 1, lanes], w8_im[r:r + 1, lanes]
        add_r, add_i = wr * cr - wi * ci, wr * ci + wi * cr
        for q in range(nq):
            sr[q, pl.ds(r, nb, stride=SUB), :] += add_r[:, cols[q]]
            si[q, pl.ds(r, nb, stride=SUB), :] += add_i[:, cols[q]]
    hr = jnp.concatenate([sr[q] for q in range(nq)], axis=1)
    hi = jnp.concatenate([si[q] for q in range(nq)], axis=1)
    return hr, hi, car, cai


def _scan_scratch(tt, NS):
    nb = tt // SUB
    return [pltpu.VMEM((8, NS), F32), pltpu.VMEM((8, NS), F32), pltpu.VMEM((SUB, NS), F32), pltpu.VMEM((SUB, NS), F32),
            pltpu.VMEM((nb, NS), F32), pltpu.VMEM((nb, NS), F32), pltpu.VMEM((1, NS), F32), pltpu.VMEM((1, NS), F32),
            pltpu.VMEM((SLAB_ST // LANES, tt, LANES), F32), pltpu.VMEM((SLAB_ST // LANES, tt, LANES), F32)]


def _scan_init(lr, li, tb, reverse):
    pw_re, pw_im, w8_re, w8_im, wb_re, wb_im, carry_re, carry_im, sr, _ = tb
    nb = wb_re.shape[0]
    carry_re[...] = jnp.zeros_like(carry_re)
    carry_im[...] = jnp.zeros_like(carry_im)
    pr, pi = lr, li
    for k in range(3 + nb.bit_length() - 1):
        pw_re[k:k + 1, :] = pr
        pw_im[k:k + 1, :] = pi
        if k == 3:
            l8r, l8i = pr, pi
        pr, pi = pr * pr - pi * pi, 2.0 * pr * pi
    pr, pi = lr, li
    for r in range(SUB):
        row = SUB - 1 - r if reverse else r
        w8_re[row:row + 1, :] = pr
        w8_im[row:row + 1, :] = pi
        pr, pi = pr * lr - pi * li, pr * li + pi * lr
    pr, pi = l8r, l8i
    for b in range(nb):
        row = nb - 1 - b if reverse else b
        wb_re[row:row + 1, :] = pr
        wb_im[row:row + 1, :] = pi
        pr, pi = pr * l8r - pi * l8i, pr * l8i + pi * l8r


def _ssm_tiles(T, Lc):
    tt = min(128, Lc)
    return tt, T // tt, Lc // tt


def _ssm_fwd(name, u, bbd, cbd_re, cbd_im, lam_re, lam_im, coef_re, coef_im, Lc, reverse):
    T, W = u.shape
    nslab = W // SLAB_CH
    NS = nslab * SLAB_ST
    tt, nt, nc = _ssm_tiles(T, Lc)
    if reverse:
        tile = lambda s: jnp.where(s < nc, nc - 1 - s, nt - 1 - (s - nc))
    else:
        tile = lambda s: s

    def body(u_ref, b_ref, cr_ref, ci_ref, lr_ref, li_ref, kr_ref, ki_ref, hr_ref, hi_ref, y_ref, *tb):
        @pl.when(pl.program_id(0) == 0)
        def _():
            _scan_init(lr_ref[...], li_ref[...], tb, reverse)

        for j in range(nslab):
            lanes = slice(j * SLAB_ST, (j + 1) * SLAB_ST)
            bu = jnp.dot(u_ref[:, j * SLAB_CH:(j + 1) * SLAB_CH], b_ref[j], preferred_element_type=F32)
            br, bi = bu[:, :SLAB_ST], bu[:, SLAB_ST:]
            kr, ki = kr_ref[:, lanes], ki_ref[:, lanes]
            hr, hi, _, _ = _scan_tile(kr * br - ki * bi, kr * bi + ki * br, tb, lanes, reverse)
            hrb, hib = hr.astype(BF16), hi.astype(BF16)
            hr_ref[:, lanes] = hrb
            hi_ref[:, lanes] = hib
            y_ref[:, j * SLAB_CH:(j + 1) * SLAB_CH] = (
                jnp.dot(hrb, cr_ref[j], preferred_element_type=F32)
                - jnp.dot(hib, ci_ref[j], preferred_element_type=F32))

    whole3 = lambda arr: pl.BlockSpec(arr.shape, lambda s: (0, 0, 0))
    vec = pl.BlockSpec((1, NS), lambda s: (0, 0))
    return pl.pallas_call(
        body, name=name, grid=(nt,),
        in_specs=[pl.BlockSpec((tt, W), lambda s: (tile(s), 0)), whole3(bbd), whole3(cbd_re), whole3(cbd_im),
                  vec, vec, vec, vec],
        out_specs=[pl.BlockSpec((tt, NS), lambda s: (tile(s), 0)), pl.BlockSpec((tt, NS), lambda s: (tile(s), 0)),
                   pl.BlockSpec((tt, W), lambda s: (tile(s), 0))],
        out_shape=[jax.ShapeDtypeStruct((T, NS), BF16), jax.ShapeDtypeStruct((T, NS), BF16),
                   jax.ShapeDtypeStruct((T, W), F32)],
        scratch_shapes=_scan_scratch(tt, NS),
        compiler_params=_cparams())(u, bbd, cbd_re, cbd_im, lam_re, lam_im, coef_re, coef_im)


def _ssm_bwd(name, dy, h_re, h_im, u, bbd, bbdt_re, bbdt_im, cbdt_re, cbdt_im, lam_re, lam_im,
             coef_re, coef_im, Lc, reverse):
    T, W = u.shape
    nslab = W // SLAB_CH
    NS = nslab * SLAB_ST
    tt, nt, nc = _ssm_tiles(T, Lc)
    adj_reverse = not reverse
    if reverse:
        tile = lambda s: jnp.where(s < nt - nc, nc + s, s - (nt - nc))
    else:
        tile = lambda s: nt - 1 - s

    def body(dy_ref, hr_ref, hi_ref, u_ref, b_ref, btr_ref, bti_ref, ctr_ref, cti_ref, lr_ref, li_ref,
             kr_ref, ki_ref, du_ref, dlr_ref, dli_ref, dkr_ref, dki_ref, dbf_ref, dcrf_ref, dcif_ref,
             db_ref, dcr_ref, dci_ref, *tb):
        @pl.when(pl.program_id(0) == 0)
        def _():
            _scan_init(lr_ref[...], -li_ref[...], tb, adj_reverse)
            for ref in (dlr_ref, dli_ref, dkr_ref, dki_ref, db_ref, dcr_ref, dci_ref):
                ref[...] = jnp.zeros_like(ref)

        rows = lax.broadcasted_iota(jnp.int32, (tt, 1), 0)
        far_row = tt - 1 if adj_reverse else 0
        tn_dims = (((0,), (0,)), ((), ()))
        for j in range(nslab):
            lanes = slice(j * SLAB_ST, (j + 1) * SLAB_ST)
            chans = slice(j * SLAB_CH, (j + 1) * SLAB_CH)
            dys, us = dy_ref[:, chans], u_ref[:, chans]
            er = jnp.dot(dys, ctr_ref[j], preferred_element_type=F32)
            ei = -jnp.dot(dys, cti_ref[j], preferred_element_type=F32)
            ar, ai, car, cai = _scan_tile(er, ei, tb, lanes, adj_reverse)
            shift = tt - 1 if adj_reverse else 1
            nr = jnp.where(rows == far_row, car, pltpu.roll(ar, shift, 0))
            ni = jnp.where(rows == far_row, cai, pltpu.roll(ai, shift, 0))
            hrb, hib = hr_ref[:, lanes], hi_ref[:, lanes]
            hr, hi = hrb.astype(F32), hib.astype(F32)
            dlr_ref[:, lanes] += jnp.sum(nr * hr + ni * hi, axis=0, keepdims=True)
            dli_ref[:, lanes] += jnp.sum(ni * hr - nr * hi, axis=0, keepdims=True)
            bu = jnp.dot(us, b_ref[j], preferred_element_type=F32)
            br, bi = bu[:, :SLAB_ST], bu[:, SLAB_ST:]
            dkr_ref[:, lanes] += jnp.sum(ar * br + ai * bi, axis=0, keepdims=True)
            dki_ref[:, lanes] += jnp.sum(ai * br - ar * bi, axis=0, keepdims=True)
            kr, ki = kr_ref[:, lanes], ki_ref[:, lanes]
            dbr = (ar * kr + ai * ki).astype(BF16)
            dbi = (ai * kr - ar * ki).astype(BF16)
            du_ref[:, chans] = (jnp.dot(dbr, btr_ref[j], preferred_element_type=F32)
                                + jnp.dot(dbi, bti_ref[j], preferred_element_type=F32))
            db_ref[j, :, :SLAB_ST] += lax.dot_general(us, dbr, tn_dims, preferred_element_type=F32)
            db_ref[j, :, SLAB_ST:] += lax.dot_general(us, dbi, tn_dims, preferred_element_type=F32)
            dcr_ref[j] += lax.dot_general(hrb, dys, tn_dims, preferred_element_type=F32)
            dci_ref[j] -= lax.dot_general(hib, dys, tn_dims, preferred_element_type=F32)

        @pl.when(pl.program_id(0) == nt - 1)
        def _():
            def iota(shape, axis):
                return lax.broadcasted_iota(jnp.int32, shape, axis)

            sg, ss = SSM_GROUP.bit_length() - 1, SSM_STATE.bit_length() - 1
            b_mask = (iota((SLAB_CH, SLAB_ST), 0) >> sg) == (iota((SLAB_CH, SLAB_ST), 1) >> ss)
            c_mask = (iota((SLAB_ST, SLAB_CH), 0) >> ss) == (iota((SLAB_ST, SLAB_CH), 1) >> sg)
            fold = jnp.where((iota((SLAB_ST, SSM_STATE), 0) & (SSM_STATE - 1)) == iota((SLAB_ST, SSM_STATE), 1),
                             1.0, 0.0).astype(BF16)
            fold_t = jnp.where((iota((SSM_STATE, SLAB_ST), 1) & (SSM_STATE - 1)) == iota((SSM_STATE, SLAB_ST), 0),
                               1.0, 0.0).astype(BF16)

            def exact_dot(a, b, a_is_value):
                terms = _split3(a if a_is_value else b)
                acc = None
                for t in terms:
                    part = jnp.dot(t, b, preferred_element_type=F32) if a_is_value else jnp.dot(a, t, preferred_element_type=F32)
                    acc = part if acc is None else acc + part
                return acc

            for j in range(nslab):
                dbj = db_ref[j]
                dbf_ref[j, :, :SSM_STATE] = exact_dot(jnp.where(b_mask, dbj[:, :SLAB_ST], 0.0), fold, True)
                dbf_ref[j, :, SSM_STATE:] = exact_dot(jnp.where(b_mask, dbj[:, SLAB_ST:], 0.0), fold, True)
                dcrf_ref[j] = exact_dot(fold_t, jnp.where(c_mask, dcr_ref[j], 0.0), False)
                dcif_ref[j] = exact_dot(fold_t, jnp.where(c_mask, dci_ref[j], 0.0), False)

    whole3 = lambda arr: pl.BlockSpec(arr.shape, lambda s: (0, 0, 0))
    vec = pl.BlockSpec((1, NS), lambda s: (0, 0))
    row_w = pl.BlockSpec((tt, W), lambda s: (tile(s), 0))
    row_s = pl.BlockSpec((tt, NS), lambda s: (tile(s), 0))
    dbf = jax.ShapeDtypeStruct((nslab, SLAB_CH, 2 * SSM_STATE), F32)
    dcf = jax.ShapeDtypeStruct((nslab, SSM_STATE, SLAB_CH), F32)
    return pl.pallas_call(
        body, name=name, grid=(nt,),
        in_specs=[row_w, row_s, row_s, row_w, whole3(bbd), whole3(bbdt_re), whole3(bbdt_im), whole3(cbdt_re),
                  whole3(cbdt_im), vec, vec, vec, vec],
        out_specs=[row_w, vec, vec, vec, vec, whole3(dbf), whole3(dcf), whole3(dcf)],
        out_shape=[jax.ShapeDtypeStruct((T, W), F32)] + [jax.ShapeDtypeStruct((1, NS), F32)] * 4 + [dbf, dcf, dcf],
        scratch_shapes=[pltpu.VMEM(bbd.shape, F32), pltpu.VMEM(bbdt_re.shape, F32), pltpu.VMEM(bbdt_re.shape, F32)]
        + _scan_scratch(tt, NS),
        compiler_params=_cparams())(dy, h_re, h_im, u, bbd, bbdt_re, bbdt_im, cbdt_re, cbdt_im,
                                    lam_re, lam_im, coef_re, coef_im)


def _zoh_math(a_re, a_im, log_dt):
    dt = jnp.exp(log_dt)
    mag = jnp.exp(a_re * dt)
    lb_re = mag * jnp.cos(a_im * dt)
    lb_im = mag * jnp.sin(a_im * dt)
    den = a_re * a_re + a_im * a_im
    coef_re = ((lb_re - 1.0) * a_re + lb_im * a_im) / den
    coef_im = (lb_im * a_re - (lb_re - 1.0) * a_im) / den
    return lb_re, lb_im, coef_re, coef_im


def _zoh_fwd(a_re, a_im, log_dt):
    def body(ar, ai, ld, o0, o1, o2, o3):
        for ref, val in zip((o0, o1, o2, o3), _zoh_math(ar[...], ai[...], ld[...])):
            ref[...] = val

    return pl.pallas_call(body, name="zoh_fwd", out_shape=[jax.ShapeDtypeStruct(a_re.shape, F32)] * 4,
                          compiler_params=_cparams())(a_re, a_im, log_dt)


def _zoh_bwd(a_re, a_im, log_dt, cots):
    def body(ar, ai, ld, c0, c1, c2, c3, o0, o1, o2):
        _, vjp = jax.vjp(_zoh_math, ar[...], ai[...], ld[...])
        for ref, val in zip((o0, o1, o2), vjp((c0[...], c1[...], c2[...], c3[...]))):
            ref[...] = val

    return pl.pallas_call(
        body, name="zoh_bwd",
        out_shape=[jax.ShapeDtypeStruct(a_re.shape, F32), jax.ShapeDtypeStruct(a_re.shape, F32),
                   jax.ShapeDtypeStruct(log_dt.shape, F32)],
        compiler_params=_cparams())(a_re, a_im, log_dt, *cots)


def _outer_sum(acts, cots):
    D, N = acts.shape[1], cots.shape[1]
    tm, tn = _div(D, 512), _div(N, 1152)
    dims = (((0,), (0,)), ((), ()))

    def body(a_ref, b_ref, o_ref):
        a = a_ref[...]
        aa = _split3(a * _sigmoid(a))
        bb = _split3(b_ref[...])
        acc = None
        for ia in range(3):
            for ib in range(3 - ia):
                t = lax.dot_general(aa[ia], bb[ib], dims, preferred_element_type=F32)
                acc = t if acc is None else acc + t
        o_ref[...] = acc

    return pl.pallas_call(
        body, name="mod_dw", grid=(D // tm, N // tn),
        in_specs=[pl.BlockSpec((16, tm), lambda i, j: (0, i)), pl.BlockSpec((16, tn), lambda i, j: (0, j))],
        out_specs=pl.BlockSpec((tm, tn), lambda i, j: (i, j)),
        out_shape=jax.ShapeDtypeStruct((D, N), F32), compiler_params=_cparams())(acts, cots)


def _adamw_math(w, g, m, v):
    m = ADAM_B1 * m + (1.0 - ADAM_B1) * g
    v = ADAM_B2 * v + (1.0 - ADAM_B2) * (g * g)
    m_hat = m / (1.0 - ADAM_B1 ** ADAM_STEP)
    v_hat = v / (1.0 - ADAM_B2 ** ADAM_STEP)
    delta = -ADAM_LR * (m_hat / (jnp.sqrt(v_hat) + ADAM_EPS) + ADAM_WD * w)
    return delta, m, v


def _adamw(name, w, m, v, gparts):
    R, C = w.shape[-2:]
    kind = 'row1' if w.ndim == 3 else 'row'
    tr = _div(R, max(8, 262144 // C), mult=8)

    def fn(i, wv, mv, vv, *gs):
        g = gs[0]
        for extra in gs[1:]:
            g = g + extra
        return (g,) + _adamw_math(wv, g, mv, vv)

    return _rowk(name, fn, R, tr, [(w, kind), (m, kind), (v, kind)] + [(g, 'row') for g in gparts],
                 [(w.shape, F32, kind)] * 4)


def _pack(pieces, rows_mult=8):
    flat = jnp.concatenate([p.reshape(-1).astype(F32) for p in pieces])
    unit = rows_mult * PACK_W
    total = -(-flat.shape[0] // unit) * unit
    return jnp.pad(flat, (0, total - flat.shape[0])).reshape(total // PACK_W, PACK_W)


def _unpack(buf, shapes):
    flat = buf.reshape(-1)
    out, off = [], 0
    for s in shapes:
        n = math.prod(s)
        out.append(flat[off:off + n].reshape(s))
        off += n
    return out


def _bd_expand(t):
    S, g, a, b = t.shape
    eye = jnp.eye(g, dtype=t.dtype)
    return (t[:, :, :, None, :] * eye[None, :, None, :, None]).reshape(S, g * a, g * b)


def _rope_tables(L, Lc):
    rows = L // GRID_W
    row_ids = jnp.broadcast_to(jnp.arange(rows)[:, None], (rows, GRID_W)).reshape(-1).astype(F32)
    col_ids = jnp.broadcast_to(jnp.arange(GRID_W)[None, :], (rows, GRID_W)).reshape(-1).astype(F32)
    quarter = HEAD_DIM // 4
    inv_freq = ROPE_THETA ** (-jnp.arange(quarter, dtype=F32) / quarter)
    ang_r = row_ids[:, None] * inv_freq
    ang_c = col_ids[:, None] * inv_freq
    cos = jnp.concatenate([jnp.cos(ang_r), jnp.cos(ang_r), jnp.cos(ang_c), jnp.cos(ang_c)], axis=1)
    sin = jnp.concatenate([-jnp.sin(ang_r), jnp.sin(ang_r), -jnp.sin(ang_c), jnp.sin(ang_c)], axis=1)
    cos = jnp.concatenate([jnp.ones((Lc, HEAD_DIM), F32), cos], axis=0)
    sin = jnp.concatenate([jnp.zeros((Lc, HEAD_DIM), F32), sin], axis=0)
    return cos, sin


def _rot(v):
    lane = lax.broadcasted_iota(jnp.int32, (1, HEAD_DIM), 1)
    first = (lane % (HEAD_DIM // 2)) < (HEAD_DIM // 4)
    return jnp.where(first, pltpu.roll(v, HEAD_DIM - HEAD_DIM // 4, 1), pltpu.roll(v, HEAD_DIM // 4, 1))


def _head_norm(xh, g):
    return xh * lax.rsqrt(jnp.mean(xh * xh, axis=-1, keepdims=True) + NORM_EPS) * g


def _norm_mod(xv, g, sh, sc):
    r = lax.rsqrt(jnp.mean(xv * xv, axis=-1, keepdims=True) + NORM_EPS)
    return (xv * r) * g * (1.0 + sc) + sh


def kernel(x, c, ctx, c_ctx, w_mod, b_mod, norm_g, w_ffn1_gate, w_ffn1_up, w_ffn1_down, w_in, q_norm_g, k_norm_g, ssm_a_re, ssm_a_im, ssm_log_dt, ssm_b_re, ssm_b_im, ssm_c_re, ssm_c_im, ssm_d, w_glu, b_glu, w_br_attn, w_br_ssm, w_out, w_ffn2_gate, w_ffn2_up, w_ffn2_down, loss_target, m_c_ctx, m_w_mod, m_b_mod, m_norm_g, m_w_ffn1_gate, m_w_ffn1_up, m_w_ffn1_down, m_w_in, m_q_norm_g, m_k_norm_g, m_ssm_a_re, m_ssm_a_im, m_ssm_log_dt, m_ssm_b_re, m_ssm_b_im, m_ssm_c_re, m_ssm_c_im, m_ssm_d, m_w_glu, m_b_glu, m_w_br_attn, m_w_br_ssm, m_w_out, m_w_ffn2_gate, m_w_ffn2_up, m_w_ffn2_down, v_c_ctx, v_w_mod, v_b_mod, v_norm_g, v_w_ffn1_gate, v_w_ffn1_up, v_w_ffn1_down, v_w_in, v_q_norm_g, v_k_norm_g, v_ssm_a_re, v_ssm_a_im, v_ssm_log_dt, v_ssm_b_re, v_ssm_b_im, v_ssm_c_re, v_ssm_c_im, v_ssm_d, v_w_glu, v_b_glu, v_w_br_attn, v_w_br_ssm, v_w_out, v_w_ffn2_gate, v_w_ffn2_up, v_w_ffn2_down):
    A = dict(locals())
    xi, yi, ci = _mesh_pos()
    chip = 2 * xi + yi
    me = 4 * xi + 2 * yi + ci
    L, D = x.shape[1], x.shape[2]
    Lc = ctx.shape[1]
    T = L + Lc
    F4 = w_ffn1_gate.shape[2]
    F = N_CHIPS * F4
    W, KV, Dq = D // 2, D // 4, D // 4
    G = W // SSM_GROUP
    P, E = SSM_STATE, SSM_GROUP
    NS = G * P
    nslab = W // SLAB_CH
    tr = min(256, Lc)
    ncr = Lc // tr
    assert L % tr == 0 and Lc % tr == 0 and W % SLAB_CH == 0 and D % (4 * LANES) == 0

    def sel(i, v):
        return v if v.shape[0] == 1 else jnp.where(i < ncr, v[0:1], v[1:2])

    def put(i, v, nrow):
        if nrow == 1:
            return v
        which = (i >= ncr).astype(jnp.int32)
        r2 = lax.broadcasted_iota(jnp.int32, (nrow, 1), 0)
        return jnp.where(r2 == which, jnp.broadcast_to(v, (nrow, v.shape[1])), 0.0)

    ident = lambda accs, rows, vecs, ri: [accs[0]]

    NM = w_mod.shape[2]
    first = jnp.zeros((8, D), F32).at[0].set(c[0]).at[1:4, :Dq].set(norm_g[0])
    g0 = _allgather_small("gather_c", first).reshape(N_CHIPS, 2, 8, D)
    c_all = g0[:, :, 0].reshape(N_DEV, D)
    ng = jnp.transpose(g0[:, 0, 1:4, :Dq], (1, 0, 2)).reshape(3, D)
    acts = jnp.concatenate([c_all, c_ctx[None], jnp.zeros((7, D), F32)], axis=0)
    wm = w_mod[0]
    b_shard = lax.dynamic_slice(b_mod[0], (chip * NM,), (NM,))[None]
    silu_bf = lambda a: (a * _sigmoid(a)).astype(BF16)
    to_bf = lambda b: b.astype(BF16)
    mod_part = _mm("mod_fwd", [(acts, wm, D)], 16, NM, tm=16, tn=_div(NM, 1152),
                   epi=lambda accs, rows, vecs, ri: [accs[0] + vecs[0]], outs=[(F32, False)],
                   vecs=[b_shard], a_pro=silu_bf, b_pro=to_bf)[0]
    mg = _allgather_small("gather_mod", mod_part).reshape(N_CHIPS, 2, 16, NM)[:, 0]
    mod_all = jnp.transpose(mg, (1, 0, 2)).reshape(16, N_CHIPS * NM)
    mod_x = lax.dynamic_slice(mod_all, (me, 0), (1, 9 * D))
    mod_c = jnp.where(jnp.arange(9 * D)[None] < 5 * D, mod_all[8:9], 0.0)
    modv = jnp.concatenate([mod_c, mod_x], axis=0)
    mv = lambda k: modv[:, k * D:(k + 1) * D]
    sh1, sc1, g1, sh2, sc2 = mv(0), mv(1), mv(2), mv(3), mv(4)
    g2, sh3, sc3, g3 = mv(5)[1:2], mv(6)[1:2], mv(7)[1:2], mv(8)[1:2]

    big = ['w_ffn1_gate', 'w_ffn1_up', 'w_ffn1_down', 'w_ffn2_gate', 'w_ffn2_up', 'w_ffn2_down',
           'w_in', 'w_glu', 'w_br_attn', 'w_br_ssm', 'w_out']
    row_sharded = {'w_ffn1_down', 'w_ffn2_down', 'w_glu', 'w_br_attn', 'w_out'}
    groups = [big[0:2], big[2:3], big[6:7], big[7:11], big[3:6]]
    chip_index = jnp.reshape(chip, (1,)).astype(jnp.int32)
    slots = {n: _cast_slot("cast_" + n, A[n], chip_index) for n in big}
    tok, gather_finish = modv, []
    for gi, names in enumerate(groups):
        tok, fin = _gather_split("gather_w%d" % gi, [slots[n] for n in names], tok)
        gather_finish.append(fin)
    ng = ng + tok[0:1, 0:1]
    Wt = {}

    def weights_ready(gi, after_work):
        names = groups[gi]
        lands = gather_finish[gi](after_work)
        full = _gather_finish("gather_w%d_pass" % gi, lands)
        for n, gw in zip(names, full):
            Wt[n] = gw.reshape(N_CHIPS * gw.shape[1], gw.shape[2]) if n in row_sharded else gw

    a_re2, a_im2 = ssm_a_re[0].reshape(2 * G, P), ssm_a_im[0].reshape(2 * G, P)
    ldt2 = ssm_log_dt[0].reshape(2 * G, 1)
    zoh = _zoh_fwd(a_re2, a_im2, ldt2)
    lam_re, lam_im, coef_re, coef_im = [[z[d * G:(d + 1) * G].reshape(1, NS) for d in range(2)] for z in zoh]
    bd_b = lambda b: _bd_expand(jnp.transpose(b, (0, 2, 1)).reshape(nslab, SLAB_GROUPS, E, P))
    bd_c = lambda cc: _bd_expand(jnp.transpose(cc, (0, 2, 1)).reshape(nslab, SLAB_GROUPS, P, E))
    bbd, bbdt_re, bbdt_im, cbd_re, cbd_im, cbdt_re, cbdt_im = [], [], [], [], [], [], []
    for d in range(2):
        br_, bi_ = bd_b(ssm_b_re[0, d]).astype(BF16), bd_b(ssm_b_im[0, d]).astype(BF16)
        cr_, ci_ = bd_c(ssm_c_re[0, d]).astype(BF16), bd_c(ssm_c_im[0, d]).astype(BF16)
        bbd.append(jnp.concatenate([br_, bi_], axis=2))
        bbdt_re.append(jnp.transpose(br_, (0, 2, 1)))
        bbdt_im.append(jnp.transpose(bi_, (0, 2, 1)))
        cbd_re.append(cr_)
        cbd_im.append(ci_)
        cbdt_re.append(jnp.transpose(cr_, (0, 2, 1)))
        cbdt_im.append(jnp.transpose(ci_, (0, 2, 1)))
    cos_t, sin_t = _rope_tables(L, Lc)
    qg, kg = q_norm_g, k_norm_g
    small = ['c_ctx', 'b_mod', 'norm_g', 'q_norm_g', 'k_norm_g', 'ssm_a_re', 'ssm_a_im', 'ssm_log_dt', 'ssm_b_re',
             'ssm_b_im', 'ssm_c_re', 'ssm_c_im', 'ssm_d', 'b_glu']
    packs_wmv = [_pack([A[pre + n] for n in small]) for pre in ('', 'm_', 'v_')]
    prepared = packs_wmv + [cos_t, sin_t, coef_im[0], coef_im[1]] + [
        t[d][0] for t in (bbd, bbdt_re, bbdt_im, cbd_re, cbd_im, cbdt_re, cbdt_im) for d in range(2)]
    weights_ready(0, tok + sum(t[0:1, 0:1].astype(F32) for t in prepared))

    def norm_mod(name, xv, g, sh, sc):
        rows = xv.shape[0]
        return _rowk(name, lambda i, xt, gt, sht, sct: [_norm_mod(xt, gt, sel(i, sht), sel(i, sct))],
                     rows, tr, [(xv, 'row'), (g, 'vec'), (sh, 'vec'), (sc, 'vec')], [((rows, D), BF16, 'row')])[0]

    def swiglu_epi(accs, rows, vecs, ri):
        a_, b_ = accs
        return [a_, b_, a_ * _sigmoid(a_) * b_]

    def res_epi(coef):
        def epi(accs, rows, vecs, ri):
            gate = vecs[0]
            if gate.shape[0] == 2:
                gate = jnp.where(ri < Lc, gate[0:1], gate[1:2])
            return [accs[0], rows[0] + (coef * gate) * accs[0]]
        return epi

    def ffn_fwd(tag, h, xres, gate, down_ready=None):
        rows = h.shape[0]
        a_, b_, s_ = _mm(tag + "_up", [(h, Wt['w_' + tag + '_gate'], D), (h, Wt['w_' + tag + '_up'], D)], rows, F,
                         tm=_div(rows, 256), tn=F4, epi=swiglu_epi, outs=[(F32, False), (F32, False), (BF16, False)])
        if down_ready is not None:
            down_ready(s_)
        f_, xo = _mm(tag + "_down", [(s_, Wt['w_' + tag + '_down'], F)], rows, D, tm=_div(rows, 384),
                     tn=_div(D, 512), epi=res_epi(0.5), outs=[(F32, False), (F32, False)],
                     rows=[(xres, 0, 0)], vecs=[gate])
        return a_, b_, s_, f_, xo

    xc = jnp.concatenate([ctx[0], x[0]], axis=0)
    h1 = norm_mod("norm1", xc, ng[0:1], sh1, sc1)
    a1, b1, s1, f1, x1 = ffn_fwd("ffn1", h1, xc, g1, down_ready=lambda s_: weights_ready(1, s_))
    weights_ready(2, x1)
    h2 = norm_mod("norm2", x1, ng[1:2], sh2, sc2)
    proj = _mm("in_proj", [(h2, Wt['w_in'], D)], T, 4 * D, tm=_div(T, 768), tn=_div(D, 1024), epi=ident,
               outs=[(F32, False)])[0]
    nh, nkvh = D // HEAD_DIM, KV // HEAD_DIM

    def prep_fn(i, kt, vt, ut, qt, qgt, kgt, ct, st):
        qs = [_head_norm(qt[:, h * HEAD_DIM:(h + 1) * HEAD_DIM], qgt) for h in range(nh)]
        ks = [_head_norm(kt[:, h * HEAD_DIM:(h + 1) * HEAD_DIM], kgt) for h in range(nkvh)]
        qs = [v * ct + _rot(v) * st for v in qs]
        ks = [v * ct + _rot(v) * st for v in ks]
        return [jnp.concatenate(qs, axis=1), jnp.concatenate(ks, axis=1), vt, ut]

    qr, kr, vb, ub = _rowk(
        "qk_prep", prep_fn, T, tr,
        [(proj, ('col', KV, 0)), (proj, ('col', KV, 1)), (proj, ('col', W, 1)), (proj, ('col', D, 1)),
         (qg, 'vec'), (kg, 'vec'), (cos_t, 'row'), (sin_t, 'row')],
        [((T, D), BF16, 'row'), ((T, KV), BF16, 'row'), ((T, KV), BF16, 'row'), ((T, W), BF16, 'row')])
    attn = _attn_fwd(qr, kr, vb, L, Lc, D)
    weights_ready(3, attn)
    hs_re, hs_im, ys = [], [], []
    for d in range(2):
        hr_, hi_, y_ = _ssm_fwd("ssm_fwd%d" % d, ub, bbd[d], cbd_re[d], cbd_im[d], lam_re[d], lam_im[d],
                                coef_re[d], coef_im[d], Lc, reverse=bool(d))
        hs_re.append(hr_)
        hs_im.append(hi_)
        ys.append(y_)

    def ssm_out_fn(i, y0, y1, ut, dt):
        pre = dt * ut + y0 + y1
        yg_ = _gelu(pre)
        return [pre, yg_, yg_]

    ssm_pre, yg, ygb = _rowk(
        "ssm_out", ssm_out_fn, L, tr,
        [(ys[0], 'orow'), (ys[1], 'orow'), (proj, ('ocol', W, 1)), (ssm_d, 'vec')],
        [((L, W), F32, 'row'), ((L, W), F32, 'row'), ((L, W), BF16, 'row')], nc=ncr)

    def glu_epi(accs, rows, vecs, ri):
        z_ = accs[0] + vecs[0]
        return [z_, rows[0] * _sigmoid(z_)]

    zglu, y2 = _mm("glu", [(ygb, Wt['w_glu'], W)], L, W, tm=_div(L, 512), tn=_div(W, 512), epi=glu_epi,
                   outs=[(F32, False), (BF16, False)], rows=[(yg, 0, 0)], vecs=[b_glu])
    tnm = _div(Dq, 512)

    def merge_epi(accs, rows, vecs, ri):
        ga, gs = _sigmoid(rows[0]), _sigmoid(rows[1])
        return [accs[0], accs[1], ga * accs[0] + gs * accs[1]]

    ba, bs, merged = _mm("merge", [(attn, Wt['w_br_attn'], D), (y2, Wt['w_br_ssm'], W)], L, D, tm=tr, tn=tnm,
                         epi=merge_epi, outs=[(F32, False), (F32, False), (BF16, False)],
                         rows=[(proj, ncr, 2 * D // tnm), (proj, ncr, 3 * D // tnm)])
    mix, x2 = _mm("out_proj", [(merged, Wt['w_out'], D)], L, D, tm=tr, tn=_div(D, 1024), epi=res_epi(1.0),
                  outs=[(F32, False), (F32, False)], rows=[(x1, ncr, 0)], vecs=[g2])
    weights_ready(4, x2)
    h3 = norm_mod("norm3", x2, ng[2:3], sh3, sc3)
    a3, b3, s3, f3, x3 = ffn_fwd("ffn2", h3, x2, g3)

    def loss_fn(i, yt, tt_):
        diff = yt - tt_
        return [diff * (1.0 / D), jnp.sum(diff * diff, axis=0, keepdims=True)]

    dy, sq = _rowk("loss", loss_fn, L, tr, [(x3, 'row'), (loss_target[0], 'row')],
                   [((L, D), F32, 'row'), ((1, D), F32, 'acc')])
    loss = lax.psum(0.5 * jnp.sum(sq) / D, ("x", "y", "c"))

    def res_bwd(name, dxo, f_, gate, coef):
        rows, nrow = dxo.shape[0], gate.shape[0]

        def fn(i, dt, ft, gt):
            return [(coef * sel(i, gt)) * dt, put(i, jnp.sum(dt * ft, axis=0, keepdims=True) * coef, nrow)]

        return _rowk(name, fn, rows, tr, [(dxo, 'row'), (f_, 'row'), (gate, 'vec')],
                     [((rows, D), BF16, 'row'), ((nrow, D), F32, 'acc')])

    def swiglu_bwd_epi(accs, rows, vecs, ri):
        ds_, a_, b_ = accs[0], rows[0], rows[1]
        sg = _sigmoid(a_)
        return [ds_ * b_ * (sg * (1.0 + a_ * (1.0 - sg))), ds_ * (a_ * sg)]

    def norm_mod_bwd(name, xv, g, sh, sc, dh, dres, dres_kind):
        rows, nrow = xv.shape[0], sh.shape[0]

        def fn(i, xt, gt, sht, sct, dht, rest):
            _, vjp = jax.vjp(_norm_mod, xt, gt, sel(i, sht), sel(i, sct))
            dx_, dg_, dsh_, dsc_ = vjp(dht)
            dx_ = dx_ + (jnp.where(i >= ncr, rest, 0.0) if dres_kind == 'xrow' else rest)
            return [dx_, dg_, put(i, dsh_, nrow), put(i, dsc_, nrow)]

        return _rowk(name, fn, rows, tr,
                     [(xv, 'row'), (g, 'vec'), (sh, 'vec'), (sc, 'vec'), (dh, 'row'), (dres, dres_kind)],
                     [((rows, D), F32, 'row'), ((1, D), F32, 'acc'), ((nrow, D), F32, 'acc'), ((nrow, D), F32, 'acc')],
                     nc=ncr)

    def ffn_bwd(tag, dxo, h, a_, b_, s_, f_, gate, wg, wu, wd, on_dwd=None):
        rows = dxo.shape[0]
        df, dgate = res_bwd(tag + "_dres", dxo, f_, gate, 0.5)
        dwd = _mm(tag + "_dwd", [(s_, df, rows)], F, D, tm=_div(F, 512), tn=_div(D, 1024), ta=True, epi=ident,
                  outs=[(BF16, False)])[0].reshape(N_CHIPS, F4, D)
        if on_dwd is not None:
            on_dwd(dwd)
        da, db = _mm(tag + "_dact", [(df, wd, D)], rows, F, tm=_div(rows, 384), tn=F4, tb=True, epi=swiglu_bwd_epi,
                     outs=[(BF16, False), (BF16, False)], rows=[(a_, 0, 0), (b_, 0, 0)])
        dwg = _mm(tag + "_dwg", [(h, da, rows)], D, F, tm=_div(D, 512), tn=F4, ta=True, epi=ident,
                  outs=[(BF16, True)])[0]
        dwu = _mm(tag + "_dwu", [(h, db, rows)], D, F, tm=_div(D, 512), tn=F4, ta=True, epi=ident,
                  outs=[(BF16, True)])[0]
        dh = _mm(tag + "_dh", [(da, wg, F), (db, wu, F)], rows, D, tm=_div(rows, 768), tn=_div(D, 1024), nk=N_CHIPS,
                 tb=True, epi=ident, outs=[(F32, False)], summed=True)[0]
        return dh, dgate, dwg, dwu, dwd

    dh3, dg3, dwg2, dwu2, dwd2 = ffn_bwd("ffn2", dy, h3, a3, b3, s3, f3, g3, Wt['w_ffn2_gate'], Wt['w_ffn2_up'],
                                         Wt['w_ffn2_down'])
    tok_r1, scatter_fin1 = _scatter_split("scatter_ffn2", [dwg2, dwu2, dwd2], dg3)
    dx2, dng3, dsh3, dsc3 = norm_mod_bwd("norm3_bwd", x2, ng[2:3], sh3, sc3, dh3, dy, 'row')
    dmix, dg2 = res_bwd("mix_dres", dx2, mix, g2 + tok_r1[0:1, 0:1], 1.0)

    def dmerge_epi(accs, rows, vecs, ri):
        dm_, ba_, bs_ = accs[0], rows[0], rows[1]
        ga, gs = _sigmoid(rows[2]), _sigmoid(rows[3])
        return [dm_ * ga, dm_ * gs, dm_ * ba_ * ga * (1.0 - ga), dm_ * bs_ * gs * (1.0 - gs)]

    dba, dbs, dga, dgs = _mm("dmerge", [(dmix, Wt['w_out'], D)], L, D, tm=tr, tn=tnm, tb=True, epi=dmerge_epi,
                             outs=[(BF16, False)] * 4,
                             rows=[(ba, 0, 0), (bs, 0, 0), (proj, ncr, 2 * D // tnm), (proj, ncr, 3 * D // tnm)])
    dwout = _mm("dw_out", [(merged, dmix, L)], D, D, tm=_div(D, 512), tn=_div(D, 1024), ta=True, epi=ident,
                outs=[(BF16, False)])[0].reshape(N_CHIPS, Dq, D)
    dattn = _mm("dattn", [(dba, Wt['w_br_attn'], D)], L, D, tm=_div(L, 512), tn=_div(D, 1024), tb=True, epi=ident,
                outs=[(BF16, False)])[0]
    dwba = _mm("dw_br_attn", [(attn, dba, L)], D, D, tm=_div(D, 512), tn=_div(D, 1024), ta=True, epi=ident,
               outs=[(BF16, False)])[0].reshape(N_CHIPS, Dq, D)
    dy2 = _mm("dy2", [(dbs, Wt['w_br_ssm'], D)], L, W, tm=_div(L, 512), tn=_div(W, 1024), nk=N_CHIPS, tb=True,
              epi=ident, outs=[(F32, False)])[0]
    dwbs = _mm("dw_br_ssm", [(y2, dbs, L)], W, D, tm=_div(W, 512), tn=_div(Dq, 512), ta=True, epi=ident,
               outs=[(BF16, True)])[0]

    def glu_bwd_fn(i, d2, ygt, zt):
        sz = _sigmoid(zt)
        dz_ = d2 * ygt * sz * (1.0 - sz)
        return [dz_, d2 * sz, jnp.sum(dz_, axis=0, keepdims=True)]

    dz, dyd, dbglu = _rowk("glu_bwd", glu_bwd_fn, L, tr, [(dy2, 'row'), (yg, 'row'), (zglu, 'row')],
                           [((L, W), BF16, 'row'), ((L, W), F32, 'row'), ((1, W), F32, 'acc')])

    def dssm_epi(accs, rows, vecs, ri):
        _, vjp = jax.vjp(_gelu, rows[1])
        ds_ = vjp(accs[0] + rows[0])[0]
        return [ds_, ds_]

    dssm, dssm_b = _mm("dssm", [(dz, Wt['w_glu'], W)], L, W, tm=_div(L, 512), tn=_div(W, 512), tb=True, epi=dssm_epi,
                       outs=[(F32, False), (BF16, False)], rows=[(dyd, 0, 0), (ssm_pre, 0, 0)])
    dwglu = _mm("dw_glu", [(ygb, dz, L)], W, W, tm=_div(W, 512), tn=_div(W, 1024), ta=True, epi=ident,
                outs=[(BF16, False)])[0].reshape(N_CHIPS, W // N_CHIPS, W)
    tok_r2a, scatter_fin2a = _scatter_split("scatter_mix", [dwglu, dwba, dwbs, dwout], dbglu)
    dssm_full = jnp.concatenate([jnp.zeros((Lc, W), BF16), dssm_b], axis=0)
    dus, dlam_re, dlam_im, dcoef_re, dcoef_im, dbf, dcf_re, dcf_im = [], [], [], [], [], [], [], []
    for d in range(2):
        r = _ssm_bwd("ssm_bwd%d" % d, dssm_full, hs_re[d], hs_im[d], ub, bbd[d], bbdt_re[d], bbdt_im[d],
                     cbdt_re[d], cbdt_im[d], lam_re[d] + tok_r2a[0:1, 0:1], lam_im[d], coef_re[d], coef_im[d], Lc,
                     reverse=bool(d))
        for lst, val in zip((dus, dlam_re, dlam_im, dcoef_re, dcoef_im, dbf, dcf_re, dcf_im), r):
            lst.append(val)
    dqr, dkr, dvf = _attn_bwd(qr, kr, vb, dattn, L, Lc, D)

    def prep_bwd_fn(i, qt, kt, ut, dqt, dkt, dvt, du0, du1, dst, dt, qgt, kgt, ct, st):
        live = i >= ncr
        dqt = jnp.where(live, dqt, 0.0)
        dst = jnp.where(live, dst, 0.0)
        dqs, dks = [], []
        dqg_ = jnp.zeros((1, HEAD_DIM), F32)
        dkg_ = jnp.zeros((1, HEAD_DIM), F32)
        for h in range(nh):
            hl = slice(h * HEAD_DIM, (h + 1) * HEAD_DIM)
            dn = dqt[:, hl] * ct + _rot(dqt[:, hl] * st)
            _, vjp = jax.vjp(_head_norm, qt[:, hl], qgt)
            dxh, dgh = vjp(dn)
            dqs.append(dxh)
            dqg_ = dqg_ + dgh
        for h in range(nkvh):
            hl = slice(h * HEAD_DIM, (h + 1) * HEAD_DIM)
            dn = dkt[:, hl] * ct + _rot(dkt[:, hl] * st)
            _, vjp = jax.vjp(_head_norm, kt[:, hl], kgt)
            dxh, dgh = vjp(dn)
            dks.append(dxh)
            dkg_ = dkg_ + dgh
        du_ = du0 + du1 + dst * dt
        return [jnp.concatenate(dqs, axis=1), jnp.concatenate(dks, axis=1), dvt, du_, dqg_, dkg_,
                jnp.sum(dst * ut, axis=0, keepdims=True)]

    dq_b, dk_b, dv_b, du_b, dqg, dkg, dssd = _rowk(
        "qk_prep_bwd", prep_bwd_fn, T, tr,
        [(proj, ('col', D, 1)), (proj, ('col', KV, 0)), (proj, ('col', W, 1)), (dqr, 'xrow'), (dkr, 'row'),
         (dvf, 'row'), (dus[0], 'row'), (dus[1], 'row'), (dssm, 'xrow'), (ssm_d, 'vec'), (qg, 'vec'), (kg, 'vec'),
         (cos_t, 'row'), (sin_t, 'row')],
        [((T, D), BF16, 'row'), ((T, KV), BF16, 'row'), ((T, KV), BF16, 'row'), ((T, W), BF16, 'row'),
         ((1, HEAD_DIM), F32, 'acc'), ((1, HEAD_DIM), F32, 'acc'), ((1, W), F32, 'acc')], nc=ncr)
    dgate = jnp.concatenate([jnp.zeros((Lc, 2 * D), BF16), jnp.concatenate([dga, dgs], axis=1)], axis=0)
    dproj = jnp.concatenate([dk_b, dv_b, du_b, dq_b, dgate], axis=1)
    dh2 = _mm("in_proj_dx", [(dproj, Wt['w_in'], 4 * D)], T, D, tm=_div(T, 768), tn=_div(D, 1024), nk=N_CHIPS, tb=True,
              epi=ident, outs=[(F32, False)])[0]
    dwin = _mm("in_proj_dw", [(h2, dproj, T)], D, 4 * D, tm=_div(D, 512), tn=_div(D, 1024), ta=True, epi=ident,
               outs=[(BF16, True)])[0]
    tok_r2, scatter_fin2 = _scatter_split("scatter_w_in", [dwin], dqg)
    dx1, dng2, dsh2, dsc2 = norm_mod_bwd("norm2_bwd", x1, ng[1:2] + tok_r2[0:1, 0:1], sh2, sc2, dh2, dx2, 'xrow')
    early = {}

    def start_down(dwd):
        early['tok'], early['fin'] = _scatter_split("scatter_ffn1_down", [dwd], dg2)

    dh1, dg1, dwg1, dwu1, dwd1 = ffn_bwd("ffn1", dx1, h1, a1, b1, s1, f1, g1, Wt['w_ffn1_gate'], Wt['w_ffn1_up'],
                                         Wt['w_ffn1_down'], on_dwd=start_down)
    dx0, dng1, dsh1, dsc1 = norm_mod_bwd("norm1_bwd", xc, ng[0:1] + early['tok'][0:1, 0:1], sh1, sc1, dh1, dx1, 'row')
    grad_x = dx0[Lc:][None]

    zD = jnp.zeros((1, D), F32)
    dmod_x = jnp.concatenate([dsh1[1:2], dsc1[1:2], dg1[1:2], dsh2[1:2], dsc2[1:2], dg2, dsh3, dsc3, dg3], axis=1)
    dmod_c = jnp.concatenate([dsh1[0:1], dsc1[0:1], dg1[0:1], dsh2[0:1], dsc2[0:1], zD, zD, zD, zD], axis=1)
    pieces = [dmod_x, dmod_c, dng1, dng2, dng3, dqg, dkg] + dlam_re + dlam_im + dcoef_re + dcoef_im \
        + dbf + dcf_re + dcf_im + [dssd, dbglu]
    shapes = [p_.shape for p_ in pieces]
    pack = _pack(pieces)
    RP = pack.shape[0]
    allp = _allgather_small("gather_small", pack).reshape(N_DEV, RP, PACK_W)

    head_rows = -(-18 * D // PACK_W)
    head = allp[:, :head_rows].reshape(N_DEV, head_rows * PACK_W)
    dmx_all = head[:, :9 * D]

    def sum_rows_fn(i, t):
        s_ = t[0:1]
        for k in range(1, N_DEV):
            s_ = s_ + t[k:k + 1]
        return [s_]

    dmc_sum = _rowk("sum_dmod_c", sum_rows_fn, 1, 1, [(head[:, 9 * D:18 * D], 'vec')], [((1, 9 * D), F32, 'row')])[0]
    cots = jnp.concatenate([dmx_all, dmc_sum, jnp.zeros((7, 9 * D), F32)], axis=0)
    cots_sh = lax.dynamic_slice(cots, (0, chip * NM), (16, NM))
    part = _mm("cctx_part", [(cots_sh[8:16], wm, NM)], 8, D, tm=8, tn=_div(D, 1024), nk=NM // _div(NM, 1152), tb=True,
               epi=ident, outs=[(F32, False)], a_pro=to_bf, b_pro=to_bf)[0]
    parts = _allgather_small("gather_cctx", part).reshape(N_CHIPS, 2, 8, D)[:, 0, 0]

    def cctx_fn(i, pt, ct):
        ds_ = ((pt[0:1] + pt[1:2]) + pt[2:3]) + pt[3:4]
        _, vjp = jax.vjp(lambda v: v * _sigmoid(v), ct)
        return [vjp(ds_)[0]]

    g_cctx = _rowk("cctx_grad", cctx_fn, 1, 1, [(parts, 'vec'), (c_ctx[None], 'row')], [((1, D), F32, 'row')])[0]

    tok_r3, scatter_fin3 = _scatter_split("scatter_ffn1_up", [dwg1, dwu1], g_cctx)
    zero = tok_r3[0:1, 0:1]

    def sum_dev_fn(i, t):
        s_ = t[0]
        for k in range(1, N_DEV):
            s_ = s_ + t[k]
        return [s_]

    tot = _rowk("sum_small", sum_dev_fn, RP, 8, [(allp, 'row3')], [((RP, PACK_W), F32, 'row')])[0]
    (t_dmod_x, t_dmod_c, t_ng1, t_ng2, t_ng3, t_qg, t_kg, t_lr0, t_lr1, t_li0, t_li1, t_kr0, t_kr1, t_ki0, t_ki1,
     t_dbf0, t_dbf1, t_dcr0, t_dcr1, t_dci0, t_dci1, t_d, t_bglu) = _unpack(tot, shapes)
    b_grad = lambda t, lo: jnp.transpose(t[:, :, lo:lo + P].reshape(G, E, P), (0, 2, 1))
    c_grad = lambda t: jnp.transpose(t.reshape(nslab, P, SLAB_GROUPS, E), (0, 2, 3, 1)).reshape(G, E, P)
    cat2 = lambda u0, u1: jnp.concatenate([u0.reshape(G, P), u1.reshape(G, P)], axis=0)
    g_are, g_aim, g_ldt = _zoh_bwd(a_re2 + zero, a_im2, ldt2, [cat2(t_lr0, t_lr1), cat2(t_li0, t_li1),
                                                                cat2(t_kr0, t_kr1), cat2(t_ki0, t_ki1)])
    g_bmod = _rowk("bmod_grad", lambda i, u0, u1: [u0 + u1], 1, 1, [(t_dmod_x, 'row'), (t_dmod_c, 'row')],
                   [((1, 9 * D), F32, 'row')])[0]
    g_wmod = _outer_sum(acts + zero, cots_sh)
    results = {}
    results['w_mod'] = _adamw("adamw_w_mod", w_mod, m_w_mod, v_w_mod, [g_wmod])

    def reduce_group(tag, names, grads, fin, after_work):
        landed = fin(after_work)
        plane = [_sum_plane("sum_" + n, g_, rb, chip_index) for n, g_, rb in zip(names, grads, landed)]
        other = _swap_sibling("swap_" + tag, plane)
        for n, mine, theirs in zip(names, plane, other):
            results[n] = _adamw("adamw_" + n, A[n], A['m_' + n], A['v_' + n], [mine, theirs])

    reduce_group("ffn2", big[3:6], [dwg2, dwu2, dwd2], scatter_fin1, tok_r3)
    reduce_group("mix", big[7:11], [dwglu, dwba, dwbs, dwout], scatter_fin2a, results['w_ffn2_down'][0])
    reduce_group("w_in", big[6:7], [dwin], scatter_fin2, results['w_out'][0])
    reduce_group("ffn1_down", big[2:3], [dwd1], early['fin'], results['w_out'][0])
    reduce_group("ffn1_up", big[0:2], [dwg1, dwu1], scatter_fin3, results['w_ffn1_down'][0])

    ng_full =jnp.concatenate([t_ng1, t_ng2, t_ng3], axis=0)
    gsmall = {
        'c_ctx': g_cctx, 'b_mod': g_bmod, 'norm_g': lax.dynamic_slice(ng_full, (0, chip * Dq), (3, Dq)),
        'q_norm_g': t_qg, 'k_norm_g': t_kg, 'ssm_a_re': g_are, 'ssm_a_im': g_aim, 'ssm_log_dt': g_ldt,
        'ssm_b_re': jnp.stack([b_grad(t_dbf0, 0), b_grad(t_dbf1, 0)]),
        'ssm_b_im': jnp.stack([b_grad(t_dbf0, P), b_grad(t_dbf1, P)]),
        'ssm_c_re': jnp.stack([c_grad(t_dcr0), c_grad(t_dcr1)]), 'ssm_c_im': jnp.stack([c_grad(t_dci0), c_grad(t_dci1)]),
        'ssm_d': t_d, 'b_glu': t_bglu}
    sshapes = [A[n].shape for n in small]
    sres = _adamw("adamw_small", packs_wmv[0], packs_wmv[1], packs_wmv[2], [_pack([gsmall[n] for n in small])])
    sres = [_unpack(b_, sshapes) for b_ in sres]
    for k, n in enumerate(small):
        results[n] = tuple(sres[q][k] for q in range(4))

    order = ['c_ctx', 'w_mod', 'b_mod', 'norm_g', 'w_ffn1_gate', 'w_ffn1_up', 'w_ffn1_down', 'w_in', 'q_norm_g',
             'k_norm_g', 'ssm_a_re', 'ssm_a_im', 'ssm_log_dt', 'ssm_b_re', 'ssm_b_im', 'ssm_c_re', 'ssm_c_im',
             'ssm_d', 'w_glu', 'b_glu', 'w_br_attn', 'w_br_ssm', 'w_out', 'w_ffn2_gate', 'w_ffn2_up', 'w_ffn2_down']
    outs = [loss, grad_x]
    for q in range(4):
        outs += [results[n][q].reshape(A[n].shape) for n in order]
    return tuple(outs)
```

```python
import math

import jax
import jax.numpy as jnp
from jax import lax
from jax.experimental import pallas as pl
from jax.experimental.pallas import tpu as pltpu

F32 = jnp.float32
BF16 = jnp.bfloat16
MESH = pl.DeviceIdType.MESH

NORM_EPS = 1e-6
ROPE_THETA = 10000.0
GRID_W = 64
HEAD_DIM = 128
Q_PER_KV = 4
SSM_GROUP = 16
SSM_STATE = 64
ADAM_LR = 0.001
ADAM_B1 = 0.9
ADAM_B2 = 0.999
ADAM_EPS = 1e-08
ADAM_WD = 0.01
ADAM_STEP = 10

N_CHIPS = 4
N_DEV = 8
LANES = 128
SLAB_CH = 128
SLAB_GROUPS = SLAB_CH // SSM_GROUP
SLAB_ST = SLAB_GROUPS * SSM_STATE
VMEM_LIMIT_BYTES = 56 * 1024 * 1024
PACK_W = 1024


def _cparams(**kw):
    return pltpu.CompilerParams(vmem_limit_bytes=VMEM_LIMIT_BYTES, **kw)


def _div(n, pref, mult=LANES):
    t = (min(pref, n) // mult) * mult
    while t >= mult:
        if n % t == 0:
            return t
        t -= mult
    return n


def _sigmoid(x):
    return jax.nn.sigmoid(x)


def _gelu(x):
    return x * (0.5 * (1.0 + jnp.tanh(math.sqrt(2.0 / math.pi) * (x + 0.044715 * (x * x * x)))))


def _rowk(name, fn, nrows, tr, ins, outs, nc=0):
    nt = nrows // tr
    in_specs, arrays = [], []
    for arr, kind in ins:
        arrays.append(arr)
        if kind == 'row':
            in_specs.append(pl.BlockSpec((tr, arr.shape[1]), lambda i: (i, 0)))
        elif kind == 'xrow':
            in_specs.append(pl.BlockSpec((tr, arr.shape[1]), lambda i: (jnp.maximum(i - nc, 0), 0)))
        elif kind == 'orow':
            in_specs.append(pl.BlockSpec((tr, arr.shape[1]), lambda i: (i + nc, 0)))
        elif kind == 'vec':
            in_specs.append(pl.BlockSpec(arr.shape, lambda i, nd=arr.ndim: (0,) * nd))
        elif kind == 'row3':
            in_specs.append(pl.BlockSpec((arr.shape[0], tr, arr.shape[2]), lambda i: (0, i, 0)))
        elif kind == 'row1':
            in_specs.append(pl.BlockSpec((None, tr, arr.shape[2]), lambda i: (0, i, 0)))
        elif kind[0] == 'ocol':
            _, width, blk = kind
            in_specs.append(pl.BlockSpec((tr, width), lambda i, blk=blk: (i + nc, blk)))
        else:
            _, width, blk = kind
            in_specs.append(pl.BlockSpec((tr, width), lambda i, blk=blk: (i, blk)))
    out_shape, out_specs = [], []
    for shape, dtype, kind in outs:
        out_shape.append(jax.ShapeDtypeStruct(shape, dtype))
        if kind == 'row':
            out_specs.append(pl.BlockSpec((tr, shape[1]), lambda i: (i, 0)))
        elif kind == 'row1':
            out_specs.append(pl.BlockSpec((None, tr, shape[2]), lambda i: (0, i, 0)))
        else:
            out_specs.append(pl.BlockSpec(shape, lambda i, nd=len(shape): (0,) * nd))
    nin = len(ins)

    def body(*refs):
        i = pl.program_id(0)
        res = fn(i, *[r[...] for r in refs[:nin]])
        for (shape, dtype, kind), ref, val in zip(outs, refs[nin:], res):
            if kind in ('row', 'row1'):
                ref[...] = val.astype(dtype)
            else:
                @pl.when(i == 0)
                def _():
                    ref[...] = val.astype(dtype)

                @pl.when(i > 0)
                def _():
                    ref[...] += val.astype(dtype)

    return pl.pallas_call(body, name=name, grid=(nt,), in_specs=in_specs, out_specs=out_specs,
                          out_shape=out_shape, compiler_params=_cparams())(*arrays)


def _mm(name, pairs, M, N, *, tm, tn, nk=1, epi, outs, ta=False, tb=False, rows=(), vecs=(),
        a_pro=None, b_pro=None, n_outer=True, summed=False):
    nm, nn = M // tm, N // tn
    npair = len(pairs)

    def idx(f):
        if n_outer:
            return lambda j, i, k: f(i, j, k)
        return lambda i, j, k: f(i, j, k)

    in_specs, args = [], []
    for a, b, K in pairs:
        tk = K // nk
        if ta:
            in_specs.append(pl.BlockSpec((tk, tm), idx(lambda i, j, k: (k, i))))
        else:
            in_specs.append(pl.BlockSpec((tm, tk), idx(lambda i, j, k: (i, k))))
        args.append(a)
        if b.ndim == 3:
            if tb:
                per = b.shape[2] // tk
                in_specs.append(pl.BlockSpec((None, tn, tk), idx(lambda i, j, k, per=per: (k // per, j, k % per))))
            else:
                per = b.shape[2] // tn
                in_specs.append(pl.BlockSpec((None, tk, tn), idx(lambda i, j, k, per=per: (j // per, k, j % per))))
        elif tb:
            in_specs.append(pl.BlockSpec((tn, tk), idx(lambda i, j, k: (j, k))))
        else:
            in_specs.append(pl.BlockSpec((tk, tn), idx(lambda i, j, k: (k, j))))
        args.append(b)
    for arr, ro, co in rows:
        in_specs.append(pl.BlockSpec((tm, tn), idx(lambda i, j, k, ro=ro, co=co: (i + ro, j + co))))
        args.append(arr)
    for arr in vecs:
        in_specs.append(pl.BlockSpec((arr.shape[0], tn), idx(lambda i, j, k: (0, j))))
        args.append(arr)
    out_shape, out_specs = [], []
    for dtype, chunked in outs:
        if chunked:
            per = (N // N_CHIPS) // tn
            out_shape.append(jax.ShapeDtypeStruct((N_CHIPS, M, N // N_CHIPS), dtype))
            out_specs.append(pl.BlockSpec((None, tm, tn), idx(lambda i, j, k, per=per: (j // per, i, j % per))))
        else:
            out_shape.append(jax.ShapeDtypeStruct((M, N), dtype))
            out_specs.append(pl.BlockSpec((tm, tn), idx(lambda i, j, k: (i, j))))
    nacc = 1 if summed else npair
    scratch = [pltpu.VMEM((tm, tn), F32) for _ in range(nacc)] if nk > 1 else []
    nrow, nvec, nout = len(rows), len(vecs), len(outs)
    dims = (((0 if ta else 1,), (1 if tb else 0,)), ((), ()))

    def body(*refs):
        ab = refs[:2 * npair]
        row_refs = refs[2 * npair:2 * npair + nrow]
        vec_refs = refs[2 * npair + nrow:2 * npair + nrow + nvec]
        out_refs = refs[2 * npair + nrow + nvec:2 * npair + nrow + nvec + nout]
        acc_refs = refs[2 * npair + nrow + nvec + nout:]
        if n_outer:
            j, i, k = pl.program_id(0), pl.program_id(1), pl.program_id(2)
        else:
            i, j, k = pl.program_id(0), pl.program_id(1), pl.program_id(2)

        def part(p):
            av, bv = ab[2 * p][...], ab[2 * p + 1][...]
            if a_pro is not None:
                av = a_pro(av)
            if b_pro is not None:
                bv = b_pro(bv)
            return lax.dot_general(av, bv, dims, preferred_element_type=F32)

        def finish(accs):
            row_index = i * tm + lax.broadcasted_iota(jnp.int32, (tm, 1), 0)
            res = epi(accs, [r[...] for r in row_refs], [v[...] for v in vec_refs], row_index)
            for ref, val in zip(out_refs, res):
                ref[...] = val.astype(ref.dtype)

        parts = [part(p) for p in range(npair)]
        if summed:
            total = parts[0]
            for extra in parts[1:]:
                total = total + extra
            parts = [total]
        if nk == 1:
            finish(parts)
        else:
            @pl.when(k == 0)
            def _():
                for q in range(nacc):
                    acc_refs[q][...] = parts[q]

            @pl.when(jnp.logical_and(k > 0, k < nk - 1))
            def _():
                for q in range(nacc):
                    acc_refs[q][...] += parts[q]

            @pl.when(k == nk - 1)
            def _():
                finish([acc_refs[q][...] + parts[q] for q in range(nacc)])

    grid = (nn, nm, nk) if n_outer else (nm, nn, nk)
    return pl.pallas_call(body, name=name, grid=grid, in_specs=in_specs, out_specs=out_specs,
                          out_shape=out_shape, scratch_shapes=scratch, compiler_params=_cparams())(*args)


def _split3(v):
    v0 = v.astype(BF16)
    r1 = v - v0.astype(F32)
    v1 = r1.astype(BF16)
    v2 = (r1 - v1.astype(F32)).astype(BF16)
    return v0, v1, v2


def _mesh_pos():
    return lax.axis_index("x"), lax.axis_index("y"), lax.axis_index("c")


def _allgather_small(name, x):
    m, n = x.shape

    def body(x_ref, out_ref, send_sems, recv_sems, local_sem):
        xi, yi, ci = _mesh_pos()
        me, sibling = (xi, yi, ci), (xi, yi, 1 - ci)
        chips = [(1 - xi, yi), (xi, 1 - yi), (1 - xi, 1 - yi)]

        def rows(px, py, pc):
            return out_ref.at[pl.ds((4 * px + 2 * py + pc) * m, m), :]

        def copy(k, block, to, src=None):
            return pltpu.make_async_remote_copy(
                src_ref=rows(*block) if src is None else src, dst_ref=rows(*block),
                send_sem=send_sems.at[k], recv_sem=recv_sems.at[k], device_id=to, device_id_type=MESH)

        mine = pltpu.make_async_copy(x_ref, rows(*me), local_sem)
        mine.start()
        first = [copy(0, me, sibling, src=x_ref)]
        first += [copy(1 + j, me, (*chip, ci), src=x_ref) for j, chip in enumerate(chips)]
        for cp in first:
            cp.start()
        passed = [copy(4 + j, (*chip, ci), sibling) for j, chip in enumerate(chips)]
        for j, chip in enumerate(chips):
            copy(1 + j, (*chip, ci), me).wait_recv()
            passed[j].start()
        copy(0, sibling, me).wait_recv()
        for j, chip in enumerate(chips):
            copy(4 + j, (*chip, 1 - ci), me).wait_recv()
        for cp in first + passed:
            cp.wait_send()
        mine.wait()

    return pl.pallas_call(
        body, name=name, out_shape=jax.ShapeDtypeStruct((N_DEV * m, n), x.dtype),
        in_specs=[pl.BlockSpec(memory_space=pltpu.VMEM)], out_specs=pl.BlockSpec(memory_space=pltpu.VMEM),
        scratch_shapes=[pltpu.SemaphoreType.DMA((7,)), pltpu.SemaphoreType.DMA((7,)), pltpu.SemaphoreType.DMA],
        compiler_params=_cparams())(x)


_HBM = pl.BlockSpec(memory_space=pltpu.HBM)
_SEM = pl.BlockSpec(memory_space=pltpu.SEMAPHORE)
_ANY = pl.BlockSpec(memory_space=pl.ANY)
_EFFECT = pltpu.SideEffectType.DATAFLOW_SIDE_EFFECTING


def _in_hbm(v):
    return pltpu.with_memory_space_constraint(v, pltpu.HBM)


def _other_chips(xi, yi):
    return [(1 - xi, yi), (xi, 1 - yi), (1 - xi, 1 - yi)]


def _guarded(core, fn):
    if core is None:
        fn()
    else:
        pl.when(lax.axis_index("c") == core)(fn)


def _split_copies(name, srcs, lands, after, pairs, senders, receivers, ncopy):
    ns, nl = len(srcs), len(lands)
    dma = pltpu.SemaphoreType.DMA((ncopy,))
    thru = [pltpu.HBM(v.shape, v.dtype) for v in list(srcs) + list(lands)]

    def start_body(*refs):
        src_refs, land_refs = refs[:ns], refs[ns:ns + nl]
        descs = pairs(src_refs, land_refs, refs[ns + nl + 1], refs[ns + nl + 2])

        def go():
            for send, _ in descs:
                send.start()

        _guarded(senders, go)
        refs[-1][...] = jnp.zeros_like(refs[-1])

    res = pl.pallas_call(
        start_body, name=name + "_start",
        out_shape=(dma, dma, *thru, jax.ShapeDtypeStruct((8, LANES), F32)),
        in_specs=[_HBM] * (ns + nl) + [_ANY],
        out_specs=(_SEM, _SEM, *([_HBM] * (ns + nl)), pl.BlockSpec(memory_space=pltpu.VMEM)),
        input_output_aliases={k: 2 + k for k in range(ns + nl)},
        compiler_params=_cparams(has_side_effects=_EFFECT),
    )(*[_in_hbm(v) for v in srcs], *[_in_hbm(v) for v in lands], after)
    send_sems, recv_sems, token = res[0], res[1], res[-1]
    carried = res[2:2 + ns + nl]

    def finish(after_work):
        def wait_body(*refs):
            src_refs, land_refs = refs[:ns], refs[ns:ns + nl]
            descs = pairs(src_refs, land_refs, refs[ns + nl], refs[ns + nl + 1])

            def sent():
                for send, _ in descs:
                    send.wait_send()

            def landed():
                for _, recv in descs:
                    recv.wait_recv()

            _guarded(senders, sent)
            _guarded(receivers, landed)

        out = pl.pallas_call(
            wait_body, name=name + "_wait", out_shape=tuple(thru),
            in_specs=[_HBM] * (ns + nl) + [_SEM, _SEM, _ANY], out_specs=tuple([_HBM] * (ns + nl)),
            input_output_aliases={k: k for k in range(ns + nl)},
            compiler_params=_cparams(has_side_effects=_EFFECT),
        )(*carried, send_sems, recv_sems, after_work)
        return list(out[:ns]), list(out[ns:])

    return token, finish


def _cast_slot(name, w, chip_index):
    R, C = w.shape[1:]
    tr = _div(R, max(16, 524288 // C), mult=16)

    def body(chip_ref, w_ref, o_ref):
        o_ref[...] = w_ref[...].astype(BF16)

    return pl.pallas_call(
        body, name=name, out_shape=jax.ShapeDtypeStruct((N_CHIPS, R, C), BF16),
        grid_spec=pltpu.PrefetchScalarGridSpec(
            num_scalar_prefetch=1, grid=(R // tr,),
            in_specs=[pl.BlockSpec((None, tr, C), lambda i, chip_ref: (0, i, 0))],
            out_specs=pl.BlockSpec((None, tr, C), lambda i, chip_ref: (chip_ref[0], i, 0))),
        compiler_params=_cparams())(chip_index, w)


def _sum_plane(name, grads, landed, chip_index):
    R, C = grads.shape[1:]
    tr = _div(R, max(16, 262144 // C), mult=16)

    def body(chip_ref, own_ref, land_ref, o_ref):
        o_ref[...] = ((own_ref[...].astype(F32) + land_ref[0].astype(F32)) + land_ref[1].astype(F32)) \
            + land_ref[2].astype(F32)

    return pl.pallas_call(
        body, name=name, out_shape=jax.ShapeDtypeStruct((R, C), F32),
        grid_spec=pltpu.PrefetchScalarGridSpec(
            num_scalar_prefetch=1, grid=(R // tr,),
            in_specs=[pl.BlockSpec((None, tr, C), lambda i, chip_ref: (chip_ref[0], i, 0)),
                      pl.BlockSpec((3, tr, C), lambda i, chip_ref: (0, i, 0))],
            out_specs=pl.BlockSpec((tr, C), lambda i, chip_ref: (i, 0))),
        compiler_params=_cparams())(chip_index, grads, landed)


def _gather_split(name, lands, after):
    def pairs(src_refs, land_refs, send_sems, recv_sems):
        xi, yi, _ = _mesh_pos()
        mine = 2 * xi + yi
        out = []
        for a in range(len(lands)):
            for j, (px, py) in enumerate(_other_chips(xi, yi)):
                def to_slot(slot, a=a, j=j, px=px, py=py):
                    return pltpu.make_async_remote_copy(
                        src_ref=land_refs[a].at[mine], dst_ref=land_refs[a].at[slot], send_sem=send_sems.at[3 * a + j],
                        recv_sem=recv_sems.at[3 * a + j], device_id=(px, py, 1), device_id_type=MESH)
                out.append((to_slot(mine), to_slot(2 * px + py)))
        return out

    return _split_copies(name, [], lands, after, pairs, senders=1, receivers=1, ncopy=3 * len(lands))


def _pass_split(name, lands, after):
    def pairs(src_refs, land_refs, send_sems, recv_sems):
        xi, yi, _ = _mesh_pos()
        out = []
        for a in range(len(lands)):
            for j, (px, py) in enumerate(_other_chips(xi, yi)):
                cp = pltpu.make_async_remote_copy(
                    src_ref=land_refs[a].at[2 * px + py], dst_ref=land_refs[a].at[2 * px + py],
                    send_sem=send_sems.at[3 * a + j], recv_sem=recv_sems.at[3 * a + j],
                    device_id=(xi, yi, 0), device_id_type=MESH)
                out.append((cp, cp))
        return out

    return _split_copies(name, [], lands, after, pairs, senders=1, receivers=0, ncopy=3 * len(lands))


def _scatter_split(name, grads, after):
    lands = [lax.empty((3,) + g.shape[1:], g.dtype) for g in grads]

    def pairs(src_refs, land_refs, send_sems, recv_sems):
        xi, yi, ci = _mesh_pos()
        out = []
        for a in range(len(grads)):
            for j, (px, py) in enumerate(_other_chips(xi, yi)):
                cp = pltpu.make_async_remote_copy(
                    src_ref=src_refs[a].at[2 * px + py], dst_ref=land_refs[a].at[j], send_sem=send_sems.at[3 * a + j],
                    recv_sem=recv_sems.at[3 * a + j], device_id=(px, py, ci), device_id_type=MESH)
                out.append((cp, cp))
        return out

    return _split_copies(name, grads, lands, after, pairs, senders=None, receivers=None, ncopy=3 * len(grads))


def _gather_finish(name, lands):
    na = len(lands)

    def body(*refs):
        outs = refs[na:2 * na]
        send_sems, recv_sems = refs[2 * na:]
        xi, yi, ci = _mesh_pos()
        passes = [pltpu.make_async_remote_copy(
            src_ref=outs[a].at[2 * px + py], dst_ref=outs[a].at[2 * px + py],
            send_sem=send_sems.at[a, j], recv_sem=recv_sems.at[a, j], device_id=(xi, yi, 0), device_id_type=MESH)
            for a in range(na) for j, (px, py) in enumerate(_other_chips(xi, yi))]

        @pl.when(ci == 1)
        def _():
            for cp in passes:
                cp.start()
            for cp in passes:
                cp.wait_send()

        @pl.when(ci == 0)
        def _():
            for cp in passes:
                cp.wait_recv()

    return pl.pallas_call(
        body, name=name, out_shape=[jax.ShapeDtypeStruct(v.shape, v.dtype) for v in lands],
        in_specs=[_ANY] * na, out_specs=[_ANY] * na,
        input_output_aliases={a: a for a in range(na)},
        scratch_shapes=[pltpu.SemaphoreType.DMA((na, 3)), pltpu.SemaphoreType.DMA((na, 3))],
        compiler_params=_cparams())(*lands)


def _swap_sibling(name, arrs):
    na = len(arrs)

    def body(*refs):
        ins, outs = refs[:na], refs[na:2 * na]
        send_sems, recv_sems = refs[2 * na:]
        xi, yi, ci = _mesh_pos()
        copies = [pltpu.make_async_remote_copy(
            src_ref=ins[a], dst_ref=outs[a], send_sem=send_sems.at[a], recv_sem=recv_sems.at[a],
            device_id=(xi, yi, 1 - ci), device_id_type=MESH) for a in range(na)]
        for cp in copies:
            cp.start()
        for cp in copies:
            cp.wait()

    return pl.pallas_call(
        body, name=name,
        out_shape=[jax.ShapeDtypeStruct(g.shape, g.dtype) for g in arrs],
        in_specs=[_ANY] * na, out_specs=[_ANY] * na,
        scratch_shapes=[pltpu.SemaphoreType.DMA((na,)), pltpu.SemaphoreType.DMA((na,))],
        compiler_params=_cparams())(*arrs)


def _attn_tiles(L, Lc, D):
    tq = min(256, Lc)
    return tq, L // tq, Lc // tq, D // HEAD_DIM // Q_PER_KV


def _attn_probs(q, k):
    s = lax.dot_general(q, k, (((1,), (1,)), ((), ())), preferred_element_type=F32) * (HEAD_DIM ** -0.5)
    e = jnp.exp(s - jnp.max(s, axis=-1, keepdims=True))
    return e * (1.0 / jnp.sum(e, axis=-1, keepdims=True))


def _attn_fwd(qr, kr, v, L, Lc, D):
    T = L + Lc
    tq, nq, qoff, nkv = _attn_tiles(L, Lc, D)

    def body(q_ref, k_ref, v_ref, o_ref):
        p = _attn_probs(q_ref[...], k_ref[...])
        o_ref[...] = jnp.dot(p.astype(BF16), v_ref[...], preferred_element_type=F32).astype(o_ref.dtype)

    kv_spec = pl.BlockSpec((T, HEAD_DIM), lambda h, r, q: (0, h))
    return pl.pallas_call(
        body, name="attn_fwd", grid=(nkv, Q_PER_KV, nq),
        in_specs=[pl.BlockSpec((tq, HEAD_DIM), lambda h, r, q: (q + qoff, h * Q_PER_KV + r)), kv_spec, kv_spec],
        out_specs=pl.BlockSpec((tq, HEAD_DIM), lambda h, r, q: (q, h * Q_PER_KV + r)),
        out_shape=jax.ShapeDtypeStruct((L, D), BF16), compiler_params=_cparams())(qr, kr, v)


def _attn_bwd(qr, kr, v, do, L, Lc, D):
    T = L + Lc
    tq, nq, qoff, nkv = _attn_tiles(L, Lc, D)
    scale = HEAD_DIM ** -0.5

    def body(q_ref, k_ref, v_ref, do_ref, dq_ref, dk_ref, dv_ref):
        first = jnp.logical_and(pl.program_id(1) == 0, pl.program_id(2) == 0)
        q, k, dout = q_ref[...], k_ref[...], do_ref[...]
        p = _attn_probs(q, k)
        dp = lax.dot_general(dout, v_ref[...], (((1,), (1,)), ((), ())), preferred_element_type=F32)
        ds = (p * (dp - jnp.sum(p * dp, axis=-1, keepdims=True)) * scale).astype(BF16)
        dq_ref[...] = jnp.dot(ds, k, preferred_element_type=F32)
        dk = lax.dot_general(ds, q, (((0,), (0,)), ((), ())), preferred_element_type=F32)
        dv = lax.dot_general(p.astype(BF16), dout, (((0,), (0,)), ((), ())), preferred_element_type=F32)

        @pl.when(first)
        def _():
            dk_ref[...] = dk
            dv_ref[...] = dv

        @pl.when(jnp.logical_not(first))
        def _():
            dk_ref[...] += dk
            dv_ref[...] += dv

    kv_spec = pl.BlockSpec((T, HEAD_DIM), lambda h, r, q: (0, h))
    q_spec = pl.BlockSpec((tq, HEAD_DIM), lambda h, r, q: (q + qoff, h * Q_PER_KV + r))
    o_spec = pl.BlockSpec((tq, HEAD_DIM), lambda h, r, q: (q, h * Q_PER_KV + r))
    return pl.pallas_call(
        body, name="attn_bwd", grid=(nkv, Q_PER_KV, nq),
        in_specs=[q_spec, kv_spec, kv_spec, o_spec], out_specs=[o_spec, kv_spec, kv_spec],
        out_shape=[jax.ShapeDtypeStruct((L, D), F32), jax.ShapeDtypeStruct((T, D // Q_PER_KV), F32),
                   jax.ShapeDtypeStruct((T, D // Q_PER_KV), F32)],
        compiler_params=_cparams())(qr, kr, v, do)


SUB = 8


def _doubling(xr, xi, pw_re, pw_im, lanes, first_power, period, reverse):
    n = xr.shape[0]
    rows = lax.broadcasted_iota(jnp.int32, (n, 1), 0) & (period - 1)
    for k in range(period.bit_length() - 1):
        d = 1 << k
        keep = rows < period - d if reverse else rows >= d
        sr = jnp.where(keep, pltpu.roll(xr, n - d if reverse else d, 0), 0.0)
        si = jnp.where(keep, pltpu.roll(xi, n - d if reverse else d, 0), 0.0)
        pr, pi = pw_re[first_power + k:first_power + k + 1, lanes], pw_im[first_power + k:first_power + k + 1, lanes]
        xr, xi = xr + (pr * sr - pi * si), xi + (pr * si + pi * sr)
    return xr, xi


def _scan_tile(xr, xi, tb, lanes, reverse):
    pw_re, pw_im, w8_re, w8_im, wb_re, wb_im, carry_re, carry_im, sr, si = tb
    tt = xr.shape[0]
    nb = tt // SUB
    xr, xi = _doubling(xr, xi, pw_re, pw_im, lanes, 0, SUB, reverse)
    nq = sr.shape[0]
    cols = [slice(q * LANES, (q + 1) * LANES) for q in range(nq)]
    for q in range(nq):
        sr[q] = xr[:, cols[q]]
        si[q] = xi[:, cols[q]]
    last = 0 if reverse else SUB - 1
    er = jnp.concatenate([sr[q, pl.ds(last, nb, stride=SUB), :] for q in range(nq)], axis=1)
    ei = jnp.concatenate([si[q, pl.ds(last, nb, stride=SUB), :] for q in range(nq)], axis=1)
    er, ei = _doubling(er, ei, pw_re, pw_im, lanes, 3, nb, reverse)
    car, cai = carry_re[:, lanes], carry_im[:, lanes]
    wbr, wbi = wb_re[:, lanes], wb_im[:, lanes]
    er = er + (wbr * car - wbi * cai)
    ei = ei + (wbr * cai + wbi * car)
    out_block = 0 if reverse else nb - 1
    carry_re[:, lanes] = er[out_block:out_block + 1, :]
    carry_im[:, lanes] = ei[out_block:out_block + 1, :]
    blocks = lax.broadcasted_iota(jnp.int32, (nb, 1), 0)
    first = blocks == (nb - 1 if reverse else 0)
    cr = jnp.where(first, car, pltpu.roll(er, nb - 1 if reverse else 1, 0))
    ci = jnp.where(first, cai, pltpu.roll(ei, nb - 1 if reverse else 1, 0))
    for r in range(SUB):
        wr, wi = w8_re[r:r + 1, lanes], w8_im[r:r + 1, lanes]
        add_r, add_i = wr * cr - wi * ci, wr * ci + wi * cr
        for q in range(nq):
            sr[q, pl.ds(r, nb, stride=SUB), :] += add_r[:, cols[q]]
            si[q, pl.ds(r, nb, stride=SUB), :] += add_i[:, cols[q]]
    hr = jnp.concatenate([sr[q] for q in range(nq)], axis=1)
    hi = jnp.concatenate([si[q] for q in range(nq)], axis=1)
    return hr, hi, car, cai


def _scan_scratch(tt, NS):
    nb = tt // SUB
    return [pltpu.VMEM((8, NS), F32), pltpu.VMEM((8, NS), F32), pltpu.VMEM((SUB, NS), F32), pltpu.VMEM((SUB, NS), F32),
            pltpu.VMEM((nb, NS), F32), pltpu.VMEM((nb, NS), F32), pltpu.VMEM((1, NS), F32), pltpu.VMEM((1, NS), F32),
            pltpu.VMEM((SLAB_ST // LANES, tt, LANES), F32), pltpu.VMEM((SLAB_ST // LANES, tt, LANES), F32)]


def _scan_init(lr, li, tb, reverse):
    pw_re, pw_im, w8_re, w8_im, wb_re, wb_im, carry_re, carry_im, sr, _ = tb
    nb = wb_re.shape[0]
    carry_re[...] = jnp.zeros_like(carry_re)
    carry_im[...] = jnp.zeros_like(carry_im)
    pr, pi = lr, li
    for k in range(3 + nb.bit_length() - 1):
        pw_re[k:k + 1, :] = pr
        pw_im[k:k + 1, :] = pi
        if k == 3:
            l8r, l8i = pr, pi
        pr, pi = pr * pr - pi * pi, 2.0 * pr * pi
    pr, pi = lr, li
    for r in range(SUB):
        row = SUB - 1 - r if reverse else r
        w8_re[row:row + 1, :] = pr
        w8_im[row:row + 1, :] = pi
        pr, pi = pr * lr - pi * li, pr * li + pi * lr
    pr, pi = l8r, l8i
    for b in range(nb):
        row = nb - 1 - b if reverse else b
        wb_re[row:row + 1, :] = pr
        wb_im[row:row + 1, :] = pi
        pr, pi = pr * l8r - pi * l8i, pr * l8i + pi * l8r


def _ssm_tiles(T, Lc):
    tt = min(128, Lc)
    return tt, T // tt, Lc // tt


def _ssm_fwd(name, u, bbd, cbd_re, cbd_im, lam_re, lam_im, coef_re, coef_im, Lc, reverse):
    T, W = u.shape
    nslab = W // SLAB_CH
    NS = nslab * SLAB_ST
    tt, nt, nc = _ssm_tiles(T, Lc)
    if reverse:
        tile = lambda s: jnp.where(s < nc, nc - 1 - s, nt - 1 - (s - nc))
    else:
        tile = lambda s: s

    def body(u_ref, b_ref, cr_ref, ci_ref, lr_ref, li_ref, kr_ref, ki_ref, hr_ref, hi_ref, y_ref, *tb):
        @pl.when(pl.program_id(0) == 0)
        def _():
            _scan_init(lr_ref[...], li_ref[...], tb, reverse)

        for j in range(nslab):
            lanes = slice(j * SLAB_ST, (j + 1) * SLAB_ST)
            bu = jnp.dot(u_ref[:, j * SLAB_CH:(j + 1) * SLAB_CH], b_ref[j], preferred_element_type=F32)
            br, bi = bu[:, :SLAB_ST], bu[:, SLAB_ST:]
            kr, ki = kr_ref[:, lanes], ki_ref[:, lanes]
            hr, hi, _, _ = _scan_tile(kr * br - ki * bi, kr * bi + ki * br, tb, lanes, reverse)
            hrb, hib = hr.astype(BF16), hi.astype(BF16)
            hr_ref[:, lanes] = hrb
            hi_ref[:, lanes] = hib
            y_ref[:, j * SLAB_CH:(j + 1) * SLAB_CH] = (
                jnp.dot(hrb, cr_ref[j], preferred_element_type=F32)
                - jnp.dot(hib, ci_ref[j], preferred_element_type=F32))

    whole3 = lambda arr: pl.BlockSpec(arr.shape, lambda s: (0, 0, 0))
    vec = pl.BlockSpec((1, NS), lambda s: (0, 0))
    return pl.pallas_call(
        body, name=name, grid=(nt,),
        in_specs=[pl.BlockSpec((tt, W), lambda s: (tile(s), 0)), whole3(bbd), whole3(cbd_re), whole3(cbd_im),
                  vec, vec, vec, vec],
        out_specs=[pl.BlockSpec((tt, NS), lambda s: (tile(s), 0)), pl.BlockSpec((tt, NS), lambda s: (tile(s), 0)),
                   pl.BlockSpec((tt, W), lambda s: (tile(s), 0))],
        out_shape=[jax.ShapeDtypeStruct((T, NS), BF16), jax.ShapeDtypeStruct((T, NS), BF16),
                   jax.ShapeDtypeStruct((T, W), F32)],
        scratch_shapes=_scan_scratch(tt, NS),
        compiler_params=_cparams())(u, bbd, cbd_re, cbd_im, lam_re, lam_im, coef_re, coef_im)


def _ssm_bwd(name, dy, h_re, h_im, u, bbd, bbdt_re, bbdt_im, cbdt_re, cbdt_im, lam_re, lam_im,
             coef_re, coef_im, Lc, reverse):
    T, W = u.shape
    nslab = W // SLAB_CH
    NS = nslab * SLAB_ST
    tt, nt, nc = _ssm_tiles(T, Lc)
    adj_reverse = not reverse
    if reverse:
        tile = lambda s: jnp.where(s < nt - nc, nc + s, s - (nt - nc))
    else:
        tile = lambda s: nt - 1 - s

    def body(dy_ref, hr_ref, hi_ref, u_ref, b_ref, btr_ref, bti_ref, ctr_ref, cti_ref, lr_ref, li_ref,
             kr_ref, ki_ref, du_ref, dlr_ref, dli_ref, dkr_ref, dki_ref, dbf_ref, dcrf_ref, dcif_ref,
             db_ref, dcr_ref, dci_ref, *tb):
        @pl.when(pl.program_id(0) == 0)
        def _():
            _scan_init(lr_ref[...], -li_ref[...], tb, adj_reverse)
            for ref in (dlr_ref, dli_ref, dkr_ref, dki_ref, db_ref, dcr_ref, dci_ref):
                ref[...] = jnp.zeros_like(ref)

        rows = lax.broadcasted_iota(jnp.int32, (tt, 1), 0)
        far_row = tt - 1 if adj_reverse else 0
        tn_dims = (((0,), (0,)), ((), ()))
        for j in range(nslab):
            lanes = slice(j * SLAB_ST, (j + 1) * SLAB_ST)
            chans = slice(j * SLAB_CH, (j + 1) * SLAB_CH)
            dys, us = dy_ref[:, chans], u_ref[:, chans]
            er = jnp.dot(dys, ctr_ref[j], preferred_element_type=F32)
            ei = -jnp.dot(dys, cti_ref[j], preferred_element_type=F32)
            ar, ai, car, cai = _scan_tile(er, ei, tb, lanes, adj_reverse)
            shift = tt - 1 if adj_reverse else 1
            nr = jnp.where(rows == far_row, car, pltpu.roll(ar, shift, 0))
            ni = jnp.where(rows == far_row, cai, pltpu.roll(ai, shift, 0))
            hrb, hib = hr_ref[:, lanes], hi_ref[:, lanes]
            hr, hi = hrb.astype(F32), hib.astype(F32)
            dlr_ref[:, lanes] += jnp.sum(nr * hr + ni * hi, axis=0, keepdims=True)
            dli_ref[:, lanes] += jnp.sum(ni * hr - nr * hi, axis=0, keepdims=True)
            bu = jnp.dot(us, b_ref[j], preferred_element_type=F32)
            br, bi = bu[:, :SLAB_ST], bu[:, SLAB_ST:]
            dkr_ref[:, lanes] += jnp.sum(ar * br + ai * bi, axis=0, keepdims=True)
            dki_ref[:, lanes] += jnp.sum(ai * br - ar * bi, axis=0, keepdims=True)
            kr, ki = kr_ref[:, lanes], ki_ref[:, lanes]
            dbr = (ar * kr + ai * ki).astype(BF16)
            dbi = (ai * kr - ar * ki).astype(BF16)
            du_ref[:, chans] = (jnp.dot(dbr, btr_ref[j], preferred_element_type=F32)
                                + jnp.dot(dbi, bti_ref[j], preferred_element_type=F32))
            db_ref[j, :, :SLAB_ST] += lax.dot_general(us, dbr, tn_dims, preferred_element_type=F32)
            db_ref[j, :, SLAB_ST:] += lax.dot_general(us, dbi, tn_dims, preferred_element_type=F32)
            dcr_ref[j] += lax.dot_general(hrb, dys, tn_dims, preferred_element_type=F32)
            dci_ref[j] -= lax.dot_general(hib, dys, tn_dims, preferred_element_type=F32)

        @pl.when(pl.program_id(0) == nt - 1)
        def _():
            def iota(shape, axis):
                return lax.broadcasted_iota(jnp.int32, shape, axis)

            sg, ss = SSM_GROUP.bit_length() - 1, SSM_STATE.bit_length() - 1
            b_mask = (iota((SLAB_CH, SLAB_ST), 0) >> sg) == (iota((SLAB_CH, SLAB_ST), 1) >> ss)
            c_mask = (iota((SLAB_ST, SLAB_CH), 0) >> ss) == (iota((SLAB_ST, SLAB_CH), 1) >> sg)
            fold = jnp.where((iota((SLAB_ST, SSM_STATE), 0) & (SSM_STATE - 1)) == iota((SLAB_ST, SSM_STATE), 1),
                             1.0, 0.0).astype(BF16)
            fold_t = jnp.where((iota((SSM_STATE, SLAB_ST), 1) & (SSM_STATE - 1)) == iota((SSM_STATE, SLAB_ST), 0),
                               1.0, 0.0).astype(BF16)

            def exact_dot(a, b, a_is_value):
                terms = _split3(a if a_is_value else b)
                acc = None
                for t in terms:
                    part = jnp.dot(t, b, preferred_element_type=F32) if a_is_value else jnp.dot(a, t, preferred_element_type=F32)
                    acc = part if acc is None else acc + part
                return acc

            for j in range(nslab):
                dbj = db_ref[j]
                dbf_ref[j, :, :SSM_STATE] = exact_dot(jnp.where(b_mask, dbj[:, :SLAB_ST], 0.0), fold, True)
                dbf_ref[j, :, SSM_STATE:] = exact_dot(jnp.where(b_mask, dbj[:, SLAB_ST:], 0.0), fold, True)
                dcrf_ref[j] = exact_dot(fold_t, jnp.where(c_mask, dcr_ref[j], 0.0), False)
                dcif_ref[j] = exact_dot(fold_t, jnp.where(c_mask, dci_ref[j], 0.0), False)

    whole3 = lambda arr: pl.BlockSpec(arr.shape, lambda s: (0, 0, 0))
    vec = pl.BlockSpec((1, NS), lambda s: (0, 0))
    row_w = pl.BlockSpec((tt, W), lambda s: (tile(s), 0))
    row_s = pl.BlockSpec((tt, NS), lambda s: (tile(s), 0))
    dbf = jax.ShapeDtypeStruct((nslab, SLAB_CH, 2 * SSM_STATE), F32)
    dcf = jax.ShapeDtypeStruct((nslab, SSM_STATE, SLAB_CH), F32)
    return pl.pallas_call(
        body, name=name, grid=(nt,),
        in_specs=[row_w, row_s, row_s, row_w, whole3(bbd), whole3(bbdt_re), whole3(bbdt_im), whole3(cbdt_re),
                  whole3(cbdt_im), vec, vec, vec, vec],
        out_specs=[row_w, vec, vec, vec, vec, whole3(dbf), whole3(dcf), whole3(dcf)],
        out_shape=[jax.ShapeDtypeStruct((T, W), F32)] + [jax.ShapeDtypeStruct((1, NS), F32)] * 4 + [dbf, dcf, dcf],
        scratch_shapes=[pltpu.VMEM(bbd.shape, F32), pltpu.VMEM(bbdt_re.shape, F32), pltpu.VMEM(bbdt_re.shape, F32)]
        + _scan_scratch(tt, NS),
        compiler_params=_cparams())(dy, h_re, h_im, u, bbd, bbdt_re, bbdt_im, cbdt_re, cbdt_im,
                                    lam_re, lam_im, coef_re, coef_im)


def _zoh_math(a_re, a_im, log_dt):
    dt = jnp.exp(log_dt)
    mag = jnp.exp(a_re * dt)
    lb_re = mag * jnp.cos(a_im * dt)
    lb_im = mag * jnp.sin(a_im * dt)
    den = a_re * a_re + a_im * a_im
    coef_re = ((lb_re - 1.0) * a_re + lb_im * a_im) / den
    coef_im = (lb_im * a_re - (lb_re - 1.0) * a_im) / den
    return lb_re, lb_im, coef_re, coef_im


def _zoh_fwd(a_re, a_im, log_dt):
    def body(ar, ai, ld, o0, o1, o2, o3):
        for ref, val in zip((o0, o1, o2, o3), _zoh_math(ar[...], ai[...], ld[...])):
            ref[...] = val

    return pl.pallas_call(body, name="zoh_fwd", out_shape=[jax.ShapeDtypeStruct(a_re.shape, F32)] * 4,
                          compiler_params=_cparams())(a_re, a_im, log_dt)


def _zoh_bwd(a_re, a_im, log_dt, cots):
    def body(ar, ai, ld, c0, c1, c2, c3, o0, o1, o2):
        _, vjp = jax.vjp(_zoh_math, ar[...], ai[...], ld[...])
        for ref, val in zip((o0, o1, o2), vjp((c0[...], c1[...], c2[...], c3[...]))):
            ref[...] = val

    return pl.pallas_call(
        body, name="zoh_bwd",
        out_shape=[jax.ShapeDtypeStruct(a_re.shape, F32), jax.ShapeDtypeStruct(a_re.shape, F32),
                   jax.ShapeDtypeStruct(log_dt.shape, F32)],
        compiler_params=_cparams())(a_re, a_im, log_dt, *cots)


def _outer_sum(acts, cots):
    D, N = acts.shape[1], cots.shape[1]
    tm, tn = _div(D, 512), _div(N, 1152)
    dims = (((0,), (0,)), ((), ()))

    def body(a_ref, b_ref, o_ref):
        a = a_ref[...]
        aa = _split3(a * _sigmoid(a))
        bb = _split3(b_ref[...])
        acc = None
        for ia in range(3):
            for ib in range(3 - ia):
                t = lax.dot_general(aa[ia], bb[ib], dims, preferred_element_type=F32)
                acc = t if acc is None else acc + t
        o_ref[...] = acc

    return pl.pallas_call(
        body, name="mod_dw", grid=(D // tm, N // tn),
        in_specs=[pl.BlockSpec((16, tm), lambda i, j: (0, i)), pl.BlockSpec((16, tn), lambda i, j: (0, j))],
        out_specs=pl.BlockSpec((tm, tn), lambda i, j: (i, j)),
        out_shape=jax.ShapeDtypeStruct((D, N), F32), compiler_params=_cparams())(acts, cots)


def _adamw_math(w, g, m, v):
    m = ADAM_B1 * m + (1.0 - ADAM_B1) * g
    v = ADAM_B2 * v + (1.0 - ADAM_B2) * (g * g)
    m_hat = m / (1.0 - ADAM_B1 ** ADAM_STEP)
    v_hat = v / (1.0 - ADAM_B2 ** ADAM_STEP)
    delta = -ADAM_LR * (m_hat / (jnp.sqrt(v_hat) + ADAM_EPS) + ADAM_WD * w)
    return delta, m, v


def _adamw(name, w, m, v, gparts):
    R, C = w.shape[-2:]
    kind = 'row1' if w.ndim == 3 else 'row'
    tr = _div(R, max(8, 262144 // C), mult=8)

    def fn(i, wv, mv, vv, *gs):
        g = gs[0]
        for extra in gs[1:]:
            g = g + extra
        return (g,) + _adamw_math(wv, g, mv, vv)

    return _rowk(name, fn, R, tr, [(w, kind), (m, kind), (v, kind)] + [(g, 'row') for g in gparts],
                 [(w.shape, F32, kind)] * 4)


def _pack(pieces, rows_mult=8):
    flat = jnp.concatenate([p.reshape(-1).astype(F32) for p in pieces])
    unit = rows_mult * PACK_W
    total = -(-flat.shape[0] // unit) * unit
    return jnp.pad(flat, (0, total - flat.shape[0])).reshape(total // PACK_W, PACK_W)


def _unpack(buf, shapes):
    flat = buf.reshape(-1)
    out, off = [], 0
    for s in shapes:
        n = math.prod(s)
        out.append(flat[off:off + n].reshape(s))
        off += n
    return out


def _bd_expand(t):
    S, g, a, b = t.shape
    eye = jnp.eye(g, dtype=t.dtype)
    return (t[:, :, :, None, :] * eye[None, :, None, :, None]).reshape(S, g * a, g * b)


def _rope_tables(L, Lc):
    rows = L // GRID_W
    row_ids = jnp.broadcast_to(jnp.arange(rows)[:, None], (rows, GRID_W)).reshape(-1).astype(F32)
    col_ids = jnp.broadcast_to(jnp.arange(GRID_W)[None, :], (rows, GRID_W)).reshape(-1).astype(F32)
    quarter = HEAD_DIM // 4
    inv_freq = ROPE_THETA ** (-jnp.arange(quarter, dtype=F32) / quarter)
    ang_r = row_ids[:, None] * inv_freq
    ang_c = col_ids[:, None] * inv_freq
    cos = jnp.concatenate([jnp.cos(ang_r), jnp.cos(ang_r), jnp.cos(ang_c), jnp.cos(ang_c)], axis=1)
    sin = jnp.concatenate([-jnp.sin(ang_r), jnp.sin(ang_r), -jnp.sin(ang_c), jnp.sin(ang_c)], axis=1)
    cos = jnp.concatenate([jnp.ones((Lc, HEAD_DIM), F32), cos], axis=0)
    sin = jnp.concatenate([jnp.zeros((Lc, HEAD_DIM), F32), sin], axis=0)
    return cos, sin


def _rot(v):
    lane = lax.broadcasted_iota(jnp.int32, (1, HEAD_DIM), 1)
    first = (lane % (HEAD_DIM // 2)) < (HEAD_DIM // 4)
    return jnp.where(first, pltpu.roll(v, HEAD_DIM - HEAD_DIM // 4, 1), pltpu.roll(v, HEAD_DIM // 4, 1))


def _head_norm(xh, g):
    return xh * lax.rsqrt(jnp.mean(xh * xh, axis=-1, keepdims=True) + NORM_EPS) * g


def _norm_mod(xv, g, sh, sc):
    r = lax.rsqrt(jnp.mean(xv * xv, axis=-1, keepdims=True) + NORM_EPS)
    return (xv * r) * g * (1.0 + sc) + sh


def kernel(x, c, ctx, c_ctx, w_mod, b_mod, norm_g, w_ffn1_gate, w_ffn1_up, w_ffn1_down, w_in, q_norm_g, k_norm_g, ssm_a_re, ssm_a_im, ssm_log_dt, ssm_b_re, ssm_b_im, ssm_c_re, ssm_c_im, ssm_d, w_glu, b_glu, w_br_attn, w_br_ssm, w_out, w_ffn2_gate, w_ffn2_up, w_ffn2_down, loss_target, m_c_ctx, m_w_mod, m_b_mod, m_norm_g, m_w_ffn1_gate, m_w_ffn1_up, m_w_ffn1_down, m_w_in, m_q_norm_g, m_k_norm_g, m_ssm_a_re, m_ssm_a_im, m_ssm_log_dt, m_ssm_b_re, m_ssm_b_im, m_ssm_c_re, m_ssm_c_im, m_ssm_d, m_w_glu, m_b_glu, m_w_br_attn, m_w_br_ssm, m_w_out, m_w_ffn2_gate, m_w_ffn2_up, m_w_ffn2_down, v_c_ctx, v_w_mod, v_b_mod, v_norm_g, v_w_ffn1_gate, v_w_ffn1_up, v_w_ffn1_down, v_w_in, v_q_norm_g, v_k_norm_g, v_ssm_a_re, v_ssm_a_im, v_ssm_log_dt, v_ssm_b_re, v_ssm_b_im, v_ssm_c_re, v_ssm_c_im, v_ssm_d, v_w_glu, v_b_glu, v_w_br_attn, v_w_br_ssm, v_w_out, v_w_ffn2_gate, v_w_ffn2_up, v_w_ffn2_down):
    A = dict(locals())
    xi, yi, ci = _mesh_pos()
    chip = 2 * xi + yi
    me = 4 * xi + 2 * yi + ci
    L, D = x.shape[1], x.shape[2]
    Lc = ctx.shape[1]
    T = L + Lc
    F4 = w_ffn1_gate.shape[2]
    F = N_CHIPS * F4
    W, KV, Dq = D // 2, D // 4, D // 4
    G = W // SSM_GROUP
    P, E = SSM_STATE, SSM_GROUP
    NS = G * P
    nslab = W // SLAB_CH
    tr = min(256, Lc)
    ncr = Lc // tr
    assert L % tr == 0 and Lc % tr == 0 and W % SLAB_CH == 0 and D % (4 * LANES) == 0

    def sel(i, v):
        return v if v.shape[0] == 1 else jnp.where(i < ncr, v[0:1], v[1:2])

    def put(i, v, nrow):
        if nrow == 1:
            return v
        which = (i >= ncr).astype(jnp.int32)
        r2 = lax.broadcasted_iota(jnp.int32, (nrow, 1), 0)
        return jnp.where(r2 == which, jnp.broadcast_to(v, (nrow, v.shape[1])), 0.0)

    ident = lambda accs, rows, vecs, ri: [accs[0]]

    NM = w_mod.shape[2]
    first = jnp.zeros((8, D), F32).at[0].set(c[0]).at[1:4, :Dq].set(norm_g[0])
    g0 = _allgather_small("gather_c", first).reshape(N_CHIPS, 2, 8, D)
    c_all = g0[:, :, 0].reshape(N_DEV, D)
    ng = jnp.transpose(g0[:, 0, 1:4, :Dq], (1, 0, 2)).reshape(3, D)
    acts = jnp.concatenate([c_all, c_ctx[None], jnp.zeros((7, D), F32)], axis=0)
    wm = w_mod[0]
    b_shard = lax.dynamic_slice(b_mod[0], (chip * NM,), (NM,))[None]
    silu_bf = lambda a: (a * _sigmoid(a)).astype(BF16)
    to_bf = lambda b: b.astype(BF16)
    mod_part = _mm("mod_fwd", [(acts, wm, D)], 16, NM, tm=16, tn=_div(NM, 1152),
                   epi=lambda accs, rows, vecs, ri: [accs[0] + vecs[0]], outs=[(F32, False)],
                   vecs=[b_shard], a_pro=silu_bf, b_pro=to_bf)[0]
    mg = _allgather_small("gather_mod", mod_part).reshape(N_CHIPS, 2, 16, NM)[:, 0]
    mod_all = jnp.transpose(mg, (1, 0, 2)).reshape(16, N_CHIPS * NM)
    mod_x = lax.dynamic_slice(mod_all, (me, 0), (1, 9 * D))
    mod_c = jnp.where(jnp.arange(9 * D)[None] < 5 * D, mod_all[8:9], 0.0)
    modv = jnp.concatenate([mod_c, mod_x], axis=0)
    mv = lambda k: modv[:, k * D:(k + 1) * D]
    sh1, sc1, g1, sh2, sc2 = mv(0), mv(1), mv(2), mv(3), mv(4)
    g2, sh3, sc3, g3 = mv(5)[1:2], mv(6)[1:2], mv(7)[1:2], mv(8)[1:2]

    big = ['w_ffn1_gate', 'w_ffn1_up', 'w_ffn1_down', 'w_ffn2_gate', 'w_ffn2_up', 'w_ffn2_down',
           'w_in', 'w_glu', 'w_br_attn', 'w_br_ssm', 'w_out']
    row_sharded = {'w_ffn1_down', 'w_ffn2_down', 'w_glu', 'w_br_attn', 'w_out'}
    groups = [big[0:2], big[2:3], big[6:7], big[7:11], big[3:6]]
    chip_index = jnp.reshape(chip, (1,)).astype(jnp.int32)
    slots = {n: _cast_slot("cast_" + n, A[n], chip_index) for n in big}
    tok, gather_finish = modv, []
    for gi, names in enumerate(groups):
        tok, fin = _gather_split("gather_w%d" % gi, [slots[n] for n in names], tok)
        gather_finish.append(fin)
    ng = ng + tok[0:1, 0:1]
    Wt = {}

    def register(names, full):
        for n, gw in zip(names, full):
            Wt[n] = gw.reshape(N_CHIPS * gw.shape[1], gw.shape[2]) if n in row_sharded else gw

    def weights_ready(gi, after_work):
        _, lands = gather_finish[gi](after_work)
        register(groups[gi], _gather_finish("gather_w%d_pass" % gi, lands))

    def weights_pass(gi, after_work):
        _, lands = gather_finish[gi](after_work)
        tok_, fin_ = _pass_split("gather_w%d_pass" % gi, lands, after_work)
        return tok_, lambda later: register(groups[gi], fin_(later)[1])

    a_re2, a_im2 = ssm_a_re[0].reshape(2 * G, P), ssm_a_im[0].reshape(2 * G, P)
    ldt2 = ssm_log_dt[0].reshape(2 * G, 1)
    zoh = _zoh_fwd(a_re2, a_im2, ldt2)
    lam_re, lam_im, coef_re, coef_im = [[z[d * G:(d + 1) * G].reshape(1, NS) for d in range(2)] for z in zoh]
    bd_b = lambda b: _bd_expand(jnp.transpose(b, (0, 2, 1)).reshape(nslab, SLAB_GROUPS, E, P))
    bd_c = lambda cc: _bd_expand(jnp.transpose(cc, (0, 2, 1)).reshape(nslab, SLAB_GROUPS, P, E))
    bbd, bbdt_re, bbdt_im, cbd_re, cbd_im, cbdt_re, cbdt_im = [], [], [], [], [], [], []
    for d in range(2):
        br_, bi_ = bd_b(ssm_b_re[0, d]).astype(BF16), bd_b(ssm_b_im[0, d]).astype(BF16)
        cr_, ci_ = bd_c(ssm_c_re[0, d]).astype(BF16), bd_c(ssm_c_im[0, d]).astype(BF16)
        bbd.append(jnp.concatenate([br_, bi_], axis=2))
        bbdt_re.append(jnp.transpose(br_, (0, 2, 1)))
        bbdt_im.append(jnp.transpose(bi_, (0, 2, 1)))
        cbd_re.append(cr_)
        cbd_im.append(ci_)
        cbdt_re.append(jnp.transpose(cr_, (0, 2, 1)))
        cbdt_im.append(jnp.transpose(ci_, (0, 2, 1)))
    cos_t, sin_t = _rope_tables(L, Lc)
    qg, kg = q_norm_g, k_norm_g
    small = ['c_ctx', 'b_mod', 'norm_g', 'q_norm_g', 'k_norm_g', 'ssm_a_re', 'ssm_a_im', 'ssm_log_dt', 'ssm_b_re',
             'ssm_b_im', 'ssm_c_re', 'ssm_c_im', 'ssm_d', 'b_glu']
    packs_wmv = [_pack([A[pre + n] for n in small]) for pre in ('', 'm_', 'v_')]
    prepared = packs_wmv + [cos_t, sin_t, coef_im[0], coef_im[1]] + [
        t[d][0] for t in (bbd, bbdt_re, bbdt_im, cbd_re, cbd_im, cbdt_re, cbdt_im) for d in range(2)]
    weights_ready(0, tok + sum(t[0:1, 0:1].astype(F32) for t in prepared))

    def norm_mod(name, xv, g, sh, sc):
        rows = xv.shape[0]
        return _rowk(name, lambda i, xt, gt, sht, sct: [_norm_mod(xt, gt, sel(i, sht), sel(i, sct))],
                     rows, tr, [(xv, 'row'), (g, 'vec'), (sh, 'vec'), (sc, 'vec')], [((rows, D), BF16, 'row')])[0]

    def swiglu_epi(accs, rows, vecs, ri):
        a_, b_ = accs
        return [a_, b_, a_ * _sigmoid(a_) * b_]

    def res_epi(coef):
        def epi(accs, rows, vecs, ri):
            gate = vecs[0]
            if gate.shape[0] == 2:
                gate = jnp.where(ri < Lc, gate[0:1], gate[1:2])
            return [accs[0], rows[0] + (coef * gate) * accs[0]]
        return epi

    def ffn_fwd(tag, h, xres, gate, down_ready=None):
        rows = h.shape[0]
        a_, b_, s_ = _mm(tag + "_up", [(h, Wt['w_' + tag + '_gate'], D), (h, Wt['w_' + tag + '_up'], D)], rows, F,
                         tm=_div(rows, 256), tn=F4, epi=swiglu_epi, outs=[(F32, False), (F32, False), (BF16, False)])
        if down_ready is not None:
            down_ready(s_)
        f_, xo = _mm(tag + "_down", [(s_, Wt['w_' + tag + '_down'], F)], rows, D, tm=_div(rows, 384),
                     tn=_div(D, 512), epi=res_epi(0.5), outs=[(F32, False), (F32, False)],
                     rows=[(xres, 0, 0)], vecs=[gate])
        return a_, b_, s_, f_, xo

    xc = jnp.concatenate([ctx[0], x[0]], axis=0)
    h1 = norm_mod("norm1", xc, ng[0:1], sh1, sc1)
    a1, b1, s1, f1, x1 = ffn_fwd("ffn1", h1, xc, g1, down_ready=lambda s_: weights_ready(1, s_))
    weights_ready(2, x1)
    h2 = norm_mod("norm2", x1, ng[1:2], sh2, sc2)
    proj = _mm("in_proj", [(h2, Wt['w_in'], D)], T, 4 * D, tm=_div(T, 768), tn=_div(D, 1024), epi=ident,
               outs=[(F32, False)])[0]
    nh, nkvh = D // HEAD_DIM, KV // HEAD_DIM

    def prep_fn(i, kt, vt, ut, qt, qgt, kgt, ct, st):
        qs = [_head_norm(qt[:, h * HEAD_DIM:(h + 1) * HEAD_DIM], qgt) for h in range(nh)]
        ks = [_head_norm(kt[:, h * HEAD_DIM:(h + 1) * HEAD_DIM], kgt) for h in range(nkvh)]
        qs = [v * ct + _rot(v) * st for v in qs]
        ks = [v * ct + _rot(v) * st for v in ks]
        return [jnp.concatenate(qs, axis=1), jnp.concatenate(ks, axis=1), vt, ut]

    qr, kr, vb, ub = _rowk(
        "qk_prep", prep_fn, T, tr,
        [(proj, ('col', KV, 0)), (proj, ('col', KV, 1)), (proj, ('col', W, 1)), (proj, ('col', D, 1)),
         (qg, 'vec'), (kg, 'vec'), (cos_t, 'row'), (sin_t, 'row')],
        [((T, D), BF16, 'row'), ((T, KV), BF16, 'row'), ((T, KV), BF16, 'row'), ((T, W), BF16, 'row')])
    _, mixer_weights = weights_pass(3, qr)
    attn = _attn_fwd(qr, kr, vb, L, Lc, D)
    hs_re, hs_im, ys = [], [], []
    for d in range(2):
        hr_, hi_, y_ = _ssm_fwd("ssm_fwd%d" % d, ub, bbd[d], cbd_re[d], cbd_im[d], lam_re[d], lam_im[d],
                                coef_re[d], coef_im[d], Lc, reverse=bool(d))
        hs_re.append(hr_)
        hs_im.append(hi_)
        ys.append(y_)
        if d == 0:
            _, ffn2_weights = weights_pass(4, y_)
    mixer_weights(ys[1])

    def ssm_out_fn(i, y0, y1, ut, dt):
        pre = dt * ut + y0 + y1
        yg_ = _gelu(pre)
        return [pre, yg_, yg_]

    ssm_pre, yg, ygb = _rowk(
        "ssm_out", ssm_out_fn, L, tr,
        [(ys[0], 'orow'), (ys[1], 'orow'), (proj, ('ocol', W, 1)), (ssm_d, 'vec')],
        [((L, W), F32, 'row'), ((L, W), F32, 'row'), ((L, W), BF16, 'row')], nc=ncr)

    def glu_epi(accs, rows, vecs, ri):
        z_ = accs[0] + vecs[0]
        return [z_, rows[0] * _sigmoid(z_)]

    zglu, y2 = _mm("glu", [(ygb, Wt['w_glu'], W)], L, W, tm=_div(L, 512), tn=_div(W, 512), epi=glu_epi,
                   outs=[(F32, False), (BF16, False)], rows=[(yg, 0, 0)], vecs=[b_glu])
    tnm = _div(Dq, 512)

    def merge_epi(accs, rows, vecs, ri):
        ga, gs = _sigmoid(rows[0]), _sigmoid(rows[1])
        return [accs[0], accs[1], ga * accs[0] + gs * accs[1]]

    ba, bs, merged = _mm("merge", [(attn, Wt['w_br_attn'], D), (y2, Wt['w_br_ssm'], W)], L, D, tm=tr, tn=tnm,
                         epi=merge_epi, outs=[(F32, False), (F32, False), (BF16, False)],
                         rows=[(proj, ncr, 2 * D // tnm), (proj, ncr, 3 * D // tnm)])
    mix, x2 = _mm("out_proj", [(merged, Wt['w_out'], D)], L, D, tm=tr, tn=_div(D, 1024), epi=res_epi(1.0),
                  outs=[(F32, False), (F32, False)], rows=[(x1, ncr, 0)], vecs=[g2])
    ffn2_weights(x2)
    h3 = norm_mod("norm3", x2, ng[2:3], sh3, sc3)
    a3, b3, s3, f3, x3 = ffn_fwd("ffn2", h3, x2, g3)

    def loss_fn(i, yt, tt_):
        diff = yt - tt_
        return [diff * (1.0 / D), jnp.sum(diff * diff, axis=0, keepdims=True)]

    dy, sq = _rowk("loss", loss_fn, L, tr, [(x3, 'row'), (loss_target[0], 'row')],
                   [((L, D), F32, 'row'), ((1, D), F32, 'acc')])
    loss = lax.psum(0.5 * jnp.sum(sq) / D, ("x", "y", "c"))

    def res_bwd(name, dxo, f_, gate, coef):
        rows, nrow = dxo.shape[0], gate.shape[0]

        def fn(i, dt, ft, gt):
            return [(coef * sel(i, gt)) * dt, put(i, jnp.sum(dt * ft, axis=0, keepdims=True) * coef, nrow)]

        return _rowk(name, fn, rows, tr, [(dxo, 'row'), (f_, 'row'), (gate, 'vec')],
                     [((rows, D), BF16, 'row'), ((nrow, D), F32, 'acc')])

    def swiglu_bwd_epi(accs, rows, vecs, ri):
        ds_, a_, b_ = accs[0], rows[0], rows[1]
        sg = _sigmoid(a_)
        return [ds_ * b_ * (sg * (1.0 + a_ * (1.0 - sg))), ds_ * (a_ * sg)]

    def norm_mod_bwd(name, xv, g, sh, sc, dh, dres, dres_kind):
        rows, nrow = xv.shape[0], sh.shape[0]

        def fn(i, xt, gt, sht, sct, dht, rest):
            _, vjp = jax.vjp(_norm_mod, xt, gt, sel(i, sht), sel(i, sct))
            dx_, dg_, dsh_, dsc_ = vjp(dht)
            dx_ = dx_ + (jnp.where(i >= ncr, rest, 0.0) if dres_kind == 'xrow' else rest)
            return [dx_, dg_, put(i, dsh_, nrow), put(i, dsc_, nrow)]

        return _rowk(name, fn, rows, tr,
                     [(xv, 'row'), (g, 'vec'), (sh, 'vec'), (sc, 'vec'), (dh, 'row'), (dres, dres_kind)],
                     [((rows, D), F32, 'row'), ((1, D), F32, 'acc'), ((nrow, D), F32, 'acc'), ((nrow, D), F32, 'acc')],
                     nc=ncr)

    def ffn_bwd(tag, dxo, h, a_, b_, s_, f_, gate, wg, wu, wd, on_dwd=None):
        rows = dxo.shape[0]
        df, dgate = res_bwd(tag + "_dres", dxo, f_, gate, 0.5)
        dwd = _mm(tag + "_dwd", [(s_, df, rows)], F, D, tm=_div(F, 512), tn=_div(D, 1024), ta=True, epi=ident,
                  outs=[(BF16, False)])[0].reshape(N_CHIPS, F4, D)
        if on_dwd is not None:
            on_dwd(dwd)
        da, db = _mm(tag + "_dact", [(df, wd, D)], rows, F, tm=_div(rows, 384), tn=F4, tb=True, epi=swiglu_bwd_epi,
                     outs=[(BF16, False), (BF16, False)], rows=[(a_, 0, 0), (b_, 0, 0)])
        dwg = _mm(tag + "_dwg", [(h, da, rows)], D, F, tm=_div(D, 512), tn=F4, ta=True, epi=ident,
                  outs=[(BF16, True)])[0]
        dwu = _mm(tag + "_dwu", [(h, db, rows)], D, F, tm=_div(D, 512), tn=F4, ta=True, epi=ident,
                  outs=[(BF16, True)])[0]
        dh = _mm(tag + "_dh", [(da, wg, F), (db, wu, F)], rows, D, tm=_div(rows, 768), tn=_div(D, 1024), nk=N_CHIPS,
                 tb=True, epi=ident, outs=[(F32, False)], summed=True)[0]
        return dh, dgate, dwg, dwu, dwd

    dh3, dg3, dwg2, dwu2, dwd2 = ffn_bwd("ffn2", dy, h3, a3, b3, s3, f3, g3, Wt['w_ffn2_gate'], Wt['w_ffn2_up'],
                                         Wt['w_ffn2_down'])
    tok_r1, scatter_fin1 = _scatter_split("scatter_ffn2", [dwg2, dwu2, dwd2], dg3)
    dx2, dng3, dsh3, dsc3 = norm_mod_bwd("norm3_bwd", x2, ng[2:3], sh3, sc3, dh3, dy, 'row')
    dmix, dg2 = res_bwd("mix_dres", dx2, mix, g2 + tok_r1[0:1, 0:1], 1.0)

    def dmerge_epi(accs, rows, vecs, ri):
        dm_, ba_, bs_ = accs[0], rows[0], rows[1]
        ga, gs = _sigmoid(rows[2]), _sigmoid(rows[3])
        return [dm_ * ga, dm_ * gs, dm_ * ba_ * ga * (1.0 - ga), dm_ * bs_ * gs * (1.0 - gs)]

    tnd = _div(D, 1024)
    dba, dbs, dga, dgs = _mm("dmerge", [(dmix, Wt['w_out'], D)], L, D, tm=tr, tn=tnd, tb=True, epi=dmerge_epi,
                             outs=[(BF16, False)] * 4,
                             rows=[(ba, 0, 0), (bs, 0, 0), (proj, ncr, 2 * D // tnd), (proj, ncr, 3 * D // tnd)])
    dwout = _mm("dw_out", [(merged, dmix, L)], D, D, tm=_div(D, 512), tn=_div(D, 1024), ta=True, epi=ident,
                outs=[(BF16, False)])[0].reshape(N_CHIPS, Dq, D)
    dattn = _mm("dattn", [(dba, Wt['w_br_attn'], D)], L, D, tm=_div(L, 512), tn=_div(D, 1024), tb=True, epi=ident,
                outs=[(BF16, False)])[0]
    dwba = _mm("dw_br_attn", [(attn, dba, L)], D, D, tm=_div(D, 512), tn=_div(D, 1024), ta=True, epi=ident,
               outs=[(BF16, False)])[0].reshape(N_CHIPS, Dq, D)
    dy2 = _mm("dy2", [(dbs, Wt['w_br_ssm'], D)], L, W, tm=_div(L, 512), tn=_div(W, 1024), nk=N_CHIPS, tb=True,
              epi=ident, outs=[(F32, False)])[0]
    dwbs = _mm("dw_br_ssm", [(y2, dbs, L)], W, D, tm=_div(W, 512), tn=_div(Dq, 512), ta=True, epi=ident,
               outs=[(BF16, True)])[0]

    def glu_bwd_fn(i, d2, ygt, zt):
        sz = _sigmoid(zt)
        dz_ = d2 * ygt * sz * (1.0 - sz)
        return [dz_, d2 * sz, jnp.sum(dz_, axis=0, keepdims=True)]

    dz, dyd, dbglu = _rowk("glu_bwd", glu_bwd_fn, L, tr, [(dy2, 'row'), (yg, 'row'), (zglu, 'row')],
                           [((L, W), BF16, 'row'), ((L, W), F32, 'row'), ((1, W), F32, 'acc')])

    def dssm_epi(accs, rows, vecs, ri):
        _, vjp = jax.vjp(_gelu, rows[1])
        ds_ = vjp(accs[0] + rows[0])[0]
        return [ds_, ds_]

    dssm, dssm_b = _mm("dssm", [(dz, Wt['w_glu'], W)], L, W, tm=_div(L, 512), tn=_div(W, 512), tb=True, epi=dssm_epi,
                       outs=[(F32, False), (BF16, False)], rows=[(dyd, 0, 0), (ssm_pre, 0, 0)])
    dwglu = _mm("dw_glu", [(ygb, dz, L)], W, W, tm=_div(W, 512), tn=_div(W, 1024), ta=True, epi=ident,
                outs=[(BF16, False)])[0].reshape(N_CHIPS, W // N_CHIPS, W)
    tok_r2a, scatter_fin2a = _scatter_split("scatter_mix", [dwglu, dwba, dwbs, dwout], dbglu)
    dssm_full = jnp.concatenate([jnp.zeros((Lc, W), BF16), dssm_b], axis=0)
    dus, dlam_re, dlam_im, dcoef_re, dcoef_im, dbf, dcf_re, dcf_im = [], [], [], [], [], [], [], []
    for d in range(2):
        r = _ssm_bwd("ssm_bwd%d" % d, dssm_full, hs_re[d], hs_im[d], ub, bbd[d], bbdt_re[d], bbdt_im[d],
                     cbdt_re[d], cbdt_im[d], lam_re[d] + tok_r2a[0:1, 0:1], lam_im[d], coef_re[d], coef_im[d], Lc,
                     reverse=bool(d))
        for lst, val in zip((dus, dlam_re, dlam_im, dcoef_re, dcoef_im, dbf, dcf_re, dcf_im), r):
            lst.append(val)
    dqr, dkr, dvf = _attn_bwd(qr, kr, vb, dattn, L, Lc, D)

    def prep_bwd_fn(i, qt, kt, ut, dqt, dkt, dvt, du0, du1, dst, dt, qgt, kgt, ct, st):
        live = i >= ncr
        dqt = jnp.where(live, dqt, 0.0)
        dst = jnp.where(live, dst, 0.0)
        dqs, dks = [], []
        dqg_ = jnp.zeros((1, HEAD_DIM), F32)
        dkg_ = jnp.zeros((1, HEAD_DIM), F32)
        for h in range(nh):
            hl = slice(h * HEAD_DIM, (h + 1) * HEAD_DIM)
            dn = dqt[:, hl] * ct + _rot(dqt[:, hl] * st)
            _, vjp = jax.vjp(_head_norm, qt[:, hl], qgt)
            dxh, dgh = vjp(dn)
            dqs.append(dxh)
            dqg_ = dqg_ + dgh
        for h in range(nkvh):
            hl = slice(h * HEAD_DIM, (h + 1) * HEAD_DIM)
            dn = dkt[:, hl] * ct + _rot(dkt[:, hl] * st)
            _, vjp = jax.vjp(_head_norm, kt[:, hl], kgt)
            dxh, dgh = vjp(dn)
            dks.append(dxh)
            dkg_ = dkg_ + dgh
        du_ = du0 + du1 + dst * dt
        return [jnp.concatenate(dqs, axis=1), jnp.concatenate(dks, axis=1), dvt, du_, dqg_, dkg_,
                jnp.sum(dst * ut, axis=0, keepdims=True)]

    dq_b, dk_b, dv_b, du_b, dqg, dkg, dssd = _rowk(
        "qk_prep_bwd", prep_bwd_fn, T, tr,
        [(proj, ('col', D, 1)), (proj, ('col', KV, 0)), (proj, ('col', W, 1)), (dqr, 'xrow'), (dkr, 'row'),
         (dvf, 'row'), (dus[0], 'row'), (dus[1], 'row'), (dssm, 'xrow'), (ssm_d, 'vec'), (qg, 'vec'), (kg, 'vec'),
         (cos_t, 'row'), (sin_t, 'row')],
        [((T, D), BF16, 'row'), ((T, KV), BF16, 'row'), ((T, KV), BF16, 'row'), ((T, W), BF16, 'row'),
         ((1, HEAD_DIM), F32, 'acc'), ((1, HEAD_DIM), F32, 'acc'), ((1, W), F32, 'acc')], nc=ncr)
    dgate = jnp.concatenate([jnp.zeros((Lc, 2 * D), BF16), jnp.concatenate([dga, dgs], axis=1)], axis=0)
    dproj = jnp.concatenate([dk_b, dv_b, du_b, dq_b, dgate], axis=1)
    dh2 = _mm("in_proj_dx", [(dproj, Wt['w_in'], 4 * D)], T, D, tm=_div(T, 768), tn=_div(D, 1024), nk=N_CHIPS, tb=True,
              epi=ident, outs=[(F32, False)])[0]
    dwin = _mm("in_proj_dw", [(h2, dproj, T)], D, 4 * D, tm=_div(D, 512), tn=_div(D, 1024), ta=True, epi=ident,
               outs=[(BF16, True)])[0]
    tok_r2, scatter_fin2 = _scatter_split("scatter_w_in", [dwin], dqg)
    dx1, dng2, dsh2, dsc2 = norm_mod_bwd("norm2_bwd", x1, ng[1:2] + tok_r2[0:1, 0:1], sh2, sc2, dh2, dx2, 'xrow')
    early = {}

    def start_down(dwd):
        early['tok'], early['fin'] = _scatter_split("scatter_ffn1_down", [dwd], dg2)

    dh1, dg1, dwg1, dwu1, dwd1 = ffn_bwd("ffn1", dx1, h1, a1, b1, s1, f1, g1, Wt['w_ffn1_gate'], Wt['w_ffn1_up'],
                                         Wt['w_ffn1_down'], on_dwd=start_down)
    dx0, dng1, dsh1, dsc1 = norm_mod_bwd("norm1_bwd", xc, ng[0:1] + early['tok'][0:1, 0:1], sh1, sc1, dh1, dx1, 'row')
    grad_x = dx0[Lc:][None]

    zD = jnp.zeros((1, D), F32)
    dmod_x = jnp.concatenate([dsh1[1:2], dsc1[1:2], dg1[1:2], dsh2[1:2], dsc2[1:2], dg2, dsh3, dsc3, dg3], axis=1)
    dmod_c = jnp.concatenate([dsh1[0:1], dsc1[0:1], dg1[0:1], dsh2[0:1], dsc2[0:1], zD, zD, zD, zD], axis=1)
    pieces = [dmod_x, dmod_c, dng1, dng2, dng3, dqg, dkg] + dlam_re + dlam_im + dcoef_re + dcoef_im \
        + dbf + dcf_re + dcf_im + [dssd, dbglu]
    shapes = [p_.shape for p_ in pieces]
    pack = _pack(pieces)
    RP = pack.shape[0]
    allp = _allgather_small("gather_small", pack).reshape(N_DEV, RP, PACK_W)

    head_rows = -(-18 * D // PACK_W)
    head = allp[:, :head_rows].reshape(N_DEV, head_rows * PACK_W)
    dmx_all = head[:, :9 * D]

    def sum_rows_fn(i, t):
        s_ = t[0:1]
        for k in range(1, N_DEV):
            s_ = s_ + t[k:k + 1]
        return [s_]

    dmc_sum = _rowk("sum_dmod_c", sum_rows_fn, 1, 1, [(head[:, 9 * D:18 * D], 'vec')], [((1, 9 * D), F32, 'row')])[0]
    cots = jnp.concatenate([dmx_all, dmc_sum, jnp.zeros((7, 9 * D), F32)], axis=0)
    cots_sh = lax.dynamic_slice(cots, (0, chip * NM), (16, NM))
    part = _mm("cctx_part", [(cots_sh[8:16], wm, NM)], 8, D, tm=8, tn=_div(D, 1024), nk=NM // _div(NM, 1152), tb=True,
               epi=ident, outs=[(F32, False)], a_pro=to_bf, b_pro=to_bf)[0]
    parts = _allgather_small("gather_cctx", part).reshape(N_CHIPS, 2, 8, D)[:, 0, 0]

    def cctx_fn(i, pt, ct):
        ds_ = ((pt[0:1] + pt[1:2]) + pt[2:3]) + pt[3:4]
        _, vjp = jax.vjp(lambda v: v * _sigmoid(v), ct)
        return [vjp(ds_)[0]]

    g_cctx = _rowk("cctx_grad", cctx_fn, 1, 1, [(parts, 'vec'), (c_ctx[None], 'row')], [((1, D), F32, 'row')])[0]

    tok_r3, scatter_fin3 = _scatter_split("scatter_ffn1_up", [dwg1, dwu1], g_cctx)
    zero = tok_r3[0:1, 0:1]

    def sum_dev_fn(i, t):
        s_ = t[0]
        for k in range(1, N_DEV):
            s_ = s_ + t[k]
        return [s_]

    tot = _rowk("sum_small", sum_dev_fn, RP, 8, [(allp, 'row3')], [((RP, PACK_W), F32, 'row')])[0]
    (t_dmod_x, t_dmod_c, t_ng1, t_ng2, t_ng3, t_qg, t_kg, t_lr0, t_lr1, t_li0, t_li1, t_kr0, t_kr1, t_ki0, t_ki1,
     t_dbf0, t_dbf1, t_dcr0, t_dcr1, t_dci0, t_dci1, t_d, t_bglu) = _unpack(tot, shapes)
    b_grad = lambda t, lo: jnp.transpose(t[:, :, lo:lo + P].reshape(G, E, P), (0, 2, 1))
    c_grad = lambda t: jnp.transpose(t.reshape(nslab, P, SLAB_GROUPS, E), (0, 2, 3, 1)).reshape(G, E, P)
    cat2 = lambda u0, u1: jnp.concatenate([u0.reshape(G, P), u1.reshape(G, P)], axis=0)
    g_are, g_aim, g_ldt = _zoh_bwd(a_re2 + zero, a_im2, ldt2, [cat2(t_lr0, t_lr1), cat2(t_li0, t_li1),
                                                                cat2(t_kr0, t_kr1), cat2(t_ki0, t_ki1)])
    g_bmod = _rowk("bmod_grad", lambda i, u0, u1: [u0 + u1], 1, 1, [(t_dmod_x, 'row'), (t_dmod_c, 'row')],
                   [((1, 9 * D), F32, 'row')])[0]
    g_wmod = _outer_sum(acts + zero, cots_sh)
    results = {}
    results['w_mod'] = _adamw("adamw_w_mod", w_mod, m_w_mod, v_w_mod, [g_wmod])

    def reduce_group(tag, names, grads, fin, after_work):
        sent, landed = fin(after_work)
        plane = [_sum_plane("sum_" + n, g_, rb, chip_index) for n, g_, rb in zip(names, sent, landed)]
        other = _swap_sibling("swap_" + tag, plane)
        for n, mine, theirs in zip(names, plane, other):
            results[n] = _adamw("adamw_" + n, A[n], A['m_' + n], A['v_' + n], [mine, theirs])

    reduce_group("ffn2", big[3:6], [dwg2, dwu2, dwd2], scatter_fin1, tok_r3)
    reduce_group("mix", big[7:11], [dwglu, dwba, dwbs, dwout], scatter_fin2a, results['w_ffn2_down'][0])
    reduce_group("w_in", big[6:7], [dwin], scatter_fin2, results['w_out'][0])
    reduce_group("ffn1_down", big[2:3], [dwd1], early['fin'], results['w_out'][0])
    reduce_group("ffn1_up", big[0:2], [dwg1, dwu1], scatter_fin3, results['w_ffn1_down'][0])

    ng_full =jnp.concatenate([t_ng1, t_ng2, t_ng3], axis=0)
    gsmall = {
        'c_ctx': g_cctx, 'b_mod': g_bmod, 'norm_g': lax.dynamic_slice(ng_full, (0, chip * Dq), (3, Dq)),
        'q_norm_g': t_qg, 'k_norm_g': t_kg, 'ssm_a_re': g_are, 'ssm_a_im': g_aim, 'ssm_log_dt': g_ldt,
        'ssm_b_re': jnp.stack([b_grad(t_dbf0, 0), b_grad(t_dbf1, 0)]),
        'ssm_b_im': jnp.stack([b_grad(t_dbf0, P), b_grad(t_dbf1, P)]),
        'ssm_c_re': jnp.stack([c_grad(t_dcr0), c_grad(t_dcr1)]), 'ssm_c_im': jnp.stack([c_grad(t_dci0), c_grad(t_dci1)]),
        'ssm_d': t_d, 'b_glu': t_bglu}
    sshapes = [A[n].shape for n in small]
    sres = _adamw("adamw_small", packs_wmv[0], packs_wmv[1], packs_wmv[2], [_pack([gsmall[n] for n in small])])
    sres = [_unpack(b_, sshapes) for b_ in sres]
    for k, n in enumerate(small):
        results[n] = tuple(sres[q][k] for q in range(4))

    order = ['c_ctx', 'w_mod', 'b_mod', 'norm_g', 'w_ffn1_gate', 'w_ffn1_up', 'w_ffn1_down', 'w_in', 'q_norm_g',
             'k_norm_g', 'ssm_a_re', 'ssm_a_im', 'ssm_log_dt', 'ssm_b_re', 'ssm_b_im', 'ssm_c_re', 'ssm_c_im',
             'ssm_d', 'w_glu', 'b_glu', 'w_br_attn', 'w_br_ssm', 'w_out', 'w_ffn2_gate', 'w_ffn2_up', 'w_ffn2_down']
    outs = [loss, grad_x]
    for q in range(4):
        outs += [results[n][q].reshape(A[n].shape) for n in order]
    return tuple(outs)
```

```python
import math

import jax
import jax.numpy as jnp
from jax import lax
from jax.experimental import pallas as pl
from jax.experimental.pallas import tpu as pltpu

F32 = jnp.float32
BF16 = jnp.bfloat16
MESH = pl.DeviceIdType.MESH

NORM_EPS = 1e-6
ROPE_THETA = 10000.0
GRID_W = 64
HEAD_DIM = 128
Q_PER_KV = 4
SSM_GROUP = 16
SSM_STATE = 64
ADAM_LR = 0.001
ADAM_B1 = 0.9
ADAM_B2 = 0.999
ADAM_EPS = 1e-08
ADAM_WD = 0.01
ADAM_STEP = 10

N_CHIPS = 4
N_DEV = 8
LANES = 128
SLAB_CH = 128
SLAB_GROUPS = SLAB_CH // SSM_GROUP
SLAB_ST = SLAB_GROUPS * SSM_STATE
VMEM_LIMIT_BYTES = 56 * 1024 * 1024
PACK_W = 1024


def _cparams(**kw):
    return pltpu.CompilerParams(vmem_limit_bytes=VMEM_LIMIT_BYTES, **kw)


def _div(n, pref, mult=LANES):
    t = (min(pref, n) // mult) * mult
    while t >= mult:
        if n % t == 0:
            return t
        t -= mult
    return n


def _sigmoid(x):
    return jax.nn.sigmoid(x)


def _gelu(x):
    return x * (0.5 * (1.0 + jnp.tanh(math.sqrt(2.0 / math.pi) * (x + 0.044715 * (x * x * x)))))


def _rowk(name, fn, nrows, tr, ins, outs, nc=0):
    nt = nrows // tr
    in_specs, arrays = [], []
    for arr, kind in ins:
        arrays.append(arr)
        if kind == 'row':
            in_specs.append(pl.BlockSpec((tr, arr.shape[1]), lambda i: (i, 0)))
        elif kind == 'xrow':
            in_specs.append(pl.BlockSpec((tr, arr.shape[1]), lambda i: (jnp.maximum(i - nc, 0), 0)))
        elif kind == 'orow':
            in_specs.append(pl.BlockSpec((tr, arr.shape[1]), lambda i: (i + nc, 0)))
        elif kind == 'vec':
            in_specs.append(pl.BlockSpec(arr.shape, lambda i, nd=arr.ndim: (0,) * nd))
        elif kind == 'row3':
            in_specs.append(pl.BlockSpec((arr.shape[0], tr, arr.shape[2]), lambda i: (0, i, 0)))
        elif kind == 'row1':
            in_specs.append(pl.BlockSpec((None, tr, arr.shape[2]), lambda i: (0, i, 0)))
        elif kind[0] == 'ocol':
            _, width, blk = kind
            in_specs.append(pl.BlockSpec((tr, width), lambda i, blk=blk: (i + nc, blk)))
        else:
            _, width, blk = kind
            in_specs.append(pl.BlockSpec((tr, width), lambda i, blk=blk: (i, blk)))
    out_shape, out_specs = [], []
    for shape, dtype, kind in outs:
        out_shape.append(jax.ShapeDtypeStruct(shape, dtype))
        if kind == 'row':
            out_specs.append(pl.BlockSpec((tr, shape[1]), lambda i: (i, 0)))
        elif kind == 'row1':
            out_specs.append(pl.BlockSpec((None, tr, shape[2]), lambda i: (0, i, 0)))
        else:
            out_specs.append(pl.BlockSpec(shape, lambda i, nd=len(shape): (0,) * nd))
    nin = len(ins)

    def body(*refs):
        i = pl.program_id(0)
        res = fn(i, *[r[...] for r in refs[:nin]])
        for (shape, dtype, kind), ref, val in zip(outs, refs[nin:], res):
            if kind in ('row', 'row1'):
                ref[...] = val.astype(dtype)
            else:
                @pl.when(i == 0)
                def _():
                    ref[...] = val.astype(dtype)

                @pl.when(i > 0)
                def _():
                    ref[...] += val.astype(dtype)

    return pl.pallas_call(body, name=name, grid=(nt,), in_specs=in_specs, out_specs=out_specs,
                          out_shape=out_shape, compiler_params=_cparams())(*arrays)


def _mm(name, pairs, M, N, *, tm, tn, nk=1, epi, outs, ta=False, tb=False, rows=(), vecs=(),
        a_pro=None, b_pro=None, n_outer=True, summed=False):
    nm, nn = M // tm, N // tn
    npair = len(pairs)

    def idx(f):
        if n_outer:
            return lambda j, i, k: f(i, j, k)
        return lambda i, j, k: f(i, j, k)

    in_specs, args = [], []
    for a, b, K in pairs:
        tk = K // nk
        if ta:
            in_specs.append(pl.BlockSpec((tk, tm), idx(lambda i, j, k: (k, i))))
        else:
            in_specs.append(pl.BlockSpec((tm, tk), idx(lambda i, j, k: (i, k))))
        args.append(a)
        if b.ndim == 3:
            if tb:
                per = b.shape[2] // tk
                in_specs.append(pl.BlockSpec((None, tn, tk), idx(lambda i, j, k, per=per: (k // per, j, k % per))))
            else:
                per = b.shape[2] // tn
                in_specs.append(pl.BlockSpec((None, tk, tn), idx(lambda i, j, k, per=per: (j // per, k, j % per))))
        elif tb:
            in_specs.append(pl.BlockSpec((tn, tk), idx(lambda i, j, k: (j, k))))
        else:
            in_specs.append(pl.BlockSpec((tk, tn), idx(lambda i, j, k: (k, j))))
        args.append(b)
    for arr, ro, co in rows:
        in_specs.append(pl.BlockSpec((tm, tn), idx(lambda i, j, k, ro=ro, co=co: (i + ro, j + co))))
        args.append(arr)
    for arr in vecs:
        in_specs.append(pl.BlockSpec((arr.shape[0], tn), idx(lambda i, j, k: (0, j))))
        args.append(arr)
    out_shape, out_specs = [], []
    for dtype, chunked in outs:
        if chunked:
            per = (N // N_CHIPS) // tn
            out_shape.append(jax.ShapeDtypeStruct((N_CHIPS, M, N // N_CHIPS), dtype))
            out_specs.append(pl.BlockSpec((None, tm, tn), idx(lambda i, j, k, per=per: (j // per, i, j % per))))
        else:
            out_shape.append(jax.ShapeDtypeStruct((M, N), dtype))
            out_specs.append(pl.BlockSpec((tm, tn), idx(lambda i, j, k: (i, j))))
    nacc = 1 if summed else npair
    scratch = [pltpu.VMEM((tm, tn), F32) for _ in range(nacc)] if nk > 1 else []
    nrow, nvec, nout = len(rows), len(vecs), len(outs)
    dims = (((0 if ta else 1,), (1 if tb else 0,)), ((), ()))

    def body(*refs):
        ab = refs[:2 * npair]
        row_refs = refs[2 * npair:2 * npair + nrow]
        vec_refs = refs[2 * npair + nrow:2 * npair + nrow + nvec]
        out_refs = refs[2 * npair + nrow + nvec:2 * npair + nrow + nvec + nout]
        acc_refs = refs[2 * npair + nrow + nvec + nout:]
        if n_outer:
            j, i, k = pl.program_id(0), pl.program_id(1), pl.program_id(2)
        else:
            i, j, k = pl.program_id(0), pl.program_id(1), pl.program_id(2)

        def part(p):
            av, bv = ab[2 * p][...], ab[2 * p + 1][...]
            if a_pro is not None:
                av = a_pro(av)
            if b_pro is not None:
                bv = b_pro(bv)
            return lax.dot_general(av, bv, dims, preferred_element_type=F32)

        def finish(accs):
            row_index = i * tm + lax.broadcasted_iota(jnp.int32, (tm, 1), 0)
            res = epi(accs, [r[...] for r in row_refs], [v[...] for v in vec_refs], row_index)
            for ref, val in zip(out_refs, res):
                ref[...] = val.astype(ref.dtype)

        parts = [part(p) for p in range(npair)]
        if summed:
            total = parts[0]
            for extra in parts[1:]:
                total = total + extra
            parts = [total]
        if nk == 1:
            finish(parts)
        else:
            @pl.when(k == 0)
            def _():
                for q in range(nacc):
                    acc_refs[q][...] = parts[q]

            @pl.when(jnp.logical_and(k > 0, k < nk - 1))
            def _():
                for q in range(nacc):
                    acc_refs[q][...] += parts[q]

            @pl.when(k == nk - 1)
            def _():
                finish([acc_refs[q][...] + parts[q] for q in range(nacc)])

    grid = (nn, nm, nk) if n_outer else (nm, nn, nk)
    return pl.pallas_call(body, name=name, grid=grid, in_specs=in_specs, out_specs=out_specs,
                          out_shape=out_shape, scratch_shapes=scratch, compiler_params=_cparams())(*args)


def _split3(v):
    v0 = v.astype(BF16)
    r1 = v - v0.astype(F32)
    v1 = r1.astype(BF16)
    v2 = (r1 - v1.astype(F32)).astype(BF16)
    return v0, v1, v2


def _mesh_pos():
    return lax.axis_index("x"), lax.axis_index("y"), lax.axis_index("c")


def _allgather_small(name, x):
    m, n = x.shape

    def body(x_ref, out_ref, send_sems, recv_sems, local_sem):
        xi, yi, ci = _mesh_pos()
        me, sibling = (xi, yi, ci), (xi, yi, 1 - ci)
        chips = [(1 - xi, yi), (xi, 1 - yi), (1 - xi, 1 - yi)]

        def rows(px, py, pc):
            return out_ref.at[pl.ds((4 * px + 2 * py + pc) * m, m), :]

        def copy(k, block, to, src=None):
            return pltpu.make_async_remote_copy(
                src_ref=rows(*block) if src is None else src, dst_ref=rows(*block),
                send_sem=send_sems.at[k], recv_sem=recv_sems.at[k], device_id=to, device_id_type=MESH)

        mine = pltpu.make_async_copy(x_ref, rows(*me), local_sem)
        mine.start()
        first = [copy(0, me, sibling, src=x_ref)]
        first += [copy(1 + j, me, (*chip, ci), src=x_ref) for j, chip in enumerate(chips)]
        for cp in first:
            cp.start()
        passed = [copy(4 + j, (*chip, ci), sibling) for j, chip in enumerate(chips)]
        for j, chip in enumerate(chips):
            copy(1 + j, (*chip, ci), me).wait_recv()
            passed[j].start()
        copy(0, sibling, me).wait_recv()
        for j, chip in enumerate(chips):
            copy(4 + j, (*chip, 1 - ci), me).wait_recv()
        for cp in first + passed:
            cp.wait_send()
        mine.wait()

    return pl.pallas_call(
        body, name=name, out_shape=jax.ShapeDtypeStruct((N_DEV * m, n), x.dtype),
        in_specs=[pl.BlockSpec(memory_space=pltpu.VMEM)], out_specs=pl.BlockSpec(memory_space=pltpu.VMEM),
        scratch_shapes=[pltpu.SemaphoreType.DMA((7,)), pltpu.SemaphoreType.DMA((7,)), pltpu.SemaphoreType.DMA],
        compiler_params=_cparams())(x)


_HBM = pl.BlockSpec(memory_space=pltpu.HBM)
_SEM = pl.BlockSpec(memory_space=pltpu.SEMAPHORE)
_ANY = pl.BlockSpec(memory_space=pl.ANY)
_EFFECT = pltpu.SideEffectType.DATAFLOW_SIDE_EFFECTING


def _in_hbm(v):
    return pltpu.with_memory_space_constraint(v, pltpu.HBM)


def _other_chips(xi, yi):
    return [(1 - xi, yi), (xi, 1 - yi), (1 - xi, 1 - yi)]


def _guarded(core, fn):
    if core is None:
        fn()
    else:
        pl.when(lax.axis_index("c") == core)(fn)


def _split_copies(name, srcs, lands, after, pairs, senders, receivers, ncopy):
    ns, nl = len(srcs), len(lands)
    dma = pltpu.SemaphoreType.DMA((ncopy,))
    thru = [pltpu.HBM(v.shape, v.dtype) for v in list(srcs) + list(lands)]

    def start_body(*refs):
        src_refs, land_refs = refs[:ns], refs[ns:ns + nl]
        descs = pairs(src_refs, land_refs, refs[ns + nl + 1], refs[ns + nl + 2])

        def go():
            for send, _ in descs:
                send.start()

        _guarded(senders, go)
        refs[-1][...] = jnp.zeros_like(refs[-1])

    res = pl.pallas_call(
        start_body, name=name + "_start",
        out_shape=(dma, dma, *thru, jax.ShapeDtypeStruct((8, LANES), F32)),
        in_specs=[_HBM] * (ns + nl) + [_ANY],
        out_specs=(_SEM, _SEM, *([_HBM] * (ns + nl)), pl.BlockSpec(memory_space=pltpu.VMEM)),
        input_output_aliases={k: 2 + k for k in range(ns + nl)},
        compiler_params=_cparams(has_side_effects=_EFFECT),
    )(*[_in_hbm(v) for v in srcs], *[_in_hbm(v) for v in lands], after)
    send_sems, recv_sems, token = res[0], res[1], res[-1]
    carried = res[2:2 + ns + nl]

    def finish(after_work):
        def wait_body(*refs):
            src_refs, land_refs = refs[:ns], refs[ns:ns + nl]
            descs = pairs(src_refs, land_refs, refs[ns + nl], refs[ns + nl + 1])

            def sent():
                for send, _ in descs:
                    send.wait_send()

            def landed():
                for _, recv in descs:
                    recv.wait_recv()

            _guarded(senders, sent)
            _guarded(receivers, landed)

        out = pl.pallas_call(
            wait_body, name=name + "_wait", out_shape=tuple(thru),
            in_specs=[_HBM] * (ns + nl) + [_SEM, _SEM, _ANY], out_specs=tuple([_HBM] * (ns + nl)),
            input_output_aliases={k: k for k in range(ns + nl)},
            compiler_params=_cparams(has_side_effects=_EFFECT),
        )(*carried, send_sems, recv_sems, after_work)
        return list(out[:ns]), list(out[ns:])

    return token, finish


def _cast_slot(name, w, chip_index):
    R, C = w.shape[1:]
    tr = _div(R, max(16, 524288 // C), mult=16)

    def body(chip_ref, w_ref, o_ref):
        o_ref[...] = w_ref[...].astype(BF16)

    return pl.pallas_call(
        body, name=name, out_shape=jax.ShapeDtypeStruct((N_CHIPS, R, C), BF16),
        grid_spec=pltpu.PrefetchScalarGridSpec(
            num_scalar_prefetch=1, grid=(R // tr,),
            in_specs=[pl.BlockSpec((None, tr, C), lambda i, chip_ref: (0, i, 0))],
            out_specs=pl.BlockSpec((None, tr, C), lambda i, chip_ref: (chip_ref[0], i, 0))),
        compiler_params=_cparams())(chip_index, w)


def _sum_plane(name, grads, landed, chip_index):
    R, C = grads.shape[1:]
    tr = _div(R, max(16, 262144 // C), mult=16)

    def body(chip_ref, own_ref, land_ref, o_ref):
        o_ref[...] = ((own_ref[...].astype(F32) + land_ref[0].astype(F32)) + land_ref[1].astype(F32)) \
            + land_ref[2].astype(F32)

    return pl.pallas_call(
        body, name=name, out_shape=jax.ShapeDtypeStruct((R, C), F32),
        grid_spec=pltpu.PrefetchScalarGridSpec(
            num_scalar_prefetch=1, grid=(R // tr,),
            in_specs=[pl.BlockSpec((None, tr, C), lambda i, chip_ref: (chip_ref[0], i, 0)),
                      pl.BlockSpec((3, tr, C), lambda i, chip_ref: (0, i, 0))],
            out_specs=pl.BlockSpec((tr, C), lambda i, chip_ref: (i, 0))),
        compiler_params=_cparams())(chip_index, grads, landed)


def _gather_split(name, lands, after):
    def pairs(src_refs, land_refs, send_sems, recv_sems):
        xi, yi, _ = _mesh_pos()
        mine = 2 * xi + yi
        out = []
        for a in range(len(lands)):
            for j, (px, py) in enumerate(_other_chips(xi, yi)):
                def to_slot(slot, a=a, j=j, px=px, py=py):
                    return pltpu.make_async_remote_copy(
                        src_ref=land_refs[a].at[mine], dst_ref=land_refs[a].at[slot], send_sem=send_sems.at[3 * a + j],
                        recv_sem=recv_sems.at[3 * a + j], device_id=(px, py, 1), device_id_type=MESH)
                out.append((to_slot(mine), to_slot(2 * px + py)))
        return out

    return _split_copies(name, [], lands, after, pairs, senders=1, receivers=1, ncopy=3 * len(lands))


def _allgather_split(name, block, me, after):
    land = lax.dynamic_update_slice(lax.empty((N_DEV,) + block.shape, block.dtype), block[None], (me, 0, 0))

    def pairs(src_refs, land_refs, send_sems, recv_sems):
        xi, yi, ci = _mesh_pos()
        mine = 4 * xi + 2 * yi + ci
        out = []
        for k in range(1, N_DEV):
            kx, ky, kc = (k >> 2) & 1, (k >> 1) & 1, k & 1
            px = 1 - xi if kx else xi
            py = 1 - yi if ky else yi
            pc = 1 - ci if kc else ci

            def to_slot(slot, k=k, px=px, py=py, pc=pc):
                return pltpu.make_async_remote_copy(
                    src_ref=land_refs[0].at[mine], dst_ref=land_refs[0].at[slot], send_sem=send_sems.at[k - 1],
                    recv_sem=recv_sems.at[k - 1], device_id=(px, py, pc), device_id_type=MESH)
            out.append((to_slot(mine), to_slot(4 * px + 2 * py + pc)))
        return out

    tok, fin = _split_copies(name, [], [land], after, pairs, senders=None, receivers=None, ncopy=N_DEV - 1)
    return tok, lambda later: fin(later)[1][0]


def _pass_split(name, lands, after):
    def pairs(src_refs, land_refs, send_sems, recv_sems):
        xi, yi, _ = _mesh_pos()
        out = []
        for a in range(len(lands)):
            for j, (px, py) in enumerate(_other_chips(xi, yi)):
                cp = pltpu.make_async_remote_copy(
                    src_ref=land_refs[a].at[2 * px + py], dst_ref=land_refs[a].at[2 * px + py],
                    send_sem=send_sems.at[3 * a + j], recv_sem=recv_sems.at[3 * a + j],
                    device_id=(xi, yi, 0), device_id_type=MESH)
                out.append((cp, cp))
        return out

    return _split_copies(name, [], lands, after, pairs, senders=1, receivers=0, ncopy=3 * len(lands))


def _scatter_split(name, grads, after):
    lands = [lax.empty((3,) + g.shape[1:], g.dtype) for g in grads]

    def pairs(src_refs, land_refs, send_sems, recv_sems):
        xi, yi, ci = _mesh_pos()
        out = []
        for a in range(len(grads)):
            for j, (px, py) in enumerate(_other_chips(xi, yi)):
                cp = pltpu.make_async_remote_copy(
                    src_ref=src_refs[a].at[2 * px + py], dst_ref=land_refs[a].at[j], send_sem=send_sems.at[3 * a + j],
                    recv_sem=recv_sems.at[3 * a + j], device_id=(px, py, ci), device_id_type=MESH)
                out.append((cp, cp))
        return out

    return _split_copies(name, grads, lands, after, pairs, senders=None, receivers=None, ncopy=3 * len(grads))


def _gather_finish(name, lands):
    na = len(lands)

    def body(*refs):
        outs = refs[na:2 * na]
        send_sems, recv_sems = refs[2 * na:]
        xi, yi, ci = _mesh_pos()
        passes = [pltpu.make_async_remote_copy(
            src_ref=outs[a].at[2 * px + py], dst_ref=outs[a].at[2 * px + py],
            send_sem=send_sems.at[a, j], recv_sem=recv_sems.at[a, j], device_id=(xi, yi, 0), device_id_type=MESH)
            for a in range(na) for j, (px, py) in enumerate(_other_chips(xi, yi))]

        @pl.when(ci == 1)
        def _():
            for cp in passes:
                cp.start()
            for cp in passes:
                cp.wait_send()

        @pl.when(ci == 0)
        def _():
            for cp in passes:
                cp.wait_recv()

    return pl.pallas_call(
        body, name=name, out_shape=[jax.ShapeDtypeStruct(v.shape, v.dtype) for v in lands],
        in_specs=[_ANY] * na, out_specs=[_ANY] * na,
        input_output_aliases={a: a for a in range(na)},
        scratch_shapes=[pltpu.SemaphoreType.DMA((na, 3)), pltpu.SemaphoreType.DMA((na, 3))],
        compiler_params=_cparams())(*lands)


def _swap_sibling(name, arrs):
    na = len(arrs)

    def body(*refs):
        ins, outs = refs[:na], refs[na:2 * na]
        send_sems, recv_sems = refs[2 * na:]
        xi, yi, ci = _mesh_pos()
        copies = [pltpu.make_async_remote_copy(
            src_ref=ins[a], dst_ref=outs[a], send_sem=send_sems.at[a], recv_sem=recv_sems.at[a],
            device_id=(xi, yi, 1 - ci), device_id_type=MESH) for a in range(na)]
        for cp in copies:
            cp.start()
        for cp in copies:
            cp.wait()

    return pl.pallas_call(
        body, name=name,
        out_shape=[jax.ShapeDtypeStruct(g.shape, g.dtype) for g in arrs],
        in_specs=[_ANY] * na, out_specs=[_ANY] * na,
        scratch_shapes=[pltpu.SemaphoreType.DMA((na,)), pltpu.SemaphoreType.DMA((na,))],
        compiler_params=_cparams())(*arrs)


def _attn_tiles(L, Lc, D):
    tq = min(256, Lc)
    return tq, L // tq, Lc // tq, D // HEAD_DIM // Q_PER_KV


def _attn_probs(q, k):
    s = lax.dot_general(q, k, (((1,), (1,)), ((), ())), preferred_element_type=F32) * (HEAD_DIM ** -0.5)
    e = jnp.exp(s - jnp.max(s, axis=-1, keepdims=True))
    return e * (1.0 / jnp.sum(e, axis=-1, keepdims=True))


def _attn_fwd(qr, kr, v, L, Lc, D):
    T = L + Lc
    tq, nq, qoff, nkv = _attn_tiles(L, Lc, D)

    def body(q_ref, k_ref, v_ref, o_ref):
        p = _attn_probs(q_ref[...], k_ref[...])
        o_ref[...] = jnp.dot(p.astype(BF16), v_ref[...], preferred_element_type=F32).astype(o_ref.dtype)

    kv_spec = pl.BlockSpec((T, HEAD_DIM), lambda h, r, q: (0, h))
    return pl.pallas_call(
        body, name="attn_fwd", grid=(nkv, Q_PER_KV, nq),
        in_specs=[pl.BlockSpec((tq, HEAD_DIM), lambda h, r, q: (q + qoff, h * Q_PER_KV + r)), kv_spec, kv_spec],
        out_specs=pl.BlockSpec((tq, HEAD_DIM), lambda h, r, q: (q, h * Q_PER_KV + r)),
        out_shape=jax.ShapeDtypeStruct((L, D), BF16), compiler_params=_cparams())(qr, kr, v)


def _attn_bwd(qr, kr, v, do, L, Lc, D):
    T = L + Lc
    tq, nq, qoff, nkv = _attn_tiles(L, Lc, D)
    scale = HEAD_DIM ** -0.5

    def body(q_ref, k_ref, v_ref, do_ref, dq_ref, dk_ref, dv_ref):
        first = jnp.logical_and(pl.program_id(1) == 0, pl.program_id(2) == 0)
        q, k, dout = q_ref[...], k_ref[...], do_ref[...]
        p = _attn_probs(q, k)
        dp = lax.dot_general(dout, v_ref[...], (((1,), (1,)), ((), ())), preferred_element_type=F32)
        ds = (p * (dp - jnp.sum(p * dp, axis=-1, keepdims=True)) * scale).astype(BF16)
        dq_ref[...] = jnp.dot(ds, k, preferred_element_type=F32)
        dk = lax.dot_general(ds, q, (((0,), (0,)), ((), ())), preferred_element_type=F32)
        dv = lax.dot_general(p.astype(BF16), dout, (((0,), (0,)), ((), ())), preferred_element_type=F32)

        @pl.when(first)
        def _():
            dk_ref[...] = dk
            dv_ref[...] = dv

        @pl.when(jnp.logical_not(first))
        def _():
            dk_ref[...] += dk
            dv_ref[...] += dv

    kv_spec = pl.BlockSpec((T, HEAD_DIM), lambda h, r, q: (0, h))
    q_spec = pl.BlockSpec((tq, HEAD_DIM), lambda h, r, q: (q + qoff, h * Q_PER_KV + r))
    o_spec = pl.BlockSpec((tq, HEAD_DIM), lambda h, r, q: (q, h * Q_PER_KV + r))
    return pl.pallas_call(
        body, name="attn_bwd", grid=(nkv, Q_PER_KV, nq),
        in_specs=[q_spec, kv_spec, kv_spec, o_spec], out_specs=[o_spec, kv_spec, kv_spec],
        out_shape=[jax.ShapeDtypeStruct((L, D), F32), jax.ShapeDtypeStruct((T, D // Q_PER_KV), F32),
                   jax.ShapeDtypeStruct((T, D // Q_PER_KV), F32)],
        compiler_params=_cparams())(qr, kr, v, do)


SUB = 8


def _doubling(xr, xi, pw_re, pw_im, lanes, first_power, period, reverse):
    n = xr.shape[0]
    rows = lax.broadcasted_iota(jnp.int32, (n, 1), 0) & (period - 1)
    for k in range(period.bit_length() - 1):
        d = 1 << k
        keep = rows < period - d if reverse else rows >= d
        sr = jnp.where(keep, pltpu.roll(xr, n - d if reverse else d, 0), 0.0)
        si = jnp.where(keep, pltpu.roll(xi, n - d if reverse else d, 0), 0.0)
        pr, pi = pw_re[first_power + k:first_power + k + 1, lanes], pw_im[first_power + k:first_power + k + 1, lanes]
        xr, xi = xr + (pr * sr - pi * si), xi + (pr * si + pi * sr)
    return xr, xi


def _scan_tile(xr, xi, tb, lanes, reverse):
    pw_re, pw_im, w8_re, w8_im, wb_re, wb_im, carry_re, carry_im, sr, si = tb
    tt = xr.shape[0]
    nb = tt // SUB
    xr, xi = _doubling(xr, xi, pw_re, pw_im, lanes, 0, SUB, reverse)
    nq = sr.shape[0]
    cols = [slice(q * LANES, (q + 1) * LANES) for q in range(nq)]
    for q in range(nq):
        sr[q] = xr[:, cols[q]]
        si[q] = xi[:, cols[q]]
    last = 0 if reverse else SUB - 1
    er = jnp.concatenate([sr[q, pl.ds(last, nb, stride=SUB), :] for q in range(nq)], axis=1)
    ei = jnp.concatenate([si[q, pl.ds(last, nb, stride=SUB), :] for q in range(nq)], axis=1)
    er, ei = _doubling(er, ei, pw_re, pw_im, lanes, 3, nb, reverse)
    car, cai = carry_re[:, lanes], carry_im[:, lanes]
    wbr, wbi = wb_re[:, lanes], wb_im[:, lanes]
    er = er + (wbr * car - wbi * cai)
    ei = ei + (wbr * cai + wbi * car)
    out_block = 0 if reverse else nb - 1
    carry_re[:, lanes] = er[out_block:out_block + 1, :]
    carry_im[:, lanes] = ei[out_block:out_block + 1, :]
    blocks = lax.broadcasted_iota(jnp.int32, (nb, 1), 0)
    first = blocks == (nb - 1 if reverse else 0)
    cr = jnp.where(first, car, pltpu.roll(er, nb - 1 if reverse else 1, 0))
    ci = jnp.where(first, cai, pltpu.roll(ei, nb - 1 if reverse else 1, 0))
    for r in range(SUB):
        wr, wi = w8_re[r:r + 1, lanes], w8_im[r:r + 1, lanes]
        add_r, add_i = wr * cr - wi * ci, wr * ci + wi * cr
        for q in range(nq):
            sr[q, pl.ds(r, nb, stride=SUB), :] += add_r[:, cols[q]]
            si[q, pl.ds(r, nb, stride=SUB), :] += add_i[:, cols[q]]
    hr = jnp.concatenate([sr[q] for q in range(nq)], axis=1)
    hi = jnp.concatenate([si[q] for q in range(nq)], axis=1)
    return hr, hi, car, cai


def _scan_scratch(tt, NS):
    nb = tt // SUB
    return [pltpu.VMEM((8, NS), F32), pltpu.VMEM((8, NS), F32), pltpu.VMEM((SUB, NS), F32), pltpu.VMEM((SUB, NS), F32),
            pltpu.VMEM((nb, NS), F32), pltpu.VMEM((nb, NS), F32), pltpu.VMEM((1, NS), F32), pltpu.VMEM((1, NS), F32),
            pltpu.VMEM((SLAB_ST // LANES, tt, LANES), F32), pltpu.VMEM((SLAB_ST // LANES, tt, LANES), F32)]


def _scan_init(lr, li, tb, reverse):
    pw_re, pw_im, w8_re, w8_im, wb_re, wb_im, carry_re, carry_im, sr, _ = tb
    nb = wb_re.shape[0]
    carry_re[...] = jnp.zeros_like(carry_re)
    carry_im[...] = jnp.zeros_like(carry_im)
    pr, pi = lr, li
    for k in range(3 + nb.bit_length() - 1):
        pw_re[k:k + 1, :] = pr
        pw_im[k:k + 1, :] = pi
        if k == 3:
            l8r, l8i = pr, pi
        pr, pi = pr * pr - pi * pi, 2.0 * pr * pi
    pr, pi = lr, li
    for r in range(SUB):
        row = SUB - 1 - r if reverse else r
        w8_re[row:row + 1, :] = pr
        w8_im[row:row + 1, :] = pi
        pr, pi = pr * lr - pi * li, pr * li + pi * lr
    pr, pi = l8r, l8i
    for b in range(nb):
        row = nb - 1 - b if reverse else b
        wb_re[row:row + 1, :] = pr
        wb_im[row:row + 1, :] = pi
        pr, pi = pr * l8r - pi * l8i, pr * l8i + pi * l8r


def _ssm_tiles(T, Lc):
    tt = min(128, Lc)
    return tt, T // tt, Lc // tt


def _ssm_fwd(name, u, bbd, cbd_re, cbd_im, lam_re, lam_im, coef_re, coef_im, Lc, reverse):
    T, W = u.shape
    nslab = W // SLAB_CH
    NS = nslab * SLAB_ST
    tt, nt, nc = _ssm_tiles(T, Lc)
    if reverse:
        tile = lambda s: jnp.where(s < nc, nc - 1 - s, nt - 1 - (s - nc))
    else:
        tile = lambda s: s

    def body(u_ref, b_ref, cr_ref, ci_ref, lr_ref, li_ref, kr_ref, ki_ref, hr_ref, hi_ref, y_ref, *tb):
        @pl.when(pl.program_id(0) == 0)
        def _():
            _scan_init(lr_ref[...], li_ref[...], tb, reverse)

        for j in range(nslab):
            lanes = slice(j * SLAB_ST, (j + 1) * SLAB_ST)
            bu = jnp.dot(u_ref[:, j * SLAB_CH:(j + 1) * SLAB_CH], b_ref[j], preferred_element_type=F32)
            br, bi = bu[:, :SLAB_ST], bu[:, SLAB_ST:]
            kr, ki = kr_ref[:, lanes], ki_ref[:, lanes]
            hr, hi, _, _ = _scan_tile(kr * br - ki * bi, kr * bi + ki * br, tb, lanes, reverse)
            hrb, hib = hr.astype(BF16), hi.astype(BF16)
            hr_ref[:, lanes] = hrb
            hi_ref[:, lanes] = hib
            y_ref[:, j * SLAB_CH:(j + 1) * SLAB_CH] = (
                jnp.dot(hrb, cr_ref[j], preferred_element_type=F32)
                - jnp.dot(hib, ci_ref[j], preferred_element_type=F32))

    whole3 = lambda arr: pl.BlockSpec(arr.shape, lambda s: (0, 0, 0))
    vec = pl.BlockSpec((1, NS), lambda s: (0, 0))
    return pl.pallas_call(
        body, name=name, grid=(nt,),
        in_specs=[pl.BlockSpec((tt, W), lambda s: (tile(s), 0)), whole3(bbd), whole3(cbd_re), whole3(cbd_im),
                  vec, vec, vec, vec],
        out_specs=[pl.BlockSpec((tt, NS), lambda s: (tile(s), 0)), pl.BlockSpec((tt, NS), lambda s: (tile(s), 0)),
                   pl.BlockSpec((tt, W), lambda s: (tile(s), 0))],
        out_shape=[jax.ShapeDtypeStruct((T, NS), BF16), jax.ShapeDtypeStruct((T, NS), BF16),
                   jax.ShapeDtypeStruct((T, W), F32)],
        scratch_shapes=_scan_scratch(tt, NS),
        compiler_params=_cparams())(u, bbd, cbd_re, cbd_im, lam_re, lam_im, coef_re, coef_im)


def _ssm_bwd(name, dy, h_re, h_im, u, bbd, bbdt_re, bbdt_im, cbdt_re, cbdt_im, lam_re, lam_im,
             coef_re, coef_im, Lc, reverse):
    T, W = u.shape
    nslab = W // SLAB_CH
    NS = nslab * SLAB_ST
    tt, nt, nc = _ssm_tiles(T, Lc)
    adj_reverse = not reverse
    if reverse:
        tile = lambda s: jnp.where(s < nt - nc, nc + s, s - (nt - nc))
    else:
        tile = lambda s: nt - 1 - s

    def body(dy_ref, hr_ref, hi_ref, u_ref, b_ref, btr_ref, bti_ref, ctr_ref, cti_ref, lr_ref, li_ref,
             kr_ref, ki_ref, du_ref, dlr_ref, dli_ref, dkr_ref, dki_ref, dbf_ref, dcrf_ref, dcif_ref,
             db_ref, dcr_ref, dci_ref, *tb):
        @pl.when(pl.program_id(0) == 0)
        def _():
            _scan_init(lr_ref[...], -li_ref[...], tb, adj_reverse)
            for ref in (dlr_ref, dli_ref, dkr_ref, dki_ref, db_ref, dcr_ref, dci_ref):
                ref[...] = jnp.zeros_like(ref)

        rows = lax.broadcasted_iota(jnp.int32, (tt, 1), 0)
        far_row = tt - 1 if adj_reverse else 0
        tn_dims = (((0,), (0,)), ((), ()))
        for j in range(nslab):
            lanes = slice(j * SLAB_ST, (j + 1) * SLAB_ST)
            chans = slice(j * SLAB_CH, (j + 1) * SLAB_CH)
            dys, us = dy_ref[:, chans], u_ref[:, chans]
            er = jnp.dot(dys, ctr_ref[j], preferred_element_type=F32)
            ei = -jnp.dot(dys, cti_ref[j], preferred_element_type=F32)
            ar, ai, car, cai = _scan_tile(er, ei, tb, lanes, adj_reverse)
            shift = tt - 1 if adj_reverse else 1
            nr = jnp.where(rows == far_row, car, pltpu.roll(ar, shift, 0))
            ni = jnp.where(rows == far_row, cai, pltpu.roll(ai, shift, 0))
            hrb, hib = hr_ref[:, lanes], hi_ref[:, lanes]
            hr, hi = hrb.astype(F32), hib.astype(F32)
            dlr_ref[:, lanes] += jnp.sum(nr * hr + ni * hi, axis=0, keepdims=True)
            dli_ref[:, lanes] += jnp.sum(ni * hr - nr * hi, axis=0, keepdims=True)
            bu = jnp.dot(us, b_ref[j], preferred_element_type=F32)
            br, bi = bu[:, :SLAB_ST], bu[:, SLAB_ST:]
            dkr_ref[:, lanes] += jnp.sum(ar * br + ai * bi, axis=0, keepdims=True)
            dki_ref[:, lanes] += jnp.sum(ai * br - ar * bi, axis=0, keepdims=True)
            kr, ki = kr_ref[:, lanes], ki_ref[:, lanes]
            dbr = (ar * kr + ai * ki).astype(BF16)
            dbi = (ai * kr - ar * ki).astype(BF16)
            du_ref[:, chans] = (jnp.dot(dbr, btr_ref[j], preferred_element_type=F32)
                                + jnp.dot(dbi, bti_ref[j], preferred_element_type=F32))
            db_ref[j, :, :SLAB_ST] += lax.dot_general(us, dbr, tn_dims, preferred_element_type=F32)
            db_ref[j, :, SLAB_ST:] += lax.dot_general(us, dbi, tn_dims, preferred_element_type=F32)
            dcr_ref[j] += lax.dot_general(hrb, dys, tn_dims, preferred_element_type=F32)
            dci_ref[j] -= lax.dot_general(hib, dys, tn_dims, preferred_element_type=F32)

        @pl.when(pl.program_id(0) == nt - 1)
        def _():
            def iota(shape, axis):
                return lax.broadcasted_iota(jnp.int32, shape, axis)

            sg, ss = SSM_GROUP.bit_length() - 1, SSM_STATE.bit_length() - 1
            b_mask = (iota((SLAB_CH, SLAB_ST), 0) >> sg) == (iota((SLAB_CH, SLAB_ST), 1) >> ss)
            c_mask = (iota((SLAB_ST, SLAB_CH), 0) >> ss) == (iota((SLAB_ST, SLAB_CH), 1) >> sg)
            fold = jnp.where((iota((SLAB_ST, SSM_STATE), 0) & (SSM_STATE - 1)) == iota((SLAB_ST, SSM_STATE), 1),
                             1.0, 0.0).astype(BF16)
            fold_t = jnp.where((iota((SSM_STATE, SLAB_ST), 1) & (SSM_STATE - 1)) == iota((SSM_STATE, SLAB_ST), 0),
                               1.0, 0.0).astype(BF16)

            def exact_dot(a, b, a_is_value):
                terms = _split3(a if a_is_value else b)
                acc = None
                for t in terms:
                    part = jnp.dot(t, b, preferred_element_type=F32) if a_is_value else jnp.dot(a, t, preferred_element_type=F32)
                    acc = part if acc is None else acc + part
                return acc

            for j in range(nslab):
                dbj = db_ref[j]
                dbf_ref[j, :, :SSM_STATE] = exact_dot(jnp.where(b_mask, dbj[:, :SLAB_ST], 0.0), fold, True)
                dbf_ref[j, :, SSM_STATE:] = exact_dot(jnp.where(b_mask, dbj[:, SLAB_ST:], 0.0), fold, True)
                dcrf_ref[j] = exact_dot(fold_t, jnp.where(c_mask, dcr_ref[j], 0.0), False)
                dcif_ref[j] = exact_dot(fold_t, jnp.where(c_mask, dci_ref[j], 0.0), False)

    whole3 = lambda arr: pl.BlockSpec(arr.shape, lambda s: (0, 0, 0))
    vec = pl.BlockSpec((1, NS), lambda s: (0, 0))
    row_w = pl.BlockSpec((tt, W), lambda s: (tile(s), 0))
    row_s = pl.BlockSpec((tt, NS), lambda s: (tile(s), 0))
    dbf = jax.ShapeDtypeStruct((nslab, SLAB_CH, 2 * SSM_STATE), F32)
    dcf = jax.ShapeDtypeStruct((nslab, SSM_STATE, SLAB_CH), F32)
    return pl.pallas_call(
        body, name=name, grid=(nt,),
        in_specs=[row_w, row_s, row_s, row_w, whole3(bbd), whole3(bbdt_re), whole3(bbdt_im), whole3(cbdt_re),
                  whole3(cbdt_im), vec, vec, vec, vec],
        out_specs=[row_w, vec, vec, vec, vec, whole3(dbf), whole3(dcf), whole3(dcf)],
        out_shape=[jax.ShapeDtypeStruct((T, W), F32)] + [jax.ShapeDtypeStruct((1, NS), F32)] * 4 + [dbf, dcf, dcf],
        scratch_shapes=[pltpu.VMEM(bbd.shape, F32), pltpu.VMEM(bbdt_re.shape, F32), pltpu.VMEM(bbdt_re.shape, F32)]
        + _scan_scratch(tt, NS),
        compiler_params=_cparams())(dy, h_re, h_im, u, bbd, bbdt_re, bbdt_im, cbdt_re, cbdt_im,
                                    lam_re, lam_im, coef_re, coef_im)


def _zoh_math(a_re, a_im, log_dt):
    dt = jnp.exp(log_dt)
    mag = jnp.exp(a_re * dt)
    lb_re = mag * jnp.cos(a_im * dt)
    lb_im = mag * jnp.sin(a_im * dt)
    den = a_re * a_re + a_im * a_im
    coef_re = ((lb_re - 1.0) * a_re + lb_im * a_im) / den
    coef_im = (lb_im * a_re - (lb_re - 1.0) * a_im) / den
    return lb_re, lb_im, coef_re, coef_im


def _zoh_fwd(a_re, a_im, log_dt):
    def body(ar, ai, ld, o0, o1, o2, o3):
        for ref, val in zip((o0, o1, o2, o3), _zoh_math(ar[...], ai[...], ld[...])):
            ref[...] = val

    return pl.pallas_call(body, name="zoh_fwd", out_shape=[jax.ShapeDtypeStruct(a_re.shape, F32)] * 4,
                          compiler_params=_cparams())(a_re, a_im, log_dt)


def _zoh_bwd(a_re, a_im, log_dt, cots):
    def body(ar, ai, ld, c0, c1, c2, c3, o0, o1, o2):
        _, vjp = jax.vjp(_zoh_math, ar[...], ai[...], ld[...])
        for ref, val in zip((o0, o1, o2), vjp((c0[...], c1[...], c2[...], c3[...]))):
            ref[...] = val

    return pl.pallas_call(
        body, name="zoh_bwd",
        out_shape=[jax.ShapeDtypeStruct(a_re.shape, F32), jax.ShapeDtypeStruct(a_re.shape, F32),
                   jax.ShapeDtypeStruct(log_dt.shape, F32)],
        compiler_params=_cparams())(a_re, a_im, log_dt, *cots)


def _outer_sum(acts, cots):
    D, N = acts.shape[1], cots.shape[1]
    tm, tn = _div(D, 512), _div(N, 1152)
    dims = (((0,), (0,)), ((), ()))

    def body(a_ref, b_ref, o_ref):
        a = a_ref[...]
        aa = _split3(a * _sigmoid(a))
        bb = _split3(b_ref[...])
        acc = None
        for ia in range(3):
            for ib in range(3 - ia):
                t = lax.dot_general(aa[ia], bb[ib], dims, preferred_element_type=F32)
                acc = t if acc is None else acc + t
        o_ref[...] = acc

    return pl.pallas_call(
        body, name="mod_dw", grid=(D // tm, N // tn),
        in_specs=[pl.BlockSpec((16, tm), lambda i, j: (0, i)), pl.BlockSpec((16, tn), lambda i, j: (0, j))],
        out_specs=pl.BlockSpec((tm, tn), lambda i, j: (i, j)),
        out_shape=jax.ShapeDtypeStruct((D, N), F32), compiler_params=_cparams())(acts, cots)


def _adamw_math(w, g, m, v):
    m = ADAM_B1 * m + (1.0 - ADAM_B1) * g
    v = ADAM_B2 * v + (1.0 - ADAM_B2) * (g * g)
    m_hat = m / (1.0 - ADAM_B1 ** ADAM_STEP)
    v_hat = v / (1.0 - ADAM_B2 ** ADAM_STEP)
    delta = -ADAM_LR * (m_hat / (jnp.sqrt(v_hat) + ADAM_EPS) + ADAM_WD * w)
    return delta, m, v


def _adamw(name, w, m, v, gparts):
    R, C = w.shape[-2:]
    kind = 'row1' if w.ndim == 3 else 'row'
    tr = _div(R, max(8, 262144 // C), mult=8)

    def fn(i, wv, mv, vv, *gs):
        g = gs[0]
        for extra in gs[1:]:
            g = g + extra
        return (g,) + _adamw_math(wv, g, mv, vv)

    return _rowk(name, fn, R, tr, [(w, kind), (m, kind), (v, kind)] + [(g, 'row') for g in gparts],
                 [(w.shape, F32, kind)] * 4)


def _pack(pieces, rows_mult=8):
    flat = jnp.concatenate([p.reshape(-1).astype(F32) for p in pieces])
    unit = rows_mult * PACK_W
    total = -(-flat.shape[0] // unit) * unit
    return jnp.pad(flat, (0, total - flat.shape[0])).reshape(total // PACK_W, PACK_W)


def _unpack(buf, shapes):
    flat = buf.reshape(-1)
    out, off = [], 0
    for s in shapes:
        n = math.prod(s)
        out.append(flat[off:off + n].reshape(s))
        off += n
    return out


def _bd_expand(t):
    S, g, a, b = t.shape
    eye = jnp.eye(g, dtype=t.dtype)
    return (t[:, :, :, None, :] * eye[None, :, None, :, None]).reshape(S, g * a, g * b)


def _rope_tables(L, Lc):
    rows = L // GRID_W
    row_ids = jnp.broadcast_to(jnp.arange(rows)[:, None], (rows, GRID_W)).reshape(-1).astype(F32)
    col_ids = jnp.broadcast_to(jnp.arange(GRID_W)[None, :], (rows, GRID_W)).reshape(-1).astype(F32)
    quarter = HEAD_DIM // 4
    inv_freq = ROPE_THETA ** (-jnp.arange(quarter, dtype=F32) / quarter)
    ang_r = row_ids[:, None] * inv_freq
    ang_c = col_ids[:, None] * inv_freq
    cos = jnp.concatenate([jnp.cos(ang_r), jnp.cos(ang_r), jnp.cos(ang_c), jnp.cos(ang_c)], axis=1)
    sin = jnp.concatenate([-jnp.sin(ang_r), jnp.sin(ang_r), -jnp.sin(ang_c), jnp.sin(ang_c)], axis=1)
    cos = jnp.concatenate([jnp.ones((Lc, HEAD_DIM), F32), cos], axis=0)
    sin = jnp.concatenate([jnp.zeros((Lc, HEAD_DIM), F32), sin], axis=0)
    return cos, sin


def _rot(v):
    lane = lax.broadcasted_iota(jnp.int32, (1, HEAD_DIM), 1)
    first = (lane % (HEAD_DIM // 2)) < (HEAD_DIM // 4)
    return jnp.where(first, pltpu.roll(v, HEAD_DIM - HEAD_DIM // 4, 1), pltpu.roll(v, HEAD_DIM // 4, 1))


def _head_norm(xh, g):
    return xh * lax.rsqrt(jnp.mean(xh * xh, axis=-1, keepdims=True) + NORM_EPS) * g


def _norm_mod(xv, g, sh, sc):
    r = lax.rsqrt(jnp.mean(xv * xv, axis=-1, keepdims=True) + NORM_EPS)
    return (xv * r) * g * (1.0 + sc) + sh


def kernel(x, c, ctx, c_ctx, w_mod, b_mod, norm_g, w_ffn1_gate, w_ffn1_up, w_ffn1_down, w_in, q_norm_g, k_norm_g, ssm_a_re, ssm_a_im, ssm_log_dt, ssm_b_re, ssm_b_im, ssm_c_re, ssm_c_im, ssm_d, w_glu, b_glu, w_br_attn, w_br_ssm, w_out, w_ffn2_gate, w_ffn2_up, w_ffn2_down, loss_target, m_c_ctx, m_w_mod, m_b_mod, m_norm_g, m_w_ffn1_gate, m_w_ffn1_up, m_w_ffn1_down, m_w_in, m_q_norm_g, m_k_norm_g, m_ssm_a_re, m_ssm_a_im, m_ssm_log_dt, m_ssm_b_re, m_ssm_b_im, m_ssm_c_re, m_ssm_c_im, m_ssm_d, m_w_glu, m_b_glu, m_w_br_attn, m_w_br_ssm, m_w_out, m_w_ffn2_gate, m_w_ffn2_up, m_w_ffn2_down, v_c_ctx, v_w_mod, v_b_mod, v_norm_g, v_w_ffn1_gate, v_w_ffn1_up, v_w_ffn1_down, v_w_in, v_q_norm_g, v_k_norm_g, v_ssm_a_re, v_ssm_a_im, v_ssm_log_dt, v_ssm_b_re, v_ssm_b_im, v_ssm_c_re, v_ssm_c_im, v_ssm_d, v_w_glu, v_b_glu, v_w_br_attn, v_w_br_ssm, v_w_out, v_w_ffn2_gate, v_w_ffn2_up, v_w_ffn2_down):
    A = dict(locals())
    xi, yi, ci = _mesh_pos()
    chip = 2 * xi + yi
    me = 4 * xi + 2 * yi + ci
    L, D = x.shape[1], x.shape[2]
    Lc = ctx.shape[1]
    T = L + Lc
    F4 = w_ffn1_gate.shape[2]
    F = N_CHIPS * F4
    W, KV, Dq = D // 2, D // 4, D // 4
    G = W // SSM_GROUP
    P, E = SSM_STATE, SSM_GROUP
    NS = G * P
    nslab = W // SLAB_CH
    tr = min(256, Lc)
    ncr = Lc // tr
    assert L % tr == 0 and Lc % tr == 0 and W % SLAB_CH == 0 and D % (4 * LANES) == 0

    def sel(i, v):
        return v if v.shape[0] == 1 else jnp.where(i < ncr, v[0:1], v[1:2])

    def put(i, v, nrow):
        if nrow == 1:
            return v
        which = (i >= ncr).astype(jnp.int32)
        r2 = lax.broadcasted_iota(jnp.int32, (nrow, 1), 0)
        return jnp.where(r2 == which, jnp.broadcast_to(v, (nrow, v.shape[1])), 0.0)

    ident = lambda accs, rows, vecs, ri: [accs[0]]

    NM = w_mod.shape[2]
    first = jnp.zeros((8, D), F32).at[0].set(c[0]).at[1:4, :Dq].set(norm_g[0])
    g0 = _allgather_small("gather_c", first).reshape(N_CHIPS, 2, 8, D)
    c_all = g0[:, :, 0].reshape(N_DEV, D)
    ng = jnp.transpose(g0[:, 0, 1:4, :Dq], (1, 0, 2)).reshape(3, D)
    acts = jnp.concatenate([c_all, c_ctx[None], jnp.zeros((7, D), F32)], axis=0)
    wm = w_mod[0]
    b_shard = lax.dynamic_slice(b_mod[0], (chip * NM,), (NM,))[None]
    silu_bf = lambda a: (a * _sigmoid(a)).astype(BF16)
    to_bf = lambda b: b.astype(BF16)
    mod_part = _mm("mod_fwd", [(acts, wm, D)], 16, NM, tm=16, tn=_div(NM, 1152),
                   epi=lambda accs, rows, vecs, ri: [accs[0] + vecs[0]], outs=[(F32, False)],
                   vecs=[b_shard], a_pro=silu_bf, b_pro=to_bf)[0]
    mg = _allgather_small("gather_mod", mod_part).reshape(N_CHIPS, 2, 16, NM)[:, 0]
    mod_all = jnp.transpose(mg, (1, 0, 2)).reshape(16, N_CHIPS * NM)
    mod_x = lax.dynamic_slice(mod_all, (me, 0), (1, 9 * D))
    mod_c = jnp.where(jnp.arange(9 * D)[None] < 5 * D, mod_all[8:9], 0.0)
    modv = jnp.concatenate([mod_c, mod_x], axis=0)
    mv = lambda k: modv[:, k * D:(k + 1) * D]
    sh1, sc1, g1, sh2, sc2 = mv(0), mv(1), mv(2), mv(3), mv(4)
    g2, sh3, sc3, g3 = mv(5)[1:2], mv(6)[1:2], mv(7)[1:2], mv(8)[1:2]

    big = ['w_ffn1_gate', 'w_ffn1_up', 'w_ffn1_down', 'w_ffn2_gate', 'w_ffn2_up', 'w_ffn2_down',
           'w_in', 'w_glu', 'w_br_attn', 'w_br_ssm', 'w_out']
    row_sharded = {'w_ffn1_down', 'w_ffn2_down', 'w_glu', 'w_br_attn', 'w_out'}
    groups = [big[0:2], big[2:3], big[6:7], big[7:11], big[3:6]]
    chip_index = jnp.reshape(chip, (1,)).astype(jnp.int32)
    slots = {n: _cast_slot("cast_" + n, A[n], chip_index) for n in big}
    tok, gather_finish = modv, []
    for gi, names in enumerate(groups):
        tok, fin = _gather_split("gather_w%d" % gi, [slots[n] for n in names], tok)
        gather_finish.append(fin)
    ng = ng + tok[0:1, 0:1]
    Wt = {}

    def register(names, full):
        for n, gw in zip(names, full):
            Wt[n] = gw.reshape(N_CHIPS * gw.shape[1], gw.shape[2]) if n in row_sharded else gw

    def weights_ready(gi, after_work):
        _, lands = gather_finish[gi](after_work)
        register(groups[gi], _gather_finish("gather_w%d_pass" % gi, lands))

    def weights_pass(gi, after_work):
        _, lands = gather_finish[gi](after_work)
        tok_, fin_ = _pass_split("gather_w%d_pass" % gi, lands, after_work)
        return tok_, lambda later: register(groups[gi], fin_(later)[1])

    a_re2, a_im2 = ssm_a_re[0].reshape(2 * G, P), ssm_a_im[0].reshape(2 * G, P)
    ldt2 = ssm_log_dt[0].reshape(2 * G, 1)
    zoh = _zoh_fwd(a_re2, a_im2, ldt2)
    lam_re, lam_im, coef_re, coef_im = [[z[d * G:(d + 1) * G].reshape(1, NS) for d in range(2)] for z in zoh]
    bd_b = lambda b: _bd_expand(jnp.transpose(b, (0, 2, 1)).reshape(nslab, SLAB_GROUPS, E, P))
    bd_c = lambda cc: _bd_expand(jnp.transpose(cc, (0, 2, 1)).reshape(nslab, SLAB_GROUPS, P, E))
    bbd, bbdt_re, bbdt_im, cbd_re, cbd_im, cbdt_re, cbdt_im = [], [], [], [], [], [], []
    for d in range(2):
        br_, bi_ = bd_b(ssm_b_re[0, d]).astype(BF16), bd_b(ssm_b_im[0, d]).astype(BF16)
        cr_, ci_ = bd_c(ssm_c_re[0, d]).astype(BF16), bd_c(ssm_c_im[0, d]).astype(BF16)
        bbd.append(jnp.concatenate([br_, bi_], axis=2))
        bbdt_re.append(jnp.transpose(br_, (0, 2, 1)))
        bbdt_im.append(jnp.transpose(bi_, (0, 2, 1)))
        cbd_re.append(cr_)
        cbd_im.append(ci_)
        cbdt_re.append(jnp.transpose(cr_, (0, 2, 1)))
        cbdt_im.append(jnp.transpose(ci_, (0, 2, 1)))
    cos_t, sin_t = _rope_tables(L, Lc)
    qg, kg = q_norm_g, k_norm_g
    small = ['c_ctx', 'b_mod', 'norm_g', 'q_norm_g', 'k_norm_g', 'ssm_a_re', 'ssm_a_im', 'ssm_log_dt', 'ssm_b_re',
             'ssm_b_im', 'ssm_c_re', 'ssm_c_im', 'ssm_d', 'b_glu']
    packs_wmv = [_pack([A[pre + n] for n in small]) for pre in ('', 'm_', 'v_')]
    prepared = packs_wmv + [cos_t, sin_t, coef_im[0], coef_im[1]] + [
        t[d][0] for t in (bbd, bbdt_re, bbdt_im, cbd_re, cbd_im, cbdt_re, cbdt_im) for d in range(2)]
    weights_ready(0, tok + sum(t[0:1, 0:1].astype(F32) for t in prepared))

    def norm_mod(name, xv, g, sh, sc):
        rows = xv.shape[0]
        return _rowk(name, lambda i, xt, gt, sht, sct: [_norm_mod(xt, gt, sel(i, sht), sel(i, sct))],
                     rows, tr, [(xv, 'row'), (g, 'vec'), (sh, 'vec'), (sc, 'vec')], [((rows, D), BF16, 'row')])[0]

    def swiglu_epi(accs, rows, vecs, ri):
        a_, b_ = accs
        return [a_, b_, a_ * _sigmoid(a_) * b_]

    def res_epi(coef):
        def epi(accs, rows, vecs, ri):
            gate = vecs[0]
            if gate.shape[0] == 2:
                gate = jnp.where(ri < Lc, gate[0:1], gate[1:2])
            return [accs[0], rows[0] + (coef * gate) * accs[0]]
        return epi

    def ffn_fwd(tag, h, xres, gate, down_ready=None):
        rows = h.shape[0]
        a_, b_, s_ = _mm(tag + "_up", [(h, Wt['w_' + tag + '_gate'], D), (h, Wt['w_' + tag + '_up'], D)], rows, F,
                         tm=_div(rows, 256), tn=F4, epi=swiglu_epi, outs=[(F32, False), (F32, False), (BF16, False)])
        if down_ready is not None:
            down_ready(s_)
        f_, xo = _mm(tag + "_down", [(s_, Wt['w_' + tag + '_down'], F)], rows, D, tm=_div(rows, 384),
                     tn=_div(D, 512), epi=res_epi(0.5), outs=[(F32, False), (F32, False)],
                     rows=[(xres, 0, 0)], vecs=[gate])
        return a_, b_, s_, f_, xo

    xc = jnp.concatenate([ctx[0], x[0]], axis=0)
    h1 = norm_mod("norm1", xc, ng[0:1], sh1, sc1)
    a1, b1, s1, f1, x1 = ffn_fwd("ffn1", h1, xc, g1, down_ready=lambda s_: weights_ready(1, s_))
    weights_ready(2, x1)
    h2 = norm_mod("norm2", x1, ng[1:2], sh2, sc2)
    proj = _mm("in_proj", [(h2, Wt['w_in'], D)], T, 4 * D, tm=_div(T, 768), tn=_div(D, 1024), epi=ident,
               outs=[(F32, False)])[0]
    nh, nkvh = D // HEAD_DIM, KV // HEAD_DIM

    def prep_fn(i, kt, vt, ut, qt, qgt, kgt, ct, st):
        qs = [_head_norm(qt[:, h * HEAD_DIM:(h + 1) * HEAD_DIM], qgt) for h in range(nh)]
        ks = [_head_norm(kt[:, h * HEAD_DIM:(h + 1) * HEAD_DIM], kgt) for h in range(nkvh)]
        qs = [v * ct + _rot(v) * st for v in qs]
        ks = [v * ct + _rot(v) * st for v in ks]
        return [jnp.concatenate(qs, axis=1), jnp.concatenate(ks, axis=1), vt, ut]

    qr, kr, vb, ub = _rowk(
        "qk_prep", prep_fn, T, tr,
        [(proj, ('col', KV, 0)), (proj, ('col', KV, 1)), (proj, ('col', W, 1)), (proj, ('col', D, 1)),
         (qg, 'vec'), (kg, 'vec'), (cos_t, 'row'), (sin_t, 'row')],
        [((T, D), BF16, 'row'), ((T, KV), BF16, 'row'), ((T, KV), BF16, 'row'), ((T, W), BF16, 'row')])
    _, mixer_weights = weights_pass(3, qr)
    attn = _attn_fwd(qr, kr, vb, L, Lc, D)
    hs_re, hs_im, ys = [], [], []
    lam_in = lam_re[0]
    for d in range(2):
        hr_, hi_, y_ = _ssm_fwd("ssm_fwd%d" % d, ub, bbd[d], cbd_re[d], cbd_im[d], lam_in, lam_im[d],
                                coef_re[d], coef_im[d], Lc, reverse=bool(d))
        hs_re.append(hr_)
        hs_im.append(hi_)
        ys.append(y_)
        if d == 0:
            tok_p4, ffn2_weights = weights_pass(4, y_)
            lam_in = lam_re[1] + tok_p4[0:1, 0:1]
    mixer_weights(ys[1])

    def ssm_out_fn(i, y0, y1, ut, dt):
        pre = dt * ut + y0 + y1
        yg_ = _gelu(pre)
        return [pre, yg_, yg_]

    ssm_pre, yg, ygb = _rowk(
        "ssm_out", ssm_out_fn, L, tr,
        [(ys[0], 'orow'), (ys[1], 'orow'), (proj, ('ocol', W, 1)), (ssm_d, 'vec')],
        [((L, W), F32, 'row'), ((L, W), F32, 'row'), ((L, W), BF16, 'row')], nc=ncr)

    def glu_epi(accs, rows, vecs, ri):
        z_ = accs[0] + vecs[0]
        return [z_, rows[0] * _sigmoid(z_)]

    zglu, y2 = _mm("glu", [(ygb, Wt['w_glu'], W)], L, W, tm=_div(L, 512), tn=_div(W, 512), epi=glu_epi,
                   outs=[(F32, False), (BF16, False)], rows=[(yg, 0, 0)], vecs=[b_glu])
    tnm = _div(Dq, 512)

    def merge_epi(accs, rows, vecs, ri):
        ga, gs = _sigmoid(rows[0]), _sigmoid(rows[1])
        return [accs[0], accs[1], ga * accs[0] + gs * accs[1]]

    ba, bs, merged = _mm("merge", [(attn, Wt['w_br_attn'], D), (y2, Wt['w_br_ssm'], W)], L, D, tm=tr, tn=tnm,
                         epi=merge_epi, outs=[(F32, False), (F32, False), (BF16, False)],
                         rows=[(proj, ncr, 2 * D // tnm), (proj, ncr, 3 * D // tnm)])
    mix, x2 = _mm("out_proj", [(merged, Wt['w_out'], D)], L, D, tm=tr, tn=_div(D, 1024), epi=res_epi(1.0),
                  outs=[(F32, False), (F32, False)], rows=[(x1, ncr, 0)], vecs=[g2])
    ffn2_weights(x2)
    h3 = norm_mod("norm3", x2, ng[2:3], sh3, sc3)
    a3, b3, s3, f3, x3 = ffn_fwd("ffn2", h3, x2, g3)

    def loss_fn(i, yt, tt_):
        diff = yt - tt_
        return [diff * (1.0 / D), jnp.sum(diff * diff, axis=0, keepdims=True)]

    dy, sq = _rowk("loss", loss_fn, L, tr, [(x3, 'row'), (loss_target[0], 'row')],
                   [((L, D), F32, 'row'), ((1, D), F32, 'acc')])
    loss = lax.psum(0.5 * jnp.sum(sq) / D, ("x", "y", "c"))

    def res_bwd(name, dxo, f_, gate, coef):
        rows, nrow = dxo.shape[0], gate.shape[0]

        def fn(i, dt, ft, gt):
            return [(coef * sel(i, gt)) * dt, put(i, jnp.sum(dt * ft, axis=0, keepdims=True) * coef, nrow)]

        return _rowk(name, fn, rows, tr, [(dxo, 'row'), (f_, 'row'), (gate, 'vec')],
                     [((rows, D), BF16, 'row'), ((nrow, D), F32, 'acc')])

    def swiglu_bwd_epi(accs, rows, vecs, ri):
        ds_, a_, b_ = accs[0], rows[0], rows[1]
        sg = _sigmoid(a_)
        return [ds_ * b_ * (sg * (1.0 + a_ * (1.0 - sg))), ds_ * (a_ * sg)]

    def norm_mod_bwd(name, xv, g, sh, sc, dh, dres, dres_kind):
        rows, nrow = xv.shape[0], sh.shape[0]

        def fn(i, xt, gt, sht, sct, dht, rest):
            _, vjp = jax.vjp(_norm_mod, xt, gt, sel(i, sht), sel(i, sct))
            dx_, dg_, dsh_, dsc_ = vjp(dht)
            dx_ = dx_ + (jnp.where(i >= ncr, rest, 0.0) if dres_kind == 'xrow' else rest)
            return [dx_, dg_, put(i, dsh_, nrow), put(i, dsc_, nrow)]

        return _rowk(name, fn, rows, tr,
                     [(xv, 'row'), (g, 'vec'), (sh, 'vec'), (sc, 'vec'), (dh, 'row'), (dres, dres_kind)],
                     [((rows, D), F32, 'row'), ((1, D), F32, 'acc'), ((nrow, D), F32, 'acc'), ((nrow, D), F32, 'acc')],
                     nc=ncr)

    def ffn_bwd(tag, dxo, h, a_, b_, s_, f_, gate, wg, wu, wd, on_dwd=None):
        rows = dxo.shape[0]
        df, dgate = res_bwd(tag + "_dres", dxo, f_, gate, 0.5)
        dwd = _mm(tag + "_dwd", [(s_, df, rows)], F, D, tm=_div(F, 512), tn=_div(D, 1024), ta=True, epi=ident,
                  outs=[(BF16, False)])[0].reshape(N_CHIPS, F4, D)
        if on_dwd is not None:
            on_dwd(dwd)
        da, db = _mm(tag + "_dact", [(df, wd, D)], rows, F, tm=_div(rows, 384), tn=F4, tb=True, epi=swiglu_bwd_epi,
                     outs=[(BF16, False), (BF16, False)], rows=[(a_, 0, 0), (b_, 0, 0)])
        dwg = _mm(tag + "_dwg", [(h, da, rows)], D, F, tm=_div(D, 512), tn=F4, ta=True, epi=ident,
                  outs=[(BF16, True)])[0]
        dwu = _mm(tag + "_dwu", [(h, db, rows)], D, F, tm=_div(D, 512), tn=F4, ta=True, epi=ident,
                  outs=[(BF16, True)])[0]
        dh = _mm(tag + "_dh", [(da, wg, F), (db, wu, F)], rows, D, tm=_div(rows, 768), tn=_div(D, 1024), nk=N_CHIPS,
                 tb=True, epi=ident, outs=[(F32, False)], summed=True)[0]
        return dh, dgate, dwg, dwu, dwd

    dh3, dg3, dwg2, dwu2, dwd2 = ffn_bwd("ffn2", dy, h3, a3, b3, s3, f3, g3, Wt['w_ffn2_gate'], Wt['w_ffn2_up'],
                                         Wt['w_ffn2_down'])
    tok_r1, scatter_fin1 = _scatter_split("scatter_ffn2", [dwg2, dwu2, dwd2], dg3)
    dx2, dng3, dsh3, dsc3 = norm_mod_bwd("norm3_bwd", x2, ng[2:3], sh3, sc3, dh3, dy, 'row')
    dmix, dg2 = res_bwd("mix_dres", dx2, mix, g2 + tok_r1[0:1, 0:1], 1.0)

    def dmerge_epi(accs, rows, vecs, ri):
        dm_, ba_, bs_ = accs[0], rows[0], rows[1]
        ga, gs = _sigmoid(rows[2]), _sigmoid(rows[3])
        return [dm_ * ga, dm_ * gs, dm_ * ba_ * ga * (1.0 - ga), dm_ * bs_ * gs * (1.0 - gs)]

    tnd = _div(D, 1024)
    dba, dbs, dga, dgs = _mm("dmerge", [(dmix, Wt['w_out'], D)], L, D, tm=tr, tn=tnd, tb=True, epi=dmerge_epi,
                             outs=[(BF16, False)] * 4,
                             rows=[(ba, 0, 0), (bs, 0, 0), (proj, ncr, 2 * D // tnd), (proj, ncr, 3 * D // tnd)])
    dwout = _mm("dw_out", [(merged, dmix, L)], D, D, tm=_div(D, 512), tn=_div(D, 1024), ta=True, epi=ident,
                outs=[(BF16, False)])[0].reshape(N_CHIPS, Dq, D)
    dattn = _mm("dattn", [(dba, Wt['w_br_attn'], D)], L, D, tm=_div(L, 512), tn=_div(D, 1024), tb=True, epi=ident,
                outs=[(BF16, False)])[0]
    dwba = _mm("dw_br_attn", [(attn, dba, L)], D, D, tm=_div(D, 512), tn=_div(D, 1024), ta=True, epi=ident,
               outs=[(BF16, False)])[0].reshape(N_CHIPS, Dq, D)
    dy2 = _mm("dy2", [(dbs, Wt['w_br_ssm'], D)], L, W, tm=_div(L, 512), tn=_div(W, 1024), nk=N_CHIPS, tb=True,
              epi=ident, outs=[(F32, False)])[0]
    dwbs = _mm("dw_br_ssm", [(y2, dbs, L)], W, D, tm=_div(W, 512), tn=_div(Dq, 512), ta=True, epi=ident,
               outs=[(BF16, True)])[0]

    def glu_bwd_fn(i, d2, ygt, zt):
        sz = _sigmoid(zt)
        dz_ = d2 * ygt * sz * (1.0 - sz)
        return [dz_, d2 * sz, jnp.sum(dz_, axis=0, keepdims=True)]

    dz, dyd, dbglu = _rowk("glu_bwd", glu_bwd_fn, L, tr, [(dy2, 'row'), (yg, 'row'), (zglu, 'row')],
                           [((L, W), BF16, 'row'), ((L, W), F32, 'row'), ((1, W), F32, 'acc')])

    def dssm_epi(accs, rows, vecs, ri):
        _, vjp = jax.vjp(_gelu, rows[1])
        ds_ = vjp(accs[0] + rows[0])[0]
        return [ds_, ds_]

    dssm, dssm_b = _mm("dssm", [(dz, Wt['w_glu'], W)], L, W, tm=_div(L, 512), tn=_div(W, 512), tb=True, epi=dssm_epi,
                       outs=[(F32, False), (BF16, False)], rows=[(dyd, 0, 0), (ssm_pre, 0, 0)])
    dwglu = _mm("dw_glu", [(ygb, dz, L)], W, W, tm=_div(W, 512), tn=_div(W, 1024), ta=True, epi=ident,
                outs=[(BF16, False)])[0].reshape(N_CHIPS, W // N_CHIPS, W)
    tok_r2a, scatter_fin2a = _scatter_split("scatter_mix", [dwglu, dwba, dwbs, dwout], dbglu)
    dssm_full = jnp.concatenate([jnp.zeros((Lc, W), BF16), dssm_b], axis=0)
    dus, dlam_re, dlam_im, dcoef_re, dcoef_im, dbf, dcf_re, dcf_im = [], [], [], [], [], [], [], []
    for d in range(2):
        r = _ssm_bwd("ssm_bwd%d" % d, dssm_full, hs_re[d], hs_im[d], ub, bbd[d], bbdt_re[d], bbdt_im[d],
                     cbdt_re[d], cbdt_im[d], lam_re[d] + tok_r2a[0:1, 0:1], lam_im[d], coef_re[d], coef_im[d], Lc,
                     reverse=bool(d))
        for lst, val in zip((dus, dlam_re, dlam_im, dcoef_re, dcoef_im, dbf, dcf_re, dcf_im), r):
            lst.append(val)
    dqr, dkr, dvf = _attn_bwd(qr, kr, vb, dattn, L, Lc, D)

    def prep_bwd_fn(i, qt, kt, ut, dqt, dkt, dvt, du0, du1, dst, dt, qgt, kgt, ct, st):
        live = i >= ncr
        dqt = jnp.where(live, dqt, 0.0)
        dst = jnp.where(live, dst, 0.0)
        dqs, dks = [], []
        dqg_ = jnp.zeros((1, HEAD_DIM), F32)
        dkg_ = jnp.zeros((1, HEAD_DIM), F32)
        for h in range(nh):
            hl = slice(h * HEAD_DIM, (h + 1) * HEAD_DIM)
            dn = dqt[:, hl] * ct + _rot(dqt[:, hl] * st)
            _, vjp = jax.vjp(_head_norm, qt[:, hl], qgt)
            dxh, dgh = vjp(dn)
            dqs.append(dxh)
            dqg_ = dqg_ + dgh
        for h in range(nkvh):
            hl = slice(h * HEAD_DIM, (h + 1) * HEAD_DIM)
            dn = dkt[:, hl] * ct + _rot(dkt[:, hl] * st)
            _, vjp = jax.vjp(_head_norm, kt[:, hl], kgt)
            dxh, dgh = vjp(dn)
            dks.append(dxh)
            dkg_ = dkg_ + dgh
        du_ = du0 + du1 + dst * dt
        return [jnp.concatenate(dqs, axis=1), jnp.concatenate(dks, axis=1), dvt, du_, dqg_, dkg_,
                jnp.sum(dst * ut, axis=0, keepdims=True)]

    dq_b, dk_b, dv_b, du_b, dqg, dkg, dssd = _rowk(
        "qk_prep_bwd", prep_bwd_fn, T, tr,
        [(proj, ('col', D, 1)), (proj, ('col', KV, 0)), (proj, ('col', W, 1)), (dqr, 'xrow'), (dkr, 'row'),
         (dvf, 'row'), (dus[0], 'row'), (dus[1], 'row'), (dssm, 'xrow'), (ssm_d, 'vec'), (qg, 'vec'), (kg, 'vec'),
         (cos_t, 'row'), (sin_t, 'row')],
        [((T, D), BF16, 'row'), ((T, KV), BF16, 'row'), ((T, KV), BF16, 'row'), ((T, W), BF16, 'row'),
         ((1, HEAD_DIM), F32, 'acc'), ((1, HEAD_DIM), F32, 'acc'), ((1, W), F32, 'acc')], nc=ncr)
    dgate = jnp.concatenate([jnp.zeros((Lc, 2 * D), BF16), jnp.concatenate([dga, dgs], axis=1)], axis=0)
    dproj = jnp.concatenate([dk_b, dv_b, du_b, dq_b, dgate], axis=1)
    dh2 = _mm("in_proj_dx", [(dproj, Wt['w_in'], 4 * D)], T, D, tm=_div(T, 768), tn=_div(D, 1024), nk=N_CHIPS, tb=True,
              epi=ident, outs=[(F32, False)])[0]
    dwin = _mm("in_proj_dw", [(h2, dproj, T)], D, 4 * D, tm=_div(D, 512), tn=_div(D, 1024), ta=True, epi=ident,
               outs=[(BF16, True)])[0]
    tok_r2, scatter_fin2 = _scatter_split("scatter_w_in", [dwin], dqg)
    dx1, dng2, dsh2, dsc2 = norm_mod_bwd("norm2_bwd", x1, ng[1:2] + tok_r2[0:1, 0:1], sh2, sc2, dh2, dx2, 'xrow')
    early = {}

    def start_down(dwd):
        early['tok'], early['fin'] = _scatter_split("scatter_ffn1_down", [dwd], dg2)

    dh1, dg1, dwg1, dwu1, dwd1 = ffn_bwd("ffn1", dx1, h1, a1, b1, s1, f1, g1, Wt['w_ffn1_gate'], Wt['w_ffn1_up'],
                                         Wt['w_ffn1_down'], on_dwd=start_down)
    dx0, dng1, dsh1, dsc1 = norm_mod_bwd("norm1_bwd", xc, ng[0:1] + early['tok'][0:1, 0:1], sh1, sc1, dh1, dx1, 'row')
    grad_x = dx0[Lc:][None]

    zD = jnp.zeros((1, D), F32)
    dmod_x = jnp.concatenate([dsh1[1:2], dsc1[1:2], dg1[1:2], dsh2[1:2], dsc2[1:2], dg2, dsh3, dsc3, dg3], axis=1)
    dmod_c = jnp.concatenate([dsh1[0:1], dsc1[0:1], dg1[0:1], dsh2[0:1], dsc2[0:1], zD, zD, zD, zD], axis=1)
    pieces = [dmod_x, dmod_c, dng1, dng2, dng3, dqg, dkg] + dlam_re + dlam_im + dcoef_re + dcoef_im \
        + dbf + dcf_re + dcf_im + [dssd, dbglu]
    shapes = [p_.shape for p_ in pieces]
    pack = _pack(pieces)
    RP = pack.shape[0]
    tok_small, small_gathered = _allgather_split("gather_small", pack, me, dng1)
    tok_r3, scatter_fin3 = _scatter_split("scatter_ffn1_up", [dwg1, dwu1], tok_small)
    results = {}

    def reduce_group(tag, names, fin, after_work):
        sent, landed = fin(after_work)
        plane = [_sum_plane("sum_" + n, g_, rb, chip_index) for n, g_, rb in zip(names, sent, landed)]
        other = _swap_sibling("swap_" + tag, plane)
        for n, mine, theirs in zip(names, plane, other):
            results[n] = _adamw("adamw_" + n, A[n], A['m_' + n], A['v_' + n], [mine, theirs])

    reduce_group("ffn2", big[3:6], scatter_fin1, tok_r3)
    reduce_group("mix", big[7:11], scatter_fin2a, results['w_ffn2_down'][0])
    reduce_group("w_in", big[6:7], scatter_fin2, results['w_out'][0])
    reduce_group("ffn1_down", big[2:3], early['fin'], results['w_in'][0])
    allp = small_gathered(results['w_ffn1_down'][0])
    head_rows = -(-18 * D // PACK_W)
    head = allp[:, :head_rows].reshape(N_DEV, head_rows * PACK_W)
    dmx_all = head[:, :9 * D]

    def sum_rows_fn(i, t):
        s_ = t[0:1]
        for k in range(1, N_DEV):
            s_ = s_ + t[k:k + 1]
        return [s_]

    dmc_sum = _rowk("sum_dmod_c", sum_rows_fn, 1, 1, [(head[:, 9 * D:18 * D], 'vec')], [((1, 9 * D), F32, 'row')])[0]
    cots = jnp.concatenate([dmx_all, dmc_sum, jnp.zeros((7, 9 * D), F32)], axis=0)
    cots_sh = lax.dynamic_slice(cots, (0, chip * NM), (16, NM))
    part = _mm("cctx_part", [(cots_sh[8:16], wm, NM)], 8, D, tm=8, tn=_div(D, 1024), nk=NM // _div(NM, 1152), tb=True,
               epi=ident, outs=[(F32, False)], a_pro=to_bf, b_pro=to_bf)[0]
    _, cctx_gathered = _allgather_split("gather_cctx", part, me, part)

    def sum_dev_fn(i, t):
        s_ = t[0]
        for k in range(1, N_DEV):
            s_ = s_ + t[k]
        return [s_]

    tot = _rowk("sum_small", sum_dev_fn, RP, 8, [(allp, 'row3')], [((RP, PACK_W), F32, 'row')])[0]
    (t_dmod_x, t_dmod_c, t_ng1, t_ng2, t_ng3, t_qg, t_kg, t_lr0, t_lr1, t_li0, t_li1, t_kr0, t_kr1, t_ki0, t_ki1,
     t_dbf0, t_dbf1, t_dcr0, t_dcr1, t_dci0, t_dci1, t_d, t_bglu) = _unpack(tot, shapes)
    b_grad = lambda t, lo: jnp.transpose(t[:, :, lo:lo + P].reshape(G, E, P), (0, 2, 1))
    c_grad = lambda t: jnp.transpose(t.reshape(nslab, P, SLAB_GROUPS, E), (0, 2, 3, 1)).reshape(G, E, P)
    cat2 = lambda u0, u1: jnp.concatenate([u0.reshape(G, P), u1.reshape(G, P)], axis=0)
    g_are, g_aim, g_ldt = _zoh_bwd(a_re2, a_im2, ldt2, [cat2(t_lr0, t_lr1), cat2(t_li0, t_li1),
                                                         cat2(t_kr0, t_kr1), cat2(t_ki0, t_ki1)])
    g_bmod = _rowk("bmod_grad", lambda i, u0, u1: [u0 + u1], 1, 1, [(t_dmod_x, 'row'), (t_dmod_c, 'row')],
                   [((1, 9 * D), F32, 'row')])[0]
    g_wmod = _outer_sum(acts, cots_sh)
    results['w_mod'] = _adamw("adamw_w_mod", w_mod, m_w_mod, v_w_mod, [g_wmod])
    reduce_group("ffn1_up", big[0:2], scatter_fin3, results['w_mod'][0])
    parts = cctx_gathered(results['w_ffn1_up'][0]).reshape(N_CHIPS, 2, 8, D)[:, 0, 0]

    def cctx_fn(i, pt, ct):
        ds_ = ((pt[0:1] + pt[1:2]) + pt[2:3]) + pt[3:4]
        _, vjp = jax.vjp(lambda v: v * _sigmoid(v), ct)
        return [vjp(ds_)[0]]

    g_cctx = _rowk("cctx_grad", cctx_fn, 1, 1, [(parts, 'vec'), (c_ctx[None], 'row')], [((1, D), F32, 'row')])[0]

    ng_full =jnp.concatenate([t_ng1, t_ng2, t_ng3], axis=0)
    gsmall = {
        'c_ctx': g_cctx, 'b_mod': g_bmod, 'norm_g': lax.dynamic_slice(ng_full, (0, chip * Dq), (3, Dq)),
        'q_norm_g': t_qg, 'k_norm_g': t_kg, 'ssm_a_re': g_are, 'ssm_a_im': g_aim, 'ssm_log_dt': g_ldt,
        'ssm_b_re': jnp.stack([b_grad(t_dbf0, 0), b_grad(t_dbf1, 0)]),
        'ssm_b_im': jnp.stack([b_grad(t_dbf0, P), b_grad(t_dbf1, P)]),
        'ssm_c_re': jnp.stack([c_grad(t_dcr0), c_grad(t_dcr1)]), 'ssm_c_im': jnp.stack([c_grad(t_dci0), c_grad(t_dci1)]),
        'ssm_d': t_d, 'b_glu': t_bglu}
    sshapes = [A[n].shape for n in small]
    sres = _adamw("adamw_small", packs_wmv[0], packs_wmv[1], packs_wmv[2], [_pack([gsmall[n] for n in small])])
    sres = [_unpack(b_, sshapes) for b_ in sres]
    for k, n in enumerate(small):
        results[n] = tuple(sres[q][k] for q in range(4))

    order = ['c_ctx', 'w_mod', 'b_mod', 'norm_g', 'w_ffn1_gate', 'w_ffn1_up', 'w_ffn1_down', 'w_in', 'q_norm_g',
             'k_norm_g', 'ssm_a_re', 'ssm_a_im', 'ssm_log_dt', 'ssm_b_re', 'ssm_b_im', 'ssm_c_re', 'ssm_c_im',
             'ssm_d', 'w_glu', 'b_glu', 'w_br_attn', 'w_br_ssm', 'w_out', 'w_ffn2_gate', 'w_ffn2_up', 'w_ffn2_down']
    outs = [loss, grad_x]
    for q in range(4):
        outs += [results[n][q].reshape(A[n].shape) for n in order]
    return tuple(outs)
```

```python
import math

import jax
import jax.numpy as jnp
from jax import lax
from jax.experimental import pallas as pl
from jax.experimental.pallas import tpu as pltpu

F32 = jnp.float32
BF16 = jnp.bfloat16
MESH = pl.DeviceIdType.MESH

NORM_EPS = 1e-6
ROPE_THETA = 10000.0
GRID_W = 64
HEAD_DIM = 128
Q_PER_KV = 4
SSM_GROUP = 16
SSM_STATE = 64
ADAM_LR = 0.001
ADAM_B1 = 0.9
ADAM_B2 = 0.999
ADAM_EPS = 1e-08
ADAM_WD = 0.01
ADAM_STEP = 10

N_CHIPS = 4
N_DEV = 8
LANES = 128
SLAB_CH = 128
SLAB_GROUPS = SLAB_CH // SSM_GROUP
SLAB_ST = SLAB_GROUPS * SSM_STATE
VMEM_LIMIT_BYTES = 56 * 1024 * 1024
PACK_W = 1024


def _cparams(**kw):
    return pltpu.CompilerParams(vmem_limit_bytes=VMEM_LIMIT_BYTES, **kw)


def _div(n, pref, mult=LANES):
    t = (min(pref, n) // mult) * mult
    while t >= mult:
        if n % t == 0:
            return t
        t -= mult
    return n


def _sigmoid(x):
    return jax.nn.sigmoid(x)


def _gelu(x):
    return x * (0.5 * (1.0 + jnp.tanh(math.sqrt(2.0 / math.pi) * (x + 0.044715 * (x * x * x)))))


def _rowk(name, fn, nrows, tr, ins, outs, nc=0):
    nt = nrows // tr
    in_specs, arrays = [], []
    for arr, kind in ins:
        arrays.append(arr)
        if kind == 'row':
            in_specs.append(pl.BlockSpec((tr, arr.shape[1]), lambda i: (i, 0)))
        elif kind == 'xrow':
            in_specs.append(pl.BlockSpec((tr, arr.shape[1]), lambda i: (jnp.maximum(i - nc, 0), 0)))
        elif kind == 'orow':
            in_specs.append(pl.BlockSpec((tr, arr.shape[1]), lambda i: (i + nc, 0)))
        elif kind == 'vec':
            in_specs.append(pl.BlockSpec(arr.shape, lambda i, nd=arr.ndim: (0,) * nd))
        elif kind == 'row3':
            in_specs.append(pl.BlockSpec((arr.shape[0], tr, arr.shape[2]), lambda i: (0, i, 0)))
        elif kind == 'row1':
            in_specs.append(pl.BlockSpec((None, tr, arr.shape[2]), lambda i: (0, i, 0)))
        elif kind[0] == 'ocol':
            _, width, blk = kind
            in_specs.append(pl.BlockSpec((tr, width), lambda i, blk=blk: (i + nc, blk)))
        else:
            _, width, blk = kind
            in_specs.append(pl.BlockSpec((tr, width), lambda i, blk=blk: (i, blk)))
    out_shape, out_specs = [], []
    for shape, dtype, kind in outs:
        out_shape.append(jax.ShapeDtypeStruct(shape, dtype))
        if kind == 'row':
            out_specs.append(pl.BlockSpec((tr, shape[1]), lambda i: (i, 0)))
        elif kind == 'row1':
            out_specs.append(pl.BlockSpec((None, tr, shape[2]), lambda i: (0, i, 0)))
        else:
            out_specs.append(pl.BlockSpec(shape, lambda i, nd=len(shape): (0,) * nd))
    nin = len(ins)

    def body(*refs):
        i = pl.program_id(0)
        res = fn(i, *[r[...] for r in refs[:nin]])
        for (shape, dtype, kind), ref, val in zip(outs, refs[nin:], res):
            if kind in ('row', 'row1'):
                ref[...] = val.astype(dtype)
            else:
                @pl.when(i == 0)
                def _():
                    ref[...] = val.astype(dtype)

                @pl.when(i > 0)
                def _():
                    ref[...] += val.astype(dtype)

    return pl.pallas_call(body, name=name, grid=(nt,), in_specs=in_specs, out_specs=out_specs,
                          out_shape=out_shape, compiler_params=_cparams())(*arrays)


def _mm(name, pairs, M, N, *, tm, tn, nk=1, epi, outs, ta=False, tb=False, rows=(), vecs=(),
        a_pro=None, b_pro=None, n_outer=True, summed=False):
    nm, nn = M // tm, N // tn
    npair = len(pairs)

    def idx(f):
        if n_outer:
            return lambda j, i, k: f(i, j, k)
        return lambda i, j, k: f(i, j, k)

    in_specs, args = [], []
    for a, b, K in pairs:
        tk = K // nk
        if ta:
            in_specs.append(pl.BlockSpec((tk, tm), idx(lambda i, j, k: (k, i))))
        else:
            in_specs.append(pl.BlockSpec((tm, tk), idx(lambda i, j, k: (i, k))))
        args.append(a)
        if b.ndim == 3:
            if tb:
                per = b.shape[2] // tk
                in_specs.append(pl.BlockSpec((None, tn, tk), idx(lambda i, j, k, per=per: (k // per, j, k % per))))
            else:
                per = b.shape[2] // tn
                in_specs.append(pl.BlockSpec((None, tk, tn), idx(lambda i, j, k, per=per: (j // per, k, j % per))))
        elif tb:
            in_specs.append(pl.BlockSpec((tn, tk), idx(lambda i, j, k: (j, k))))
        else:
            in_specs.append(pl.BlockSpec((tk, tn), idx(lambda i, j, k: (k, j))))
        args.append(b)
    for arr, ro, co in rows:
        in_specs.append(pl.BlockSpec((tm, tn), idx(lambda i, j, k, ro=ro, co=co: (i + ro, j + co))))
        args.append(arr)
    for arr in vecs:
        in_specs.append(pl.BlockSpec((arr.shape[0], tn), idx(lambda i, j, k: (0, j))))
        args.append(arr)
    out_shape, out_specs = [], []
    for dtype, chunked in outs:
        if chunked:
            per = (N // N_CHIPS) // tn
            out_shape.append(jax.ShapeDtypeStruct((N_CHIPS, M, N // N_CHIPS), dtype))
            out_specs.append(pl.BlockSpec((None, tm, tn), idx(lambda i, j, k, per=per: (j // per, i, j % per))))
        else:
            out_shape.append(jax.ShapeDtypeStruct((M, N), dtype))
            out_specs.append(pl.BlockSpec((tm, tn), idx(lambda i, j, k: (i, j))))
    nacc = 1 if summed else npair
    scratch = [pltpu.VMEM((tm, tn), F32) for _ in range(nacc)] if nk > 1 else []
    nrow, nvec, nout = len(rows), len(vecs), len(outs)
    dims = (((0 if ta else 1,), (1 if tb else 0,)), ((), ()))

    def body(*refs):
        ab = refs[:2 * npair]
        row_refs = refs[2 * npair:2 * npair + nrow]
        vec_refs = refs[2 * npair + nrow:2 * npair + nrow + nvec]
        out_refs = refs[2 * npair + nrow + nvec:2 * npair + nrow + nvec + nout]
        acc_refs = refs[2 * npair + nrow + nvec + nout:]
        if n_outer:
            j, i, k = pl.program_id(0), pl.program_id(1), pl.program_id(2)
        else:
            i, j, k = pl.program_id(0), pl.program_id(1), pl.program_id(2)

        def part(p):
            av, bv = ab[2 * p][...], ab[2 * p + 1][...]
            if a_pro is not None:
                av = a_pro(av)
            if b_pro is not None:
                bv = b_pro(bv)
            return lax.dot_general(av, bv, dims, preferred_element_type=F32)

        def finish(accs):
            row_index = i * tm + lax.broadcasted_iota(jnp.int32, (tm, 1), 0)
            res = epi(accs, [r[...] for r in row_refs], [v[...] for v in vec_refs], row_index)
            for ref, val in zip(out_refs, res):
                ref[...] = val.astype(ref.dtype)

        parts = [part(p) for p in range(npair)]
        if summed:
            total = parts[0]
            for extra in parts[1:]:
                total = total + extra
            parts = [total]
        if nk == 1:
            finish(parts)
        else:
            @pl.when(k == 0)
            def _():
                for q in range(nacc):
                    acc_refs[q][...] = parts[q]

            @pl.when(jnp.logical_and(k > 0, k < nk - 1))
            def _():
                for q in range(nacc):
                    acc_refs[q][...] += parts[q]

            @pl.when(k == nk - 1)
            def _():
                finish([acc_refs[q][...] + parts[q] for q in range(nacc)])

    grid = (nn, nm, nk) if n_outer else (nm, nn, nk)
    return pl.pallas_call(body, name=name, grid=grid, in_specs=in_specs, out_specs=out_specs,
                          out_shape=out_shape, scratch_shapes=scratch, compiler_params=_cparams())(*args)


def _split3(v):
    v0 = v.astype(BF16)
    r1 = v - v0.astype(F32)
    v1 = r1.astype(BF16)
    v2 = (r1 - v1.astype(F32)).astype(BF16)
    return v0, v1, v2


def _mesh_pos():
    return lax.axis_index("x"), lax.axis_index("y"), lax.axis_index("c")


def _allgather_small(name, x):
    m, n = x.shape

    def body(x_ref, out_ref, send_sems, recv_sems, local_sem):
        xi, yi, ci = _mesh_pos()
        me, sibling = (xi, yi, ci), (xi, yi, 1 - ci)
        chips = [(1 - xi, yi), (xi, 1 - yi), (1 - xi, 1 - yi)]

        def rows(px, py, pc):
            return out_ref.at[pl.ds((4 * px + 2 * py + pc) * m, m), :]

        def copy(k, block, to, src=None):
            return pltpu.make_async_remote_copy(
                src_ref=rows(*block) if src is None else src, dst_ref=rows(*block),
                send_sem=send_sems.at[k], recv_sem=recv_sems.at[k], device_id=to, device_id_type=MESH)

        mine = pltpu.make_async_copy(x_ref, rows(*me), local_sem)
        mine.start()
        first = [copy(0, me, sibling, src=x_ref)]
        first += [copy(1 + j, me, (*chip, ci), src=x_ref) for j, chip in enumerate(chips)]
        for cp in first:
            cp.start()
        passed = [copy(4 + j, (*chip, ci), sibling) for j, chip in enumerate(chips)]
        for j, chip in enumerate(chips):
            copy(1 + j, (*chip, ci), me).wait_recv()
            passed[j].start()
        copy(0, sibling, me).wait_recv()
        for j, chip in enumerate(chips):
            copy(4 + j, (*chip, 1 - ci), me).wait_recv()
        for cp in first + passed:
            cp.wait_send()
        mine.wait()

    return pl.pallas_call(
        body, name=name, out_shape=jax.ShapeDtypeStruct((N_DEV * m, n), x.dtype),
        in_specs=[pl.BlockSpec(memory_space=pltpu.VMEM)], out_specs=pl.BlockSpec(memory_space=pltpu.VMEM),
        scratch_shapes=[pltpu.SemaphoreType.DMA((7,)), pltpu.SemaphoreType.DMA((7,)), pltpu.SemaphoreType.DMA],
        compiler_params=_cparams())(x)


_HBM = pl.BlockSpec(memory_space=pltpu.HBM)
_SEM = pl.BlockSpec(memory_space=pltpu.SEMAPHORE)
_ANY = pl.BlockSpec(memory_space=pl.ANY)
_EFFECT = pltpu.SideEffectType.DATAFLOW_SIDE_EFFECTING


def _in_hbm(v):
    return pltpu.with_memory_space_constraint(v, pltpu.HBM)


def _other_chips(xi, yi):
    return [(1 - xi, yi), (xi, 1 - yi), (1 - xi, 1 - yi)]


def _guarded(core, fn):
    if core is None:
        fn()
    else:
        pl.when(lax.axis_index("c") == core)(fn)


def _split_copies(name, srcs, lands, after, pairs, senders, receivers, ncopy):
    ns, nl = len(srcs), len(lands)
    dma = pltpu.SemaphoreType.DMA((ncopy,))
    thru = [pltpu.HBM(v.shape, v.dtype) for v in list(srcs) + list(lands)]

    def start_body(*refs):
        src_refs, land_refs = refs[:ns], refs[ns:ns + nl]
        descs = pairs(src_refs, land_refs, refs[ns + nl + 1], refs[ns + nl + 2])

        def go():
            for send, _ in descs:
                send.start()

        _guarded(senders, go)
        refs[-1][...] = jnp.zeros_like(refs[-1])

    res = pl.pallas_call(
        start_body, name=name + "_start",
        out_shape=(dma, dma, *thru, jax.ShapeDtypeStruct((8, LANES), F32)),
        in_specs=[_HBM] * (ns + nl) + [_ANY],
        out_specs=(_SEM, _SEM, *([_HBM] * (ns + nl)), pl.BlockSpec(memory_space=pltpu.VMEM)),
        input_output_aliases={k: 2 + k for k in range(ns + nl)},
        compiler_params=_cparams(has_side_effects=_EFFECT),
    )(*[_in_hbm(v) for v in srcs], *[_in_hbm(v) for v in lands], after)
    send_sems, recv_sems, token = res[0], res[1], res[-1]
    carried = res[2:2 + ns + nl]

    def finish(after_work):
        def wait_body(*refs):
            src_refs, land_refs = refs[:ns], refs[ns:ns + nl]
            descs = pairs(src_refs, land_refs, refs[ns + nl], refs[ns + nl + 1])

            def sent():
                for send, _ in descs:
                    send.wait_send()

            def landed():
                for _, recv in descs:
                    recv.wait_recv()

            _guarded(senders, sent)
            _guarded(receivers, landed)

        out = pl.pallas_call(
            wait_body, name=name + "_wait", out_shape=tuple(thru),
            in_specs=[_HBM] * (ns + nl) + [_SEM, _SEM, _ANY], out_specs=tuple([_HBM] * (ns + nl)),
            input_output_aliases={k: k for k in range(ns + nl)},
            compiler_params=_cparams(has_side_effects=_EFFECT),
        )(*carried, send_sems, recv_sems, after_work)
        return list(out[:ns]), list(out[ns:])

    return token, finish


def _cast_slot(name, w, chip_index, after):
    R, C = w.shape[1:]
    tr = _div(R, max(16, 524288 // C), mult=16)

    def body(chip_ref, w_ref, after_ref, o_ref):
        o_ref[...] = w_ref[...].astype(BF16)

    return pl.pallas_call(
        body, name=name, out_shape=jax.ShapeDtypeStruct((N_CHIPS, R, C), BF16),
        grid_spec=pltpu.PrefetchScalarGridSpec(
            num_scalar_prefetch=1, grid=(R // tr,),
            in_specs=[pl.BlockSpec((None, tr, C), lambda i, chip_ref: (0, i, 0)), _ANY],
            out_specs=pl.BlockSpec((None, tr, C), lambda i, chip_ref: (chip_ref[0], i, 0))),
        compiler_params=_cparams())(chip_index, w, after)


def _sum_plane(name, grads, landed, chip_index):
    R, C = grads.shape[1:]
    tr = _div(R, max(16, 262144 // C), mult=16)

    def body(chip_ref, own_ref, land_ref, o_ref):
        o_ref[...] = ((own_ref[...].astype(F32) + land_ref[0].astype(F32)) + land_ref[1].astype(F32)) \
            + land_ref[2].astype(F32)

    return pl.pallas_call(
        body, name=name, out_shape=jax.ShapeDtypeStruct((R, C), F32),
        grid_spec=pltpu.PrefetchScalarGridSpec(
            num_scalar_prefetch=1, grid=(R // tr,),
            in_specs=[pl.BlockSpec((None, tr, C), lambda i, chip_ref: (chip_ref[0], i, 0)),
                      pl.BlockSpec((3, tr, C), lambda i, chip_ref: (0, i, 0))],
            out_specs=pl.BlockSpec((tr, C), lambda i, chip_ref: (i, 0))),
        compiler_params=_cparams())(chip_index, grads, landed)


def _gather_split(name, lands, after):
    def pairs(src_refs, land_refs, send_sems, recv_sems):
        xi, yi, _ = _mesh_pos()
        mine = 2 * xi + yi
        out = []
        for a in range(len(lands)):
            for j, (px, py) in enumerate(_other_chips(xi, yi)):
                def to_slot(slot, a=a, j=j, px=px, py=py):
                    return pltpu.make_async_remote_copy(
                        src_ref=land_refs[a].at[mine], dst_ref=land_refs[a].at[slot], send_sem=send_sems.at[3 * a + j],
                        recv_sem=recv_sems.at[3 * a + j], device_id=(px, py, 1), device_id_type=MESH)
                out.append((to_slot(mine), to_slot(2 * px + py)))
        return out

    return _split_copies(name, [], lands, after, pairs, senders=1, receivers=1, ncopy=3 * len(lands))


def _allgather_split(name, block, me, after):
    land = lax.dynamic_update_slice(lax.empty((N_DEV,) + block.shape, block.dtype), block[None], (me, 0, 0))

    def pairs(src_refs, land_refs, send_sems, recv_sems):
        xi, yi, ci = _mesh_pos()
        mine = 4 * xi + 2 * yi + ci
        out = []
        for k in range(1, N_DEV):
            kx, ky, kc = (k >> 2) & 1, (k >> 1) & 1, k & 1
            px = 1 - xi if kx else xi
            py = 1 - yi if ky else yi
            pc = 1 - ci if kc else ci

            def to_slot(slot, k=k, px=px, py=py, pc=pc):
                return pltpu.make_async_remote_copy(
                    src_ref=land_refs[0].at[mine], dst_ref=land_refs[0].at[slot], send_sem=send_sems.at[k - 1],
                    recv_sem=recv_sems.at[k - 1], device_id=(px, py, pc), device_id_type=MESH)
            out.append((to_slot(mine), to_slot(4 * px + 2 * py + pc)))
        return out

    tok, fin = _split_copies(name, [], [land], after, pairs, senders=None, receivers=None, ncopy=N_DEV - 1)
    return tok, lambda later: fin(later)[1][0]


def _swap_split(name, arrs, after):
    lands = [lax.empty(v.shape, v.dtype) for v in arrs]

    def pairs(src_refs, land_refs, send_sems, recv_sems):
        xi, yi, ci = _mesh_pos()
        out = []
        for a in range(len(arrs)):
            cp = pltpu.make_async_remote_copy(
                src_ref=src_refs[a], dst_ref=land_refs[a], send_sem=send_sems.at[a], recv_sem=recv_sems.at[a],
                device_id=(xi, yi, 1 - ci), device_id_type=MESH)
            out.append((cp, cp))
        return out

    return _split_copies(name, arrs, lands, after, pairs, senders=None, receivers=None, ncopy=len(arrs))


def _pass_split(name, lands, after):
    def pairs(src_refs, land_refs, send_sems, recv_sems):
        xi, yi, _ = _mesh_pos()
        out = []
        for a in range(len(lands)):
            for j, (px, py) in enumerate(_other_chips(xi, yi)):
                cp = pltpu.make_async_remote_copy(
                    src_ref=land_refs[a].at[2 * px + py], dst_ref=land_refs[a].at[2 * px + py],
                    send_sem=send_sems.at[3 * a + j], recv_sem=recv_sems.at[3 * a + j],
                    device_id=(xi, yi, 0), device_id_type=MESH)
                out.append((cp, cp))
        return out

    return _split_copies(name, [], lands, after, pairs, senders=1, receivers=0, ncopy=3 * len(lands))


def _scatter_split(name, grads, after):
    lands = [lax.empty((3,) + g.shape[1:], g.dtype) for g in grads]

    def pairs(src_refs, land_refs, send_sems, recv_sems):
        xi, yi, ci = _mesh_pos()
        out = []
        for a in range(len(grads)):
            for j, (px, py) in enumerate(_other_chips(xi, yi)):
                cp = pltpu.make_async_remote_copy(
                    src_ref=src_refs[a].at[2 * px + py], dst_ref=land_refs[a].at[j], send_sem=send_sems.at[3 * a + j],
                    recv_sem=recv_sems.at[3 * a + j], device_id=(px, py, ci), device_id_type=MESH)
                out.append((cp, cp))
        return out

    return _split_copies(name, grads, lands, after, pairs, senders=None, receivers=None, ncopy=3 * len(grads))


def _gather_finish(name, lands):
    na = len(lands)

    def body(*refs):
        outs = refs[na:2 * na]
        send_sems, recv_sems = refs[2 * na:]
        xi, yi, ci = _mesh_pos()
        passes = [pltpu.make_async_remote_copy(
            src_ref=outs[a].at[2 * px + py], dst_ref=outs[a].at[2 * px + py],
            send_sem=send_sems.at[a, j], recv_sem=recv_sems.at[a, j], device_id=(xi, yi, 0), device_id_type=MESH)
            for a in range(na) for j, (px, py) in enumerate(_other_chips(xi, yi))]

        @pl.when(ci == 1)
        def _():
            for cp in passes:
                cp.start()
            for cp in passes:
                cp.wait_send()

        @pl.when(ci == 0)
        def _():
            for cp in passes:
                cp.wait_recv()

    return pl.pallas_call(
        body, name=name, out_shape=[jax.ShapeDtypeStruct(v.shape, v.dtype) for v in lands],
        in_specs=[_ANY] * na, out_specs=[_ANY] * na,
        input_output_aliases={a: a for a in range(na)},
        scratch_shapes=[pltpu.SemaphoreType.DMA((na, 3)), pltpu.SemaphoreType.DMA((na, 3))],
        compiler_params=_cparams())(*lands)


def _attn_tiles(L, Lc, D):
    tq = min(256, Lc)
    return tq, L // tq, Lc // tq, D // HEAD_DIM // Q_PER_KV


def _attn_probs(q, k):
    s = lax.dot_general(q, k, (((1,), (1,)), ((), ())), preferred_element_type=F32) * (HEAD_DIM ** -0.5)
    e = jnp.exp(s - jnp.max(s, axis=-1, keepdims=True))
    return e * (1.0 / jnp.sum(e, axis=-1, keepdims=True))


def _attn_fwd(qr, kr, v, L, Lc, D):
    T = L + Lc
    tq, nq, qoff, nkv = _attn_tiles(L, Lc, D)

    def body(q_ref, k_ref, v_ref, o_ref):
        p = _attn_probs(q_ref[...], k_ref[...])
        o_ref[...] = jnp.dot(p.astype(BF16), v_ref[...], preferred_element_type=F32).astype(o_ref.dtype)

    kv_spec = pl.BlockSpec((T, HEAD_DIM), lambda h, r, q: (0, h))
    return pl.pallas_call(
        body, name="attn_fwd", grid=(nkv, Q_PER_KV, nq),
        in_specs=[pl.BlockSpec((tq, HEAD_DIM), lambda h, r, q: (q + qoff, h * Q_PER_KV + r)), kv_spec, kv_spec],
        out_specs=pl.BlockSpec((tq, HEAD_DIM), lambda h, r, q: (q, h * Q_PER_KV + r)),
        out_shape=jax.ShapeDtypeStruct((L, D), BF16), compiler_params=_cparams())(qr, kr, v)


def _attn_bwd(qr, kr, v, do, L, Lc, D):
    T = L + Lc
    tq, nq, qoff, nkv = _attn_tiles(L, Lc, D)
    scale = HEAD_DIM ** -0.5

    def body(q_ref, k_ref, v_ref, do_ref, dq_ref, dk_ref, dv_ref):
        first = jnp.logical_and(pl.program_id(1) == 0, pl.program_id(2) == 0)
        q, k, dout = q_ref[...], k_ref[...], do_ref[...]
        p = _attn_probs(q, k)
        dp = lax.dot_general(dout, v_ref[...], (((1,), (1,)), ((), ())), preferred_element_type=F32)
        ds = (p * (dp - jnp.sum(p * dp, axis=-1, keepdims=True)) * scale).astype(BF16)
        dq_ref[...] = jnp.dot(ds, k, preferred_element_type=F32)
        dk = lax.dot_general(ds, q, (((0,), (0,)), ((), ())), preferred_element_type=F32)
        dv = lax.dot_general(p.astype(BF16), dout, (((0,), (0,)), ((), ())), preferred_element_type=F32)

        @pl.when(first)
        def _():
            dk_ref[...] = dk
            dv_ref[...] = dv

        @pl.when(jnp.logical_not(first))
        def _():
            dk_ref[...] += dk
            dv_ref[...] += dv

    kv_spec = pl.BlockSpec((T, HEAD_DIM), lambda h, r, q: (0, h))
    q_spec = pl.BlockSpec((tq, HEAD_DIM), lambda h, r, q: (q + qoff, h * Q_PER_KV + r))
    o_spec = pl.BlockSpec((tq, HEAD_DIM), lambda h, r, q: (q, h * Q_PER_KV + r))
    return pl.pallas_call(
        body, name="attn_bwd", grid=(nkv, Q_PER_KV, nq),
        in_specs=[q_spec, kv_spec, kv_spec, o_spec], out_specs=[o_spec, kv_spec, kv_spec],
        out_shape=[jax.ShapeDtypeStruct((L, D), F32), jax.ShapeDtypeStruct((T, D // Q_PER_KV), F32),
                   jax.ShapeDtypeStruct((T, D // Q_PER_KV), F32)],
        compiler_params=_cparams())(qr, kr, v, do)


SUB = 8


def _doubling(xr, xi, pw_re, pw_im, lanes, first_power, period, reverse):
    n = xr.shape[0]
    rows = lax.broadcasted_iota(jnp.int32, (n, 1), 0) & (period - 1)
    for k in range(period.bit_length() - 1):
        d = 1 << k
        keep = rows < period - d if reverse else rows >= d
        sr = jnp.where(keep, pltpu.roll(xr, n - d if reverse else d, 0), 0.0)
        si = jnp.where(keep, pltpu.roll(xi, n - d if reverse else d, 0), 0.0)
        pr, pi = pw_re[first_power + k:first_power + k + 1, lanes], pw_im[first_power + k:first_power + k + 1, lanes]
        xr, xi = xr + (pr * sr - pi * si), xi + (pr * si + pi * sr)
    return xr, xi


def _scan_tile(xr, xi, tb, lanes, reverse):
    pw_re, pw_im, w8_re, w8_im, wb_re, wb_im, carry_re, carry_im, sr, si = tb
    tt = xr.shape[0]
    nb = tt // SUB
    xr, xi = _doubling(xr, xi, pw_re, pw_im, lanes, 0, SUB, reverse)
    nq = sr.shape[0]
    cols = [slice(q * LANES, (q + 1) * LANES) for q in range(nq)]
    for q in range(nq):
        sr[q] = xr[:, cols[q]]
        si[q] = xi[:, cols[q]]
    last = 0 if reverse else SUB - 1
    er = jnp.concatenate([sr[q, pl.ds(last, nb, stride=SUB), :] for q in range(nq)], axis=1)
    ei = jnp.concatenate([si[q, pl.ds(last, nb, stride=SUB), :] for q in range(nq)], axis=1)
    er, ei = _doubling(er, ei, pw_re, pw_im, lanes, 3, nb, reverse)
    car, cai = carry_re[:, lanes], carry_im[:, lanes]
    wbr, wbi = wb_re[:, lanes], wb_im[:, lanes]
    er = er + (wbr * car - wbi * cai)
    ei = ei + (wbr * cai + wbi * car)
    out_block = 0 if reverse else nb - 1
    carry_re[:, lanes] = er[out_block:out_block + 1, :]
    carry_im[:, lanes] = ei[out_block:out_block + 1, :]
    blocks = lax.broadcasted_iota(jnp.int32, (nb, 1), 0)
    first = blocks == (nb - 1 if reverse else 0)
    cr = jnp.where(first, car, pltpu.roll(er, nb - 1 if reverse else 1, 0))
    ci = jnp.where(first, cai, pltpu.roll(ei, nb - 1 if reverse else 1, 0))
    for r in range(SUB):
        wr, wi = w8_re[r:r + 1, lanes], w8_im[r:r + 1, lanes]
        add_r, add_i = wr * cr - wi * ci, wr * ci + wi * cr
        for q in range(nq):
            sr[q, pl.ds(r, nb, stride=SUB), :] += add_r[:, cols[q]]
            si[q, pl.ds(r, nb, stride=SUB), :] += add_i[:, cols[q]]
    hr = jnp.concatenate([sr[q] for q in range(nq)], axis=1)
    hi = jnp.concatenate([si[q] for q in range(nq)], axis=1)
    return hr, hi, car, cai


def _scan_scratch(tt, NS):
    nb = tt // SUB
    return [pltpu.VMEM((8, NS), F32), pltpu.VMEM((8, NS), F32), pltpu.VMEM((SUB, NS), F32), pltpu.VMEM((SUB, NS), F32),
            pltpu.VMEM((nb, NS), F32), pltpu.VMEM((nb, NS), F32), pltpu.VMEM((1, NS), F32), pltpu.VMEM((1, NS), F32),
            pltpu.VMEM((SLAB_ST // LANES, tt, LANES), F32), pltpu.VMEM((SLAB_ST // LANES, tt, LANES), F32)]


def _scan_init(lr, li, tb, reverse):
    pw_re, pw_im, w8_re, w8_im, wb_re, wb_im, carry_re, carry_im, sr, _ = tb
    nb = wb_re.shape[0]
    carry_re[...] = jnp.zeros_like(carry_re)
    carry_im[...] = jnp.zeros_like(carry_im)
    pr, pi = lr, li
    for k in range(3 + nb.bit_length() - 1):
        pw_re[k:k + 1, :] = pr
        pw_im[k:k + 1, :] = pi
        if k == 3:
            l8r, l8i = pr, pi
        pr, pi = pr * pr - pi * pi, 2.0 * pr * pi
    pr, pi = lr, li
    for r in range(SUB):
        row = SUB - 1 - r if reverse else r
        w8_re[row:row + 1, :] = pr
        w8_im[row:row + 1, :] = pi
        pr, pi = pr * lr - pi * li, pr * li + pi * lr
    pr, pi = l8r, l8i
    for b in range(nb):
        row = nb - 1 - b if reverse else b
        wb_re[row:row + 1, :] = pr
        wb_im[row:row + 1, :] = pi
        pr, pi = pr * l8r - pi * l8i, pr * l8i + pi * l8r


def _ssm_tiles(T, Lc):
    tt = min(128, Lc)
    return tt, T // tt, Lc // tt


def _ssm_fwd(name, u, bbd, cbd_re, cbd_im, lam_re, lam_im, coef_re, coef_im, Lc, reverse):
    T, W = u.shape
    nslab = W // SLAB_CH
    NS = nslab * SLAB_ST
    tt, nt, nc = _ssm_tiles(T, Lc)
    if reverse:
        tile = lambda s: jnp.where(s < nc, nc - 1 - s, nt - 1 - (s - nc))
    else:
        tile = lambda s: s

    def body(u_ref, b_ref, cr_ref, ci_ref, lr_ref, li_ref, kr_ref, ki_ref, hr_ref, hi_ref, y_ref, *tb):
        @pl.when(pl.program_id(0) == 0)
        def _():
            _scan_init(lr_ref[...], li_ref[...], tb, reverse)

        for j in range(nslab):
            lanes = slice(j * SLAB_ST, (j + 1) * SLAB_ST)
            bu = jnp.dot(u_ref[:, j * SLAB_CH:(j + 1) * SLAB_CH], b_ref[j], preferred_element_type=F32)
            br, bi = bu[:, :SLAB_ST], bu[:, SLAB_ST:]
            kr, ki = kr_ref[:, lanes], ki_ref[:, lanes]
            hr, hi, _, _ = _scan_tile(kr * br - ki * bi, kr * bi + ki * br, tb, lanes, reverse)
            hrb, hib = hr.astype(BF16), hi.astype(BF16)
            hr_ref[:, lanes] = hrb
            hi_ref[:, lanes] = hib
            y_ref[:, j * SLAB_CH:(j + 1) * SLAB_CH] = (
                jnp.dot(hrb, cr_ref[j], preferred_element_type=F32)
                - jnp.dot(hib, ci_ref[j], preferred_element_type=F32))

    whole3 = lambda arr: pl.BlockSpec(arr.shape, lambda s: (0, 0, 0))
    vec = pl.BlockSpec((1, NS), lambda s: (0, 0))
    return pl.pallas_call(
        body, name=name, grid=(nt,),
        in_specs=[pl.BlockSpec((tt, W), lambda s: (tile(s), 0)), whole3(bbd), whole3(cbd_re), whole3(cbd_im),
                  vec, vec, vec, vec],
        out_specs=[pl.BlockSpec((tt, NS), lambda s: (tile(s), 0)), pl.BlockSpec((tt, NS), lambda s: (tile(s), 0)),
                   pl.BlockSpec((tt, W), lambda s: (tile(s), 0))],
        out_shape=[jax.ShapeDtypeStruct((T, NS), BF16), jax.ShapeDtypeStruct((T, NS), BF16),
                   jax.ShapeDtypeStruct((T, W), F32)],
        scratch_shapes=_scan_scratch(tt, NS),
        compiler_params=_cparams())(u, bbd, cbd_re, cbd_im, lam_re, lam_im, coef_re, coef_im)


def _ssm_bwd(name, dy, h_re, h_im, u, bbd, bbdt_re, bbdt_im, cbdt_re, cbdt_im, lam_re, lam_im,
             coef_re, coef_im, Lc, reverse):
    T, W = u.shape
    nslab = W // SLAB_CH
    NS = nslab * SLAB_ST
    tt, nt, nc = _ssm_tiles(T, Lc)
    adj_reverse = not reverse
    if reverse:
        tile = lambda s: jnp.where(s < nt - nc, nc + s, s - (nt - nc))
    else:
        tile = lambda s: nt - 1 - s

    def body(dy_ref, hr_ref, hi_ref, u_ref, b_ref, btr_ref, bti_ref, ctr_ref, cti_ref, lr_ref, li_ref,
             kr_ref, ki_ref, du_ref, dlr_ref, dli_ref, dkr_ref, dki_ref, dbf_ref, dcrf_ref, dcif_ref,
             db_ref, dcr_ref, dci_ref, *tb):
        @pl.when(pl.program_id(0) == 0)
        def _():
            _scan_init(lr_ref[...], -li_ref[...], tb, adj_reverse)
            for ref in (dlr_ref, dli_ref, dkr_ref, dki_ref, db_ref, dcr_ref, dci_ref):
                ref[...] = jnp.zeros_like(ref)

        rows = lax.broadcasted_iota(jnp.int32, (tt, 1), 0)
        far_row = tt - 1 if adj_reverse else 0
        tn_dims = (((0,), (0,)), ((), ()))
        for j in range(nslab):
            lanes = slice(j * SLAB_ST, (j + 1) * SLAB_ST)
            chans = slice(j * SLAB_CH, (j + 1) * SLAB_CH)
            dys, us = dy_ref[:, chans], u_ref[:, chans]
            er = jnp.dot(dys, ctr_ref[j], preferred_element_type=F32)
            ei = -jnp.dot(dys, cti_ref[j], preferred_element_type=F32)
            ar, ai, car, cai = _scan_tile(er, ei, tb, lanes, adj_reverse)
            shift = tt - 1 if adj_reverse else 1
            nr = jnp.where(rows == far_row, car, pltpu.roll(ar, shift, 0))
            ni = jnp.where(rows == far_row, cai, pltpu.roll(ai, shift, 0))
            hrb, hib = hr_ref[:, lanes], hi_ref[:, lanes]
            hr, hi = hrb.astype(F32), hib.astype(F32)
            dlr_ref[:, lanes] += jnp.sum(nr * hr + ni * hi, axis=0, keepdims=True)
            dli_ref[:, lanes] += jnp.sum(ni * hr - nr * hi, axis=0, keepdims=True)
            bu = jnp.dot(us, b_ref[j], preferred_element_type=F32)
            br, bi = bu[:, :SLAB_ST], bu[:, SLAB_ST:]
            dkr_ref[:, lanes] += jnp.sum(ar * br + ai * bi, axis=0, keepdims=True)
            dki_ref[:, lanes] += jnp.sum(ai * br - ar * bi, axis=0, keepdims=True)
            kr, ki = kr_ref[:, lanes], ki_ref[:, lanes]
            dbr = (ar * kr + ai * ki).astype(BF16)
            dbi = (ai * kr - ar * ki).astype(BF16)
            du_ref[:, chans] = (jnp.dot(dbr, btr_ref[j], preferred_element_type=F32)
                                + jnp.dot(dbi, bti_ref[j], preferred_element_type=F32))
            db_ref[j, :, :SLAB_ST] += lax.dot_general(us, dbr, tn_dims, preferred_element_type=F32)
            db_ref[j, :, SLAB_ST:] += lax.dot_general(us, dbi, tn_dims, preferred_element_type=F32)
            dcr_ref[j] += lax.dot_general(hrb, dys, tn_dims, preferred_element_type=F32)
            dci_ref[j] -= lax.dot_general(hib, dys, tn_dims, preferred_element_type=F32)

        @pl.when(pl.program_id(0) == nt - 1)
        def _():
            def iota(shape, axis):
                return lax.broadcasted_iota(jnp.int32, shape, axis)

            sg, ss = SSM_GROUP.bit_length() - 1, SSM_STATE.bit_length() - 1
            b_mask = (iota((SLAB_CH, SLAB_ST), 0) >> sg) == (iota((SLAB_CH, SLAB_ST), 1) >> ss)
            c_mask = (iota((SLAB_ST, SLAB_CH), 0) >> ss) == (iota((SLAB_ST, SLAB_CH), 1) >> sg)
            fold = jnp.where((iota((SLAB_ST, SSM_STATE), 0) & (SSM_STATE - 1)) == iota((SLAB_ST, SSM_STATE), 1),
                             1.0, 0.0).astype(BF16)
            fold_t = jnp.where((iota((SSM_STATE, SLAB_ST), 1) & (SSM_STATE - 1)) == iota((SSM_STATE, SLAB_ST), 0),
                               1.0, 0.0).astype(BF16)

            def exact_dot(a, b, a_is_value):
                terms = _split3(a if a_is_value else b)
                acc = None
                for t in terms:
                    part = jnp.dot(t, b, preferred_element_type=F32) if a_is_value else jnp.dot(a, t, preferred_element_type=F32)
                    acc = part if acc is None else acc + part
                return acc

            for j in range(nslab):
                dbj = db_ref[j]
                dbf_ref[j, :, :SSM_STATE] = exact_dot(jnp.where(b_mask, dbj[:, :SLAB_ST], 0.0), fold, True)
                dbf_ref[j, :, SSM_STATE:] = exact_dot(jnp.where(b_mask, dbj[:, SLAB_ST:], 0.0), fold, True)
                dcrf_ref[j] = exact_dot(fold_t, jnp.where(c_mask, dcr_ref[j], 0.0), False)
                dcif_ref[j] = exact_dot(fold_t, jnp.where(c_mask, dci_ref[j], 0.0), False)

    whole3 = lambda arr: pl.BlockSpec(arr.shape, lambda s: (0, 0, 0))
    vec = pl.BlockSpec((1, NS), lambda s: (0, 0))
    row_w = pl.BlockSpec((tt, W), lambda s: (tile(s), 0))
    row_s = pl.BlockSpec((tt, NS), lambda s: (tile(s), 0))
    dbf = jax.ShapeDtypeStruct((nslab, SLAB_CH, 2 * SSM_STATE), F32)
    dcf = jax.ShapeDtypeStruct((nslab, SSM_STATE, SLAB_CH), F32)
    return pl.pallas_call(
        body, name=name, grid=(nt,),
        in_specs=[row_w, row_s, row_s, row_w, whole3(bbd), whole3(bbdt_re), whole3(bbdt_im), whole3(cbdt_re),
                  whole3(cbdt_im), vec, vec, vec, vec],
        out_specs=[row_w, vec, vec, vec, vec, whole3(dbf), whole3(dcf), whole3(dcf)],
        out_shape=[jax.ShapeDtypeStruct((T, W), F32)] + [jax.ShapeDtypeStruct((1, NS), F32)] * 4 + [dbf, dcf, dcf],
        scratch_shapes=[pltpu.VMEM(bbd.shape, F32), pltpu.VMEM(bbdt_re.shape, F32), pltpu.VMEM(bbdt_re.shape, F32)]
        + _scan_scratch(tt, NS),
        compiler_params=_cparams())(dy, h_re, h_im, u, bbd, bbdt_re, bbdt_im, cbdt_re, cbdt_im,
                                    lam_re, lam_im, coef_re, coef_im)


def _zoh_math(a_re, a_im, log_dt):
    dt = jnp.exp(log_dt)
    mag = jnp.exp(a_re * dt)
    lb_re = mag * jnp.cos(a_im * dt)
    lb_im = mag * jnp.sin(a_im * dt)
    den = a_re * a_re + a_im * a_im
    coef_re = ((lb_re - 1.0) * a_re + lb_im * a_im) / den
    coef_im = (lb_im * a_re - (lb_re - 1.0) * a_im) / den
    return lb_re, lb_im, coef_re, coef_im


def _zoh_fwd(a_re, a_im, log_dt):
    def body(ar, ai, ld, o0, o1, o2, o3):
        for ref, val in zip((o0, o1, o2, o3), _zoh_math(ar[...], ai[...], ld[...])):
            ref[...] = val

    return pl.pallas_call(body, name="zoh_fwd", out_shape=[jax.ShapeDtypeStruct(a_re.shape, F32)] * 4,
                          compiler_params=_cparams())(a_re, a_im, log_dt)


def _zoh_bwd(a_re, a_im, log_dt, cots):
    def body(ar, ai, ld, c0, c1, c2, c3, o0, o1, o2):
        _, vjp = jax.vjp(_zoh_math, ar[...], ai[...], ld[...])
        for ref, val in zip((o0, o1, o2), vjp((c0[...], c1[...], c2[...], c3[...]))):
            ref[...] = val

    return pl.pallas_call(
        body, name="zoh_bwd",
        out_shape=[jax.ShapeDtypeStruct(a_re.shape, F32), jax.ShapeDtypeStruct(a_re.shape, F32),
                   jax.ShapeDtypeStruct(log_dt.shape, F32)],
        compiler_params=_cparams())(a_re, a_im, log_dt, *cots)


def _outer_sum(acts, cots):
    D, N = acts.shape[1], cots.shape[1]
    tm, tn = _div(D, 512), _div(N, 1152)
    dims = (((0,), (0,)), ((), ()))

    def body(a_ref, b_ref, o_ref):
        a = a_ref[...]
        aa = _split3(a * _sigmoid(a))
        bb = _split3(b_ref[...])
        acc = None
        for ia in range(3):
            for ib in range(3 - ia):
                t = lax.dot_general(aa[ia], bb[ib], dims, preferred_element_type=F32)
                acc = t if acc is None else acc + t
        o_ref[...] = acc

    return pl.pallas_call(
        body, name="mod_dw", grid=(D // tm, N // tn),
        in_specs=[pl.BlockSpec((16, tm), lambda i, j: (0, i)), pl.BlockSpec((16, tn), lambda i, j: (0, j))],
        out_specs=pl.BlockSpec((tm, tn), lambda i, j: (i, j)),
        out_shape=jax.ShapeDtypeStruct((D, N), F32), compiler_params=_cparams())(acts, cots)


def _adamw_math(w, g, m, v):
    m = ADAM_B1 * m + (1.0 - ADAM_B1) * g
    v = ADAM_B2 * v + (1.0 - ADAM_B2) * (g * g)
    m_hat = m / (1.0 - ADAM_B1 ** ADAM_STEP)
    v_hat = v / (1.0 - ADAM_B2 ** ADAM_STEP)
    delta = -ADAM_LR * (m_hat / (jnp.sqrt(v_hat) + ADAM_EPS) + ADAM_WD * w)
    return delta, m, v


def _adamw(name, w, m, v, gparts):
    R, C = w.shape[-2:]
    kind = 'row1' if w.ndim == 3 else 'row'
    tr = _div(R, max(8, 262144 // C), mult=8)

    def fn(i, wv, mv, vv, *gs):
        g = gs[0]
        for extra in gs[1:]:
            g = g + extra
        return (g,) + _adamw_math(wv, g, mv, vv)

    return _rowk(name, fn, R, tr, [(w, kind), (m, kind), (v, kind)] + [(g, 'row') for g in gparts],
                 [(w.shape, F32, kind)] * 4)


def _pack(pieces, rows_mult=8):
    flat = jnp.concatenate([p.reshape(-1).astype(F32) for p in pieces])
    unit = rows_mult * PACK_W
    total = -(-flat.shape[0] // unit) * unit
    return jnp.pad(flat, (0, total - flat.shape[0])).reshape(total // PACK_W, PACK_W)


def _unpack(buf, shapes):
    flat = buf.reshape(-1)
    out, off = [], 0
    for s in shapes:
        n = math.prod(s)
        out.append(flat[off:off + n].reshape(s))
        off += n
    return out


def _bd_expand(t):
    S, g, a, b = t.shape
    eye = jnp.eye(g, dtype=t.dtype)
    return (t[:, :, :, None, :] * eye[None, :, None, :, None]).reshape(S, g * a, g * b)


def _rope_tables(L, Lc):
    rows = L // GRID_W
    row_ids = jnp.broadcast_to(jnp.arange(rows)[:, None], (rows, GRID_W)).reshape(-1).astype(F32)
    col_ids = jnp.broadcast_to(jnp.arange(GRID_W)[None, :], (rows, GRID_W)).reshape(-1).astype(F32)
    quarter = HEAD_DIM // 4
    inv_freq = ROPE_THETA ** (-jnp.arange(quarter, dtype=F32) / quarter)
    ang_r = row_ids[:, None] * inv_freq
    ang_c = col_ids[:, None] * inv_freq
    cos = jnp.concatenate([jnp.cos(ang_r), jnp.cos(ang_r), jnp.cos(ang_c), jnp.cos(ang_c)], axis=1)
    sin = jnp.concatenate([-jnp.sin(ang_r), jnp.sin(ang_r), -jnp.sin(ang_c), jnp.sin(ang_c)], axis=1)
    cos = jnp.concatenate([jnp.ones((Lc, HEAD_DIM), F32), cos], axis=0)
    sin = jnp.concatenate([jnp.zeros((Lc, HEAD_DIM), F32), sin], axis=0)
    return cos, sin


def _rot(v):
    lane = lax.broadcasted_iota(jnp.int32, (1, HEAD_DIM), 1)
    first = (lane % (HEAD_DIM // 2)) < (HEAD_DIM // 4)
    return jnp.where(first, pltpu.roll(v, HEAD_DIM - HEAD_DIM // 4, 1), pltpu.roll(v, HEAD_DIM // 4, 1))


def _head_norm(xh, g):
    return xh * lax.rsqrt(jnp.mean(xh * xh, axis=-1, keepdims=True) + NORM_EPS) * g


def _norm_mod(xv, g, sh, sc):
    r = lax.rsqrt(jnp.mean(xv * xv, axis=-1, keepdims=True) + NORM_EPS)
    return (xv * r) * g * (1.0 + sc) + sh


def kernel(x, c, ctx, c_ctx, w_mod, b_mod, norm_g, w_ffn1_gate, w_ffn1_up, w_ffn1_down, w_in, q_norm_g, k_norm_g, ssm_a_re, ssm_a_im, ssm_log_dt, ssm_b_re, ssm_b_im, ssm_c_re, ssm_c_im, ssm_d, w_glu, b_glu, w_br_attn, w_br_ssm, w_out, w_ffn2_gate, w_ffn2_up, w_ffn2_down, loss_target, m_c_ctx, m_w_mod, m_b_mod, m_norm_g, m_w_ffn1_gate, m_w_ffn1_up, m_w_ffn1_down, m_w_in, m_q_norm_g, m_k_norm_g, m_ssm_a_re, m_ssm_a_im, m_ssm_log_dt, m_ssm_b_re, m_ssm_b_im, m_ssm_c_re, m_ssm_c_im, m_ssm_d, m_w_glu, m_b_glu, m_w_br_attn, m_w_br_ssm, m_w_out, m_w_ffn2_gate, m_w_ffn2_up, m_w_ffn2_down, v_c_ctx, v_w_mod, v_b_mod, v_norm_g, v_w_ffn1_gate, v_w_ffn1_up, v_w_ffn1_down, v_w_in, v_q_norm_g, v_k_norm_g, v_ssm_a_re, v_ssm_a_im, v_ssm_log_dt, v_ssm_b_re, v_ssm_b_im, v_ssm_c_re, v_ssm_c_im, v_ssm_d, v_w_glu, v_b_glu, v_w_br_attn, v_w_br_ssm, v_w_out, v_w_ffn2_gate, v_w_ffn2_up, v_w_ffn2_down):
    A = dict(locals())
    xi, yi, ci = _mesh_pos()
    chip = 2 * xi + yi
    me = 4 * xi + 2 * yi + ci
    L, D = x.shape[1], x.shape[2]
    Lc = ctx.shape[1]
    T = L + Lc
    F4 = w_ffn1_gate.shape[2]
    F = N_CHIPS * F4
    W, KV, Dq = D // 2, D // 4, D // 4
    G = W // SSM_GROUP
    P, E = SSM_STATE, SSM_GROUP
    NS = G * P
    nslab = W // SLAB_CH
    tr = min(256, Lc)
    ncr = Lc // tr
    assert L % tr == 0 and Lc % tr == 0 and W % SLAB_CH == 0 and D % (4 * LANES) == 0

    def sel(i, v):
        return v if v.shape[0] == 1 else jnp.where(i < ncr, v[0:1], v[1:2])

    def put(i, v, nrow):
        if nrow == 1:
            return v
        which = (i >= ncr).astype(jnp.int32)
        r2 = lax.broadcasted_iota(jnp.int32, (nrow, 1), 0)
        return jnp.where(r2 == which, jnp.broadcast_to(v, (nrow, v.shape[1])), 0.0)

    ident = lambda accs, rows, vecs, ri: [accs[0]]

    NM = w_mod.shape[2]
    first = jnp.zeros((8, D), F32).at[0].set(c[0]).at[1:4, :Dq].set(norm_g[0])
    g0 = _allgather_small("gather_c", first).reshape(N_CHIPS, 2, 8, D)
    c_all = g0[:, :, 0].reshape(N_DEV, D)
    ng = jnp.transpose(g0[:, 0, 1:4, :Dq], (1, 0, 2)).reshape(3, D)
    acts = jnp.concatenate([c_all, c_ctx[None], jnp.zeros((7, D), F32)], axis=0)
    wm = w_mod[0]
    b_shard = lax.dynamic_slice(b_mod[0], (chip * NM,), (NM,))[None]
    silu_bf = lambda a: (a * _sigmoid(a)).astype(BF16)
    to_bf = lambda b: b.astype(BF16)
    mod_part = _mm("mod_fwd", [(acts, wm, D)], 16, NM, tm=16, tn=_div(NM, 1152),
                   epi=lambda accs, rows, vecs, ri: [accs[0] + vecs[0]], outs=[(F32, False)],
                   vecs=[b_shard], a_pro=silu_bf, b_pro=to_bf)[0]
    mg = _allgather_small("gather_mod", mod_part).reshape(N_CHIPS, 2, 16, NM)[:, 0]
    mod_all = jnp.transpose(mg, (1, 0, 2)).reshape(16, N_CHIPS * NM)
    mod_x = lax.dynamic_slice(mod_all, (me, 0), (1, 9 * D))
    mod_c = jnp.where(jnp.arange(9 * D)[None] < 5 * D, mod_all[8:9], 0.0)
    modv = jnp.concatenate([mod_c, mod_x], axis=0)
    mv = lambda k: modv[:, k * D:(k + 1) * D]
    sh1, sc1, g1, sh2, sc2 = mv(0), mv(1), mv(2), mv(3), mv(4)
    g2, sh3, sc3, g3 = mv(5)[1:2], mv(6)[1:2], mv(7)[1:2], mv(8)[1:2]

    big = ['w_ffn1_gate', 'w_ffn1_up', 'w_ffn1_down', 'w_ffn2_gate', 'w_ffn2_up', 'w_ffn2_down',
           'w_in', 'w_glu', 'w_br_attn', 'w_br_ssm', 'w_out']
    row_sharded = {'w_ffn1_down', 'w_ffn2_down', 'w_glu', 'w_br_attn', 'w_out'}
    groups = [big[0:2], big[2:3], big[6:7], big[7:11], big[3:6]]
    chip_index = jnp.reshape(chip, (1,)).astype(jnp.int32)
    tok, gather_finish = modv, []
    pin = c
    for gi, names in enumerate(groups):
        tok, fin = _gather_split("gather_w%d" % gi, [_cast_slot("cast_" + n, A[n], chip_index, pin) for n in names], tok)
        gather_finish.append(fin)
        pin = tok
    ng = ng + tok[0:1, 0:1]
    Wt = {}

    def register(names, full):
        for n, gw in zip(names, full):
            Wt[n] = gw.reshape(N_CHIPS * gw.shape[1], gw.shape[2]) if n in row_sharded else gw

    def weights_ready(gi, after_work):
        _, lands = gather_finish[gi](after_work)
        register(groups[gi], _gather_finish("gather_w%d_pass" % gi, lands))

    def weights_pass(gi, after_work):
        _, lands = gather_finish[gi](after_work)
        tok_, fin_ = _pass_split("gather_w%d_pass" % gi, lands, after_work)
        return tok_, lambda later: register(groups[gi], fin_(later)[1])

    a_re2, a_im2 = ssm_a_re[0].reshape(2 * G, P), ssm_a_im[0].reshape(2 * G, P)
    ldt2 = ssm_log_dt[0].reshape(2 * G, 1)
    zoh = _zoh_fwd(a_re2, a_im2, ldt2)
    lam_re, lam_im, coef_re, coef_im = [[z[d * G:(d + 1) * G].reshape(1, NS) for d in range(2)] for z in zoh]
    bd_b = lambda b: _bd_expand(jnp.transpose(b, (0, 2, 1)).reshape(nslab, SLAB_GROUPS, E, P))
    bd_c = lambda cc: _bd_expand(jnp.transpose(cc, (0, 2, 1)).reshape(nslab, SLAB_GROUPS, P, E))
    bbd, bbdt_re, bbdt_im, cbd_re, cbd_im, cbdt_re, cbdt_im = [], [], [], [], [], [], []
    for d in range(2):
        br_, bi_ = bd_b(ssm_b_re[0, d]).astype(BF16), bd_b(ssm_b_im[0, d]).astype(BF16)
        cr_, ci_ = bd_c(ssm_c_re[0, d]).astype(BF16), bd_c(ssm_c_im[0, d]).astype(BF16)
        bbd.append(jnp.concatenate([br_, bi_], axis=2))
        bbdt_re.append(jnp.transpose(br_, (0, 2, 1)))
        bbdt_im.append(jnp.transpose(bi_, (0, 2, 1)))
        cbd_re.append(cr_)
        cbd_im.append(ci_)
        cbdt_re.append(jnp.transpose(cr_, (0, 2, 1)))
        cbdt_im.append(jnp.transpose(ci_, (0, 2, 1)))
    cos_t, sin_t = _rope_tables(L, Lc)
    qg, kg = q_norm_g, k_norm_g
    small = ['c_ctx', 'b_mod', 'norm_g', 'q_norm_g', 'k_norm_g', 'ssm_a_re', 'ssm_a_im', 'ssm_log_dt', 'ssm_b_re',
             'ssm_b_im', 'ssm_c_re', 'ssm_c_im', 'ssm_d', 'b_glu']
    packs_wmv = [_pack([A[pre + n] for n in small]) for pre in ('', 'm_', 'v_')]
    prepared = packs_wmv + [cos_t, sin_t, coef_im[0], coef_im[1]] + [
        t[d][0] for t in (bbd, bbdt_re, bbdt_im, cbd_re, cbd_im, cbdt_re, cbdt_im) for d in range(2)]
    weights_ready(0, tok + sum(t[0:1, 0:1].astype(F32) for t in prepared))

    def norm_mod(name, xv, g, sh, sc):
        rows = xv.shape[0]
        return _rowk(name, lambda i, xt, gt, sht, sct: [_norm_mod(xt, gt, sel(i, sht), sel(i, sct))],
                     rows, tr, [(xv, 'row'), (g, 'vec'), (sh, 'vec'), (sc, 'vec')], [((rows, D), BF16, 'row')])[0]

    def swiglu_epi(accs, rows, vecs, ri):
        a_, b_ = accs
        return [a_, b_, a_ * _sigmoid(a_) * b_]

    def res_epi(coef):
        def epi(accs, rows, vecs, ri):
            gate = vecs[0]
            if gate.shape[0] == 2:
                gate = jnp.where(ri < Lc, gate[0:1], gate[1:2])
            return [accs[0], rows[0] + (coef * gate) * accs[0]]
        return epi

    def ffn_fwd(tag, h, xres, gate, down_ready=None):
        rows = h.shape[0]
        a_, b_, s_ = _mm(tag + "_up", [(h, Wt['w_' + tag + '_gate'], D), (h, Wt['w_' + tag + '_up'], D)], rows, F,
                         tm=_div(rows, 256), tn=F4, epi=swiglu_epi, outs=[(F32, False), (F32, False), (BF16, False)])
        if down_ready is not None:
            down_ready(s_)
        f_, xo = _mm(tag + "_down", [(s_, Wt['w_' + tag + '_down'], F)], rows, D, tm=_div(rows, 384),
                     tn=_div(D, 512), epi=res_epi(0.5), outs=[(F32, False), (F32, False)],
                     rows=[(xres, 0, 0)], vecs=[gate])
        return a_, b_, s_, f_, xo

    xc = jnp.concatenate([ctx[0], x[0]], axis=0)
    h1 = norm_mod("norm1", xc, ng[0:1], sh1, sc1)
    a1, b1, s1, f1, x1 = ffn_fwd("ffn1", h1, xc, g1, down_ready=lambda s_: weights_ready(1, s_))
    weights_ready(2, x1)
    h2 = norm_mod("norm2", x1, ng[1:2], sh2, sc2)
    proj = _mm("in_proj", [(h2, Wt['w_in'], D)], T, 4 * D, tm=_div(T, 768), tn=_div(D, 1024), epi=ident,
               outs=[(F32, False)])[0]
    nh, nkvh = D // HEAD_DIM, KV // HEAD_DIM

    def prep_fn(i, kt, vt, ut, qt, qgt, kgt, ct, st):
        qs = [_head_norm(qt[:, h * HEAD_DIM:(h + 1) * HEAD_DIM], qgt) for h in range(nh)]
        ks = [_head_norm(kt[:, h * HEAD_DIM:(h + 1) * HEAD_DIM], kgt) for h in range(nkvh)]
        qs = [v * ct + _rot(v) * st for v in qs]
        ks = [v * ct + _rot(v) * st for v in ks]
        return [jnp.concatenate(qs, axis=1), jnp.concatenate(ks, axis=1), vt, ut]

    qr, kr, vb, ub = _rowk(
        "qk_prep", prep_fn, T, tr,
        [(proj, ('col', KV, 0)), (proj, ('col', KV, 1)), (proj, ('col', W, 1)), (proj, ('col', D, 1)),
         (qg, 'vec'), (kg, 'vec'), (cos_t, 'row'), (sin_t, 'row')],
        [((T, D), BF16, 'row'), ((T, KV), BF16, 'row'), ((T, KV), BF16, 'row'), ((T, W), BF16, 'row')])
    _, mixer_weights = weights_pass(3, qr)
    attn = _attn_fwd(qr, kr, vb, L, Lc, D)
    hs_re, hs_im, ys = [], [], []
    lam_in = lam_re[0]
    for d in range(2):
        hr_, hi_, y_ = _ssm_fwd("ssm_fwd%d" % d, ub, bbd[d], cbd_re[d], cbd_im[d], lam_in, lam_im[d],
                                coef_re[d], coef_im[d], Lc, reverse=bool(d))
        hs_re.append(hr_)
        hs_im.append(hi_)
        ys.append(y_)
        if d == 0:
            tok_p4, ffn2_weights = weights_pass(4, y_)
            lam_in = lam_re[1] + tok_p4[0:1, 0:1]
    mixer_weights(ys[1])

    def ssm_out_fn(i, y0, y1, ut, dt):
        pre = dt * ut + y0 + y1
        yg_ = _gelu(pre)
        return [pre, yg_, yg_]

    ssm_pre, yg, ygb = _rowk(
        "ssm_out", ssm_out_fn, L, tr,
        [(ys[0], 'orow'), (ys[1], 'orow'), (proj, ('ocol', W, 1)), (ssm_d, 'vec')],
        [((L, W), F32, 'row'), ((L, W), F32, 'row'), ((L, W), BF16, 'row')], nc=ncr)

    def glu_epi(accs, rows, vecs, ri):
        z_ = accs[0] + vecs[0]
        return [z_, rows[0] * _sigmoid(z_)]

    zglu, y2 = _mm("glu", [(ygb, Wt['w_glu'], W)], L, W, tm=_div(L, 512), tn=_div(W, 512), epi=glu_epi,
                   outs=[(F32, False), (BF16, False)], rows=[(yg, 0, 0)], vecs=[b_glu])
    tnm = _div(Dq, 512)

    def merge_epi(accs, rows, vecs, ri):
        ga, gs = _sigmoid(rows[0]), _sigmoid(rows[1])
        return [accs[0], accs[1], ga * accs[0] + gs * accs[1]]

    ba, bs, merged = _mm("merge", [(attn, Wt['w_br_attn'], D), (y2, Wt['w_br_ssm'], W)], L, D, tm=tr, tn=tnm,
                         epi=merge_epi, outs=[(F32, False), (F32, False), (BF16, False)],
                         rows=[(proj, ncr, 2 * D // tnm), (proj, ncr, 3 * D // tnm)])
    mix, x2 = _mm("out_proj", [(merged, Wt['w_out'], D)], L, D, tm=tr, tn=_div(D, 1024), epi=res_epi(1.0),
                  outs=[(F32, False), (F32, False)], rows=[(x1, ncr, 0)], vecs=[g2])
    ffn2_weights(x2)
    h3 = norm_mod("norm3", x2, ng[2:3], sh3, sc3)
    a3, b3, s3, f3, x3 = ffn_fwd("ffn2", h3, x2, g3)

    def loss_fn(i, yt, tt_):
        diff = yt - tt_
        return [diff * (1.0 / D), jnp.sum(diff * diff, axis=0, keepdims=True)]

    dy, sq = _rowk("loss", loss_fn, L, tr, [(x3, 'row'), (loss_target[0], 'row')],
                   [((L, D), F32, 'row'), ((1, D), F32, 'acc')])
    loss = lax.psum(0.5 * jnp.sum(sq) / D, ("x", "y", "c"))

    def res_bwd(name, dxo, f_, gate, coef):
        rows, nrow = dxo.shape[0], gate.shape[0]

        def fn(i, dt, ft, gt):
            return [(coef * sel(i, gt)) * dt, put(i, jnp.sum(dt * ft, axis=0, keepdims=True) * coef, nrow)]

        return _rowk(name, fn, rows, tr, [(dxo, 'row'), (f_, 'row'), (gate, 'vec')],
                     [((rows, D), BF16, 'row'), ((nrow, D), F32, 'acc')])

    def swiglu_bwd_epi(accs, rows, vecs, ri):
        ds_, a_, b_ = accs[0], rows[0], rows[1]
        sg = _sigmoid(a_)
        return [ds_ * b_ * (sg * (1.0 + a_ * (1.0 - sg))), ds_ * (a_ * sg)]

    def norm_mod_bwd(name, xv, g, sh, sc, dh, dres, dres_kind):
        rows, nrow = xv.shape[0], sh.shape[0]

        def fn(i, xt, gt, sht, sct, dht, rest):
            _, vjp = jax.vjp(_norm_mod, xt, gt, sel(i, sht), sel(i, sct))
            dx_, dg_, dsh_, dsc_ = vjp(dht)
            dx_ = dx_ + (jnp.where(i >= ncr, rest, 0.0) if dres_kind == 'xrow' else rest)
            return [dx_, dg_, put(i, dsh_, nrow), put(i, dsc_, nrow)]

        return _rowk(name, fn, rows, tr,
                     [(xv, 'row'), (g, 'vec'), (sh, 'vec'), (sc, 'vec'), (dh, 'row'), (dres, dres_kind)],
                     [((rows, D), F32, 'row'), ((1, D), F32, 'acc'), ((nrow, D), F32, 'acc'), ((nrow, D), F32, 'acc')],
                     nc=ncr)

    def ffn_bwd(tag, dxo, h, a_, b_, s_, f_, gate, wg, wu, wd, on_dwd=None):
        rows = dxo.shape[0]
        df, dgate = res_bwd(tag + "_dres", dxo, f_, gate, 0.5)
        dwd = _mm(tag + "_dwd", [(s_, df, rows)], F, D, tm=_div(F, 512), tn=_div(D, 1024), ta=True, epi=ident,
                  outs=[(BF16, False)])[0].reshape(N_CHIPS, F4, D)
        if on_dwd is not None:
            on_dwd(dwd)
        da, db = _mm(tag + "_dact", [(df, wd, D)], rows, F, tm=_div(rows, 384), tn=F4, tb=True, epi=swiglu_bwd_epi,
                     outs=[(BF16, False), (BF16, False)], rows=[(a_, 0, 0), (b_, 0, 0)])
        dwg = _mm(tag + "_dwg", [(h, da, rows)], D, F, tm=_div(D, 512), tn=F4, ta=True, epi=ident,
                  outs=[(BF16, True)])[0]
        dwu = _mm(tag + "_dwu", [(h, db, rows)], D, F, tm=_div(D, 512), tn=F4, ta=True, epi=ident,
                  outs=[(BF16, True)])[0]
        dh = _mm(tag + "_dh", [(da, wg, F), (db, wu, F)], rows, D, tm=_div(rows, 768), tn=_div(D, 1024), nk=N_CHIPS,
                 tb=True, epi=ident, outs=[(F32, False)], summed=True)[0]
        return dh, dgate, dwg, dwu, dwd

    dh3, dg3, dwg2, dwu2, dwd2 = ffn_bwd("ffn2", dy, h3, a3, b3, s3, f3, g3, Wt['w_ffn2_gate'], Wt['w_ffn2_up'],
                                         Wt['w_ffn2_down'])
    tok_r1, scatter_fin1 = _scatter_split("scatter_ffn2", [dwg2, dwu2, dwd2], dg3)
    dx2, dng3, dsh3, dsc3 = norm_mod_bwd("norm3_bwd", x2, ng[2:3], sh3, sc3, dh3, dy, 'row')
    dmix, dg2 = res_bwd("mix_dres", dx2, mix, g2 + tok_r1[0:1, 0:1], 1.0)

    def dmerge_epi(accs, rows, vecs, ri):
        dm_, ba_, bs_ = accs[0], rows[0], rows[1]
        ga, gs = _sigmoid(rows[2]), _sigmoid(rows[3])
        return [dm_ * ga, dm_ * gs, dm_ * ba_ * ga * (1.0 - ga), dm_ * bs_ * gs * (1.0 - gs)]

    tnd = _div(D, 1024)
    dba, dbs, dga, dgs = _mm("dmerge", [(dmix, Wt['w_out'], D)], L, D, tm=tr, tn=tnd, tb=True, epi=dmerge_epi,
                             outs=[(BF16, False)] * 4,
                             rows=[(ba, 0, 0), (bs, 0, 0), (proj, ncr, 2 * D // tnd), (proj, ncr, 3 * D // tnd)])
    dwout = _mm("dw_out", [(merged, dmix, L)], D, D, tm=_div(D, 512), tn=_div(D, 1024), ta=True, epi=ident,
                outs=[(BF16, False)])[0].reshape(N_CHIPS, Dq, D)
    dattn = _mm("dattn", [(dba, Wt['w_br_attn'], D)], L, D, tm=_div(L, 512), tn=_div(D, 1024), tb=True, epi=ident,
                outs=[(BF16, False)])[0]
    dwba = _mm("dw_br_attn", [(attn, dba, L)], D, D, tm=_div(D, 512), tn=_div(D, 1024), ta=True, epi=ident,
               outs=[(BF16, False)])[0].reshape(N_CHIPS, Dq, D)
    dy2 = _mm("dy2", [(dbs, Wt['w_br_ssm'], D)], L, W, tm=_div(L, 512), tn=_div(W, 1024), nk=N_CHIPS, tb=True,
              epi=ident, outs=[(F32, False)])[0]
    dwbs = _mm("dw_br_ssm", [(y2, dbs, L)], W, D, tm=_div(W, 512), tn=_div(Dq, 512), ta=True, epi=ident,
               outs=[(BF16, True)])[0]

    def glu_bwd_fn(i, d2, ygt, zt):
        sz = _sigmoid(zt)
        dz_ = d2 * ygt * sz * (1.0 - sz)
        return [dz_, d2 * sz, jnp.sum(dz_, axis=0, keepdims=True)]

    dz, dyd, dbglu = _rowk("glu_bwd", glu_bwd_fn, L, tr, [(dy2, 'row'), (yg, 'row'), (zglu, 'row')],
                           [((L, W), BF16, 'row'), ((L, W), F32, 'row'), ((1, W), F32, 'acc')])

    def dssm_epi(accs, rows, vecs, ri):
        _, vjp = jax.vjp(_gelu, rows[1])
        ds_ = vjp(accs[0] + rows[0])[0]
        return [ds_, ds_]

    dssm, dssm_b = _mm("dssm", [(dz, Wt['w_glu'], W)], L, W, tm=_div(L, 512), tn=_div(W, 512), tb=True, epi=dssm_epi,
                       outs=[(F32, False), (BF16, False)], rows=[(dyd, 0, 0), (ssm_pre, 0, 0)])
    dwglu = _mm("dw_glu", [(ygb, dz, L)], W, W, tm=_div(W, 512), tn=_div(W, 1024), ta=True, epi=ident,
                outs=[(BF16, False)])[0].reshape(N_CHIPS, W // N_CHIPS, W)
    tok_r2a, scatter_fin2a = _scatter_split("scatter_mix", [dwglu, dwba, dwbs, dwout], dbglu)
    dssm_full = jnp.concatenate([jnp.zeros((Lc, W), BF16), dssm_b], axis=0)
    dus, dlam_re, dlam_im, dcoef_re, dcoef_im, dbf, dcf_re, dcf_im = [], [], [], [], [], [], [], []
    for d in range(2):
        r = _ssm_bwd("ssm_bwd%d" % d, dssm_full, hs_re[d], hs_im[d], ub, bbd[d], bbdt_re[d], bbdt_im[d],
                     cbdt_re[d], cbdt_im[d], lam_re[d] + tok_r2a[0:1, 0:1], lam_im[d], coef_re[d], coef_im[d], Lc,
                     reverse=bool(d))
        for lst, val in zip((dus, dlam_re, dlam_im, dcoef_re, dcoef_im, dbf, dcf_re, dcf_im), r):
            lst.append(val)
    dqr, dkr, dvf = _attn_bwd(qr, kr, vb, dattn, L, Lc, D)

    def prep_bwd_fn(i, qt, kt, ut, dqt, dkt, dvt, du0, du1, dst, dt, qgt, kgt, ct, st):
        live = i >= ncr
        dqt = jnp.where(live, dqt, 0.0)
        dst = jnp.where(live, dst, 0.0)
        dqs, dks = [], []
        dqg_ = jnp.zeros((1, HEAD_DIM), F32)
        dkg_ = jnp.zeros((1, HEAD_DIM), F32)
        for h in range(nh):
            hl = slice(h * HEAD_DIM, (h + 1) * HEAD_DIM)
            dn = dqt[:, hl] * ct + _rot(dqt[:, hl] * st)
            _, vjp = jax.vjp(_head_norm, qt[:, hl], qgt)
            dxh, dgh = vjp(dn)
            dqs.append(dxh)
            dqg_ = dqg_ + dgh
        for h in range(nkvh):
            hl = slice(h * HEAD_DIM, (h + 1) * HEAD_DIM)
            dn = dkt[:, hl] * ct + _rot(dkt[:, hl] * st)
            _, vjp = jax.vjp(_head_norm, kt[:, hl], kgt)
            dxh, dgh = vjp(dn)
            dks.append(dxh)
            dkg_ = dkg_ + dgh
        du_ = du0 + du1 + dst * dt
        return [jnp.concatenate(dqs, axis=1), jnp.concatenate(dks, axis=1), dvt, du_, dqg_, dkg_,
                jnp.sum(dst * ut, axis=0, keepdims=True)]

    dq_b, dk_b, dv_b, du_b, dqg, dkg, dssd = _rowk(
        "qk_prep_bwd", prep_bwd_fn, T, tr,
        [(proj, ('col', D, 1)), (proj, ('col', KV, 0)), (proj, ('col', W, 1)), (dqr, 'xrow'), (dkr, 'row'),
         (dvf, 'row'), (dus[0], 'row'), (dus[1], 'row'), (dssm, 'xrow'), (ssm_d, 'vec'), (qg, 'vec'), (kg, 'vec'),
         (cos_t, 'row'), (sin_t, 'row')],
        [((T, D), BF16, 'row'), ((T, KV), BF16, 'row'), ((T, KV), BF16, 'row'), ((T, W), BF16, 'row'),
         ((1, HEAD_DIM), F32, 'acc'), ((1, HEAD_DIM), F32, 'acc'), ((1, W), F32, 'acc')], nc=ncr)
    dgate = jnp.concatenate([jnp.zeros((Lc, 2 * D), BF16), jnp.concatenate([dga, dgs], axis=1)], axis=0)
    dproj = jnp.concatenate([dk_b, dv_b, du_b, dq_b, dgate], axis=1)
    dh2 = _mm("in_proj_dx", [(dproj, Wt['w_in'], 4 * D)], T, D, tm=_div(T, 768), tn=_div(D, 1024), nk=N_CHIPS, tb=True,
              epi=ident, outs=[(F32, False)])[0]
    dwin = _mm("in_proj_dw", [(h2, dproj, T)], D, 4 * D, tm=_div(D, 512), tn=_div(D, 1024), ta=True, epi=ident,
               outs=[(BF16, True)])[0]
    tok_r2, scatter_fin2 = _scatter_split("scatter_w_in", [dwin], dqg)
    dx1, dng2, dsh2, dsc2 = norm_mod_bwd("norm2_bwd", x1, ng[1:2] + tok_r2[0:1, 0:1], sh2, sc2, dh2, dx2, 'xrow')
    early = {}

    def start_down(dwd):
        early['tok'], early['fin'] = _scatter_split("scatter_ffn1_down", [dwd], dg2)

    dh1, dg1, dwg1, dwu1, dwd1 = ffn_bwd("ffn1", dx1, h1, a1, b1, s1, f1, g1, Wt['w_ffn1_gate'], Wt['w_ffn1_up'],
                                         Wt['w_ffn1_down'], on_dwd=start_down)
    dx0, dng1, dsh1, dsc1 = norm_mod_bwd("norm1_bwd", xc, ng[0:1] + early['tok'][0:1, 0:1], sh1, sc1, dh1, dx1, 'row')
    grad_x = dx0[Lc:][None]

    zD = jnp.zeros((1, D), F32)
    dmod_x = jnp.concatenate([dsh1[1:2], dsc1[1:2], dg1[1:2], dsh2[1:2], dsc2[1:2], dg2, dsh3, dsc3, dg3], axis=1)
    dmod_c = jnp.concatenate([dsh1[0:1], dsc1[0:1], dg1[0:1], dsh2[0:1], dsc2[0:1], zD, zD, zD, zD], axis=1)
    pieces = [dmod_x, dmod_c, dng1, dng2, dng3, dqg, dkg] + dlam_re + dlam_im + dcoef_re + dcoef_im \
        + dbf + dcf_re + dcf_im + [dssd, dbglu]
    shapes = [p_.shape for p_ in pieces]
    pack = _pack(pieces)
    RP = pack.shape[0]
    tok_small, small_gathered = _allgather_split("gather_small", pack, me, dng1)
    tok_r3, scatter_fin3 = _scatter_split("scatter_ffn1_up", [dwg1, dwu1], tok_small)
    results = {}

    def sum_group(tag, names, fin, after_work):
        sent, landed = fin(after_work)
        plane = [_sum_plane("sum_" + n, g_, rb, chip_index) for n, g_, rb in zip(names, sent, landed)]
        tok_, swapped = _swap_split("swap_" + tag, plane, plane[0])
        return tok_, (names, swapped)

    def update_group(group, after_work):
        names, swapped = group
        mine, theirs = swapped(after_work)
        for n, m_, t_ in zip(names, mine, theirs):
            results[n] = _adamw("adamw_" + n, A[n], A['m_' + n], A['v_' + n], [m_, t_])

    tok_a, grp_ffn2 = sum_group("ffn2", big[3:6], scatter_fin1, tok_r3)
    tok_b, grp_mix = sum_group("mix", big[7:11], scatter_fin2a, tok_a)
    tok_c, grp_w_in = sum_group("w_in", big[6:7], scatter_fin2, tok_b)
    update_group(grp_ffn2, tok_c)
    tok_d, grp_down = sum_group("ffn1_down", big[2:3], early['fin'], results['w_ffn2_down'][0])
    update_group(grp_mix, tok_d)
    update_group(grp_w_in, results['w_out'][0])
    update_group(grp_down, results['w_in'][0])
    allp = small_gathered(results['w_ffn1_down'][0])
    head_rows = -(-18 * D // PACK_W)
    head = allp[:, :head_rows].reshape(N_DEV, head_rows * PACK_W)
    dmx_all = head[:, :9 * D]

    def sum_rows_fn(i, t):
        s_ = t[0:1]
        for k in range(1, N_DEV):
            s_ = s_ + t[k:k + 1]
        return [s_]

    dmc_sum = _rowk("sum_dmod_c", sum_rows_fn, 1, 1, [(head[:, 9 * D:18 * D], 'vec')], [((1, 9 * D), F32, 'row')])[0]
    cots = jnp.concatenate([dmx_all, dmc_sum, jnp.zeros((7, 9 * D), F32)], axis=0)
    cots_sh = lax.dynamic_slice(cots, (0, chip * NM), (16, NM))
    part = _mm("cctx_part", [(cots_sh[8:16], wm, NM)], 8, D, tm=8, tn=_div(D, 1024), nk=NM // _div(NM, 1152), tb=True,
               epi=ident, outs=[(F32, False)], a_pro=to_bf, b_pro=to_bf)[0]
    _, cctx_gathered = _allgather_split("gather_cctx", part, me, part)

    def sum_dev_fn(i, t):
        s_ = t[0]
        for k in range(1, N_DEV):
            s_ = s_ + t[k]
        return [s_]

    tot = _rowk("sum_small", sum_dev_fn, RP, 8, [(allp, 'row3')], [((RP, PACK_W), F32, 'row')])[0]
    (t_dmod_x, t_dmod_c, t_ng1, t_ng2, t_ng3, t_qg, t_kg, t_lr0, t_lr1, t_li0, t_li1, t_kr0, t_kr1, t_ki0, t_ki1,
     t_dbf0, t_dbf1, t_dcr0, t_dcr1, t_dci0, t_dci1, t_d, t_bglu) = _unpack(tot, shapes)
    b_grad = lambda t, lo: jnp.transpose(t[:, :, lo:lo + P].reshape(G, E, P), (0, 2, 1))
    c_grad = lambda t: jnp.transpose(t.reshape(nslab, P, SLAB_GROUPS, E), (0, 2, 3, 1)).reshape(G, E, P)
    cat2 = lambda u0, u1: jnp.concatenate([u0.reshape(G, P), u1.reshape(G, P)], axis=0)
    g_are, g_aim, g_ldt = _zoh_bwd(a_re2, a_im2, ldt2, [cat2(t_lr0, t_lr1), cat2(t_li0, t_li1),
                                                         cat2(t_kr0, t_kr1), cat2(t_ki0, t_ki1)])
    g_bmod = _rowk("bmod_grad", lambda i, u0, u1: [u0 + u1], 1, 1, [(t_dmod_x, 'row'), (t_dmod_c, 'row')],
                   [((1, 9 * D), F32, 'row')])[0]
    g_wmod = _outer_sum(acts, cots_sh)
    results['w_mod'] = _adamw("adamw_w_mod", w_mod, m_w_mod, v_w_mod, [g_wmod])
    tok_e, grp_up = sum_group("ffn1_up", big[0:2], scatter_fin3, results['w_mod'][0])
    parts = cctx_gathered(tok_e).reshape(N_CHIPS, 2, 8, D)[:, 0, 0]

    def cctx_fn(i, pt, ct):
        ds_ = ((pt[0:1] + pt[1:2]) + pt[2:3]) + pt[3:4]
        _, vjp = jax.vjp(lambda v: v * _sigmoid(v), ct)
        return [vjp(ds_)[0]]

    g_cctx = _rowk("cctx_grad", cctx_fn, 1, 1, [(parts, 'vec'), (c_ctx[None], 'row')], [((1, D), F32, 'row')])[0]

    ng_full =jnp.concatenate([t_ng1, t_ng2, t_ng3], axis=0)
    gsmall = {
        'c_ctx': g_cctx, 'b_mod': g_bmod, 'norm_g': lax.dynamic_slice(ng_full, (0, chip * Dq), (3, Dq)),
        'q_norm_g': t_qg, 'k_norm_g': t_kg, 'ssm_a_re': g_are, 'ssm_a_im': g_aim, 'ssm_log_dt': g_ldt,
        'ssm_b_re': jnp.stack([b_grad(t_dbf0, 0), b_grad(t_dbf1, 0)]),
        'ssm_b_im': jnp.stack([b_grad(t_dbf0, P), b_grad(t_dbf1, P)]),
        'ssm_c_re': jnp.stack([c_grad(t_dcr0), c_grad(t_dcr1)]), 'ssm_c_im': jnp.stack([c_grad(t_dci0), c_grad(t_dci1)]),
        'ssm_d': t_d, 'b_glu': t_bglu}
    sshapes = [A[n].shape for n in small]
    sres = _adamw("adamw_small", packs_wmv[0], packs_wmv[1], packs_wmv[2], [_pack([gsmall[n] for n in small])])
    update_group(grp_up, sres[0])
    sres = [_unpack(b_, sshapes) for b_ in sres]
    for k, n in enumerate(small):
        results[n] = tuple(sres[q][k] for q in range(4))

    order = ['c_ctx', 'w_mod', 'b_mod', 'norm_g', 'w_ffn1_gate', 'w_ffn1_up', 'w_ffn1_down', 'w_in', 'q_norm_g',
             'k_norm_g', 'ssm_a_re', 'ssm_a_im', 'ssm_log_dt', 'ssm_b_re', 'ssm_b_im', 'ssm_c_re', 'ssm_c_im',
             'ssm_d', 'w_glu', 'b_glu', 'w_br_attn', 'w_br_ssm', 'w_out', 'w_ffn2_gate', 'w_ffn2_up', 'w_ffn2_down']
    outs = [loss, grad_x]
    for q in range(4):
        outs += [results[n][q].reshape(A[n].shape) for n in order]
    return tuple(outs)
```

```python
import math

import jax
import jax.numpy as jnp
from jax import lax
from jax.experimental import pallas as pl
from jax.experimental.pallas import tpu as pltpu

F32 = jnp.float32
BF16 = jnp.bfloat16
MESH = pl.DeviceIdType.MESH

NORM_EPS = 1e-6
ROPE_THETA = 10000.0
GRID_W = 64
HEAD_DIM = 128
Q_PER_KV = 4
SSM_GROUP = 16
SSM_STATE = 64
ADAM_LR = 0.001
ADAM_B1 = 0.9
ADAM_B2 = 0.999
ADAM_EPS = 1e-08
ADAM_WD = 0.01
ADAM_STEP = 10

N_CHIPS = 4
N_DEV = 8
LANES = 128
SLAB_CH = 128
SLAB_GROUPS = SLAB_CH // SSM_GROUP
SLAB_ST = SLAB_GROUPS * SSM_STATE
VMEM_LIMIT_BYTES = 56 * 1024 * 1024
PACK_W = 1024


def _cparams(**kw):
    return pltpu.CompilerParams(vmem_limit_bytes=VMEM_LIMIT_BYTES, **kw)


def _div(n, pref, mult=LANES):
    t = (min(pref, n) // mult) * mult
    while t >= mult:
        if n % t == 0:
            return t
        t -= mult
    return n


def _sigmoid(x):
    return jax.nn.sigmoid(x)


def _gelu(x):
    return x * (0.5 * (1.0 + jnp.tanh(math.sqrt(2.0 / math.pi) * (x + 0.044715 * (x * x * x)))))


def _rowk(name, fn, nrows, tr, ins, outs, nc=0):
    nt = nrows // tr
    in_specs, arrays = [], []
    for arr, kind in ins:
        arrays.append(arr)
        if kind == 'row':
            in_specs.append(pl.BlockSpec((tr, arr.shape[1]), lambda i: (i, 0)))
        elif kind == 'xrow':
            in_specs.append(pl.BlockSpec((tr, arr.shape[1]), lambda i: (jnp.maximum(i - nc, 0), 0)))
        elif kind == 'orow':
            in_specs.append(pl.BlockSpec((tr, arr.shape[1]), lambda i: (i + nc, 0)))
        elif kind == 'vec':
            in_specs.append(pl.BlockSpec(arr.shape, lambda i, nd=arr.ndim: (0,) * nd))
        elif kind == 'row3':
            in_specs.append(pl.BlockSpec((arr.shape[0], tr, arr.shape[2]), lambda i: (0, i, 0)))
        elif kind == 'row1':
            in_specs.append(pl.BlockSpec((None, tr, arr.shape[2]), lambda i: (0, i, 0)))
        elif kind[0] == 'ocol':
            _, width, blk = kind
            in_specs.append(pl.BlockSpec((tr, width), lambda i, blk=blk: (i + nc, blk)))
        else:
            _, width, blk = kind
            in_specs.append(pl.BlockSpec((tr, width), lambda i, blk=blk: (i, blk)))
    out_shape, out_specs = [], []
    for shape, dtype, kind in outs:
        out_shape.append(jax.ShapeDtypeStruct(shape, dtype))
        if kind == 'row':
            out_specs.append(pl.BlockSpec((tr, shape[1]), lambda i: (i, 0)))
        elif kind == 'row1':
            out_specs.append(pl.BlockSpec((None, tr, shape[2]), lambda i: (0, i, 0)))
        else:
            out_specs.append(pl.BlockSpec(shape, lambda i, nd=len(shape): (0,) * nd))
    nin = len(ins)

    def body(*refs):
        i = pl.program_id(0)
        res = fn(i, *[r[...] for r in refs[:nin]])
        for (shape, dtype, kind), ref, val in zip(outs, refs[nin:], res):
            if kind in ('row', 'row1'):
                ref[...] = val.astype(dtype)
            else:
                @pl.when(i == 0)
                def _():
                    ref[...] = val.astype(dtype)

                @pl.when(i > 0)
                def _():
                    ref[...] += val.astype(dtype)

    return pl.pallas_call(body, name=name, grid=(nt,), in_specs=in_specs, out_specs=out_specs,
                          out_shape=out_shape, compiler_params=_cparams())(*arrays)


def _mm(name, pairs, M, N, *, tm, tn, nk=1, epi, outs, ta=False, tb=False, rows=(), vecs=(),
        a_pro=None, b_pro=None, n_outer=True, summed=False):
    nm, nn = M // tm, N // tn
    npair = len(pairs)

    def idx(f):
        if n_outer:
            return lambda j, i, k: f(i, j, k)
        return lambda i, j, k: f(i, j, k)

    in_specs, args = [], []
    for a, b, K in pairs:
        tk = K // nk
        if ta:
            in_specs.append(pl.BlockSpec((tk, tm), idx(lambda i, j, k: (k, i))))
        else:
            in_specs.append(pl.BlockSpec((tm, tk), idx(lambda i, j, k: (i, k))))
        args.append(a)
        if b.ndim == 3:
            if tb:
                per = b.shape[2] // tk
                in_specs.append(pl.BlockSpec((None, tn, tk), idx(lambda i, j, k, per=per: (k // per, j, k % per))))
            else:
                per = b.shape[2] // tn
                in_specs.append(pl.BlockSpec((None, tk, tn), idx(lambda i, j, k, per=per: (j // per, k, j % per))))
        elif tb:
            in_specs.append(pl.BlockSpec((tn, tk), idx(lambda i, j, k: (j, k))))
        else:
            in_specs.append(pl.BlockSpec((tk, tn), idx(lambda i, j, k: (k, j))))
        args.append(b)
    for arr, ro, co in rows:
        in_specs.append(pl.BlockSpec((tm, tn), idx(lambda i, j, k, ro=ro, co=co: (i + ro, j + co))))
        args.append(arr)
    for arr in vecs:
        in_specs.append(pl.BlockSpec((arr.shape[0], tn), idx(lambda i, j, k: (0, j))))
        args.append(arr)
    out_shape, out_specs = [], []
    for dtype, chunked in outs:
        if chunked:
            per = (N // N_CHIPS) // tn
            out_shape.append(jax.ShapeDtypeStruct((N_CHIPS, M, N // N_CHIPS), dtype))
            out_specs.append(pl.BlockSpec((None, tm, tn), idx(lambda i, j, k, per=per: (j // per, i, j % per))))
        else:
            out_shape.append(jax.ShapeDtypeStruct((M, N), dtype))
            out_specs.append(pl.BlockSpec((tm, tn), idx(lambda i, j, k: (i, j))))
    nacc = 1 if summed else npair
    scratch = [pltpu.VMEM((tm, tn), F32) for _ in range(nacc)] if nk > 1 else []
    nrow, nvec, nout = len(rows), len(vecs), len(outs)
    dims = (((0 if ta else 1,), (1 if tb else 0,)), ((), ()))

    def body(*refs):
        ab = refs[:2 * npair]
        row_refs = refs[2 * npair:2 * npair + nrow]
        vec_refs = refs[2 * npair + nrow:2 * npair + nrow + nvec]
        out_refs = refs[2 * npair + nrow + nvec:2 * npair + nrow + nvec + nout]
        acc_refs = refs[2 * npair + nrow + nvec + nout:]
        if n_outer:
            j, i, k = pl.program_id(0), pl.program_id(1), pl.program_id(2)
        else:
            i, j, k = pl.program_id(0), pl.program_id(1), pl.program_id(2)

        def part(p):
            av, bv = ab[2 * p][...], ab[2 * p + 1][...]
            if a_pro is not None:
                av = a_pro(av)
            if b_pro is not None:
                bv = b_pro(bv)
            return lax.dot_general(av, bv, dims, preferred_element_type=F32)

        def finish(accs):
            row_index = i * tm + lax.broadcasted_iota(jnp.int32, (tm, 1), 0)
            res = epi(accs, [r[...] for r in row_refs], [v[...] for v in vec_refs], row_index)
            for ref, val in zip(out_refs, res):
                ref[...] = val.astype(ref.dtype)

        parts = [part(p) for p in range(npair)]
        if summed:
            total = parts[0]
            for extra in parts[1:]:
                total = total + extra
            parts = [total]
        if nk == 1:
            finish(parts)
        else:
            @pl.when(k == 0)
            def _():
                for q in range(nacc):
                    acc_refs[q][...] = parts[q]

            @pl.when(jnp.logical_and(k > 0, k < nk - 1))
            def _():
                for q in range(nacc):
                    acc_refs[q][...] += parts[q]

            @pl.when(k == nk - 1)
            def _():
                finish([acc_refs[q][...] + parts[q] for q in range(nacc)])

    grid = (nn, nm, nk) if n_outer else (nm, nn, nk)
    return pl.pallas_call(body, name=name, grid=grid, in_specs=in_specs, out_specs=out_specs,
                          out_shape=out_shape, scratch_shapes=scratch, compiler_params=_cparams())(*args)


def _split3(v):
    v0 = v.astype(BF16)
    r1 = v - v0.astype(F32)
    v1 = r1.astype(BF16)
    v2 = (r1 - v1.astype(F32)).astype(BF16)
    return v0, v1, v2


def _mesh_pos():
    return lax.axis_index("x"), lax.axis_index("y"), lax.axis_index("c")


def _allgather_small(name, x):
    m, n = x.shape

    def body(x_ref, out_ref, send_sems, recv_sems, local_sem):
        xi, yi, ci = _mesh_pos()
        me, sibling = (xi, yi, ci), (xi, yi, 1 - ci)
        chips = [(1 - xi, yi), (xi, 1 - yi), (1 - xi, 1 - yi)]

        def rows(px, py, pc):
            return out_ref.at[pl.ds((4 * px + 2 * py + pc) * m, m), :]

        def copy(k, block, to, src=None):
            return pltpu.make_async_remote_copy(
                src_ref=rows(*block) if src is None else src, dst_ref=rows(*block),
                send_sem=send_sems.at[k], recv_sem=recv_sems.at[k], device_id=to, device_id_type=MESH)

        mine = pltpu.make_async_copy(x_ref, rows(*me), local_sem)
        mine.start()
        first = [copy(0, me, sibling, src=x_ref)]
        first += [copy(1 + j, me, (*chip, ci), src=x_ref) for j, chip in enumerate(chips)]
        for cp in first:
            cp.start()
        passed = [copy(4 + j, (*chip, ci), sibling) for j, chip in enumerate(chips)]
        for j, chip in enumerate(chips):
            copy(1 + j, (*chip, ci), me).wait_recv()
            passed[j].start()
        copy(0, sibling, me).wait_recv()
        for j, chip in enumerate(chips):
            copy(4 + j, (*chip, 1 - ci), me).wait_recv()
        for cp in first + passed:
            cp.wait_send()
        mine.wait()

    return pl.pallas_call(
        body, name=name, out_shape=jax.ShapeDtypeStruct((N_DEV * m, n), x.dtype),
        in_specs=[pl.BlockSpec(memory_space=pltpu.VMEM)], out_specs=pl.BlockSpec(memory_space=pltpu.VMEM),
        scratch_shapes=[pltpu.SemaphoreType.DMA((7,)), pltpu.SemaphoreType.DMA((7,)), pltpu.SemaphoreType.DMA],
        compiler_params=_cparams())(x)


_HBM = pl.BlockSpec(memory_space=pltpu.HBM)
_SEM = pl.BlockSpec(memory_space=pltpu.SEMAPHORE)
_ANY = pl.BlockSpec(memory_space=pl.ANY)
_EFFECT = pltpu.SideEffectType.DATAFLOW_SIDE_EFFECTING


def _in_hbm(v):
    return pltpu.with_memory_space_constraint(v, pltpu.HBM)


def _other_chips(xi, yi):
    return [(1 - xi, yi), (xi, 1 - yi), (1 - xi, 1 - yi)]


def _guarded(core, fn):
    if core is None:
        fn()
    else:
        pl.when(lax.axis_index("c") == core)(fn)


def _split_copies(name, srcs, lands, after, pairs, senders, receivers, ncopy):
    ns, nl = len(srcs), len(lands)
    dma = pltpu.SemaphoreType.DMA((ncopy,))
    thru = [pltpu.HBM(v.shape, v.dtype) for v in list(srcs) + list(lands)]

    def start_body(*refs):
        src_refs, land_refs = refs[:ns], refs[ns:ns + nl]
        descs = pairs(src_refs, land_refs, refs[ns + nl + 1], refs[ns + nl + 2])

        def go():
            for send, _ in descs:
                send.start()

        _guarded(senders, go)
        refs[-1][...] = jnp.zeros_like(refs[-1])

    res = pl.pallas_call(
        start_body, name=name + "_start",
        out_shape=(dma, dma, *thru, jax.ShapeDtypeStruct((8, LANES), F32)),
        in_specs=[_HBM] * (ns + nl) + [_ANY],
        out_specs=(_SEM, _SEM, *([_HBM] * (ns + nl)), pl.BlockSpec(memory_space=pltpu.VMEM)),
        input_output_aliases={k: 2 + k for k in range(ns + nl)},
        compiler_params=_cparams(has_side_effects=_EFFECT),
    )(*[_in_hbm(v) for v in srcs], *[_in_hbm(v) for v in lands], after)
    send_sems, recv_sems, token = res[0], res[1], res[-1]
    carried = res[2:2 + ns + nl]

    def finish(after_work):
        def wait_body(*refs):
            src_refs, land_refs = refs[:ns], refs[ns:ns + nl]
            descs = pairs(src_refs, land_refs, refs[ns + nl], refs[ns + nl + 1])

            def sent():
                for send, _ in descs:
                    send.wait_send()

            def landed():
                for _, recv in descs:
                    recv.wait_recv()

            _guarded(senders, sent)
            _guarded(receivers, landed)

        out = pl.pallas_call(
            wait_body, name=name + "_wait", out_shape=tuple(thru),
            in_specs=[_HBM] * (ns + nl) + [_SEM, _SEM, _ANY], out_specs=tuple([_HBM] * (ns + nl)),
            input_output_aliases={k: k for k in range(ns + nl)},
            compiler_params=_cparams(has_side_effects=_EFFECT),
        )(*carried, send_sems, recv_sems, after_work)
        return list(out[:ns]), list(out[ns:])

    return token, finish


def _cast_slot(name, w, chip_index, after):
    R, C = w.shape[1:]
    tr = _div(R, max(16, 524288 // C), mult=16)

    def body(chip_ref, w_ref, after_ref, o_ref):
        o_ref[...] = w_ref[...].astype(BF16)

    return pl.pallas_call(
        body, name=name, out_shape=jax.ShapeDtypeStruct((N_CHIPS, R, C), BF16),
        grid_spec=pltpu.PrefetchScalarGridSpec(
            num_scalar_prefetch=1, grid=(R // tr,),
            in_specs=[pl.BlockSpec((None, tr, C), lambda i, chip_ref: (0, i, 0)), _ANY],
            out_specs=pl.BlockSpec((None, tr, C), lambda i, chip_ref: (chip_ref[0], i, 0))),
        compiler_params=_cparams())(chip_index, w, after)


def _sum_plane(name, grads, landed, chip_index):
    R, C = grads.shape[1:]
    tr = _div(R, max(16, 262144 // C), mult=16)

    def body(chip_ref, own_ref, land_ref, o_ref):
        o_ref[...] = ((own_ref[...].astype(F32) + land_ref[0].astype(F32)) + land_ref[1].astype(F32)) \
            + land_ref[2].astype(F32)

    return pl.pallas_call(
        body, name=name, out_shape=jax.ShapeDtypeStruct((R, C), F32),
        grid_spec=pltpu.PrefetchScalarGridSpec(
            num_scalar_prefetch=1, grid=(R // tr,),
            in_specs=[pl.BlockSpec((None, tr, C), lambda i, chip_ref: (chip_ref[0], i, 0)),
                      pl.BlockSpec((3, tr, C), lambda i, chip_ref: (0, i, 0))],
            out_specs=pl.BlockSpec((tr, C), lambda i, chip_ref: (i, 0))),
        compiler_params=_cparams())(chip_index, grads, landed)


def _gather_split(name, lands, after):
    def pairs(src_refs, land_refs, send_sems, recv_sems):
        xi, yi, _ = _mesh_pos()
        mine = 2 * xi + yi
        out = []
        for a in range(len(lands)):
            for j, (px, py) in enumerate(_other_chips(xi, yi)):
                def to_slot(slot, a=a, j=j, px=px, py=py):
                    return pltpu.make_async_remote_copy(
                        src_ref=land_refs[a].at[mine], dst_ref=land_refs[a].at[slot], send_sem=send_sems.at[3 * a + j],
                        recv_sem=recv_sems.at[3 * a + j], device_id=(px, py, 1), device_id_type=MESH)
                out.append((to_slot(mine), to_slot(2 * px + py)))
        return out

    return _split_copies(name, [], lands, after, pairs, senders=1, receivers=1, ncopy=3 * len(lands))


def _allgather_split(name, block, me, after):
    land = lax.dynamic_update_slice(lax.empty((N_DEV,) + block.shape, block.dtype), block[None], (me, 0, 0))

    def pairs(src_refs, land_refs, send_sems, recv_sems):
        xi, yi, ci = _mesh_pos()
        mine = 4 * xi + 2 * yi + ci
        out = []
        for k in range(1, N_DEV):
            kx, ky, kc = (k >> 2) & 1, (k >> 1) & 1, k & 1
            px = 1 - xi if kx else xi
            py = 1 - yi if ky else yi
            pc = 1 - ci if kc else ci

            def to_slot(slot, k=k, px=px, py=py, pc=pc):
                return pltpu.make_async_remote_copy(
                    src_ref=land_refs[0].at[mine], dst_ref=land_refs[0].at[slot], send_sem=send_sems.at[k - 1],
                    recv_sem=recv_sems.at[k - 1], device_id=(px, py, pc), device_id_type=MESH)
            out.append((to_slot(mine), to_slot(4 * px + 2 * py + pc)))
        return out

    tok, fin = _split_copies(name, [], [land], after, pairs, senders=None, receivers=None, ncopy=N_DEV - 1)
    return tok, lambda later: fin(later)[1][0]


def _swap_split(name, arrs, after):
    lands = [lax.empty(v.shape, v.dtype) for v in arrs]

    def pairs(src_refs, land_refs, send_sems, recv_sems):
        xi, yi, ci = _mesh_pos()
        out = []
        for a in range(len(arrs)):
            cp = pltpu.make_async_remote_copy(
                src_ref=src_refs[a], dst_ref=land_refs[a], send_sem=send_sems.at[a], recv_sem=recv_sems.at[a],
                device_id=(xi, yi, 1 - ci), device_id_type=MESH)
            out.append((cp, cp))
        return out

    return _split_copies(name, arrs, lands, after, pairs, senders=None, receivers=None, ncopy=len(arrs))


def _pass_split(name, lands, after):
    def pairs(src_refs, land_refs, send_sems, recv_sems):
        xi, yi, _ = _mesh_pos()
        out = []
        for a in range(len(lands)):
            for j, (px, py) in enumerate(_other_chips(xi, yi)):
                cp = pltpu.make_async_remote_copy(
                    src_ref=land_refs[a].at[2 * px + py], dst_ref=land_refs[a].at[2 * px + py],
                    send_sem=send_sems.at[3 * a + j], recv_sem=recv_sems.at[3 * a + j],
                    device_id=(xi, yi, 0), device_id_type=MESH)
                out.append((cp, cp))
        return out

    return _split_copies(name, [], lands, after, pairs, senders=1, receivers=0, ncopy=3 * len(lands))


def _scatter_split(name, grads, after):
    lands = [lax.empty((3,) + g.shape[1:], g.dtype) for g in grads]

    def pairs(src_refs, land_refs, send_sems, recv_sems):
        xi, yi, ci = _mesh_pos()
        out = []
        for a in range(len(grads)):
            for j, (px, py) in enumerate(_other_chips(xi, yi)):
                cp = pltpu.make_async_remote_copy(
                    src_ref=src_refs[a].at[2 * px + py], dst_ref=land_refs[a].at[j], send_sem=send_sems.at[3 * a + j],
                    recv_sem=recv_sems.at[3 * a + j], device_id=(px, py, ci), device_id_type=MESH)
                out.append((cp, cp))
        return out

    return _split_copies(name, grads, lands, after, pairs, senders=None, receivers=None, ncopy=3 * len(grads))


def _gather_finish(name, lands):
    na = len(lands)

    def body(*refs):
        outs = refs[na:2 * na]
        send_sems, recv_sems = refs[2 * na:]
        xi, yi, ci = _mesh_pos()
        passes = [pltpu.make_async_remote_copy(
            src_ref=outs[a].at[2 * px + py], dst_ref=outs[a].at[2 * px + py],
            send_sem=send_sems.at[a, j], recv_sem=recv_sems.at[a, j], device_id=(xi, yi, 0), device_id_type=MESH)
            for a in range(na) for j, (px, py) in enumerate(_other_chips(xi, yi))]

        @pl.when(ci == 1)
        def _():
            for cp in passes:
                cp.start()
            for cp in passes:
                cp.wait_send()

        @pl.when(ci == 0)
        def _():
            for cp in passes:
                cp.wait_recv()

    return pl.pallas_call(
        body, name=name, out_shape=[jax.ShapeDtypeStruct(v.shape, v.dtype) for v in lands],
        in_specs=[_ANY] * na, out_specs=[_ANY] * na,
        input_output_aliases={a: a for a in range(na)},
        scratch_shapes=[pltpu.SemaphoreType.DMA((na, 3)), pltpu.SemaphoreType.DMA((na, 3))],
        compiler_params=_cparams())(*lands)


def _attn_tiles(L, Lc, D):
    tq = min(256, Lc)
    return tq, L // tq, Lc // tq, D // HEAD_DIM // Q_PER_KV


def _attn_probs(q, k):
    s = lax.dot_general(q, k, (((1,), (1,)), ((), ())), preferred_element_type=F32) * (HEAD_DIM ** -0.5)
    e = jnp.exp(s - jnp.max(s, axis=-1, keepdims=True))
    return e * (1.0 / jnp.sum(e, axis=-1, keepdims=True))


def _attn_fwd(qr, kr, v, L, Lc, D):
    T = L + Lc
    tq, nq, qoff, nkv = _attn_tiles(L, Lc, D)

    def body(q_ref, k_ref, v_ref, o_ref):
        p = _attn_probs(q_ref[...], k_ref[...])
        o_ref[...] = jnp.dot(p.astype(BF16), v_ref[...], preferred_element_type=F32).astype(o_ref.dtype)

    kv_spec = pl.BlockSpec((T, HEAD_DIM), lambda h, r, q: (0, h))
    return pl.pallas_call(
        body, name="attn_fwd", grid=(nkv, Q_PER_KV, nq),
        in_specs=[pl.BlockSpec((tq, HEAD_DIM), lambda h, r, q: (q + qoff, h * Q_PER_KV + r)), kv_spec, kv_spec],
        out_specs=pl.BlockSpec((tq, HEAD_DIM), lambda h, r, q: (q, h * Q_PER_KV + r)),
        out_shape=jax.ShapeDtypeStruct((L, D), BF16), compiler_params=_cparams())(qr, kr, v)


def _attn_bwd(qr, kr, v, do, L, Lc, D):
    T = L + Lc
    tq, nq, qoff, nkv = _attn_tiles(L, Lc, D)
    scale = HEAD_DIM ** -0.5

    def body(q_ref, k_ref, v_ref, do_ref, dq_ref, dk_ref, dv_ref):
        first = jnp.logical_and(pl.program_id(1) == 0, pl.program_id(2) == 0)
        q, k, dout = q_ref[...], k_ref[...], do_ref[...]
        p = _attn_probs(q, k)
        dp = lax.dot_general(dout, v_ref[...], (((1,), (1,)), ((), ())), preferred_element_type=F32)
        ds = (p * (dp - jnp.sum(p * dp, axis=-1, keepdims=True)) * scale).astype(BF16)
        dq_ref[...] = jnp.dot(ds, k, preferred_element_type=F32)
        dk = lax.dot_general(ds, q, (((0,), (0,)), ((), ())), preferred_element_type=F32)
        dv = lax.dot_general(p.astype(BF16), dout, (((0,), (0,)), ((), ())), preferred_element_type=F32)

        @pl.when(first)
        def _():
            dk_ref[...] = dk
            dv_ref[...] = dv

        @pl.when(jnp.logical_not(first))
        def _():
            dk_ref[...] += dk
            dv_ref[...] += dv

    kv_spec = pl.BlockSpec((T, HEAD_DIM), lambda h, r, q: (0, h))
    q_spec = pl.BlockSpec((tq, HEAD_DIM), lambda h, r, q: (q + qoff, h * Q_PER_KV + r))
    o_spec = pl.BlockSpec((tq, HEAD_DIM), lambda h, r, q: (q, h * Q_PER_KV + r))
    return pl.pallas_call(
        body, name="attn_bwd", grid=(nkv, Q_PER_KV, nq),
        in_specs=[q_spec, kv_spec, kv_spec, o_spec], out_specs=[o_spec, kv_spec, kv_spec],
        out_shape=[jax.ShapeDtypeStruct((L, D), F32), jax.ShapeDtypeStruct((T, D // Q_PER_KV), F32),
                   jax.ShapeDtypeStruct((T, D // Q_PER_KV), F32)],
        compiler_params=_cparams())(qr, kr, v, do)


SUB = 8


def _doubling(xr, xi, pw_re, pw_im, lanes, first_power, period, reverse):
    n = xr.shape[0]
    rows = lax.broadcasted_iota(jnp.int32, (n, 1), 0) & (period - 1)
    for k in range(period.bit_length() - 1):
        d = 1 << k
        keep = rows < period - d if reverse else rows >= d
        sr = jnp.where(keep, pltpu.roll(xr, n - d if reverse else d, 0), 0.0)
        si = jnp.where(keep, pltpu.roll(xi, n - d if reverse else d, 0), 0.0)
        pr, pi = pw_re[first_power + k:first_power + k + 1, lanes], pw_im[first_power + k:first_power + k + 1, lanes]
        xr, xi = xr + (pr * sr - pi * si), xi + (pr * si + pi * sr)
    return xr, xi


def _scan_tile(xr, xi, tb, lanes, reverse):
    pw_re, pw_im, w8_re, w8_im, wb_re, wb_im, carry_re, carry_im, sr, si = tb
    tt = xr.shape[0]
    nb = tt // SUB
    xr, xi = _doubling(xr, xi, pw_re, pw_im, lanes, 0, SUB, reverse)
    nq = sr.shape[0]
    cols = [slice(q * LANES, (q + 1) * LANES) for q in range(nq)]
    for q in range(nq):
        sr[q] = xr[:, cols[q]]
        si[q] = xi[:, cols[q]]
    last = 0 if reverse else SUB - 1
    er = jnp.concatenate([sr[q, pl.ds(last, nb, stride=SUB), :] for q in range(nq)], axis=1)
    ei = jnp.concatenate([si[q, pl.ds(last, nb, stride=SUB), :] for q in range(nq)], axis=1)
    er, ei = _doubling(er, ei, pw_re, pw_im, lanes, 3, nb, reverse)
    car, cai = carry_re[:, lanes], carry_im[:, lanes]
    wbr, wbi = wb_re[:, lanes], wb_im[:, lanes]
    er = er + (wbr * car - wbi * cai)
    ei = ei + (wbr * cai + wbi * car)
    out_block = 0 if reverse else nb - 1
    carry_re[:, lanes] = er[out_block:out_block + 1, :]
    carry_im[:, lanes] = ei[out_block:out_block + 1, :]
    blocks = lax.broadcasted_iota(jnp.int32, (nb, 1), 0)
    first = blocks == (nb - 1 if reverse else 0)
    cr = jnp.where(first, car, pltpu.roll(er, nb - 1 if reverse else 1, 0))
    ci = jnp.where(first, cai, pltpu.roll(ei, nb - 1 if reverse else 1, 0))
    for r in range(SUB):
        wr, wi = w8_re[r:r + 1, lanes], w8_im[r:r + 1, lanes]
        add_r, add_i = wr * cr - wi * ci, wr * ci + wi * cr
        for q in range(nq):
            sr[q, pl.ds(r, nb, stride=SUB), :] += add_r[:, cols[q]]
            si[q, pl.ds(r, nb, stride=SUB), :] += add_i[:, cols[q]]
    hr = jnp.concatenate([sr[q] for q in range(nq)], axis=1)
    hi = jnp.concatenate([si[q] for q in range(nq)], axis=1)
    return hr, hi, car, cai


def _scan_scratch(tt, NS):
    nb = tt // SUB
    return [pltpu.VMEM((8, NS), F32), pltpu.VMEM((8, NS), F32), pltpu.VMEM((SUB, NS), F32), pltpu.VMEM((SUB, NS), F32),
            pltpu.VMEM((nb, NS), F32), pltpu.VMEM((nb, NS), F32), pltpu.VMEM((1, NS), F32), pltpu.VMEM((1, NS), F32),
            pltpu.VMEM((SLAB_ST // LANES, tt, LANES), F32), pltpu.VMEM((SLAB_ST // LANES, tt, LANES), F32)]


def _scan_init(lr, li, tb, reverse):
    pw_re, pw_im, w8_re, w8_im, wb_re, wb_im, carry_re, carry_im, sr, _ = tb
    nb = wb_re.shape[0]
    carry_re[...] = jnp.zeros_like(carry_re)
    carry_im[...] = jnp.zeros_like(carry_im)
    pr, pi = lr, li
    for k in range(3 + nb.bit_length() - 1):
        pw_re[k:k + 1, :] = pr
        pw_im[k:k + 1, :] = pi
        if k == 3:
            l8r, l8i = pr, pi
        pr, pi = pr * pr - pi * pi, 2.0 * pr * pi
    pr, pi = lr, li
    for r in range(SUB):
        row = SUB - 1 - r if reverse else r
        w8_re[row:row + 1, :] = pr
        w8_im[row:row + 1, :] = pi
        pr, pi = pr * lr - pi * li, pr * li + pi * lr
    pr, pi = l8r, l8i
    for b in range(nb):
        row = nb - 1 - b if reverse else b
        wb_re[row:row + 1, :] = pr
        wb_im[row:row + 1, :] = pi
        pr, pi = pr * l8r - pi * l8i, pr * l8i + pi * l8r


def _ssm_tiles(T, Lc):
    tt = min(128, Lc)
    return tt, T // tt, Lc // tt


def _ssm_fwd(name, u, bbd, cbd_re, cbd_im, lam_re, lam_im, coef_re, coef_im, Lc, reverse):
    T, W = u.shape
    nslab = W // SLAB_CH
    NS = nslab * SLAB_ST
    tt, nt, nc = _ssm_tiles(T, Lc)
    if reverse:
        tile = lambda s: jnp.where(s < nc, nc - 1 - s, nt - 1 - (s - nc))
    else:
        tile = lambda s: s

    def body(u_ref, b_ref, cr_ref, ci_ref, lr_ref, li_ref, kr_ref, ki_ref, hr_ref, hi_ref, y_ref, *tb):
        @pl.when(pl.program_id(0) == 0)
        def _():
            _scan_init(lr_ref[...], li_ref[...], tb, reverse)

        for j in range(nslab):
            lanes = slice(j * SLAB_ST, (j + 1) * SLAB_ST)
            bu = jnp.dot(u_ref[:, j * SLAB_CH:(j + 1) * SLAB_CH], b_ref[j], preferred_element_type=F32)
            br, bi = bu[:, :SLAB_ST], bu[:, SLAB_ST:]
            kr, ki = kr_ref[:, lanes], ki_ref[:, lanes]
            hr, hi, _, _ = _scan_tile(kr * br - ki * bi, kr * bi + ki * br, tb, lanes, reverse)
            hrb, hib = hr.astype(BF16), hi.astype(BF16)
            hr_ref[:, lanes] = hrb
            hi_ref[:, lanes] = hib
            y_ref[:, j * SLAB_CH:(j + 1) * SLAB_CH] = (
                jnp.dot(hrb, cr_ref[j], preferred_element_type=F32)
                - jnp.dot(hib, ci_ref[j], preferred_element_type=F32))

    whole3 = lambda arr: pl.BlockSpec(arr.shape, lambda s: (0, 0, 0))
    vec = pl.BlockSpec((1, NS), lambda s: (0, 0))
    return pl.pallas_call(
        body, name=name, grid=(nt,),
        in_specs=[pl.BlockSpec((tt, W), lambda s: (tile(s), 0)), whole3(bbd), whole3(cbd_re), whole3(cbd_im),
                  vec, vec, vec, vec],
        out_specs=[pl.BlockSpec((tt, NS), lambda s: (tile(s), 0)), pl.BlockSpec((tt, NS), lambda s: (tile(s), 0)),
                   pl.BlockSpec((tt, W), lambda s: (tile(s), 0))],
        out_shape=[jax.ShapeDtypeStruct((T, NS), BF16), jax.ShapeDtypeStruct((T, NS), BF16),
                   jax.ShapeDtypeStruct((T, W), F32)],
        scratch_shapes=_scan_scratch(tt, NS),
        compiler_params=_cparams())(u, bbd, cbd_re, cbd_im, lam_re, lam_im, coef_re, coef_im)


def _ssm_bwd(name, dy, h_re, h_im, u, bbd, bbdt_re, bbdt_im, cbdt_re, cbdt_im, lam_re, lam_im,
             coef_re, coef_im, Lc, reverse):
    T, W = u.shape
    nslab = W // SLAB_CH
    NS = nslab * SLAB_ST
    tt, nt, nc = _ssm_tiles(T, Lc)
    adj_reverse = not reverse
    if reverse:
        tile = lambda s: jnp.where(s < nt - nc, nc + s, s - (nt - nc))
    else:
        tile = lambda s: nt - 1 - s

    def body(dy_ref, hr_ref, hi_ref, u_ref, b_ref, btr_ref, bti_ref, ctr_ref, cti_ref, lr_ref, li_ref,
             kr_ref, ki_ref, du_ref, dlr_ref, dli_ref, dkr_ref, dki_ref, dbf_ref, dcrf_ref, dcif_ref,
             db_ref, dcr_ref, dci_ref, *tb):
        @pl.when(pl.program_id(0) == 0)
        def _():
            _scan_init(lr_ref[...], -li_ref[...], tb, adj_reverse)
            for ref in (dlr_ref, dli_ref, dkr_ref, dki_ref, db_ref, dcr_ref, dci_ref):
                ref[...] = jnp.zeros_like(ref)

        rows = lax.broadcasted_iota(jnp.int32, (tt, 1), 0)
        far_row = tt - 1 if adj_reverse else 0
        tn_dims = (((0,), (0,)), ((), ()))
        for j in range(nslab):
            lanes = slice(j * SLAB_ST, (j + 1) * SLAB_ST)
            chans = slice(j * SLAB_CH, (j + 1) * SLAB_CH)
            dys, us = dy_ref[:, chans], u_ref[:, chans]
            er = jnp.dot(dys, ctr_ref[j], preferred_element_type=F32)
            ei = -jnp.dot(dys, cti_ref[j], preferred_element_type=F32)
            ar, ai, car, cai = _scan_tile(er, ei, tb, lanes, adj_reverse)
            shift = tt - 1 if adj_reverse else 1
            nr = jnp.where(rows == far_row, car, pltpu.roll(ar, shift, 0))
            ni = jnp.where(rows == far_row, cai, pltpu.roll(ai, shift, 0))
            hrb, hib = hr_ref[:, lanes], hi_ref[:, lanes]
            hr, hi = hrb.astype(F32), hib.astype(F32)
            dlr_ref[:, lanes] += jnp.sum(nr * hr + ni * hi, axis=0, keepdims=True)
            dli_ref[:, lanes] += jnp.sum(ni * hr - nr * hi, axis=0, keepdims=True)
            bu = jnp.dot(us, b_ref[j], preferred_element_type=F32)
            br, bi = bu[:, :SLAB_ST], bu[:, SLAB_ST:]
            dkr_ref[:, lanes] += jnp.sum(ar * br + ai * bi, axis=0, keepdims=True)
            dki_ref[:, lanes] += jnp.sum(ai * br - ar * bi, axis=0, keepdims=True)
            kr, ki = kr_ref[:, lanes], ki_ref[:, lanes]
            dbr = (ar * kr + ai * ki).astype(BF16)
            dbi = (ai * kr - ar * ki).astype(BF16)
            du_ref[:, chans] = (jnp.dot(dbr, btr_ref[j], preferred_element_type=F32)
                                + jnp.dot(dbi, bti_ref[j], preferred_element_type=F32))
            db_ref[j, :, :SLAB_ST] += lax.dot_general(us, dbr, tn_dims, preferred_element_type=F32)
            db_ref[j, :, SLAB_ST:] += lax.dot_general(us, dbi, tn_dims, preferred_element_type=F32)
            dcr_ref[j] += lax.dot_general(hrb, dys, tn_dims, preferred_element_type=F32)
            dci_ref[j] -= lax.dot_general(hib, dys, tn_dims, preferred_element_type=F32)

        @pl.when(pl.program_id(0) == nt - 1)
        def _():
            def iota(shape, axis):
                return lax.broadcasted_iota(jnp.int32, shape, axis)

            sg, ss = SSM_GROUP.bit_length() - 1, SSM_STATE.bit_length() - 1
            b_mask = (iota((SLAB_CH, SLAB_ST), 0) >> sg) == (iota((SLAB_CH, SLAB_ST), 1) >> ss)
            c_mask = (iota((SLAB_ST, SLAB_CH), 0) >> ss) == (iota((SLAB_ST, SLAB_CH), 1) >> sg)
            fold = jnp.where((iota((SLAB_ST, SSM_STATE), 0) & (SSM_STATE - 1)) == iota((SLAB_ST, SSM_STATE), 1),
                             1.0, 0.0).astype(BF16)
            fold_t = jnp.where((iota((SSM_STATE, SLAB_ST), 1) & (SSM_STATE - 1)) == iota((SSM_STATE, SLAB_ST), 0),
                               1.0, 0.0).astype(BF16)

            def exact_dot(a, b, a_is_value):
                terms = _split3(a if a_is_value else b)
                acc = None
                for t in terms:
                    part = jnp.dot(t, b, preferred_element_type=F32) if a_is_value else jnp.dot(a, t, preferred_element_type=F32)
                    acc = part if acc is None else acc + part
                return acc

            for j in range(nslab):
                dbj = db_ref[j]
                dbf_ref[j, :, :SSM_STATE] = exact_dot(jnp.where(b_mask, dbj[:, :SLAB_ST], 0.0), fold, True)
                dbf_ref[j, :, SSM_STATE:] = exact_dot(jnp.where(b_mask, dbj[:, SLAB_ST:], 0.0), fold, True)
                dcrf_ref[j] = exact_dot(fold_t, jnp.where(c_mask, dcr_ref[j], 0.0), False)
                dcif_ref[j] = exact_dot(fold_t, jnp.where(c_mask, dci_ref[j], 0.0), False)

    whole3 = lambda arr: pl.BlockSpec(arr.shape, lambda s: (0, 0, 0))
    vec = pl.BlockSpec((1, NS), lambda s: (0, 0))
    row_w = pl.BlockSpec((tt, W), lambda s: (tile(s), 0))
    row_s = pl.BlockSpec((tt, NS), lambda s: (tile(s), 0))
    dbf = jax.ShapeDtypeStruct((nslab, SLAB_CH, 2 * SSM_STATE), F32)
    dcf = jax.ShapeDtypeStruct((nslab, SSM_STATE, SLAB_CH), F32)
    return pl.pallas_call(
        body, name=name, grid=(nt,),
        in_specs=[row_w, row_s, row_s, row_w, whole3(bbd), whole3(bbdt_re), whole3(bbdt_im), whole3(cbdt_re),
                  whole3(cbdt_im), vec, vec, vec, vec],
        out_specs=[row_w, vec, vec, vec, vec, whole3(dbf), whole3(dcf), whole3(dcf)],
        out_shape=[jax.ShapeDtypeStruct((T, W), F32)] + [jax.ShapeDtypeStruct((1, NS), F32)] * 4 + [dbf, dcf, dcf],
        scratch_shapes=[pltpu.VMEM(bbd.shape, F32), pltpu.VMEM(bbdt_re.shape, F32), pltpu.VMEM(bbdt_re.shape, F32)]
        + _scan_scratch(tt, NS),
        compiler_params=_cparams())(dy, h_re, h_im, u, bbd, bbdt_re, bbdt_im, cbdt_re, cbdt_im,
                                    lam_re, lam_im, coef_re, coef_im)


def _zoh_math(a_re, a_im, log_dt):
    dt = jnp.exp(log_dt)
    mag = jnp.exp(a_re * dt)
    lb_re = mag * jnp.cos(a_im * dt)
    lb_im = mag * jnp.sin(a_im * dt)
    den = a_re * a_re + a_im * a_im
    coef_re = ((lb_re - 1.0) * a_re + lb_im * a_im) / den
    coef_im = (lb_im * a_re - (lb_re - 1.0) * a_im) / den
    return lb_re, lb_im, coef_re, coef_im


def _zoh_fwd(a_re, a_im, log_dt):
    def body(ar, ai, ld, o0, o1, o2, o3):
        for ref, val in zip((o0, o1, o2, o3), _zoh_math(ar[...], ai[...], ld[...])):
            ref[...] = val

    return pl.pallas_call(body, name="zoh_fwd", out_shape=[jax.ShapeDtypeStruct(a_re.shape, F32)] * 4,
                          compiler_params=_cparams())(a_re, a_im, log_dt)


def _zoh_bwd(a_re, a_im, log_dt, cots):
    def body(ar, ai, ld, c0, c1, c2, c3, o0, o1, o2):
        _, vjp = jax.vjp(_zoh_math, ar[...], ai[...], ld[...])
        for ref, val in zip((o0, o1, o2), vjp((c0[...], c1[...], c2[...], c3[...]))):
            ref[...] = val

    return pl.pallas_call(
        body, name="zoh_bwd",
        out_shape=[jax.ShapeDtypeStruct(a_re.shape, F32), jax.ShapeDtypeStruct(a_re.shape, F32),
                   jax.ShapeDtypeStruct(log_dt.shape, F32)],
        compiler_params=_cparams())(a_re, a_im, log_dt, *cots)


def _outer_sum(acts, cots):
    D, N = acts.shape[1], cots.shape[1]
    tm, tn = _div(D, 512), _div(N, 1152)
    dims = (((0,), (0,)), ((), ()))

    def body(a_ref, b_ref, o_ref):
        a = a_ref[...]
        aa = _split3(a * _sigmoid(a))
        bb = _split3(b_ref[...])
        acc = None
        for ia in range(3):
            for ib in range(3 - ia):
                t = lax.dot_general(aa[ia], bb[ib], dims, preferred_element_type=F32)
                acc = t if acc is None else acc + t
        o_ref[...] = acc

    return pl.pallas_call(
        body, name="mod_dw", grid=(D // tm, N // tn),
        in_specs=[pl.BlockSpec((16, tm), lambda i, j: (0, i)), pl.BlockSpec((16, tn), lambda i, j: (0, j))],
        out_specs=pl.BlockSpec((tm, tn), lambda i, j: (i, j)),
        out_shape=jax.ShapeDtypeStruct((D, N), F32), compiler_params=_cparams())(acts, cots)


def _adamw_math(w, g, m, v):
    m = ADAM_B1 * m + (1.0 - ADAM_B1) * g
    v = ADAM_B2 * v + (1.0 - ADAM_B2) * (g * g)
    m_hat = m / (1.0 - ADAM_B1 ** ADAM_STEP)
    v_hat = v / (1.0 - ADAM_B2 ** ADAM_STEP)
    delta = -ADAM_LR * (m_hat / (jnp.sqrt(v_hat) + ADAM_EPS) + ADAM_WD * w)
    return delta, m, v


def _adamw(name, w, m, v, gparts):
    R, C = w.shape[-2:]
    kind = 'row1' if w.ndim == 3 else 'row'
    tr = _div(R, max(8, 262144 // C), mult=8)

    def fn(i, wv, mv, vv, *gs):
        g = gs[0]
        for extra in gs[1:]:
            g = g + extra
        return (g,) + _adamw_math(wv, g, mv, vv)

    return _rowk(name, fn, R, tr, [(w, kind), (m, kind), (v, kind)] + [(g, 'row') for g in gparts],
                 [(w.shape, F32, kind)] * 4)


def _pack(pieces, rows_mult=8):
    flat = jnp.concatenate([p.reshape(-1).astype(F32) for p in pieces])
    unit = rows_mult * PACK_W
    total = -(-flat.shape[0] // unit) * unit
    return jnp.pad(flat, (0, total - flat.shape[0])).reshape(total // PACK_W, PACK_W)


def _unpack(buf, shapes):
    flat = buf.reshape(-1)
    out, off = [], 0
    for s in shapes:
        n = math.prod(s)
        out.append(flat[off:off + n].reshape(s))
        off += n
    return out


def _bd_expand(t):
    S, g, a, b = t.shape
    eye = jnp.eye(g, dtype=t.dtype)
    return (t[:, :, :, None, :] * eye[None, :, None, :, None]).reshape(S, g * a, g * b)


def _rope_tables(L, Lc):
    rows = L // GRID_W
    row_ids = jnp.broadcast_to(jnp.arange(rows)[:, None], (rows, GRID_W)).reshape(-1).astype(F32)
    col_ids = jnp.broadcast_to(jnp.arange(GRID_W)[None, :], (rows, GRID_W)).reshape(-1).astype(F32)
    quarter = HEAD_DIM // 4
    inv_freq = ROPE_THETA ** (-jnp.arange(quarter, dtype=F32) / quarter)
    ang_r = row_ids[:, None] * inv_freq
    ang_c = col_ids[:, None] * inv_freq
    cos = jnp.concatenate([jnp.cos(ang_r), jnp.cos(ang_r), jnp.cos(ang_c), jnp.cos(ang_c)], axis=1)
    sin = jnp.concatenate([-jnp.sin(ang_r), jnp.sin(ang_r), -jnp.sin(ang_c), jnp.sin(ang_c)], axis=1)
    cos = jnp.concatenate([jnp.ones((Lc, HEAD_DIM), F32), cos], axis=0)
    sin = jnp.concatenate([jnp.zeros((Lc, HEAD_DIM), F32), sin], axis=0)
    return cos, sin


def _rot(v):
    lane = lax.broadcasted_iota(jnp.int32, (1, HEAD_DIM), 1)
    first = (lane % (HEAD_DIM // 2)) < (HEAD_DIM // 4)
    return jnp.where(first, pltpu.roll(v, HEAD_DIM - HEAD_DIM // 4, 1), pltpu.roll(v, HEAD_DIM // 4, 1))


def _head_norm(xh, g):
    return xh * lax.rsqrt(jnp.mean(xh * xh, axis=-1, keepdims=True) + NORM_EPS) * g


def _norm_mod(xv, g, sh, sc):
    r = lax.rsqrt(jnp.mean(xv * xv, axis=-1, keepdims=True) + NORM_EPS)
    return (xv * r) * g * (1.0 + sc) + sh


def kernel(x, c, ctx, c_ctx, w_mod, b_mod, norm_g, w_ffn1_gate, w_ffn1_up, w_ffn1_down, w_in, q_norm_g, k_norm_g, ssm_a_re, ssm_a_im, ssm_log_dt, ssm_b_re, ssm_b_im, ssm_c_re, ssm_c_im, ssm_d, w_glu, b_glu, w_br_attn, w_br_ssm, w_out, w_ffn2_gate, w_ffn2_up, w_ffn2_down, loss_target, m_c_ctx, m_w_mod, m_b_mod, m_norm_g, m_w_ffn1_gate, m_w_ffn1_up, m_w_ffn1_down, m_w_in, m_q_norm_g, m_k_norm_g, m_ssm_a_re, m_ssm_a_im, m_ssm_log_dt, m_ssm_b_re, m_ssm_b_im, m_ssm_c_re, m_ssm_c_im, m_ssm_d, m_w_glu, m_b_glu, m_w_br_attn, m_w_br_ssm, m_w_out, m_w_ffn2_gate, m_w_ffn2_up, m_w_ffn2_down, v_c_ctx, v_w_mod, v_b_mod, v_norm_g, v_w_ffn1_gate, v_w_ffn1_up, v_w_ffn1_down, v_w_in, v_q_norm_g, v_k_norm_g, v_ssm_a_re, v_ssm_a_im, v_ssm_log_dt, v_ssm_b_re, v_ssm_b_im, v_ssm_c_re, v_ssm_c_im, v_ssm_d, v_w_glu, v_b_glu, v_w_br_attn, v_w_br_ssm, v_w_out, v_w_ffn2_gate, v_w_ffn2_up, v_w_ffn2_down):
    A = dict(locals())
    xi, yi, ci = _mesh_pos()
    chip = 2 * xi + yi
    me = 4 * xi + 2 * yi + ci
    L, D = x.shape[1], x.shape[2]
    Lc = ctx.shape[1]
    T = L + Lc
    F4 = w_ffn1_gate.shape[2]
    F = N_CHIPS * F4
    W, KV, Dq = D // 2, D // 4, D // 4
    G = W // SSM_GROUP
    P, E = SSM_STATE, SSM_GROUP
    NS = G * P
    nslab = W // SLAB_CH
    tr = min(256, Lc)
    ncr = Lc // tr
    assert L % tr == 0 and Lc % tr == 0 and W % SLAB_CH == 0 and D % (4 * LANES) == 0

    def sel(i, v):
        return v if v.shape[0] == 1 else jnp.where(i < ncr, v[0:1], v[1:2])

    def put(i, v, nrow):
        if nrow == 1:
            return v
        which = (i >= ncr).astype(jnp.int32)
        r2 = lax.broadcasted_iota(jnp.int32, (nrow, 1), 0)
        return jnp.where(r2 == which, jnp.broadcast_to(v, (nrow, v.shape[1])), 0.0)

    ident = lambda accs, rows, vecs, ri: [accs[0]]

    NM = w_mod.shape[2]
    first = jnp.zeros((8, D), F32).at[0].set(c[0]).at[1:4, :Dq].set(norm_g[0])
    g0 = _allgather_small("gather_c", first).reshape(N_CHIPS, 2, 8, D)
    c_all = g0[:, :, 0].reshape(N_DEV, D)
    ng = jnp.transpose(g0[:, 0, 1:4, :Dq], (1, 0, 2)).reshape(3, D)
    acts = jnp.concatenate([c_all, c_ctx[None], jnp.zeros((7, D), F32)], axis=0)
    wm = w_mod[0]
    b_shard = lax.dynamic_slice(b_mod[0], (chip * NM,), (NM,))[None]
    silu_bf = lambda a: (a * _sigmoid(a)).astype(BF16)
    to_bf = lambda b: b.astype(BF16)
    mod_part = _mm("mod_fwd", [(acts, wm, D)], 16, NM, tm=16, tn=_div(NM, 1152),
                   epi=lambda accs, rows, vecs, ri: [accs[0] + vecs[0]], outs=[(F32, False)],
                   vecs=[b_shard], a_pro=silu_bf, b_pro=to_bf)[0]
    mg = _allgather_small("gather_mod", mod_part).reshape(N_CHIPS, 2, 16, NM)[:, 0]
    mod_all = jnp.transpose(mg, (1, 0, 2)).reshape(16, N_CHIPS * NM)
    mod_x = lax.dynamic_slice(mod_all, (me, 0), (1, 9 * D))
    mod_c = jnp.where(jnp.arange(9 * D)[None] < 5 * D, mod_all[8:9], 0.0)
    modv = jnp.concatenate([mod_c, mod_x], axis=0)
    mv = lambda k: modv[:, k * D:(k + 1) * D]
    sh1, sc1, g1, sh2, sc2 = mv(0), mv(1), mv(2), mv(3), mv(4)
    g2, sh3, sc3, g3 = mv(5)[1:2], mv(6)[1:2], mv(7)[1:2], mv(8)[1:2]

    big = ['w_ffn1_gate', 'w_ffn1_up', 'w_ffn1_down', 'w_ffn2_gate', 'w_ffn2_up', 'w_ffn2_down',
           'w_in', 'w_glu', 'w_br_attn', 'w_br_ssm', 'w_out']
    row_sharded = {'w_ffn1_down', 'w_ffn2_down', 'w_glu', 'w_br_attn', 'w_out'}
    groups = [big[0:2], big[2:3], big[6:7], big[7:11], big[3:6]]
    chip_index = jnp.reshape(chip, (1,)).astype(jnp.int32)
    tok, gather_finish = modv, []
    pin = c
    for gi, names in enumerate(groups):
        tok, fin = _gather_split("gather_w%d" % gi, [_cast_slot("cast_" + n, A[n], chip_index, pin) for n in names], tok)
        gather_finish.append(fin)
        pin = tok
    ng = ng + tok[0:1, 0:1]
    Wt = {}

    def register(names, full):
        for n, gw in zip(names, full):
            Wt[n] = gw.reshape(N_CHIPS * gw.shape[1], gw.shape[2]) if n in row_sharded else gw

    def weights_ready(gi, after_work):
        _, lands = gather_finish[gi](after_work)
        register(groups[gi], _gather_finish("gather_w%d_pass" % gi, lands))

    def weights_pass(gi, after_work):
        _, lands = gather_finish[gi](after_work)
        tok_, fin_ = _pass_split("gather_w%d_pass" % gi, lands, after_work)
        return tok_, lambda later: register(groups[gi], fin_(later)[1])

    a_re2, a_im2 = ssm_a_re[0].reshape(2 * G, P), ssm_a_im[0].reshape(2 * G, P)
    ldt2 = ssm_log_dt[0].reshape(2 * G, 1)
    zoh = _zoh_fwd(a_re2, a_im2, ldt2)
    lam_re, lam_im, coef_re, coef_im = [[z[d * G:(d + 1) * G].reshape(1, NS) for d in range(2)] for z in zoh]
    bd_b = lambda b: _bd_expand(jnp.transpose(b, (0, 2, 1)).reshape(nslab, SLAB_GROUPS, E, P))
    bd_c = lambda cc: _bd_expand(jnp.transpose(cc, (0, 2, 1)).reshape(nslab, SLAB_GROUPS, P, E))
    bbd, bbdt_re, bbdt_im, cbd_re, cbd_im, cbdt_re, cbdt_im = [], [], [], [], [], [], []
    for d in range(2):
        br_, bi_ = bd_b(ssm_b_re[0, d]).astype(BF16), bd_b(ssm_b_im[0, d]).astype(BF16)
        cr_, ci_ = bd_c(ssm_c_re[0, d]).astype(BF16), bd_c(ssm_c_im[0, d]).astype(BF16)
        bbd.append(jnp.concatenate([br_, bi_], axis=2))
        bbdt_re.append(jnp.transpose(br_, (0, 2, 1)))
        bbdt_im.append(jnp.transpose(bi_, (0, 2, 1)))
        cbd_re.append(cr_)
        cbd_im.append(ci_)
        cbdt_re.append(jnp.transpose(cr_, (0, 2, 1)))
        cbdt_im.append(jnp.transpose(ci_, (0, 2, 1)))
    cos_t, sin_t = _rope_tables(L, Lc)
    qg, kg = q_norm_g, k_norm_g
    small = ['c_ctx', 'b_mod', 'norm_g', 'q_norm_g', 'k_norm_g', 'ssm_a_re', 'ssm_a_im', 'ssm_log_dt', 'ssm_b_re',
             'ssm_b_im', 'ssm_c_re', 'ssm_c_im', 'ssm_d', 'b_glu']
    packs_wmv = [_pack([A[pre + n] for n in small]) for pre in ('', 'm_', 'v_')]
    prepared = packs_wmv + [cos_t, sin_t, coef_im[0], coef_im[1]] + [
        t[d][0] for t in (bbd, bbdt_re, bbdt_im, cbd_re, cbd_im, cbdt_re, cbdt_im) for d in range(2)]
    weights_ready(0, tok + sum(t[0:1, 0:1].astype(F32) for t in prepared))

    def norm_mod(name, xv, g, sh, sc):
        rows = xv.shape[0]
        return _rowk(name, lambda i, xt, gt, sht, sct: [_norm_mod(xt, gt, sel(i, sht), sel(i, sct))],
                     rows, tr, [(xv, 'row'), (g, 'vec'), (sh, 'vec'), (sc, 'vec')], [((rows, D), BF16, 'row')])[0]

    def swiglu_epi(accs, rows, vecs, ri):
        a_, b_ = accs
        return [a_, b_, a_ * _sigmoid(a_) * b_]

    def res_epi(coef):
        def epi(accs, rows, vecs, ri):
            gate = vecs[0]
            if gate.shape[0] == 2:
                gate = jnp.where(ri < Lc, gate[0:1], gate[1:2])
            return [accs[0], rows[0] + (coef * gate) * accs[0]]
        return epi

    def ffn_fwd(tag, h, xres, gate, down_ready=None):
        rows = h.shape[0]
        a_, b_, s_ = _mm(tag + "_up", [(h, Wt['w_' + tag + '_gate'], D), (h, Wt['w_' + tag + '_up'], D)], rows, F,
                         tm=_div(rows, 512), tn=F4, epi=swiglu_epi, outs=[(BF16, False), (BF16, False), (BF16, False)])
        if down_ready is not None:
            down_ready(s_)
        f_, xo = _mm(tag + "_down", [(s_, Wt['w_' + tag + '_down'], F)], rows, D, tm=_div(rows, 768),
                     tn=_div(D, 512), epi=res_epi(0.5), outs=[(F32, False), (F32, False)],
                     rows=[(xres, 0, 0)], vecs=[gate])
        return a_, b_, s_, f_, xo

    xc = jnp.concatenate([ctx[0], x[0]], axis=0)
    h1 = norm_mod("norm1", xc, ng[0:1], sh1, sc1)
    a1, b1, s1, f1, x1 = ffn_fwd("ffn1", h1, xc, g1, down_ready=lambda s_: weights_ready(1, s_))
    weights_ready(2, x1)
    h2 = norm_mod("norm2", x1, ng[1:2], sh2, sc2)
    proj = _mm("in_proj", [(h2, Wt['w_in'], D)], T, 4 * D, tm=_div(T, 768), tn=_div(D, 1024), epi=ident,
               outs=[(F32, False)])[0]
    nh, nkvh = D // HEAD_DIM, KV // HEAD_DIM

    def prep_fn(i, kt, vt, ut, qt, qgt, kgt, ct, st):
        qs = [_head_norm(qt[:, h * HEAD_DIM:(h + 1) * HEAD_DIM], qgt) for h in range(nh)]
        ks = [_head_norm(kt[:, h * HEAD_DIM:(h + 1) * HEAD_DIM], kgt) for h in range(nkvh)]
        qs = [v * ct + _rot(v) * st for v in qs]
        ks = [v * ct + _rot(v) * st for v in ks]
        return [jnp.concatenate(qs, axis=1), jnp.concatenate(ks, axis=1), vt, ut]

    qr, kr, vb, ub = _rowk(
        "qk_prep", prep_fn, T, tr,
        [(proj, ('col', KV, 0)), (proj, ('col', KV, 1)), (proj, ('col', W, 1)), (proj, ('col', D, 1)),
         (qg, 'vec'), (kg, 'vec'), (cos_t, 'row'), (sin_t, 'row')],
        [((T, D), BF16, 'row'), ((T, KV), BF16, 'row'), ((T, KV), BF16, 'row'), ((T, W), BF16, 'row')])
    _, mixer_weights = weights_pass(3, qr)
    attn = _attn_fwd(qr, kr, vb, L, Lc, D)
    hs_re, hs_im, ys = [], [], []
    lam_in = lam_re[0]
    for d in range(2):
        hr_, hi_, y_ = _ssm_fwd("ssm_fwd%d" % d, ub, bbd[d], cbd_re[d], cbd_im[d], lam_in, lam_im[d],
                                coef_re[d], coef_im[d], Lc, reverse=bool(d))
        hs_re.append(hr_)
        hs_im.append(hi_)
        ys.append(y_)
        if d == 0:
            tok_p4, ffn2_weights = weights_pass(4, y_)
            lam_in = lam_re[1] + tok_p4[0:1, 0:1]
    mixer_weights(ys[1])

    def ssm_out_fn(i, y0, y1, ut, dt):
        pre = dt * ut + y0 + y1
        yg_ = _gelu(pre)
        return [pre, yg_, yg_]

    ssm_pre, yg, ygb = _rowk(
        "ssm_out", ssm_out_fn, L, tr,
        [(ys[0], 'orow'), (ys[1], 'orow'), (proj, ('ocol', W, 1)), (ssm_d, 'vec')],
        [((L, W), F32, 'row'), ((L, W), F32, 'row'), ((L, W), BF16, 'row')], nc=ncr)

    def glu_epi(accs, rows, vecs, ri):
        z_ = accs[0] + vecs[0]
        return [z_, rows[0] * _sigmoid(z_)]

    zglu, y2 = _mm("glu", [(ygb, Wt['w_glu'], W)], L, W, tm=_div(L, 512), tn=_div(W, 512), epi=glu_epi,
                   outs=[(F32, False), (BF16, False)], rows=[(yg, 0, 0)], vecs=[b_glu])
    tnm = _div(Dq, 512)

    def merge_epi(accs, rows, vecs, ri):
        ga, gs = _sigmoid(rows[0]), _sigmoid(rows[1])
        return [accs[0], accs[1], ga * accs[0] + gs * accs[1]]

    ba, bs, merged = _mm("merge", [(attn, Wt['w_br_attn'], D), (y2, Wt['w_br_ssm'], W)], L, D, tm=tr, tn=tnm,
                         epi=merge_epi, outs=[(F32, False), (F32, False), (BF16, False)],
                         rows=[(proj, ncr, 2 * D // tnm), (proj, ncr, 3 * D // tnm)])
    mix, x2 = _mm("out_proj", [(merged, Wt['w_out'], D)], L, D, tm=tr, tn=_div(D, 1024), epi=res_epi(1.0),
                  outs=[(F32, False), (F32, False)], rows=[(x1, ncr, 0)], vecs=[g2])
    ffn2_weights(x2)
    h3 = norm_mod("norm3", x2, ng[2:3], sh3, sc3)
    a3, b3, s3, f3, x3 = ffn_fwd("ffn2", h3, x2, g3)

    def loss_fn(i, yt, tt_):
        diff = yt - tt_
        return [diff * (1.0 / D), jnp.sum(diff * diff, axis=0, keepdims=True)]

    dy, sq = _rowk("loss", loss_fn, L, tr, [(x3, 'row'), (loss_target[0], 'row')],
                   [((L, D), F32, 'row'), ((1, D), F32, 'acc')])
    loss = lax.psum(0.5 * jnp.sum(sq) / D, ("x", "y", "c"))

    def res_bwd(name, dxo, f_, gate, coef):
        rows, nrow = dxo.shape[0], gate.shape[0]

        def fn(i, dt, ft, gt):
            return [(coef * sel(i, gt)) * dt, put(i, jnp.sum(dt * ft, axis=0, keepdims=True) * coef, nrow)]

        return _rowk(name, fn, rows, tr, [(dxo, 'row'), (f_, 'row'), (gate, 'vec')],
                     [((rows, D), BF16, 'row'), ((nrow, D), F32, 'acc')])

    def swiglu_bwd_epi(accs, rows, vecs, ri):
        ds_, a_, b_ = accs[0], rows[0].astype(F32), rows[1].astype(F32)
        sg = _sigmoid(a_)
        return [ds_ * b_ * (sg * (1.0 + a_ * (1.0 - sg))), ds_ * (a_ * sg)]

    def norm_mod_bwd(name, xv, g, sh, sc, dh, dres, dres_kind):
        rows, nrow = xv.shape[0], sh.shape[0]

        def fn(i, xt, gt, sht, sct, dht, rest):
            _, vjp = jax.vjp(_norm_mod, xt, gt, sel(i, sht), sel(i, sct))
            dx_, dg_, dsh_, dsc_ = vjp(dht)
            dx_ = dx_ + (jnp.where(i >= ncr, rest, 0.0) if dres_kind == 'xrow' else rest)
            return [dx_, dg_, put(i, dsh_, nrow), put(i, dsc_, nrow)]

        return _rowk(name, fn, rows, tr,
                     [(xv, 'row'), (g, 'vec'), (sh, 'vec'), (sc, 'vec'), (dh, 'row'), (dres, dres_kind)],
                     [((rows, D), F32, 'row'), ((1, D), F32, 'acc'), ((nrow, D), F32, 'acc'), ((nrow, D), F32, 'acc')],
                     nc=ncr)

    def ffn_bwd(tag, dxo, h, a_, b_, s_, f_, gate, wg, wu, wd, on_dwd=None):
        rows = dxo.shape[0]
        df, dgate = res_bwd(tag + "_dres", dxo, f_, gate, 0.5)
        dwd = _mm(tag + "_dwd", [(s_, df, rows)], F, D, tm=_div(F, 512), tn=_div(D, 1024), ta=True, epi=ident,
                  outs=[(BF16, False)])[0].reshape(N_CHIPS, F4, D)
        if on_dwd is not None:
            on_dwd(dwd)
        da, db = _mm(tag + "_dact", [(df, wd, D)], rows, F, tm=_div(rows, 512), tn=F4, tb=True, epi=swiglu_bwd_epi,
                     outs=[(BF16, False), (BF16, False)], rows=[(a_, 0, 0), (b_, 0, 0)])
        dwg = _mm(tag + "_dwg", [(h, da, rows)], D, F, tm=_div(D, 512), tn=F4, ta=True, epi=ident,
                  outs=[(BF16, True)])[0]
        dwu = _mm(tag + "_dwu", [(h, db, rows)], D, F, tm=_div(D, 512), tn=F4, ta=True, epi=ident,
                  outs=[(BF16, True)])[0]
        dh = _mm(tag + "_dh", [(da, wg, F), (db, wu, F)], rows, D, tm=_div(rows, 768), tn=_div(D, 1024), nk=N_CHIPS,
                 tb=True, epi=ident, outs=[(F32, False)], summed=True)[0]
        return dh, dgate, dwg, dwu, dwd

    dh3, dg3, dwg2, dwu2, dwd2 = ffn_bwd("ffn2", dy, h3, a3, b3, s3, f3, g3, Wt['w_ffn2_gate'], Wt['w_ffn2_up'],
                                         Wt['w_ffn2_down'])
    tok_r1, scatter_fin1 = _scatter_split("scatter_ffn2", [dwg2, dwu2, dwd2], dg3)
    dx2, dng3, dsh3, dsc3 = norm_mod_bwd("norm3_bwd", x2, ng[2:3], sh3, sc3, dh3, dy, 'row')
    dmix, dg2 = res_bwd("mix_dres", dx2, mix, g2 + tok_r1[0:1, 0:1], 1.0)

    def dmerge_epi(accs, rows, vecs, ri):
        dm_, ba_, bs_ = accs[0], rows[0], rows[1]
        ga, gs = _sigmoid(rows[2]), _sigmoid(rows[3])
        return [dm_ * ga, dm_ * gs, dm_ * ba_ * ga * (1.0 - ga), dm_ * bs_ * gs * (1.0 - gs)]

    tnd = _div(D, 1024)
    dba, dbs, dga, dgs = _mm("dmerge", [(dmix, Wt['w_out'], D)], L, D, tm=tr, tn=tnd, tb=True, epi=dmerge_epi,
                             outs=[(BF16, False)] * 4,
                             rows=[(ba, 0, 0), (bs, 0, 0), (proj, ncr, 2 * D // tnd), (proj, ncr, 3 * D // tnd)])
    dwout = _mm("dw_out", [(merged, dmix, L)], D, D, tm=_div(D, 512), tn=_div(D, 1024), ta=True, epi=ident,
                outs=[(BF16, False)])[0].reshape(N_CHIPS, Dq, D)
    dattn = _mm("dattn", [(dba, Wt['w_br_attn'], D)], L, D, tm=_div(L, 512), tn=_div(D, 1024), tb=True, epi=ident,
                outs=[(BF16, False)])[0]
    dwba = _mm("dw_br_attn", [(attn, dba, L)], D, D, tm=_div(D, 512), tn=_div(D, 1024), ta=True, epi=ident,
               outs=[(BF16, False)])[0].reshape(N_CHIPS, Dq, D)
    dy2 = _mm("dy2", [(dbs, Wt['w_br_ssm'], D)], L, W, tm=_div(L, 512), tn=_div(W, 1024), nk=N_CHIPS, tb=True,
              epi=ident, outs=[(F32, False)])[0]
    dwbs = _mm("dw_br_ssm", [(y2, dbs, L)], W, D, tm=_div(W, 512), tn=_div(Dq, 512), ta=True, epi=ident,
               outs=[(BF16, True)])[0]

    def glu_bwd_fn(i, d2, ygt, zt):
        sz = _sigmoid(zt)
        dz_ = d2 * ygt * sz * (1.0 - sz)
        return [dz_, d2 * sz, jnp.sum(dz_, axis=0, keepdims=True)]

    dz, dyd, dbglu = _rowk("glu_bwd", glu_bwd_fn, L, tr, [(dy2, 'row'), (yg, 'row'), (zglu, 'row')],
                           [((L, W), BF16, 'row'), ((L, W), F32, 'row'), ((1, W), F32, 'acc')])

    def dssm_epi(accs, rows, vecs, ri):
        _, vjp = jax.vjp(_gelu, rows[1])
        ds_ = vjp(accs[0] + rows[0])[0]
        return [ds_, ds_]

    dssm, dssm_b = _mm("dssm", [(dz, Wt['w_glu'], W)], L, W, tm=_div(L, 512), tn=_div(W, 512), tb=True, epi=dssm_epi,
                       outs=[(F32, False), (BF16, False)], rows=[(dyd, 0, 0), (ssm_pre, 0, 0)])
    dwglu = _mm("dw_glu", [(ygb, dz, L)], W, W, tm=_div(W, 512), tn=_div(W, 1024), ta=True, epi=ident,
                outs=[(BF16, False)])[0].reshape(N_CHIPS, W // N_CHIPS, W)
    tok_r2a, scatter_fin2a = _scatter_split("scatter_mix", [dwglu, dwba, dwbs, dwout], dbglu)
    dssm_full = jnp.concatenate([jnp.zeros((Lc, W), BF16), dssm_b], axis=0)
    dus, dlam_re, dlam_im, dcoef_re, dcoef_im, dbf, dcf_re, dcf_im = [], [], [], [], [], [], [], []
    for d in range(2):
        r = _ssm_bwd("ssm_bwd%d" % d, dssm_full, hs_re[d], hs_im[d], ub, bbd[d], bbdt_re[d], bbdt_im[d],
                     cbdt_re[d], cbdt_im[d], lam_re[d] + tok_r2a[0:1, 0:1], lam_im[d], coef_re[d], coef_im[d], Lc,
                     reverse=bool(d))
        for lst, val in zip((dus, dlam_re, dlam_im, dcoef_re, dcoef_im, dbf, dcf_re, dcf_im), r):
            lst.append(val)
    dqr, dkr, dvf = _attn_bwd(qr, kr, vb, dattn, L, Lc, D)

    def prep_bwd_fn(i, qt, kt, ut, dqt, dkt, dvt, du0, du1, dst, dt, qgt, kgt, ct, st):
        live = i >= ncr
        dqt = jnp.where(live, dqt, 0.0)
        dst = jnp.where(live, dst, 0.0)
        dqs, dks = [], []
        dqg_ = jnp.zeros((1, HEAD_DIM), F32)
        dkg_ = jnp.zeros((1, HEAD_DIM), F32)
        for h in range(nh):
            hl = slice(h * HEAD_DIM, (h + 1) * HEAD_DIM)
            dn = dqt[:, hl] * ct + _rot(dqt[:, hl] * st)
            _, vjp = jax.vjp(_head_norm, qt[:, hl], qgt)
            dxh, dgh = vjp(dn)
            dqs.append(dxh)
            dqg_ = dqg_ + dgh
        for h in range(nkvh):
            hl = slice(h * HEAD_DIM, (h + 1) * HEAD_DIM)
            dn = dkt[:, hl] * ct + _rot(dkt[:, hl] * st)
            _, vjp = jax.vjp(_head_norm, kt[:, hl], kgt)
            dxh, dgh = vjp(dn)
            dks.append(dxh)
            dkg_ = dkg_ + dgh
        du_ = du0 + du1 + dst * dt
        return [jnp.concatenate(dqs, axis=1), jnp.concatenate(dks, axis=1), dvt, du_, dqg_, dkg_,
                jnp.sum(dst * ut, axis=0, keepdims=True)]

    dq_b, dk_b, dv_b, du_b, dqg, dkg, dssd = _rowk(
        "qk_prep_bwd", prep_bwd_fn, T, tr,
        [(proj, ('col', D, 1)), (proj, ('col', KV, 0)), (proj, ('col', W, 1)), (dqr, 'xrow'), (dkr, 'row'),
         (dvf, 'row'), (dus[0], 'row'), (dus[1], 'row'), (dssm, 'xrow'), (ssm_d, 'vec'), (qg, 'vec'), (kg, 'vec'),
         (cos_t, 'row'), (sin_t, 'row')],
        [((T, D), BF16, 'row'), ((T, KV), BF16, 'row'), ((T, KV), BF16, 'row'), ((T, W), BF16, 'row'),
         ((1, HEAD_DIM), F32, 'acc'), ((1, HEAD_DIM), F32, 'acc'), ((1, W), F32, 'acc')], nc=ncr)
    dgate = jnp.concatenate([jnp.zeros((Lc, 2 * D), BF16), jnp.concatenate([dga, dgs], axis=1)], axis=0)
    dproj = jnp.concatenate([dk_b, dv_b, du_b, dq_b, dgate], axis=1)
    dh2 = _mm("in_proj_dx", [(dproj, Wt['w_in'], 4 * D)], T, D, tm=_div(T, 768), tn=_div(D, 1024), nk=N_CHIPS, tb=True,
              epi=ident, outs=[(F32, False)])[0]
    dwin = _mm("in_proj_dw", [(h2, dproj, T)], D, 4 * D, tm=_div(D, 512), tn=_div(D, 1024), ta=True, epi=ident,
               outs=[(BF16, True)])[0]
    tok_r2, scatter_fin2 = _scatter_split("scatter_w_in", [dwin], dqg)
    dx1, dng2, dsh2, dsc2 = norm_mod_bwd("norm2_bwd", x1, ng[1:2] + tok_r2[0:1, 0:1], sh2, sc2, dh2, dx2, 'xrow')
    early = {}

    def start_down(dwd):
        early['tok'], early['fin'] = _scatter_split("scatter_ffn1_down", [dwd], dg2)

    dh1, dg1, dwg1, dwu1, dwd1 = ffn_bwd("ffn1", dx1, h1, a1, b1, s1, f1, g1, Wt['w_ffn1_gate'], Wt['w_ffn1_up'],
                                         Wt['w_ffn1_down'], on_dwd=start_down)
    dx0, dng1, dsh1, dsc1 = norm_mod_bwd("norm1_bwd", xc, ng[0:1] + early['tok'][0:1, 0:1], sh1, sc1, dh1, dx1, 'row')
    grad_x = dx0[Lc:][None]

    zD = jnp.zeros((1, D), F32)
    dmod_x = jnp.concatenate([dsh1[1:2], dsc1[1:2], dg1[1:2], dsh2[1:2], dsc2[1:2], dg2, dsh3, dsc3, dg3], axis=1)
    dmod_c = jnp.concatenate([dsh1[0:1], dsc1[0:1], dg1[0:1], dsh2[0:1], dsc2[0:1], zD, zD, zD, zD], axis=1)
    pieces = [dmod_x, dmod_c, dng1, dng2, dng3, dqg, dkg] + dlam_re + dlam_im + dcoef_re + dcoef_im \
        + dbf + dcf_re + dcf_im + [dssd, dbglu]
    shapes = [p_.shape for p_ in pieces]
    pack = _pack(pieces)
    RP = pack.shape[0]
    tok_small, small_gathered = _allgather_split("gather_small", pack, me, dng1)
    tok_r3, scatter_fin3 = _scatter_split("scatter_ffn1_up", [dwg1, dwu1], tok_small)
    results = {}

    def sum_group(tag, names, fin, after_work):
        sent, landed = fin(after_work)
        plane = [_sum_plane("sum_" + n, g_, rb, chip_index) for n, g_, rb in zip(names, sent, landed)]
        tok_, swapped = _swap_split("swap_" + tag, plane, plane[0])
        return tok_, (names, swapped)

    def update_group(group, after_work):
        names, swapped = group
        mine, theirs = swapped(after_work)
        for n, m_, t_ in zip(names, mine, theirs):
            results[n] = _adamw("adamw_" + n, A[n], A['m_' + n], A['v_' + n], [m_, t_])

    tok_a, grp_ffn2 = sum_group("ffn2", big[3:6], scatter_fin1, tok_r3)
    tok_b, grp_mix = sum_group("mix", big[7:11], scatter_fin2a, tok_a)
    tok_c, grp_w_in = sum_group("w_in", big[6:7], scatter_fin2, tok_b)
    update_group(grp_ffn2, tok_c)
    tok_d, grp_down = sum_group("ffn1_down", big[2:3], early['fin'], results['w_ffn2_down'][0])
    update_group(grp_mix, tok_d)
    update_group(grp_w_in, results['w_out'][0])
    update_group(grp_down, results['w_in'][0])
    allp = small_gathered(results['w_ffn1_down'][0])
    head_rows = -(-18 * D // PACK_W)
    head = allp[:, :head_rows].reshape(N_DEV, head_rows * PACK_W)
    dmx_all = head[:, :9 * D]

    def sum_rows_fn(i, t):
        s_ = t[0:1]
        for k in range(1, N_DEV):
            s_ = s_ + t[k:k + 1]
        return [s_]

    dmc_sum = _rowk("sum_dmod_c", sum_rows_fn, 1, 1, [(head[:, 9 * D:18 * D], 'vec')], [((1, 9 * D), F32, 'row')])[0]
    cots = jnp.concatenate([dmx_all, dmc_sum, jnp.zeros((7, 9 * D), F32)], axis=0)
    cots_sh = lax.dynamic_slice(cots, (0, chip * NM), (16, NM))
    part = _mm("cctx_part", [(cots_sh[8:16], wm, NM)], 8, D, tm=8, tn=_div(D, 1024), nk=NM // _div(NM, 1152), tb=True,
               epi=ident, outs=[(F32, False)], a_pro=to_bf, b_pro=to_bf)[0]
    _, cctx_gathered = _allgather_split("gather_cctx", part, me, part)

    def sum_dev_fn(i, t):
        s_ = t[0]
        for k in range(1, N_DEV):
            s_ = s_ + t[k]
        return [s_]

    tot = _rowk("sum_small", sum_dev_fn, RP, 8, [(allp, 'row3')], [((RP, PACK_W), F32, 'row')])[0]
    (t_dmod_x, t_dmod_c, t_ng1, t_ng2, t_ng3, t_qg, t_kg, t_lr0, t_lr1, t_li0, t_li1, t_kr0, t_kr1, t_ki0, t_ki1,
     t_dbf0, t_dbf1, t_dcr0, t_dcr1, t_dci0, t_dci1, t_d, t_bglu) = _unpack(tot, shapes)
    b_grad = lambda t, lo: jnp.transpose(t[:, :, lo:lo + P].reshape(G, E, P), (0, 2, 1))
    c_grad = lambda t: jnp.transpose(t.reshape(nslab, P, SLAB_GROUPS, E), (0, 2, 3, 1)).reshape(G, E, P)
    cat2 = lambda u0, u1: jnp.concatenate([u0.reshape(G, P), u1.reshape(G, P)], axis=0)
    g_are, g_aim, g_ldt = _zoh_bwd(a_re2, a_im2, ldt2, [cat2(t_lr0, t_lr1), cat2(t_li0, t_li1),
                                                         cat2(t_kr0, t_kr1), cat2(t_ki0, t_ki1)])
    g_bmod = _rowk("bmod_grad", lambda i, u0, u1: [u0 + u1], 1, 1, [(t_dmod_x, 'row'), (t_dmod_c, 'row')],
                   [((1, 9 * D), F32, 'row')])[0]
    g_wmod = _outer_sum(acts, cots_sh)
    results['w_mod'] = _adamw("adamw_w_mod", w_mod, m_w_mod, v_w_mod, [g_wmod])
    done = sum(results[n][1].reshape(-1, results[n][1].shape[-1])[0:1, 0:1] for n in list(results)) + g_are[0:1, 0:1] \
        + g_bmod[0:1, 0:1]
    tok_e, grp_up = sum_group("ffn1_up", big[0:2], scatter_fin3, done)
    parts = cctx_gathered(tok_e).reshape(N_CHIPS, 2, 8, D)[:, 0, 0]

    def cctx_fn(i, pt, ct):
        ds_ = ((pt[0:1] + pt[1:2]) + pt[2:3]) + pt[3:4]
        _, vjp = jax.vjp(lambda v: v * _sigmoid(v), ct)
        return [vjp(ds_)[0]]

    g_cctx = _rowk("cctx_grad", cctx_fn, 1, 1, [(parts, 'vec'), (c_ctx[None], 'row')], [((1, D), F32, 'row')])[0]

    ng_full =jnp.concatenate([t_ng1, t_ng2, t_ng3], axis=0)
    gsmall = {
        'c_ctx': g_cctx, 'b_mod': g_bmod, 'norm_g': lax.dynamic_slice(ng_full, (0, chip * Dq), (3, Dq)),
        'q_norm_g': t_qg, 'k_norm_g': t_kg, 'ssm_a_re': g_are, 'ssm_a_im': g_aim, 'ssm_log_dt': g_ldt,
        'ssm_b_re': jnp.stack([b_grad(t_dbf0, 0), b_grad(t_dbf1, 0)]),
        'ssm_b_im': jnp.stack([b_grad(t_dbf0, P), b_grad(t_dbf1, P)]),
        'ssm_c_re': jnp.stack([c_grad(t_dcr0), c_grad(t_dcr1)]), 'ssm_c_im': jnp.stack([c_grad(t_dci0), c_grad(t_dci1)]),
        'ssm_d': t_d, 'b_glu': t_bglu}
    sshapes = [A[n].shape for n in small]
    sres = _adamw("adamw_small", packs_wmv[0], packs_wmv[1], packs_wmv[2], [_pack([gsmall[n] for n in small])])
    update_group(grp_up, sres[0])
    sres = [_unpack(b_, sshapes) for b_ in sres]
    for k, n in enumerate(small):
        results[n] = tuple(sres[q][k] for q in range(4))

    order = ['c_ctx', 'w_mod', 'b_mod', 'norm_g', 'w_ffn1_gate', 'w_ffn1_up', 'w_ffn1_down', 'w_in', 'q_norm_g',
             'k_norm_g', 'ssm_a_re', 'ssm_a_im', 'ssm_log_dt', 'ssm_b_re', 'ssm_b_im', 'ssm_c_re', 'ssm_c_im',
             'ssm_d', 'w_glu', 'b_glu', 'w_br_attn', 'w_br_ssm', 'w_out', 'w_ffn2_gate', 'w_ffn2_up', 'w_ffn2_down']
    outs = [loss, grad_x]
    for q in range(4):
        outs += [results[n][q].reshape(A[n].shape) for n in order]
    return tuple(outs)
```

```python
import math

import jax
import jax.numpy as jnp
from jax import lax
from jax.experimental import pallas as pl
from jax.experimental.pallas import tpu as pltpu

F32 = jnp.float32
BF16 = jnp.bfloat16
MESH = pl.DeviceIdType.MESH

NORM_EPS = 1e-6
ROPE_THETA = 10000.0
GRID_W = 64
HEAD_DIM = 128
Q_PER_KV = 4
SSM_GROUP = 16
SSM_STATE = 64
ADAM_LR = 0.001
ADAM_B1 = 0.9
ADAM_B2 = 0.999
ADAM_EPS = 1e-08
ADAM_WD = 0.01
ADAM_STEP = 10

N_CHIPS = 4
N_DEV = 8
LANES = 128
SLAB_CH = 128
SLAB_GROUPS = SLAB_CH // SSM_GROUP
SLAB_ST = SLAB_GROUPS * SSM_STATE
VMEM_LIMIT_BYTES = 56 * 1024 * 1024
PACK_W = 1024


def _cparams(**kw):
    return pltpu.CompilerParams(vmem_limit_bytes=VMEM_LIMIT_BYTES, **kw)


def _div(n, pref, mult=LANES):
    t = (min(pref, n) // mult) * mult
    while t >= mult:
        if n % t == 0:
            return t
        t -= mult
    return n


def _sigmoid(x):
    return jax.nn.sigmoid(x)


def _gelu(x):
    return x * (0.5 * (1.0 + jnp.tanh(math.sqrt(2.0 / math.pi) * (x + 0.044715 * (x * x * x)))))


def _rowk(name, fn, nrows, tr, ins, outs, nc=0):
    nt = nrows // tr
    in_specs, arrays = [], []
    for arr, kind in ins:
        arrays.append(arr)
        if kind == 'row':
            in_specs.append(pl.BlockSpec((tr, arr.shape[1]), lambda i: (i, 0)))
        elif kind == 'xrow':
            in_specs.append(pl.BlockSpec((tr, arr.shape[1]), lambda i: (jnp.maximum(i - nc, 0), 0)))
        elif kind == 'orow':
            in_specs.append(pl.BlockSpec((tr, arr.shape[1]), lambda i: (i + nc, 0)))
        elif kind == 'vec':
            in_specs.append(pl.BlockSpec(arr.shape, lambda i, nd=arr.ndim: (0,) * nd))
        elif kind == 'row3':
            in_specs.append(pl.BlockSpec((arr.shape[0], tr, arr.shape[2]), lambda i: (0, i, 0)))
        elif kind == 'row1':
            in_specs.append(pl.BlockSpec((None, tr, arr.shape[2]), lambda i: (0, i, 0)))
        elif kind[0] == 'ocol':
            _, width, blk = kind
            in_specs.append(pl.BlockSpec((tr, width), lambda i, blk=blk: (i + nc, blk)))
        else:
            _, width, blk = kind
            in_specs.append(pl.BlockSpec((tr, width), lambda i, blk=blk: (i, blk)))
    out_shape, out_specs = [], []
    for shape, dtype, kind in outs:
        out_shape.append(jax.ShapeDtypeStruct(shape, dtype))
        if kind == 'row':
            out_specs.append(pl.BlockSpec((tr, shape[1]), lambda i: (i, 0)))
        elif kind == 'row1':
            out_specs.append(pl.BlockSpec((None, tr, shape[2]), lambda i: (0, i, 0)))
        else:
            out_specs.append(pl.BlockSpec(shape, lambda i, nd=len(shape): (0,) * nd))
    nin = len(ins)

    def body(*refs):
        i = pl.program_id(0)
        res = fn(i, *[r[...] for r in refs[:nin]])
        for (shape, dtype, kind), ref, val in zip(outs, refs[nin:], res):
            if kind in ('row', 'row1'):
                ref[...] = val.astype(dtype)
            else:
                @pl.when(i == 0)
                def _():
                    ref[...] = val.astype(dtype)

                @pl.when(i > 0)
                def _():
                    ref[...] += val.astype(dtype)

    return pl.pallas_call(body, name=name, grid=(nt,), in_specs=in_specs, out_specs=out_specs,
                          out_shape=out_shape, compiler_params=_cparams())(*arrays)


def _mm(name, pairs, M, N, *, tm, tn, nk=1, epi, outs, ta=False, tb=False, rows=(), vecs=(),
        a_pro=None, b_pro=None, n_outer=True, summed=False):
    nm, nn = M // tm, N // tn
    npair = len(pairs)

    def idx(f):
        if n_outer:
            return lambda j, i, k: f(i, j, k)
        return lambda i, j, k: f(i, j, k)

    in_specs, args = [], []
    for a, b, K in pairs:
        tk = K // nk
        if ta:
            in_specs.append(pl.BlockSpec((tk, tm), idx(lambda i, j, k: (k, i))))
        else:
            in_specs.append(pl.BlockSpec((tm, tk), idx(lambda i, j, k: (i, k))))
        args.append(a)
        if b.ndim == 3:
            if tb:
                per = b.shape[2] // tk
                in_specs.append(pl.BlockSpec((None, tn, tk), idx(lambda i, j, k, per=per: (k // per, j, k % per))))
            else:
                per = b.shape[2] // tn
                in_specs.append(pl.BlockSpec((None, tk, tn), idx(lambda i, j, k, per=per: (j // per, k, j % per))))
        elif tb:
            in_specs.append(pl.BlockSpec((tn, tk), idx(lambda i, j, k: (j, k))))
        else:
            in_specs.append(pl.BlockSpec((tk, tn), idx(lambda i, j, k: (k, j))))
        args.append(b)
    for arr, ro, co in rows:
        in_specs.append(pl.BlockSpec((tm, tn), idx(lambda i, j, k, ro=ro, co=co: (i + ro, j + co))))
        args.append(arr)
    for arr in vecs:
        in_specs.append(pl.BlockSpec((arr.shape[0], tn), idx(lambda i, j, k: (0, j))))
        args.append(arr)
    out_shape, out_specs = [], []
    for dtype, chunked in outs:
        if chunked:
            per = (N // N_CHIPS) // tn
            out_shape.append(jax.ShapeDtypeStruct((N_CHIPS, M, N // N_CHIPS), dtype))
            out_specs.append(pl.BlockSpec((None, tm, tn), idx(lambda i, j, k, per=per: (j // per, i, j % per))))
        else:
            out_shape.append(jax.ShapeDtypeStruct((M, N), dtype))
            out_specs.append(pl.BlockSpec((tm, tn), idx(lambda i, j, k: (i, j))))
    nacc = 1 if summed else npair
    scratch = [pltpu.VMEM((tm, tn), F32) for _ in range(nacc)] if nk > 1 else []
    nrow, nvec, nout = len(rows), len(vecs), len(outs)
    dims = (((0 if ta else 1,), (1 if tb else 0,)), ((), ()))

    def body(*refs):
        ab = refs[:2 * npair]
        row_refs = refs[2 * npair:2 * npair + nrow]
        vec_refs = refs[2 * npair + nrow:2 * npair + nrow + nvec]
        out_refs = refs[2 * npair + nrow + nvec:2 * npair + nrow + nvec + nout]
        acc_refs = refs[2 * npair + nrow + nvec + nout:]
        if n_outer:
            j, i, k = pl.program_id(0), pl.program_id(1), pl.program_id(2)
        else:
            i, j, k = pl.program_id(0), pl.program_id(1), pl.program_id(2)

        def part(p):
            av, bv = ab[2 * p][...], ab[2 * p + 1][...]
            if a_pro is not None:
                av = a_pro(av)
            if b_pro is not None:
                bv = b_pro(bv)
            return lax.dot_general(av, bv, dims, preferred_element_type=F32)

        def finish(accs):
            row_index = i * tm + lax.broadcasted_iota(jnp.int32, (tm, 1), 0)
            res = epi(accs, [r[...] for r in row_refs], [v[...] for v in vec_refs], row_index)
            for ref, val in zip(out_refs, res):
                ref[...] = val.astype(ref.dtype)

        parts = [part(p) for p in range(npair)]
        if summed:
            total = parts[0]
            for extra in parts[1:]:
                total = total + extra
            parts = [total]
        if nk == 1:
            finish(parts)
        else:
            @pl.when(k == 0)
            def _():
                for q in range(nacc):
                    acc_refs[q][...] = parts[q]

            @pl.when(jnp.logical_and(k > 0, k < nk - 1))
            def _():
                for q in range(nacc):
                    acc_refs[q][...] += parts[q]

            @pl.when(k == nk - 1)
            def _():
                finish([acc_refs[q][...] + parts[q] for q in range(nacc)])

    grid = (nn, nm, nk) if n_outer else (nm, nn, nk)
    return pl.pallas_call(body, name=name, grid=grid, in_specs=in_specs, out_specs=out_specs,
                          out_shape=out_shape, scratch_shapes=scratch, compiler_params=_cparams())(*args)


def _split3(v):
    v0 = v.astype(BF16)
    r1 = v - v0.astype(F32)
    v1 = r1.astype(BF16)
    v2 = (r1 - v1.astype(F32)).astype(BF16)
    return v0, v1, v2


def _mesh_pos():
    return lax.axis_index("x"), lax.axis_index("y"), lax.axis_index("c")


def _allgather_small(name, x):
    m, n = x.shape

    def body(x_ref, out_ref, send_sems, recv_sems, local_sem):
        xi, yi, ci = _mesh_pos()
        me, sibling = (xi, yi, ci), (xi, yi, 1 - ci)
        chips = [(1 - xi, yi), (xi, 1 - yi), (1 - xi, 1 - yi)]

        def rows(px, py, pc):
            return out_ref.at[pl.ds((4 * px + 2 * py + pc) * m, m), :]

        def copy(k, block, to, src=None):
            return pltpu.make_async_remote_copy(
                src_ref=rows(*block) if src is None else src, dst_ref=rows(*block),
                send_sem=send_sems.at[k], recv_sem=recv_sems.at[k], device_id=to, device_id_type=MESH)

        mine = pltpu.make_async_copy(x_ref, rows(*me), local_sem)
        mine.start()
        first = [copy(0, me, sibling, src=x_ref)]
        first += [copy(1 + j, me, (*chip, ci), src=x_ref) for j, chip in enumerate(chips)]
        for cp in first:
            cp.start()
        passed = [copy(4 + j, (*chip, ci), sibling) for j, chip in enumerate(chips)]
        for j, chip in enumerate(chips):
            copy(1 + j, (*chip, ci), me).wait_recv()
            passed[j].start()
        copy(0, sibling, me).wait_recv()
        for j, chip in enumerate(chips):
            copy(4 + j, (*chip, 1 - ci), me).wait_recv()
        for cp in first + passed:
            cp.wait_send()
        mine.wait()

    return pl.pallas_call(
        body, name=name, out_shape=jax.ShapeDtypeStruct((N_DEV * m, n), x.dtype),
        in_specs=[pl.BlockSpec(memory_space=pltpu.VMEM)], out_specs=pl.BlockSpec(memory_space=pltpu.VMEM),
        scratch_shapes=[pltpu.SemaphoreType.DMA((7,)), pltpu.SemaphoreType.DMA((7,)), pltpu.SemaphoreType.DMA],
        compiler_params=_cparams())(x)


_HBM = pl.BlockSpec(memory_space=pltpu.HBM)
_SEM = pl.BlockSpec(memory_space=pltpu.SEMAPHORE)
_ANY = pl.BlockSpec(memory_space=pl.ANY)
_EFFECT = pltpu.SideEffectType.DATAFLOW_SIDE_EFFECTING


def _in_hbm(v):
    return pltpu.with_memory_space_constraint(v, pltpu.HBM)


def _other_chips(xi, yi):
    return [(1 - xi, yi), (xi, 1 - yi), (1 - xi, 1 - yi)]


def _guarded(core, fn):
    if core is None:
        fn()
    else:
        pl.when(lax.axis_index("c") == core)(fn)


def _split_copies(name, srcs, lands, after, pairs, senders, receivers, ncopy):
    ns, nl = len(srcs), len(lands)
    dma = pltpu.SemaphoreType.DMA((ncopy,))
    thru = [pltpu.HBM(v.shape, v.dtype) for v in list(srcs) + list(lands)]

    def start_body(*refs):
        src_refs, land_refs = refs[:ns], refs[ns:ns + nl]
        descs = pairs(src_refs, land_refs, refs[ns + nl + 1], refs[ns + nl + 2])

        def go():
            for send, _ in descs:
                send.start()

        _guarded(senders, go)
        refs[-1][...] = jnp.zeros_like(refs[-1])

    res = pl.pallas_call(
        start_body, name=name + "_start",
        out_shape=(dma, dma, *thru, jax.ShapeDtypeStruct((8, LANES), F32)),
        in_specs=[_HBM] * (ns + nl) + [_ANY],
        out_specs=(_SEM, _SEM, *([_HBM] * (ns + nl)), pl.BlockSpec(memory_space=pltpu.VMEM)),
        input_output_aliases={k: 2 + k for k in range(ns + nl)},
        compiler_params=_cparams(has_side_effects=_EFFECT),
    )(*[_in_hbm(v) for v in srcs], *[_in_hbm(v) for v in lands], after)
    send_sems, recv_sems, token = res[0], res[1], res[-1]
    carried = res[2:2 + ns + nl]

    def finish(after_work):
        def wait_body(*refs):
            src_refs, land_refs = refs[:ns], refs[ns:ns + nl]
            descs = pairs(src_refs, land_refs, refs[ns + nl], refs[ns + nl + 1])

            def sent():
                for send, _ in descs:
                    send.wait_send()

            def landed():
                for _, recv in descs:
                    recv.wait_recv()

            _guarded(senders, sent)
            _guarded(receivers, landed)

        out = pl.pallas_call(
            wait_body, name=name + "_wait", out_shape=tuple(thru),
            in_specs=[_HBM] * (ns + nl) + [_SEM, _SEM, _ANY], out_specs=tuple([_HBM] * (ns + nl)),
            input_output_aliases={k: k for k in range(ns + nl)},
            compiler_params=_cparams(has_side_effects=_EFFECT),
        )(*carried, send_sems, recv_sems, after_work)
        return list(out[:ns]), list(out[ns:])

    return token, finish


def _cast_slot(name, w, chip_index, after):
    R, C = w.shape[1:]
    tr = _div(R, max(16, 524288 // C), mult=16)

    def body(chip_ref, w_ref, after_ref, o_ref):
        o_ref[...] = w_ref[...].astype(BF16)

    return pl.pallas_call(
        body, name=name, out_shape=jax.ShapeDtypeStruct((N_CHIPS, R, C), BF16),
        grid_spec=pltpu.PrefetchScalarGridSpec(
            num_scalar_prefetch=1, grid=(R // tr,),
            in_specs=[pl.BlockSpec((None, tr, C), lambda i, chip_ref: (0, i, 0)), _ANY],
            out_specs=pl.BlockSpec((None, tr, C), lambda i, chip_ref: (chip_ref[0], i, 0))),
        compiler_params=_cparams())(chip_index, w, after)


def _sum_plane(name, grads, landed, chip_index):
    R, C = grads.shape[1:]
    tr = _div(R, max(16, 262144 // C), mult=16)

    def body(chip_ref, own_ref, land_ref, o_ref):
        o_ref[...] = ((own_ref[...].astype(F32) + land_ref[0].astype(F32)) + land_ref[1].astype(F32)) \
            + land_ref[2].astype(F32)

    return pl.pallas_call(
        body, name=name, out_shape=jax.ShapeDtypeStruct((R, C), F32),
        grid_spec=pltpu.PrefetchScalarGridSpec(
            num_scalar_prefetch=1, grid=(R // tr,),
            in_specs=[pl.BlockSpec((None, tr, C), lambda i, chip_ref: (chip_ref[0], i, 0)),
                      pl.BlockSpec((3, tr, C), lambda i, chip_ref: (0, i, 0))],
            out_specs=pl.BlockSpec((tr, C), lambda i, chip_ref: (i, 0))),
        compiler_params=_cparams())(chip_index, grads, landed)


def _gather_split(name, lands, after):
    def pairs(src_refs, land_refs, send_sems, recv_sems):
        xi, yi, _ = _mesh_pos()
        mine = 2 * xi + yi
        out = []
        for a in range(len(lands)):
            for j, (px, py) in enumerate(_other_chips(xi, yi)):
                def to_slot(slot, a=a, j=j, px=px, py=py):
                    return pltpu.make_async_remote_copy(
                        src_ref=land_refs[a].at[mine], dst_ref=land_refs[a].at[slot], send_sem=send_sems.at[3 * a + j],
                        recv_sem=recv_sems.at[3 * a + j], device_id=(px, py, 1), device_id_type=MESH)
                out.append((to_slot(mine), to_slot(2 * px + py)))
        return out

    return _split_copies(name, [], lands, after, pairs, senders=1, receivers=1, ncopy=3 * len(lands))


def _allgather_split(name, block, me, after):
    land = lax.dynamic_update_slice(lax.empty((N_DEV,) + block.shape, block.dtype), block[None], (me, 0, 0))

    def pairs(src_refs, land_refs, send_sems, recv_sems):
        xi, yi, ci = _mesh_pos()
        mine = 4 * xi + 2 * yi + ci
        out = []
        for k in range(1, N_DEV):
            kx, ky, kc = (k >> 2) & 1, (k >> 1) & 1, k & 1
            px = 1 - xi if kx else xi
            py = 1 - yi if ky else yi
            pc = 1 - ci if kc else ci

            def to_slot(slot, k=k, px=px, py=py, pc=pc):
                return pltpu.make_async_remote_copy(
                    src_ref=land_refs[0].at[mine], dst_ref=land_refs[0].at[slot], send_sem=send_sems.at[k - 1],
                    recv_sem=recv_sems.at[k - 1], device_id=(px, py, pc), device_id_type=MESH)
            out.append((to_slot(mine), to_slot(4 * px + 2 * py + pc)))
        return out

    tok, fin = _split_copies(name, [], [land], after, pairs, senders=None, receivers=None, ncopy=N_DEV - 1)
    return tok, lambda later: fin(later)[1][0]


def _swap_split(name, arrs, after):
    lands = [lax.empty(v.shape, v.dtype) for v in arrs]

    def pairs(src_refs, land_refs, send_sems, recv_sems):
        xi, yi, ci = _mesh_pos()
        out = []
        for a in range(len(arrs)):
            cp = pltpu.make_async_remote_copy(
                src_ref=src_refs[a], dst_ref=land_refs[a], send_sem=send_sems.at[a], recv_sem=recv_sems.at[a],
                device_id=(xi, yi, 1 - ci), device_id_type=MESH)
            out.append((cp, cp))
        return out

    return _split_copies(name, arrs, lands, after, pairs, senders=None, receivers=None, ncopy=len(arrs))


def _pass_split(name, lands, after):
    def pairs(src_refs, land_refs, send_sems, recv_sems):
        xi, yi, _ = _mesh_pos()
        out = []
        for a in range(len(lands)):
            for j, (px, py) in enumerate(_other_chips(xi, yi)):
                cp = pltpu.make_async_remote_copy(
                    src_ref=land_refs[a].at[2 * px + py], dst_ref=land_refs[a].at[2 * px + py],
                    send_sem=send_sems.at[3 * a + j], recv_sem=recv_sems.at[3 * a + j],
                    device_id=(xi, yi, 0), device_id_type=MESH)
                out.append((cp, cp))
        return out

    return _split_copies(name, [], lands, after, pairs, senders=1, receivers=0, ncopy=3 * len(lands))


def _scatter_split(name, grads, after):
    lands = [lax.empty((3,) + g.shape[1:], g.dtype) for g in grads]

    def pairs(src_refs, land_refs, send_sems, recv_sems):
        xi, yi, ci = _mesh_pos()
        out = []
        for a in range(len(grads)):
            for j, (px, py) in enumerate(_other_chips(xi, yi)):
                cp = pltpu.make_async_remote_copy(
                    src_ref=src_refs[a].at[2 * px + py], dst_ref=land_refs[a].at[j], send_sem=send_sems.at[3 * a + j],
                    recv_sem=recv_sems.at[3 * a + j], device_id=(px, py, ci), device_id_type=MESH)
                out.append((cp, cp))
        return out

    return _split_copies(name, grads, lands, after, pairs, senders=None, receivers=None, ncopy=3 * len(grads))


def _gather_finish(name, lands):
    na = len(lands)

    def body(*refs):
        outs = refs[na:2 * na]
        send_sems, recv_sems = refs[2 * na:]
        xi, yi, ci = _mesh_pos()
        passes = [pltpu.make_async_remote_copy(
            src_ref=outs[a].at[2 * px + py], dst_ref=outs[a].at[2 * px + py],
            send_sem=send_sems.at[a, j], recv_sem=recv_sems.at[a, j], device_id=(xi, yi, 0), device_id_type=MESH)
            for a in range(na) for j, (px, py) in enumerate(_other_chips(xi, yi))]

        @pl.when(ci == 1)
        def _():
            for cp in passes:
                cp.start()
            for cp in passes:
                cp.wait_send()

        @pl.when(ci == 0)
        def _():
            for cp in passes:
                cp.wait_recv()

    return pl.pallas_call(
        body, name=name, out_shape=[jax.ShapeDtypeStruct(v.shape, v.dtype) for v in lands],
        in_specs=[_ANY] * na, out_specs=[_ANY] * na,
        input_output_aliases={a: a for a in range(na)},
        scratch_shapes=[pltpu.SemaphoreType.DMA((na, 3)), pltpu.SemaphoreType.DMA((na, 3))],
        compiler_params=_cparams())(*lands)


ATTN_HEADS_PER_STEP = 2


def _attn_tiles(L, Lc, D):
    tq = min(256, Lc)
    return tq, L // tq, Lc // tq, D // HEAD_DIM // Q_PER_KV


def _attn_scores(q, k):
    return lax.dot_general(q, k, (((1,), (1,)), ((), ())), preferred_element_type=F32) * (HEAD_DIM ** -0.5)


def _softmax_rows(s):
    e = jnp.exp(s - jnp.max(s, axis=-1, keepdims=True))
    return e * (1.0 / jnp.sum(e, axis=-1, keepdims=True))


def _attn_probs(q, k):
    return _softmax_rows(_attn_scores(q, k))


def _attn_fwd(qr, kr, v, L, Lc, D):
    T = L + Lc
    tq, nq, qoff, nkv = _attn_tiles(L, Lc, D)
    hp = Q_PER_KV
    ng = Q_PER_KV // hp

    def body(q_ref, k_ref, v_ref, o_ref):
        k, vv = k_ref[...], v_ref[...]
        heads = [slice(r * HEAD_DIM, (r + 1) * HEAD_DIM) for r in range(hp)]
        scores = [_attn_scores(q_ref[:, cols], k) for cols in heads]
        probs = [_softmax_rows(s) for s in scores]
        for cols, p in zip(heads, probs):
            o_ref[:, cols] = jnp.dot(p.astype(BF16), vv, preferred_element_type=F32).astype(o_ref.dtype)

    kv_spec = pl.BlockSpec((T, HEAD_DIM), lambda h, r, q: (0, h))
    return pl.pallas_call(
        body, name="attn_fwd", grid=(nkv, ng, nq),
        in_specs=[pl.BlockSpec((tq, hp * HEAD_DIM), lambda h, r, q: (q + qoff, h * ng + r)), kv_spec, kv_spec],
        out_specs=pl.BlockSpec((tq, hp * HEAD_DIM), lambda h, r, q: (q, h * ng + r)),
        out_shape=jax.ShapeDtypeStruct((L, D), BF16), compiler_params=_cparams())(qr, kr, v)


def _attn_bwd(qr, kr, v, do, L, Lc, D):
    T = L + Lc
    tq, nq, qoff, nkv = _attn_tiles(L, Lc, D)
    scale = HEAD_DIM ** -0.5
    hp = ATTN_HEADS_PER_STEP
    ng = Q_PER_KV // hp

    def body(q_ref, k_ref, v_ref, do_ref, dq_ref, dk_ref, dv_ref):
        first = jnp.logical_and(pl.program_id(1) == 0, pl.program_id(2) == 0)
        k, vv = k_ref[...], v_ref[...]
        nt_dims, tn_dims = (((1,), (1,)), ((), ())), (((0,), (0,)), ((), ()))
        heads = [slice(r * HEAD_DIM, (r + 1) * HEAD_DIM) for r in range(hp)]
        qs = [q_ref[:, cols] for cols in heads]
        douts = [do_ref[:, cols] for cols in heads]
        scores = [_attn_scores(q, k) for q in qs]
        dps = [lax.dot_general(dout, vv, nt_dims, preferred_element_type=F32) for dout in douts]
        probs = [_softmax_rows(s) for s in scores]
        dss = [(p * (dp - jnp.sum(p * dp, axis=-1, keepdims=True)) * scale).astype(BF16) for p, dp in zip(probs, dps)]
        for cols, ds in zip(heads, dss):
            dq_ref[:, cols] = jnp.dot(ds, k, preferred_element_type=F32)
        dk = dv = None
        for q, dout, p, ds in zip(qs, douts, probs, dss):
            dk_r = lax.dot_general(ds, q, tn_dims, preferred_element_type=F32)
            dv_r = lax.dot_general(p.astype(BF16), dout, tn_dims, preferred_element_type=F32)
            dk = dk_r if dk is None else dk + dk_r
            dv = dv_r if dv is None else dv + dv_r

        @pl.when(first)
        def _():
            dk_ref[...] = dk
            dv_ref[...] = dv

        @pl.when(jnp.logical_not(first))
        def _():
            dk_ref[...] += dk
            dv_ref[...] += dv

    kv_spec = pl.BlockSpec((T, HEAD_DIM), lambda h, r, q: (0, h))
    q_spec = pl.BlockSpec((tq, hp * HEAD_DIM), lambda h, r, q: (q + qoff, h * ng + r))
    o_spec = pl.BlockSpec((tq, hp * HEAD_DIM), lambda h, r, q: (q, h * ng + r))
    return pl.pallas_call(
        body, name="attn_bwd", grid=(nkv, ng, nq),
        in_specs=[q_spec, kv_spec, kv_spec, o_spec], out_specs=[o_spec, kv_spec, kv_spec],
        out_shape=[jax.ShapeDtypeStruct((L, D), F32), jax.ShapeDtypeStruct((T, D // Q_PER_KV), F32),
                   jax.ShapeDtypeStruct((T, D // Q_PER_KV), F32)],
        compiler_params=_cparams())(qr, kr, v, do)


SUB = 8


def _doubling(xr, xi, pw_re, pw_im, lanes, first_power, period, reverse):
    n = xr.shape[0]
    rows = lax.broadcasted_iota(jnp.int32, (n, 1), 0) & (period - 1)
    for k in range(period.bit_length() - 1):
        d = 1 << k
        keep = rows < period - d if reverse else rows >= d
        sr = jnp.where(keep, pltpu.roll(xr, n - d if reverse else d, 0), 0.0)
        si = jnp.where(keep, pltpu.roll(xi, n - d if reverse else d, 0), 0.0)
        pr, pi = pw_re[first_power + k:first_power + k + 1, lanes], pw_im[first_power + k:first_power + k + 1, lanes]
        xr, xi = xr + (pr * sr - pi * si), xi + (pr * si + pi * sr)
    return xr, xi


def _scan_tile(xr, xi, tb, lanes, reverse):
    pw_re, pw_im, w8_re, w8_im, wb_re, wb_im, carry_re, carry_im, sr, si = tb
    tt = xr.shape[0]
    nb = tt // SUB
    xr, xi = _doubling(xr, xi, pw_re, pw_im, lanes, 0, SUB, reverse)
    nq = sr.shape[0]
    cols = [slice(q * LANES, (q + 1) * LANES) for q in range(nq)]
    for q in range(nq):
        sr[q] = xr[:, cols[q]]
        si[q] = xi[:, cols[q]]
    last = 0 if reverse else SUB - 1
    er = jnp.concatenate([sr[q, pl.ds(last, nb, stride=SUB), :] for q in range(nq)], axis=1)
    ei = jnp.concatenate([si[q, pl.ds(last, nb, stride=SUB), :] for q in range(nq)], axis=1)
    er, ei = _doubling(er, ei, pw_re, pw_im, lanes, 3, nb, reverse)
    car, cai = carry_re[:, lanes], carry_im[:, lanes]
    wbr, wbi = wb_re[:, lanes], wb_im[:, lanes]
    er = er + (wbr * car - wbi * cai)
    ei = ei + (wbr * cai + wbi * car)
    out_block = 0 if reverse else nb - 1
    carry_re[:, lanes] = er[out_block:out_block + 1, :]
    carry_im[:, lanes] = ei[out_block:out_block + 1, :]
    blocks = lax.broadcasted_iota(jnp.int32, (nb, 1), 0)
    first = blocks == (nb - 1 if reverse else 0)
    cr = jnp.where(first, car, pltpu.roll(er, nb - 1 if reverse else 1, 0))
    ci = jnp.where(first, cai, pltpu.roll(ei, nb - 1 if reverse else 1, 0))
    for r in range(SUB):
        wr, wi = w8_re[r:r + 1, lanes], w8_im[r:r + 1, lanes]
        add_r, add_i = wr * cr - wi * ci, wr * ci + wi * cr
        for q in range(nq):
            sr[q, pl.ds(r, nb, stride=SUB), :] += add_r[:, cols[q]]
            si[q, pl.ds(r, nb, stride=SUB), :] += add_i[:, cols[q]]
    hr = jnp.concatenate([sr[q] for q in range(nq)], axis=1)
    hi = jnp.concatenate([si[q] for q in range(nq)], axis=1)
    return hr, hi, car, cai


def _scan_scratch(tt, NS):
    nb = tt // SUB
    return [pltpu.VMEM((8, NS), F32), pltpu.VMEM((8, NS), F32), pltpu.VMEM((SUB, NS), F32), pltpu.VMEM((SUB, NS), F32),
            pltpu.VMEM((nb, NS), F32), pltpu.VMEM((nb, NS), F32), pltpu.VMEM((1, NS), F32), pltpu.VMEM((1, NS), F32),
            pltpu.VMEM((SLAB_ST // LANES, tt, LANES), F32), pltpu.VMEM((SLAB_ST // LANES, tt, LANES), F32)]


def _scan_init(lr, li, tb, reverse):
    pw_re, pw_im, w8_re, w8_im, wb_re, wb_im, carry_re, carry_im, sr, _ = tb
    nb = wb_re.shape[0]
    carry_re[...] = jnp.zeros_like(carry_re)
    carry_im[...] = jnp.zeros_like(carry_im)
    pr, pi = lr, li
    for k in range(3 + nb.bit_length() - 1):
        pw_re[k:k + 1, :] = pr
        pw_im[k:k + 1, :] = pi
        if k == 3:
            l8r, l8i = pr, pi
        pr, pi = pr * pr - pi * pi, 2.0 * pr * pi
    pr, pi = lr, li
    for r in range(SUB):
        row = SUB - 1 - r if reverse else r
        w8_re[row:row + 1, :] = pr
        w8_im[row:row + 1, :] = pi
        pr, pi = pr * lr - pi * li, pr * li + pi * lr
    pr, pi = l8r, l8i
    for b in range(nb):
        row = nb - 1 - b if reverse else b
        wb_re[row:row + 1, :] = pr
        wb_im[row:row + 1, :] = pi
        pr, pi = pr * l8r - pi * l8i, pr * l8i + pi * l8r


def _ssm_tiles(T, Lc):
    tt = min(128, Lc)
    return tt, T // tt, Lc // tt


def _ssm_fwd(name, u, bbd, cbd_re, cbd_im, lam_re, lam_im, coef_re, coef_im, Lc, reverse):
    T, W = u.shape
    nslab = W // SLAB_CH
    NS = nslab * SLAB_ST
    tt, nt, nc = _ssm_tiles(T, Lc)
    if reverse:
        tile = lambda s: jnp.where(s < nc, nc - 1 - s, nt - 1 - (s - nc))
    else:
        tile = lambda s: s

    def body(u_ref, b_ref, cr_ref, ci_ref, lr_ref, li_ref, kr_ref, ki_ref, hr_ref, hi_ref, y_ref, *tb):
        @pl.when(pl.program_id(0) == 0)
        def _():
            _scan_init(lr_ref[...], li_ref[...], tb, reverse)

        for j in range(nslab):
            lanes = slice(j * SLAB_ST, (j + 1) * SLAB_ST)
            bu = jnp.dot(u_ref[:, j * SLAB_CH:(j + 1) * SLAB_CH], b_ref[j], preferred_element_type=F32)
            br, bi = bu[:, :SLAB_ST], bu[:, SLAB_ST:]
            kr, ki = kr_ref[:, lanes], ki_ref[:, lanes]
            hr, hi, _, _ = _scan_tile(kr * br - ki * bi, kr * bi + ki * br, tb, lanes, reverse)
            hrb, hib = hr.astype(BF16), hi.astype(BF16)
            hr_ref[:, lanes] = hrb
            hi_ref[:, lanes] = hib
            y_ref[:, j * SLAB_CH:(j + 1) * SLAB_CH] = (
                jnp.dot(hrb, cr_ref[j], preferred_element_type=F32)
                - jnp.dot(hib, ci_ref[j], preferred_element_type=F32))

    whole3 = lambda arr: pl.BlockSpec(arr.shape, lambda s: (0, 0, 0))
    vec = pl.BlockSpec((1, NS), lambda s: (0, 0))
    return pl.pallas_call(
        body, name=name, grid=(nt,),
        in_specs=[pl.BlockSpec((tt, W), lambda s: (tile(s), 0)), whole3(bbd), whole3(cbd_re), whole3(cbd_im),
                  vec, vec, vec, vec],
        out_specs=[pl.BlockSpec((tt, NS), lambda s: (tile(s), 0)), pl.BlockSpec((tt, NS), lambda s: (tile(s), 0)),
                   pl.BlockSpec((tt, W), lambda s: (tile(s), 0))],
        out_shape=[jax.ShapeDtypeStruct((T, NS), BF16), jax.ShapeDtypeStruct((T, NS), BF16),
                   jax.ShapeDtypeStruct((T, W), F32)],
        scratch_shapes=_scan_scratch(tt, NS),
        compiler_params=_cparams())(u, bbd, cbd_re, cbd_im, lam_re, lam_im, coef_re, coef_im)


def _ssm_bwd(name, dy, h_re, h_im, u, bbd, bbdt_re, bbdt_im, cbdt_re, cbdt_im, lam_re, lam_im,
             coef_re, coef_im, Lc, reverse):
    T, W = u.shape
    nslab = W // SLAB_CH
    NS = nslab * SLAB_ST
    tt, nt, nc = _ssm_tiles(T, Lc)
    adj_reverse = not reverse
    if reverse:
        tile = lambda s: jnp.where(s < nt - nc, nc + s, s - (nt - nc))
    else:
        tile = lambda s: nt - 1 - s

    def body(dy_ref, hr_ref, hi_ref, u_ref, b_ref, btr_ref, bti_ref, ctr_ref, cti_ref, lr_ref, li_ref,
             kr_ref, ki_ref, du_ref, dlr_ref, dli_ref, dkr_ref, dki_ref, dbf_ref, dcrf_ref, dcif_ref,
             db_ref, dcr_ref, dci_ref, *tb):
        @pl.when(pl.program_id(0) == 0)
        def _():
            _scan_init(lr_ref[...], -li_ref[...], tb, adj_reverse)
            for ref in (dlr_ref, dli_ref, dkr_ref, dki_ref, db_ref, dcr_ref, dci_ref):
                ref[...] = jnp.zeros_like(ref)

        rows = lax.broadcasted_iota(jnp.int32, (tt, 1), 0)
        far_row = tt - 1 if adj_reverse else 0
        tn_dims = (((0,), (0,)), ((), ()))
        for j in range(nslab):
            lanes = slice(j * SLAB_ST, (j + 1) * SLAB_ST)
            chans = slice(j * SLAB_CH, (j + 1) * SLAB_CH)
            dys, us = dy_ref[:, chans], u_ref[:, chans]
            er = jnp.dot(dys, ctr_ref[j], preferred_element_type=F32)
            ei = -jnp.dot(dys, cti_ref[j], preferred_element_type=F32)
            ar, ai, car, cai = _scan_tile(er, ei, tb, lanes, adj_reverse)
            shift = tt - 1 if adj_reverse else 1
            nr = jnp.where(rows == far_row, car, pltpu.roll(ar, shift, 0))
            ni = jnp.where(rows == far_row, cai, pltpu.roll(ai, shift, 0))
            hrb, hib = hr_ref[:, lanes], hi_ref[:, lanes]
            hr, hi = hrb.astype(F32), hib.astype(F32)
            dlr_ref[:, lanes] += jnp.sum(nr * hr + ni * hi, axis=0, keepdims=True)
            dli_ref[:, lanes] += jnp.sum(ni * hr - nr * hi, axis=0, keepdims=True)
            bu = jnp.dot(us, b_ref[j], preferred_element_type=F32)
            br, bi = bu[:, :SLAB_ST], bu[:, SLAB_ST:]
            dkr_ref[:, lanes] += jnp.sum(ar * br + ai * bi, axis=0, keepdims=True)
            dki_ref[:, lanes] += jnp.sum(ai * br - ar * bi, axis=0, keepdims=True)
            kr, ki = kr_ref[:, lanes], ki_ref[:, lanes]
            dbr = (ar * kr + ai * ki).astype(BF16)
            dbi = (ai * kr - ar * ki).astype(BF16)
            du_ref[:, chans] = (jnp.dot(dbr, btr_ref[j], preferred_element_type=F32)
                                + jnp.dot(dbi, bti_ref[j], preferred_element_type=F32))
            db_ref[j, :, :SLAB_ST] += lax.dot_general(us, dbr, tn_dims, preferred_element_type=F32)
            db_ref[j, :, SLAB_ST:] += lax.dot_general(us, dbi, tn_dims, preferred_element_type=F32)
            dcr_ref[j] += lax.dot_general(hrb, dys, tn_dims, preferred_element_type=F32)
            dci_ref[j] -= lax.dot_general(hib, dys, tn_dims, preferred_element_type=F32)

        @pl.when(pl.program_id(0) == nt - 1)
        def _():
            def iota(shape, axis):
                return lax.broadcasted_iota(jnp.int32, shape, axis)

            sg, ss = SSM_GROUP.bit_length() - 1, SSM_STATE.bit_length() - 1
            b_mask = (iota((SLAB_CH, SLAB_ST), 0) >> sg) == (iota((SLAB_CH, SLAB_ST), 1) >> ss)
            c_mask = (iota((SLAB_ST, SLAB_CH), 0) >> ss) == (iota((SLAB_ST, SLAB_CH), 1) >> sg)
            fold = jnp.where((iota((SLAB_ST, SSM_STATE), 0) & (SSM_STATE - 1)) == iota((SLAB_ST, SSM_STATE), 1),
                             1.0, 0.0).astype(BF16)
            fold_t = jnp.where((iota((SSM_STATE, SLAB_ST), 1) & (SSM_STATE - 1)) == iota((SSM_STATE, SLAB_ST), 0),
                               1.0, 0.0).astype(BF16)

            def exact_dot(a, b, a_is_value):
                terms = _split3(a if a_is_value else b)
                acc = None
                for t in terms:
                    part = jnp.dot(t, b, preferred_element_type=F32) if a_is_value else jnp.dot(a, t, preferred_element_type=F32)
                    acc = part if acc is None else acc + part
                return acc

            for j in range(nslab):
                dbj = db_ref[j]
                dbf_ref[j, :, :SSM_STATE] = exact_dot(jnp.where(b_mask, dbj[:, :SLAB_ST], 0.0), fold, True)
                dbf_ref[j, :, SSM_STATE:] = exact_dot(jnp.where(b_mask, dbj[:, SLAB_ST:], 0.0), fold, True)
                dcrf_ref[j] = exact_dot(fold_t, jnp.where(c_mask, dcr_ref[j], 0.0), False)
                dcif_ref[j] = exact_dot(fold_t, jnp.where(c_mask, dci_ref[j], 0.0), False)

    whole3 = lambda arr: pl.BlockSpec(arr.shape, lambda s: (0, 0, 0))
    vec = pl.BlockSpec((1, NS), lambda s: (0, 0))
    row_w = pl.BlockSpec((tt, W), lambda s: (tile(s), 0))
    row_s = pl.BlockSpec((tt, NS), lambda s: (tile(s), 0))
    dbf = jax.ShapeDtypeStruct((nslab, SLAB_CH, 2 * SSM_STATE), F32)
    dcf = jax.ShapeDtypeStruct((nslab, SSM_STATE, SLAB_CH), F32)
    return pl.pallas_call(
        body, name=name, grid=(nt,),
        in_specs=[row_w, row_s, row_s, row_w, whole3(bbd), whole3(bbdt_re), whole3(bbdt_im), whole3(cbdt_re),
                  whole3(cbdt_im), vec, vec, vec, vec],
        out_specs=[row_w, vec, vec, vec, vec, whole3(dbf), whole3(dcf), whole3(dcf)],
        out_shape=[jax.ShapeDtypeStruct((T, W), F32)] + [jax.ShapeDtypeStruct((1, NS), F32)] * 4 + [dbf, dcf, dcf],
        scratch_shapes=[pltpu.VMEM(bbd.shape, F32), pltpu.VMEM(bbdt_re.shape, F32), pltpu.VMEM(bbdt_re.shape, F32)]
        + _scan_scratch(tt, NS),
        compiler_params=_cparams())(dy, h_re, h_im, u, bbd, bbdt_re, bbdt_im, cbdt_re, cbdt_im,
                                    lam_re, lam_im, coef_re, coef_im)


def _zoh_math(a_re, a_im, log_dt):
    dt = jnp.exp(log_dt)
    mag = jnp.exp(a_re * dt)
    lb_re = mag * jnp.cos(a_im * dt)
    lb_im = mag * jnp.sin(a_im * dt)
    den = a_re * a_re + a_im * a_im
    coef_re = ((lb_re - 1.0) * a_re + lb_im * a_im) / den
    coef_im = (lb_im * a_re - (lb_re - 1.0) * a_im) / den
    return lb_re, lb_im, coef_re, coef_im


def _zoh_fwd(a_re, a_im, log_dt):
    def body(ar, ai, ld, o0, o1, o2, o3):
        for ref, val in zip((o0, o1, o2, o3), _zoh_math(ar[...], ai[...], ld[...])):
            ref[...] = val

    return pl.pallas_call(body, name="zoh_fwd", out_shape=[jax.ShapeDtypeStruct(a_re.shape, F32)] * 4,
                          compiler_params=_cparams())(a_re, a_im, log_dt)


def _zoh_bwd(a_re, a_im, log_dt, cots):
    def body(ar, ai, ld, c0, c1, c2, c3, o0, o1, o2):
        _, vjp = jax.vjp(_zoh_math, ar[...], ai[...], ld[...])
        for ref, val in zip((o0, o1, o2), vjp((c0[...], c1[...], c2[...], c3[...]))):
            ref[...] = val

    return pl.pallas_call(
        body, name="zoh_bwd",
        out_shape=[jax.ShapeDtypeStruct(a_re.shape, F32), jax.ShapeDtypeStruct(a_re.shape, F32),
                   jax.ShapeDtypeStruct(log_dt.shape, F32)],
        compiler_params=_cparams())(a_re, a_im, log_dt, *cots)


def _outer_sum(acts, cots):
    D, N = acts.shape[1], cots.shape[1]
    tm, tn = _div(D, 512), _div(N, 1152)
    dims = (((0,), (0,)), ((), ()))

    def body(a_ref, b_ref, o_ref):
        a = a_ref[...]
        aa = _split3(a * _sigmoid(a))
        bb = _split3(b_ref[...])
        acc = None
        for ia in range(3):
            for ib in range(3 - ia):
                t = lax.dot_general(aa[ia], bb[ib], dims, preferred_element_type=F32)
                acc = t if acc is None else acc + t
        o_ref[...] = acc

    return pl.pallas_call(
        body, name="mod_dw", grid=(D // tm, N // tn),
        in_specs=[pl.BlockSpec((16, tm), lambda i, j: (0, i)), pl.BlockSpec((16, tn), lambda i, j: (0, j))],
        out_specs=pl.BlockSpec((tm, tn), lambda i, j: (i, j)),
        out_shape=jax.ShapeDtypeStruct((D, N), F32), compiler_params=_cparams())(acts, cots)


def _adamw_math(w, g, m, v):
    m = ADAM_B1 * m + (1.0 - ADAM_B1) * g
    v = ADAM_B2 * v + (1.0 - ADAM_B2) * (g * g)
    m_hat = m / (1.0 - ADAM_B1 ** ADAM_STEP)
    v_hat = v / (1.0 - ADAM_B2 ** ADAM_STEP)
    delta = -ADAM_LR * (m_hat / (jnp.sqrt(v_hat) + ADAM_EPS) + ADAM_WD * w)
    return delta, m, v


def _adamw(name, w, m, v, gparts):
    R, C = w.shape[-2:]
    kind = 'row1' if w.ndim == 3 else 'row'
    tr = _div(R, max(8, 262144 // C), mult=8)

    def fn(i, wv, mv, vv, *gs):
        g = gs[0]
        for extra in gs[1:]:
            g = g + extra
        return (g,) + _adamw_math(wv, g, mv, vv)

    return _rowk(name, fn, R, tr, [(w, kind), (m, kind), (v, kind)] + [(g, 'row') for g in gparts],
                 [(w.shape, F32, kind)] * 4)


def _pack(pieces, rows_mult=8):
    flat = jnp.concatenate([p.reshape(-1).astype(F32) for p in pieces])
    unit = rows_mult * PACK_W
    total = -(-flat.shape[0] // unit) * unit
    return jnp.pad(flat, (0, total - flat.shape[0])).reshape(total // PACK_W, PACK_W)


def _unpack(buf, shapes):
    flat = buf.reshape(-1)
    out, off = [], 0
    for s in shapes:
        n = math.prod(s)
        out.append(flat[off:off + n].reshape(s))
        off += n
    return out


def _bd_expand(t):
    S, g, a, b = t.shape
    eye = jnp.eye(g, dtype=t.dtype)
    return (t[:, :, :, None, :] * eye[None, :, None, :, None]).reshape(S, g * a, g * b)


def _rope_tables(L, Lc):
    rows = L // GRID_W
    row_ids = jnp.broadcast_to(jnp.arange(rows)[:, None], (rows, GRID_W)).reshape(-1).astype(F32)
    col_ids = jnp.broadcast_to(jnp.arange(GRID_W)[None, :], (rows, GRID_W)).reshape(-1).astype(F32)
    quarter = HEAD_DIM // 4
    inv_freq = ROPE_THETA ** (-jnp.arange(quarter, dtype=F32) / quarter)
    ang_r = row_ids[:, None] * inv_freq
    ang_c = col_ids[:, None] * inv_freq
    cos = jnp.concatenate([jnp.cos(ang_r), jnp.cos(ang_r), jnp.cos(ang_c), jnp.cos(ang_c)], axis=1)
    sin = jnp.concatenate([-jnp.sin(ang_r), jnp.sin(ang_r), -jnp.sin(ang_c), jnp.sin(ang_c)], axis=1)
    cos = jnp.concatenate([jnp.ones((Lc, HEAD_DIM), F32), cos], axis=0)
    sin = jnp.concatenate([jnp.zeros((Lc, HEAD_DIM), F32), sin], axis=0)
    return cos, sin


def _rot(v):
    lane = lax.broadcasted_iota(jnp.int32, (1, HEAD_DIM), 1)
    first = (lane % (HEAD_DIM // 2)) < (HEAD_DIM // 4)
    return jnp.where(first, pltpu.roll(v, HEAD_DIM - HEAD_DIM // 4, 1), pltpu.roll(v, HEAD_DIM // 4, 1))


def _head_norm(xh, g):
    return xh * lax.rsqrt(jnp.mean(xh * xh, axis=-1, keepdims=True) + NORM_EPS) * g


def _norm_mod(xv, g, sh, sc):
    r = lax.rsqrt(jnp.mean(xv * xv, axis=-1, keepdims=True) + NORM_EPS)
    return (xv * r) * g * (1.0 + sc) + sh


def kernel(x, c, ctx, c_ctx, w_mod, b_mod, norm_g, w_ffn1_gate, w_ffn1_up, w_ffn1_down, w_in, q_norm_g, k_norm_g, ssm_a_re, ssm_a_im, ssm_log_dt, ssm_b_re, ssm_b_im, ssm_c_re, ssm_c_im, ssm_d, w_glu, b_glu, w_br_attn, w_br_ssm, w_out, w_ffn2_gate, w_ffn2_up, w_ffn2_down, loss_target, m_c_ctx, m_w_mod, m_b_mod, m_norm_g, m_w_ffn1_gate, m_w_ffn1_up, m_w_ffn1_down, m_w_in, m_q_norm_g, m_k_norm_g, m_ssm_a_re, m_ssm_a_im, m_ssm_log_dt, m_ssm_b_re, m_ssm_b_im, m_ssm_c_re, m_ssm_c_im, m_ssm_d, m_w_glu, m_b_glu, m_w_br_attn, m_w_br_ssm, m_w_out, m_w_ffn2_gate, m_w_ffn2_up, m_w_ffn2_down, v_c_ctx, v_w_mod, v_b_mod, v_norm_g, v_w_ffn1_gate, v_w_ffn1_up, v_w_ffn1_down, v_w_in, v_q_norm_g, v_k_norm_g, v_ssm_a_re, v_ssm_a_im, v_ssm_log_dt, v_ssm_b_re, v_ssm_b_im, v_ssm_c_re, v_ssm_c_im, v_ssm_d, v_w_glu, v_b_glu, v_w_br_attn, v_w_br_ssm, v_w_out, v_w_ffn2_gate, v_w_ffn2_up, v_w_ffn2_down):
    A = dict(locals())
    xi, yi, ci = _mesh_pos()
    chip = 2 * xi + yi
    me = 4 * xi + 2 * yi + ci
    L, D = x.shape[1], x.shape[2]
    Lc = ctx.shape[1]
    T = L + Lc
    F4 = w_ffn1_gate.shape[2]
    F = N_CHIPS * F4
    W, KV, Dq = D // 2, D // 4, D // 4
    G = W // SSM_GROUP
    P, E = SSM_STATE, SSM_GROUP
    NS = G * P
    nslab = W // SLAB_CH
    tr = min(256, Lc)
    ncr = Lc // tr
    assert L % tr == 0 and Lc % tr == 0 and W % SLAB_CH == 0 and D % (4 * LANES) == 0

    def sel(i, v):
        return v if v.shape[0] == 1 else jnp.where(i < ncr, v[0:1], v[1:2])

    def put(i, v, nrow):
        if nrow == 1:
            return v
        which = (i >= ncr).astype(jnp.int32)
        r2 = lax.broadcasted_iota(jnp.int32, (nrow, 1), 0)
        return jnp.where(r2 == which, jnp.broadcast_to(v, (nrow, v.shape[1])), 0.0)

    ident = lambda accs, rows, vecs, ri: [accs[0]]

    NM = w_mod.shape[2]
    first = jnp.zeros((8, D), F32).at[0].set(c[0]).at[1:4, :Dq].set(norm_g[0])
    g0 = _allgather_small("gather_c", first).reshape(N_CHIPS, 2, 8, D)
    c_all = g0[:, :, 0].reshape(N_DEV, D)
    ng = jnp.transpose(g0[:, 0, 1:4, :Dq], (1, 0, 2)).reshape(3, D)
    acts = jnp.concatenate([c_all, c_ctx[None], jnp.zeros((7, D), F32)], axis=0)
    wm = w_mod[0]
    b_shard = lax.dynamic_slice(b_mod[0], (chip * NM,), (NM,))[None]
    silu_bf = lambda a: (a * _sigmoid(a)).astype(BF16)
    to_bf = lambda b: b.astype(BF16)
    mod_part = _mm("mod_fwd", [(acts, wm, D)], 16, NM, tm=16, tn=_div(NM, 1152),
                   epi=lambda accs, rows, vecs, ri: [accs[0] + vecs[0]], outs=[(F32, False)],
                   vecs=[b_shard], a_pro=silu_bf, b_pro=to_bf)[0]
    mg = _allgather_small("gather_mod", mod_part).reshape(N_CHIPS, 2, 16, NM)[:, 0]
    mod_all = jnp.transpose(mg, (1, 0, 2)).reshape(16, N_CHIPS * NM)
    mod_x = lax.dynamic_slice(mod_all, (me, 0), (1, 9 * D))
    mod_c = jnp.where(jnp.arange(9 * D)[None] < 5 * D, mod_all[8:9], 0.0)
    modv = jnp.concatenate([mod_c, mod_x], axis=0)
    mv = lambda k: modv[:, k * D:(k + 1) * D]
    sh1, sc1, g1, sh2, sc2 = mv(0), mv(1), mv(2), mv(3), mv(4)
    g2, sh3, sc3, g3 = mv(5)[1:2], mv(6)[1:2], mv(7)[1:2], mv(8)[1:2]

    big = ['w_ffn1_gate', 'w_ffn1_up', 'w_ffn1_down', 'w_ffn2_gate', 'w_ffn2_up', 'w_ffn2_down',
           'w_in', 'w_glu', 'w_br_attn', 'w_br_ssm', 'w_out']
    row_sharded = {'w_ffn1_down', 'w_ffn2_down', 'w_glu', 'w_br_attn', 'w_out'}
    groups = [big[0:2], big[2:3], big[6:7], big[7:11], big[3:6]]
    chip_index = jnp.reshape(chip, (1,)).astype(jnp.int32)
    tok, gather_finish = modv, []
    pin = c
    for gi, names in enumerate(groups):
        tok, fin = _gather_split("gather_w%d" % gi, [_cast_slot("cast_" + n, A[n], chip_index, pin) for n in names], tok)
        gather_finish.append(fin)
        pin = tok
    ng = ng + tok[0:1, 0:1]
    Wt = {}

    def register(names, full):
        for n, gw in zip(names, full):
            Wt[n] = gw.reshape(N_CHIPS * gw.shape[1], gw.shape[2]) if n in row_sharded else gw

    def weights_ready(gi, after_work):
        _, lands = gather_finish[gi](after_work)
        register(groups[gi], _gather_finish("gather_w%d_pass" % gi, lands))

    def weights_pass(gi, after_work):
        _, lands = gather_finish[gi](after_work)
        tok_, fin_ = _pass_split("gather_w%d_pass" % gi, lands, after_work)
        return tok_, lambda later: register(groups[gi], fin_(later)[1])

    a_re2, a_im2 = ssm_a_re[0].reshape(2 * G, P), ssm_a_im[0].reshape(2 * G, P)
    ldt2 = ssm_log_dt[0].reshape(2 * G, 1)
    zoh = _zoh_fwd(a_re2, a_im2, ldt2)
    lam_re, lam_im, coef_re, coef_im = [[z[d * G:(d + 1) * G].reshape(1, NS) for d in range(2)] for z in zoh]
    bd_b = lambda b: _bd_expand(jnp.transpose(b, (0, 2, 1)).reshape(nslab, SLAB_GROUPS, E, P))
    bd_c = lambda cc: _bd_expand(jnp.transpose(cc, (0, 2, 1)).reshape(nslab, SLAB_GROUPS, P, E))
    bbd, bbdt_re, bbdt_im, cbd_re, cbd_im, cbdt_re, cbdt_im = [], [], [], [], [], [], []
    for d in range(2):
        br_, bi_ = bd_b(ssm_b_re[0, d]).astype(BF16), bd_b(ssm_b_im[0, d]).astype(BF16)
        cr_, ci_ = bd_c(ssm_c_re[0, d]).astype(BF16), bd_c(ssm_c_im[0, d]).astype(BF16)
        bbd.append(jnp.concatenate([br_, bi_], axis=2))
        bbdt_re.append(jnp.transpose(br_, (0, 2, 1)))
        bbdt_im.append(jnp.transpose(bi_, (0, 2, 1)))
        cbd_re.append(cr_)
        cbd_im.append(ci_)
        cbdt_re.append(jnp.transpose(cr_, (0, 2, 1)))
        cbdt_im.append(jnp.transpose(ci_, (0, 2, 1)))
    cos_t, sin_t = _rope_tables(L, Lc)
    qg, kg = q_norm_g, k_norm_g
    small = ['c_ctx', 'b_mod', 'norm_g', 'q_norm_g', 'k_norm_g', 'ssm_a_re', 'ssm_a_im', 'ssm_log_dt', 'ssm_b_re',
             'ssm_b_im', 'ssm_c_re', 'ssm_c_im', 'ssm_d', 'b_glu']
    packs_wmv = [_pack([A[pre + n] for n in small]) for pre in ('', 'm_', 'v_')]
    prepared = packs_wmv + [cos_t, sin_t, coef_im[0], coef_im[1]] + [
        t[d][0] for t in (bbd, bbdt_re, bbdt_im, cbd_re, cbd_im, cbdt_re, cbdt_im) for d in range(2)]
    weights_ready(0, tok + sum(t[0:1, 0:1].astype(F32) for t in prepared))

    def norm_mod(name, xv, g, sh, sc):
        rows = xv.shape[0]
        return _rowk(name, lambda i, xt, gt, sht, sct: [_norm_mod(xt, gt, sel(i, sht), sel(i, sct))],
                     rows, tr, [(xv, 'row'), (g, 'vec'), (sh, 'vec'), (sc, 'vec')], [((rows, D), BF16, 'row')])[0]

    def swiglu_epi(accs, rows, vecs, ri):
        a_, b_ = accs
        return [a_, b_, a_ * _sigmoid(a_) * b_]

    def res_epi(coef):
        def epi(accs, rows, vecs, ri):
            gate = vecs[0]
            if gate.shape[0] == 2:
                gate = jnp.where(ri < Lc, gate[0:1], gate[1:2])
            return [accs[0], rows[0] + (coef * gate) * accs[0]]
        return epi

    def ffn_fwd(tag, h, xres, gate, down_ready=None):
        rows = h.shape[0]
        a_, b_, s_ = _mm(tag + "_up", [(h, Wt['w_' + tag + '_gate'], D), (h, Wt['w_' + tag + '_up'], D)], rows, F,
                         tm=_div(rows, 512), tn=F4, epi=swiglu_epi, outs=[(BF16, False), (BF16, False), (BF16, False)])
        if down_ready is not None:
            down_ready(s_)
        f_, xo = _mm(tag + "_down", [(s_, Wt['w_' + tag + '_down'], F)], rows, D, tm=_div(rows, 768),
                     tn=_div(D, 512), epi=res_epi(0.5), outs=[(F32, False), (F32, False)],
                     rows=[(xres, 0, 0)], vecs=[gate])
        return a_, b_, s_, f_, xo

    xc = jnp.concatenate([ctx[0], x[0]], axis=0)
    h1 = norm_mod("norm1", xc, ng[0:1], sh1, sc1)
    a1, b1, s1, f1, x1 = ffn_fwd("ffn1", h1, xc, g1, down_ready=lambda s_: weights_ready(1, s_))
    weights_ready(2, x1)
    h2 = norm_mod("norm2", x1, ng[1:2], sh2, sc2)
    proj = _mm("in_proj", [(h2, Wt['w_in'], D)], T, 4 * D, tm=_div(T, 768), tn=_div(D, 1024), epi=ident,
               outs=[(F32, False)])[0]
    nh, nkvh = D // HEAD_DIM, KV // HEAD_DIM

    def prep_fn(i, kt, vt, ut, qt, qgt, kgt, ct, st):
        qs = [_head_norm(qt[:, h * HEAD_DIM:(h + 1) * HEAD_DIM], qgt) for h in range(nh)]
        ks = [_head_norm(kt[:, h * HEAD_DIM:(h + 1) * HEAD_DIM], kgt) for h in range(nkvh)]
        qs = [v * ct + _rot(v) * st for v in qs]
        ks = [v * ct + _rot(v) * st for v in ks]
        return [jnp.concatenate(qs, axis=1), jnp.concatenate(ks, axis=1), vt, ut]

    qr, kr, vb, ub = _rowk(
        "qk_prep", prep_fn, T, tr,
        [(proj, ('col', KV, 0)), (proj, ('col', KV, 1)), (proj, ('col', W, 1)), (proj, ('col', D, 1)),
         (qg, 'vec'), (kg, 'vec'), (cos_t, 'row'), (sin_t, 'row')],
        [((T, D), BF16, 'row'), ((T, KV), BF16, 'row'), ((T, KV), BF16, 'row'), ((T, W), BF16, 'row')])
    _, mixer_weights = weights_pass(3, qr)
    attn = _attn_fwd(qr, kr, vb, L, Lc, D)
    hs_re, hs_im, ys = [], [], []
    lam_in = lam_re[0]
    for d in range(2):
        hr_, hi_, y_ = _ssm_fwd("ssm_fwd%d" % d, ub, bbd[d], cbd_re[d], cbd_im[d], lam_in, lam_im[d],
                                coef_re[d], coef_im[d], Lc, reverse=bool(d))
        hs_re.append(hr_)
        hs_im.append(hi_)
        ys.append(y_)
        if d == 0:
            tok_p4, ffn2_weights = weights_pass(4, y_)
            lam_in = lam_re[1] + tok_p4[0:1, 0:1]
    mixer_weights(ys[1])

    def ssm_out_fn(i, y0, y1, ut, dt):
        pre = dt * ut + y0 + y1
        yg_ = _gelu(pre)
        return [pre, yg_, yg_]

    ssm_pre, yg, ygb = _rowk(
        "ssm_out", ssm_out_fn, L, tr,
        [(ys[0], 'orow'), (ys[1], 'orow'), (proj, ('ocol', W, 1)), (ssm_d, 'vec')],
        [((L, W), F32, 'row'), ((L, W), F32, 'row'), ((L, W), BF16, 'row')], nc=ncr)

    def glu_epi(accs, rows, vecs, ri):
        z_ = accs[0] + vecs[0]
        return [z_, rows[0] * _sigmoid(z_)]

    zglu, y2 = _mm("glu", [(ygb, Wt['w_glu'], W)], L, W, tm=_div(L, 512), tn=_div(W, 512), epi=glu_epi,
                   outs=[(F32, False), (BF16, False)], rows=[(yg, 0, 0)], vecs=[b_glu])
    tnm = _div(Dq, 512)

    def merge_epi(accs, rows, vecs, ri):
        ga, gs = _sigmoid(rows[0]), _sigmoid(rows[1])
        return [accs[0], accs[1], ga * accs[0] + gs * accs[1]]

    ba, bs, merged = _mm("merge", [(attn, Wt['w_br_attn'], D), (y2, Wt['w_br_ssm'], W)], L, D, tm=tr, tn=tnm,
                         epi=merge_epi, outs=[(F32, False), (F32, False), (BF16, False)],
                         rows=[(proj, ncr, 2 * D // tnm), (proj, ncr, 3 * D // tnm)])
    mix, x2 = _mm("out_proj", [(merged, Wt['w_out'], D)], L, D, tm=tr, tn=_div(D, 1024), epi=res_epi(1.0),
                  outs=[(F32, False), (F32, False)], rows=[(x1, ncr, 0)], vecs=[g2])
    ffn2_weights(x2)
    h3 = norm_mod("norm3", x2, ng[2:3], sh3, sc3)
    a3, b3, s3, f3, x3 = ffn_fwd("ffn2", h3, x2, g3)

    def loss_fn(i, yt, tt_):
        diff = yt - tt_
        return [diff * (1.0 / D), jnp.sum(diff * diff, axis=0, keepdims=True)]

    dy, sq = _rowk("loss", loss_fn, L, tr, [(x3, 'row'), (loss_target[0], 'row')],
                   [((L, D), F32, 'row'), ((1, D), F32, 'acc')])
    loss = lax.psum(0.5 * jnp.sum(sq) / D, ("x", "y", "c"))

    def res_bwd(name, dxo, f_, gate, coef):
        rows, nrow = dxo.shape[0], gate.shape[0]

        def fn(i, dt, ft, gt):
            return [(coef * sel(i, gt)) * dt, put(i, jnp.sum(dt * ft, axis=0, keepdims=True) * coef, nrow)]

        return _rowk(name, fn, rows, tr, [(dxo, 'row'), (f_, 'row'), (gate, 'vec')],
                     [((rows, D), BF16, 'row'), ((nrow, D), F32, 'acc')])

    def swiglu_bwd_epi(accs, rows, vecs, ri):
        ds_, a_, b_ = accs[0], rows[0].astype(F32), rows[1].astype(F32)
        sg = _sigmoid(a_)
        return [ds_ * b_ * (sg * (1.0 + a_ * (1.0 - sg))), ds_ * (a_ * sg)]

    def norm_mod_bwd(name, xv, g, sh, sc, dh, dres, dres_kind):
        rows, nrow = xv.shape[0], sh.shape[0]

        def fn(i, xt, gt, sht, sct, dht, rest):
            _, vjp = jax.vjp(_norm_mod, xt, gt, sel(i, sht), sel(i, sct))
            dx_, dg_, dsh_, dsc_ = vjp(dht)
            dx_ = dx_ + (jnp.where(i >= ncr, rest, 0.0) if dres_kind == 'xrow' else rest)
            return [dx_, dg_, put(i, dsh_, nrow), put(i, dsc_, nrow)]

        return _rowk(name, fn, rows, tr,
                     [(xv, 'row'), (g, 'vec'), (sh, 'vec'), (sc, 'vec'), (dh, 'row'), (dres, dres_kind)],
                     [((rows, D), F32, 'row'), ((1, D), F32, 'acc'), ((nrow, D), F32, 'acc'), ((nrow, D), F32, 'acc')],
                     nc=ncr)

    def ffn_bwd(tag, dxo, h, a_, b_, s_, f_, gate, wg, wu, wd, on_dwd=None):
        rows = dxo.shape[0]
        df, dgate = res_bwd(tag + "_dres", dxo, f_, gate, 0.5)
        dwd = _mm(tag + "_dwd", [(s_, df, rows)], F, D, tm=_div(F, 512), tn=_div(D, 1024), ta=True, epi=ident,
                  outs=[(BF16, False)])[0].reshape(N_CHIPS, F4, D)
        if on_dwd is not None:
            on_dwd(dwd)
        da, db = _mm(tag + "_dact", [(df, wd, D)], rows, F, tm=_div(rows, 512), tn=F4, tb=True, epi=swiglu_bwd_epi,
                     outs=[(BF16, False), (BF16, False)], rows=[(a_, 0, 0), (b_, 0, 0)])
        dwg = _mm(tag + "_dwg", [(h, da, rows)], D, F, tm=_div(D, 512), tn=F4, ta=True, epi=ident,
                  outs=[(BF16, True)])[0]
        dwu = _mm(tag + "_dwu", [(h, db, rows)], D, F, tm=_div(D, 512), tn=F4, ta=True, epi=ident,
                  outs=[(BF16, True)])[0]
        dh = _mm(tag + "_dh", [(da, wg, F), (db, wu, F)], rows, D, tm=_div(rows, 768), tn=_div(D, 1024), nk=N_CHIPS,
                 tb=True, epi=ident, outs=[(F32, False)], summed=True)[0]
        return dh, dgate, dwg, dwu, dwd

    dh3, dg3, dwg2, dwu2, dwd2 = ffn_bwd("ffn2", dy, h3, a3, b3, s3, f3, g3, Wt['w_ffn2_gate'], Wt['w_ffn2_up'],
                                         Wt['w_ffn2_down'])
    tok_r1, scatter_fin1 = _scatter_split("scatter_ffn2", [dwg2, dwu2, dwd2], dg3)
    dx2, dng3, dsh3, dsc3 = norm_mod_bwd("norm3_bwd", x2, ng[2:3], sh3, sc3, dh3, dy, 'row')
    dmix, dg2 = res_bwd("mix_dres", dx2, mix, g2 + tok_r1[0:1, 0:1], 1.0)

    def dmerge_epi(accs, rows, vecs, ri):
        dm_, ba_, bs_ = accs[0], rows[0], rows[1]
        ga, gs = _sigmoid(rows[2]), _sigmoid(rows[3])
        return [dm_ * ga, dm_ * gs, dm_ * ba_ * ga * (1.0 - ga), dm_ * bs_ * gs * (1.0 - gs)]

    tnd = _div(D, 1024)
    dba, dbs, dga, dgs = _mm("dmerge", [(dmix, Wt['w_out'], D)], L, D, tm=tr, tn=tnd, tb=True, epi=dmerge_epi,
                             outs=[(BF16, False)] * 4,
                             rows=[(ba, 0, 0), (bs, 0, 0), (proj, ncr, 2 * D // tnd), (proj, ncr, 3 * D // tnd)])
    dwout = _mm("dw_out", [(merged, dmix, L)], D, D, tm=_div(D, 512), tn=_div(D, 1024), ta=True, epi=ident,
                outs=[(BF16, False)])[0].reshape(N_CHIPS, Dq, D)
    dattn = _mm("dattn", [(dba, Wt['w_br_attn'], D)], L, D, tm=_div(L, 512), tn=_div(D, 1024), tb=True, epi=ident,
                outs=[(BF16, False)])[0]
    dwba = _mm("dw_br_attn", [(attn, dba, L)], D, D, tm=_div(D, 512), tn=_div(D, 1024), ta=True, epi=ident,
               outs=[(BF16, False)])[0].reshape(N_CHIPS, Dq, D)
    dy2 = _mm("dy2", [(dbs, Wt['w_br_ssm'], D)], L, W, tm=_div(L, 512), tn=_div(W, 1024), nk=N_CHIPS, tb=True,
              epi=ident, outs=[(F32, False)])[0]
    dwbs = _mm("dw_br_ssm", [(y2, dbs, L)], W, D, tm=_div(W, 512), tn=_div(Dq, 512), ta=True, epi=ident,
               outs=[(BF16, True)])[0]

    def glu_bwd_fn(i, d2, ygt, zt):
        sz = _sigmoid(zt)
        dz_ = d2 * ygt * sz * (1.0 - sz)
        return [dz_, d2 * sz, jnp.sum(dz_, axis=0, keepdims=True)]

    dz, dyd, dbglu = _rowk("glu_bwd", glu_bwd_fn, L, tr, [(dy2, 'row'), (yg, 'row'), (zglu, 'row')],
                           [((L, W), BF16, 'row'), ((L, W), F32, 'row'), ((1, W), F32, 'acc')])

    def dssm_epi(accs, rows, vecs, ri):
        _, vjp = jax.vjp(_gelu, rows[1])
        ds_ = vjp(accs[0] + rows[0])[0]
        return [ds_, ds_]

    dssm, dssm_b = _mm("dssm", [(dz, Wt['w_glu'], W)], L, W, tm=_div(L, 512), tn=_div(W, 512), tb=True, epi=dssm_epi,
                       outs=[(F32, False), (BF16, False)], rows=[(dyd, 0, 0), (ssm_pre, 0, 0)])
    dwglu = _mm("dw_glu", [(ygb, dz, L)], W, W, tm=_div(W, 512), tn=_div(W, 1024), ta=True, epi=ident,
                outs=[(BF16, False)])[0].reshape(N_CHIPS, W // N_CHIPS, W)
    tok_r2a, scatter_fin2a = _scatter_split("scatter_mix", [dwglu, dwba, dwbs, dwout], dbglu)
    dssm_full = jnp.concatenate([jnp.zeros((Lc, W), BF16), dssm_b], axis=0)
    dus, dlam_re, dlam_im, dcoef_re, dcoef_im, dbf, dcf_re, dcf_im = [], [], [], [], [], [], [], []
    for d in range(2):
        r = _ssm_bwd("ssm_bwd%d" % d, dssm_full, hs_re[d], hs_im[d], ub, bbd[d], bbdt_re[d], bbdt_im[d],
                     cbdt_re[d], cbdt_im[d], lam_re[d] + tok_r2a[0:1, 0:1], lam_im[d], coef_re[d], coef_im[d], Lc,
                     reverse=bool(d))
        for lst, val in zip((dus, dlam_re, dlam_im, dcoef_re, dcoef_im, dbf, dcf_re, dcf_im), r):
            lst.append(val)
    dqr, dkr, dvf = _attn_bwd(qr, kr, vb, dattn, L, Lc, D)

    def prep_bwd_fn(i, qt, kt, ut, dqt, dkt, dvt, du0, du1, dst, dt, qgt, kgt, ct, st):
        live = i >= ncr
        dqt = jnp.where(live, dqt, 0.0)
        dst = jnp.where(live, dst, 0.0)
        dqs, dks = [], []
        dqg_ = jnp.zeros((1, HEAD_DIM), F32)
        dkg_ = jnp.zeros((1, HEAD_DIM), F32)
        for h in range(nh):
            hl = slice(h * HEAD_DIM, (h + 1) * HEAD_DIM)
            dn = dqt[:, hl] * ct + _rot(dqt[:, hl] * st)
            _, vjp = jax.vjp(_head_norm, qt[:, hl], qgt)
            dxh, dgh = vjp(dn)
            dqs.append(dxh)
            dqg_ = dqg_ + dgh
        for h in range(nkvh):
            hl = slice(h * HEAD_DIM, (h + 1) * HEAD_DIM)
            dn = dkt[:, hl] * ct + _rot(dkt[:, hl] * st)
            _, vjp = jax.vjp(_head_norm, kt[:, hl], kgt)
            dxh, dgh = vjp(dn)
            dks.append(dxh)
            dkg_ = dkg_ + dgh
        du_ = du0 + du1 + dst * dt
        return [jnp.concatenate(dqs, axis=1), jnp.concatenate(dks, axis=1), dvt, du_, dqg_, dkg_,
                jnp.sum(dst * ut, axis=0, keepdims=True)]

    dq_b, dk_b, dv_b, du_b, dqg, dkg, dssd = _rowk(
        "qk_prep_bwd", prep_bwd_fn, T, tr,
        [(proj, ('col', D, 1)), (proj, ('col', KV, 0)), (proj, ('col', W, 1)), (dqr, 'xrow'), (dkr, 'row'),
         (dvf, 'row'), (dus[0], 'row'), (dus[1], 'row'), (dssm, 'xrow'), (ssm_d, 'vec'), (qg, 'vec'), (kg, 'vec'),
         (cos_t, 'row'), (sin_t, 'row')],
        [((T, D), BF16, 'row'), ((T, KV), BF16, 'row'), ((T, KV), BF16, 'row'), ((T, W), BF16, 'row'),
         ((1, HEAD_DIM), F32, 'acc'), ((1, HEAD_DIM), F32, 'acc'), ((1, W), F32, 'acc')], nc=ncr)
    dgate = jnp.concatenate([jnp.zeros((Lc, 2 * D), BF16), jnp.concatenate([dga, dgs], axis=1)], axis=0)
    dproj = jnp.concatenate([dk_b, dv_b, du_b, dq_b, dgate], axis=1)
    dh2 = _mm("in_proj_dx", [(dproj, Wt['w_in'], 4 * D)], T, D, tm=_div(T, 768), tn=_div(D, 1024), nk=N_CHIPS, tb=True,
              epi=ident, outs=[(F32, False)])[0]
    dwin = _mm("in_proj_dw", [(h2, dproj, T)], D, 4 * D, tm=_div(D, 512), tn=_div(D, 1024), ta=True, epi=ident,
               outs=[(BF16, True)])[0]
    tok_r2, scatter_fin2 = _scatter_split("scatter_w_in", [dwin], dqg)
    dx1, dng2, dsh2, dsc2 = norm_mod_bwd("norm2_bwd", x1, ng[1:2] + tok_r2[0:1, 0:1], sh2, sc2, dh2, dx2, 'xrow')
    early = {}

    def start_down(dwd):
        early['tok'], early['fin'] = _scatter_split("scatter_ffn1_down", [dwd], dg2)

    dh1, dg1, dwg1, dwu1, dwd1 = ffn_bwd("ffn1", dx1, h1, a1, b1, s1, f1, g1, Wt['w_ffn1_gate'], Wt['w_ffn1_up'],
                                         Wt['w_ffn1_down'], on_dwd=start_down)
    dx0, dng1, dsh1, dsc1 = norm_mod_bwd("norm1_bwd", xc, ng[0:1] + early['tok'][0:1, 0:1], sh1, sc1, dh1, dx1, 'row')
    grad_x = dx0[Lc:][None]

    zD = jnp.zeros((1, D), F32)
    dmod_x = jnp.concatenate([dsh1[1:2], dsc1[1:2], dg1[1:2], dsh2[1:2], dsc2[1:2], dg2, dsh3, dsc3, dg3], axis=1)
    dmod_c = jnp.concatenate([dsh1[0:1], dsc1[0:1], dg1[0:1], dsh2[0:1], dsc2[0:1], zD, zD, zD, zD], axis=1)
    pieces = [dmod_x, dmod_c, dng1, dng2, dng3, dqg, dkg] + dlam_re + dlam_im + dcoef_re + dcoef_im \
        + dbf + dcf_re + dcf_im + [dssd, dbglu]
    shapes = [p_.shape for p_ in pieces]
    pack = _pack(pieces)
    RP = pack.shape[0]
    tok_small, small_gathered = _allgather_split("gather_small", pack, me, dng1)
    tok_r3, scatter_fin3 = _scatter_split("scatter_ffn1_up", [dwg1, dwu1], tok_small)
    results = {}

    def sum_group(tag, names, fin, after_work):
        sent, landed = fin(after_work)
        plane = [_sum_plane("sum_" + n, g_, rb, chip_index) for n, g_, rb in zip(names, sent, landed)]
        tok_, swapped = _swap_split("swap_" + tag, plane, plane[0])
        return tok_, (names, swapped)

    def update_group(group, after_work):
        names, swapped = group
        mine, theirs = swapped(after_work)
        for n, m_, t_ in zip(names, mine, theirs):
            results[n] = _adamw("adamw_" + n, A[n], A['m_' + n], A['v_' + n], [m_, t_])

    tok_a, grp_ffn2 = sum_group("ffn2", big[3:6], scatter_fin1, tok_r3)
    tok_b, grp_mix = sum_group("mix", big[7:11], scatter_fin2a, tok_a)
    tok_c, grp_w_in = sum_group("w_in", big[6:7], scatter_fin2, tok_b)
    update_group(grp_ffn2, tok_c)
    tok_d, grp_down = sum_group("ffn1_down", big[2:3], early['fin'], results['w_ffn2_down'][0])
    update_group(grp_mix, tok_d)
    update_group(grp_w_in, results['w_out'][0])
    update_group(grp_down, results['w_in'][0])
    allp = small_gathered(results['w_ffn1_down'][0])
    head_rows = -(-18 * D // PACK_W)
    head = allp[:, :head_rows].reshape(N_DEV, head_rows * PACK_W)
    dmx_all = head[:, :9 * D]

    def sum_rows_fn(i, t):
        s_ = t[0:1]
        for k in range(1, N_DEV):
            s_ = s_ + t[k:k + 1]
        return [s_]

    dmc_sum = _rowk("sum_dmod_c", sum_rows_fn, 1, 1, [(head[:, 9 * D:18 * D], 'vec')], [((1, 9 * D), F32, 'row')])[0]
    cots = jnp.concatenate([dmx_all, dmc_sum, jnp.zeros((7, 9 * D), F32)], axis=0)
    cots_sh = lax.dynamic_slice(cots, (0, chip * NM), (16, NM))
    part = _mm("cctx_part", [(cots_sh[8:16], wm, NM)], 8, D, tm=8, tn=_div(D, 1024), nk=NM // _div(NM, 1152), tb=True,
               epi=ident, outs=[(F32, False)], a_pro=to_bf, b_pro=to_bf)[0]
    _, cctx_gathered = _allgather_split("gather_cctx", part, me, part)

    def sum_dev_fn(i, t):
        s_ = t[0]
        for k in range(1, N_DEV):
            s_ = s_ + t[k]
        return [s_]

    tot = _rowk("sum_small", sum_dev_fn, RP, 8, [(allp, 'row3')], [((RP, PACK_W), F32, 'row')])[0]
    (t_dmod_x, t_dmod_c, t_ng1, t_ng2, t_ng3, t_qg, t_kg, t_lr0, t_lr1, t_li0, t_li1, t_kr0, t_kr1, t_ki0, t_ki1,
     t_dbf0, t_dbf1, t_dcr0, t_dcr1, t_dci0, t_dci1, t_d, t_bglu) = _unpack(tot, shapes)
    b_grad = lambda t, lo: jnp.transpose(t[:, :, lo:lo + P].reshape(G, E, P), (0, 2, 1))
    c_grad = lambda t: jnp.transpose(t.reshape(nslab, P, SLAB_GROUPS, E), (0, 2, 3, 1)).reshape(G, E, P)
    cat2 = lambda u0, u1: jnp.concatenate([u0.reshape(G, P), u1.reshape(G, P)], axis=0)
    g_are, g_aim, g_ldt = _zoh_bwd(a_re2, a_im2, ldt2, [cat2(t_lr0, t_lr1), cat2(t_li0, t_li1),
                                                         cat2(t_kr0, t_kr1), cat2(t_ki0, t_ki1)])
    g_bmod = _rowk("bmod_grad", lambda i, u0, u1: [u0 + u1], 1, 1, [(t_dmod_x, 'row'), (t_dmod_c, 'row')],
                   [((1, 9 * D), F32, 'row')])[0]
    g_wmod = _outer_sum(acts, cots_sh)
    results['w_mod'] = _adamw("adamw_w_mod", w_mod, m_w_mod, v_w_mod, [g_wmod])
    done = sum(results[n][1].reshape(-1, results[n][1].shape[-1])[0:1, 0:1] for n in list(results)) + g_are[0:1, 0:1] \
        + g_bmod[0:1, 0:1]
    tok_e, grp_up = sum_group("ffn1_up", big[0:2], scatter_fin3, done)
    parts = cctx_gathered(tok_e).reshape(N_CHIPS, 2, 8, D)[:, 0, 0]

    def cctx_fn(i, pt, ct):
        ds_ = ((pt[0:1] + pt[1:2]) + pt[2:3]) + pt[3:4]
        _, vjp = jax.vjp(lambda v: v * _sigmoid(v), ct)
        return [vjp(ds_)[0]]

    g_cctx = _rowk("cctx_grad", cctx_fn, 1, 1, [(parts, 'vec'), (c_ctx[None], 'row')], [((1, D), F32, 'row')])[0]

    ng_full =jnp.concatenate([t_ng1, t_ng2, t_ng3], axis=0)
    gsmall = {
        'c_ctx': g_cctx, 'b_mod': g_bmod, 'norm_g': lax.dynamic_slice(ng_full, (0, chip * Dq), (3, Dq)),
        'q_norm_g': t_qg, 'k_norm_g': t_kg, 'ssm_a_re': g_are, 'ssm_a_im': g_aim, 'ssm_log_dt': g_ldt,
        'ssm_b_re': jnp.stack([b_grad(t_dbf0, 0), b_grad(t_dbf1, 0)]),
        'ssm_b_im': jnp.stack([b_grad(t_dbf0, P), b_grad(t_dbf1, P)]),
        'ssm_c_re': jnp.stack([c_grad(t_dcr0), c_grad(t_dcr1)]), 'ssm_c_im': jnp.stack([c_grad(t_dci0), c_grad(t_dci1)]),
        'ssm_d': t_d, 'b_glu': t_bglu}
    sshapes = [A[n].shape for n in small]
    sres = _adamw("adamw_small", packs_wmv[0], packs_wmv[1], packs_wmv[2], [_pack([gsmall[n] for n in small])])
    update_group(grp_up, sres[0])
    sres = [_unpack(b_, sshapes) for b_ in sres]
    for k, n in enumerate(small):
        results[n] = tuple(sres[q][k] for q in range(4))

    order = ['c_ctx', 'w_mod', 'b_mod', 'norm_g', 'w_ffn1_gate', 'w_ffn1_up', 'w_ffn1_down', 'w_in', 'q_norm_g',
             'k_norm_g', 'ssm_a_re', 'ssm_a_im', 'ssm_log_dt', 'ssm_b_re', 'ssm_b_im', 'ssm_c_re', 'ssm_c_im',
             'ssm_d', 'w_glu', 'b_glu', 'w_br_attn', 'w_br_ssm', 'w_out', 'w_ffn2_gate', 'w_ffn2_up', 'w_ffn2_down']
    outs = [loss, grad_x]
    for q in range(4):
        outs += [results[n][q].reshape(A[n].shape) for n in order]
    return tuple(outs)
```

```python
import math

import jax
import jax.numpy as jnp
from jax import lax
from jax.experimental import pallas as pl
from jax.experimental.pallas import tpu as pltpu

F32 = jnp.float32
BF16 = jnp.bfloat16
MESH = pl.DeviceIdType.MESH

NORM_EPS = 1e-6
ROPE_THETA = 10000.0
GRID_W = 64
HEAD_DIM = 128
Q_PER_KV = 4
SSM_GROUP = 16
SSM_STATE = 64
ADAM_LR = 0.001
ADAM_B1 = 0.9
ADAM_B2 = 0.999
ADAM_EPS = 1e-08
ADAM_WD = 0.01
ADAM_STEP = 10

N_CHIPS = 4
N_DEV = 8
LANES = 128
SLAB_CH = 128
SLAB_GROUPS = SLAB_CH // SSM_GROUP
SLAB_ST = SLAB_GROUPS * SSM_STATE
VMEM_LIMIT_BYTES = 56 * 1024 * 1024
PACK_W = 1024


def _cparams(**kw):
    return pltpu.CompilerParams(vmem_limit_bytes=VMEM_LIMIT_BYTES, **kw)


def _div(n, pref, mult=LANES):
    t = (min(pref, n) // mult) * mult
    while t >= mult:
        if n % t == 0:
            return t
        t -= mult
    return n


def _sigmoid(x):
    return jax.nn.sigmoid(x)


def _gelu(x):
    return x * (0.5 * (1.0 + jnp.tanh(math.sqrt(2.0 / math.pi) * (x + 0.044715 * (x * x * x)))))


def _rowk(name, fn, nrows, tr, ins, outs, nc=0):
    nt = nrows // tr
    in_specs, arrays = [], []
    for arr, kind in ins:
        arrays.append(arr)
        if kind == 'row':
            in_specs.append(pl.BlockSpec((tr, arr.shape[1]), lambda i: (i, 0)))
        elif kind == 'xrow':
            in_specs.append(pl.BlockSpec((tr, arr.shape[1]), lambda i: (jnp.maximum(i - nc, 0), 0)))
        elif kind == 'orow':
            in_specs.append(pl.BlockSpec((tr, arr.shape[1]), lambda i: (i + nc, 0)))
        elif kind == 'vec':
            in_specs.append(pl.BlockSpec(arr.shape, lambda i, nd=arr.ndim: (0,) * nd))
        elif kind == 'row3':
            in_specs.append(pl.BlockSpec((arr.shape[0], tr, arr.shape[2]), lambda i: (0, i, 0)))
        elif kind == 'row1':
            in_specs.append(pl.BlockSpec((None, tr, arr.shape[2]), lambda i: (0, i, 0)))
        elif kind[0] == 'ocol':
            _, width, blk = kind
            in_specs.append(pl.BlockSpec((tr, width), lambda i, blk=blk: (i + nc, blk)))
        else:
            _, width, blk = kind
            in_specs.append(pl.BlockSpec((tr, width), lambda i, blk=blk: (i, blk)))
    out_shape, out_specs = [], []
    for shape, dtype, kind in outs:
        out_shape.append(jax.ShapeDtypeStruct(shape, dtype))
        if kind == 'row':
            out_specs.append(pl.BlockSpec((tr, shape[1]), lambda i: (i, 0)))
        elif kind == 'row1':
            out_specs.append(pl.BlockSpec((None, tr, shape[2]), lambda i: (0, i, 0)))
        else:
            out_specs.append(pl.BlockSpec(shape, lambda i, nd=len(shape): (0,) * nd))
    nin = len(ins)

    def body(*refs):
        i = pl.program_id(0)
        res = fn(i, *[r[...] for r in refs[:nin]])
        for (shape, dtype, kind), ref, val in zip(outs, refs[nin:], res):
            if kind in ('row', 'row1'):
                ref[...] = val.astype(dtype)
            else:
                @pl.when(i == 0)
                def _():
                    ref[...] = val.astype(dtype)

                @pl.when(i > 0)
                def _():
                    ref[...] += val.astype(dtype)

    return pl.pallas_call(body, name=name, grid=(nt,), in_specs=in_specs, out_specs=out_specs,
                          out_shape=out_shape, compiler_params=_cparams())(*arrays)


def _mm(name, pairs, M, N, *, tm, tn, nk=1, epi, outs, ta=False, tb=False, rows=(), vecs=(),
        a_pro=None, b_pro=None, n_outer=True, summed=False):
    nm, nn = M // tm, N // tn
    npair = len(pairs)

    def idx(f):
        if n_outer:
            return lambda j, i, k: f(i, j, k)
        return lambda i, j, k: f(i, j, k)

    in_specs, args = [], []
    for a, b, K in pairs:
        tk = K // nk
        if ta:
            in_specs.append(pl.BlockSpec((tk, tm), idx(lambda i, j, k: (k, i))))
        else:
            in_specs.append(pl.BlockSpec((tm, tk), idx(lambda i, j, k: (i, k))))
        args.append(a)
        if b.ndim == 3:
            if tb:
                per = b.shape[2] // tk
                in_specs.append(pl.BlockSpec((None, tn, tk), idx(lambda i, j, k, per=per: (k // per, j, k % per))))
            else:
                per = b.shape[2] // tn
                in_specs.append(pl.BlockSpec((None, tk, tn), idx(lambda i, j, k, per=per: (j // per, k, j % per))))
        elif tb:
            in_specs.append(pl.BlockSpec((tn, tk), idx(lambda i, j, k: (j, k))))
        else:
            in_specs.append(pl.BlockSpec((tk, tn), idx(lambda i, j, k: (k, j))))
        args.append(b)
    for arr, ro, co in rows:
        in_specs.append(pl.BlockSpec((tm, tn), idx(lambda i, j, k, ro=ro, co=co: (i + ro, j + co))))
        args.append(arr)
    for arr in vecs:
        in_specs.append(pl.BlockSpec((arr.shape[0], tn), idx(lambda i, j, k: (0, j))))
        args.append(arr)
    out_shape, out_specs = [], []
    for dtype, chunked in outs:
        if chunked:
            per = (N // N_CHIPS) // tn
            out_shape.append(jax.ShapeDtypeStruct((N_CHIPS, M, N // N_CHIPS), dtype))
            out_specs.append(pl.BlockSpec((None, tm, tn), idx(lambda i, j, k, per=per: (j // per, i, j % per))))
        else:
            out_shape.append(jax.ShapeDtypeStruct((M, N), dtype))
            out_specs.append(pl.BlockSpec((tm, tn), idx(lambda i, j, k: (i, j))))
    nacc = 1 if summed else npair
    scratch = [pltpu.VMEM((tm, tn), F32) for _ in range(nacc)] if nk > 1 else []
    nrow, nvec, nout = len(rows), len(vecs), len(outs)
    dims = (((0 if ta else 1,), (1 if tb else 0,)), ((), ()))

    def body(*refs):
        ab = refs[:2 * npair]
        row_refs = refs[2 * npair:2 * npair + nrow]
        vec_refs = refs[2 * npair + nrow:2 * npair + nrow + nvec]
        out_refs = refs[2 * npair + nrow + nvec:2 * npair + nrow + nvec + nout]
        acc_refs = refs[2 * npair + nrow + nvec + nout:]
        if n_outer:
            j, i, k = pl.program_id(0), pl.program_id(1), pl.program_id(2)
        else:
            i, j, k = pl.program_id(0), pl.program_id(1), pl.program_id(2)

        def part(p):
            av, bv = ab[2 * p][...], ab[2 * p + 1][...]
            if a_pro is not None:
                av = a_pro(av)
            if b_pro is not None:
                bv = b_pro(bv)
            return lax.dot_general(av, bv, dims, preferred_element_type=F32)

        def finish(accs):
            row_index = i * tm + lax.broadcasted_iota(jnp.int32, (tm, 1), 0)
            res = epi(accs, [r[...] for r in row_refs], [v[...] for v in vec_refs], row_index)
            for ref, val in zip(out_refs, res):
                ref[...] = val.astype(ref.dtype)

        parts = [part(p) for p in range(npair)]
        if summed:
            total = parts[0]
            for extra in parts[1:]:
                total = total + extra
            parts = [total]
        if nk == 1:
            finish(parts)
        else:
            @pl.when(k == 0)
            def _():
                for q in range(nacc):
                    acc_refs[q][...] = parts[q]

            @pl.when(jnp.logical_and(k > 0, k < nk - 1))
            def _():
                for q in range(nacc):
                    acc_refs[q][...] += parts[q]

            @pl.when(k == nk - 1)
            def _():
                finish([acc_refs[q][...] + parts[q] for q in range(nacc)])

    grid = (nn, nm, nk) if n_outer else (nm, nn, nk)
    return pl.pallas_call(body, name=name, grid=grid, in_specs=in_specs, out_specs=out_specs,
                          out_shape=out_shape, scratch_shapes=scratch, compiler_params=_cparams())(*args)


def _split3(v):
    v0 = v.astype(BF16)
    r1 = v - v0.astype(F32)
    v1 = r1.astype(BF16)
    v2 = (r1 - v1.astype(F32)).astype(BF16)
    return v0, v1, v2


def _mesh_pos():
    return lax.axis_index("x"), lax.axis_index("y"), lax.axis_index("c")


def _allgather_small(name, x):
    m, n = x.shape

    def body(x_ref, out_ref, send_sems, recv_sems, local_sem):
        xi, yi, ci = _mesh_pos()
        me, sibling = (xi, yi, ci), (xi, yi, 1 - ci)
        chips = [(1 - xi, yi), (xi, 1 - yi), (1 - xi, 1 - yi)]

        def rows(px, py, pc):
            return out_ref.at[pl.ds((4 * px + 2 * py + pc) * m, m), :]

        def copy(k, block, to, src=None):
            return pltpu.make_async_remote_copy(
                src_ref=rows(*block) if src is None else src, dst_ref=rows(*block),
                send_sem=send_sems.at[k], recv_sem=recv_sems.at[k], device_id=to, device_id_type=MESH)

        mine = pltpu.make_async_copy(x_ref, rows(*me), local_sem)
        mine.start()
        first = [copy(0, me, sibling, src=x_ref)]
        first += [copy(1 + j, me, (*chip, ci), src=x_ref) for j, chip in enumerate(chips)]
        for cp in first:
            cp.start()
        passed = [copy(4 + j, (*chip, ci), sibling) for j, chip in enumerate(chips)]
        for j, chip in enumerate(chips):
            copy(1 + j, (*chip, ci), me).wait_recv()
            passed[j].start()
        copy(0, sibling, me).wait_recv()
        for j, chip in enumerate(chips):
            copy(4 + j, (*chip, 1 - ci), me).wait_recv()
        for cp in first + passed:
            cp.wait_send()
        mine.wait()

    return pl.pallas_call(
        body, name=name, out_shape=jax.ShapeDtypeStruct((N_DEV * m, n), x.dtype),
        in_specs=[pl.BlockSpec(memory_space=pltpu.VMEM)], out_specs=pl.BlockSpec(memory_space=pltpu.VMEM),
        scratch_shapes=[pltpu.SemaphoreType.DMA((7,)), pltpu.SemaphoreType.DMA((7,)), pltpu.SemaphoreType.DMA],
        compiler_params=_cparams())(x)


_HBM = pl.BlockSpec(memory_space=pltpu.HBM)
_SEM = pl.BlockSpec(memory_space=pltpu.SEMAPHORE)
_ANY = pl.BlockSpec(memory_space=pl.ANY)
_EFFECT = pltpu.SideEffectType.DATAFLOW_SIDE_EFFECTING


def _in_hbm(v):
    return pltpu.with_memory_space_constraint(v, pltpu.HBM)


def _other_chips(xi, yi):
    return [(1 - xi, yi), (xi, 1 - yi), (1 - xi, 1 - yi)]


def _guarded(core, fn):
    if core is None:
        fn()
    else:
        pl.when(lax.axis_index("c") == core)(fn)


def _split_copies(name, srcs, lands, after, pairs, senders, receivers, ncopy):
    ns, nl = len(srcs), len(lands)
    dma = pltpu.SemaphoreType.DMA((ncopy,))
    thru = [pltpu.HBM(v.shape, v.dtype) for v in list(srcs) + list(lands)]

    def start_body(*refs):
        src_refs, land_refs = refs[:ns], refs[ns:ns + nl]
        descs = pairs(src_refs, land_refs, refs[ns + nl + 1], refs[ns + nl + 2])

        def go():
            for send, _ in descs:
                send.start()

        _guarded(senders, go)
        refs[-1][...] = jnp.zeros_like(refs[-1])

    res = pl.pallas_call(
        start_body, name=name + "_start",
        out_shape=(dma, dma, *thru, jax.ShapeDtypeStruct((8, LANES), F32)),
        in_specs=[_HBM] * (ns + nl) + [_ANY],
        out_specs=(_SEM, _SEM, *([_HBM] * (ns + nl)), pl.BlockSpec(memory_space=pltpu.VMEM)),
        input_output_aliases={k: 2 + k for k in range(ns + nl)},
        compiler_params=_cparams(has_side_effects=_EFFECT),
    )(*[_in_hbm(v) for v in srcs], *[_in_hbm(v) for v in lands], after)
    send_sems, recv_sems, token = res[0], res[1], res[-1]
    carried = res[2:2 + ns + nl]

    def finish(after_work):
        def wait_body(*refs):
            src_refs, land_refs = refs[:ns], refs[ns:ns + nl]
            descs = pairs(src_refs, land_refs, refs[ns + nl], refs[ns + nl + 1])

            def sent():
                for send, _ in descs:
                    send.wait_send()

            def landed():
                for _, recv in descs:
                    recv.wait_recv()

            _guarded(senders, sent)
            _guarded(receivers, landed)

        out = pl.pallas_call(
            wait_body, name=name + "_wait", out_shape=tuple(thru),
            in_specs=[_HBM] * (ns + nl) + [_SEM, _SEM, _ANY], out_specs=tuple([_HBM] * (ns + nl)),
            input_output_aliases={k: k for k in range(ns + nl)},
            compiler_params=_cparams(has_side_effects=_EFFECT),
        )(*carried, send_sems, recv_sems, after_work)
        return list(out[:ns]), list(out[ns:])

    return token, finish


def _cast_slot(name, w, chip_index, after):
    R, C = w.shape[1:]
    tr = _div(R, max(16, 524288 // C), mult=16)

    def body(chip_ref, w_ref, after_ref, o_ref):
        o_ref[...] = w_ref[...].astype(BF16)

    return pl.pallas_call(
        body, name=name, out_shape=jax.ShapeDtypeStruct((N_CHIPS, R, C), BF16),
        grid_spec=pltpu.PrefetchScalarGridSpec(
            num_scalar_prefetch=1, grid=(R // tr,),
            in_specs=[pl.BlockSpec((None, tr, C), lambda i, chip_ref: (0, i, 0)), _ANY],
            out_specs=pl.BlockSpec((None, tr, C), lambda i, chip_ref: (chip_ref[0], i, 0))),
        compiler_params=_cparams())(chip_index, w, after)


def _sum_plane(name, grads, landed, chip_index):
    R, C = grads.shape[1:]
    tr = _div(R, max(16, 262144 // C), mult=16)

    def body(chip_ref, own_ref, land_ref, o_ref):
        o_ref[...] = ((own_ref[...].astype(F32) + land_ref[0].astype(F32)) + land_ref[1].astype(F32)) \
            + land_ref[2].astype(F32)

    return pl.pallas_call(
        body, name=name, out_shape=jax.ShapeDtypeStruct((R, C), F32),
        grid_spec=pltpu.PrefetchScalarGridSpec(
            num_scalar_prefetch=1, grid=(R // tr,),
            in_specs=[pl.BlockSpec((None, tr, C), lambda i, chip_ref: (chip_ref[0], i, 0)),
                      pl.BlockSpec((3, tr, C), lambda i, chip_ref: (0, i, 0))],
            out_specs=pl.BlockSpec((tr, C), lambda i, chip_ref: (i, 0))),
        compiler_params=_cparams())(chip_index, grads, landed)


def _gather_split(name, lands, after):
    def pairs(src_refs, land_refs, send_sems, recv_sems):
        xi, yi, _ = _mesh_pos()
        mine = 2 * xi + yi
        out = []
        for a in range(len(lands)):
            for j, (px, py) in enumerate(_other_chips(xi, yi)):
                def to_slot(slot, a=a, j=j, px=px, py=py):
                    return pltpu.make_async_remote_copy(
                        src_ref=land_refs[a].at[mine], dst_ref=land_refs[a].at[slot], send_sem=send_sems.at[3 * a + j],
                        recv_sem=recv_sems.at[3 * a + j], device_id=(px, py, 1), device_id_type=MESH)
                out.append((to_slot(mine), to_slot(2 * px + py)))
        return out

    return _split_copies(name, [], lands, after, pairs, senders=1, receivers=1, ncopy=3 * len(lands))


def _allgather_split(name, block, me, after):
    land = lax.dynamic_update_slice(lax.empty((N_DEV,) + block.shape, block.dtype), block[None], (me, 0, 0))

    def pairs(src_refs, land_refs, send_sems, recv_sems):
        xi, yi, ci = _mesh_pos()
        mine = 4 * xi + 2 * yi + ci
        out = []
        for k in range(1, N_DEV):
            kx, ky, kc = (k >> 2) & 1, (k >> 1) & 1, k & 1
            px = 1 - xi if kx else xi
            py = 1 - yi if ky else yi
            pc = 1 - ci if kc else ci

            def to_slot(slot, k=k, px=px, py=py, pc=pc):
                return pltpu.make_async_remote_copy(
                    src_ref=land_refs[0].at[mine], dst_ref=land_refs[0].at[slot], send_sem=send_sems.at[k - 1],
                    recv_sem=recv_sems.at[k - 1], device_id=(px, py, pc), device_id_type=MESH)
            out.append((to_slot(mine), to_slot(4 * px + 2 * py + pc)))
        return out

    tok, fin = _split_copies(name, [], [land], after, pairs, senders=None, receivers=None, ncopy=N_DEV - 1)
    return tok, lambda later: fin(later)[1][0]


def _swap_split(name, arrs, after):
    lands = [lax.empty(v.shape, v.dtype) for v in arrs]

    def pairs(src_refs, land_refs, send_sems, recv_sems):
        xi, yi, ci = _mesh_pos()
        out = []
        for a in range(len(arrs)):
            cp = pltpu.make_async_remote_copy(
                src_ref=src_refs[a], dst_ref=land_refs[a], send_sem=send_sems.at[a], recv_sem=recv_sems.at[a],
                device_id=(xi, yi, 1 - ci), device_id_type=MESH)
            out.append((cp, cp))
        return out

    return _split_copies(name, arrs, lands, after, pairs, senders=None, receivers=None, ncopy=len(arrs))


def _pass_split(name, lands, after):
    def pairs(src_refs, land_refs, send_sems, recv_sems):
        xi, yi, _ = _mesh_pos()
        out = []
        for a in range(len(lands)):
            for j, (px, py) in enumerate(_other_chips(xi, yi)):
                cp = pltpu.make_async_remote_copy(
                    src_ref=land_refs[a].at[2 * px + py], dst_ref=land_refs[a].at[2 * px + py],
                    send_sem=send_sems.at[3 * a + j], recv_sem=recv_sems.at[3 * a + j],
                    device_id=(xi, yi, 0), device_id_type=MESH)
                out.append((cp, cp))
        return out

    return _split_copies(name, [], lands, after, pairs, senders=1, receivers=0, ncopy=3 * len(lands))


def _scatter_split(name, grads, after):
    lands = [lax.empty((3,) + g.shape[1:], g.dtype) for g in grads]

    def pairs(src_refs, land_refs, send_sems, recv_sems):
        xi, yi, ci = _mesh_pos()
        out = []
        for a in range(len(grads)):
            for j, (px, py) in enumerate(_other_chips(xi, yi)):
                cp = pltpu.make_async_remote_copy(
                    src_ref=src_refs[a].at[2 * px + py], dst_ref=land_refs[a].at[j], send_sem=send_sems.at[3 * a + j],
                    recv_sem=recv_sems.at[3 * a + j], device_id=(px, py, ci), device_id_type=MESH)
                out.append((cp, cp))
        return out

    return _split_copies(name, grads, lands, after, pairs, senders=None, receivers=None, ncopy=3 * len(grads))


def _gather_finish(name, lands):
    na = len(lands)

    def body(*refs):
        outs = refs[na:2 * na]
        send_sems, recv_sems = refs[2 * na:]
        xi, yi, ci = _mesh_pos()
        passes = [pltpu.make_async_remote_copy(
            src_ref=outs[a].at[2 * px + py], dst_ref=outs[a].at[2 * px + py],
            send_sem=send_sems.at[a, j], recv_sem=recv_sems.at[a, j], device_id=(xi, yi, 0), device_id_type=MESH)
            for a in range(na) for j, (px, py) in enumerate(_other_chips(xi, yi))]

        @pl.when(ci == 1)
        def _():
            for cp in passes:
                cp.start()
            for cp in passes:
                cp.wait_send()

        @pl.when(ci == 0)
        def _():
            for cp in passes:
                cp.wait_recv()

    return pl.pallas_call(
        body, name=name, out_shape=[jax.ShapeDtypeStruct(v.shape, v.dtype) for v in lands],
        in_specs=[_ANY] * na, out_specs=[_ANY] * na,
        input_output_aliases={a: a for a in range(na)},
        scratch_shapes=[pltpu.SemaphoreType.DMA((na, 3)), pltpu.SemaphoreType.DMA((na, 3))],
        compiler_params=_cparams())(*lands)


ATTN_HEADS_PER_STEP = 2


def _attn_tiles(L, Lc, D):
    tq = min(256, Lc)
    return tq, L // tq, Lc // tq, D // HEAD_DIM // Q_PER_KV


def _attn_scores(q, k):
    return lax.dot_general(q, k, (((1,), (1,)), ((), ())), preferred_element_type=F32) * (HEAD_DIM ** -0.5)


def _softmax_rows(s):
    e = jnp.exp(s - jnp.max(s, axis=-1, keepdims=True))
    return e * (1.0 / jnp.sum(e, axis=-1, keepdims=True))


def _attn_probs(q, k):
    return _softmax_rows(_attn_scores(q, k))


def _attn_fwd(qr, kr, v, L, Lc, D):
    T = L + Lc
    tq, nq, qoff, nkv = _attn_tiles(L, Lc, D)
    hp = Q_PER_KV
    ng = Q_PER_KV // hp

    def body(q_ref, k_ref, v_ref, o_ref):
        k, vv = k_ref[...], v_ref[...]
        heads = [slice(r * HEAD_DIM, (r + 1) * HEAD_DIM) for r in range(hp)]
        scores = [_attn_scores(q_ref[:, cols], k) for cols in heads]
        probs = [_softmax_rows(s) for s in scores]
        for cols, p in zip(heads, probs):
            o_ref[:, cols] = jnp.dot(p.astype(BF16), vv, preferred_element_type=F32).astype(o_ref.dtype)

    kv_spec = pl.BlockSpec((T, HEAD_DIM), lambda h, r, q: (0, h))
    return pl.pallas_call(
        body, name="attn_fwd", grid=(nkv, ng, nq),
        in_specs=[pl.BlockSpec((tq, hp * HEAD_DIM), lambda h, r, q: (q + qoff, h * ng + r)), kv_spec, kv_spec],
        out_specs=pl.BlockSpec((tq, hp * HEAD_DIM), lambda h, r, q: (q, h * ng + r)),
        out_shape=jax.ShapeDtypeStruct((L, D), BF16), compiler_params=_cparams())(qr, kr, v)


def _attn_bwd(qr, kr, v, do, L, Lc, D):
    T = L + Lc
    tq, nq, qoff, nkv = _attn_tiles(L, Lc, D)
    scale = HEAD_DIM ** -0.5
    hp = ATTN_HEADS_PER_STEP
    ng = Q_PER_KV // hp

    def body(q_ref, k_ref, v_ref, do_ref, dq_ref, dk_ref, dv_ref):
        first = jnp.logical_and(pl.program_id(1) == 0, pl.program_id(2) == 0)
        k, vv = k_ref[...], v_ref[...]
        nt_dims, tn_dims = (((1,), (1,)), ((), ())), (((0,), (0,)), ((), ()))
        heads = [slice(r * HEAD_DIM, (r + 1) * HEAD_DIM) for r in range(hp)]
        qs = [q_ref[:, cols] for cols in heads]
        douts = [do_ref[:, cols] for cols in heads]
        scores = [_attn_scores(q, k) for q in qs]
        dps = [lax.dot_general(dout, vv, nt_dims, preferred_element_type=F32) for dout in douts]
        probs = [_softmax_rows(s) for s in scores]
        dss = [(p * (dp - jnp.sum(p * dp, axis=-1, keepdims=True)) * scale).astype(BF16) for p, dp in zip(probs, dps)]
        for cols, ds in zip(heads, dss):
            dq_ref[:, cols] = jnp.dot(ds, k, preferred_element_type=F32)
        dk = dv = None
        for q, dout, p, ds in zip(qs, douts, probs, dss):
            dk_r = lax.dot_general(ds, q, tn_dims, preferred_element_type=F32)
            dv_r = lax.dot_general(p.astype(BF16), dout, tn_dims, preferred_element_type=F32)
            dk = dk_r if dk is None else dk + dk_r
            dv = dv_r if dv is None else dv + dv_r

        @pl.when(first)
        def _():
            dk_ref[...] = dk
            dv_ref[...] = dv

        @pl.when(jnp.logical_not(first))
        def _():
            dk_ref[...] += dk
            dv_ref[...] += dv

    kv_spec = pl.BlockSpec((T, HEAD_DIM), lambda h, r, q: (0, h))
    q_spec = pl.BlockSpec((tq, hp * HEAD_DIM), lambda h, r, q: (q + qoff, h * ng + r))
    o_spec = pl.BlockSpec((tq, hp * HEAD_DIM), lambda h, r, q: (q, h * ng + r))
    return pl.pallas_call(
        body, name="attn_bwd", grid=(nkv, ng, nq),
        in_specs=[q_spec, kv_spec, kv_spec, o_spec], out_specs=[o_spec, kv_spec, kv_spec],
        out_shape=[jax.ShapeDtypeStruct((L, D), F32), jax.ShapeDtypeStruct((T, D // Q_PER_KV), F32),
                   jax.ShapeDtypeStruct((T, D // Q_PER_KV), F32)],
        compiler_params=_cparams())(qr, kr, v, do)


SUB = 8


def _doubling(xr, xi, pw_re, pw_im, lanes, first_power, period, reverse):
    n = xr.shape[0]
    rows = lax.broadcasted_iota(jnp.int32, (n, 1), 0) & (period - 1)
    for k in range(period.bit_length() - 1):
        d = 1 << k
        keep = rows < period - d if reverse else rows >= d
        sr = jnp.where(keep, pltpu.roll(xr, n - d if reverse else d, 0), 0.0)
        si = jnp.where(keep, pltpu.roll(xi, n - d if reverse else d, 0), 0.0)
        pr, pi = pw_re[first_power + k:first_power + k + 1, lanes], pw_im[first_power + k:first_power + k + 1, lanes]
        xr, xi = xr + (pr * sr - pi * si), xi + (pr * si + pi * sr)
    return xr, xi


def _scan_tile(xr, xi, tb, lanes, reverse):
    pw_re, pw_im, w8_re, w8_im, wb_re, wb_im, carry_re, carry_im, sr, si = tb
    tt = xr.shape[0]
    nb = tt // SUB
    nq = sr.shape[0]
    cols = [slice(q * LANES, (q + 1) * LANES) for q in range(nq)]
    for q in range(nq):
        sr[q] = xr[:, cols[q]]
        si[q] = xi[:, cols[q]]
    order = list(range(SUB - 2, -1, -1)) if reverse else list(range(1, SUB))
    ends_r, ends_i = [], []
    for q in range(nq):
        ql = slice(lanes.start + q * LANES, lanes.start + (q + 1) * LANES)
        lr, li = pw_re[0:1, ql], pw_im[0:1, ql]
        first_row = pl.ds(SUB - 1 if reverse else 0, nb, stride=SUB)
        pr, pi = sr[q, first_row, :], si[q, first_row, :]
        for r in order:
            rows = pl.ds(r, nb, stride=SUB)
            pr, pi = sr[q, rows, :] + (lr * pr - li * pi), si[q, rows, :] + (lr * pi + li * pr)
            sr[q, rows, :] = pr
            si[q, rows, :] = pi
        ends_r.append(pr)
        ends_i.append(pi)
    er, ei = jnp.concatenate(ends_r, axis=1), jnp.concatenate(ends_i, axis=1)
    er, ei = _doubling(er, ei, pw_re, pw_im, lanes, 3, nb, reverse)
    car, cai = carry_re[:, lanes], carry_im[:, lanes]
    wbr, wbi = wb_re[:, lanes], wb_im[:, lanes]
    er = er + (wbr * car - wbi * cai)
    ei = ei + (wbr * cai + wbi * car)
    out_block = 0 if reverse else nb - 1
    carry_re[:, lanes] = er[out_block:out_block + 1, :]
    carry_im[:, lanes] = ei[out_block:out_block + 1, :]
    blocks = lax.broadcasted_iota(jnp.int32, (nb, 1), 0)
    first = blocks == (nb - 1 if reverse else 0)
    cr = jnp.where(first, car, pltpu.roll(er, nb - 1 if reverse else 1, 0))
    ci = jnp.where(first, cai, pltpu.roll(ei, nb - 1 if reverse else 1, 0))
    for r in range(SUB):
        wr, wi = w8_re[r:r + 1, lanes], w8_im[r:r + 1, lanes]
        add_r, add_i = wr * cr - wi * ci, wr * ci + wi * cr
        for q in range(nq):
            sr[q, pl.ds(r, nb, stride=SUB), :] += add_r[:, cols[q]]
            si[q, pl.ds(r, nb, stride=SUB), :] += add_i[:, cols[q]]
    hr = jnp.concatenate([sr[q] for q in range(nq)], axis=1)
    hi = jnp.concatenate([si[q] for q in range(nq)], axis=1)
    return hr, hi, car, cai


def _scan_scratch(tt, NS):
    nb = tt // SUB
    return [pltpu.VMEM((8, NS), F32), pltpu.VMEM((8, NS), F32), pltpu.VMEM((SUB, NS), F32), pltpu.VMEM((SUB, NS), F32),
            pltpu.VMEM((nb, NS), F32), pltpu.VMEM((nb, NS), F32), pltpu.VMEM((1, NS), F32), pltpu.VMEM((1, NS), F32),
            pltpu.VMEM((SLAB_ST // LANES, tt, LANES), F32), pltpu.VMEM((SLAB_ST // LANES, tt, LANES), F32)]


def _scan_init(lr, li, tb, reverse):
    pw_re, pw_im, w8_re, w8_im, wb_re, wb_im, carry_re, carry_im, sr, _ = tb
    nb = wb_re.shape[0]
    carry_re[...] = jnp.zeros_like(carry_re)
    carry_im[...] = jnp.zeros_like(carry_im)
    pr, pi = lr, li
    for k in range(3 + nb.bit_length() - 1):
        pw_re[k:k + 1, :] = pr
        pw_im[k:k + 1, :] = pi
        if k == 3:
            l8r, l8i = pr, pi
        pr, pi = pr * pr - pi * pi, 2.0 * pr * pi
    pr, pi = lr, li
    for r in range(SUB):
        row = SUB - 1 - r if reverse else r
        w8_re[row:row + 1, :] = pr
        w8_im[row:row + 1, :] = pi
        pr, pi = pr * lr - pi * li, pr * li + pi * lr
    pr, pi = l8r, l8i
    for b in range(nb):
        row = nb - 1 - b if reverse else b
        wb_re[row:row + 1, :] = pr
        wb_im[row:row + 1, :] = pi
        pr, pi = pr * l8r - pi * l8i, pr * l8i + pi * l8r


def _ssm_tiles(T, Lc):
    tt = min(128, Lc)
    return tt, T // tt, Lc // tt


def _ssm_fwd(name, u, bbd, cbd_re, cbd_im, lam_re, lam_im, coef_re, coef_im, Lc, reverse):
    T, W = u.shape
    nslab = W // SLAB_CH
    NS = nslab * SLAB_ST
    tt, nt, nc = _ssm_tiles(T, Lc)
    if reverse:
        tile = lambda s: jnp.where(s < nc, nc - 1 - s, nt - 1 - (s - nc))
    else:
        tile = lambda s: s

    def body(u_ref, b_ref, cr_ref, ci_ref, lr_ref, li_ref, kr_ref, ki_ref, hr_ref, hi_ref, y_ref, *tb):
        @pl.when(pl.program_id(0) == 0)
        def _():
            _scan_init(lr_ref[...], li_ref[...], tb, reverse)

        for j in range(nslab):
            lanes = slice(j * SLAB_ST, (j + 1) * SLAB_ST)
            bu = jnp.dot(u_ref[:, j * SLAB_CH:(j + 1) * SLAB_CH], b_ref[j], preferred_element_type=F32)
            br, bi = bu[:, :SLAB_ST], bu[:, SLAB_ST:]
            kr, ki = kr_ref[:, lanes], ki_ref[:, lanes]
            hr, hi, _, _ = _scan_tile(kr * br - ki * bi, kr * bi + ki * br, tb, lanes, reverse)
            hrb, hib = hr.astype(BF16), hi.astype(BF16)
            hr_ref[:, lanes] = hrb
            hi_ref[:, lanes] = hib
            y_ref[:, j * SLAB_CH:(j + 1) * SLAB_CH] = (
                jnp.dot(hrb, cr_ref[j], preferred_element_type=F32)
                - jnp.dot(hib, ci_ref[j], preferred_element_type=F32))

    whole3 = lambda arr: pl.BlockSpec(arr.shape, lambda s: (0, 0, 0))
    vec = pl.BlockSpec((1, NS), lambda s: (0, 0))
    return pl.pallas_call(
        body, name=name, grid=(nt,),
        in_specs=[pl.BlockSpec((tt, W), lambda s: (tile(s), 0)), whole3(bbd), whole3(cbd_re), whole3(cbd_im),
                  vec, vec, vec, vec],
        out_specs=[pl.BlockSpec((tt, NS), lambda s: (tile(s), 0)), pl.BlockSpec((tt, NS), lambda s: (tile(s), 0)),
                   pl.BlockSpec((tt, W), lambda s: (tile(s), 0))],
        out_shape=[jax.ShapeDtypeStruct((T, NS), BF16), jax.ShapeDtypeStruct((T, NS), BF16),
                   jax.ShapeDtypeStruct((T, W), F32)],
        scratch_shapes=_scan_scratch(tt, NS),
        compiler_params=_cparams())(u, bbd, cbd_re, cbd_im, lam_re, lam_im, coef_re, coef_im)


def _ssm_bwd(name, dy, h_re, h_im, u, bbd, bbdt_re, bbdt_im, cbdt_re, cbdt_im, lam_re, lam_im,
             coef_re, coef_im, Lc, reverse):
    T, W = u.shape
    nslab = W // SLAB_CH
    NS = nslab * SLAB_ST
    tt, nt, nc = _ssm_tiles(T, Lc)
    adj_reverse = not reverse
    if reverse:
        tile = lambda s: jnp.where(s < nt - nc, nc + s, s - (nt - nc))
    else:
        tile = lambda s: nt - 1 - s

    def body(dy_ref, hr_ref, hi_ref, u_ref, b_ref, btr_ref, bti_ref, ctr_ref, cti_ref, lr_ref, li_ref,
             kr_ref, ki_ref, du_ref, dlr_ref, dli_ref, dkr_ref, dki_ref, dbf_ref, dcrf_ref, dcif_ref,
             db_ref, dcr_ref, dci_ref, *tb):
        @pl.when(pl.program_id(0) == 0)
        def _():
            _scan_init(lr_ref[...], -li_ref[...], tb, adj_reverse)
            for ref in (dlr_ref, dli_ref, dkr_ref, dki_ref, db_ref, dcr_ref, dci_ref):
                ref[...] = jnp.zeros_like(ref)

        rows = lax.broadcasted_iota(jnp.int32, (tt, 1), 0)
        far_row = tt - 1 if adj_reverse else 0
        tn_dims = (((0,), (0,)), ((), ()))
        for j in range(nslab):
            lanes = slice(j * SLAB_ST, (j + 1) * SLAB_ST)
            chans = slice(j * SLAB_CH, (j + 1) * SLAB_CH)
            dys, us = dy_ref[:, chans], u_ref[:, chans]
            er = jnp.dot(dys, ctr_ref[j], preferred_element_type=F32)
            ei = -jnp.dot(dys, cti_ref[j], preferred_element_type=F32)
            ar, ai, car, cai = _scan_tile(er, ei, tb, lanes, adj_reverse)
            shift = tt - 1 if adj_reverse else 1
            nr = jnp.where(rows == far_row, car, pltpu.roll(ar, shift, 0))
            ni = jnp.where(rows == far_row, cai, pltpu.roll(ai, shift, 0))
            hrb, hib = hr_ref[:, lanes], hi_ref[:, lanes]
            hr, hi = hrb.astype(F32), hib.astype(F32)
            dlr_ref[:, lanes] += jnp.sum(nr * hr + ni * hi, axis=0, keepdims=True)
            dli_ref[:, lanes] += jnp.sum(ni * hr - nr * hi, axis=0, keepdims=True)
            bu = jnp.dot(us, b_ref[j], preferred_element_type=F32)
            br, bi = bu[:, :SLAB_ST], bu[:, SLAB_ST:]
            dkr_ref[:, lanes] += jnp.sum(ar * br + ai * bi, axis=0, keepdims=True)
            dki_ref[:, lanes] += jnp.sum(ai * br - ar * bi, axis=0, keepdims=True)
            kr, ki = kr_ref[:, lanes], ki_ref[:, lanes]
            dbr = (ar * kr + ai * ki).astype(BF16)
            dbi = (ai * kr - ar * ki).astype(BF16)
            du_ref[:, chans] = (jnp.dot(dbr, btr_ref[j], preferred_element_type=F32)
                                + jnp.dot(dbi, bti_ref[j], preferred_element_type=F32))
            db_ref[j, :, :SLAB_ST] += lax.dot_general(us, dbr, tn_dims, preferred_element_type=F32)
            db_ref[j, :, SLAB_ST:] += lax.dot_general(us, dbi, tn_dims, preferred_element_type=F32)
            dcr_ref[j] += lax.dot_general(hrb, dys, tn_dims, preferred_element_type=F32)
            dci_ref[j] -= lax.dot_general(hib, dys, tn_dims, preferred_element_type=F32)

        @pl.when(pl.program_id(0) == nt - 1)
        def _():
            def iota(shape, axis):
                return lax.broadcasted_iota(jnp.int32, shape, axis)

            sg, ss = SSM_GROUP.bit_length() - 1, SSM_STATE.bit_length() - 1
            b_mask = (iota((SLAB_CH, SLAB_ST), 0) >> sg) == (iota((SLAB_CH, SLAB_ST), 1) >> ss)
            c_mask = (iota((SLAB_ST, SLAB_CH), 0) >> ss) == (iota((SLAB_ST, SLAB_CH), 1) >> sg)
            fold = jnp.where((iota((SLAB_ST, SSM_STATE), 0) & (SSM_STATE - 1)) == iota((SLAB_ST, SSM_STATE), 1),
                             1.0, 0.0).astype(BF16)
            fold_t = jnp.where((iota((SSM_STATE, SLAB_ST), 1) & (SSM_STATE - 1)) == iota((SSM_STATE, SLAB_ST), 0),
                               1.0, 0.0).astype(BF16)

            def exact_dot(a, b, a_is_value):
                terms = _split3(a if a_is_value else b)
                acc = None
                for t in terms:
                    part = jnp.dot(t, b, preferred_element_type=F32) if a_is_value else jnp.dot(a, t, preferred_element_type=F32)
                    acc = part if acc is None else acc + part
                return acc

            for j in range(nslab):
                dbj = db_ref[j]
                dbf_ref[j, :, :SSM_STATE] = exact_dot(jnp.where(b_mask, dbj[:, :SLAB_ST], 0.0), fold, True)
                dbf_ref[j, :, SSM_STATE:] = exact_dot(jnp.where(b_mask, dbj[:, SLAB_ST:], 0.0), fold, True)
                dcrf_ref[j] = exact_dot(fold_t, jnp.where(c_mask, dcr_ref[j], 0.0), False)
                dcif_ref[j] = exact_dot(fold_t, jnp.where(c_mask, dci_ref[j], 0.0), False)

    whole3 = lambda arr: pl.BlockSpec(arr.shape, lambda s: (0, 0, 0))
    vec = pl.BlockSpec((1, NS), lambda s: (0, 0))
    row_w = pl.BlockSpec((tt, W), lambda s: (tile(s), 0))
    row_s = pl.BlockSpec((tt, NS), lambda s: (tile(s), 0))
    dbf = jax.ShapeDtypeStruct((nslab, SLAB_CH, 2 * SSM_STATE), F32)
    dcf = jax.ShapeDtypeStruct((nslab, SSM_STATE, SLAB_CH), F32)
    return pl.pallas_call(
        body, name=name, grid=(nt,),
        in_specs=[row_w, row_s, row_s, row_w, whole3(bbd), whole3(bbdt_re), whole3(bbdt_im), whole3(cbdt_re),
                  whole3(cbdt_im), vec, vec, vec, vec],
        out_specs=[row_w, vec, vec, vec, vec, whole3(dbf), whole3(dcf), whole3(dcf)],
        out_shape=[jax.ShapeDtypeStruct((T, W), F32)] + [jax.ShapeDtypeStruct((1, NS), F32)] * 4 + [dbf, dcf, dcf],
        scratch_shapes=[pltpu.VMEM(bbd.shape, F32), pltpu.VMEM(bbdt_re.shape, F32), pltpu.VMEM(bbdt_re.shape, F32)]
        + _scan_scratch(tt, NS),
        compiler_params=_cparams())(dy, h_re, h_im, u, bbd, bbdt_re, bbdt_im, cbdt_re, cbdt_im,
                                    lam_re, lam_im, coef_re, coef_im)


def _zoh_math(a_re, a_im, log_dt):
    dt = jnp.exp(log_dt)
    mag = jnp.exp(a_re * dt)
    lb_re = mag * jnp.cos(a_im * dt)
    lb_im = mag * jnp.sin(a_im * dt)
    den = a_re * a_re + a_im * a_im
    coef_re = ((lb_re - 1.0) * a_re + lb_im * a_im) / den
    coef_im = (lb_im * a_re - (lb_re - 1.0) * a_im) / den
    return lb_re, lb_im, coef_re, coef_im


def _zoh_fwd(a_re, a_im, log_dt):
    def body(ar, ai, ld, o0, o1, o2, o3):
        for ref, val in zip((o0, o1, o2, o3), _zoh_math(ar[...], ai[...], ld[...])):
            ref[...] = val

    return pl.pallas_call(body, name="zoh_fwd", out_shape=[jax.ShapeDtypeStruct(a_re.shape, F32)] * 4,
                          compiler_params=_cparams())(a_re, a_im, log_dt)


def _zoh_bwd(a_re, a_im, log_dt, cots):
    def body(ar, ai, ld, c0, c1, c2, c3, o0, o1, o2):
        _, vjp = jax.vjp(_zoh_math, ar[...], ai[...], ld[...])
        for ref, val in zip((o0, o1, o2), vjp((c0[...], c1[...], c2[...], c3[...]))):
            ref[...] = val

    return pl.pallas_call(
        body, name="zoh_bwd",
        out_shape=[jax.ShapeDtypeStruct(a_re.shape, F32), jax.ShapeDtypeStruct(a_re.shape, F32),
                   jax.ShapeDtypeStruct(log_dt.shape, F32)],
        compiler_params=_cparams())(a_re, a_im, log_dt, *cots)


def _outer_sum(acts, cots):
    D, N = acts.shape[1], cots.shape[1]
    tm, tn = _div(D, 512), _div(N, 1152)
    dims = (((0,), (0,)), ((), ()))

    def body(a_ref, b_ref, o_ref):
        a = a_ref[...]
        aa = _split3(a * _sigmoid(a))
        bb = _split3(b_ref[...])
        acc = None
        for ia in range(3):
            for ib in range(3 - ia):
                t = lax.dot_general(aa[ia], bb[ib], dims, preferred_element_type=F32)
                acc = t if acc is None else acc + t
        o_ref[...] = acc

    return pl.pallas_call(
        body, name="mod_dw", grid=(D // tm, N // tn),
        in_specs=[pl.BlockSpec((16, tm), lambda i, j: (0, i)), pl.BlockSpec((16, tn), lambda i, j: (0, j))],
        out_specs=pl.BlockSpec((tm, tn), lambda i, j: (i, j)),
        out_shape=jax.ShapeDtypeStruct((D, N), F32), compiler_params=_cparams())(acts, cots)


def _adamw_math(w, g, m, v):
    m = ADAM_B1 * m + (1.0 - ADAM_B1) * g
    v = ADAM_B2 * v + (1.0 - ADAM_B2) * (g * g)
    m_hat = m / (1.0 - ADAM_B1 ** ADAM_STEP)
    v_hat = v / (1.0 - ADAM_B2 ** ADAM_STEP)
    delta = -ADAM_LR * (m_hat / (jnp.sqrt(v_hat) + ADAM_EPS) + ADAM_WD * w)
    return delta, m, v


def _adamw(name, w, m, v, gparts):
    R, C = w.shape[-2:]
    kind = 'row1' if w.ndim == 3 else 'row'
    tr = _div(R, max(8, 262144 // C), mult=8)

    def fn(i, wv, mv, vv, *gs):
        g = gs[0]
        for extra in gs[1:]:
            g = g + extra
        return (g,) + _adamw_math(wv, g, mv, vv)

    return _rowk(name, fn, R, tr, [(w, kind), (m, kind), (v, kind)] + [(g, 'row') for g in gparts],
                 [(w.shape, F32, kind)] * 4)


def _pack(pieces, rows_mult=8):
    flat = jnp.concatenate([p.reshape(-1).astype(F32) for p in pieces])
    unit = rows_mult * PACK_W
    total = -(-flat.shape[0] // unit) * unit
    return jnp.pad(flat, (0, total - flat.shape[0])).reshape(total // PACK_W, PACK_W)


def _unpack(buf, shapes):
    flat = buf.reshape(-1)
    out, off = [], 0
    for s in shapes:
        n = math.prod(s)
        out.append(flat[off:off + n].reshape(s))
        off += n
    return out


def _bd_expand(t):
    S, g, a, b = t.shape
    eye = jnp.eye(g, dtype=t.dtype)
    return (t[:, :, :, None, :] * eye[None, :, None, :, None]).reshape(S, g * a, g * b)


def _rope_tables(L, Lc):
    rows = L // GRID_W
    row_ids = jnp.broadcast_to(jnp.arange(rows)[:, None], (rows, GRID_W)).reshape(-1).astype(F32)
    col_ids = jnp.broadcast_to(jnp.arange(GRID_W)[None, :], (rows, GRID_W)).reshape(-1).astype(F32)
    quarter = HEAD_DIM // 4
    inv_freq = ROPE_THETA ** (-jnp.arange(quarter, dtype=F32) / quarter)
    ang_r = row_ids[:, None] * inv_freq
    ang_c = col_ids[:, None] * inv_freq
    cos = jnp.concatenate([jnp.cos(ang_r), jnp.cos(ang_r), jnp.cos(ang_c), jnp.cos(ang_c)], axis=1)
    sin = jnp.concatenate([-jnp.sin(ang_r), jnp.sin(ang_r), -jnp.sin(ang_c), jnp.sin(ang_c)], axis=1)
    cos = jnp.concatenate([jnp.ones((Lc, HEAD_DIM), F32), cos], axis=0)
    sin = jnp.concatenate([jnp.zeros((Lc, HEAD_DIM), F32), sin], axis=0)
    return cos, sin


def _rot(v):
    lane = lax.broadcasted_iota(jnp.int32, (1, HEAD_DIM), 1)
    first = (lane % (HEAD_DIM // 2)) < (HEAD_DIM // 4)
    return jnp.where(first, pltpu.roll(v, HEAD_DIM - HEAD_DIM // 4, 1), pltpu.roll(v, HEAD_DIM // 4, 1))


def _head_norm(xh, g):
    return xh * lax.rsqrt(jnp.mean(xh * xh, axis=-1, keepdims=True) + NORM_EPS) * g


def _norm_mod(xv, g, sh, sc):
    r = lax.rsqrt(jnp.mean(xv * xv, axis=-1, keepdims=True) + NORM_EPS)
    return (xv * r) * g * (1.0 + sc) + sh


def kernel(x, c, ctx, c_ctx, w_mod, b_mod, norm_g, w_ffn1_gate, w_ffn1_up, w_ffn1_down, w_in, q_norm_g, k_norm_g, ssm_a_re, ssm_a_im, ssm_log_dt, ssm_b_re, ssm_b_im, ssm_c_re, ssm_c_im, ssm_d, w_glu, b_glu, w_br_attn, w_br_ssm, w_out, w_ffn2_gate, w_ffn2_up, w_ffn2_down, loss_target, m_c_ctx, m_w_mod, m_b_mod, m_norm_g, m_w_ffn1_gate, m_w_ffn1_up, m_w_ffn1_down, m_w_in, m_q_norm_g, m_k_norm_g, m_ssm_a_re, m_ssm_a_im, m_ssm_log_dt, m_ssm_b_re, m_ssm_b_im, m_ssm_c_re, m_ssm_c_im, m_ssm_d, m_w_glu, m_b_glu, m_w_br_attn, m_w_br_ssm, m_w_out, m_w_ffn2_gate, m_w_ffn2_up, m_w_ffn2_down, v_c_ctx, v_w_mod, v_b_mod, v_norm_g, v_w_ffn1_gate, v_w_ffn1_up, v_w_ffn1_down, v_w_in, v_q_norm_g, v_k_norm_g, v_ssm_a_re, v_ssm_a_im, v_ssm_log_dt, v_ssm_b_re, v_ssm_b_im, v_ssm_c_re, v_ssm_c_im, v_ssm_d, v_w_glu, v_b_glu, v_w_br_attn, v_w_br_ssm, v_w_out, v_w_ffn2_gate, v_w_ffn2_up, v_w_ffn2_down):
    A = dict(locals())
    xi, yi, ci = _mesh_pos()
    chip = 2 * xi + yi
    me = 4 * xi + 2 * yi + ci
    L, D = x.shape[1], x.shape[2]
    Lc = ctx.shape[1]
    T = L + Lc
    F4 = w_ffn1_gate.shape[2]
    F = N_CHIPS * F4
    W, KV, Dq = D // 2, D // 4, D // 4
    G = W // SSM_GROUP
    P, E = SSM_STATE, SSM_GROUP
    NS = G * P
    nslab = W // SLAB_CH
    tr = min(256, Lc)
    ncr = Lc // tr
    assert L % tr == 0 and Lc % tr == 0 and W % SLAB_CH == 0 and D % (4 * LANES) == 0

    def sel(i, v):
        return v if v.shape[0] == 1 else jnp.where(i < ncr, v[0:1], v[1:2])

    def put(i, v, nrow):
        if nrow == 1:
            return v
        which = (i >= ncr).astype(jnp.int32)
        r2 = lax.broadcasted_iota(jnp.int32, (nrow, 1), 0)
        return jnp.where(r2 == which, jnp.broadcast_to(v, (nrow, v.shape[1])), 0.0)

    ident = lambda accs, rows, vecs, ri: [accs[0]]

    NM = w_mod.shape[2]
    first = jnp.zeros((8, D), F32).at[0].set(c[0]).at[1:4, :Dq].set(norm_g[0])
    g0 = _allgather_small("gather_c", first).reshape(N_CHIPS, 2, 8, D)
    c_all = g0[:, :, 0].reshape(N_DEV, D)
    ng = jnp.transpose(g0[:, 0, 1:4, :Dq], (1, 0, 2)).reshape(3, D)
    acts = jnp.concatenate([c_all, c_ctx[None], jnp.zeros((7, D), F32)], axis=0)
    wm = w_mod[0]
    b_shard = lax.dynamic_slice(b_mod[0], (chip * NM,), (NM,))[None]
    silu_bf = lambda a: (a * _sigmoid(a)).astype(BF16)
    to_bf = lambda b: b.astype(BF16)
    mod_part = _mm("mod_fwd", [(acts, wm, D)], 16, NM, tm=16, tn=_div(NM, 1152),
                   epi=lambda accs, rows, vecs, ri: [accs[0] + vecs[0]], outs=[(F32, False)],
                   vecs=[b_shard], a_pro=silu_bf, b_pro=to_bf)[0]
    mg = _allgather_small("gather_mod", mod_part).reshape(N_CHIPS, 2, 16, NM)[:, 0]
    mod_all = jnp.transpose(mg, (1, 0, 2)).reshape(16, N_CHIPS * NM)
    mod_x = lax.dynamic_slice(mod_all, (me, 0), (1, 9 * D))
    mod_c = jnp.where(jnp.arange(9 * D)[None] < 5 * D, mod_all[8:9], 0.0)
    modv = jnp.concatenate([mod_c, mod_x], axis=0)
    mv = lambda k: modv[:, k * D:(k + 1) * D]
    sh1, sc1, g1, sh2, sc2 = mv(0), mv(1), mv(2), mv(3), mv(4)
    g2, sh3, sc3, g3 = mv(5)[1:2], mv(6)[1:2], mv(7)[1:2], mv(8)[1:2]

    big = ['w_ffn1_gate', 'w_ffn1_up', 'w_ffn1_down', 'w_ffn2_gate', 'w_ffn2_up', 'w_ffn2_down',
           'w_in', 'w_glu', 'w_br_attn', 'w_br_ssm', 'w_out']
    row_sharded = {'w_ffn1_down', 'w_ffn2_down', 'w_glu', 'w_br_attn', 'w_out'}
    groups = [big[0:2], big[2:3], big[6:7], big[7:11], big[3:6]]
    chip_index = jnp.reshape(chip, (1,)).astype(jnp.int32)
    tok, gather_finish = modv, []
    pin = c
    for gi, names in enumerate(groups):
        tok, fin = _gather_split("gather_w%d" % gi, [_cast_slot("cast_" + n, A[n], chip_index, pin) for n in names], tok)
        gather_finish.append(fin)
        pin = tok
    ng = ng + tok[0:1, 0:1]
    Wt = {}

    def register(names, full):
        for n, gw in zip(names, full):
            Wt[n] = gw.reshape(N_CHIPS * gw.shape[1], gw.shape[2]) if n in row_sharded else gw

    def weights_ready(gi, after_work):
        _, lands = gather_finish[gi](after_work)
        register(groups[gi], _gather_finish("gather_w%d_pass" % gi, lands))

    def weights_pass(gi, after_work):
        _, lands = gather_finish[gi](after_work)
        tok_, fin_ = _pass_split("gather_w%d_pass" % gi, lands, after_work)
        return tok_, lambda later: register(groups[gi], fin_(later)[1])

    a_re2, a_im2 = ssm_a_re[0].reshape(2 * G, P), ssm_a_im[0].reshape(2 * G, P)
    ldt2 = ssm_log_dt[0].reshape(2 * G, 1)
    zoh = _zoh_fwd(a_re2, a_im2, ldt2)
    lam_re, lam_im, coef_re, coef_im = [[z[d * G:(d + 1) * G].reshape(1, NS) for d in range(2)] for z in zoh]
    bd_b = lambda b: _bd_expand(jnp.transpose(b, (0, 2, 1)).reshape(nslab, SLAB_GROUPS, E, P))
    bd_c = lambda cc: _bd_expand(jnp.transpose(cc, (0, 2, 1)).reshape(nslab, SLAB_GROUPS, P, E))
    bbd, bbdt_re, bbdt_im, cbd_re, cbd_im, cbdt_re, cbdt_im = [], [], [], [], [], [], []
    for d in range(2):
        br_, bi_ = bd_b(ssm_b_re[0, d]).astype(BF16), bd_b(ssm_b_im[0, d]).astype(BF16)
        cr_, ci_ = bd_c(ssm_c_re[0, d]).astype(BF16), bd_c(ssm_c_im[0, d]).astype(BF16)
        bbd.append(jnp.concatenate([br_, bi_], axis=2))
        bbdt_re.append(jnp.transpose(br_, (0, 2, 1)))
        bbdt_im.append(jnp.transpose(bi_, (0, 2, 1)))
        cbd_re.append(cr_)
        cbd_im.append(ci_)
        cbdt_re.append(jnp.transpose(cr_, (0, 2, 1)))
        cbdt_im.append(jnp.transpose(ci_, (0, 2, 1)))
    cos_t, sin_t = _rope_tables(L, Lc)
    qg, kg = q_norm_g, k_norm_g
    small = ['c_ctx', 'b_mod', 'norm_g', 'q_norm_g', 'k_norm_g', 'ssm_a_re', 'ssm_a_im', 'ssm_log_dt', 'ssm_b_re',
             'ssm_b_im', 'ssm_c_re', 'ssm_c_im', 'ssm_d', 'b_glu']
    packs_wmv = [_pack([A[pre + n] for n in small]) for pre in ('', 'm_', 'v_')]
    prepared = packs_wmv + [cos_t, sin_t, coef_im[0], coef_im[1]] + [
        t[d][0] for t in (bbd, bbdt_re, bbdt_im, cbd_re, cbd_im, cbdt_re, cbdt_im) for d in range(2)]
    weights_ready(0, tok + sum(t[0:1, 0:1].astype(F32) for t in prepared))

    def norm_mod(name, xv, g, sh, sc):
        rows = xv.shape[0]
        return _rowk(name, lambda i, xt, gt, sht, sct: [_norm_mod(xt, gt, sel(i, sht), sel(i, sct))],
                     rows, tr, [(xv, 'row'), (g, 'vec'), (sh, 'vec'), (sc, 'vec')], [((rows, D), BF16, 'row')])[0]

    def swiglu_epi(accs, rows, vecs, ri):
        a_, b_ = accs
        return [a_, b_, a_ * _sigmoid(a_) * b_]

    def res_epi(coef):
        def epi(accs, rows, vecs, ri):
            gate = vecs[0]
            if gate.shape[0] == 2:
                gate = jnp.where(ri < Lc, gate[0:1], gate[1:2])
            return [accs[0], rows[0] + (coef * gate) * accs[0]]
        return epi

    def ffn_fwd(tag, h, xres, gate, down_ready=None):
        rows = h.shape[0]
        a_, b_, s_ = _mm(tag + "_up", [(h, Wt['w_' + tag + '_gate'], D), (h, Wt['w_' + tag + '_up'], D)], rows, F,
                         tm=_div(rows, 512), tn=F4, epi=swiglu_epi, outs=[(BF16, False), (BF16, False), (BF16, False)])
        if down_ready is not None:
            down_ready(s_)
        f_, xo = _mm(tag + "_down", [(s_, Wt['w_' + tag + '_down'], F)], rows, D, tm=_div(rows, 768),
                     tn=_div(D, 512), epi=res_epi(0.5), outs=[(F32, False), (F32, False)],
                     rows=[(xres, 0, 0)], vecs=[gate])
        return a_, b_, s_, f_, xo

    xc = jnp.concatenate([ctx[0], x[0]], axis=0)
    h1 = norm_mod("norm1", xc, ng[0:1], sh1, sc1)
    a1, b1, s1, f1, x1 = ffn_fwd("ffn1", h1, xc, g1, down_ready=lambda s_: weights_ready(1, s_))
    weights_ready(2, x1)
    h2 = norm_mod("norm2", x1, ng[1:2], sh2, sc2)
    proj = _mm("in_proj", [(h2, Wt['w_in'], D)], T, 4 * D, tm=_div(T, 768), tn=_div(D, 1024), epi=ident,
               outs=[(F32, False)])[0]
    nh, nkvh = D // HEAD_DIM, KV // HEAD_DIM

    def prep_fn(i, kt, vt, ut, qt, qgt, kgt, ct, st):
        qs = [_head_norm(qt[:, h * HEAD_DIM:(h + 1) * HEAD_DIM], qgt) for h in range(nh)]
        ks = [_head_norm(kt[:, h * HEAD_DIM:(h + 1) * HEAD_DIM], kgt) for h in range(nkvh)]
        qs = [v * ct + _rot(v) * st for v in qs]
        ks = [v * ct + _rot(v) * st for v in ks]
        return [jnp.concatenate(qs, axis=1), jnp.concatenate(ks, axis=1), vt, ut]

    qr, kr, vb, ub = _rowk(
        "qk_prep", prep_fn, T, tr,
        [(proj, ('col', KV, 0)), (proj, ('col', KV, 1)), (proj, ('col', W, 1)), (proj, ('col', D, 1)),
         (qg, 'vec'), (kg, 'vec'), (cos_t, 'row'), (sin_t, 'row')],
        [((T, D), BF16, 'row'), ((T, KV), BF16, 'row'), ((T, KV), BF16, 'row'), ((T, W), BF16, 'row')])
    _, mixer_weights = weights_pass(3, qr)
    attn = _attn_fwd(qr, kr, vb, L, Lc, D)
    hs_re, hs_im, ys = [], [], []
    lam_in = lam_re[0]
    for d in range(2):
        hr_, hi_, y_ = _ssm_fwd("ssm_fwd%d" % d, ub, bbd[d], cbd_re[d], cbd_im[d], lam_in, lam_im[d],
                                coef_re[d], coef_im[d], Lc, reverse=bool(d))
        hs_re.append(hr_)
        hs_im.append(hi_)
        ys.append(y_)
        if d == 0:
            tok_p4, ffn2_weights = weights_pass(4, y_)
            lam_in = lam_re[1] + tok_p4[0:1, 0:1]
    mixer_weights(ys[1])

    def ssm_out_fn(i, y0, y1, ut, dt):
        pre = dt * ut + y0 + y1
        yg_ = _gelu(pre)
        return [pre, yg_, yg_]

    ssm_pre, yg, ygb = _rowk(
        "ssm_out", ssm_out_fn, L, tr,
        [(ys[0], 'orow'), (ys[1], 'orow'), (proj, ('ocol', W, 1)), (ssm_d, 'vec')],
        [((L, W), F32, 'row'), ((L, W), F32, 'row'), ((L, W), BF16, 'row')], nc=ncr)

    def glu_epi(accs, rows, vecs, ri):
        z_ = accs[0] + vecs[0]
        return [z_, rows[0] * _sigmoid(z_)]

    zglu, y2 = _mm("glu", [(ygb, Wt['w_glu'], W)], L, W, tm=_div(L, 512), tn=_div(W, 512), epi=glu_epi,
                   outs=[(F32, False), (BF16, False)], rows=[(yg, 0, 0)], vecs=[b_glu])
    tnm = _div(Dq, 512)

    def merge_epi(accs, rows, vecs, ri):
        ga, gs = _sigmoid(rows[0]), _sigmoid(rows[1])
        return [accs[0], accs[1], ga * accs[0] + gs * accs[1]]

    ba, bs, merged = _mm("merge", [(attn, Wt['w_br_attn'], D), (y2, Wt['w_br_ssm'], W)], L, D, tm=tr, tn=tnm,
                         epi=merge_epi, outs=[(F32, False), (F32, False), (BF16, False)],
                         rows=[(proj, ncr, 2 * D // tnm), (proj, ncr, 3 * D // tnm)])
    mix, x2 = _mm("out_proj", [(merged, Wt['w_out'], D)], L, D, tm=tr, tn=_div(D, 1024), epi=res_epi(1.0),
                  outs=[(F32, False), (F32, False)], rows=[(x1, ncr, 0)], vecs=[g2])
    ffn2_weights(x2)
    h3 = norm_mod("norm3", x2, ng[2:3], sh3, sc3)
    a3, b3, s3, f3, x3 = ffn_fwd("ffn2", h3, x2, g3)

    def loss_fn(i, yt, tt_):
        diff = yt - tt_
        return [diff * (1.0 / D), jnp.sum(diff * diff, axis=0, keepdims=True)]

    dy, sq = _rowk("loss", loss_fn, L, tr, [(x3, 'row'), (loss_target[0], 'row')],
                   [((L, D), F32, 'row'), ((1, D), F32, 'acc')])
    loss = lax.psum(0.5 * jnp.sum(sq) / D, ("x", "y", "c"))

    def res_bwd(name, dxo, f_, gate, coef):
        rows, nrow = dxo.shape[0], gate.shape[0]

        def fn(i, dt, ft, gt):
            return [(coef * sel(i, gt)) * dt, put(i, jnp.sum(dt * ft, axis=0, keepdims=True) * coef, nrow)]

        return _rowk(name, fn, rows, tr, [(dxo, 'row'), (f_, 'row'), (gate, 'vec')],
                     [((rows, D), BF16, 'row'), ((nrow, D), F32, 'acc')])

    def swiglu_bwd_epi(accs, rows, vecs, ri):
        ds_, a_, b_ = accs[0], rows[0].astype(F32), rows[1].astype(F32)
        sg = _sigmoid(a_)
        return [ds_ * b_ * (sg * (1.0 + a_ * (1.0 - sg))), ds_ * (a_ * sg)]

    def norm_mod_bwd(name, xv, g, sh, sc, dh, dres, dres_kind):
        rows, nrow = xv.shape[0], sh.shape[0]

        def fn(i, xt, gt, sht, sct, dht, rest):
            _, vjp = jax.vjp(_norm_mod, xt, gt, sel(i, sht), sel(i, sct))
            dx_, dg_, dsh_, dsc_ = vjp(dht)
            dx_ = dx_ + (jnp.where(i >= ncr, rest, 0.0) if dres_kind == 'xrow' else rest)
            return [dx_, dg_, put(i, dsh_, nrow), put(i, dsc_, nrow)]

        return _rowk(name, fn, rows, tr,
                     [(xv, 'row'), (g, 'vec'), (sh, 'vec'), (sc, 'vec'), (dh, 'row'), (dres, dres_kind)],
                     [((rows, D), F32, 'row'), ((1, D), F32, 'acc'), ((nrow, D), F32, 'acc'), ((nrow, D), F32, 'acc')],
                     nc=ncr)

    def ffn_bwd(tag, dxo, h, a_, b_, s_, f_, gate, wg, wu, wd, on_dwd=None):
        rows = dxo.shape[0]
        df, dgate = res_bwd(tag + "_dres", dxo, f_, gate, 0.5)
        dwd = _mm(tag + "_dwd", [(s_, df, rows)], F, D, tm=_div(F, 512), tn=_div(D, 1024), ta=True, epi=ident,
                  outs=[(BF16, False)])[0].reshape(N_CHIPS, F4, D)
        if on_dwd is not None:
            on_dwd(dwd)
        da, db = _mm(tag + "_dact", [(df, wd, D)], rows, F, tm=_div(rows, 512), tn=F4, tb=True, epi=swiglu_bwd_epi,
                     outs=[(BF16, False), (BF16, False)], rows=[(a_, 0, 0), (b_, 0, 0)])
        dwg = _mm(tag + "_dwg", [(h, da, rows)], D, F, tm=_div(D, 512), tn=F4, ta=True, epi=ident,
                  outs=[(BF16, True)])[0]
        dwu = _mm(tag + "_dwu", [(h, db, rows)], D, F, tm=_div(D, 512), tn=F4, ta=True, epi=ident,
                  outs=[(BF16, True)])[0]
        dh = _mm(tag + "_dh", [(da, wg, F), (db, wu, F)], rows, D, tm=_div(rows, 768), tn=_div(D, 1024), nk=N_CHIPS,
                 tb=True, epi=ident, outs=[(F32, False)], summed=True)[0]
        return dh, dgate, dwg, dwu, dwd

    dh3, dg3, dwg2, dwu2, dwd2 = ffn_bwd("ffn2", dy, h3, a3, b3, s3, f3, g3, Wt['w_ffn2_gate'], Wt['w_ffn2_up'],
                                         Wt['w_ffn2_down'])
    tok_r1, scatter_fin1 = _scatter_split("scatter_ffn2", [dwg2, dwu2, dwd2], dg3)
    dx2, dng3, dsh3, dsc3 = norm_mod_bwd("norm3_bwd", x2, ng[2:3], sh3, sc3, dh3, dy, 'row')
    dmix, dg2 = res_bwd("mix_dres", dx2, mix, g2 + tok_r1[0:1, 0:1], 1.0)

    def dmerge_epi(accs, rows, vecs, ri):
        dm_, ba_, bs_ = accs[0], rows[0], rows[1]
        ga, gs = _sigmoid(rows[2]), _sigmoid(rows[3])
        return [dm_ * ga, dm_ * gs, dm_ * ba_ * ga * (1.0 - ga), dm_ * bs_ * gs * (1.0 - gs)]

    tnd = _div(D, 1024)
    dba, dbs, dga, dgs = _mm("dmerge", [(dmix, Wt['w_out'], D)], L, D, tm=tr, tn=tnd, tb=True, epi=dmerge_epi,
                             outs=[(BF16, False)] * 4,
                             rows=[(ba, 0, 0), (bs, 0, 0), (proj, ncr, 2 * D // tnd), (proj, ncr, 3 * D // tnd)])
    dwout = _mm("dw_out", [(merged, dmix, L)], D, D, tm=_div(D, 512), tn=_div(D, 1024), ta=True, epi=ident,
                outs=[(BF16, False)])[0].reshape(N_CHIPS, Dq, D)
    dattn = _mm("dattn", [(dba, Wt['w_br_attn'], D)], L, D, tm=_div(L, 512), tn=_div(D, 1024), tb=True, epi=ident,
                outs=[(BF16, False)])[0]
    dwba = _mm("dw_br_attn", [(attn, dba, L)], D, D, tm=_div(D, 512), tn=_div(D, 1024), ta=True, epi=ident,
               outs=[(BF16, False)])[0].reshape(N_CHIPS, Dq, D)
    dy2 = _mm("dy2", [(dbs, Wt['w_br_ssm'], D)], L, W, tm=_div(L, 512), tn=_div(W, 1024), nk=N_CHIPS, tb=True,
              epi=ident, outs=[(F32, False)])[0]
    dwbs = _mm("dw_br_ssm", [(y2, dbs, L)], W, D, tm=_div(W, 512), tn=_div(Dq, 512), ta=True, epi=ident,
               outs=[(BF16, True)])[0]

    def glu_bwd_fn(i, d2, ygt, zt):
        sz = _sigmoid(zt)
        dz_ = d2 * ygt * sz * (1.0 - sz)
        return [dz_, d2 * sz, jnp.sum(dz_, axis=0, keepdims=True)]

    dz, dyd, dbglu = _rowk("glu_bwd", glu_bwd_fn, L, tr, [(dy2, 'row'), (yg, 'row'), (zglu, 'row')],
                           [((L, W), BF16, 'row'), ((L, W), F32, 'row'), ((1, W), F32, 'acc')])

    def dssm_epi(accs, rows, vecs, ri):
        _, vjp = jax.vjp(_gelu, rows[1])
        ds_ = vjp(accs[0] + rows[0])[0]
        return [ds_, ds_]

    dssm, dssm_b = _mm("dssm", [(dz, Wt['w_glu'], W)], L, W, tm=_div(L, 512), tn=_div(W, 512), tb=True, epi=dssm_epi,
                       outs=[(F32, False), (BF16, False)], rows=[(dyd, 0, 0), (ssm_pre, 0, 0)])
    dwglu = _mm("dw_glu", [(ygb, dz, L)], W, W, tm=_div(W, 512), tn=_div(W, 1024), ta=True, epi=ident,
                outs=[(BF16, False)])[0].reshape(N_CHIPS, W // N_CHIPS, W)
    tok_r2a, scatter_fin2a = _scatter_split("scatter_mix", [dwglu, dwba, dwbs, dwout], dbglu)
    dssm_full = jnp.concatenate([jnp.zeros((Lc, W), BF16), dssm_b], axis=0)
    dus, dlam_re, dlam_im, dcoef_re, dcoef_im, dbf, dcf_re, dcf_im = [], [], [], [], [], [], [], []
    for d in range(2):
        r = _ssm_bwd("ssm_bwd%d" % d, dssm_full, hs_re[d], hs_im[d], ub, bbd[d], bbdt_re[d], bbdt_im[d],
                     cbdt_re[d], cbdt_im[d], lam_re[d] + tok_r2a[0:1, 0:1], lam_im[d], coef_re[d], coef_im[d], Lc,
                     reverse=bool(d))
        for lst, val in zip((dus, dlam_re, dlam_im, dcoef_re, dcoef_im, dbf, dcf_re, dcf_im), r):
            lst.append(val)
    dqr, dkr, dvf = _attn_bwd(qr, kr, vb, dattn, L, Lc, D)

    def prep_bwd_fn(i, qt, kt, ut, dqt, dkt, dvt, du0, du1, dst, dt, qgt, kgt, ct, st):
        live = i >= ncr
        dqt = jnp.where(live, dqt, 0.0)
        dst = jnp.where(live, dst, 0.0)
        dqs, dks = [], []
        dqg_ = jnp.zeros((1, HEAD_DIM), F32)
        dkg_ = jnp.zeros((1, HEAD_DIM), F32)
        for h in range(nh):
            hl = slice(h * HEAD_DIM, (h + 1) * HEAD_DIM)
            dn = dqt[:, hl] * ct + _rot(dqt[:, hl] * st)
            _, vjp = jax.vjp(_head_norm, qt[:, hl], qgt)
            dxh, dgh = vjp(dn)
            dqs.append(dxh)
            dqg_ = dqg_ + dgh
        for h in range(nkvh):
            hl = slice(h * HEAD_DIM, (h + 1) * HEAD_DIM)
            dn = dkt[:, hl] * ct + _rot(dkt[:, hl] * st)
            _, vjp = jax.vjp(_head_norm, kt[:, hl], kgt)
            dxh, dgh = vjp(dn)
            dks.append(dxh)
            dkg_ = dkg_ + dgh
        du_ = du0 + du1 + dst * dt
        return [jnp.concatenate(dqs, axis=1), jnp.concatenate(dks, axis=1), dvt, du_, dqg_, dkg_,
                jnp.sum(dst * ut, axis=0, keepdims=True)]

    dq_b, dk_b, dv_b, du_b, dqg, dkg, dssd = _rowk(
        "qk_prep_bwd", prep_bwd_fn, T, tr,
        [(proj, ('col', D, 1)), (proj, ('col', KV, 0)), (proj, ('col', W, 1)), (dqr, 'xrow'), (dkr, 'row'),
         (dvf, 'row'), (dus[0], 'row'), (dus[1], 'row'), (dssm, 'xrow'), (ssm_d, 'vec'), (qg, 'vec'), (kg, 'vec'),
         (cos_t, 'row'), (sin_t, 'row')],
        [((T, D), BF16, 'row'), ((T, KV), BF16, 'row'), ((T, KV), BF16, 'row'), ((T, W), BF16, 'row'),
         ((1, HEAD_DIM), F32, 'acc'), ((1, HEAD_DIM), F32, 'acc'), ((1, W), F32, 'acc')], nc=ncr)
    dgate = jnp.concatenate([jnp.zeros((Lc, 2 * D), BF16), jnp.concatenate([dga, dgs], axis=1)], axis=0)
    dproj = jnp.concatenate([dk_b, dv_b, du_b, dq_b, dgate], axis=1)
    dh2 = _mm("in_proj_dx", [(dproj, Wt['w_in'], 4 * D)], T, D, tm=_div(T, 768), tn=_div(D, 1024), nk=N_CHIPS, tb=True,
              epi=ident, outs=[(F32, False)])[0]
    dwin = _mm("in_proj_dw", [(h2, dproj, T)], D, 4 * D, tm=_div(D, 512), tn=_div(D, 1024), ta=True, epi=ident,
               outs=[(BF16, True)])[0]
    tok_r2, scatter_fin2 = _scatter_split("scatter_w_in", [dwin], dqg)
    dx1, dng2, dsh2, dsc2 = norm_mod_bwd("norm2_bwd", x1, ng[1:2] + tok_r2[0:1, 0:1], sh2, sc2, dh2, dx2, 'xrow')
    early = {}

    def start_down(dwd):
        early['tok'], early['fin'] = _scatter_split("scatter_ffn1_down", [dwd], dg2)

    dh1, dg1, dwg1, dwu1, dwd1 = ffn_bwd("ffn1", dx1, h1, a1, b1, s1, f1, g1, Wt['w_ffn1_gate'], Wt['w_ffn1_up'],
                                         Wt['w_ffn1_down'], on_dwd=start_down)
    dx0, dng1, dsh1, dsc1 = norm_mod_bwd("norm1_bwd", xc, ng[0:1] + early['tok'][0:1, 0:1], sh1, sc1, dh1, dx1, 'row')
    grad_x = dx0[Lc:][None]

    zD = jnp.zeros((1, D), F32)
    dmod_x = jnp.concatenate([dsh1[1:2], dsc1[1:2], dg1[1:2], dsh2[1:2], dsc2[1:2], dg2, dsh3, dsc3, dg3], axis=1)
    dmod_c = jnp.concatenate([dsh1[0:1], dsc1[0:1], dg1[0:1], dsh2[0:1], dsc2[0:1], zD, zD, zD, zD], axis=1)
    pieces = [dmod_x, dmod_c, dng1, dng2, dng3, dqg, dkg] + dlam_re + dlam_im + dcoef_re + dcoef_im \
        + dbf + dcf_re + dcf_im + [dssd, dbglu]
    shapes = [p_.shape for p_ in pieces]
    pack = _pack(pieces)
    RP = pack.shape[0]
    tok_small, small_gathered = _allgather_split("gather_small", pack, me, dng1)
    tok_r3, scatter_fin3 = _scatter_split("scatter_ffn1_up", [dwg1, dwu1], tok_small)
    results = {}

    def sum_group(tag, names, fin, after_work):
        sent, landed = fin(after_work)
        plane = [_sum_plane("sum_" + n, g_, rb, chip_index) for n, g_, rb in zip(names, sent, landed)]
        tok_, swapped = _swap_split("swap_" + tag, plane, plane[0])
        return tok_, (names, swapped)

    def update_group(group, after_work):
        names, swapped = group
        mine, theirs = swapped(after_work)
        for n, m_, t_ in zip(names, mine, theirs):
            results[n] = _adamw("adamw_" + n, A[n], A['m_' + n], A['v_' + n], [m_, t_])

    tok_a, grp_ffn2 = sum_group("ffn2", big[3:6], scatter_fin1, tok_r3)
    tok_b, grp_mix = sum_group("mix", big[7:11], scatter_fin2a, tok_a)
    tok_c, grp_w_in = sum_group("w_in", big[6:7], scatter_fin2, tok_b)
    update_group(grp_ffn2, tok_c)
    tok_d, grp_down = sum_group("ffn1_down", big[2:3], early['fin'], results['w_ffn2_down'][0])
    update_group(grp_mix, tok_d)
    update_group(grp_w_in, results['w_out'][0])
    update_group(grp_down, results['w_in'][0])
    allp = small_gathered(results['w_ffn1_down'][0])
    head_rows = -(-18 * D // PACK_W)
    head = allp[:, :head_rows].reshape(N_DEV, head_rows * PACK_W)
    dmx_all = head[:, :9 * D]

    def sum_rows_fn(i, t):
        s_ = t[0:1]
        for k in range(1, N_DEV):
            s_ = s_ + t[k:k + 1]
        return [s_]

    dmc_sum = _rowk("sum_dmod_c", sum_rows_fn, 1, 1, [(head[:, 9 * D:18 * D], 'vec')], [((1, 9 * D), F32, 'row')])[0]
    cots = jnp.concatenate([dmx_all, dmc_sum, jnp.zeros((7, 9 * D), F32)], axis=0)
    cots_sh = lax.dynamic_slice(cots, (0, chip * NM), (16, NM))
    part = _mm("cctx_part", [(cots_sh[8:16], wm, NM)], 8, D, tm=8, tn=_div(D, 1024), nk=NM // _div(NM, 1152), tb=True,
               epi=ident, outs=[(F32, False)], a_pro=to_bf, b_pro=to_bf)[0]
    _, cctx_gathered = _allgather_split("gather_cctx", part, me, part)

    def sum_dev_fn(i, t):
        s_ = t[0]
        for k in range(1, N_DEV):
            s_ = s_ + t[k]
        return [s_]

    tot = _rowk("sum_small", sum_dev_fn, RP, 8, [(allp, 'row3')], [((RP, PACK_W), F32, 'row')])[0]
    (t_dmod_x, t_dmod_c, t_ng1, t_ng2, t_ng3, t_qg, t_kg, t_lr0, t_lr1, t_li0, t_li1, t_kr0, t_kr1, t_ki0, t_ki1,
     t_dbf0, t_dbf1, t_dcr0, t_dcr1, t_dci0, t_dci1, t_d, t_bglu) = _unpack(tot, shapes)
    b_grad = lambda t, lo: jnp.transpose(t[:, :, lo:lo + P].reshape(G, E, P), (0, 2, 1))
    c_grad = lambda t: jnp.transpose(t.reshape(nslab, P, SLAB_GROUPS, E), (0, 2, 3, 1)).reshape(G, E, P)
    cat2 = lambda u0, u1: jnp.concatenate([u0.reshape(G, P), u1.reshape(G, P)], axis=0)
    g_are, g_aim, g_ldt = _zoh_bwd(a_re2, a_im2, ldt2, [cat2(t_lr0, t_lr1), cat2(t_li0, t_li1),
                                                         cat2(t_kr0, t_kr1), cat2(t_ki0, t_ki1)])
    g_bmod = _rowk("bmod_grad", lambda i, u0, u1: [u0 + u1], 1, 1, [(t_dmod_x, 'row'), (t_dmod_c, 'row')],
                   [((1, 9 * D), F32, 'row')])[0]
    g_wmod = _outer_sum(acts, cots_sh)
    results['w_mod'] = _adamw("adamw_w_mod", w_mod, m_w_mod, v_w_mod, [g_wmod])
    done = sum(results[n][1].reshape(-1, results[n][1].shape[-1])[0:1, 0:1] for n in list(results)) + g_are[0:1, 0:1] \
        + g_bmod[0:1, 0:1]
    tok_e, grp_up = sum_group("ffn1_up", big[0:2], scatter_fin3, done)
    parts = cctx_gathered(tok_e).reshape(N_CHIPS, 2, 8, D)[:, 0, 0]

    def cctx_fn(i, pt, ct):
        ds_ = ((pt[0:1] + pt[1:2]) + pt[2:3]) + pt[3:4]
        _, vjp = jax.vjp(lambda v: v * _sigmoid(v), ct)
        return [vjp(ds_)[0]]

    g_cctx = _rowk("cctx_grad", cctx_fn, 1, 1, [(parts, 'vec'), (c_ctx[None], 'row')], [((1, D), F32, 'row')])[0]

    ng_full =jnp.concatenate([t_ng1, t_ng2, t_ng3], axis=0)
    gsmall = {
        'c_ctx': g_cctx, 'b_mod': g_bmod, 'norm_g': lax.dynamic_slice(ng_full, (0, chip * Dq), (3, Dq)),
        'q_norm_g': t_qg, 'k_norm_g': t_kg, 'ssm_a_re': g_are, 'ssm_a_im': g_aim, 'ssm_log_dt': g_ldt,
        'ssm_b_re': jnp.stack([b_grad(t_dbf0, 0), b_grad(t_dbf1, 0)]),
        'ssm_b_im': jnp.stack([b_grad(t_dbf0, P), b_grad(t_dbf1, P)]),
        'ssm_c_re': jnp.stack([c_grad(t_dcr0), c_grad(t_dcr1)]), 'ssm_c_im': jnp.stack([c_grad(t_dci0), c_grad(t_dci1)]),
        'ssm_d': t_d, 'b_glu': t_bglu}
    sshapes = [A[n].shape for n in small]
    sres = _adamw("adamw_small", packs_wmv[0], packs_wmv[1], packs_wmv[2], [_pack([gsmall[n] for n in small])])
    update_group(grp_up, sres[0])
    sres = [_unpack(b_, sshapes) for b_ in sres]
    for k, n in enumerate(small):
        results[n] = tuple(sres[q][k] for q in range(4))

    order = ['c_ctx', 'w_mod', 'b_mod', 'norm_g', 'w_ffn1_gate', 'w_ffn1_up', 'w_ffn1_down', 'w_in', 'q_norm_g',
             'k_norm_g', 'ssm_a_re', 'ssm_a_im', 'ssm_log_dt', 'ssm_b_re', 'ssm_b_im', 'ssm_c_re', 'ssm_c_im',
             'ssm_d', 'w_glu', 'b_glu', 'w_br_attn', 'w_br_ssm', 'w_out', 'w_ffn2_gate', 'w_ffn2_up', 'w_ffn2_down']
    outs = [loss, grad_x]
    for q in range(4):
        outs += [results[n][q].reshape(A[n].shape) for n in order]
    return tuple(outs)
```

```python
import math

import jax
import jax.numpy as jnp
from jax import lax
from jax.experimental import pallas as pl
from jax.experimental.pallas import tpu as pltpu

F32 = jnp.float32
BF16 = jnp.bfloat16
MESH = pl.DeviceIdType.MESH

NORM_EPS = 1e-6
ROPE_THETA = 10000.0
GRID_W = 64
HEAD_DIM = 128
Q_PER_KV = 4
SSM_GROUP = 16
SSM_STATE = 64
ADAM_LR = 0.001
ADAM_B1 = 0.9
ADAM_B2 = 0.999
ADAM_EPS = 1e-08
ADAM_WD = 0.01
ADAM_STEP = 10

N_CHIPS = 4
N_DEV = 8
LANES = 128
SLAB_CH = 128
SLAB_GROUPS = SLAB_CH // SSM_GROUP
SLAB_ST = SLAB_GROUPS * SSM_STATE
VMEM_LIMIT_BYTES = 56 * 1024 * 1024
PACK_W = 1024


def _cparams(**kw):
    return pltpu.CompilerParams(vmem_limit_bytes=VMEM_LIMIT_BYTES, **kw)


def _div(n, pref, mult=LANES):
    t = (min(pref, n) // mult) * mult
    while t >= mult:
        if n % t == 0:
            return t
        t -= mult
    return n


def _sigmoid(x):
    return jax.nn.sigmoid(x)


def _gelu(x):
    return x * (0.5 * (1.0 + jnp.tanh(math.sqrt(2.0 / math.pi) * (x + 0.044715 * (x * x * x)))))


def _rowk(name, fn, nrows, tr, ins, outs, nc=0):
    nt = nrows // tr
    in_specs, arrays = [], []
    for arr, kind in ins:
        arrays.append(arr)
        if kind == 'row':
            in_specs.append(pl.BlockSpec((tr, arr.shape[1]), lambda i: (i, 0)))
        elif kind == 'xrow':
            in_specs.append(pl.BlockSpec((tr, arr.shape[1]), lambda i: (jnp.maximum(i - nc, 0), 0)))
        elif kind == 'orow':
            in_specs.append(pl.BlockSpec((tr, arr.shape[1]), lambda i: (i + nc, 0)))
        elif kind == 'vec':
            in_specs.append(pl.BlockSpec(arr.shape, lambda i, nd=arr.ndim: (0,) * nd))
        elif kind == 'row3':
            in_specs.append(pl.BlockSpec((arr.shape[0], tr, arr.shape[2]), lambda i: (0, i, 0)))
        elif kind == 'row1':
            in_specs.append(pl.BlockSpec((None, tr, arr.shape[2]), lambda i: (0, i, 0)))
        elif kind[0] == 'ocol':
            _, width, blk = kind
            in_specs.append(pl.BlockSpec((tr, width), lambda i, blk=blk: (i + nc, blk)))
        else:
            _, width, blk = kind
            in_specs.append(pl.BlockSpec((tr, width), lambda i, blk=blk: (i, blk)))
    out_shape, out_specs = [], []
    for shape, dtype, kind in outs:
        out_shape.append(jax.ShapeDtypeStruct(shape, dtype))
        if kind == 'row':
            out_specs.append(pl.BlockSpec((tr, shape[1]), lambda i: (i, 0)))
        elif kind == 'row1':
            out_specs.append(pl.BlockSpec((None, tr, shape[2]), lambda i: (0, i, 0)))
        else:
            out_specs.append(pl.BlockSpec(shape, lambda i, nd=len(shape): (0,) * nd))
    nin = len(ins)

    def body(*refs):
        i = pl.program_id(0)
        res = fn(i, *[r[...] for r in refs[:nin]])
        for (shape, dtype, kind), ref, val in zip(outs, refs[nin:], res):
            if kind in ('row', 'row1'):
                ref[...] = val.astype(dtype)
            else:
                @pl.when(i == 0)
                def _():
                    ref[...] = val.astype(dtype)

                @pl.when(i > 0)
                def _():
                    ref[...] += val.astype(dtype)

    return pl.pallas_call(body, name=name, grid=(nt,), in_specs=in_specs, out_specs=out_specs,
                          out_shape=out_shape, compiler_params=_cparams())(*arrays)


def _mm(name, pairs, M, N, *, tm, tn, nk=1, epi, outs, ta=False, tb=False, rows=(), vecs=(),
        a_pro=None, b_pro=None, n_outer=True, summed=False):
    nm, nn = M // tm, N // tn
    npair = len(pairs)

    def idx(f):
        if n_outer:
            return lambda j, i, k: f(i, j, k)
        return lambda i, j, k: f(i, j, k)

    in_specs, args = [], []
    for a, b, K in pairs:
        tk = K // nk
        if ta:
            in_specs.append(pl.BlockSpec((tk, tm), idx(lambda i, j, k: (k, i))))
        else:
            in_specs.append(pl.BlockSpec((tm, tk), idx(lambda i, j, k: (i, k))))
        args.append(a)
        if b.ndim == 3:
            if tb:
                per = b.shape[2] // tk
                in_specs.append(pl.BlockSpec((None, tn, tk), idx(lambda i, j, k, per=per: (k // per, j, k % per))))
            else:
                per = b.shape[2] // tn
                in_specs.append(pl.BlockSpec((None, tk, tn), idx(lambda i, j, k, per=per: (j // per, k, j % per))))
        elif tb:
            in_specs.append(pl.BlockSpec((tn, tk), idx(lambda i, j, k: (j, k))))
        else:
            in_specs.append(pl.BlockSpec((tk, tn), idx(lambda i, j, k: (k, j))))
        args.append(b)
    for arr, ro, co in rows:
        in_specs.append(pl.BlockSpec((tm, tn), idx(lambda i, j, k, ro=ro, co=co: (i + ro, j + co))))
        args.append(arr)
    for arr in vecs:
        in_specs.append(pl.BlockSpec((arr.shape[0], tn), idx(lambda i, j, k: (0, j))))
        args.append(arr)
    out_shape, out_specs = [], []
    for dtype, chunked in outs:
        if chunked:
            per = (N // N_CHIPS) // tn
            out_shape.append(jax.ShapeDtypeStruct((N_CHIPS, M, N // N_CHIPS), dtype))
            out_specs.append(pl.BlockSpec((None, tm, tn), idx(lambda i, j, k, per=per: (j // per, i, j % per))))
        else:
            out_shape.append(jax.ShapeDtypeStruct((M, N), dtype))
            out_specs.append(pl.BlockSpec((tm, tn), idx(lambda i, j, k: (i, j))))
    nacc = 1 if summed else npair
    scratch = [pltpu.VMEM((tm, tn), F32) for _ in range(nacc)] if nk > 1 else []
    nrow, nvec, nout = len(rows), len(vecs), len(outs)
    dims = (((0 if ta else 1,), (1 if tb else 0,)), ((), ()))

    def body(*refs):
        ab = refs[:2 * npair]
        row_refs = refs[2 * npair:2 * npair + nrow]
        vec_refs = refs[2 * npair + nrow:2 * npair + nrow + nvec]
        out_refs = refs[2 * npair + nrow + nvec:2 * npair + nrow + nvec + nout]
        acc_refs = refs[2 * npair + nrow + nvec + nout:]
        if n_outer:
            j, i, k = pl.program_id(0), pl.program_id(1), pl.program_id(2)
        else:
            i, j, k = pl.program_id(0), pl.program_id(1), pl.program_id(2)

        def part(p):
            av, bv = ab[2 * p][...], ab[2 * p + 1][...]
            if a_pro is not None:
                av = a_pro(av)
            if b_pro is not None:
                bv = b_pro(bv)
            return lax.dot_general(av, bv, dims, preferred_element_type=F32)

        def finish(accs):
            row_index = i * tm + lax.broadcasted_iota(jnp.int32, (tm, 1), 0)
            res = epi(accs, [r[...] for r in row_refs], [v[...] for v in vec_refs], row_index)
            for ref, val in zip(out_refs, res):
                ref[...] = val.astype(ref.dtype)

        parts = [part(p) for p in range(npair)]
        if summed:
            total = parts[0]
            for extra in parts[1:]:
                total = total + extra
            parts = [total]
        if nk == 1:
            finish(parts)
        else:
            @pl.when(k == 0)
            def _():
                for q in range(nacc):
                    acc_refs[q][...] = parts[q]

            @pl.when(jnp.logical_and(k > 0, k < nk - 1))
            def _():
                for q in range(nacc):
                    acc_refs[q][...] += parts[q]

            @pl.when(k == nk - 1)
            def _():
                finish([acc_refs[q][...] + parts[q] for q in range(nacc)])

    grid = (nn, nm, nk) if n_outer else (nm, nn, nk)
    return pl.pallas_call(body, name=name, grid=grid, in_specs=in_specs, out_specs=out_specs,
                          out_shape=out_shape, scratch_shapes=scratch, compiler_params=_cparams())(*args)


def _split3(v):
    v0 = v.astype(BF16)
    r1 = v - v0.astype(F32)
    v1 = r1.astype(BF16)
    v2 = (r1 - v1.astype(F32)).astype(BF16)
    return v0, v1, v2


def _mesh_pos():
    return lax.axis_index("x"), lax.axis_index("y"), lax.axis_index("c")


def _allgather_small(name, x):
    m, n = x.shape

    def body(x_ref, out_ref, send_sems, recv_sems, local_sem):
        xi, yi, ci = _mesh_pos()
        me, sibling = (xi, yi, ci), (xi, yi, 1 - ci)
        chips = [(1 - xi, yi), (xi, 1 - yi), (1 - xi, 1 - yi)]

        def rows(px, py, pc):
            return out_ref.at[pl.ds((4 * px + 2 * py + pc) * m, m), :]

        def copy(k, block, to, src=None):
            return pltpu.make_async_remote_copy(
                src_ref=rows(*block) if src is None else src, dst_ref=rows(*block),
                send_sem=send_sems.at[k], recv_sem=recv_sems.at[k], device_id=to, device_id_type=MESH)

        mine = pltpu.make_async_copy(x_ref, rows(*me), local_sem)
        mine.start()
        first = [copy(0, me, sibling, src=x_ref)]
        first += [copy(1 + j, me, (*chip, ci), src=x_ref) for j, chip in enumerate(chips)]
        for cp in first:
            cp.start()
        passed = [copy(4 + j, (*chip, ci), sibling) for j, chip in enumerate(chips)]
        for j, chip in enumerate(chips):
            copy(1 + j, (*chip, ci), me).wait_recv()
            passed[j].start()
        copy(0, sibling, me).wait_recv()
        for j, chip in enumerate(chips):
            copy(4 + j, (*chip, 1 - ci), me).wait_recv()
        for cp in first + passed:
            cp.wait_send()
        mine.wait()

    return pl.pallas_call(
        body, name=name, out_shape=jax.ShapeDtypeStruct((N_DEV * m, n), x.dtype),
        in_specs=[pl.BlockSpec(memory_space=pltpu.VMEM)], out_specs=pl.BlockSpec(memory_space=pltpu.VMEM),
        scratch_shapes=[pltpu.SemaphoreType.DMA((7,)), pltpu.SemaphoreType.DMA((7,)), pltpu.SemaphoreType.DMA],
        compiler_params=_cparams())(x)


_HBM = pl.BlockSpec(memory_space=pltpu.HBM)
_SEM = pl.BlockSpec(memory_space=pltpu.SEMAPHORE)
_ANY = pl.BlockSpec(memory_space=pl.ANY)
_EFFECT = pltpu.SideEffectType.DATAFLOW_SIDE_EFFECTING


def _in_hbm(v):
    return pltpu.with_memory_space_constraint(v, pltpu.HBM)


def _other_chips(xi, yi):
    return [(1 - xi, yi), (xi, 1 - yi), (1 - xi, 1 - yi)]


def _guarded(core, fn):
    if core is None:
        fn()
    else:
        pl.when(lax.axis_index("c") == core)(fn)


def _split_copies(name, srcs, lands, after, pairs, senders, receivers, ncopy):
    ns, nl = len(srcs), len(lands)
    dma = pltpu.SemaphoreType.DMA((ncopy,))
    thru = [pltpu.HBM(v.shape, v.dtype) for v in list(srcs) + list(lands)]

    def start_body(*refs):
        src_refs, land_refs = refs[:ns], refs[ns:ns + nl]
        descs = pairs(src_refs, land_refs, refs[ns + nl + 1], refs[ns + nl + 2])

        def go():
            for send, _ in descs:
                send.start()

        _guarded(senders, go)
        refs[-1][...] = jnp.zeros_like(refs[-1])

    res = pl.pallas_call(
        start_body, name=name + "_start",
        out_shape=(dma, dma, *thru, jax.ShapeDtypeStruct((8, LANES), F32)),
        in_specs=[_HBM] * (ns + nl) + [_ANY],
        out_specs=(_SEM, _SEM, *([_HBM] * (ns + nl)), pl.BlockSpec(memory_space=pltpu.VMEM)),
        input_output_aliases={k: 2 + k for k in range(ns + nl)},
        compiler_params=_cparams(has_side_effects=_EFFECT),
    )(*[_in_hbm(v) for v in srcs], *[_in_hbm(v) for v in lands], after)
    send_sems, recv_sems, token = res[0], res[1], res[-1]
    carried = res[2:2 + ns + nl]

    def finish(after_work):
        def wait_body(*refs):
            src_refs, land_refs = refs[:ns], refs[ns:ns + nl]
            descs = pairs(src_refs, land_refs, refs[ns + nl], refs[ns + nl + 1])

            def sent():
                for send, _ in descs:
                    send.wait_send()

            def landed():
                for _, recv in descs:
                    recv.wait_recv()

            _guarded(senders, sent)
            _guarded(receivers, landed)

        out = pl.pallas_call(
            wait_body, name=name + "_wait", out_shape=tuple(thru),
            in_specs=[_HBM] * (ns + nl) + [_SEM, _SEM, _ANY], out_specs=tuple([_HBM] * (ns + nl)),
            input_output_aliases={k: k for k in range(ns + nl)},
            compiler_params=_cparams(has_side_effects=_EFFECT),
        )(*carried, send_sems, recv_sems, after_work)
        return list(out[:ns]), list(out[ns:])

    return token, finish


def _cast_slot(name, w, chip_index, after):
    R, C = w.shape[1:]
    tr = _div(R, max(16, 524288 // C), mult=16)

    def body(chip_ref, w_ref, after_ref, o_ref):
        o_ref[...] = w_ref[...].astype(BF16)

    return pl.pallas_call(
        body, name=name, out_shape=jax.ShapeDtypeStruct((N_CHIPS, R, C), BF16),
        grid_spec=pltpu.PrefetchScalarGridSpec(
            num_scalar_prefetch=1, grid=(R // tr,),
            in_specs=[pl.BlockSpec((None, tr, C), lambda i, chip_ref: (0, i, 0)), _ANY],
            out_specs=pl.BlockSpec((None, tr, C), lambda i, chip_ref: (chip_ref[0], i, 0))),
        compiler_params=_cparams())(chip_index, w, after)


def _sum_plane(name, grads, landed, chip_index):
    R, C = grads.shape[1:]
    tr = _div(R, max(16, 262144 // C), mult=16)

    def body(chip_ref, own_ref, land_ref, o_ref):
        o_ref[...] = ((own_ref[...].astype(F32) + land_ref[0].astype(F32)) + land_ref[1].astype(F32)) \
            + land_ref[2].astype(F32)

    return pl.pallas_call(
        body, name=name, out_shape=jax.ShapeDtypeStruct((R, C), F32),
        grid_spec=pltpu.PrefetchScalarGridSpec(
            num_scalar_prefetch=1, grid=(R // tr,),
            in_specs=[pl.BlockSpec((None, tr, C), lambda i, chip_ref: (chip_ref[0], i, 0)),
                      pl.BlockSpec((3, tr, C), lambda i, chip_ref: (0, i, 0))],
            out_specs=pl.BlockSpec((tr, C), lambda i, chip_ref: (i, 0))),
        compiler_params=_cparams())(chip_index, grads, landed)


def _gather_split(name, lands, after):
    def pairs(src_refs, land_refs, send_sems, recv_sems):
        xi, yi, _ = _mesh_pos()
        mine = 2 * xi + yi
        out = []
        for a in range(len(lands)):
            for j, (px, py) in enumerate(_other_chips(xi, yi)):
                def to_slot(slot, a=a, j=j, px=px, py=py):
                    return pltpu.make_async_remote_copy(
                        src_ref=land_refs[a].at[mine], dst_ref=land_refs[a].at[slot], send_sem=send_sems.at[3 * a + j],
                        recv_sem=recv_sems.at[3 * a + j], device_id=(px, py, 1), device_id_type=MESH)
                out.append((to_slot(mine), to_slot(2 * px + py)))
        return out

    return _split_copies(name, [], lands, after, pairs, senders=1, receivers=1, ncopy=3 * len(lands))


def _allgather_split(name, block, me, after):
    land = lax.dynamic_update_slice(lax.empty((N_DEV,) + block.shape, block.dtype), block[None], (me, 0, 0))

    def pairs(src_refs, land_refs, send_sems, recv_sems):
        xi, yi, ci = _mesh_pos()
        mine = 4 * xi + 2 * yi + ci
        out = []
        for k in range(1, N_DEV):
            kx, ky, kc = (k >> 2) & 1, (k >> 1) & 1, k & 1
            px = 1 - xi if kx else xi
            py = 1 - yi if ky else yi
            pc = 1 - ci if kc else ci

            def to_slot(slot, k=k, px=px, py=py, pc=pc):
                return pltpu.make_async_remote_copy(
                    src_ref=land_refs[0].at[mine], dst_ref=land_refs[0].at[slot], send_sem=send_sems.at[k - 1],
                    recv_sem=recv_sems.at[k - 1], device_id=(px, py, pc), device_id_type=MESH)
            out.append((to_slot(mine), to_slot(4 * px + 2 * py + pc)))
        return out

    tok, fin = _split_copies(name, [], [land], after, pairs, senders=None, receivers=None, ncopy=N_DEV - 1)
    return tok, lambda later: fin(later)[1][0]


def _swap_split(name, arrs, after):
    lands = [lax.empty(v.shape, v.dtype) for v in arrs]

    def pairs(src_refs, land_refs, send_sems, recv_sems):
        xi, yi, ci = _mesh_pos()
        out = []
        for a in range(len(arrs)):
            cp = pltpu.make_async_remote_copy(
                src_ref=src_refs[a], dst_ref=land_refs[a], send_sem=send_sems.at[a], recv_sem=recv_sems.at[a],
                device_id=(xi, yi, 1 - ci), device_id_type=MESH)
            out.append((cp, cp))
        return out

    return _split_copies(name, arrs, lands, after, pairs, senders=None, receivers=None, ncopy=len(arrs))


def _pass_split(name, lands, after):
    def pairs(src_refs, land_refs, send_sems, recv_sems):
        xi, yi, _ = _mesh_pos()
        out = []
        for a in range(len(lands)):
            for j, (px, py) in enumerate(_other_chips(xi, yi)):
                cp = pltpu.make_async_remote_copy(
                    src_ref=land_refs[a].at[2 * px + py], dst_ref=land_refs[a].at[2 * px + py],
                    send_sem=send_sems.at[3 * a + j], recv_sem=recv_sems.at[3 * a + j],
                    device_id=(xi, yi, 0), device_id_type=MESH)
                out.append((cp, cp))
        return out

    return _split_copies(name, [], lands, after, pairs, senders=1, receivers=0, ncopy=3 * len(lands))


def _scatter_split(name, grads, after):
    lands = [lax.empty((3,) + g.shape[1:], g.dtype) for g in grads]

    def pairs(src_refs, land_refs, send_sems, recv_sems):
        xi, yi, ci = _mesh_pos()
        out = []
        for a in range(len(grads)):
            for j, (px, py) in enumerate(_other_chips(xi, yi)):
                cp = pltpu.make_async_remote_copy(
                    src_ref=src_refs[a].at[2 * px + py], dst_ref=land_refs[a].at[j], send_sem=send_sems.at[3 * a + j],
                    recv_sem=recv_sems.at[3 * a + j], device_id=(px, py, ci), device_id_type=MESH)
                out.append((cp, cp))
        return out

    return _split_copies(name, grads, lands, after, pairs, senders=None, receivers=None, ncopy=3 * len(grads))


def _gather_finish(name, lands):
    na = len(lands)

    def body(*refs):
        outs = refs[na:2 * na]
        send_sems, recv_sems = refs[2 * na:]
        xi, yi, ci = _mesh_pos()
        passes = [pltpu.make_async_remote_copy(
            src_ref=outs[a].at[2 * px + py], dst_ref=outs[a].at[2 * px + py],
            send_sem=send_sems.at[a, j], recv_sem=recv_sems.at[a, j], device_id=(xi, yi, 0), device_id_type=MESH)
            for a in range(na) for j, (px, py) in enumerate(_other_chips(xi, yi))]

        @pl.when(ci == 1)
        def _():
            for cp in passes:
                cp.start()
            for cp in passes:
                cp.wait_send()

        @pl.when(ci == 0)
        def _():
            for cp in passes:
                cp.wait_recv()

    return pl.pallas_call(
        body, name=name, out_shape=[jax.ShapeDtypeStruct(v.shape, v.dtype) for v in lands],
        in_specs=[_ANY] * na, out_specs=[_ANY] * na,
        input_output_aliases={a: a for a in range(na)},
        scratch_shapes=[pltpu.SemaphoreType.DMA((na, 3)), pltpu.SemaphoreType.DMA((na, 3))],
        compiler_params=_cparams())(*lands)


ATTN_HEADS_PER_STEP = 2


def _attn_tiles(L, Lc, D):
    tq = min(256, Lc)
    return tq, L // tq, Lc // tq, D // HEAD_DIM // Q_PER_KV


def _attn_scores(q, k):
    return lax.dot_general(q, k, (((1,), (1,)), ((), ())), preferred_element_type=F32) * (HEAD_DIM ** -0.5)


def _softmax_rows(s):
    e = jnp.exp(s - jnp.max(s, axis=-1, keepdims=True))
    return e * (1.0 / jnp.sum(e, axis=-1, keepdims=True))


def _attn_probs(q, k):
    return _softmax_rows(_attn_scores(q, k))


def _attn_fwd(qr, kr, v, L, Lc, D):
    T = L + Lc
    tq, nq, qoff, nkv = _attn_tiles(L, Lc, D)
    hp = Q_PER_KV
    ng = Q_PER_KV // hp

    def body(q_ref, k_ref, v_ref, o_ref):
        k, vv = k_ref[...], v_ref[...]
        heads = [slice(r * HEAD_DIM, (r + 1) * HEAD_DIM) for r in range(hp)]
        scores = [_attn_scores(q_ref[:, cols], k) for cols in heads]
        probs = [_softmax_rows(s) for s in scores]
        for cols, p in zip(heads, probs):
            o_ref[:, cols] = jnp.dot(p.astype(BF16), vv, preferred_element_type=F32).astype(o_ref.dtype)

    kv_spec = pl.BlockSpec((T, HEAD_DIM), lambda h, r, q: (0, h))
    return pl.pallas_call(
        body, name="attn_fwd", grid=(nkv, ng, nq),
        in_specs=[pl.BlockSpec((tq, hp * HEAD_DIM), lambda h, r, q: (q + qoff, h * ng + r)), kv_spec, kv_spec],
        out_specs=pl.BlockSpec((tq, hp * HEAD_DIM), lambda h, r, q: (q, h * ng + r)),
        out_shape=jax.ShapeDtypeStruct((L, D), BF16), compiler_params=_cparams())(qr, kr, v)


def _attn_bwd(qr, kr, v, do, L, Lc, D):
    T = L + Lc
    tq, nq, qoff, nkv = _attn_tiles(L, Lc, D)
    scale = HEAD_DIM ** -0.5
    hp = ATTN_HEADS_PER_STEP
    ng = Q_PER_KV // hp

    def body(q_ref, k_ref, v_ref, do_ref, dq_ref, dk_ref, dv_ref):
        first = jnp.logical_and(pl.program_id(1) == 0, pl.program_id(2) == 0)
        k, vv = k_ref[...], v_ref[...]
        nt_dims, tn_dims = (((1,), (1,)), ((), ())), (((0,), (0,)), ((), ()))
        heads = [slice(r * HEAD_DIM, (r + 1) * HEAD_DIM) for r in range(hp)]
        qs = [q_ref[:, cols] for cols in heads]
        douts = [do_ref[:, cols] for cols in heads]
        scores = [_attn_scores(q, k) for q in qs]
        dps = [lax.dot_general(dout, vv, nt_dims, preferred_element_type=F32) for dout in douts]
        probs = [_softmax_rows(s) for s in scores]
        dss = [(p * (dp - jnp.sum(p * dp, axis=-1, keepdims=True)) * scale).astype(BF16) for p, dp in zip(probs, dps)]
        for cols, ds in zip(heads, dss):
            dq_ref[:, cols] = jnp.dot(ds, k, preferred_element_type=F32)
        dk = dv = None
        for q, dout, p, ds in zip(qs, douts, probs, dss):
            dk_r = lax.dot_general(ds, q, tn_dims, preferred_element_type=F32)
            dv_r = lax.dot_general(p.astype(BF16), dout, tn_dims, preferred_element_type=F32)
            dk = dk_r if dk is None else dk + dk_r
            dv = dv_r if dv is None else dv + dv_r

        @pl.when(first)
        def _():
            dk_ref[...] = dk
            dv_ref[...] = dv

        @pl.when(jnp.logical_not(first))
        def _():
            dk_ref[...] += dk
            dv_ref[...] += dv

    kv_spec = pl.BlockSpec((T, HEAD_DIM), lambda h, r, q: (0, h))
    q_spec = pl.BlockSpec((tq, hp * HEAD_DIM), lambda h, r, q: (q + qoff, h * ng + r))
    o_spec = pl.BlockSpec((tq, hp * HEAD_DIM), lambda h, r, q: (q, h * ng + r))
    return pl.pallas_call(
        body, name="attn_bwd", grid=(nkv, ng, nq),
        in_specs=[q_spec, kv_spec, kv_spec, o_spec], out_specs=[o_spec, kv_spec, kv_spec],
        out_shape=[jax.ShapeDtypeStruct((L, D), F32), jax.ShapeDtypeStruct((T, D // Q_PER_KV), F32),
                   jax.ShapeDtypeStruct((T, D // Q_PER_KV), F32)],
        compiler_params=_cparams())(qr, kr, v, do)


SUB = 8


def _doubling(xr, xi, pw_re, pw_im, lanes, first_power, period, reverse):
    n = xr.shape[0]
    rows = lax.broadcasted_iota(jnp.int32, (n, 1), 0) & (period - 1)
    for k in range(period.bit_length() - 1):
        d = 1 << k
        keep = rows < period - d if reverse else rows >= d
        sr = jnp.where(keep, pltpu.roll(xr, n - d if reverse else d, 0), 0.0)
        si = jnp.where(keep, pltpu.roll(xi, n - d if reverse else d, 0), 0.0)
        pr, pi = pw_re[first_power + k:first_power + k + 1, lanes], pw_im[first_power + k:first_power + k + 1, lanes]
        xr, xi = xr + (pr * sr - pi * si), xi + (pr * si + pi * sr)
    return xr, xi


def _scan_tile(xr, xi, tb, lanes, reverse):
    pw_re, pw_im, w8_re, w8_im, wb_re, wb_im, carry_re, carry_im, sr, si = tb
    tt = xr.shape[0]
    nb = tt // SUB
    nq = sr.shape[0]
    cols = [slice(q * LANES, (q + 1) * LANES) for q in range(nq)]
    for q in range(nq):
        sr[q] = xr[:, cols[q]]
        si[q] = xi[:, cols[q]]
    order = list(range(SUB - 2, -1, -1)) if reverse else list(range(1, SUB))
    ends_r, ends_i = [], []
    for q in range(nq):
        ql = slice(lanes.start + q * LANES, lanes.start + (q + 1) * LANES)
        lr, li = pw_re[0:1, ql], pw_im[0:1, ql]
        first_row = pl.ds(SUB - 1 if reverse else 0, nb, stride=SUB)
        pr, pi = sr[q, first_row, :], si[q, first_row, :]
        for r in order:
            rows = pl.ds(r, nb, stride=SUB)
            pr, pi = sr[q, rows, :] + (lr * pr - li * pi), si[q, rows, :] + (lr * pi + li * pr)
            sr[q, rows, :] = pr
            si[q, rows, :] = pi
        ends_r.append(pr)
        ends_i.append(pi)
    er, ei = jnp.concatenate(ends_r, axis=1), jnp.concatenate(ends_i, axis=1)
    er, ei = _doubling(er, ei, pw_re, pw_im, lanes, 3, nb, reverse)
    car, cai = carry_re[:, lanes], carry_im[:, lanes]
    wbr, wbi = wb_re[:, lanes], wb_im[:, lanes]
    er = er + (wbr * car - wbi * cai)
    ei = ei + (wbr * cai + wbi * car)
    out_block = 0 if reverse else nb - 1
    carry_re[:, lanes] = er[out_block:out_block + 1, :]
    carry_im[:, lanes] = ei[out_block:out_block + 1, :]
    blocks = lax.broadcasted_iota(jnp.int32, (nb, 1), 0)
    first = blocks == (nb - 1 if reverse else 0)
    cr = jnp.where(first, car, pltpu.roll(er, nb - 1 if reverse else 1, 0))
    ci = jnp.where(first, cai, pltpu.roll(ei, nb - 1 if reverse else 1, 0))
    for r in range(SUB):
        wr, wi = w8_re[r:r + 1, lanes], w8_im[r:r + 1, lanes]
        add_r, add_i = wr * cr - wi * ci, wr * ci + wi * cr
        for q in range(nq):
            sr[q, pl.ds(r, nb, stride=SUB), :] += add_r[:, cols[q]]
            si[q, pl.ds(r, nb, stride=SUB), :] += add_i[:, cols[q]]
    hr = jnp.concatenate([sr[q] for q in range(nq)], axis=1)
    hi = jnp.concatenate([si[q] for q in range(nq)], axis=1)
    return hr, hi, car, cai


def _scan_scratch(tt, NS):
    nb = tt // SUB
    return [pltpu.VMEM((8, NS), F32), pltpu.VMEM((8, NS), F32), pltpu.VMEM((SUB, NS), F32), pltpu.VMEM((SUB, NS), F32),
            pltpu.VMEM((nb, NS), F32), pltpu.VMEM((nb, NS), F32), pltpu.VMEM((1, NS), F32), pltpu.VMEM((1, NS), F32),
            pltpu.VMEM((SLAB_ST // LANES, tt, LANES), F32), pltpu.VMEM((SLAB_ST // LANES, tt, LANES), F32)]


def _scan_init(lr, li, tb, reverse):
    pw_re, pw_im, w8_re, w8_im, wb_re, wb_im, carry_re, carry_im, sr, _ = tb
    nb = wb_re.shape[0]
    carry_re[...] = jnp.zeros_like(carry_re)
    carry_im[...] = jnp.zeros_like(carry_im)
    pr, pi = lr, li
    for k in range(3 + nb.bit_length() - 1):
        pw_re[k:k + 1, :] = pr
        pw_im[k:k + 1, :] = pi
        if k == 3:
            l8r, l8i = pr, pi
        pr, pi = pr * pr - pi * pi, 2.0 * pr * pi
    pr, pi = lr, li
    for r in range(SUB):
        row = SUB - 1 - r if reverse else r
        w8_re[row:row + 1, :] = pr
        w8_im[row:row + 1, :] = pi
        pr, pi = pr * lr - pi * li, pr * li + pi * lr
    pr, pi = l8r, l8i
    for b in range(nb):
        row = nb - 1 - b if reverse else b
        wb_re[row:row + 1, :] = pr
        wb_im[row:row + 1, :] = pi
        pr, pi = pr * l8r - pi * l8i, pr * l8i + pi * l8r


def _ssm_tiles(T, Lc):
    tt = min(128, Lc)
    return tt, T // tt, Lc // tt


def _ssm_fwd(name, u, bbd, cbd_re, cbd_im, lam_re, lam_im, coef_re, coef_im, Lc, reverse):
    T, W = u.shape
    nslab = W // SLAB_CH
    NS = nslab * SLAB_ST
    tt, nt, nc = _ssm_tiles(T, Lc)
    if reverse:
        tile = lambda s: jnp.where(s < nc, nc - 1 - s, nt - 1 - (s - nc))
    else:
        tile = lambda s: s

    def body(u_ref, b_ref, cr_ref, ci_ref, lr_ref, li_ref, kr_ref, ki_ref, hr_ref, hi_ref, y_ref, *tb):
        @pl.when(pl.program_id(0) == 0)
        def _():
            _scan_init(lr_ref[...], li_ref[...], tb, reverse)

        for j in range(nslab):
            lanes = slice(j * SLAB_ST, (j + 1) * SLAB_ST)
            bu = jnp.dot(u_ref[:, j * SLAB_CH:(j + 1) * SLAB_CH], b_ref[j], preferred_element_type=F32)
            br, bi = bu[:, :SLAB_ST], bu[:, SLAB_ST:]
            kr, ki = kr_ref[:, lanes], ki_ref[:, lanes]
            hr, hi, _, _ = _scan_tile(kr * br - ki * bi, kr * bi + ki * br, tb, lanes, reverse)
            hrb, hib = hr.astype(BF16), hi.astype(BF16)
            hr_ref[:, lanes] = hrb
            hi_ref[:, lanes] = hib
            y_ref[:, j * SLAB_CH:(j + 1) * SLAB_CH] = (
                jnp.dot(hrb, cr_ref[j], preferred_element_type=F32)
                - jnp.dot(hib, ci_ref[j], preferred_element_type=F32))

    whole3 = lambda arr: pl.BlockSpec(arr.shape, lambda s: (0, 0, 0))
    vec = pl.BlockSpec((1, NS), lambda s: (0, 0))
    return pl.pallas_call(
        body, name=name, grid=(nt,),
        in_specs=[pl.BlockSpec((tt, W), lambda s: (tile(s), 0)), whole3(bbd), whole3(cbd_re), whole3(cbd_im),
                  vec, vec, vec, vec],
        out_specs=[pl.BlockSpec((tt, NS), lambda s: (tile(s), 0)), pl.BlockSpec((tt, NS), lambda s: (tile(s), 0)),
                   pl.BlockSpec((tt, W), lambda s: (tile(s), 0))],
        out_shape=[jax.ShapeDtypeStruct((T, NS), BF16), jax.ShapeDtypeStruct((T, NS), BF16),
                   jax.ShapeDtypeStruct((T, W), F32)],
        scratch_shapes=_scan_scratch(tt, NS),
        compiler_params=_cparams())(u, bbd, cbd_re, cbd_im, lam_re, lam_im, coef_re, coef_im)


def _ssm_bwd(name, dy, h_re, h_im, u, bbd, bbdt_re, bbdt_im, cbdt_re, cbdt_im, lam_re, lam_im,
             coef_re, coef_im, Lc, reverse):
    T, W = u.shape
    nslab = W // SLAB_CH
    NS = nslab * SLAB_ST
    tt, nt, nc = _ssm_tiles(T, Lc)
    adj_reverse = not reverse
    if reverse:
        tile = lambda s: jnp.where(s < nt - nc, nc + s, s - (nt - nc))
    else:
        tile = lambda s: nt - 1 - s

    def body(dy_ref, hr_ref, hi_ref, u_ref, b_ref, btr_ref, bti_ref, ctr_ref, cti_ref, lr_ref, li_ref,
             kr_ref, ki_ref, du_ref, dlr_ref, dli_ref, dkr_ref, dki_ref, dbf_ref, dcrf_ref, dcif_ref,
             db_ref, dcr_ref, dci_ref, *tb):
        @pl.when(pl.program_id(0) == 0)
        def _():
            _scan_init(lr_ref[...], -li_ref[...], tb, adj_reverse)
            for ref in (dlr_ref, dli_ref, dkr_ref, dki_ref, db_ref, dcr_ref, dci_ref):
                ref[...] = jnp.zeros_like(ref)

        rows = lax.broadcasted_iota(jnp.int32, (tt, 1), 0)
        far_row = tt - 1 if adj_reverse else 0
        tn_dims = (((0,), (0,)), ((), ()))
        for j in range(nslab):
            lanes = slice(j * SLAB_ST, (j + 1) * SLAB_ST)
            chans = slice(j * SLAB_CH, (j + 1) * SLAB_CH)
            dys, us = dy_ref[:, chans], u_ref[:, chans]
            er = jnp.dot(dys, ctr_ref[j], preferred_element_type=F32)
            ei = -jnp.dot(dys, cti_ref[j], preferred_element_type=F32)
            ar, ai, car, cai = _scan_tile(er, ei, tb, lanes, adj_reverse)
            shift = tt - 1 if adj_reverse else 1
            nr = jnp.where(rows == far_row, car, pltpu.roll(ar, shift, 0))
            ni = jnp.where(rows == far_row, cai, pltpu.roll(ai, shift, 0))
            hrb, hib = hr_ref[:, lanes], hi_ref[:, lanes]
            hr, hi = hrb.astype(F32), hib.astype(F32)
            dlr_ref[:, lanes] += jnp.sum(nr * hr + ni * hi, axis=0, keepdims=True)
            dli_ref[:, lanes] += jnp.sum(ni * hr - nr * hi, axis=0, keepdims=True)
            bu = jnp.dot(us, b_ref[j], preferred_element_type=F32)
            br, bi = bu[:, :SLAB_ST], bu[:, SLAB_ST:]
            dkr_ref[:, lanes] += jnp.sum(ar * br + ai * bi, axis=0, keepdims=True)
            dki_ref[:, lanes] += jnp.sum(ai * br - ar * bi, axis=0, keepdims=True)
            kr, ki = kr_ref[:, lanes], ki_ref[:, lanes]
            dbr = (ar * kr + ai * ki).astype(BF16)
            dbi = (ai * kr - ar * ki).astype(BF16)
            du_ref[:, chans] = (jnp.dot(dbr, btr_ref[j], preferred_element_type=F32)
                                + jnp.dot(dbi, bti_ref[j], preferred_element_type=F32))
            db_ref[j, :, :SLAB_ST] += lax.dot_general(us, dbr, tn_dims, preferred_element_type=F32)
            db_ref[j, :, SLAB_ST:] += lax.dot_general(us, dbi, tn_dims, preferred_element_type=F32)
            dcr_ref[j] += lax.dot_general(hrb, dys, tn_dims, preferred_element_type=F32)
            dci_ref[j] -= lax.dot_general(hib, dys, tn_dims, preferred_element_type=F32)

        @pl.when(pl.program_id(0) == nt - 1)
        def _():
            def iota(shape, axis):
                return lax.broadcasted_iota(jnp.int32, shape, axis)

            sg, ss = SSM_GROUP.bit_length() - 1, SSM_STATE.bit_length() - 1
            b_mask = (iota((SLAB_CH, SLAB_ST), 0) >> sg) == (iota((SLAB_CH, SLAB_ST), 1) >> ss)
            c_mask = (iota((SLAB_ST, SLAB_CH), 0) >> ss) == (iota((SLAB_ST, SLAB_CH), 1) >> sg)
            fold = jnp.where((iota((SLAB_ST, SSM_STATE), 0) & (SSM_STATE - 1)) == iota((SLAB_ST, SSM_STATE), 1),
                             1.0, 0.0).astype(BF16)
            fold_t = jnp.where((iota((SSM_STATE, SLAB_ST), 1) & (SSM_STATE - 1)) == iota((SSM_STATE, SLAB_ST), 0),
                               1.0, 0.0).astype(BF16)

            def exact_dot(a, b, a_is_value):
                terms = _split3(a if a_is_value else b)
                acc = None
                for t in terms:
                    part = jnp.dot(t, b, preferred_element_type=F32) if a_is_value else jnp.dot(a, t, preferred_element_type=F32)
                    acc = part if acc is None else acc + part
                return acc

            for j in range(nslab):
                dbj = db_ref[j]
                dbf_ref[j, :, :SSM_STATE] = exact_dot(jnp.where(b_mask, dbj[:, :SLAB_ST], 0.0), fold, True)
                dbf_ref[j, :, SSM_STATE:] = exact_dot(jnp.where(b_mask, dbj[:, SLAB_ST:], 0.0), fold, True)
                dcrf_ref[j] = exact_dot(fold_t, jnp.where(c_mask, dcr_ref[j], 0.0), False)
                dcif_ref[j] = exact_dot(fold_t, jnp.where(c_mask, dci_ref[j], 0.0), False)

    whole3 = lambda arr: pl.BlockSpec(arr.shape, lambda s: (0, 0, 0))
    vec = pl.BlockSpec((1, NS), lambda s: (0, 0))
    row_w = pl.BlockSpec((tt, W), lambda s: (tile(s), 0))
    row_s = pl.BlockSpec((tt, NS), lambda s: (tile(s), 0))
    dbf = jax.ShapeDtypeStruct((nslab, SLAB_CH, 2 * SSM_STATE), F32)
    dcf = jax.ShapeDtypeStruct((nslab, SSM_STATE, SLAB_CH), F32)
    return pl.pallas_call(
        body, name=name, grid=(nt,),
        in_specs=[row_w, row_s, row_s, row_w, whole3(bbd), whole3(bbdt_re), whole3(bbdt_im), whole3(cbdt_re),
                  whole3(cbdt_im), vec, vec, vec, vec],
        out_specs=[row_w, vec, vec, vec, vec, whole3(dbf), whole3(dcf), whole3(dcf)],
        out_shape=[jax.ShapeDtypeStruct((T, W), F32)] + [jax.ShapeDtypeStruct((1, NS), F32)] * 4 + [dbf, dcf, dcf],
        scratch_shapes=[pltpu.VMEM(bbd.shape, F32), pltpu.VMEM(bbdt_re.shape, F32), pltpu.VMEM(bbdt_re.shape, F32)]
        + _scan_scratch(tt, NS),
        compiler_params=_cparams())(dy, h_re, h_im, u, bbd, bbdt_re, bbdt_im, cbdt_re, cbdt_im,
                                    lam_re, lam_im, coef_re, coef_im)


def _zoh_math(a_re, a_im, log_dt):
    dt = jnp.exp(log_dt)
    mag = jnp.exp(a_re * dt)
    lb_re = mag * jnp.cos(a_im * dt)
    lb_im = mag * jnp.sin(a_im * dt)
    den = a_re * a_re + a_im * a_im
    coef_re = ((lb_re - 1.0) * a_re + lb_im * a_im) / den
    coef_im = (lb_im * a_re - (lb_re - 1.0) * a_im) / den
    return lb_re, lb_im, coef_re, coef_im


def _zoh_fwd(a_re, a_im, log_dt):
    def body(ar, ai, ld, o0, o1, o2, o3):
        for ref, val in zip((o0, o1, o2, o3), _zoh_math(ar[...], ai[...], ld[...])):
            ref[...] = val

    return pl.pallas_call(body, name="zoh_fwd", out_shape=[jax.ShapeDtypeStruct(a_re.shape, F32)] * 4,
                          compiler_params=_cparams())(a_re, a_im, log_dt)


def _zoh_bwd(a_re, a_im, log_dt, cots):
    def body(ar, ai, ld, c0, c1, c2, c3, o0, o1, o2):
        _, vjp = jax.vjp(_zoh_math, ar[...], ai[...], ld[...])
        for ref, val in zip((o0, o1, o2), vjp((c0[...], c1[...], c2[...], c3[...]))):
            ref[...] = val

    return pl.pallas_call(
        body, name="zoh_bwd",
        out_shape=[jax.ShapeDtypeStruct(a_re.shape, F32), jax.ShapeDtypeStruct(a_re.shape, F32),
                   jax.ShapeDtypeStruct(log_dt.shape, F32)],
        compiler_params=_cparams())(a_re, a_im, log_dt, *cots)


def _outer_sum(acts, cots):
    D, N = acts.shape[1], cots.shape[1]
    tm, tn = _div(D, 512), _div(N, 1152)
    dims = (((0,), (0,)), ((), ()))

    def body(a_ref, b_ref, o_ref):
        a = a_ref[...]
        aa = _split3(a * _sigmoid(a))
        bb = _split3(b_ref[...])
        acc = None
        for ia in range(3):
            for ib in range(3 - ia):
                t = lax.dot_general(aa[ia], bb[ib], dims, preferred_element_type=F32)
                acc = t if acc is None else acc + t
        o_ref[...] = acc

    return pl.pallas_call(
        body, name="mod_dw", grid=(D // tm, N // tn),
        in_specs=[pl.BlockSpec((16, tm), lambda i, j: (0, i)), pl.BlockSpec((16, tn), lambda i, j: (0, j))],
        out_specs=pl.BlockSpec((tm, tn), lambda i, j: (i, j)),
        out_shape=jax.ShapeDtypeStruct((D, N), F32), compiler_params=_cparams())(acts, cots)


def _adamw_math(w, g, m, v):
    m = ADAM_B1 * m + (1.0 - ADAM_B1) * g
    v = ADAM_B2 * v + (1.0 - ADAM_B2) * (g * g)
    m_hat = m / (1.0 - ADAM_B1 ** ADAM_STEP)
    v_hat = v / (1.0 - ADAM_B2 ** ADAM_STEP)
    delta = -ADAM_LR * (m_hat / (jnp.sqrt(v_hat) + ADAM_EPS) + ADAM_WD * w)
    return delta, m, v


def _adamw(name, w, m, v, gparts):
    R, C = w.shape[-2:]
    kind = 'row1' if w.ndim == 3 else 'row'
    tr = _div(R, max(8, 262144 // C), mult=8)

    def fn(i, wv, mv, vv, *gs):
        g = gs[0]
        for extra in gs[1:]:
            g = g + extra
        return (g,) + _adamw_math(wv, g, mv, vv)

    return _rowk(name, fn, R, tr, [(w, kind), (m, kind), (v, kind)] + [(g, 'row') for g in gparts],
                 [(w.shape, F32, kind)] * 4)


def _pack(pieces, rows_mult=8):
    flat = jnp.concatenate([p.reshape(-1).astype(F32) for p in pieces])
    unit = rows_mult * PACK_W
    total = -(-flat.shape[0] // unit) * unit
    return jnp.pad(flat, (0, total - flat.shape[0])).reshape(total // PACK_W, PACK_W)


def _unpack(buf, shapes):
    flat = buf.reshape(-1)
    out, off = [], 0
    for s in shapes:
        n = math.prod(s)
        out.append(flat[off:off + n].reshape(s))
        off += n
    return out


def _bd_expand(t):
    S, g, a, b = t.shape
    eye = jnp.eye(g, dtype=t.dtype)
    return (t[:, :, :, None, :] * eye[None, :, None, :, None]).reshape(S, g * a, g * b)


def _rope_tables(L, Lc):
    rows = L // GRID_W
    row_ids = jnp.broadcast_to(jnp.arange(rows)[:, None], (rows, GRID_W)).reshape(-1).astype(F32)
    col_ids = jnp.broadcast_to(jnp.arange(GRID_W)[None, :], (rows, GRID_W)).reshape(-1).astype(F32)
    quarter = HEAD_DIM // 4
    inv_freq = ROPE_THETA ** (-jnp.arange(quarter, dtype=F32) / quarter)
    ang_r = row_ids[:, None] * inv_freq
    ang_c = col_ids[:, None] * inv_freq
    cos = jnp.concatenate([jnp.cos(ang_r), jnp.cos(ang_r), jnp.cos(ang_c), jnp.cos(ang_c)], axis=1)
    sin = jnp.concatenate([-jnp.sin(ang_r), jnp.sin(ang_r), -jnp.sin(ang_c), jnp.sin(ang_c)], axis=1)
    cos = jnp.concatenate([jnp.ones((Lc, HEAD_DIM), F32), cos], axis=0)
    sin = jnp.concatenate([jnp.zeros((Lc, HEAD_DIM), F32), sin], axis=0)
    return cos, sin


def _rot(v):
    lane = lax.broadcasted_iota(jnp.int32, (1, HEAD_DIM), 1)
    first = (lane % (HEAD_DIM // 2)) < (HEAD_DIM // 4)
    return jnp.where(first, pltpu.roll(v, HEAD_DIM - HEAD_DIM // 4, 1), pltpu.roll(v, HEAD_DIM // 4, 1))


def _head_norm(xh, g):
    return xh * lax.rsqrt(jnp.mean(xh * xh, axis=-1, keepdims=True) + NORM_EPS) * g


def _norm_mod(xv, g, sh, sc):
    r = lax.rsqrt(jnp.mean(xv * xv, axis=-1, keepdims=True) + NORM_EPS)
    return (xv * r) * g * (1.0 + sc) + sh


def kernel(x, c, ctx, c_ctx, w_mod, b_mod, norm_g, w_ffn1_gate, w_ffn1_up, w_ffn1_down, w_in, q_norm_g, k_norm_g, ssm_a_re, ssm_a_im, ssm_log_dt, ssm_b_re, ssm_b_im, ssm_c_re, ssm_c_im, ssm_d, w_glu, b_glu, w_br_attn, w_br_ssm, w_out, w_ffn2_gate, w_ffn2_up, w_ffn2_down, loss_target, m_c_ctx, m_w_mod, m_b_mod, m_norm_g, m_w_ffn1_gate, m_w_ffn1_up, m_w_ffn1_down, m_w_in, m_q_norm_g, m_k_norm_g, m_ssm_a_re, m_ssm_a_im, m_ssm_log_dt, m_ssm_b_re, m_ssm_b_im, m_ssm_c_re, m_ssm_c_im, m_ssm_d, m_w_glu, m_b_glu, m_w_br_attn, m_w_br_ssm, m_w_out, m_w_ffn2_gate, m_w_ffn2_up, m_w_ffn2_down, v_c_ctx, v_w_mod, v_b_mod, v_norm_g, v_w_ffn1_gate, v_w_ffn1_up, v_w_ffn1_down, v_w_in, v_q_norm_g, v_k_norm_g, v_ssm_a_re, v_ssm_a_im, v_ssm_log_dt, v_ssm_b_re, v_ssm_b_im, v_ssm_c_re, v_ssm_c_im, v_ssm_d, v_w_glu, v_b_glu, v_w_br_attn, v_w_br_ssm, v_w_out, v_w_ffn2_gate, v_w_ffn2_up, v_w_ffn2_down):
    A = dict(locals())
    xi, yi, ci = _mesh_pos()
    chip = 2 * xi + yi
    me = 4 * xi + 2 * yi + ci
    L, D = x.shape[1], x.shape[2]
    Lc = ctx.shape[1]
    T = L + Lc
    F4 = w_ffn1_gate.shape[2]
    F = N_CHIPS * F4
    W, KV, Dq = D // 2, D // 4, D // 4
    G = W // SSM_GROUP
    P, E = SSM_STATE, SSM_GROUP
    NS = G * P
    nslab = W // SLAB_CH
    tr = min(256, Lc)
    ncr = Lc // tr
    assert L % tr == 0 and Lc % tr == 0 and W % SLAB_CH == 0 and D % (4 * LANES) == 0

    def sel(i, v):
        return v if v.shape[0] == 1 else jnp.where(i < ncr, v[0:1], v[1:2])

    def put(i, v, nrow):
        if nrow == 1:
            return v
        which = (i >= ncr).astype(jnp.int32)
        r2 = lax.broadcasted_iota(jnp.int32, (nrow, 1), 0)
        return jnp.where(r2 == which, jnp.broadcast_to(v, (nrow, v.shape[1])), 0.0)

    ident = lambda accs, rows, vecs, ri: [accs[0]]

    NM = w_mod.shape[2]
    first = jnp.zeros((8, D), F32).at[0].set(c[0]).at[1:4, :Dq].set(norm_g[0])
    g0 = _allgather_small("gather_c", first).reshape(N_CHIPS, 2, 8, D)
    c_all = g0[:, :, 0].reshape(N_DEV, D)
    ng = jnp.transpose(g0[:, 0, 1:4, :Dq], (1, 0, 2)).reshape(3, D)
    acts = jnp.concatenate([c_all, c_ctx[None], jnp.zeros((7, D), F32)], axis=0)
    wm = w_mod[0]
    b_shard = lax.dynamic_slice(b_mod[0], (chip * NM,), (NM,))[None]
    silu_bf = lambda a: (a * _sigmoid(a)).astype(BF16)
    to_bf = lambda b: b.astype(BF16)
    mod_part = _mm("mod_fwd", [(acts, wm, D)], 16, NM, tm=16, tn=_div(NM, 1152),
                   epi=lambda accs, rows, vecs, ri: [accs[0] + vecs[0]], outs=[(F32, False)],
                   vecs=[b_shard], a_pro=silu_bf, b_pro=to_bf)[0]
    mg = _allgather_small("gather_mod", mod_part).reshape(N_CHIPS, 2, 16, NM)[:, 0]
    mod_all = jnp.transpose(mg, (1, 0, 2)).reshape(16, N_CHIPS * NM)
    mod_x = lax.dynamic_slice(mod_all, (me, 0), (1, 9 * D))
    mod_c = jnp.where(jnp.arange(9 * D)[None] < 5 * D, mod_all[8:9], 0.0)
    modv = jnp.concatenate([mod_c, mod_x], axis=0)
    mv = lambda k: modv[:, k * D:(k + 1) * D]
    sh1, sc1, g1, sh2, sc2 = mv(0), mv(1), mv(2), mv(3), mv(4)
    g2, sh3, sc3, g3 = mv(5)[1:2], mv(6)[1:2], mv(7)[1:2], mv(8)[1:2]

    big = ['w_ffn1_gate', 'w_ffn1_up', 'w_ffn1_down', 'w_ffn2_gate', 'w_ffn2_up', 'w_ffn2_down',
           'w_in', 'w_glu', 'w_br_attn', 'w_br_ssm', 'w_out']
    row_sharded = {'w_ffn1_down', 'w_ffn2_down', 'w_glu', 'w_br_attn', 'w_out'}
    groups = [big[0:2], big[2:3], big[6:7], big[7:11], big[3:6]]
    chip_index = jnp.reshape(chip, (1,)).astype(jnp.int32)
    tok, gather_finish = modv, []
    pin = c
    for gi, names in enumerate(groups):
        tok, fin = _gather_split("gather_w%d" % gi, [_cast_slot("cast_" + n, A[n], chip_index, pin) for n in names], tok)
        gather_finish.append(fin)
        pin = tok
    ng = ng + tok[0:1, 0:1]
    Wt = {}

    def register(names, full):
        for n, gw in zip(names, full):
            Wt[n] = gw.reshape(N_CHIPS * gw.shape[1], gw.shape[2]) if n in row_sharded else gw

    def weights_ready(gi, after_work):
        _, lands = gather_finish[gi](after_work)
        register(groups[gi], _gather_finish("gather_w%d_pass" % gi, lands))

    def weights_pass(gi, after_work):
        _, lands = gather_finish[gi](after_work)
        tok_, fin_ = _pass_split("gather_w%d_pass" % gi, lands, after_work)
        return tok_, lambda later: register(groups[gi], fin_(later)[1])

    a_re2, a_im2 = ssm_a_re[0].reshape(2 * G, P), ssm_a_im[0].reshape(2 * G, P)
    ldt2 = ssm_log_dt[0].reshape(2 * G, 1)
    zoh = _zoh_fwd(a_re2, a_im2, ldt2)
    lam_re, lam_im, coef_re, coef_im = [[z[d * G:(d + 1) * G].reshape(1, NS) for d in range(2)] for z in zoh]
    bd_b = lambda b: _bd_expand(jnp.transpose(b, (0, 2, 1)).reshape(nslab, SLAB_GROUPS, E, P))
    bd_c = lambda cc: _bd_expand(jnp.transpose(cc, (0, 2, 1)).reshape(nslab, SLAB_GROUPS, P, E))
    bbd, bbdt_re, bbdt_im, cbd_re, cbd_im, cbdt_re, cbdt_im = [], [], [], [], [], [], []
    for d in range(2):
        br_, bi_ = bd_b(ssm_b_re[0, d]).astype(BF16), bd_b(ssm_b_im[0, d]).astype(BF16)
        cr_, ci_ = bd_c(ssm_c_re[0, d]).astype(BF16), bd_c(ssm_c_im[0, d]).astype(BF16)
        bbd.append(jnp.concatenate([br_, bi_], axis=2))
        bbdt_re.append(jnp.transpose(br_, (0, 2, 1)))
        bbdt_im.append(jnp.transpose(bi_, (0, 2, 1)))
        cbd_re.append(cr_)
        cbd_im.append(ci_)
        cbdt_re.append(jnp.transpose(cr_, (0, 2, 1)))
        cbdt_im.append(jnp.transpose(ci_, (0, 2, 1)))
    cos_t, sin_t = _rope_tables(L, Lc)
    qg, kg = q_norm_g, k_norm_g
    small = ['c_ctx', 'b_mod', 'norm_g', 'q_norm_g', 'k_norm_g', 'ssm_a_re', 'ssm_a_im', 'ssm_log_dt', 'ssm_b_re',
             'ssm_b_im', 'ssm_c_re', 'ssm_c_im', 'ssm_d', 'b_glu']
    packs_wmv = [_pack([A[pre + n] for n in small]) for pre in ('', 'm_', 'v_')]
    prepared = packs_wmv + [cos_t, sin_t, coef_im[0], coef_im[1]] + [
        t[d][0] for t in (bbd, bbdt_re, bbdt_im, cbd_re, cbd_im, cbdt_re, cbdt_im) for d in range(2)]
    weights_ready(0, tok + sum(t[0:1, 0:1].astype(F32) for t in prepared))

    def norm_mod(name, xv, g, sh, sc):
        rows = xv.shape[0]
        return _rowk(name, lambda i, xt, gt, sht, sct: [_norm_mod(xt, gt, sel(i, sht), sel(i, sct))],
                     rows, tr, [(xv, 'row'), (g, 'vec'), (sh, 'vec'), (sc, 'vec')], [((rows, D), BF16, 'row')])[0]

    def swiglu_epi(accs, rows, vecs, ri):
        a_, b_ = accs
        return [a_, b_, a_ * _sigmoid(a_) * b_]

    def res_epi(coef):
        def epi(accs, rows, vecs, ri):
            gate = vecs[0]
            if gate.shape[0] == 2:
                gate = jnp.where(ri < Lc, gate[0:1], gate[1:2])
            return [accs[0], rows[0] + (coef * gate) * accs[0]]
        return epi

    def ffn_fwd(tag, h, xres, gate, down_ready=None):
        rows = h.shape[0]
        a_, b_, s_ = _mm(tag + "_up", [(h, Wt['w_' + tag + '_gate'], D), (h, Wt['w_' + tag + '_up'], D)], rows, F,
                         tm=_div(rows, 512), tn=F4, epi=swiglu_epi, outs=[(BF16, False), (BF16, False), (BF16, False)])
        if down_ready is not None:
            down_ready(s_)
        f_, xo = _mm(tag + "_down", [(s_, Wt['w_' + tag + '_down'], F)], rows, D, tm=_div(rows, 768),
                     tn=_div(D, 512), epi=res_epi(0.5), outs=[(F32, False), (F32, False)],
                     rows=[(xres, 0, 0)], vecs=[gate])
        return a_, b_, s_, f_, xo

    xc = jnp.concatenate([ctx[0], x[0]], axis=0)
    h1 = norm_mod("norm1", xc, ng[0:1], sh1, sc1)
    a1, b1, s1, f1, x1 = ffn_fwd("ffn1", h1, xc, g1, down_ready=lambda s_: weights_ready(1, s_))
    weights_ready(2, x1)
    h2 = norm_mod("norm2", x1, ng[1:2], sh2, sc2)
    proj = _mm("in_proj", [(h2, Wt['w_in'], D)], T, 4 * D, tm=_div(T, 768), tn=_div(D, 1024), epi=ident,
               outs=[(F32, False)])[0]
    nh, nkvh = D // HEAD_DIM, KV // HEAD_DIM

    def prep_fn(i, kt, vt, ut, qt, qgt, kgt, ct, st):
        qs = [_head_norm(qt[:, h * HEAD_DIM:(h + 1) * HEAD_DIM], qgt) for h in range(nh)]
        ks = [_head_norm(kt[:, h * HEAD_DIM:(h + 1) * HEAD_DIM], kgt) for h in range(nkvh)]
        qs = [v * ct + _rot(v) * st for v in qs]
        ks = [v * ct + _rot(v) * st for v in ks]
        return [jnp.concatenate(qs, axis=1), jnp.concatenate(ks, axis=1), vt, ut]

    qr, kr, vb, ub = _rowk(
        "qk_prep", prep_fn, T, tr,
        [(proj, ('col', KV, 0)), (proj, ('col', KV, 1)), (proj, ('col', W, 1)), (proj, ('col', D, 1)),
         (qg, 'vec'), (kg, 'vec'), (cos_t, 'row'), (sin_t, 'row')],
        [((T, D), BF16, 'row'), ((T, KV), BF16, 'row'), ((T, KV), BF16, 'row'), ((T, W), BF16, 'row')])
    _, mixer_weights = weights_pass(3, qr)
    attn = _attn_fwd(qr, kr, vb, L, Lc, D)
    hs_re, hs_im, ys = [], [], []
    lam_in = lam_re[0]
    for d in range(2):
        hr_, hi_, y_ = _ssm_fwd("ssm_fwd%d" % d, ub, bbd[d], cbd_re[d], cbd_im[d], lam_in, lam_im[d],
                                coef_re[d], coef_im[d], Lc, reverse=bool(d))
        hs_re.append(hr_)
        hs_im.append(hi_)
        ys.append(y_)
        if d == 0:
            tok_p4, ffn2_weights = weights_pass(4, y_)
            lam_in = lam_re[1] + tok_p4[0:1, 0:1]
    mixer_weights(ys[1])

    def ssm_out_fn(i, y0, y1, ut, dt):
        pre = dt * ut + y0 + y1
        yg_ = _gelu(pre)
        return [pre, yg_, yg_]

    ssm_pre, yg, ygb = _rowk(
        "ssm_out", ssm_out_fn, L, tr,
        [(ys[0], 'orow'), (ys[1], 'orow'), (proj, ('ocol', W, 1)), (ssm_d, 'vec')],
        [((L, W), F32, 'row'), ((L, W), F32, 'row'), ((L, W), BF16, 'row')], nc=ncr)

    def glu_epi(accs, rows, vecs, ri):
        z_ = accs[0] + vecs[0]
        return [z_, rows[0] * _sigmoid(z_)]

    zglu, y2 = _mm("glu", [(ygb, Wt['w_glu'], W)], L, W, tm=_div(L, 512), tn=_div(W, 512), epi=glu_epi,
                   outs=[(F32, False), (BF16, False)], rows=[(yg, 0, 0)], vecs=[b_glu])
    tnm = _div(Dq, 512)

    def merge_epi(accs, rows, vecs, ri):
        ga, gs = _sigmoid(rows[0]), _sigmoid(rows[1])
        return [accs[0], accs[1], ga * accs[0] + gs * accs[1]]

    ba, bs, merged = _mm("merge", [(attn, Wt['w_br_attn'], D), (y2, Wt['w_br_ssm'], W)], L, D, tm=tr, tn=tnm,
                         epi=merge_epi, outs=[(F32, False), (F32, False), (BF16, False)],
                         rows=[(proj, ncr, 2 * D // tnm), (proj, ncr, 3 * D // tnm)])
    mix, x2 = _mm("out_proj", [(merged, Wt['w_out'], D)], L, D, tm=tr, tn=_div(D, 1024), epi=res_epi(1.0),
                  outs=[(F32, False), (F32, False)], rows=[(x1, ncr, 0)], vecs=[g2])
    ffn2_weights(x2)
    h3 = norm_mod("norm3", x2, ng[2:3], sh3, sc3)
    a3, b3, s3, f3, x3 = ffn_fwd("ffn2", h3, x2, g3)

    def loss_fn(i, yt, tt_):
        diff = yt - tt_
        return [diff * (1.0 / D), jnp.sum(diff * diff, axis=0, keepdims=True)]

    dy, sq = _rowk("loss", loss_fn, L, tr, [(x3, 'row'), (loss_target[0], 'row')],
                   [((L, D), F32, 'row'), ((1, D), F32, 'acc')])
    loss = lax.psum(0.5 * jnp.sum(sq) / D, ("x", "y", "c"))

    def res_bwd(name, dxo, f_, gate, coef):
        rows, nrow = dxo.shape[0], gate.shape[0]

        def fn(i, dt, ft, gt):
            return [(coef * sel(i, gt)) * dt, put(i, jnp.sum(dt * ft, axis=0, keepdims=True) * coef, nrow)]

        return _rowk(name, fn, rows, tr, [(dxo, 'row'), (f_, 'row'), (gate, 'vec')],
                     [((rows, D), BF16, 'row'), ((nrow, D), F32, 'acc')])

    def swiglu_bwd_epi(accs, rows, vecs, ri):
        ds_, a_, b_ = accs[0], rows[0].astype(F32), rows[1].astype(F32)
        sg = _sigmoid(a_)
        return [ds_ * b_ * (sg * (1.0 + a_ * (1.0 - sg))), ds_ * (a_ * sg)]

    def norm_mod_bwd(name, xv, g, sh, sc, dh, dres, dres_kind):
        rows, nrow = xv.shape[0], sh.shape[0]

        def fn(i, xt, gt, sht, sct, dht, rest):
            _, vjp = jax.vjp(_norm_mod, xt, gt, sel(i, sht), sel(i, sct))
            dx_, dg_, dsh_, dsc_ = vjp(dht)
            dx_ = dx_ + (jnp.where(i >= ncr, rest, 0.0) if dres_kind == 'xrow' else rest)
            return [dx_, dg_, put(i, dsh_, nrow), put(i, dsc_, nrow)]

        return _rowk(name, fn, rows, tr,
                     [(xv, 'row'), (g, 'vec'), (sh, 'vec'), (sc, 'vec'), (dh, 'row'), (dres, dres_kind)],
                     [((rows, D), F32, 'row'), ((1, D), F32, 'acc'), ((nrow, D), F32, 'acc'), ((nrow, D), F32, 'acc')],
                     nc=ncr)

    def ffn_bwd(tag, dxo, h, a_, b_, s_, f_, gate, wg, wu, wd, on_dwd=None):
        rows = dxo.shape[0]
        df, dgate = res_bwd(tag + "_dres", dxo, f_, gate, 0.5)
        dwd = _mm(tag + "_dwd", [(s_, df, rows)], F, D, tm=_div(F, 512), tn=_div(D, 1024), ta=True, epi=ident,
                  outs=[(BF16, False)])[0].reshape(N_CHIPS, F4, D)
        if on_dwd is not None:
            on_dwd(dwd)
        da, db = _mm(tag + "_dact", [(df, wd, D)], rows, F, tm=_div(rows, 512), tn=F4, tb=True, epi=swiglu_bwd_epi,
                     outs=[(BF16, False), (BF16, False)], rows=[(a_, 0, 0), (b_, 0, 0)])
        dwg = _mm(tag + "_dwg", [(h, da, rows)], D, F, tm=_div(D, 512), tn=F4, ta=True, epi=ident,
                  outs=[(BF16, True)])[0]
        dwu = _mm(tag + "_dwu", [(h, db, rows)], D, F, tm=_div(D, 512), tn=F4, ta=True, epi=ident,
                  outs=[(BF16, True)])[0]
        dh = _mm(tag + "_dh", [(da, wg, F), (db, wu, F)], rows, D, tm=_div(rows, 768), tn=_div(D, 1024), nk=N_CHIPS,
                 tb=True, epi=ident, outs=[(F32, False)], summed=True)[0]
        return dh, dgate, dwg, dwu, dwd

    dh3, dg3, dwg2, dwu2, dwd2 = ffn_bwd("ffn2", dy, h3, a3, b3, s3, f3, g3, Wt['w_ffn2_gate'], Wt['w_ffn2_up'],
                                         Wt['w_ffn2_down'])
    tok_r1, scatter_fin1 = _scatter_split("scatter_ffn2", [dwg2, dwu2, dwd2], dg3)
    dx2, dng3, dsh3, dsc3 = norm_mod_bwd("norm3_bwd", x2, ng[2:3], sh3, sc3, dh3, dy, 'row')
    dmix, dg2 = res_bwd("mix_dres", dx2, mix, g2 + tok_r1[0:1, 0:1], 1.0)

    def dmerge_epi(accs, rows, vecs, ri):
        dm_, ba_, bs_ = accs[0], rows[0], rows[1]
        ga, gs = _sigmoid(rows[2]), _sigmoid(rows[3])
        return [dm_ * ga, dm_ * gs, dm_ * ba_ * ga * (1.0 - ga), dm_ * bs_ * gs * (1.0 - gs)]

    tnd = _div(D, 1024)
    dba, dbs, dga, dgs = _mm("dmerge", [(dmix, Wt['w_out'], D)], L, D, tm=tr, tn=tnd, tb=True, epi=dmerge_epi,
                             outs=[(BF16, False)] * 4,
                             rows=[(ba, 0, 0), (bs, 0, 0), (proj, ncr, 2 * D // tnd), (proj, ncr, 3 * D // tnd)])
    dwout = _mm("dw_out", [(merged, dmix, L)], D, D, tm=_div(D, 512), tn=_div(D, 1024), ta=True, epi=ident,
                outs=[(BF16, False)])[0].reshape(N_CHIPS, Dq, D)
    dattn = _mm("dattn", [(dba, Wt['w_br_attn'], D)], L, D, tm=_div(L, 512), tn=_div(D, 1024), tb=True, epi=ident,
                outs=[(BF16, False)])[0]
    dwba = _mm("dw_br_attn", [(attn, dba, L)], D, D, tm=_div(D, 512), tn=_div(D, 1024), ta=True, epi=ident,
               outs=[(BF16, False)])[0].reshape(N_CHIPS, Dq, D)
    dy2 = _mm("dy2", [(dbs, Wt['w_br_ssm'], D)], L, W, tm=_div(L, 512), tn=_div(W, 1024), nk=N_CHIPS, tb=True,
              epi=ident, outs=[(F32, False)])[0]
    dwbs = _mm("dw_br_ssm", [(y2, dbs, L)], W, D, tm=_div(W, 512), tn=_div(Dq, 512), ta=True, epi=ident,
               outs=[(BF16, True)])[0]

    def glu_bwd_fn(i, d2, ygt, zt):
        sz = _sigmoid(zt)
        dz_ = d2 * ygt * sz * (1.0 - sz)
        return [dz_, d2 * sz, jnp.sum(dz_, axis=0, keepdims=True)]

    dz, dyd, dbglu = _rowk("glu_bwd", glu_bwd_fn, L, tr, [(dy2, 'row'), (yg, 'row'), (zglu, 'row')],
                           [((L, W), BF16, 'row'), ((L, W), F32, 'row'), ((1, W), F32, 'acc')])

    def dssm_epi(accs, rows, vecs, ri):
        _, vjp = jax.vjp(_gelu, rows[1])
        ds_ = vjp(accs[0] + rows[0])[0]
        return [ds_, ds_]

    dssm, dssm_b = _mm("dssm", [(dz, Wt['w_glu'], W)], L, W, tm=_div(L, 512), tn=_div(W, 512), tb=True, epi=dssm_epi,
                       outs=[(F32, False), (BF16, False)], rows=[(dyd, 0, 0), (ssm_pre, 0, 0)])
    dwglu = _mm("dw_glu", [(ygb, dz, L)], W, W, tm=_div(W, 512), tn=_div(W, 1024), ta=True, epi=ident,
                outs=[(BF16, False)])[0].reshape(N_CHIPS, W // N_CHIPS, W)
    tok_r2a, scatter_fin2a = _scatter_split("scatter_mix", [dwglu, dwba, dwbs, dwout], dbglu)
    dssm_full = jnp.concatenate([jnp.zeros((Lc, W), BF16), dssm_b], axis=0)
    dus, dlam_re, dlam_im, dcoef_re, dcoef_im, dbf, dcf_re, dcf_im = [], [], [], [], [], [], [], []
    for d in range(2):
        r = _ssm_bwd("ssm_bwd%d" % d, dssm_full, hs_re[d], hs_im[d], ub, bbd[d], bbdt_re[d], bbdt_im[d],
                     cbdt_re[d], cbdt_im[d], lam_re[d] + tok_r2a[0:1, 0:1], lam_im[d], coef_re[d], coef_im[d], Lc,
                     reverse=bool(d))
        for lst, val in zip((dus, dlam_re, dlam_im, dcoef_re, dcoef_im, dbf, dcf_re, dcf_im), r):
            lst.append(val)
    dqr, dkr, dvf = _attn_bwd(qr, kr, vb, dattn, L, Lc, D)

    def prep_bwd_fn(i, qt, kt, ut, dqt, dkt, dvt, du0, du1, dst, dgat, dgst, dt, qgt, kgt, ct, st):
        live = i >= ncr
        dqt = jnp.where(live, dqt, 0.0)
        dst = jnp.where(live, dst, 0.0)
        dgat = jnp.where(live, dgat, jnp.zeros_like(dgat))
        dgst = jnp.where(live, dgst, jnp.zeros_like(dgst))
        dqs, dks = [], []
        dqg_ = jnp.zeros((1, HEAD_DIM), F32)
        dkg_ = jnp.zeros((1, HEAD_DIM), F32)
        for h in range(nh):
            hl = slice(h * HEAD_DIM, (h + 1) * HEAD_DIM)
            dn = dqt[:, hl] * ct + _rot(dqt[:, hl] * st)
            _, vjp = jax.vjp(_head_norm, qt[:, hl], qgt)
            dxh, dgh = vjp(dn)
            dqs.append(dxh)
            dqg_ = dqg_ + dgh
        for h in range(nkvh):
            hl = slice(h * HEAD_DIM, (h + 1) * HEAD_DIM)
            dn = dkt[:, hl] * ct + _rot(dkt[:, hl] * st)
            _, vjp = jax.vjp(_head_norm, kt[:, hl], kgt)
            dxh, dgh = vjp(dn)
            dks.append(dxh)
            dkg_ = dkg_ + dgh
        du_ = du0 + du1 + dst * dt
        dproj_ = jnp.concatenate([c_.astype(BF16) for c_ in dks + [dvt, du_] + dqs + [dgat, dgst]], axis=1)
        return [dproj_, dqg_, dkg_, jnp.sum(dst * ut, axis=0, keepdims=True)]

    dproj, dqg, dkg, dssd = _rowk(
        "qk_prep_bwd", prep_bwd_fn, T, tr,
        [(proj, ('col', D, 1)), (proj, ('col', KV, 0)), (proj, ('col', W, 1)), (dqr, 'xrow'), (dkr, 'row'),
         (dvf, 'row'), (dus[0], 'row'), (dus[1], 'row'), (dssm, 'xrow'), (dga, 'xrow'), (dgs, 'xrow'), (ssm_d, 'vec'),
         (qg, 'vec'), (kg, 'vec'), (cos_t, 'row'), (sin_t, 'row')],
        [((T, 4 * D), BF16, 'row'), ((1, HEAD_DIM), F32, 'acc'), ((1, HEAD_DIM), F32, 'acc'), ((1, W), F32, 'acc')],
        nc=ncr)
    dh2 = _mm("in_proj_dx", [(dproj, Wt['w_in'], 4 * D)], T, D, tm=_div(T, 768), tn=_div(D, 1024), nk=N_CHIPS, tb=True,
              epi=ident, outs=[(F32, False)])[0]
    dwin = _mm("in_proj_dw", [(h2, dproj, T)], D, 4 * D, tm=_div(D, 512), tn=_div(D, 1024), ta=True, epi=ident,
               outs=[(BF16, True)])[0]
    tok_r2, scatter_fin2 = _scatter_split("scatter_w_in", [dwin], dqg)
    dx1, dng2, dsh2, dsc2 = norm_mod_bwd("norm2_bwd", x1, ng[1:2] + tok_r2[0:1, 0:1], sh2, sc2, dh2, dx2, 'xrow')
    early = {}

    def start_down(dwd):
        early['tok'], early['fin'] = _scatter_split("scatter_ffn1_down", [dwd], dg2)

    dh1, dg1, dwg1, dwu1, dwd1 = ffn_bwd("ffn1", dx1, h1, a1, b1, s1, f1, g1, Wt['w_ffn1_gate'], Wt['w_ffn1_up'],
                                         Wt['w_ffn1_down'], on_dwd=start_down)
    dx0, dng1, dsh1, dsc1 = norm_mod_bwd("norm1_bwd", xc, ng[0:1] + early['tok'][0:1, 0:1], sh1, sc1, dh1, dx1, 'row')
    grad_x = dx0[Lc:][None]

    zD = jnp.zeros((1, D), F32)
    dmod_x = jnp.concatenate([dsh1[1:2], dsc1[1:2], dg1[1:2], dsh2[1:2], dsc2[1:2], dg2, dsh3, dsc3, dg3], axis=1)
    dmod_c = jnp.concatenate([dsh1[0:1], dsc1[0:1], dg1[0:1], dsh2[0:1], dsc2[0:1], zD, zD, zD, zD], axis=1)
    pieces = [dmod_x, dmod_c, dng1, dng2, dng3, dqg, dkg] + dlam_re + dlam_im + dcoef_re + dcoef_im \
        + dbf + dcf_re + dcf_im + [dssd, dbglu]
    shapes = [p_.shape for p_ in pieces]
    pack = _pack(pieces)
    RP = pack.shape[0]
    tok_small, small_gathered = _allgather_split("gather_small", pack, me, dng1)
    tok_r3, scatter_fin3 = _scatter_split("scatter_ffn1_up", [dwg1, dwu1], tok_small)
    results = {}

    def sum_group(tag, names, fin, after_work):
        sent, landed = fin(after_work)
        plane = [_sum_plane("sum_" + n, g_, rb, chip_index) for n, g_, rb in zip(names, sent, landed)]
        tok_, swapped = _swap_split("swap_" + tag, plane, chip_index)
        return tok_, (names, swapped)

    def update_group(group, after_work):
        names, swapped = group
        mine, theirs = swapped(after_work)
        for n, m_, t_ in zip(names, mine, theirs):
            results[n] = _adamw("adamw_" + n, A[n], A['m_' + n], A['v_' + n], [m_, t_])

    tok_a, grp_ffn2 = sum_group("ffn2", big[3:6], scatter_fin1, tok_r3)
    tok_b, grp_mix = sum_group("mix", big[7:11], scatter_fin2a, tok_a)
    tok_c, grp_w_in = sum_group("w_in", big[6:7], scatter_fin2, tok_b)
    update_group(grp_ffn2, tok_c)
    tok_d, grp_down = sum_group("ffn1_down", big[2:3], early['fin'], results['w_ffn2_down'][0])
    update_group(grp_mix, tok_d)
    update_group(grp_w_in, results['w_out'][0])
    update_group(grp_down, results['w_in'][0])
    allp = small_gathered(results['w_ffn1_down'][0])
    head_rows = -(-18 * D // PACK_W)
    head = allp[:, :head_rows].reshape(N_DEV, head_rows * PACK_W)
    dmx_all = head[:, :9 * D]

    def sum_rows_fn(i, t):
        s_ = t[0:1]
        for k in range(1, N_DEV):
            s_ = s_ + t[k:k + 1]
        return [s_]

    dmc_sum = _rowk("sum_dmod_c", sum_rows_fn, 1, 1, [(head[:, 9 * D:18 * D], 'vec')], [((1, 9 * D), F32, 'row')])[0]
    cots = jnp.concatenate([dmx_all, dmc_sum, jnp.zeros((7, 9 * D), F32)], axis=0)
    cots_sh = lax.dynamic_slice(cots, (0, chip * NM), (16, NM))
    part = _mm("cctx_part", [(cots_sh[8:16], wm, NM)], 8, D, tm=8, tn=_div(D, 1024), nk=NM // _div(NM, 1152), tb=True,
               epi=ident, outs=[(F32, False)], a_pro=to_bf, b_pro=to_bf)[0]
    _, cctx_gathered = _allgather_split("gather_cctx", part, me, part)

    def sum_dev_fn(i, t):
        s_ = t[0]
        for k in range(1, N_DEV):
            s_ = s_ + t[k]
        return [s_]

    tot = _rowk("sum_small", sum_dev_fn, RP, 8, [(allp, 'row3')], [((RP, PACK_W), F32, 'row')])[0]
    (t_dmod_x, t_dmod_c, t_ng1, t_ng2, t_ng3, t_qg, t_kg, t_lr0, t_lr1, t_li0, t_li1, t_kr0, t_kr1, t_ki0, t_ki1,
     t_dbf0, t_dbf1, t_dcr0, t_dcr1, t_dci0, t_dci1, t_d, t_bglu) = _unpack(tot, shapes)
    b_grad = lambda t, lo: jnp.transpose(t[:, :, lo:lo + P].reshape(G, E, P), (0, 2, 1))
    c_grad = lambda t: jnp.transpose(t.reshape(nslab, P, SLAB_GROUPS, E), (0, 2, 3, 1)).reshape(G, E, P)
    cat2 = lambda u0, u1: jnp.concatenate([u0.reshape(G, P), u1.reshape(G, P)], axis=0)
    g_are, g_aim, g_ldt = _zoh_bwd(a_re2, a_im2, ldt2, [cat2(t_lr0, t_lr1), cat2(t_li0, t_li1),
                                                         cat2(t_kr0, t_kr1), cat2(t_ki0, t_ki1)])
    g_bmod = _rowk("bmod_grad", lambda i, u0, u1: [u0 + u1], 1, 1, [(t_dmod_x, 'row'), (t_dmod_c, 'row')],
                   [((1, 9 * D), F32, 'row')])[0]
    g_wmod = _outer_sum(acts, cots_sh)
    results['w_mod'] = _adamw("adamw_w_mod", w_mod, m_w_mod, v_w_mod, [g_wmod])
    done = sum(results[n][1].reshape(-1, results[n][1].shape[-1])[0:1, 0:1] for n in list(results)) + g_are[0:1, 0:1] \
        + g_bmod[0:1, 0:1]
    tok_e, grp_up = sum_group("ffn1_up", big[0:2], scatter_fin3, done)
    parts = cctx_gathered(tok_e).reshape(N_CHIPS, 2, 8, D)[:, 0, 0]

    def cctx_fn(i, pt, ct):
        ds_ = ((pt[0:1] + pt[1:2]) + pt[2:3]) + pt[3:4]
        _, vjp = jax.vjp(lambda v: v * _sigmoid(v), ct)
        return [vjp(ds_)[0]]

    g_cctx = _rowk("cctx_grad", cctx_fn, 1, 1, [(parts, 'vec'), (c_ctx[None], 'row')], [((1, D), F32, 'row')])[0]

    ng_full =jnp.concatenate([t_ng1, t_ng2, t_ng3], axis=0)
    gsmall = {
        'c_ctx': g_cctx, 'b_mod': g_bmod, 'norm_g': lax.dynamic_slice(ng_full, (0, chip * Dq), (3, Dq)),
        'q_norm_g': t_qg, 'k_norm_g': t_kg, 'ssm_a_re': g_are, 'ssm_a_im': g_aim, 'ssm_log_dt': g_ldt,
        'ssm_b_re': jnp.stack([b_grad(t_dbf0, 0), b_grad(t_dbf1, 0)]),
        'ssm_b_im': jnp.stack([b_grad(t_dbf0, P), b_grad(t_dbf1, P)]),
        'ssm_c_re': jnp.stack([c_grad(t_dcr0), c_grad(t_dcr1)]), 'ssm_c_im': jnp.stack([c_grad(t_dci0), c_grad(t_dci1)]),
        'ssm_d': t_d, 'b_glu': t_bglu}
    sshapes = [A[n].shape for n in small]
    sres = _adamw("adamw_small", packs_wmv[0], packs_wmv[1], packs_wmv[2], [_pack([gsmall[n] for n in small])])
    update_group(grp_up, sres[0])
    sres = [_unpack(b_, sshapes) for b_ in sres]
    for k, n in enumerate(small):
        results[n] = tuple(sres[q][k] for q in range(4))

    order = ['c_ctx', 'w_mod', 'b_mod', 'norm_g', 'w_ffn1_gate', 'w_ffn1_up', 'w_ffn1_down', 'w_in', 'q_norm_g',
             'k_norm_g', 'ssm_a_re', 'ssm_a_im', 'ssm_log_dt', 'ssm_b_re', 'ssm_b_im', 'ssm_c_re', 'ssm_c_im',
             'ssm_d', 'w_glu', 'b_glu', 'w_br_attn', 'w_br_ssm', 'w_out', 'w_ffn2_gate', 'w_ffn2_up', 'w_ffn2_down']
    outs = [loss, grad_x]
    for q in range(4):
        outs += [results[n][q].reshape(A[n].shape) for n in order]
    return tuple(outs)
```

```python
import math

import jax
import jax.numpy as jnp
from jax import lax
from jax.experimental import pallas as pl
from jax.experimental.pallas import tpu as pltpu

F32 = jnp.float32
BF16 = jnp.bfloat16
MESH = pl.DeviceIdType.MESH

NORM_EPS = 1e-6
ROPE_THETA = 10000.0
GRID_W = 64
HEAD_DIM = 128
Q_PER_KV = 4
SSM_GROUP = 16
SSM_STATE = 64
ADAM_LR = 0.001
ADAM_B1 = 0.9
ADAM_B2 = 0.999
ADAM_EPS = 1e-08
ADAM_WD = 0.01
ADAM_STEP = 10

N_CHIPS = 4
N_DEV = 8
LANES = 128
SLAB_CH = 128
SLAB_GROUPS = SLAB_CH // SSM_GROUP
SLAB_ST = SLAB_GROUPS * SSM_STATE
VMEM_LIMIT_BYTES = 56 * 1024 * 1024
PACK_W = 1024


def _cparams(**kw):
    return pltpu.CompilerParams(vmem_limit_bytes=VMEM_LIMIT_BYTES, **kw)


def _div(n, pref, mult=LANES):
    t = (min(pref, n) // mult) * mult
    while t >= mult:
        if n % t == 0:
            return t
        t -= mult
    return n


def _sigmoid(x):
    return jax.nn.sigmoid(x)


def _gelu(x):
    return x * (0.5 * (1.0 + jnp.tanh(math.sqrt(2.0 / math.pi) * (x + 0.044715 * (x * x * x)))))


def _rowk(name, fn, nrows, tr, ins, outs, nc=0):
    nt = nrows // tr
    in_specs, arrays = [], []
    for arr, kind in ins:
        arrays.append(arr)
        if kind == 'row':
            in_specs.append(pl.BlockSpec((tr, arr.shape[1]), lambda i: (i, 0)))
        elif kind == 'xrow':
            in_specs.append(pl.BlockSpec((tr, arr.shape[1]), lambda i: (jnp.maximum(i - nc, 0), 0)))
        elif kind == 'orow':
            in_specs.append(pl.BlockSpec((tr, arr.shape[1]), lambda i: (i + nc, 0)))
        elif kind == 'vec':
            in_specs.append(pl.BlockSpec(arr.shape, lambda i, nd=arr.ndim: (0,) * nd))
        elif kind == 'row3':
            in_specs.append(pl.BlockSpec((arr.shape[0], tr, arr.shape[2]), lambda i: (0, i, 0)))
        elif kind == 'row1':
            in_specs.append(pl.BlockSpec((None, tr, arr.shape[2]), lambda i: (0, i, 0)))
        elif kind[0] == 'ocol':
            _, width, blk = kind
            in_specs.append(pl.BlockSpec((tr, width), lambda i, blk=blk: (i + nc, blk)))
        else:
            _, width, blk = kind
            in_specs.append(pl.BlockSpec((tr, width), lambda i, blk=blk: (i, blk)))
    out_shape, out_specs = [], []
    for shape, dtype, kind in outs:
        out_shape.append(jax.ShapeDtypeStruct(shape, dtype))
        if kind == 'row':
            out_specs.append(pl.BlockSpec((tr, shape[1]), lambda i: (i, 0)))
        elif kind == 'row1':
            out_specs.append(pl.BlockSpec((None, tr, shape[2]), lambda i: (0, i, 0)))
        else:
            out_specs.append(pl.BlockSpec(shape, lambda i, nd=len(shape): (0,) * nd))
    nin = len(ins)

    def body(*refs):
        i = pl.program_id(0)
        res = fn(i, *[r[...] for r in refs[:nin]])
        for (shape, dtype, kind), ref, val in zip(outs, refs[nin:], res):
            if kind in ('row', 'row1'):
                ref[...] = val.astype(dtype)
            else:
                @pl.when(i == 0)
                def _():
                    ref[...] = val.astype(dtype)

                @pl.when(i > 0)
                def _():
                    ref[...] += val.astype(dtype)

    return pl.pallas_call(body, name=name, grid=(nt,), in_specs=in_specs, out_specs=out_specs,
                          out_shape=out_shape, compiler_params=_cparams())(*arrays)


def _mm(name, pairs, M, N, *, tm, tn, nk=1, epi, outs, ta=False, tb=False, rows=(), vecs=(),
        a_pro=None, b_pro=None, n_outer=True, summed=False):
    nm, nn = M // tm, N // tn
    npair = len(pairs)

    def idx(f):
        if n_outer:
            return lambda j, i, k: f(i, j, k)
        return lambda i, j, k: f(i, j, k)

    in_specs, args = [], []
    for a, b, K in pairs:
        tk = K // nk
        if ta:
            in_specs.append(pl.BlockSpec((tk, tm), idx(lambda i, j, k: (k, i))))
        else:
            in_specs.append(pl.BlockSpec((tm, tk), idx(lambda i, j, k: (i, k))))
        args.append(a)
        if b.ndim == 3:
            if tb:
                per = b.shape[2] // tk
                in_specs.append(pl.BlockSpec((None, tn, tk), idx(lambda i, j, k, per=per: (k // per, j, k % per))))
            else:
                per = b.shape[2] // tn
                in_specs.append(pl.BlockSpec((None, tk, tn), idx(lambda i, j, k, per=per: (j // per, k, j % per))))
        elif tb:
            in_specs.append(pl.BlockSpec((tn, tk), idx(lambda i, j, k: (j, k))))
        else:
            in_specs.append(pl.BlockSpec((tk, tn), idx(lambda i, j, k: (k, j))))
        args.append(b)
    for arr, ro, co in rows:
        in_specs.append(pl.BlockSpec((tm, tn), idx(lambda i, j, k, ro=ro, co=co: (i + ro, j + co))))
        args.append(arr)
    for arr in vecs:
        in_specs.append(pl.BlockSpec((arr.shape[0], tn), idx(lambda i, j, k: (0, j))))
        args.append(arr)
    out_shape, out_specs = [], []
    for dtype, chunked in outs:
        if chunked:
            per = (N // N_CHIPS) // tn
            out_shape.append(jax.ShapeDtypeStruct((N_CHIPS, M, N // N_CHIPS), dtype))
            out_specs.append(pl.BlockSpec((None, tm, tn), idx(lambda i, j, k, per=per: (j // per, i, j % per))))
        else:
            out_shape.append(jax.ShapeDtypeStruct((M, N), dtype))
            out_specs.append(pl.BlockSpec((tm, tn), idx(lambda i, j, k: (i, j))))
    nacc = 1 if summed else npair
    scratch = [pltpu.VMEM((tm, tn), F32) for _ in range(nacc)] if nk > 1 else []
    nrow, nvec, nout = len(rows), len(vecs), len(outs)
    dims = (((0 if ta else 1,), (1 if tb else 0,)), ((), ()))

    def body(*refs):
        ab = refs[:2 * npair]
        row_refs = refs[2 * npair:2 * npair + nrow]
        vec_refs = refs[2 * npair + nrow:2 * npair + nrow + nvec]
        out_refs = refs[2 * npair + nrow + nvec:2 * npair + nrow + nvec + nout]
        acc_refs = refs[2 * npair + nrow + nvec + nout:]
        if n_outer:
            j, i, k = pl.program_id(0), pl.program_id(1), pl.program_id(2)
        else:
            i, j, k = pl.program_id(0), pl.program_id(1), pl.program_id(2)

        def part(p):
            av, bv = ab[2 * p][...], ab[2 * p + 1][...]
            if a_pro is not None:
                av = a_pro(av)
            if b_pro is not None:
                bv = b_pro(bv)
            return lax.dot_general(av, bv, dims, preferred_element_type=F32)

        def finish(accs):
            row_index = i * tm + lax.broadcasted_iota(jnp.int32, (tm, 1), 0)
            res = epi(accs, [r[...] for r in row_refs], [v[...] for v in vec_refs], row_index)
            for ref, val in zip(out_refs, res):
                ref[...] = val.astype(ref.dtype)

        parts = [part(p) for p in range(npair)]
        if summed:
            total = parts[0]
            for extra in parts[1:]:
                total = total + extra
            parts = [total]
        if nk == 1:
            finish(parts)
        else:
            @pl.when(k == 0)
            def _():
                for q in range(nacc):
                    acc_refs[q][...] = parts[q]

            @pl.when(jnp.logical_and(k > 0, k < nk - 1))
            def _():
                for q in range(nacc):
                    acc_refs[q][...] += parts[q]

            @pl.when(k == nk - 1)
            def _():
                finish([acc_refs[q][...] + parts[q] for q in range(nacc)])

    grid = (nn, nm, nk) if n_outer else (nm, nn, nk)
    return pl.pallas_call(body, name=name, grid=grid, in_specs=in_specs, out_specs=out_specs,
                          out_shape=out_shape, scratch_shapes=scratch, compiler_params=_cparams())(*args)


def _split3(v):
    v0 = v.astype(BF16)
    r1 = v - v0.astype(F32)
    v1 = r1.astype(BF16)
    v2 = (r1 - v1.astype(F32)).astype(BF16)
    return v0, v1, v2


def _mesh_pos():
    return lax.axis_index("x"), lax.axis_index("y"), lax.axis_index("c")


def _allgather_small(name, x):
    m, n = x.shape

    def body(x_ref, out_ref, send_sems, recv_sems, local_sem):
        xi, yi, ci = _mesh_pos()
        me, sibling = (xi, yi, ci), (xi, yi, 1 - ci)
        chips = [(1 - xi, yi), (xi, 1 - yi), (1 - xi, 1 - yi)]

        def rows(px, py, pc):
            return out_ref.at[pl.ds((4 * px + 2 * py + pc) * m, m), :]

        def copy(k, block, to, src=None):
            return pltpu.make_async_remote_copy(
                src_ref=rows(*block) if src is None else src, dst_ref=rows(*block),
                send_sem=send_sems.at[k], recv_sem=recv_sems.at[k], device_id=to, device_id_type=MESH)

        mine = pltpu.make_async_copy(x_ref, rows(*me), local_sem)
        mine.start()
        first = [copy(0, me, sibling, src=x_ref)]
        first += [copy(1 + j, me, (*chip, ci), src=x_ref) for j, chip in enumerate(chips)]
        for cp in first:
            cp.start()
        passed = [copy(4 + j, (*chip, ci), sibling) for j, chip in enumerate(chips)]
        for j, chip in enumerate(chips):
            copy(1 + j, (*chip, ci), me).wait_recv()
            passed[j].start()
        copy(0, sibling, me).wait_recv()
        for j, chip in enumerate(chips):
            copy(4 + j, (*chip, 1 - ci), me).wait_recv()
        for cp in first + passed:
            cp.wait_send()
        mine.wait()

    return pl.pallas_call(
        body, name=name, out_shape=jax.ShapeDtypeStruct((N_DEV * m, n), x.dtype),
        in_specs=[pl.BlockSpec(memory_space=pltpu.VMEM)], out_specs=pl.BlockSpec(memory_space=pltpu.VMEM),
        scratch_shapes=[pltpu.SemaphoreType.DMA((7,)), pltpu.SemaphoreType.DMA((7,)), pltpu.SemaphoreType.DMA],
        compiler_params=_cparams())(x)


_HBM = pl.BlockSpec(memory_space=pltpu.HBM)
_SEM = pl.BlockSpec(memory_space=pltpu.SEMAPHORE)
_ANY = pl.BlockSpec(memory_space=pl.ANY)
_EFFECT = pltpu.SideEffectType.DATAFLOW_SIDE_EFFECTING


def _in_hbm(v):
    return pltpu.with_memory_space_constraint(v, pltpu.HBM)


def _other_chips(xi, yi):
    return [(1 - xi, yi), (xi, 1 - yi), (1 - xi, 1 - yi)]


def _guarded(core, fn):
    if core is None:
        fn()
    else:
        pl.when(lax.axis_index("c") == core)(fn)


def _split_copies(name, srcs, lands, after, pairs, senders, receivers, ncopy):
    ns, nl = len(srcs), len(lands)
    dma = pltpu.SemaphoreType.DMA((ncopy,))
    thru = [pltpu.HBM(v.shape, v.dtype) for v in list(srcs) + list(lands)]

    def start_body(*refs):
        src_refs, land_refs = refs[:ns], refs[ns:ns + nl]
        descs = pairs(src_refs, land_refs, refs[ns + nl + 1], refs[ns + nl + 2])

        def go():
            for send, _ in descs:
                send.start()

        _guarded(senders, go)
        refs[-1][...] = jnp.zeros_like(refs[-1])

    res = pl.pallas_call(
        start_body, name=name + "_start",
        out_shape=(dma, dma, *thru, jax.ShapeDtypeStruct((8, LANES), F32)),
        in_specs=[_HBM] * (ns + nl) + [_ANY],
        out_specs=(_SEM, _SEM, *([_HBM] * (ns + nl)), pl.BlockSpec(memory_space=pltpu.VMEM)),
        input_output_aliases={k: 2 + k for k in range(ns + nl)},
        compiler_params=_cparams(has_side_effects=_EFFECT),
    )(*[_in_hbm(v) for v in srcs], *[_in_hbm(v) for v in lands], after)
    send_sems, recv_sems, token = res[0], res[1], res[-1]
    carried = res[2:2 + ns + nl]

    def finish(after_work):
        def wait_body(*refs):
            src_refs, land_refs = refs[:ns], refs[ns:ns + nl]
            descs = pairs(src_refs, land_refs, refs[ns + nl], refs[ns + nl + 1])

            def sent():
                for send, _ in descs:
                    send.wait_send()

            def landed():
                for _, recv in descs:
                    recv.wait_recv()

            _guarded(senders, sent)
            _guarded(receivers, landed)

        out = pl.pallas_call(
            wait_body, name=name + "_wait", out_shape=tuple(thru),
            in_specs=[_HBM] * (ns + nl) + [_SEM, _SEM, _ANY], out_specs=tuple([_HBM] * (ns + nl)),
            input_output_aliases={k: k for k in range(ns + nl)},
            compiler_params=_cparams(has_side_effects=_EFFECT),
        )(*carried, send_sems, recv_sems, after_work)
        return list(out[:ns]), list(out[ns:])

    return token, finish


def _cast_slot(name, w, chip_index, after):
    R, C = w.shape[1:]
    tr = _div(R, max(16, 524288 // C), mult=16)

    def body(chip_ref, w_ref, after_ref, o_ref):
        o_ref[...] = w_ref[...].astype(BF16)

    return pl.pallas_call(
        body, name=name, out_shape=jax.ShapeDtypeStruct((N_CHIPS, R, C), BF16),
        grid_spec=pltpu.PrefetchScalarGridSpec(
            num_scalar_prefetch=1, grid=(R // tr,),
            in_specs=[pl.BlockSpec((None, tr, C), lambda i, chip_ref: (0, i, 0)), _ANY],
            out_specs=pl.BlockSpec((None, tr, C), lambda i, chip_ref: (chip_ref[0], i, 0))),
        compiler_params=_cparams())(chip_index, w, after)


def _sum_plane(name, grads, landed, chip_index):
    R, C = grads.shape[1:]
    tr = _div(R, max(16, 1048576 // C), mult=16)

    def body(chip_ref, own_ref, land_ref, o_ref):
        o_ref[...] = ((own_ref[...].astype(F32) + land_ref[0].astype(F32)) + land_ref[1].astype(F32)) \
            + land_ref[2].astype(F32)

    return pl.pallas_call(
        body, name=name, out_shape=jax.ShapeDtypeStruct((R, C), F32),
        grid_spec=pltpu.PrefetchScalarGridSpec(
            num_scalar_prefetch=1, grid=(R // tr,),
            in_specs=[pl.BlockSpec((None, tr, C), lambda i, chip_ref: (chip_ref[0], i, 0)),
                      pl.BlockSpec((3, tr, C), lambda i, chip_ref: (0, i, 0))],
            out_specs=pl.BlockSpec((tr, C), lambda i, chip_ref: (i, 0))),
        compiler_params=_cparams())(chip_index, grads, landed)


def _gather_split(name, lands, after):
    def pairs(src_refs, land_refs, send_sems, recv_sems):
        xi, yi, _ = _mesh_pos()
        mine = 2 * xi + yi
        out = []
        for a in range(len(lands)):
            for j, (px, py) in enumerate(_other_chips(xi, yi)):
                def to_slot(slot, a=a, j=j, px=px, py=py):
                    return pltpu.make_async_remote_copy(
                        src_ref=land_refs[a].at[mine], dst_ref=land_refs[a].at[slot], send_sem=send_sems.at[3 * a + j],
                        recv_sem=recv_sems.at[3 * a + j], device_id=(px, py, 1), device_id_type=MESH)
                out.append((to_slot(mine), to_slot(2 * px + py)))
        return out

    return _split_copies(name, [], lands, after, pairs, senders=1, receivers=1, ncopy=3 * len(lands))


def _allgather_split(name, block, me, after):
    land = lax.dynamic_update_slice(lax.empty((N_DEV,) + block.shape, block.dtype), block[None], (me, 0, 0))

    def pairs(src_refs, land_refs, send_sems, recv_sems):
        xi, yi, ci = _mesh_pos()
        mine = 4 * xi + 2 * yi + ci
        out = []
        for k in range(1, N_DEV):
            kx, ky, kc = (k >> 2) & 1, (k >> 1) & 1, k & 1
            px = 1 - xi if kx else xi
            py = 1 - yi if ky else yi
            pc = 1 - ci if kc else ci

            def to_slot(slot, k=k, px=px, py=py, pc=pc):
                return pltpu.make_async_remote_copy(
                    src_ref=land_refs[0].at[mine], dst_ref=land_refs[0].at[slot], send_sem=send_sems.at[k - 1],
                    recv_sem=recv_sems.at[k - 1], device_id=(px, py, pc), device_id_type=MESH)
            out.append((to_slot(mine), to_slot(4 * px + 2 * py + pc)))
        return out

    tok, fin = _split_copies(name, [], [land], after, pairs, senders=None, receivers=None, ncopy=N_DEV - 1)
    return tok, lambda later: fin(later)[1][0]


def _swap_split(name, arrs, after):
    lands = [lax.empty(v.shape, v.dtype) for v in arrs]

    def pairs(src_refs, land_refs, send_sems, recv_sems):
        xi, yi, ci = _mesh_pos()
        out = []
        for a in range(len(arrs)):
            cp = pltpu.make_async_remote_copy(
                src_ref=src_refs[a], dst_ref=land_refs[a], send_sem=send_sems.at[a], recv_sem=recv_sems.at[a],
                device_id=(xi, yi, 1 - ci), device_id_type=MESH)
            out.append((cp, cp))
        return out

    return _split_copies(name, arrs, lands, after, pairs, senders=None, receivers=None, ncopy=len(arrs))


def _pass_split(name, lands, after):
    def pairs(src_refs, land_refs, send_sems, recv_sems):
        xi, yi, _ = _mesh_pos()
        out = []
        for a in range(len(lands)):
            for j, (px, py) in enumerate(_other_chips(xi, yi)):
                cp = pltpu.make_async_remote_copy(
                    src_ref=land_refs[a].at[2 * px + py], dst_ref=land_refs[a].at[2 * px + py],
                    send_sem=send_sems.at[3 * a + j], recv_sem=recv_sems.at[3 * a + j],
                    device_id=(xi, yi, 0), device_id_type=MESH)
                out.append((cp, cp))
        return out

    return _split_copies(name, [], lands, after, pairs, senders=1, receivers=0, ncopy=3 * len(lands))


def _scatter_split(name, grads, after):
    lands = [lax.empty((3,) + g.shape[1:], g.dtype) for g in grads]

    def pairs(src_refs, land_refs, send_sems, recv_sems):
        xi, yi, ci = _mesh_pos()
        out = []
        for a in range(len(grads)):
            for j, (px, py) in enumerate(_other_chips(xi, yi)):
                cp = pltpu.make_async_remote_copy(
                    src_ref=src_refs[a].at[2 * px + py], dst_ref=land_refs[a].at[j], send_sem=send_sems.at[3 * a + j],
                    recv_sem=recv_sems.at[3 * a + j], device_id=(px, py, ci), device_id_type=MESH)
                out.append((cp, cp))
        return out

    return _split_copies(name, grads, lands, after, pairs, senders=None, receivers=None, ncopy=3 * len(grads))


def _gather_finish(name, lands):
    na = len(lands)

    def body(*refs):
        outs = refs[na:2 * na]
        send_sems, recv_sems = refs[2 * na:]
        xi, yi, ci = _mesh_pos()
        passes = [pltpu.make_async_remote_copy(
            src_ref=outs[a].at[2 * px + py], dst_ref=outs[a].at[2 * px + py],
            send_sem=send_sems.at[a, j], recv_sem=recv_sems.at[a, j], device_id=(xi, yi, 0), device_id_type=MESH)
            for a in range(na) for j, (px, py) in enumerate(_other_chips(xi, yi))]

        @pl.when(ci == 1)
        def _():
            for cp in passes:
                cp.start()
            for cp in passes:
                cp.wait_send()

        @pl.when(ci == 0)
        def _():
            for cp in passes:
                cp.wait_recv()

    return pl.pallas_call(
        body, name=name, out_shape=[jax.ShapeDtypeStruct(v.shape, v.dtype) for v in lands],
        in_specs=[_ANY] * na, out_specs=[_ANY] * na,
        input_output_aliases={a: a for a in range(na)},
        scratch_shapes=[pltpu.SemaphoreType.DMA((na, 3)), pltpu.SemaphoreType.DMA((na, 3))],
        compiler_params=_cparams())(*lands)


ATTN_HEADS_PER_STEP = 2


def _attn_tiles(L, Lc, D, tq_pref=256):
    tq = min(tq_pref, Lc)
    return tq, L // tq, Lc // tq, D // HEAD_DIM // Q_PER_KV


def _attn_scores(q, k):
    return lax.dot_general(q, k, (((1,), (1,)), ((), ())), preferred_element_type=F32) * (HEAD_DIM ** -0.5)


def _softmax_rows(s):
    e = jnp.exp(s - jnp.max(s, axis=-1, keepdims=True))
    return e * (1.0 / jnp.sum(e, axis=-1, keepdims=True))


def _attn_probs(q, k):
    return _softmax_rows(_attn_scores(q, k))


def _attn_fwd(qr, kr, v, L, Lc, D):
    T = L + Lc
    tq, nq, qoff, nkv = _attn_tiles(L, Lc, D)
    hp = Q_PER_KV
    ng = Q_PER_KV // hp

    def body(q_ref, k_ref, v_ref, o_ref):
        k, vv = k_ref[...], v_ref[...]
        heads = [slice(r * HEAD_DIM, (r + 1) * HEAD_DIM) for r in range(hp)]
        scores = [_attn_scores(q_ref[:, cols], k) for cols in heads]
        probs = [_softmax_rows(s) for s in scores]
        for cols, p in zip(heads, probs):
            o_ref[:, cols] = jnp.dot(p.astype(BF16), vv, preferred_element_type=F32).astype(o_ref.dtype)

    kv_spec = pl.BlockSpec((T, HEAD_DIM), lambda h, r, q: (0, h))
    return pl.pallas_call(
        body, name="attn_fwd", grid=(nkv, ng, nq),
        in_specs=[pl.BlockSpec((tq, hp * HEAD_DIM), lambda h, r, q: (q + qoff, h * ng + r)), kv_spec, kv_spec],
        out_specs=pl.BlockSpec((tq, hp * HEAD_DIM), lambda h, r, q: (q, h * ng + r)),
        out_shape=jax.ShapeDtypeStruct((L, D), BF16), compiler_params=_cparams())(qr, kr, v)


def _attn_bwd(qr, kr, v, do, L, Lc, D):
    T = L + Lc
    tq, nq, qoff, nkv = _attn_tiles(L, Lc, D, 256)
    scale = HEAD_DIM ** -0.5
    hp = Q_PER_KV
    ng = Q_PER_KV // hp

    def body(q_ref, k_ref, v_ref, do_ref, dq_ref, dk_ref, dv_ref):
        first = jnp.logical_and(pl.program_id(1) == 0, pl.program_id(2) == 0)
        k, vv = k_ref[...], v_ref[...]
        nt_dims, tn_dims = (((1,), (1,)), ((), ())), (((0,), (0,)), ((), ()))
        heads = [slice(r * HEAD_DIM, (r + 1) * HEAD_DIM) for r in range(hp)]
        qs = [q_ref[:, cols] for cols in heads]
        douts = [do_ref[:, cols] for cols in heads]
        scores = [_attn_scores(q, k) for q in qs]
        dps = [lax.dot_general(dout, vv, nt_dims, preferred_element_type=F32) for dout in douts]
        probs = [_softmax_rows(s) for s in scores]
        dss = [(p * (dp - jnp.sum(p * dp, axis=-1, keepdims=True)) * scale).astype(BF16) for p, dp in zip(probs, dps)]
        for cols, ds in zip(heads, dss):
            dq_ref[:, cols] = jnp.dot(ds, k, preferred_element_type=F32)
        dk = dv = None
        for q, dout, p, ds in zip(qs, douts, probs, dss):
            dk_r = lax.dot_general(ds, q, tn_dims, preferred_element_type=F32)
            dv_r = lax.dot_general(p.astype(BF16), dout, tn_dims, preferred_element_type=F32)
            dk = dk_r if dk is None else dk + dk_r
            dv = dv_r if dv is None else dv + dv_r

        @pl.when(first)
        def _():
            dk_ref[...] = dk
            dv_ref[...] = dv

        @pl.when(jnp.logical_not(first))
        def _():
            dk_ref[...] += dk
            dv_ref[...] += dv

    kv_spec = pl.BlockSpec((T, HEAD_DIM), lambda h, r, q: (0, h))
    q_spec = pl.BlockSpec((tq, hp * HEAD_DIM), lambda h, r, q: (q + qoff, h * ng + r))
    o_spec = pl.BlockSpec((tq, hp * HEAD_DIM), lambda h, r, q: (q, h * ng + r))
    return pl.pallas_call(
        body, name="attn_bwd", grid=(nkv, ng, nq),
        in_specs=[q_spec, kv_spec, kv_spec, o_spec], out_specs=[o_spec, kv_spec, kv_spec],
        out_shape=[jax.ShapeDtypeStruct((L, D), F32), jax.ShapeDtypeStruct((T, D // Q_PER_KV), F32),
                   jax.ShapeDtypeStruct((T, D // Q_PER_KV), F32)],
        compiler_params=_cparams())(qr, kr, v, do)


SUB = 8


def _doubling(xr, xi, pw_re, pw_im, lanes, first_power, period, reverse):
    n = xr.shape[0]
    rows = lax.broadcasted_iota(jnp.int32, (n, 1), 0) & (period - 1)
    for k in range(period.bit_length() - 1):
        d = 1 << k
        keep = rows < period - d if reverse else rows >= d
        sr = jnp.where(keep, pltpu.roll(xr, n - d if reverse else d, 0), 0.0)
        si = jnp.where(keep, pltpu.roll(xi, n - d if reverse else d, 0), 0.0)
        pr, pi = pw_re[first_power + k:first_power + k + 1, lanes], pw_im[first_power + k:first_power + k + 1, lanes]
        xr, xi = xr + (pr * sr - pi * si), xi + (pr * si + pi * sr)
    return xr, xi


def _scan_tile(xr, xi, tb, lanes, reverse):
    pw_re, pw_im, w8_re, w8_im, wb_re, wb_im, carry_re, carry_im, sr, si = tb
    tt = xr.shape[0]
    nb = tt // SUB
    nq = sr.shape[0]
    cols = [slice(q * LANES, (q + 1) * LANES) for q in range(nq)]
    for q in range(nq):
        sr[q] = xr[:, cols[q]]
        si[q] = xi[:, cols[q]]
    order = list(range(SUB - 2, -1, -1)) if reverse else list(range(1, SUB))
    ends_r, ends_i = [], []
    for q in range(nq):
        ql = slice(lanes.start + q * LANES, lanes.start + (q + 1) * LANES)
        lr, li = pw_re[0:1, ql], pw_im[0:1, ql]
        first_row = pl.ds(SUB - 1 if reverse else 0, nb, stride=SUB)
        pr, pi = sr[q, first_row, :], si[q, first_row, :]
        for r in order:
            rows = pl.ds(r, nb, stride=SUB)
            pr, pi = sr[q, rows, :] + (lr * pr - li * pi), si[q, rows, :] + (lr * pi + li * pr)
            sr[q, rows, :] = pr
            si[q, rows, :] = pi
        ends_r.append(pr)
        ends_i.append(pi)
    er, ei = jnp.concatenate(ends_r, axis=1), jnp.concatenate(ends_i, axis=1)
    er, ei = _doubling(er, ei, pw_re, pw_im, lanes, 3, nb, reverse)
    car, cai = carry_re[:, lanes], carry_im[:, lanes]
    wbr, wbi = wb_re[:, lanes], wb_im[:, lanes]
    er = er + (wbr * car - wbi * cai)
    ei = ei + (wbr * cai + wbi * car)
    out_block = 0 if reverse else nb - 1
    carry_re[:, lanes] = er[out_block:out_block + 1, :]
    carry_im[:, lanes] = ei[out_block:out_block + 1, :]
    blocks = lax.broadcasted_iota(jnp.int32, (nb, 1), 0)
    first = blocks == (nb - 1 if reverse else 0)
    cr = jnp.where(first, car, pltpu.roll(er, nb - 1 if reverse else 1, 0))
    ci = jnp.where(first, cai, pltpu.roll(ei, nb - 1 if reverse else 1, 0))
    for r in range(SUB):
        wr, wi = w8_re[r:r + 1, lanes], w8_im[r:r + 1, lanes]
        add_r, add_i = wr * cr - wi * ci, wr * ci + wi * cr
        for q in range(nq):
            sr[q, pl.ds(r, nb, stride=SUB), :] += add_r[:, cols[q]]
            si[q, pl.ds(r, nb, stride=SUB), :] += add_i[:, cols[q]]
    hr = jnp.concatenate([sr[q] for q in range(nq)], axis=1)
    hi = jnp.concatenate([si[q] for q in range(nq)], axis=1)
    return hr, hi, car, cai


def _scan_scratch(tt, NS):
    nb = tt // SUB
    return [pltpu.VMEM((8, NS), F32), pltpu.VMEM((8, NS), F32), pltpu.VMEM((SUB, NS), F32), pltpu.VMEM((SUB, NS), F32),
            pltpu.VMEM((nb, NS), F32), pltpu.VMEM((nb, NS), F32), pltpu.VMEM((1, NS), F32), pltpu.VMEM((1, NS), F32),
            pltpu.VMEM((SLAB_ST // LANES, tt, LANES), F32), pltpu.VMEM((SLAB_ST // LANES, tt, LANES), F32)]


def _scan_init(lr, li, tb, reverse):
    pw_re, pw_im, w8_re, w8_im, wb_re, wb_im, carry_re, carry_im, sr, _ = tb
    nb = wb_re.shape[0]
    carry_re[...] = jnp.zeros_like(carry_re)
    carry_im[...] = jnp.zeros_like(carry_im)
    pr, pi = lr, li
    for k in range(3 + nb.bit_length() - 1):
        pw_re[k:k + 1, :] = pr
        pw_im[k:k + 1, :] = pi
        if k == 3:
            l8r, l8i = pr, pi
        pr, pi = pr * pr - pi * pi, 2.0 * pr * pi
    pr, pi = lr, li
    for r in range(SUB):
        row = SUB - 1 - r if reverse else r
        w8_re[row:row + 1, :] = pr
        w8_im[row:row + 1, :] = pi
        pr, pi = pr * lr - pi * li, pr * li + pi * lr
    pr, pi = l8r, l8i
    for b in range(nb):
        row = nb - 1 - b if reverse else b
        wb_re[row:row + 1, :] = pr
        wb_im[row:row + 1, :] = pi
        pr, pi = pr * l8r - pi * l8i, pr * l8i + pi * l8r


def _ssm_tiles(T, Lc):
    tt = min(128, Lc)
    return tt, T // tt, Lc // tt


def _ssm_fwd(name, u, bbd, cbd_re, cbd_im, lam_re, lam_im, coef_re, coef_im, Lc, reverse):
    T, W = u.shape
    nslab = W // SLAB_CH
    NS = nslab * SLAB_ST
    tt, nt, nc = _ssm_tiles(T, Lc)
    if reverse:
        tile = lambda s: jnp.where(s < nc, nc - 1 - s, nt - 1 - (s - nc))
    else:
        tile = lambda s: s

    def body(u_ref, b_ref, cr_ref, ci_ref, lr_ref, li_ref, kr_ref, ki_ref, hr_ref, hi_ref, y_ref, *tb):
        @pl.when(pl.program_id(0) == 0)
        def _():
            _scan_init(lr_ref[...], li_ref[...], tb, reverse)

        for j in range(nslab):
            lanes = slice(j * SLAB_ST, (j + 1) * SLAB_ST)
            bu = jnp.dot(u_ref[:, j * SLAB_CH:(j + 1) * SLAB_CH], b_ref[j], preferred_element_type=F32)
            br, bi = bu[:, :SLAB_ST], bu[:, SLAB_ST:]
            kr, ki = kr_ref[:, lanes], ki_ref[:, lanes]
            hr, hi, _, _ = _scan_tile(kr * br - ki * bi, kr * bi + ki * br, tb, lanes, reverse)
            hrb, hib = hr.astype(BF16), hi.astype(BF16)
            hr_ref[:, lanes] = hrb
            hi_ref[:, lanes] = hib
            y_ref[:, j * SLAB_CH:(j + 1) * SLAB_CH] = (
                jnp.dot(hrb, cr_ref[j], preferred_element_type=F32)
                - jnp.dot(hib, ci_ref[j], preferred_element_type=F32))

    whole3 = lambda arr: pl.BlockSpec(arr.shape, lambda s: (0, 0, 0))
    vec = pl.BlockSpec((1, NS), lambda s: (0, 0))
    return pl.pallas_call(
        body, name=name, grid=(nt,),
        in_specs=[pl.BlockSpec((tt, W), lambda s: (tile(s), 0)), whole3(bbd), whole3(cbd_re), whole3(cbd_im),
                  vec, vec, vec, vec],
        out_specs=[pl.BlockSpec((tt, NS), lambda s: (tile(s), 0)), pl.BlockSpec((tt, NS), lambda s: (tile(s), 0)),
                   pl.BlockSpec((tt, W), lambda s: (tile(s), 0))],
        out_shape=[jax.ShapeDtypeStruct((T, NS), BF16), jax.ShapeDtypeStruct((T, NS), BF16),
                   jax.ShapeDtypeStruct((T, W), F32)],
        scratch_shapes=_scan_scratch(tt, NS),
        compiler_params=_cparams())(u, bbd, cbd_re, cbd_im, lam_re, lam_im, coef_re, coef_im)


def _ssm_bwd(name, dy, h_re, h_im, u, bbd, bbdt_re, bbdt_im, cbdt_re, cbdt_im, lam_re, lam_im,
             coef_re, coef_im, Lc, reverse):
    T, W = u.shape
    nslab = W // SLAB_CH
    NS = nslab * SLAB_ST
    tt, nt, nc = _ssm_tiles(T, Lc)
    adj_reverse = not reverse
    if reverse:
        tile = lambda s: jnp.where(s < nt - nc, nc + s, s - (nt - nc))
    else:
        tile = lambda s: nt - 1 - s

    def body(dy_ref, hr_ref, hi_ref, u_ref, b_ref, btr_ref, bti_ref, ctr_ref, cti_ref, lr_ref, li_ref,
             kr_ref, ki_ref, du_ref, dlr_ref, dli_ref, dkr_ref, dki_ref, dbf_ref, dcrf_ref, dcif_ref,
             db_ref, dcr_ref, dci_ref, *tb):
        @pl.when(pl.program_id(0) == 0)
        def _():
            _scan_init(lr_ref[...], -li_ref[...], tb, adj_reverse)
            for ref in (dlr_ref, dli_ref, dkr_ref, dki_ref, db_ref, dcr_ref, dci_ref):
                ref[...] = jnp.zeros_like(ref)

        rows = lax.broadcasted_iota(jnp.int32, (tt, 1), 0)
        far_row = tt - 1 if adj_reverse else 0
        tn_dims = (((0,), (0,)), ((), ()))
        for j in range(nslab):
            lanes = slice(j * SLAB_ST, (j + 1) * SLAB_ST)
            chans = slice(j * SLAB_CH, (j + 1) * SLAB_CH)
            dys, us = dy_ref[:, chans], u_ref[:, chans]
            er = jnp.dot(dys, ctr_ref[j], preferred_element_type=F32)
            ei = -jnp.dot(dys, cti_ref[j], preferred_element_type=F32)
            ar, ai, car, cai = _scan_tile(er, ei, tb, lanes, adj_reverse)
            shift = tt - 1 if adj_reverse else 1
            nr = jnp.where(rows == far_row, car, pltpu.roll(ar, shift, 0))
            ni = jnp.where(rows == far_row, cai, pltpu.roll(ai, shift, 0))
            hrb, hib = hr_ref[:, lanes], hi_ref[:, lanes]
            hr, hi = hrb.astype(F32), hib.astype(F32)
            dlr_ref[:, lanes] += jnp.sum(nr * hr + ni * hi, axis=0, keepdims=True)
            dli_ref[:, lanes] += jnp.sum(ni * hr - nr * hi, axis=0, keepdims=True)
            bu = jnp.dot(us, b_ref[j], preferred_element_type=F32)
            br, bi = bu[:, :SLAB_ST], bu[:, SLAB_ST:]
            dkr_ref[:, lanes] += jnp.sum(ar * br + ai * bi, axis=0, keepdims=True)
            dki_ref[:, lanes] += jnp.sum(ai * br - ar * bi, axis=0, keepdims=True)
            kr, ki = kr_ref[:, lanes], ki_ref[:, lanes]
            dbr = (ar * kr + ai * ki).astype(BF16)
            dbi = (ai * kr - ar * ki).astype(BF16)
            du_ref[:, chans] = (jnp.dot(dbr, btr_ref[j], preferred_element_type=F32)
                                + jnp.dot(dbi, bti_ref[j], preferred_element_type=F32))
            db_ref[j, :, :SLAB_ST] += lax.dot_general(us, dbr, tn_dims, preferred_element_type=F32)
            db_ref[j, :, SLAB_ST:] += lax.dot_general(us, dbi, tn_dims, preferred_element_type=F32)
            dcr_ref[j] += lax.dot_general(hrb, dys, tn_dims, preferred_element_type=F32)
            dci_ref[j] -= lax.dot_general(hib, dys, tn_dims, preferred_element_type=F32)

        @pl.when(pl.program_id(0) == nt - 1)
        def _():
            def iota(shape, axis):
                return lax.broadcasted_iota(jnp.int32, shape, axis)

            sg, ss = SSM_GROUP.bit_length() - 1, SSM_STATE.bit_length() - 1
            b_mask = (iota((SLAB_CH, SLAB_ST), 0) >> sg) == (iota((SLAB_CH, SLAB_ST), 1) >> ss)
            c_mask = (iota((SLAB_ST, SLAB_CH), 0) >> ss) == (iota((SLAB_ST, SLAB_CH), 1) >> sg)
            fold = jnp.where((iota((SLAB_ST, SSM_STATE), 0) & (SSM_STATE - 1)) == iota((SLAB_ST, SSM_STATE), 1),
                             1.0, 0.0).astype(BF16)
            fold_t = jnp.where((iota((SSM_STATE, SLAB_ST), 1) & (SSM_STATE - 1)) == iota((SSM_STATE, SLAB_ST), 0),
                               1.0, 0.0).astype(BF16)

            def exact_dot(a, b, a_is_value):
                terms = _split3(a if a_is_value else b)
                acc = None
                for t in terms:
                    part = jnp.dot(t, b, preferred_element_type=F32) if a_is_value else jnp.dot(a, t, preferred_element_type=F32)
                    acc = part if acc is None else acc + part
                return acc

            for j in range(nslab):
                dbj = db_ref[j]
                dbf_ref[j, :, :SSM_STATE] = exact_dot(jnp.where(b_mask, dbj[:, :SLAB_ST], 0.0), fold, True)
                dbf_ref[j, :, SSM_STATE:] = exact_dot(jnp.where(b_mask, dbj[:, SLAB_ST:], 0.0), fold, True)
                dcrf_ref[j] = exact_dot(fold_t, jnp.where(c_mask, dcr_ref[j], 0.0), False)
                dcif_ref[j] = exact_dot(fold_t, jnp.where(c_mask, dci_ref[j], 0.0), False)

    whole3 = lambda arr: pl.BlockSpec(arr.shape, lambda s: (0, 0, 0))
    vec = pl.BlockSpec((1, NS), lambda s: (0, 0))
    row_w = pl.BlockSpec((tt, W), lambda s: (tile(s), 0))
    row_s = pl.BlockSpec((tt, NS), lambda s: (tile(s), 0))
    dbf = jax.ShapeDtypeStruct((nslab, SLAB_CH, 2 * SSM_STATE), F32)
    dcf = jax.ShapeDtypeStruct((nslab, SSM_STATE, SLAB_CH), F32)
    return pl.pallas_call(
        body, name=name, grid=(nt,),
        in_specs=[row_w, row_s, row_s, row_w, whole3(bbd), whole3(bbdt_re), whole3(bbdt_im), whole3(cbdt_re),
                  whole3(cbdt_im), vec, vec, vec, vec],
        out_specs=[row_w, vec, vec, vec, vec, whole3(dbf), whole3(dcf), whole3(dcf)],
        out_shape=[jax.ShapeDtypeStruct((T, W), F32)] + [jax.ShapeDtypeStruct((1, NS), F32)] * 4 + [dbf, dcf, dcf],
        scratch_shapes=[pltpu.VMEM(bbd.shape, F32), pltpu.VMEM(bbdt_re.shape, F32), pltpu.VMEM(bbdt_re.shape, F32)]
        + _scan_scratch(tt, NS),
        compiler_params=_cparams())(dy, h_re, h_im, u, bbd, bbdt_re, bbdt_im, cbdt_re, cbdt_im,
                                    lam_re, lam_im, coef_re, coef_im)


def _zoh_math(a_re, a_im, log_dt):
    dt = jnp.exp(log_dt)
    mag = jnp.exp(a_re * dt)
    lb_re = mag * jnp.cos(a_im * dt)
    lb_im = mag * jnp.sin(a_im * dt)
    den = a_re * a_re + a_im * a_im
    coef_re = ((lb_re - 1.0) * a_re + lb_im * a_im) / den
    coef_im = (lb_im * a_re - (lb_re - 1.0) * a_im) / den
    return lb_re, lb_im, coef_re, coef_im


def _zoh_fwd(a_re, a_im, log_dt):
    def body(ar, ai, ld, o0, o1, o2, o3):
        for ref, val in zip((o0, o1, o2, o3), _zoh_math(ar[...], ai[...], ld[...])):
            ref[...] = val

    return pl.pallas_call(body, name="zoh_fwd", out_shape=[jax.ShapeDtypeStruct(a_re.shape, F32)] * 4,
                          compiler_params=_cparams())(a_re, a_im, log_dt)


def _zoh_bwd(a_re, a_im, log_dt, cots):
    def body(ar, ai, ld, c0, c1, c2, c3, o0, o1, o2):
        _, vjp = jax.vjp(_zoh_math, ar[...], ai[...], ld[...])
        for ref, val in zip((o0, o1, o2), vjp((c0[...], c1[...], c2[...], c3[...]))):
            ref[...] = val

    return pl.pallas_call(
        body, name="zoh_bwd",
        out_shape=[jax.ShapeDtypeStruct(a_re.shape, F32), jax.ShapeDtypeStruct(a_re.shape, F32),
                   jax.ShapeDtypeStruct(log_dt.shape, F32)],
        compiler_params=_cparams())(a_re, a_im, log_dt, *cots)


def _outer_sum(acts, cots):
    D, N = acts.shape[1], cots.shape[1]
    tm, tn = _div(D, 512), _div(N, 1152)
    dims = (((0,), (0,)), ((), ()))

    def body(a_ref, b_ref, o_ref):
        a = a_ref[...]
        aa = _split3(a * _sigmoid(a))
        bb = _split3(b_ref[...])
        acc = None
        for ia in range(3):
            for ib in range(3 - ia):
                t = lax.dot_general(aa[ia], bb[ib], dims, preferred_element_type=F32)
                acc = t if acc is None else acc + t
        o_ref[...] = acc

    return pl.pallas_call(
        body, name="mod_dw", grid=(D // tm, N // tn),
        in_specs=[pl.BlockSpec((16, tm), lambda i, j: (0, i)), pl.BlockSpec((16, tn), lambda i, j: (0, j))],
        out_specs=pl.BlockSpec((tm, tn), lambda i, j: (i, j)),
        out_shape=jax.ShapeDtypeStruct((D, N), F32), compiler_params=_cparams())(acts, cots)


def _adamw_math(w, g, m, v):
    m = ADAM_B1 * m + (1.0 - ADAM_B1) * g
    v = ADAM_B2 * v + (1.0 - ADAM_B2) * (g * g)
    m_hat = m / (1.0 - ADAM_B1 ** ADAM_STEP)
    v_hat = v / (1.0 - ADAM_B2 ** ADAM_STEP)
    delta = -ADAM_LR * (m_hat / (jnp.sqrt(v_hat) + ADAM_EPS) + ADAM_WD * w)
    return delta, m, v


def _adamw(name, w, m, v, gparts):
    R, C = w.shape[-2:]
    kind = 'row1' if w.ndim == 3 else 'row'
    tr = _div(R, max(8, 524288 // C), mult=8)

    def fn(i, wv, mv, vv, *gs):
        g = gs[0]
        for extra in gs[1:]:
            g = g + extra
        return (g,) + _adamw_math(wv, g, mv, vv)

    return _rowk(name, fn, R, tr, [(w, kind), (m, kind), (v, kind)] + [(g, 'row') for g in gparts],
                 [(w.shape, F32, kind)] * 4)


def _pack(pieces, rows_mult=8):
    flat = jnp.concatenate([p.reshape(-1).astype(F32) for p in pieces])
    unit = rows_mult * PACK_W
    total = -(-flat.shape[0] // unit) * unit
    return jnp.pad(flat, (0, total - flat.shape[0])).reshape(total // PACK_W, PACK_W)


def _unpack(buf, shapes):
    flat = buf.reshape(-1)
    out, off = [], 0
    for s in shapes:
        n = math.prod(s)
        out.append(flat[off:off + n].reshape(s))
        off += n
    return out


def _bd_expand(t):
    S, g, a, b = t.shape
    eye = jnp.eye(g, dtype=t.dtype)
    return (t[:, :, :, None, :] * eye[None, :, None, :, None]).reshape(S, g * a, g * b)


def _rope_tables(L, Lc):
    rows = L // GRID_W
    row_ids = jnp.broadcast_to(jnp.arange(rows)[:, None], (rows, GRID_W)).reshape(-1).astype(F32)
    col_ids = jnp.broadcast_to(jnp.arange(GRID_W)[None, :], (rows, GRID_W)).reshape(-1).astype(F32)
    quarter = HEAD_DIM // 4
    inv_freq = ROPE_THETA ** (-jnp.arange(quarter, dtype=F32) / quarter)
    ang_r = row_ids[:, None] * inv_freq
    ang_c = col_ids[:, None] * inv_freq
    cos = jnp.concatenate([jnp.cos(ang_r), jnp.cos(ang_r), jnp.cos(ang_c), jnp.cos(ang_c)], axis=1)
    sin = jnp.concatenate([-jnp.sin(ang_r), jnp.sin(ang_r), -jnp.sin(ang_c), jnp.sin(ang_c)], axis=1)
    cos = jnp.concatenate([jnp.ones((Lc, HEAD_DIM), F32), cos], axis=0)
    sin = jnp.concatenate([jnp.zeros((Lc, HEAD_DIM), F32), sin], axis=0)
    return cos, sin


def _rot(v):
    lane = lax.broadcasted_iota(jnp.int32, (1, HEAD_DIM), 1)
    first = (lane % (HEAD_DIM // 2)) < (HEAD_DIM // 4)
    return jnp.where(first, pltpu.roll(v, HEAD_DIM - HEAD_DIM // 4, 1), pltpu.roll(v, HEAD_DIM // 4, 1))


def _head_norm(xh, g):
    return xh * lax.rsqrt(jnp.mean(xh * xh, axis=-1, keepdims=True) + NORM_EPS) * g


def _norm_mod(xv, g, sh, sc):
    r = lax.rsqrt(jnp.mean(xv * xv, axis=-1, keepdims=True) + NORM_EPS)
    return (xv * r) * g * (1.0 + sc) + sh


def kernel(x, c, ctx, c_ctx, w_mod, b_mod, norm_g, w_ffn1_gate, w_ffn1_up, w_ffn1_down, w_in, q_norm_g, k_norm_g, ssm_a_re, ssm_a_im, ssm_log_dt, ssm_b_re, ssm_b_im, ssm_c_re, ssm_c_im, ssm_d, w_glu, b_glu, w_br_attn, w_br_ssm, w_out, w_ffn2_gate, w_ffn2_up, w_ffn2_down, loss_target, m_c_ctx, m_w_mod, m_b_mod, m_norm_g, m_w_ffn1_gate, m_w_ffn1_up, m_w_ffn1_down, m_w_in, m_q_norm_g, m_k_norm_g, m_ssm_a_re, m_ssm_a_im, m_ssm_log_dt, m_ssm_b_re, m_ssm_b_im, m_ssm_c_re, m_ssm_c_im, m_ssm_d, m_w_glu, m_b_glu, m_w_br_attn, m_w_br_ssm, m_w_out, m_w_ffn2_gate, m_w_ffn2_up, m_w_ffn2_down, v_c_ctx, v_w_mod, v_b_mod, v_norm_g, v_w_ffn1_gate, v_w_ffn1_up, v_w_ffn1_down, v_w_in, v_q_norm_g, v_k_norm_g, v_ssm_a_re, v_ssm_a_im, v_ssm_log_dt, v_ssm_b_re, v_ssm_b_im, v_ssm_c_re, v_ssm_c_im, v_ssm_d, v_w_glu, v_b_glu, v_w_br_attn, v_w_br_ssm, v_w_out, v_w_ffn2_gate, v_w_ffn2_up, v_w_ffn2_down):
    A = dict(locals())
    xi, yi, ci = _mesh_pos()
    chip = 2 * xi + yi
    me = 4 * xi + 2 * yi + ci
    L, D = x.shape[1], x.shape[2]
    Lc = ctx.shape[1]
    T = L + Lc
    F4 = w_ffn1_gate.shape[2]
    F = N_CHIPS * F4
    W, KV, Dq = D // 2, D // 4, D // 4
    G = W // SSM_GROUP
    P, E = SSM_STATE, SSM_GROUP
    NS = G * P
    nslab = W // SLAB_CH
    tr = min(256, Lc)
    ncr = Lc // tr
    assert L % tr == 0 and Lc % tr == 0 and W % SLAB_CH == 0 and D % (4 * LANES) == 0

    def sel(i, v):
        return v if v.shape[0] == 1 else jnp.where(i < ncr, v[0:1], v[1:2])

    def put(i, v, nrow):
        if nrow == 1:
            return v
        which = (i >= ncr).astype(jnp.int32)
        r2 = lax.broadcasted_iota(jnp.int32, (nrow, 1), 0)
        return jnp.where(r2 == which, jnp.broadcast_to(v, (nrow, v.shape[1])), 0.0)

    ident = lambda accs, rows, vecs, ri: [accs[0]]

    NM = w_mod.shape[2]
    first = jnp.zeros((8, D), F32).at[0].set(c[0]).at[1:4, :Dq].set(norm_g[0])
    g0 = _allgather_small("gather_c", first).reshape(N_CHIPS, 2, 8, D)
    c_all = g0[:, :, 0].reshape(N_DEV, D)
    ng = jnp.transpose(g0[:, 0, 1:4, :Dq], (1, 0, 2)).reshape(3, D)
    acts = jnp.concatenate([c_all, c_ctx[None], jnp.zeros((7, D), F32)], axis=0)
    wm = w_mod[0]
    b_shard = lax.dynamic_slice(b_mod[0], (chip * NM,), (NM,))[None]
    silu_bf = lambda a: (a * _sigmoid(a)).astype(BF16)
    to_bf = lambda b: b.astype(BF16)
    mod_part = _mm("mod_fwd", [(acts, wm, D)], 16, NM, tm=16, tn=_div(NM, 1152),
                   epi=lambda accs, rows, vecs, ri: [accs[0] + vecs[0]], outs=[(F32, False)],
                   vecs=[b_shard], a_pro=silu_bf, b_pro=to_bf)[0]
    mg = _allgather_small("gather_mod", mod_part).reshape(N_CHIPS, 2, 16, NM)[:, 0]
    mod_all = jnp.transpose(mg, (1, 0, 2)).reshape(16, N_CHIPS * NM)
    mod_x = lax.dynamic_slice(mod_all, (me, 0), (1, 9 * D))
    mod_c = jnp.where(jnp.arange(9 * D)[None] < 5 * D, mod_all[8:9], 0.0)
    modv = jnp.concatenate([mod_c, mod_x], axis=0)
    mv = lambda k: modv[:, k * D:(k + 1) * D]
    sh1, sc1, g1, sh2, sc2 = mv(0), mv(1), mv(2), mv(3), mv(4)
    g2, sh3, sc3, g3 = mv(5)[1:2], mv(6)[1:2], mv(7)[1:2], mv(8)[1:2]

    big = ['w_ffn1_gate', 'w_ffn1_up', 'w_ffn1_down', 'w_ffn2_gate', 'w_ffn2_up', 'w_ffn2_down',
           'w_in', 'w_glu', 'w_br_attn', 'w_br_ssm', 'w_out']
    row_sharded = {'w_ffn1_down', 'w_ffn2_down', 'w_glu', 'w_br_attn', 'w_out'}
    groups = [big[0:2], big[2:3], big[6:7], big[7:11], big[3:6]]
    chip_index = jnp.reshape(chip, (1,)).astype(jnp.int32)
    tok, gather_finish = modv, []
    pin = c
    for gi, names in enumerate(groups):
        tok, fin = _gather_split("gather_w%d" % gi, [_cast_slot("cast_" + n, A[n], chip_index, pin) for n in names], tok)
        gather_finish.append(fin)
        pin = tok
    ng = ng + tok[0:1, 0:1]
    Wt = {}

    def register(names, full):
        for n, gw in zip(names, full):
            Wt[n] = gw.reshape(N_CHIPS * gw.shape[1], gw.shape[2]) if n in row_sharded else gw

    def weights_ready(gi, after_work):
        _, lands = gather_finish[gi](after_work)
        register(groups[gi], _gather_finish("gather_w%d_pass" % gi, lands))

    def weights_pass(gi, after_work):
        _, lands = gather_finish[gi](after_work)
        tok_, fin_ = _pass_split("gather_w%d_pass" % gi, lands, after_work)
        return tok_, lambda later: register(groups[gi], fin_(later)[1])

    a_re2, a_im2 = ssm_a_re[0].reshape(2 * G, P), ssm_a_im[0].reshape(2 * G, P)
    ldt2 = ssm_log_dt[0].reshape(2 * G, 1)
    zoh = _zoh_fwd(a_re2, a_im2, ldt2)
    lam_re, lam_im, coef_re, coef_im = [[z[d * G:(d + 1) * G].reshape(1, NS) for d in range(2)] for z in zoh]
    bd_b = lambda b: _bd_expand(jnp.transpose(b, (0, 2, 1)).reshape(nslab, SLAB_GROUPS, E, P))
    bd_c = lambda cc: _bd_expand(jnp.transpose(cc, (0, 2, 1)).reshape(nslab, SLAB_GROUPS, P, E))
    bbd, bbdt_re, bbdt_im, cbd_re, cbd_im, cbdt_re, cbdt_im = [], [], [], [], [], [], []
    for d in range(2):
        br_, bi_ = bd_b(ssm_b_re[0, d]).astype(BF16), bd_b(ssm_b_im[0, d]).astype(BF16)
        cr_, ci_ = bd_c(ssm_c_re[0, d]).astype(BF16), bd_c(ssm_c_im[0, d]).astype(BF16)
        bbd.append(jnp.concatenate([br_, bi_], axis=2))
        bbdt_re.append(jnp.transpose(br_, (0, 2, 1)))
        bbdt_im.append(jnp.transpose(bi_, (0, 2, 1)))
        cbd_re.append(cr_)
        cbd_im.append(ci_)
        cbdt_re.append(jnp.transpose(cr_, (0, 2, 1)))
        cbdt_im.append(jnp.transpose(ci_, (0, 2, 1)))
    cos_t, sin_t = _rope_tables(L, Lc)
    qg, kg = q_norm_g, k_norm_g
    small = ['c_ctx', 'b_mod', 'norm_g', 'q_norm_g', 'k_norm_g', 'ssm_a_re', 'ssm_a_im', 'ssm_log_dt', 'ssm_b_re',
             'ssm_b_im', 'ssm_c_re', 'ssm_c_im', 'ssm_d', 'b_glu']
    packs_wmv = [_pack([A[pre + n] for n in small]) for pre in ('', 'm_', 'v_')]
    prepared = packs_wmv + [cos_t, sin_t, coef_im[0], coef_im[1]] + [
        t[d][0] for t in (bbd, bbdt_re, bbdt_im, cbd_re, cbd_im, cbdt_re, cbdt_im) for d in range(2)]
    weights_ready(0, tok + sum(t[0:1, 0:1].astype(F32) for t in prepared))

    def norm_mod(name, xv, g, sh, sc):
        rows = xv.shape[0]
        return _rowk(name, lambda i, xt, gt, sht, sct: [_norm_mod(xt, gt, sel(i, sht), sel(i, sct))],
                     rows, tr, [(xv, 'row'), (g, 'vec'), (sh, 'vec'), (sc, 'vec')], [((rows, D), BF16, 'row')])[0]

    def swiglu_epi(accs, rows, vecs, ri):
        a_, b_ = accs
        return [a_, b_, a_ * _sigmoid(a_) * b_]

    def res_epi(coef):
        def epi(accs, rows, vecs, ri):
            gate = vecs[0]
            if gate.shape[0] == 2:
                gate = jnp.where(ri < Lc, gate[0:1], gate[1:2])
            return [accs[0], rows[0] + (coef * gate) * accs[0]]
        return epi

    def ffn_fwd(tag, h, xres, gate, down_ready=None):
        rows = h.shape[0]
        a_, b_, s_ = _mm(tag + "_up", [(h, Wt['w_' + tag + '_gate'], D), (h, Wt['w_' + tag + '_up'], D)], rows, F,
                         tm=_div(rows, 512), tn=F4, epi=swiglu_epi, outs=[(BF16, False), (BF16, False), (BF16, False)])
        if down_ready is not None:
            down_ready(s_)
        f_, xo = _mm(tag + "_down", [(s_, Wt['w_' + tag + '_down'], F)], rows, D, tm=_div(rows, 768),
                     tn=_div(D, 512), epi=res_epi(0.5), outs=[(F32, False), (F32, False)],
                     rows=[(xres, 0, 0)], vecs=[gate])
        return a_, b_, s_, f_, xo

    xc = jnp.concatenate([ctx[0], x[0]], axis=0)
    h1 = norm_mod("norm1", xc, ng[0:1], sh1, sc1)
    a1, b1, s1, f1, x1 = ffn_fwd("ffn1", h1, xc, g1, down_ready=lambda s_: weights_ready(1, s_))
    weights_ready(2, x1)
    h2 = norm_mod("norm2", x1, ng[1:2], sh2, sc2)
    proj = _mm("in_proj", [(h2, Wt['w_in'], D)], T, 4 * D, tm=_div(T, 768), tn=_div(D, 1024), epi=ident,
               outs=[(F32, False)])[0]
    nh, nkvh = D // HEAD_DIM, KV // HEAD_DIM

    def prep_fn(i, kt, vt, ut, qt, qgt, kgt, ct, st):
        qs = [_head_norm(qt[:, h * HEAD_DIM:(h + 1) * HEAD_DIM], qgt) for h in range(nh)]
        ks = [_head_norm(kt[:, h * HEAD_DIM:(h + 1) * HEAD_DIM], kgt) for h in range(nkvh)]
        qs = [v * ct + _rot(v) * st for v in qs]
        ks = [v * ct + _rot(v) * st for v in ks]
        return [jnp.concatenate(qs, axis=1), jnp.concatenate(ks, axis=1), vt, ut]

    qr, kr, vb, ub = _rowk(
        "qk_prep", prep_fn, T, tr,
        [(proj, ('col', KV, 0)), (proj, ('col', KV, 1)), (proj, ('col', W, 1)), (proj, ('col', D, 1)),
         (qg, 'vec'), (kg, 'vec'), (cos_t, 'row'), (sin_t, 'row')],
        [((T, D), BF16, 'row'), ((T, KV), BF16, 'row'), ((T, KV), BF16, 'row'), ((T, W), BF16, 'row')])
    _, mixer_weights = weights_pass(3, qr)
    attn = _attn_fwd(qr, kr, vb, L, Lc, D)
    hs_re, hs_im, ys = [], [], []
    lam_in = lam_re[0]
    for d in range(2):
        hr_, hi_, y_ = _ssm_fwd("ssm_fwd%d" % d, ub, bbd[d], cbd_re[d], cbd_im[d], lam_in, lam_im[d],
                                coef_re[d], coef_im[d], Lc, reverse=bool(d))
        hs_re.append(hr_)
        hs_im.append(hi_)
        ys.append(y_)
        if d == 0:
            tok_p4, ffn2_weights = weights_pass(4, y_)
            lam_in = lam_re[1] + tok_p4[0:1, 0:1]
    mixer_weights(ys[1])

    def ssm_out_fn(i, y0, y1, ut, dt):
        pre = dt * ut + y0 + y1
        yg_ = _gelu(pre)
        return [pre, yg_, yg_]

    ssm_pre, yg, ygb = _rowk(
        "ssm_out", ssm_out_fn, L, tr,
        [(ys[0], 'orow'), (ys[1], 'orow'), (proj, ('ocol', W, 1)), (ssm_d, 'vec')],
        [((L, W), F32, 'row'), ((L, W), F32, 'row'), ((L, W), BF16, 'row')], nc=ncr)

    def glu_epi(accs, rows, vecs, ri):
        z_ = accs[0] + vecs[0]
        return [z_, rows[0] * _sigmoid(z_)]

    zglu, y2 = _mm("glu", [(ygb, Wt['w_glu'], W)], L, W, tm=_div(L, 512), tn=_div(W, 512), epi=glu_epi,
                   outs=[(F32, False), (BF16, False)], rows=[(yg, 0, 0)], vecs=[b_glu])
    tnm = _div(Dq, 512)

    def merge_epi(accs, rows, vecs, ri):
        ga, gs = _sigmoid(rows[0]), _sigmoid(rows[1])
        return [accs[0], accs[1], ga * accs[0] + gs * accs[1]]

    ba, bs, merged = _mm("merge", [(attn, Wt['w_br_attn'], D), (y2, Wt['w_br_ssm'], W)], L, D, tm=tr, tn=tnm,
                         epi=merge_epi, outs=[(F32, False), (F32, False), (BF16, False)],
                         rows=[(proj, ncr, 2 * D // tnm), (proj, ncr, 3 * D // tnm)])
    mix, x2 = _mm("out_proj", [(merged, Wt['w_out'], D)], L, D, tm=tr, tn=_div(D, 1024), epi=res_epi(1.0),
                  outs=[(F32, False), (F32, False)], rows=[(x1, ncr, 0)], vecs=[g2])
    ffn2_weights(x2)
    h3 = norm_mod("norm3", x2, ng[2:3], sh3, sc3)
    a3, b3, s3, f3, x3 = ffn_fwd("ffn2", h3, x2, g3)

    def loss_fn(i, yt, tt_):
        diff = yt - tt_
        return [diff * (1.0 / D), jnp.sum(diff * diff, axis=0, keepdims=True)]

    dy, sq = _rowk("loss", loss_fn, L, tr, [(x3, 'row'), (loss_target[0], 'row')],
                   [((L, D), F32, 'row'), ((1, D), F32, 'acc')])
    loss = lax.psum(0.5 * jnp.sum(sq) / D, ("x", "y", "c"))

    def res_bwd(name, dxo, f_, gate, coef):
        rows, nrow = dxo.shape[0], gate.shape[0]

        def fn(i, dt, ft, gt):
            return [(coef * sel(i, gt)) * dt, put(i, jnp.sum(dt * ft, axis=0, keepdims=True) * coef, nrow)]

        return _rowk(name, fn, rows, tr, [(dxo, 'row'), (f_, 'row'), (gate, 'vec')],
                     [((rows, D), BF16, 'row'), ((nrow, D), F32, 'acc')])

    def swiglu_bwd_epi(accs, rows, vecs, ri):
        ds_, a_, b_ = accs[0], rows[0].astype(F32), rows[1].astype(F32)
        sg = _sigmoid(a_)
        return [ds_ * b_ * (sg * (1.0 + a_ * (1.0 - sg))), ds_ * (a_ * sg)]

    def norm_mod_bwd(name, xv, g, sh, sc, dh, dres, dres_kind):
        rows, nrow = xv.shape[0], sh.shape[0]

        def fn(i, xt, gt, sht, sct, dht, rest):
            _, vjp = jax.vjp(_norm_mod, xt, gt, sel(i, sht), sel(i, sct))
            dx_, dg_, dsh_, dsc_ = vjp(dht)
            dx_ = dx_ + (jnp.where(i >= ncr, rest, 0.0) if dres_kind == 'xrow' else rest)
            return [dx_, dg_, put(i, dsh_, nrow), put(i, dsc_, nrow)]

        return _rowk(name, fn, rows, tr,
                     [(xv, 'row'), (g, 'vec'), (sh, 'vec'), (sc, 'vec'), (dh, 'row'), (dres, dres_kind)],
                     [((rows, D), F32, 'row'), ((1, D), F32, 'acc'), ((nrow, D), F32, 'acc'), ((nrow, D), F32, 'acc')],
                     nc=ncr)

    def ffn_bwd(tag, dxo, h, a_, b_, s_, f_, gate, wg, wu, wd, on_dwd=None):
        rows = dxo.shape[0]
        df, dgate = res_bwd(tag + "_dres", dxo, f_, gate, 0.5)
        dwd = _mm(tag + "_dwd", [(s_, df, rows)], F, D, tm=_div(F, 512), tn=_div(D, 1024), ta=True, epi=ident,
                  outs=[(BF16, False)])[0].reshape(N_CHIPS, F4, D)
        if on_dwd is not None:
            on_dwd(dwd)
        da, db = _mm(tag + "_dact", [(df, wd, D)], rows, F, tm=_div(rows, 512), tn=F4, tb=True, epi=swiglu_bwd_epi,
                     outs=[(BF16, False), (BF16, False)], rows=[(a_, 0, 0), (b_, 0, 0)])
        dwg = _mm(tag + "_dwg", [(h, da, rows)], D, F, tm=_div(D, 512), tn=F4, ta=True, epi=ident,
                  outs=[(BF16, True)])[0]
        dwu = _mm(tag + "_dwu", [(h, db, rows)], D, F, tm=_div(D, 512), tn=F4, ta=True, epi=ident,
                  outs=[(BF16, True)])[0]
        dh = _mm(tag + "_dh", [(da, wg, F), (db, wu, F)], rows, D, tm=_div(rows, 768), tn=_div(D, 1024), nk=N_CHIPS,
                 tb=True, epi=ident, outs=[(F32, False)], summed=True)[0]
        return dh, dgate, dwg, dwu, dwd

    dh3, dg3, dwg2, dwu2, dwd2 = ffn_bwd("ffn2", dy, h3, a3, b3, s3, f3, g3, Wt['w_ffn2_gate'], Wt['w_ffn2_up'],
                                         Wt['w_ffn2_down'])
    tok_r1, scatter_fin1 = _scatter_split("scatter_ffn2", [dwg2, dwu2, dwd2], dg3)
    dx2, dng3, dsh3, dsc3 = norm_mod_bwd("norm3_bwd", x2, ng[2:3], sh3, sc3, dh3, dy, 'row')
    dmix, dg2 = res_bwd("mix_dres", dx2, mix, g2 + tok_r1[0:1, 0:1], 1.0)

    def dmerge_epi(accs, rows, vecs, ri):
        dm_, ba_, bs_ = accs[0], rows[0], rows[1]
        ga, gs = _sigmoid(rows[2]), _sigmoid(rows[3])
        return [dm_ * ga, dm_ * gs, dm_ * ba_ * ga * (1.0 - ga), dm_ * bs_ * gs * (1.0 - gs)]

    tnd = _div(D, 1024)
    dba, dbs, dga, dgs = _mm("dmerge", [(dmix, Wt['w_out'], D)], L, D, tm=tr, tn=tnd, tb=True, epi=dmerge_epi,
                             outs=[(BF16, False)] * 4,
                             rows=[(ba, 0, 0), (bs, 0, 0), (proj, ncr, 2 * D // tnd), (proj, ncr, 3 * D // tnd)])
    dwout = _mm("dw_out", [(merged, dmix, L)], D, D, tm=_div(D, 512), tn=_div(D, 1024), ta=True, epi=ident,
                outs=[(BF16, False)])[0].reshape(N_CHIPS, Dq, D)
    dattn = _mm("dattn", [(dba, Wt['w_br_attn'], D)], L, D, tm=_div(L, 512), tn=_div(D, 1024), tb=True, epi=ident,
                outs=[(BF16, False)])[0]
    dwba = _mm("dw_br_attn", [(attn, dba, L)], D, D, tm=_div(D, 512), tn=_div(D, 1024), ta=True, epi=ident,
               outs=[(BF16, False)])[0].reshape(N_CHIPS, Dq, D)
    dy2 = _mm("dy2", [(dbs, Wt['w_br_ssm'], D)], L, W, tm=_div(L, 512), tn=_div(W, 1024), nk=N_CHIPS, tb=True,
              epi=ident, outs=[(F32, False)])[0]
    dwbs = _mm("dw_br_ssm", [(y2, dbs, L)], W, D, tm=_div(W, 512), tn=_div(Dq, 512), ta=True, epi=ident,
               outs=[(BF16, True)])[0]

    def glu_bwd_fn(i, d2, ygt, zt):
        sz = _sigmoid(zt)
        dz_ = d2 * ygt * sz * (1.0 - sz)
        return [dz_, d2 * sz, jnp.sum(dz_, axis=0, keepdims=True)]

    dz, dyd, dbglu = _rowk("glu_bwd", glu_bwd_fn, L, tr, [(dy2, 'row'), (yg, 'row'), (zglu, 'row')],
                           [((L, W), BF16, 'row'), ((L, W), F32, 'row'), ((1, W), F32, 'acc')])

    def dssm_epi(accs, rows, vecs, ri):
        _, vjp = jax.vjp(_gelu, rows[1])
        ds_ = vjp(accs[0] + rows[0])[0]
        return [ds_, ds_]

    dssm, dssm_b = _mm("dssm", [(dz, Wt['w_glu'], W)], L, W, tm=_div(L, 512), tn=_div(W, 512), tb=True, epi=dssm_epi,
                       outs=[(F32, False), (BF16, False)], rows=[(dyd, 0, 0), (ssm_pre, 0, 0)])
    dwglu = _mm("dw_glu", [(ygb, dz, L)], W, W, tm=_div(W, 512), tn=_div(W, 1024), ta=True, epi=ident,
                outs=[(BF16, False)])[0].reshape(N_CHIPS, W // N_CHIPS, W)
    tok_r2a, scatter_fin2a = _scatter_split("scatter_mix", [dwglu, dwba, dwbs, dwout], dbglu)
    dssm_full = jnp.concatenate([jnp.zeros((Lc, W), BF16), dssm_b], axis=0)
    dus, dlam_re, dlam_im, dcoef_re, dcoef_im, dbf, dcf_re, dcf_im = [], [], [], [], [], [], [], []
    for d in range(2):
        r = _ssm_bwd("ssm_bwd%d" % d, dssm_full, hs_re[d], hs_im[d], ub, bbd[d], bbdt_re[d], bbdt_im[d],
                     cbdt_re[d], cbdt_im[d], lam_re[d] + tok_r2a[0:1, 0:1], lam_im[d], coef_re[d], coef_im[d], Lc,
                     reverse=bool(d))
        for lst, val in zip((dus, dlam_re, dlam_im, dcoef_re, dcoef_im, dbf, dcf_re, dcf_im), r):
            lst.append(val)
    dqr, dkr, dvf = _attn_bwd(qr, kr, vb, dattn, L, Lc, D)

    def prep_bwd_fn(i, qt, kt, ut, dqt, dkt, dvt, du0, du1, dst, dgat, dgst, dt, qgt, kgt, ct, st):
        live = i >= ncr
        dqt = jnp.where(live, dqt, 0.0)
        dst = jnp.where(live, dst, 0.0)
        dgat = jnp.where(live, dgat, jnp.zeros_like(dgat))
        dgst = jnp.where(live, dgst, jnp.zeros_like(dgst))
        dqs, dks = [], []
        dqg_ = jnp.zeros((1, HEAD_DIM), F32)
        dkg_ = jnp.zeros((1, HEAD_DIM), F32)
        for h in range(nh):
            hl = slice(h * HEAD_DIM, (h + 1) * HEAD_DIM)
            dn = dqt[:, hl] * ct + _rot(dqt[:, hl] * st)
            _, vjp = jax.vjp(_head_norm, qt[:, hl], qgt)
            dxh, dgh = vjp(dn)
            dqs.append(dxh)
            dqg_ = dqg_ + dgh
        for h in range(nkvh):
            hl = slice(h * HEAD_DIM, (h + 1) * HEAD_DIM)
            dn = dkt[:, hl] * ct + _rot(dkt[:, hl] * st)
            _, vjp = jax.vjp(_head_norm, kt[:, hl], kgt)
            dxh, dgh = vjp(dn)
            dks.append(dxh)
            dkg_ = dkg_ + dgh
        du_ = du0 + du1 + dst * dt
        dproj_ = jnp.concatenate([c_.astype(BF16) for c_ in dks + [dvt, du_] + dqs + [dgat, dgst]], axis=1)
        return [dproj_, dqg_, dkg_, jnp.sum(dst * ut, axis=0, keepdims=True)]

    dproj, dqg, dkg, dssd = _rowk(
        "qk_prep_bwd", prep_bwd_fn, T, tr,
        [(proj, ('col', D, 1)), (proj, ('col', KV, 0)), (proj, ('col', W, 1)), (dqr, 'xrow'), (dkr, 'row'),
         (dvf, 'row'), (dus[0], 'row'), (dus[1], 'row'), (dssm, 'xrow'), (dga, 'xrow'), (dgs, 'xrow'), (ssm_d, 'vec'),
         (qg, 'vec'), (kg, 'vec'), (cos_t, 'row'), (sin_t, 'row')],
        [((T, 4 * D), BF16, 'row'), ((1, HEAD_DIM), F32, 'acc'), ((1, HEAD_DIM), F32, 'acc'), ((1, W), F32, 'acc')],
        nc=ncr)
    dh2 = _mm("in_proj_dx", [(dproj, Wt['w_in'], 4 * D)], T, D, tm=_div(T, 768), tn=_div(D, 1024), nk=N_CHIPS, tb=True,
              epi=ident, outs=[(F32, False)])[0]
    dwin = _mm("in_proj_dw", [(h2, dproj, T)], D, 4 * D, tm=_div(D, 512), tn=_div(D, 1024), ta=True, epi=ident,
               outs=[(BF16, True)])[0]
    tok_r2, scatter_fin2 = _scatter_split("scatter_w_in", [dwin], dqg)
    dx1, dng2, dsh2, dsc2 = norm_mod_bwd("norm2_bwd", x1, ng[1:2] + tok_r2[0:1, 0:1], sh2, sc2, dh2, dx2, 'xrow')
    early = {}

    def start_down(dwd):
        early['tok'], early['fin'] = _scatter_split("scatter_ffn1_down", [dwd], dg2)

    dh1, dg1, dwg1, dwu1, dwd1 = ffn_bwd("ffn1", dx1, h1, a1, b1, s1, f1, g1, Wt['w_ffn1_gate'], Wt['w_ffn1_up'],
                                         Wt['w_ffn1_down'], on_dwd=start_down)
    dx0, dng1, dsh1, dsc1 = norm_mod_bwd("norm1_bwd", xc, ng[0:1] + early['tok'][0:1, 0:1], sh1, sc1, dh1, dx1, 'row')
    grad_x = dx0[Lc:][None]

    zD = jnp.zeros((1, D), F32)
    dmod_x = jnp.concatenate([dsh1[1:2], dsc1[1:2], dg1[1:2], dsh2[1:2], dsc2[1:2], dg2, dsh3, dsc3, dg3], axis=1)
    dmod_c = jnp.concatenate([dsh1[0:1], dsc1[0:1], dg1[0:1], dsh2[0:1], dsc2[0:1], zD, zD, zD, zD], axis=1)
    pieces = [dmod_x, dmod_c, dng1, dng2, dng3, dqg, dkg] + dlam_re + dlam_im + dcoef_re + dcoef_im \
        + dbf + dcf_re + dcf_im + [dssd, dbglu]
    shapes = [p_.shape for p_ in pieces]
    pack = _pack(pieces)
    RP = pack.shape[0]
    tok_small, small_gathered = _allgather_split("gather_small", pack, me, dng1)
    tok_r3, scatter_fin3 = _scatter_split("scatter_ffn1_up", [dwg1, dwu1], tok_small)
    results = {}

    def sum_group(tag, names, fin, after_work):
        sent, landed = fin(after_work)
        plane = [_sum_plane("sum_" + n, g_, rb, chip_index) for n, g_, rb in zip(names, sent, landed)]
        tok_, swapped = _swap_split("swap_" + tag, plane, chip_index)
        return tok_, (names, swapped)

    def update_group(group, after_work):
        names, swapped = group
        mine, theirs = swapped(after_work)
        for n, m_, t_ in zip(names, mine, theirs):
            results[n] = _adamw("adamw_" + n, A[n], A['m_' + n], A['v_' + n], [m_, t_])

    tok_a, grp_ffn2 = sum_group("ffn2", big[3:6], scatter_fin1, tok_r3)
    tok_b, grp_mix = sum_group("mix", big[7:11], scatter_fin2a, tok_a)
    tok_c, grp_w_in = sum_group("w_in", big[6:7], scatter_fin2, tok_b)
    update_group(grp_ffn2, tok_c)
    tok_d, grp_down = sum_group("ffn1_down", big[2:3], early['fin'], results['w_ffn2_down'][0])
    update_group(grp_mix, tok_d)
    update_group(grp_w_in, results['w_out'][0])
    update_group(grp_down, results['w_in'][0])
    allp = small_gathered(results['w_ffn1_down'][0])
    head_rows = -(-18 * D // PACK_W)
    head = allp[:, :head_rows].reshape(N_DEV, head_rows * PACK_W)
    dmx_all = head[:, :9 * D]

    def sum_rows_fn(i, t):
        s_ = t[0:1]
        for k in range(1, N_DEV):
            s_ = s_ + t[k:k + 1]
        return [s_]

    dmc_sum = _rowk("sum_dmod_c", sum_rows_fn, 1, 1, [(head[:, 9 * D:18 * D], 'vec')], [((1, 9 * D), F32, 'row')])[0]
    cots = jnp.concatenate([dmx_all, dmc_sum, jnp.zeros((7, 9 * D), F32)], axis=0)
    cots_sh = lax.dynamic_slice(cots, (0, chip * NM), (16, NM))
    part = _mm("cctx_part", [(cots_sh[8:16], wm, NM)], 8, D, tm=8, tn=_div(D, 1024), nk=NM // _div(NM, 1152), tb=True,
               epi=ident, outs=[(F32, False)], a_pro=to_bf, b_pro=to_bf)[0]
    _, cctx_gathered = _allgather_split("gather_cctx", part, me, part)

    def sum_dev_fn(i, t):
        s_ = t[0]
        for k in range(1, N_DEV):
            s_ = s_ + t[k]
        return [s_]

    tot = _rowk("sum_small", sum_dev_fn, RP, 8, [(allp, 'row3')], [((RP, PACK_W), F32, 'row')])[0]
    (t_dmod_x, t_dmod_c, t_ng1, t_ng2, t_ng3, t_qg, t_kg, t_lr0, t_lr1, t_li0, t_li1, t_kr0, t_kr1, t_ki0, t_ki1,
     t_dbf0, t_dbf1, t_dcr0, t_dcr1, t_dci0, t_dci1, t_d, t_bglu) = _unpack(tot, shapes)
    b_grad = lambda t, lo: jnp.transpose(t[:, :, lo:lo + P].reshape(G, E, P), (0, 2, 1))
    c_grad = lambda t: jnp.transpose(t.reshape(nslab, P, SLAB_GROUPS, E), (0, 2, 3, 1)).reshape(G, E, P)
    cat2 = lambda u0, u1: jnp.concatenate([u0.reshape(G, P), u1.reshape(G, P)], axis=0)
    g_are, g_aim, g_ldt = _zoh_bwd(a_re2, a_im2, ldt2, [cat2(t_lr0, t_lr1), cat2(t_li0, t_li1),
                                                         cat2(t_kr0, t_kr1), cat2(t_ki0, t_ki1)])
    g_bmod = _rowk("bmod_grad", lambda i, u0, u1: [u0 + u1], 1, 1, [(t_dmod_x, 'row'), (t_dmod_c, 'row')],
                   [((1, 9 * D), F32, 'row')])[0]
    g_wmod = _outer_sum(acts, cots_sh)
    results['w_mod'] = _adamw("adamw_w_mod", w_mod, m_w_mod, v_w_mod, [g_wmod])
    done = sum(results[n][1].reshape(-1, results[n][1].shape[-1])[0:1, 0:1] for n in list(results)) + g_are[0:1, 0:1] \
        + g_bmod[0:1, 0:1]
    tok_e, grp_up = sum_group("ffn1_up", big[0:2], scatter_fin3, done)
    parts = cctx_gathered(tok_e).reshape(N_CHIPS, 2, 8, D)[:, 0, 0]

    def cctx_fn(i, pt, ct):
        ds_ = ((pt[0:1] + pt[1:2]) + pt[2:3]) + pt[3:4]
        _, vjp = jax.vjp(lambda v: v * _sigmoid(v), ct)
        return [vjp(ds_)[0]]

    g_cctx = _rowk("cctx_grad", cctx_fn, 1, 1, [(parts, 'vec'), (c_ctx[None], 'row')], [((1, D), F32, 'row')])[0]

    ng_full =jnp.concatenate([t_ng1, t_ng2, t_ng3], axis=0)
    gsmall = {
        'c_ctx': g_cctx, 'b_mod': g_bmod, 'norm_g': lax.dynamic_slice(ng_full, (0, chip * Dq), (3, Dq)),
        'q_norm_g': t_qg, 'k_norm_g': t_kg, 'ssm_a_re': g_are, 'ssm_a_im': g_aim, 'ssm_log_dt': g_ldt,
        'ssm_b_re': jnp.stack([b_grad(t_dbf0, 0), b_grad(t_dbf1, 0)]),
        'ssm_b_im': jnp.stack([b_grad(t_dbf0, P), b_grad(t_dbf1, P)]),
        'ssm_c_re': jnp.stack([c_grad(t_dcr0), c_grad(t_dcr1)]), 'ssm_c_im': jnp.stack([c_grad(t_dci0), c_grad(t_dci1)]),
        'ssm_d': t_d, 'b_glu': t_bglu}
    sshapes = [A[n].shape for n in small]
    sres = _adamw("adamw_small", packs_wmv[0], packs_wmv[1], packs_wmv[2], [_pack([gsmall[n] for n in small])])
    update_group(grp_up, sres[0])
    sres = [_unpack(b_, sshapes) for b_ in sres]
    for k, n in enumerate(small):
        results[n] = tuple(sres[q][k] for q in range(4))

    order = ['c_ctx', 'w_mod', 'b_mod', 'norm_g', 'w_ffn1_gate', 'w_ffn1_up', 'w_ffn1_down', 'w_in', 'q_norm_g',
             'k_norm_g', 'ssm_a_re', 'ssm_a_im', 'ssm_log_dt', 'ssm_b_re', 'ssm_b_im', 'ssm_c_re', 'ssm_c_im',
             'ssm_d', 'w_glu', 'b_glu', 'w_br_attn', 'w_br_ssm', 'w_out', 'w_ffn2_gate', 'w_ffn2_up', 'w_ffn2_down']
    outs = [loss, grad_x]
    for q in range(4):
        outs += [results[n][q].reshape(A[n].shape) for n in order]
    return tuple(outs)
```

```python
import math

import jax
import jax.numpy as jnp
from jax import lax
from jax.experimental import pallas as pl
from jax.experimental.pallas import tpu as pltpu

F32 = jnp.float32
BF16 = jnp.bfloat16
MESH = pl.DeviceIdType.MESH

NORM_EPS = 1e-6
ROPE_THETA = 10000.0
GRID_W = 64
HEAD_DIM = 128
Q_PER_KV = 4
SSM_GROUP = 16
SSM_STATE = 64
ADAM_LR = 0.001
ADAM_B1 = 0.9
ADAM_B2 = 0.999
ADAM_EPS = 1e-08
ADAM_WD = 0.01
ADAM_STEP = 10

N_CHIPS = 4
N_DEV = 8
LANES = 128
SLAB_CH = 128
SLAB_GROUPS = SLAB_CH // SSM_GROUP
SLAB_ST = SLAB_GROUPS * SSM_STATE
VMEM_LIMIT_BYTES = 56 * 1024 * 1024
PACK_W = 1024


def _cparams(**kw):
    return pltpu.CompilerParams(vmem_limit_bytes=VMEM_LIMIT_BYTES, **kw)


def _div(n, pref, mult=LANES):
    t = (min(pref, n) // mult) * mult
    while t >= mult:
        if n % t == 0:
            return t
        t -= mult
    return n


def _sigmoid(x):
    return jax.nn.sigmoid(x)


def _gelu(x):
    return x * (0.5 * (1.0 + jnp.tanh(math.sqrt(2.0 / math.pi) * (x + 0.044715 * (x * x * x)))))


def _rowk(name, fn, nrows, tr, ins, outs, nc=0):
    nt = nrows // tr
    in_specs, arrays = [], []
    for arr, kind in ins:
        arrays.append(arr)
        if kind == 'row':
            in_specs.append(pl.BlockSpec((tr, arr.shape[1]), lambda i: (i, 0)))
        elif kind == 'xrow':
            in_specs.append(pl.BlockSpec((tr, arr.shape[1]), lambda i: (jnp.maximum(i - nc, 0), 0)))
        elif kind == 'orow':
            in_specs.append(pl.BlockSpec((tr, arr.shape[1]), lambda i: (i + nc, 0)))
        elif kind == 'vec':
            in_specs.append(pl.BlockSpec(arr.shape, lambda i, nd=arr.ndim: (0,) * nd))
        elif kind == 'row3':
            in_specs.append(pl.BlockSpec((arr.shape[0], tr, arr.shape[2]), lambda i: (0, i, 0)))
        elif kind == 'row1':
            in_specs.append(pl.BlockSpec((None, tr, arr.shape[2]), lambda i: (0, i, 0)))
        elif kind[0] == 'ocol':
            _, width, blk = kind
            in_specs.append(pl.BlockSpec((tr, width), lambda i, blk=blk: (i + nc, blk)))
        else:
            _, width, blk = kind
            in_specs.append(pl.BlockSpec((tr, width), lambda i, blk=blk: (i, blk)))
    out_shape, out_specs = [], []
    for shape, dtype, kind in outs:
        out_shape.append(jax.ShapeDtypeStruct(shape, dtype))
        if kind == 'row':
            out_specs.append(pl.BlockSpec((tr, shape[1]), lambda i: (i, 0)))
        elif kind == 'row1':
            out_specs.append(pl.BlockSpec((None, tr, shape[2]), lambda i: (0, i, 0)))
        else:
            out_specs.append(pl.BlockSpec(shape, lambda i, nd=len(shape): (0,) * nd))
    nin = len(ins)

    def body(*refs):
        i = pl.program_id(0)
        res = fn(i, *[r[...] for r in refs[:nin]])
        for (shape, dtype, kind), ref, val in zip(outs, refs[nin:], res):
            if kind in ('row', 'row1'):
                ref[...] = val.astype(dtype)
            else:
                @pl.when(i == 0)
                def _():
                    ref[...] = val.astype(dtype)

                @pl.when(i > 0)
                def _():
                    ref[...] += val.astype(dtype)

    return pl.pallas_call(body, name=name, grid=(nt,), in_specs=in_specs, out_specs=out_specs,
                          out_shape=out_shape, compiler_params=_cparams())(*arrays)


def _mm(name, pairs, M, N, *, tm, tn, nk=1, epi, outs, ta=False, tb=False, rows=(), vecs=(),
        a_pro=None, b_pro=None, n_outer=True, summed=False):
    nm, nn = M // tm, N // tn
    npair = len(pairs)

    def idx(f):
        if n_outer:
            return lambda j, i, k: f(i, j, k)
        return lambda i, j, k: f(i, j, k)

    in_specs, args = [], []
    for a, b, K in pairs:
        tk = K // nk
        if ta:
            in_specs.append(pl.BlockSpec((tk, tm), idx(lambda i, j, k: (k, i))))
        else:
            in_specs.append(pl.BlockSpec((tm, tk), idx(lambda i, j, k: (i, k))))
        args.append(a)
        if b.ndim == 3:
            if tb:
                per = b.shape[2] // tk
                in_specs.append(pl.BlockSpec((None, tn, tk), idx(lambda i, j, k, per=per: (k // per, j, k % per))))
            else:
                per = b.shape[2] // tn
                in_specs.append(pl.BlockSpec((None, tk, tn), idx(lambda i, j, k, per=per: (j // per, k, j % per))))
        elif tb:
            in_specs.append(pl.BlockSpec((tn, tk), idx(lambda i, j, k: (j, k))))
        else:
            in_specs.append(pl.BlockSpec((tk, tn), idx(lambda i, j, k: (k, j))))
        args.append(b)
    for arr, ro, co in rows:
        in_specs.append(pl.BlockSpec((tm, tn), idx(lambda i, j, k, ro=ro, co=co: (i + ro, j + co))))
        args.append(arr)
    for arr in vecs:
        in_specs.append(pl.BlockSpec((arr.shape[0], tn), idx(lambda i, j, k: (0, j))))
        args.append(arr)
    out_shape, out_specs = [], []
    for dtype, chunked in outs:
        if chunked:
            per = (N // N_CHIPS) // tn
            out_shape.append(jax.ShapeDtypeStruct((N_CHIPS, M, N // N_CHIPS), dtype))
            out_specs.append(pl.BlockSpec((None, tm, tn), idx(lambda i, j, k, per=per: (j // per, i, j % per))))
        else:
            out_shape.append(jax.ShapeDtypeStruct((M, N), dtype))
            out_specs.append(pl.BlockSpec((tm, tn), idx(lambda i, j, k: (i, j))))
    nacc = 1 if summed else npair
    scratch = [pltpu.VMEM((tm, tn), F32) for _ in range(nacc)] if nk > 1 else []
    nrow, nvec, nout = len(rows), len(vecs), len(outs)
    dims = (((0 if ta else 1,), (1 if tb else 0,)), ((), ()))

    def body(*refs):
        ab = refs[:2 * npair]
        row_refs = refs[2 * npair:2 * npair + nrow]
        vec_refs = refs[2 * npair + nrow:2 * npair + nrow + nvec]
        out_refs = refs[2 * npair + nrow + nvec:2 * npair + nrow + nvec + nout]
        acc_refs = refs[2 * npair + nrow + nvec + nout:]
        if n_outer:
            j, i, k = pl.program_id(0), pl.program_id(1), pl.program_id(2)
        else:
            i, j, k = pl.program_id(0), pl.program_id(1), pl.program_id(2)

        def part(p):
            av, bv = ab[2 * p][...], ab[2 * p + 1][...]
            if a_pro is not None:
                av = a_pro(av)
            if b_pro is not None:
                bv = b_pro(bv)
            return lax.dot_general(av, bv, dims, preferred_element_type=F32)

        def finish(accs):
            row_index = i * tm + lax.broadcasted_iota(jnp.int32, (tm, 1), 0)
            res = epi(accs, [r[...] for r in row_refs], [v[...] for v in vec_refs], row_index)
            for ref, val in zip(out_refs, res):
                ref[...] = val.astype(ref.dtype)

        parts = [part(p) for p in range(npair)]
        if summed:
            total = parts[0]
            for extra in parts[1:]:
                total = total + extra
            parts = [total]
        if nk == 1:
            finish(parts)
        else:
            @pl.when(k == 0)
            def _():
                for q in range(nacc):
                    acc_refs[q][...] = parts[q]

            @pl.when(jnp.logical_and(k > 0, k < nk - 1))
            def _():
                for q in range(nacc):
                    acc_refs[q][...] += parts[q]

            @pl.when(k == nk - 1)
            def _():
                finish([acc_refs[q][...] + parts[q] for q in range(nacc)])

    grid = (nn, nm, nk) if n_outer else (nm, nn, nk)
    return pl.pallas_call(body, name=name, grid=grid, in_specs=in_specs, out_specs=out_specs,
                          out_shape=out_shape, scratch_shapes=scratch, compiler_params=_cparams())(*args)


def _split3(v):
    v0 = v.astype(BF16)
    r1 = v - v0.astype(F32)
    v1 = r1.astype(BF16)
    v2 = (r1 - v1.astype(F32)).astype(BF16)
    return v0, v1, v2


def _mesh_pos():
    return lax.axis_index("x"), lax.axis_index("y"), lax.axis_index("c")


def _allgather_small(name, x):
    m, n = x.shape

    def body(x_ref, out_ref, send_sems, recv_sems, local_sem):
        xi, yi, ci = _mesh_pos()
        me, sibling = (xi, yi, ci), (xi, yi, 1 - ci)
        chips = [(1 - xi, yi), (xi, 1 - yi), (1 - xi, 1 - yi)]

        def rows(px, py, pc):
            return out_ref.at[pl.ds((4 * px + 2 * py + pc) * m, m), :]

        def copy(k, block, to, src=None):
            return pltpu.make_async_remote_copy(
                src_ref=rows(*block) if src is None else src, dst_ref=rows(*block),
                send_sem=send_sems.at[k], recv_sem=recv_sems.at[k], device_id=to, device_id_type=MESH)

        mine = pltpu.make_async_copy(x_ref, rows(*me), local_sem)
        mine.start()
        first = [copy(0, me, sibling, src=x_ref)]
        first += [copy(1 + j, me, (*chip, ci), src=x_ref) for j, chip in enumerate(chips)]
        for cp in first:
            cp.start()
        passed = [copy(4 + j, (*chip, ci), sibling) for j, chip in enumerate(chips)]
        for j, chip in enumerate(chips):
            copy(1 + j, (*chip, ci), me).wait_recv()
            passed[j].start()
        copy(0, sibling, me).wait_recv()
        for j, chip in enumerate(chips):
            copy(4 + j, (*chip, 1 - ci), me).wait_recv()
        for cp in first + passed:
            cp.wait_send()
        mine.wait()

    return pl.pallas_call(
        body, name=name, out_shape=jax.ShapeDtypeStruct((N_DEV * m, n), x.dtype),
        in_specs=[pl.BlockSpec(memory_space=pltpu.VMEM)], out_specs=pl.BlockSpec(memory_space=pltpu.VMEM),
        scratch_shapes=[pltpu.SemaphoreType.DMA((7,)), pltpu.SemaphoreType.DMA((7,)), pltpu.SemaphoreType.DMA],
        compiler_params=_cparams())(x)


_HBM = pl.BlockSpec(memory_space=pltpu.HBM)
_SEM = pl.BlockSpec(memory_space=pltpu.SEMAPHORE)
_ANY = pl.BlockSpec(memory_space=pl.ANY)
_EFFECT = pltpu.SideEffectType.DATAFLOW_SIDE_EFFECTING


def _in_hbm(v):
    return pltpu.with_memory_space_constraint(v, pltpu.HBM)


def _other_chips(xi, yi):
    return [(1 - xi, yi), (xi, 1 - yi), (1 - xi, 1 - yi)]


def _guarded(core, fn):
    if core is None:
        fn()
    else:
        pl.when(lax.axis_index("c") == core)(fn)


def _split_copies(name, srcs, lands, after, pairs, senders, receivers, ncopy):
    ns, nl = len(srcs), len(lands)
    dma = pltpu.SemaphoreType.DMA((ncopy,))
    thru = [pltpu.HBM(v.shape, v.dtype) for v in list(srcs) + list(lands)]

    def start_body(*refs):
        src_refs, land_refs = refs[:ns], refs[ns:ns + nl]
        descs = pairs(src_refs, land_refs, refs[ns + nl + 1], refs[ns + nl + 2])

        def go():
            for send, _ in descs:
                send.start()

        _guarded(senders, go)
        refs[-1][...] = jnp.zeros_like(refs[-1])

    res = pl.pallas_call(
        start_body, name=name + "_start",
        out_shape=(dma, dma, *thru, jax.ShapeDtypeStruct((8, LANES), F32)),
        in_specs=[_HBM] * (ns + nl) + [_ANY],
        out_specs=(_SEM, _SEM, *([_HBM] * (ns + nl)), pl.BlockSpec(memory_space=pltpu.VMEM)),
        input_output_aliases={k: 2 + k for k in range(ns + nl)},
        compiler_params=_cparams(has_side_effects=_EFFECT),
    )(*[_in_hbm(v) for v in srcs], *[_in_hbm(v) for v in lands], after)
    send_sems, recv_sems, token = res[0], res[1], res[-1]
    carried = res[2:2 + ns + nl]

    def finish(after_work):
        def wait_body(*refs):
            src_refs, land_refs = refs[:ns], refs[ns:ns + nl]
            descs = pairs(src_refs, land_refs, refs[ns + nl], refs[ns + nl + 1])

            def sent():
                for send, _ in descs:
                    send.wait_send()

            def landed():
                for _, recv in descs:
                    recv.wait_recv()

            _guarded(senders, sent)
            _guarded(receivers, landed)

        out = pl.pallas_call(
            wait_body, name=name + "_wait", out_shape=tuple(thru),
            in_specs=[_HBM] * (ns + nl) + [_SEM, _SEM, _ANY], out_specs=tuple([_HBM] * (ns + nl)),
            input_output_aliases={k: k for k in range(ns + nl)},
            compiler_params=_cparams(has_side_effects=_EFFECT),
        )(*carried, send_sems, recv_sems, after_work)
        return list(out[:ns]), list(out[ns:])

    return token, finish


def _cast_slot(name, w, chip_index, after):
    R, C = w.shape[1:]
    tr = _div(R, max(16, 524288 // C), mult=16)

    def body(chip_ref, w_ref, after_ref, o_ref):
        o_ref[...] = w_ref[...].astype(BF16)

    return pl.pallas_call(
        body, name=name, out_shape=jax.ShapeDtypeStruct((N_CHIPS, R, C), BF16),
        grid_spec=pltpu.PrefetchScalarGridSpec(
            num_scalar_prefetch=1, grid=(R // tr,),
            in_specs=[pl.BlockSpec((None, tr, C), lambda i, chip_ref: (0, i, 0)), _ANY],
            out_specs=pl.BlockSpec((None, tr, C), lambda i, chip_ref: (chip_ref[0], i, 0))),
        compiler_params=_cparams())(chip_index, w, after)


def _sum_plane(name, grads, landed, chip_index):
    R, C = grads.shape[1:]
    tr = _div(R, max(16, 1048576 // C), mult=16)

    def body(chip_ref, own_ref, land_ref, o_ref):
        o_ref[...] = ((own_ref[...].astype(F32) + land_ref[0].astype(F32)) + land_ref[1].astype(F32)) \
            + land_ref[2].astype(F32)

    return pl.pallas_call(
        body, name=name, out_shape=jax.ShapeDtypeStruct((R, C), F32),
        grid_spec=pltpu.PrefetchScalarGridSpec(
            num_scalar_prefetch=1, grid=(R // tr,),
            in_specs=[pl.BlockSpec((None, tr, C), lambda i, chip_ref: (chip_ref[0], i, 0)),
                      pl.BlockSpec((3, tr, C), lambda i, chip_ref: (0, i, 0))],
            out_specs=pl.BlockSpec((tr, C), lambda i, chip_ref: (i, 0))),
        compiler_params=_cparams())(chip_index, grads, landed)


def _gather_split(name, lands, after):
    def pairs(src_refs, land_refs, send_sems, recv_sems):
        xi, yi, _ = _mesh_pos()
        mine = 2 * xi + yi
        out = []
        for a in range(len(lands)):
            for j, (px, py) in enumerate(_other_chips(xi, yi)):
                def to_slot(slot, a=a, j=j, px=px, py=py):
                    return pltpu.make_async_remote_copy(
                        src_ref=land_refs[a].at[mine], dst_ref=land_refs[a].at[slot], send_sem=send_sems.at[3 * a + j],
                        recv_sem=recv_sems.at[3 * a + j], device_id=(px, py, 1), device_id_type=MESH)
                out.append((to_slot(mine), to_slot(2 * px + py)))
        return out

    return _split_copies(name, [], lands, after, pairs, senders=1, receivers=1, ncopy=3 * len(lands))


def _allgather_split(name, block, me, after):
    land = lax.dynamic_update_slice(lax.empty((N_DEV,) + block.shape, block.dtype), block[None], (me, 0, 0))

    def pairs(src_refs, land_refs, send_sems, recv_sems):
        xi, yi, ci = _mesh_pos()
        mine = 4 * xi + 2 * yi + ci
        out = []
        for k in range(1, N_DEV):
            kx, ky, kc = (k >> 2) & 1, (k >> 1) & 1, k & 1
            px = 1 - xi if kx else xi
            py = 1 - yi if ky else yi
            pc = 1 - ci if kc else ci

            def to_slot(slot, k=k, px=px, py=py, pc=pc):
                return pltpu.make_async_remote_copy(
                    src_ref=land_refs[0].at[mine], dst_ref=land_refs[0].at[slot], send_sem=send_sems.at[k - 1],
                    recv_sem=recv_sems.at[k - 1], device_id=(px, py, pc), device_id_type=MESH)
            out.append((to_slot(mine), to_slot(4 * px + 2 * py + pc)))
        return out

    tok, fin = _split_copies(name, [], [land], after, pairs, senders=None, receivers=None, ncopy=N_DEV - 1)
    return tok, lambda later: fin(later)[1][0]


def _swap_split(name, arrs, after):
    lands = [lax.empty(v.shape, v.dtype) for v in arrs]

    def pairs(src_refs, land_refs, send_sems, recv_sems):
        xi, yi, ci = _mesh_pos()
        out = []
        for a in range(len(arrs)):
            cp = pltpu.make_async_remote_copy(
                src_ref=src_refs[a], dst_ref=land_refs[a], send_sem=send_sems.at[a], recv_sem=recv_sems.at[a],
                device_id=(xi, yi, 1 - ci), device_id_type=MESH)
            out.append((cp, cp))
        return out

    return _split_copies(name, arrs, lands, after, pairs, senders=None, receivers=None, ncopy=len(arrs))


def _pass_split(name, lands, after):
    def pairs(src_refs, land_refs, send_sems, recv_sems):
        xi, yi, _ = _mesh_pos()
        out = []
        for a in range(len(lands)):
            for j, (px, py) in enumerate(_other_chips(xi, yi)):
                cp = pltpu.make_async_remote_copy(
                    src_ref=land_refs[a].at[2 * px + py], dst_ref=land_refs[a].at[2 * px + py],
                    send_sem=send_sems.at[3 * a + j], recv_sem=recv_sems.at[3 * a + j],
                    device_id=(xi, yi, 0), device_id_type=MESH)
                out.append((cp, cp))
        return out

    return _split_copies(name, [], lands, after, pairs, senders=1, receivers=0, ncopy=3 * len(lands))


def _scatter_split(name, grads, after):
    lands = [lax.empty((3,) + g.shape[1:], g.dtype) for g in grads]

    def pairs(src_refs, land_refs, send_sems, recv_sems):
        xi, yi, ci = _mesh_pos()
        out = []
        for a in range(len(grads)):
            for j, (px, py) in enumerate(_other_chips(xi, yi)):
                cp = pltpu.make_async_remote_copy(
                    src_ref=src_refs[a].at[2 * px + py], dst_ref=land_refs[a].at[j], send_sem=send_sems.at[3 * a + j],
                    recv_sem=recv_sems.at[3 * a + j], device_id=(px, py, ci), device_id_type=MESH)
                out.append((cp, cp))
        return out

    return _split_copies(name, grads, lands, after, pairs, senders=None, receivers=None, ncopy=3 * len(grads))


def _gather_finish(name, lands):
    na = len(lands)

    def body(*refs):
        outs = refs[na:2 * na]
        send_sems, recv_sems = refs[2 * na:]
        xi, yi, ci = _mesh_pos()
        passes = [pltpu.make_async_remote_copy(
            src_ref=outs[a].at[2 * px + py], dst_ref=outs[a].at[2 * px + py],
            send_sem=send_sems.at[a, j], recv_sem=recv_sems.at[a, j], device_id=(xi, yi, 0), device_id_type=MESH)
            for a in range(na) for j, (px, py) in enumerate(_other_chips(xi, yi))]

        @pl.when(ci == 1)
        def _():
            for cp in passes:
                cp.start()
            for cp in passes:
                cp.wait_send()

        @pl.when(ci == 0)
        def _():
            for cp in passes:
                cp.wait_recv()

    return pl.pallas_call(
        body, name=name, out_shape=[jax.ShapeDtypeStruct(v.shape, v.dtype) for v in lands],
        in_specs=[_ANY] * na, out_specs=[_ANY] * na,
        input_output_aliases={a: a for a in range(na)},
        scratch_shapes=[pltpu.SemaphoreType.DMA((na, 3)), pltpu.SemaphoreType.DMA((na, 3))],
        compiler_params=_cparams())(*lands)


ATTN_HEADS_PER_STEP = 2


def _attn_tiles(L, Lc, D, tq_pref=256):
    tq = min(tq_pref, Lc)
    return tq, L // tq, Lc // tq, D // HEAD_DIM // Q_PER_KV


def _attn_scores(q, k):
    return lax.dot_general(q, k, (((1,), (1,)), ((), ())), preferred_element_type=F32) * (HEAD_DIM ** -0.5)


def _softmax_rows(s):
    e = jnp.exp(s - jnp.max(s, axis=-1, keepdims=True))
    return e * (1.0 / jnp.sum(e, axis=-1, keepdims=True))


def _attn_probs(q, k):
    return _softmax_rows(_attn_scores(q, k))


def _attn_fwd(qr, kr, v, L, Lc, D):
    T = L + Lc
    tq, nq, qoff, nkv = _attn_tiles(L, Lc, D)
    hp = Q_PER_KV
    ng = Q_PER_KV // hp

    def body(q_ref, k_ref, v_ref, o_ref):
        k, vv = k_ref[...], v_ref[...]
        heads = [slice(r * HEAD_DIM, (r + 1) * HEAD_DIM) for r in range(hp)]
        scores = [_attn_scores(q_ref[:, cols], k) for cols in heads]
        probs = [_softmax_rows(s) for s in scores]
        for cols, p in zip(heads, probs):
            o_ref[:, cols] = jnp.dot(p.astype(BF16), vv, preferred_element_type=F32).astype(o_ref.dtype)

    kv_spec = pl.BlockSpec((T, HEAD_DIM), lambda h, r, q: (0, h))
    return pl.pallas_call(
        body, name="attn_fwd", grid=(nkv, ng, nq),
        in_specs=[pl.BlockSpec((tq, hp * HEAD_DIM), lambda h, r, q: (q + qoff, h * ng + r)), kv_spec, kv_spec],
        out_specs=pl.BlockSpec((tq, hp * HEAD_DIM), lambda h, r, q: (q, h * ng + r)),
        out_shape=jax.ShapeDtypeStruct((L, D), BF16), compiler_params=_cparams())(qr, kr, v)


def _attn_bwd(qr, kr, v, do, L, Lc, D):
    T = L + Lc
    tq, nq, qoff, nkv = _attn_tiles(L, Lc, D, 256)
    scale = HEAD_DIM ** -0.5
    hp = Q_PER_KV
    ng = Q_PER_KV // hp

    def body(q_ref, k_ref, v_ref, do_ref, dq_ref, dk_ref, dv_ref):
        first = jnp.logical_and(pl.program_id(1) == 0, pl.program_id(2) == 0)
        k, vv = k_ref[...], v_ref[...]
        nt_dims, tn_dims = (((1,), (1,)), ((), ())), (((0,), (0,)), ((), ()))
        heads = [slice(r * HEAD_DIM, (r + 1) * HEAD_DIM) for r in range(hp)]
        qs = [q_ref[:, cols] for cols in heads]
        douts = [do_ref[:, cols] for cols in heads]
        scores = [_attn_scores(q, k) for q in qs]
        dps = [lax.dot_general(dout, vv, nt_dims, preferred_element_type=F32) for dout in douts]
        probs = [_softmax_rows(s) for s in scores]
        dss = [(p * (dp - jnp.sum(p * dp, axis=-1, keepdims=True)) * scale).astype(BF16) for p, dp in zip(probs, dps)]
        for cols, ds in zip(heads, dss):
            dq_ref[:, cols] = jnp.dot(ds, k, preferred_element_type=F32)
        dk = dv = None
        for q, dout, p, ds in zip(qs, douts, probs, dss):
            dk_r = lax.dot_general(ds, q, tn_dims, preferred_element_type=F32)
            dv_r = lax.dot_general(p.astype(BF16), dout, tn_dims, preferred_element_type=F32)
            dk = dk_r if dk is None else dk + dk_r
            dv = dv_r if dv is None else dv + dv_r

        @pl.when(first)
        def _():
            dk_ref[...] = dk
            dv_ref[...] = dv

        @pl.when(jnp.logical_not(first))
        def _():
            dk_ref[...] += dk
            dv_ref[...] += dv

    kv_spec = pl.BlockSpec((T, HEAD_DIM), lambda h, r, q: (0, h))
    q_spec = pl.BlockSpec((tq, hp * HEAD_DIM), lambda h, r, q: (q + qoff, h * ng + r))
    o_spec = pl.BlockSpec((tq, hp * HEAD_DIM), lambda h, r, q: (q, h * ng + r))
    return pl.pallas_call(
        body, name="attn_bwd", grid=(nkv, ng, nq),
        in_specs=[q_spec, kv_spec, kv_spec, o_spec], out_specs=[o_spec, kv_spec, kv_spec],
        out_shape=[jax.ShapeDtypeStruct((L, D), F32), jax.ShapeDtypeStruct((T, D // Q_PER_KV), F32),
                   jax.ShapeDtypeStruct((T, D // Q_PER_KV), F32)],
        compiler_params=_cparams())(qr, kr, v, do)


SUB = 8


def _doubling(xr, xi, pw_re, pw_im, lanes, first_power, period, reverse):
    n = xr.shape[0]
    rows = lax.broadcasted_iota(jnp.int32, (n, 1), 0) & (period - 1)
    for k in range(period.bit_length() - 1):
        d = 1 << k
        keep = rows < period - d if reverse else rows >= d
        sr = jnp.where(keep, pltpu.roll(xr, n - d if reverse else d, 0), 0.0)
        si = jnp.where(keep, pltpu.roll(xi, n - d if reverse else d, 0), 0.0)
        pr, pi = pw_re[first_power + k:first_power + k + 1, lanes], pw_im[first_power + k:first_power + k + 1, lanes]
        xr, xi = xr + (pr * sr - pi * si), xi + (pr * si + pi * sr)
    return xr, xi


def _scan_tile(xr, xi, tb, lanes, reverse):
    pw_re, pw_im, w8_re, w8_im, wb_re, wb_im, carry_re, carry_im, sr, si = tb
    tt = xr.shape[0]
    nb = tt // SUB
    nq = sr.shape[0]
    cols = [slice(q * LANES, (q + 1) * LANES) for q in range(nq)]
    for q in range(nq):
        sr[q] = xr[:, cols[q]]
        si[q] = xi[:, cols[q]]
    order = list(range(SUB - 2, -1, -1)) if reverse else list(range(1, SUB))
    ends_r, ends_i = [], []
    for q in range(nq):
        ql = slice(lanes.start + q * LANES, lanes.start + (q + 1) * LANES)
        lr, li = pw_re[0:1, ql], pw_im[0:1, ql]
        first_row = pl.ds(SUB - 1 if reverse else 0, nb, stride=SUB)
        pr, pi = sr[q, first_row, :], si[q, first_row, :]
        for r in order:
            rows = pl.ds(r, nb, stride=SUB)
            pr, pi = sr[q, rows, :] + (lr * pr - li * pi), si[q, rows, :] + (lr * pi + li * pr)
            sr[q, rows, :] = pr
            si[q, rows, :] = pi
        ends_r.append(pr)
        ends_i.append(pi)
    er, ei = jnp.concatenate(ends_r, axis=1), jnp.concatenate(ends_i, axis=1)
    er, ei = _doubling(er, ei, pw_re, pw_im, lanes, 3, nb, reverse)
    car, cai = carry_re[:, lanes], carry_im[:, lanes]
    wbr, wbi = wb_re[:, lanes], wb_im[:, lanes]
    er = er + (wbr * car - wbi * cai)
    ei = ei + (wbr * cai + wbi * car)
    out_block = 0 if reverse else nb - 1
    carry_re[:, lanes] = er[out_block:out_block + 1, :]
    carry_im[:, lanes] = ei[out_block:out_block + 1, :]
    blocks = lax.broadcasted_iota(jnp.int32, (nb, 1), 0)
    first = blocks == (nb - 1 if reverse else 0)
    cr = jnp.where(first, car, pltpu.roll(er, nb - 1 if reverse else 1, 0))
    ci = jnp.where(first, cai, pltpu.roll(ei, nb - 1 if reverse else 1, 0))
    for r in range(SUB):
        wr, wi = w8_re[r:r + 1, lanes], w8_im[r:r + 1, lanes]
        add_r, add_i = wr * cr - wi * ci, wr * ci + wi * cr
        for q in range(nq):
            sr[q, pl.ds(r, nb, stride=SUB), :] += add_r[:, cols[q]]
            si[q, pl.ds(r, nb, stride=SUB), :] += add_i[:, cols[q]]
    hr = jnp.concatenate([sr[q] for q in range(nq)], axis=1)
    hi = jnp.concatenate([si[q] for q in range(nq)], axis=1)
    return hr, hi, car, cai


def _scan_scratch(tt, NS):
    nb = tt // SUB
    return [pltpu.VMEM((8, NS), F32), pltpu.VMEM((8, NS), F32), pltpu.VMEM((SUB, NS), F32), pltpu.VMEM((SUB, NS), F32),
            pltpu.VMEM((nb, NS), F32), pltpu.VMEM((nb, NS), F32), pltpu.VMEM((1, NS), F32), pltpu.VMEM((1, NS), F32),
            pltpu.VMEM((SLAB_ST // LANES, tt, LANES), F32), pltpu.VMEM((SLAB_ST // LANES, tt, LANES), F32)]


def _scan_init(lr, li, tb, reverse):
    pw_re, pw_im, w8_re, w8_im, wb_re, wb_im, carry_re, carry_im, sr, _ = tb
    nb = wb_re.shape[0]
    carry_re[...] = jnp.zeros_like(carry_re)
    carry_im[...] = jnp.zeros_like(carry_im)
    pr, pi = lr, li
    for k in range(3 + nb.bit_length() - 1):
        pw_re[k:k + 1, :] = pr
        pw_im[k:k + 1, :] = pi
        if k == 3:
            l8r, l8i = pr, pi
        pr, pi = pr * pr - pi * pi, 2.0 * pr * pi
    pr, pi = lr, li
    for r in range(SUB):
        row = SUB - 1 - r if reverse else r
        w8_re[row:row + 1, :] = pr
        w8_im[row:row + 1, :] = pi
        pr, pi = pr * lr - pi * li, pr * li + pi * lr
    pr, pi = l8r, l8i
    for b in range(nb):
        row = nb - 1 - b if reverse else b
        wb_re[row:row + 1, :] = pr
        wb_im[row:row + 1, :] = pi
        pr, pi = pr * l8r - pi * l8i, pr * l8i + pi * l8r


def _ssm_tiles(T, Lc):
    tt = min(128, Lc)
    return tt, T // tt, Lc // tt


def _ssm_fwd(name, u, bbd, cbd_re, cbd_im, lam_re, lam_im, coef_re, coef_im, Lc, reverse):
    T, W = u.shape
    nslab = W // SLAB_CH
    NS = nslab * SLAB_ST
    tt, nt, nc = _ssm_tiles(T, Lc)
    if reverse:
        tile = lambda s: jnp.where(s < nc, nc - 1 - s, nt - 1 - (s - nc))
    else:
        tile = lambda s: s

    def body(u_ref, b_ref, cr_ref, ci_ref, lr_ref, li_ref, kr_ref, ki_ref, hr_ref, hi_ref, y_ref, *tb):
        @pl.when(pl.program_id(0) == 0)
        def _():
            _scan_init(lr_ref[...], li_ref[...], tb, reverse)

        for j in range(nslab):
            lanes = slice(j * SLAB_ST, (j + 1) * SLAB_ST)
            bu = jnp.dot(u_ref[:, j * SLAB_CH:(j + 1) * SLAB_CH], b_ref[j], preferred_element_type=F32)
            br, bi = bu[:, :SLAB_ST], bu[:, SLAB_ST:]
            kr, ki = kr_ref[:, lanes], ki_ref[:, lanes]
            hr, hi, _, _ = _scan_tile(kr * br - ki * bi, kr * bi + ki * br, tb, lanes, reverse)
            hrb, hib = hr.astype(BF16), hi.astype(BF16)
            hr_ref[:, lanes] = hrb
            hi_ref[:, lanes] = hib
            y_ref[:, j * SLAB_CH:(j + 1) * SLAB_CH] = (
                jnp.dot(hrb, cr_ref[j], preferred_element_type=F32)
                - jnp.dot(hib, ci_ref[j], preferred_element_type=F32))

    whole3 = lambda arr: pl.BlockSpec(arr.shape, lambda s: (0, 0, 0))
    vec = pl.BlockSpec((1, NS), lambda s: (0, 0))
    return pl.pallas_call(
        body, name=name, grid=(nt,),
        in_specs=[pl.BlockSpec((tt, W), lambda s: (tile(s), 0)), whole3(bbd), whole3(cbd_re), whole3(cbd_im),
                  vec, vec, vec, vec],
        out_specs=[pl.BlockSpec((tt, NS), lambda s: (tile(s), 0)), pl.BlockSpec((tt, NS), lambda s: (tile(s), 0)),
                   pl.BlockSpec((tt, W), lambda s: (tile(s), 0))],
        out_shape=[jax.ShapeDtypeStruct((T, NS), BF16), jax.ShapeDtypeStruct((T, NS), BF16),
                   jax.ShapeDtypeStruct((T, W), F32)],
        scratch_shapes=_scan_scratch(tt, NS),
        compiler_params=_cparams())(u, bbd, cbd_re, cbd_im, lam_re, lam_im, coef_re, coef_im)


def _ssm_bwd(name, dy, h_re, h_im, u, bbd, bbdt_re, bbdt_im, cbdt_re, cbdt_im, lam_re, lam_im,
             coef_re, coef_im, Lc, reverse):
    T, W = u.shape
    nslab = W // SLAB_CH
    NS = nslab * SLAB_ST
    tt, nt, nc = _ssm_tiles(T, Lc)
    adj_reverse = not reverse
    if reverse:
        tile = lambda s: jnp.where(s < nt - nc, nc + s, s - (nt - nc))
    else:
        tile = lambda s: nt - 1 - s

    def body(dy_ref, hr_ref, hi_ref, u_ref, b_ref, btr_ref, bti_ref, ctr_ref, cti_ref, lr_ref, li_ref,
             kr_ref, ki_ref, du_ref, dlr_ref, dli_ref, dkr_ref, dki_ref, dbf_ref, dcrf_ref, dcif_ref,
             db_ref, dcr_ref, dci_ref, *tb):
        @pl.when(pl.program_id(0) == 0)
        def _():
            _scan_init(lr_ref[...], -li_ref[...], tb, adj_reverse)
            for ref in (dlr_ref, dli_ref, dkr_ref, dki_ref, db_ref, dcr_ref, dci_ref):
                ref[...] = jnp.zeros_like(ref)

        rows = lax.broadcasted_iota(jnp.int32, (tt, 1), 0)
        far_row = tt - 1 if adj_reverse else 0
        tn_dims = (((0,), (0,)), ((), ()))
        for j in range(nslab):
            lanes = slice(j * SLAB_ST, (j + 1) * SLAB_ST)
            chans = slice(j * SLAB_CH, (j + 1) * SLAB_CH)
            dys, us = dy_ref[:, chans], u_ref[:, chans]
            er = jnp.dot(dys, ctr_ref[j], preferred_element_type=F32)
            ei = -jnp.dot(dys, cti_ref[j], preferred_element_type=F32)
            ar, ai, car, cai = _scan_tile(er, ei, tb, lanes, adj_reverse)
            shift = tt - 1 if adj_reverse else 1
            nr = jnp.where(rows == far_row, car, pltpu.roll(ar, shift, 0))
            ni = jnp.where(rows == far_row, cai, pltpu.roll(ai, shift, 0))
            hrb, hib = hr_ref[:, lanes], hi_ref[:, lanes]
            hr, hi = hrb.astype(F32), hib.astype(F32)
            dlr_ref[:, lanes] += jnp.sum(nr * hr + ni * hi, axis=0, keepdims=True)
            dli_ref[:, lanes] += jnp.sum(ni * hr - nr * hi, axis=0, keepdims=True)
            bu = jnp.dot(us, b_ref[j], preferred_element_type=F32)
            br, bi = bu[:, :SLAB_ST], bu[:, SLAB_ST:]
            dkr_ref[:, lanes] += jnp.sum(ar * br + ai * bi, axis=0, keepdims=True)
            dki_ref[:, lanes] += jnp.sum(ai * br - ar * bi, axis=0, keepdims=True)
            kr, ki = kr_ref[:, lanes], ki_ref[:, lanes]
            dbr = (ar * kr + ai * ki).astype(BF16)
            dbi = (ai * kr - ar * ki).astype(BF16)
            du_ref[:, chans] = (jnp.dot(dbr, btr_ref[j], preferred_element_type=F32)
                                + jnp.dot(dbi, bti_ref[j], preferred_element_type=F32))
            db_ref[j, :, :SLAB_ST] += lax.dot_general(us, dbr, tn_dims, preferred_element_type=F32)
            db_ref[j, :, SLAB_ST:] += lax.dot_general(us, dbi, tn_dims, preferred_element_type=F32)
            dcr_ref[j] += lax.dot_general(hrb, dys, tn_dims, preferred_element_type=F32)
            dci_ref[j] -= lax.dot_general(hib, dys, tn_dims, preferred_element_type=F32)

        @pl.when(pl.program_id(0) == nt - 1)
        def _():
            def iota(shape, axis):
                return lax.broadcasted_iota(jnp.int32, shape, axis)

            sg, ss = SSM_GROUP.bit_length() - 1, SSM_STATE.bit_length() - 1
            b_mask = (iota((SLAB_CH, SLAB_ST), 0) >> sg) == (iota((SLAB_CH, SLAB_ST), 1) >> ss)
            c_mask = (iota((SLAB_ST, SLAB_CH), 0) >> ss) == (iota((SLAB_ST, SLAB_CH), 1) >> sg)
            fold = jnp.where((iota((SLAB_ST, SSM_STATE), 0) & (SSM_STATE - 1)) == iota((SLAB_ST, SSM_STATE), 1),
                             1.0, 0.0).astype(BF16)
            fold_t = jnp.where((iota((SSM_STATE, SLAB_ST), 1) & (SSM_STATE - 1)) == iota((SSM_STATE, SLAB_ST), 0),
                               1.0, 0.0).astype(BF16)

            def exact_dot(a, b, a_is_value):
                terms = _split3(a if a_is_value else b)
                acc = None
                for t in terms:
                    part = jnp.dot(t, b, preferred_element_type=F32) if a_is_value else jnp.dot(a, t, preferred_element_type=F32)
                    acc = part if acc is None else acc + part
                return acc

            for j in range(nslab):
                dbj = db_ref[j]
                dbf_ref[j, :, :SSM_STATE] = exact_dot(jnp.where(b_mask, dbj[:, :SLAB_ST], 0.0), fold, True)
                dbf_ref[j, :, SSM_STATE:] = exact_dot(jnp.where(b_mask, dbj[:, SLAB_ST:], 0.0), fold, True)
                dcrf_ref[j] = exact_dot(fold_t, jnp.where(c_mask, dcr_ref[j], 0.0), False)
                dcif_ref[j] = exact_dot(fold_t, jnp.where(c_mask, dci_ref[j], 0.0), False)

    whole3 = lambda arr: pl.BlockSpec(arr.shape, lambda s: (0, 0, 0))
    vec = pl.BlockSpec((1, NS), lambda s: (0, 0))
    row_w = pl.BlockSpec((tt, W), lambda s: (tile(s), 0))
    row_s = pl.BlockSpec((tt, NS), lambda s: (tile(s), 0))
    dbf = jax.ShapeDtypeStruct((nslab, SLAB_CH, 2 * SSM_STATE), F32)
    dcf = jax.ShapeDtypeStruct((nslab, SSM_STATE, SLAB_CH), F32)
    return pl.pallas_call(
        body, name=name, grid=(nt,),
        in_specs=[row_w, row_s, row_s, row_w, whole3(bbd), whole3(bbdt_re), whole3(bbdt_im), whole3(cbdt_re),
                  whole3(cbdt_im), vec, vec, vec, vec],
        out_specs=[row_w, vec, vec, vec, vec, whole3(dbf), whole3(dcf), whole3(dcf)],
        out_shape=[jax.ShapeDtypeStruct((T, W), F32)] + [jax.ShapeDtypeStruct((1, NS), F32)] * 4 + [dbf, dcf, dcf],
        scratch_shapes=[pltpu.VMEM(bbd.shape, F32), pltpu.VMEM(bbdt_re.shape, F32), pltpu.VMEM(bbdt_re.shape, F32)]
        + _scan_scratch(tt, NS),
        compiler_params=_cparams())(dy, h_re, h_im, u, bbd, bbdt_re, bbdt_im, cbdt_re, cbdt_im,
                                    lam_re, lam_im, coef_re, coef_im)


def _zoh_math(a_re, a_im, log_dt):
    dt = jnp.exp(log_dt)
    mag = jnp.exp(a_re * dt)
    lb_re = mag * jnp.cos(a_im * dt)
    lb_im = mag * jnp.sin(a_im * dt)
    den = a_re * a_re + a_im * a_im
    coef_re = ((lb_re - 1.0) * a_re + lb_im * a_im) / den
    coef_im = (lb_im * a_re - (lb_re - 1.0) * a_im) / den
    return lb_re, lb_im, coef_re, coef_im


def _zoh_fwd(a_re, a_im, log_dt):
    def body(ar, ai, ld, o0, o1, o2, o3):
        for ref, val in zip((o0, o1, o2, o3), _zoh_math(ar[...], ai[...], ld[...])):
            ref[...] = val

    return pl.pallas_call(body, name="zoh_fwd", out_shape=[jax.ShapeDtypeStruct(a_re.shape, F32)] * 4,
                          compiler_params=_cparams())(a_re, a_im, log_dt)


def _zoh_bwd(a_re, a_im, log_dt, cots):
    def body(ar, ai, ld, c0, c1, c2, c3, o0, o1, o2):
        _, vjp = jax.vjp(_zoh_math, ar[...], ai[...], ld[...])
        for ref, val in zip((o0, o1, o2), vjp((c0[...], c1[...], c2[...], c3[...]))):
            ref[...] = val

    return pl.pallas_call(
        body, name="zoh_bwd",
        out_shape=[jax.ShapeDtypeStruct(a_re.shape, F32), jax.ShapeDtypeStruct(a_re.shape, F32),
                   jax.ShapeDtypeStruct(log_dt.shape, F32)],
        compiler_params=_cparams())(a_re, a_im, log_dt, *cots)


def _outer_sum(acts, cots):
    D, N = acts.shape[1], cots.shape[1]
    tm, tn = _div(D, 512), _div(N, 1152)
    dims = (((0,), (0,)), ((), ()))

    def body(a_ref, b_ref, o_ref):
        a = a_ref[...]
        aa = _split3(a * _sigmoid(a))
        bb = _split3(b_ref[...])
        acc = None
        for ia in range(3):
            for ib in range(3 - ia):
                t = lax.dot_general(aa[ia], bb[ib], dims, preferred_element_type=F32)
                acc = t if acc is None else acc + t
        o_ref[...] = acc

    return pl.pallas_call(
        body, name="mod_dw", grid=(D // tm, N // tn),
        in_specs=[pl.BlockSpec((16, tm), lambda i, j: (0, i)), pl.BlockSpec((16, tn), lambda i, j: (0, j))],
        out_specs=pl.BlockSpec((tm, tn), lambda i, j: (i, j)),
        out_shape=jax.ShapeDtypeStruct((D, N), F32), compiler_params=_cparams())(acts, cots)


def _adamw_math(w, g, m, v):
    m = ADAM_B1 * m + (1.0 - ADAM_B1) * g
    v = ADAM_B2 * v + (1.0 - ADAM_B2) * (g * g)
    m_hat = m / (1.0 - ADAM_B1 ** ADAM_STEP)
    v_hat = v / (1.0 - ADAM_B2 ** ADAM_STEP)
    delta = -ADAM_LR * (m_hat / (jnp.sqrt(v_hat) + ADAM_EPS) + ADAM_WD * w)
    return delta, m, v


def _adamw(name, w, m, v, gparts):
    R, C = w.shape[-2:]
    kind = 'row1' if w.ndim == 3 else 'row'
    tr = _div(R, max(8, 524288 // C), mult=8)

    def fn(i, wv, mv, vv, *gs):
        g = gs[0]
        for extra in gs[1:]:
            g = g + extra
        return (g,) + _adamw_math(wv, g, mv, vv)

    return _rowk(name, fn, R, tr, [(w, kind), (m, kind), (v, kind)] + [(g, 'row') for g in gparts],
                 [(w.shape, F32, kind)] * 4)


def _adamw_whole(name, ws, ms, vs, gs):
    n = len(ws)

    def body(*refs):
        for k in range(n):
            g = refs[3 * n + k][...]
            res = (g,) + _adamw_math(refs[k][...], g, refs[n + k][...], refs[2 * n + k][...])
            for q in range(4):
                refs[4 * n + 4 * k + q][...] = res[q]

    out = pl.pallas_call(
        body, name=name, out_shape=[jax.ShapeDtypeStruct(w.shape, F32) for w in ws for _ in range(4)],
        compiler_params=_cparams())(*ws, *ms, *vs, *gs)
    return [tuple(out[4 * k:4 * k + 4]) for k in range(n)]


def _pack(pieces, rows_mult=8):
    flat = jnp.concatenate([p.reshape(-1).astype(F32) for p in pieces])
    unit = rows_mult * PACK_W
    total = -(-flat.shape[0] // unit) * unit
    return jnp.pad(flat, (0, total - flat.shape[0])).reshape(total // PACK_W, PACK_W)


def _unpack(buf, shapes):
    flat = buf.reshape(-1)
    out, off = [], 0
    for s in shapes:
        n = math.prod(s)
        out.append(flat[off:off + n].reshape(s))
        off += n
    return out


def _bd_expand(t):
    S, g, a, b = t.shape
    eye = jnp.eye(g, dtype=t.dtype)
    return (t[:, :, :, None, :] * eye[None, :, None, :, None]).reshape(S, g * a, g * b)


def _rope_tables(L, Lc):
    rows = L // GRID_W
    row_ids = jnp.broadcast_to(jnp.arange(rows)[:, None], (rows, GRID_W)).reshape(-1).astype(F32)
    col_ids = jnp.broadcast_to(jnp.arange(GRID_W)[None, :], (rows, GRID_W)).reshape(-1).astype(F32)
    quarter = HEAD_DIM // 4
    inv_freq = ROPE_THETA ** (-jnp.arange(quarter, dtype=F32) / quarter)
    ang_r = row_ids[:, None] * inv_freq
    ang_c = col_ids[:, None] * inv_freq
    cos = jnp.concatenate([jnp.cos(ang_r), jnp.cos(ang_r), jnp.cos(ang_c), jnp.cos(ang_c)], axis=1)
    sin = jnp.concatenate([-jnp.sin(ang_r), jnp.sin(ang_r), -jnp.sin(ang_c), jnp.sin(ang_c)], axis=1)
    cos = jnp.concatenate([jnp.ones((Lc, HEAD_DIM), F32), cos], axis=0)
    sin = jnp.concatenate([jnp.zeros((Lc, HEAD_DIM), F32), sin], axis=0)
    return cos, sin


def _rot(v):
    lane = lax.broadcasted_iota(jnp.int32, (1, HEAD_DIM), 1)
    first = (lane % (HEAD_DIM // 2)) < (HEAD_DIM // 4)
    return jnp.where(first, pltpu.roll(v, HEAD_DIM - HEAD_DIM // 4, 1), pltpu.roll(v, HEAD_DIM // 4, 1))


def _head_norm(xh, g):
    return xh * lax.rsqrt(jnp.mean(xh * xh, axis=-1, keepdims=True) + NORM_EPS) * g


def _norm_mod(xv, g, sh, sc):
    r = lax.rsqrt(jnp.mean(xv * xv, axis=-1, keepdims=True) + NORM_EPS)
    return (xv * r) * g * (1.0 + sc) + sh


def kernel(x, c, ctx, c_ctx, w_mod, b_mod, norm_g, w_ffn1_gate, w_ffn1_up, w_ffn1_down, w_in, q_norm_g, k_norm_g, ssm_a_re, ssm_a_im, ssm_log_dt, ssm_b_re, ssm_b_im, ssm_c_re, ssm_c_im, ssm_d, w_glu, b_glu, w_br_attn, w_br_ssm, w_out, w_ffn2_gate, w_ffn2_up, w_ffn2_down, loss_target, m_c_ctx, m_w_mod, m_b_mod, m_norm_g, m_w_ffn1_gate, m_w_ffn1_up, m_w_ffn1_down, m_w_in, m_q_norm_g, m_k_norm_g, m_ssm_a_re, m_ssm_a_im, m_ssm_log_dt, m_ssm_b_re, m_ssm_b_im, m_ssm_c_re, m_ssm_c_im, m_ssm_d, m_w_glu, m_b_glu, m_w_br_attn, m_w_br_ssm, m_w_out, m_w_ffn2_gate, m_w_ffn2_up, m_w_ffn2_down, v_c_ctx, v_w_mod, v_b_mod, v_norm_g, v_w_ffn1_gate, v_w_ffn1_up, v_w_ffn1_down, v_w_in, v_q_norm_g, v_k_norm_g, v_ssm_a_re, v_ssm_a_im, v_ssm_log_dt, v_ssm_b_re, v_ssm_b_im, v_ssm_c_re, v_ssm_c_im, v_ssm_d, v_w_glu, v_b_glu, v_w_br_attn, v_w_br_ssm, v_w_out, v_w_ffn2_gate, v_w_ffn2_up, v_w_ffn2_down):
    A = dict(locals())
    xi, yi, ci = _mesh_pos()
    chip = 2 * xi + yi
    me = 4 * xi + 2 * yi + ci
    L, D = x.shape[1], x.shape[2]
    Lc = ctx.shape[1]
    T = L + Lc
    F4 = w_ffn1_gate.shape[2]
    F = N_CHIPS * F4
    W, KV, Dq = D // 2, D // 4, D // 4
    G = W // SSM_GROUP
    P, E = SSM_STATE, SSM_GROUP
    NS = G * P
    nslab = W // SLAB_CH
    tr = min(256, Lc)
    ncr = Lc // tr
    assert L % tr == 0 and Lc % tr == 0 and W % SLAB_CH == 0 and D % (4 * LANES) == 0

    def sel(i, v):
        return v if v.shape[0] == 1 else jnp.where(i < ncr, v[0:1], v[1:2])

    def put(i, v, nrow):
        if nrow == 1:
            return v
        which = (i >= ncr).astype(jnp.int32)
        r2 = lax.broadcasted_iota(jnp.int32, (nrow, 1), 0)
        return jnp.where(r2 == which, jnp.broadcast_to(v, (nrow, v.shape[1])), 0.0)

    ident = lambda accs, rows, vecs, ri: [accs[0]]

    NM = w_mod.shape[2]
    first = jnp.zeros((8, D), F32).at[0].set(c[0]).at[1:4, :Dq].set(norm_g[0])
    g0 = _allgather_small("gather_c", first).reshape(N_CHIPS, 2, 8, D)
    c_all = g0[:, :, 0].reshape(N_DEV, D)
    ng = jnp.transpose(g0[:, 0, 1:4, :Dq], (1, 0, 2)).reshape(3, D)
    acts = jnp.concatenate([c_all, c_ctx[None], jnp.zeros((7, D), F32)], axis=0)
    wm = w_mod[0]
    b_shard = lax.dynamic_slice(b_mod[0], (chip * NM,), (NM,))[None]
    silu_bf = lambda a: (a * _sigmoid(a)).astype(BF16)
    to_bf = lambda b: b.astype(BF16)
    mod_part = _mm("mod_fwd", [(acts, wm, D)], 16, NM, tm=16, tn=_div(NM, 1152),
                   epi=lambda accs, rows, vecs, ri: [accs[0] + vecs[0]], outs=[(F32, False)],
                   vecs=[b_shard], a_pro=silu_bf, b_pro=to_bf)[0]
    mg = _allgather_small("gather_mod", mod_part).reshape(N_CHIPS, 2, 16, NM)[:, 0]
    mod_all = jnp.transpose(mg, (1, 0, 2)).reshape(16, N_CHIPS * NM)
    mod_x = lax.dynamic_slice(mod_all, (me, 0), (1, 9 * D))
    mod_c = jnp.where(jnp.arange(9 * D)[None] < 5 * D, mod_all[8:9], 0.0)
    modv = jnp.concatenate([mod_c, mod_x], axis=0)
    mv = lambda k: modv[:, k * D:(k + 1) * D]
    sh1, sc1, g1, sh2, sc2 = mv(0), mv(1), mv(2), mv(3), mv(4)
    g2, sh3, sc3, g3 = mv(5)[1:2], mv(6)[1:2], mv(7)[1:2], mv(8)[1:2]

    big = ['w_ffn1_gate', 'w_ffn1_up', 'w_ffn1_down', 'w_ffn2_gate', 'w_ffn2_up', 'w_ffn2_down',
           'w_in', 'w_glu', 'w_br_attn', 'w_br_ssm', 'w_out']
    row_sharded = {'w_ffn1_down', 'w_ffn2_down', 'w_glu', 'w_br_attn', 'w_out'}
    groups = [big[0:2], big[2:3], big[6:7], big[7:11], big[3:6]]
    chip_index = jnp.reshape(chip, (1,)).astype(jnp.int32)
    tok, gather_finish = modv, []
    pin = c
    for gi, names in enumerate(groups):
        tok, fin = _gather_split("gather_w%d" % gi, [_cast_slot("cast_" + n, A[n], chip_index, pin) for n in names], tok)
        gather_finish.append(fin)
        pin = tok
    ng = ng + tok[0:1, 0:1]
    Wt = {}

    def register(names, full):
        for n, gw in zip(names, full):
            Wt[n] = gw.reshape(N_CHIPS * gw.shape[1], gw.shape[2]) if n in row_sharded else gw

    def weights_ready(gi, after_work):
        _, lands = gather_finish[gi](after_work)
        register(groups[gi], _gather_finish("gather_w%d_pass" % gi, lands))

    def weights_pass(gi, after_work):
        _, lands = gather_finish[gi](after_work)
        tok_, fin_ = _pass_split("gather_w%d_pass" % gi, lands, after_work)
        return tok_, lambda later: register(groups[gi], fin_(later)[1])

    a_re2, a_im2 = ssm_a_re[0].reshape(2 * G, P), ssm_a_im[0].reshape(2 * G, P)
    ldt2 = ssm_log_dt[0].reshape(2 * G, 1)
    zoh = _zoh_fwd(a_re2, a_im2, ldt2)
    lam_re, lam_im, coef_re, coef_im = [[z[d * G:(d + 1) * G].reshape(1, NS) for d in range(2)] for z in zoh]
    bd_b = lambda b: _bd_expand(jnp.transpose(b, (0, 2, 1)).reshape(nslab, SLAB_GROUPS, E, P))
    bd_c = lambda cc: _bd_expand(jnp.transpose(cc, (0, 2, 1)).reshape(nslab, SLAB_GROUPS, P, E))
    bbd, bbdt_re, bbdt_im, cbd_re, cbd_im, cbdt_re, cbdt_im = [], [], [], [], [], [], []
    for d in range(2):
        br_, bi_ = bd_b(ssm_b_re[0, d]).astype(BF16), bd_b(ssm_b_im[0, d]).astype(BF16)
        cr_, ci_ = bd_c(ssm_c_re[0, d]).astype(BF16), bd_c(ssm_c_im[0, d]).astype(BF16)
        bbd.append(jnp.concatenate([br_, bi_], axis=2))
        bbdt_re.append(jnp.transpose(br_, (0, 2, 1)))
        bbdt_im.append(jnp.transpose(bi_, (0, 2, 1)))
        cbd_re.append(cr_)
        cbd_im.append(ci_)
        cbdt_re.append(jnp.transpose(cr_, (0, 2, 1)))
        cbdt_im.append(jnp.transpose(ci_, (0, 2, 1)))
    cos_t, sin_t = _rope_tables(L, Lc)
    qg, kg = q_norm_g, k_norm_g
    tiny = ['c_ctx', 'b_mod', 'norm_g', 'q_norm_g', 'k_norm_g', 'ssm_a_re', 'ssm_a_im', 'ssm_log_dt', 'ssm_d', 'b_glu']
    small = ['ssm_b_re', 'ssm_b_im', 'ssm_c_re', 'ssm_c_im']
    packs_wmv = [_pack([A[pre + n] for n in small]) for pre in ('', 'm_', 'v_')]
    prepared = packs_wmv + [cos_t, sin_t, coef_im[0], coef_im[1]] + [
        t[d][0] for t in (bbd, bbdt_re, bbdt_im, cbd_re, cbd_im, cbdt_re, cbdt_im) for d in range(2)]
    weights_ready(0, tok + sum(t[0:1, 0:1].astype(F32) for t in prepared))

    def norm_mod(name, xv, g, sh, sc):
        rows = xv.shape[0]
        return _rowk(name, lambda i, xt, gt, sht, sct: [_norm_mod(xt, gt, sel(i, sht), sel(i, sct))],
                     rows, tr, [(xv, 'row'), (g, 'vec'), (sh, 'vec'), (sc, 'vec')], [((rows, D), BF16, 'row')])[0]

    def swiglu_epi(accs, rows, vecs, ri):
        a_, b_ = accs
        return [a_, b_, a_ * _sigmoid(a_) * b_]

    def res_epi(coef):
        def epi(accs, rows, vecs, ri):
            gate = vecs[0]
            if gate.shape[0] == 2:
                gate = jnp.where(ri < Lc, gate[0:1], gate[1:2])
            return [accs[0], rows[0] + (coef * gate) * accs[0]]
        return epi

    def ffn_fwd(tag, h, xres, gate, down_ready=None):
        rows = h.shape[0]
        a_, b_, s_ = _mm(tag + "_up", [(h, Wt['w_' + tag + '_gate'], D), (h, Wt['w_' + tag + '_up'], D)], rows, F,
                         tm=_div(rows, 512), tn=F4, epi=swiglu_epi, outs=[(BF16, False), (BF16, False), (BF16, False)])
        if down_ready is not None:
            down_ready(s_)
        f_, xo = _mm(tag + "_down", [(s_, Wt['w_' + tag + '_down'], F)], rows, D, tm=_div(rows, 768),
                     tn=_div(D, 512), epi=res_epi(0.5), outs=[(F32, False), (F32, False)],
                     rows=[(xres, 0, 0)], vecs=[gate])
        return a_, b_, s_, f_, xo

    xc = jnp.concatenate([ctx[0], x[0]], axis=0)
    h1 = norm_mod("norm1", xc, ng[0:1], sh1, sc1)
    a1, b1, s1, f1, x1 = ffn_fwd("ffn1", h1, xc, g1, down_ready=lambda s_: weights_ready(1, s_))
    weights_ready(2, x1)
    h2 = norm_mod("norm2", x1, ng[1:2], sh2, sc2)
    proj = _mm("in_proj", [(h2, Wt['w_in'], D)], T, 4 * D, tm=_div(T, 768), tn=_div(D, 1024), epi=ident,
               outs=[(F32, False)])[0]
    nh, nkvh = D // HEAD_DIM, KV // HEAD_DIM

    def prep_fn(i, kt, vt, ut, qt, qgt, kgt, ct, st):
        qs = [_head_norm(qt[:, h * HEAD_DIM:(h + 1) * HEAD_DIM], qgt) for h in range(nh)]
        ks = [_head_norm(kt[:, h * HEAD_DIM:(h + 1) * HEAD_DIM], kgt) for h in range(nkvh)]
        qs = [v * ct + _rot(v) * st for v in qs]
        ks = [v * ct + _rot(v) * st for v in ks]
        return [jnp.concatenate(qs, axis=1), jnp.concatenate(ks, axis=1), vt, ut]

    qr, kr, vb, ub = _rowk(
        "qk_prep", prep_fn, T, tr,
        [(proj, ('col', KV, 0)), (proj, ('col', KV, 1)), (proj, ('col', W, 1)), (proj, ('col', D, 1)),
         (qg, 'vec'), (kg, 'vec'), (cos_t, 'row'), (sin_t, 'row')],
        [((T, D), BF16, 'row'), ((T, KV), BF16, 'row'), ((T, KV), BF16, 'row'), ((T, W), BF16, 'row')])
    _, mixer_weights = weights_pass(3, qr)
    attn = _attn_fwd(qr, kr, vb, L, Lc, D)
    hs_re, hs_im, ys = [], [], []
    lam_in = lam_re[0]
    for d in range(2):
        hr_, hi_, y_ = _ssm_fwd("ssm_fwd%d" % d, ub, bbd[d], cbd_re[d], cbd_im[d], lam_in, lam_im[d],
                                coef_re[d], coef_im[d], Lc, reverse=bool(d))
        hs_re.append(hr_)
        hs_im.append(hi_)
        ys.append(y_)
        if d == 0:
            tok_p4, ffn2_weights = weights_pass(4, y_)
            lam_in = lam_re[1] + tok_p4[0:1, 0:1]
    mixer_weights(ys[1])

    def ssm_out_fn(i, y0, y1, ut, dt):
        pre = dt * ut + y0 + y1
        yg_ = _gelu(pre)
        return [pre, yg_, yg_]

    ssm_pre, yg, ygb = _rowk(
        "ssm_out", ssm_out_fn, L, tr,
        [(ys[0], 'orow'), (ys[1], 'orow'), (proj, ('ocol', W, 1)), (ssm_d, 'vec')],
        [((L, W), F32, 'row'), ((L, W), F32, 'row'), ((L, W), BF16, 'row')], nc=ncr)

    def glu_epi(accs, rows, vecs, ri):
        z_ = accs[0] + vecs[0]
        return [z_, rows[0] * _sigmoid(z_)]

    zglu, y2 = _mm("glu", [(ygb, Wt['w_glu'], W)], L, W, tm=_div(L, 512), tn=_div(W, 512), epi=glu_epi,
                   outs=[(F32, False), (BF16, False)], rows=[(yg, 0, 0)], vecs=[b_glu])
    tnm = _div(Dq, 512)

    def merge_epi(accs, rows, vecs, ri):
        ga, gs = _sigmoid(rows[0]), _sigmoid(rows[1])
        return [accs[0], accs[1], ga * accs[0] + gs * accs[1]]

    ba, bs, merged = _mm("merge", [(attn, Wt['w_br_attn'], D), (y2, Wt['w_br_ssm'], W)], L, D, tm=tr, tn=tnm,
                         epi=merge_epi, outs=[(F32, False), (F32, False), (BF16, False)],
                         rows=[(proj, ncr, 2 * D // tnm), (proj, ncr, 3 * D // tnm)])
    mix, x2 = _mm("out_proj", [(merged, Wt['w_out'], D)], L, D, tm=tr, tn=_div(D, 1024), epi=res_epi(1.0),
                  outs=[(F32, False), (F32, False)], rows=[(x1, ncr, 0)], vecs=[g2])
    ffn2_weights(x2)
    h3 = norm_mod("norm3", x2, ng[2:3], sh3, sc3)
    a3, b3, s3, f3, x3 = ffn_fwd("ffn2", h3, x2, g3)

    def loss_fn(i, yt, tt_):
        diff = yt - tt_
        return [diff * (1.0 / D), jnp.sum(diff * diff, axis=0, keepdims=True)]

    dy, sq = _rowk("loss", loss_fn, L, tr, [(x3, 'row'), (loss_target[0], 'row')],
                   [((L, D), F32, 'row'), ((1, D), F32, 'acc')])
    loss = lax.psum(0.5 * jnp.sum(sq) / D, ("x", "y", "c"))

    def res_bwd(name, dxo, f_, gate, coef):
        rows, nrow = dxo.shape[0], gate.shape[0]

        def fn(i, dt, ft, gt):
            return [(coef * sel(i, gt)) * dt, put(i, jnp.sum(dt * ft, axis=0, keepdims=True) * coef, nrow)]

        return _rowk(name, fn, rows, tr, [(dxo, 'row'), (f_, 'row'), (gate, 'vec')],
                     [((rows, D), BF16, 'row'), ((nrow, D), F32, 'acc')])

    def swiglu_bwd_epi(accs, rows, vecs, ri):
        ds_, a_, b_ = accs[0], rows[0].astype(F32), rows[1].astype(F32)
        sg = _sigmoid(a_)
        return [ds_ * b_ * (sg * (1.0 + a_ * (1.0 - sg))), ds_ * (a_ * sg)]

    def norm_mod_bwd(name, xv, g, sh, sc, dh, dres, dres_kind):
        rows, nrow = xv.shape[0], sh.shape[0]

        def fn(i, xt, gt, sht, sct, dht, rest):
            _, vjp = jax.vjp(_norm_mod, xt, gt, sel(i, sht), sel(i, sct))
            dx_, dg_, dsh_, dsc_ = vjp(dht)
            dx_ = dx_ + (jnp.where(i >= ncr, rest, 0.0) if dres_kind == 'xrow' else rest)
            return [dx_, dg_, put(i, dsh_, nrow), put(i, dsc_, nrow)]

        return _rowk(name, fn, rows, tr,
                     [(xv, 'row'), (g, 'vec'), (sh, 'vec'), (sc, 'vec'), (dh, 'row'), (dres, dres_kind)],
                     [((rows, D), F32, 'row'), ((1, D), F32, 'acc'), ((nrow, D), F32, 'acc'), ((nrow, D), F32, 'acc')],
                     nc=ncr)

    def ffn_bwd(tag, dxo, h, a_, b_, s_, f_, gate, wg, wu, wd, on_dwd=None):
        rows = dxo.shape[0]
        df, dgate = res_bwd(tag + "_dres", dxo, f_, gate, 0.5)
        dwd = _mm(tag + "_dwd", [(s_, df, rows)], F, D, tm=_div(F, 512), tn=_div(D, 1024), ta=True, epi=ident,
                  outs=[(BF16, False)])[0].reshape(N_CHIPS, F4, D)
        if on_dwd is not None:
            on_dwd(dwd)
        da, db = _mm(tag + "_dact", [(df, wd, D)], rows, F, tm=_div(rows, 512), tn=F4, tb=True, epi=swiglu_bwd_epi,
                     outs=[(BF16, False), (BF16, False)], rows=[(a_, 0, 0), (b_, 0, 0)])
        dwg = _mm(tag + "_dwg", [(h, da, rows)], D, F, tm=_div(D, 512), tn=F4, ta=True, epi=ident,
                  outs=[(BF16, True)])[0]
        dwu = _mm(tag + "_dwu", [(h, db, rows)], D, F, tm=_div(D, 512), tn=F4, ta=True, epi=ident,
                  outs=[(BF16, True)])[0]
        dh = _mm(tag + "_dh", [(da, wg, F), (db, wu, F)], rows, D, tm=_div(rows, 768), tn=_div(D, 1024), nk=N_CHIPS,
                 tb=True, epi=ident, outs=[(F32, False)], summed=True)[0]
        return dh, dgate, dwg, dwu, dwd

    dh3, dg3, dwg2, dwu2, dwd2 = ffn_bwd("ffn2", dy, h3, a3, b3, s3, f3, g3, Wt['w_ffn2_gate'], Wt['w_ffn2_up'],
                                         Wt['w_ffn2_down'])
    tok_r1, scatter_fin1 = _scatter_split("scatter_ffn2", [dwg2, dwu2, dwd2], dg3)
    dx2, dng3, dsh3, dsc3 = norm_mod_bwd("norm3_bwd", x2, ng[2:3], sh3, sc3, dh3, dy, 'row')
    dmix, dg2 = res_bwd("mix_dres", dx2, mix, g2 + tok_r1[0:1, 0:1], 1.0)

    def dmerge_epi(accs, rows, vecs, ri):
        dm_, ba_, bs_ = accs[0], rows[0], rows[1]
        ga, gs = _sigmoid(rows[2]), _sigmoid(rows[3])
        return [dm_ * ga, dm_ * gs, dm_ * ba_ * ga * (1.0 - ga), dm_ * bs_ * gs * (1.0 - gs)]

    tnd = _div(D, 1024)
    dba, dbs, dga, dgs = _mm("dmerge", [(dmix, Wt['w_out'], D)], L, D, tm=tr, tn=tnd, tb=True, epi=dmerge_epi,
                             outs=[(BF16, False)] * 4,
                             rows=[(ba, 0, 0), (bs, 0, 0), (proj, ncr, 2 * D // tnd), (proj, ncr, 3 * D // tnd)])
    dwout = _mm("dw_out", [(merged, dmix, L)], D, D, tm=_div(D, 512), tn=_div(D, 1024), ta=True, epi=ident,
                outs=[(BF16, False)])[0].reshape(N_CHIPS, Dq, D)
    dattn = _mm("dattn", [(dba, Wt['w_br_attn'], D)], L, D, tm=_div(L, 512), tn=_div(D, 1024), tb=True, epi=ident,
                outs=[(BF16, False)])[0]
    dwba = _mm("dw_br_attn", [(attn, dba, L)], D, D, tm=_div(D, 512), tn=_div(D, 1024), ta=True, epi=ident,
               outs=[(BF16, False)])[0].reshape(N_CHIPS, Dq, D)
    dy2 = _mm("dy2", [(dbs, Wt['w_br_ssm'], D)], L, W, tm=_div(L, 512), tn=_div(W, 1024), nk=N_CHIPS, tb=True,
              epi=ident, outs=[(F32, False)])[0]
    dwbs = _mm("dw_br_ssm", [(y2, dbs, L)], W, D, tm=_div(W, 512), tn=_div(Dq, 512), ta=True, epi=ident,
               outs=[(BF16, True)])[0]

    def glu_bwd_fn(i, d2, ygt, zt):
        sz = _sigmoid(zt)
        dz_ = d2 * ygt * sz * (1.0 - sz)
        return [dz_, d2 * sz, jnp.sum(dz_, axis=0, keepdims=True)]

    dz, dyd, dbglu = _rowk("glu_bwd", glu_bwd_fn, L, tr, [(dy2, 'row'), (yg, 'row'), (zglu, 'row')],
                           [((L, W), BF16, 'row'), ((L, W), F32, 'row'), ((1, W), F32, 'acc')])

    def dssm_epi(accs, rows, vecs, ri):
        _, vjp = jax.vjp(_gelu, rows[1])
        ds_ = vjp(accs[0] + rows[0])[0]
        return [ds_, ds_]

    dssm, dssm_b = _mm("dssm", [(dz, Wt['w_glu'], W)], L, W, tm=_div(L, 512), tn=_div(W, 512), tb=True, epi=dssm_epi,
                       outs=[(F32, False), (BF16, False)], rows=[(dyd, 0, 0), (ssm_pre, 0, 0)])
    dwglu = _mm("dw_glu", [(ygb, dz, L)], W, W, tm=_div(W, 512), tn=_div(W, 1024), ta=True, epi=ident,
                outs=[(BF16, False)])[0].reshape(N_CHIPS, W // N_CHIPS, W)
    tok_r2a, scatter_fin2a = _scatter_split("scatter_mix", [dwglu, dwba, dwbs, dwout], dbglu)
    dssm_full = jnp.concatenate([jnp.zeros((Lc, W), BF16), dssm_b], axis=0)
    dus, dlam_re, dlam_im, dcoef_re, dcoef_im, dbf, dcf_re, dcf_im = [], [], [], [], [], [], [], []
    for d in range(2):
        r = _ssm_bwd("ssm_bwd%d" % d, dssm_full, hs_re[d], hs_im[d], ub, bbd[d], bbdt_re[d], bbdt_im[d],
                     cbdt_re[d], cbdt_im[d], lam_re[d] + tok_r2a[0:1, 0:1], lam_im[d], coef_re[d], coef_im[d], Lc,
                     reverse=bool(d))
        for lst, val in zip((dus, dlam_re, dlam_im, dcoef_re, dcoef_im, dbf, dcf_re, dcf_im), r):
            lst.append(val)
    dqr, dkr, dvf = _attn_bwd(qr, kr, vb, dattn, L, Lc, D)

    def prep_bwd_fn(i, qt, kt, ut, dqt, dkt, dvt, du0, du1, dst, dgat, dgst, dt, qgt, kgt, ct, st):
        live = i >= ncr
        dqt = jnp.where(live, dqt, 0.0)
        dst = jnp.where(live, dst, 0.0)
        dgat = jnp.where(live, dgat, jnp.zeros_like(dgat))
        dgst = jnp.where(live, dgst, jnp.zeros_like(dgst))
        dqs, dks = [], []
        dqg_ = jnp.zeros((1, HEAD_DIM), F32)
        dkg_ = jnp.zeros((1, HEAD_DIM), F32)
        for h in range(nh):
            hl = slice(h * HEAD_DIM, (h + 1) * HEAD_DIM)
            dn = dqt[:, hl] * ct + _rot(dqt[:, hl] * st)
            _, vjp = jax.vjp(_head_norm, qt[:, hl], qgt)
            dxh, dgh = vjp(dn)
            dqs.append(dxh)
            dqg_ = dqg_ + dgh
        for h in range(nkvh):
            hl = slice(h * HEAD_DIM, (h + 1) * HEAD_DIM)
            dn = dkt[:, hl] * ct + _rot(dkt[:, hl] * st)
            _, vjp = jax.vjp(_head_norm, kt[:, hl], kgt)
            dxh, dgh = vjp(dn)
            dks.append(dxh)
            dkg_ = dkg_ + dgh
        du_ = du0 + du1 + dst * dt
        dproj_ = jnp.concatenate([c_.astype(BF16) for c_ in dks + [dvt, du_] + dqs + [dgat, dgst]], axis=1)
        return [dproj_, dqg_, dkg_, jnp.sum(dst * ut, axis=0, keepdims=True)]

    dproj, dqg, dkg, dssd = _rowk(
        "qk_prep_bwd", prep_bwd_fn, T, tr,
        [(proj, ('col', D, 1)), (proj, ('col', KV, 0)), (proj, ('col', W, 1)), (dqr, 'xrow'), (dkr, 'row'),
         (dvf, 'row'), (dus[0], 'row'), (dus[1], 'row'), (dssm, 'xrow'), (dga, 'xrow'), (dgs, 'xrow'), (ssm_d, 'vec'),
         (qg, 'vec'), (kg, 'vec'), (cos_t, 'row'), (sin_t, 'row')],
        [((T, 4 * D), BF16, 'row'), ((1, HEAD_DIM), F32, 'acc'), ((1, HEAD_DIM), F32, 'acc'), ((1, W), F32, 'acc')],
        nc=ncr)
    dh2 = _mm("in_proj_dx", [(dproj, Wt['w_in'], 4 * D)], T, D, tm=_div(T, 768), tn=_div(D, 1024), nk=N_CHIPS, tb=True,
              epi=ident, outs=[(F32, False)])[0]
    dwin = _mm("in_proj_dw", [(h2, dproj, T)], D, 4 * D, tm=_div(D, 512), tn=_div(D, 1024), ta=True, epi=ident,
               outs=[(BF16, True)])[0]
    tok_r2, scatter_fin2 = _scatter_split("scatter_w_in", [dwin], dqg)
    dx1, dng2, dsh2, dsc2 = norm_mod_bwd("norm2_bwd", x1, ng[1:2] + tok_r2[0:1, 0:1], sh2, sc2, dh2, dx2, 'xrow')
    early = {}

    def start_down(dwd):
        early['tok'], early['fin'] = _scatter_split("scatter_ffn1_down", [dwd], dg2)

    dh1, dg1, dwg1, dwu1, dwd1 = ffn_bwd("ffn1", dx1, h1, a1, b1, s1, f1, g1, Wt['w_ffn1_gate'], Wt['w_ffn1_up'],
                                         Wt['w_ffn1_down'], on_dwd=start_down)
    dx0, dng1, dsh1, dsc1 = norm_mod_bwd("norm1_bwd", xc, ng[0:1] + early['tok'][0:1, 0:1], sh1, sc1, dh1, dx1, 'row')
    grad_x = dx0[Lc:][None]

    zD = jnp.zeros((1, D), F32)
    dmod_x = jnp.concatenate([dsh1[1:2], dsc1[1:2], dg1[1:2], dsh2[1:2], dsc2[1:2], dg2, dsh3, dsc3, dg3], axis=1)
    dmod_c = jnp.concatenate([dsh1[0:1], dsc1[0:1], dg1[0:1], dsh2[0:1], dsc2[0:1], zD, zD, zD, zD], axis=1)
    pieces = [dmod_x, dmod_c, dng1, dng2, dng3, dqg, dkg] + dlam_re + dlam_im + dcoef_re + dcoef_im \
        + dbf + dcf_re + dcf_im + [dssd, dbglu]
    shapes = [p_.shape for p_ in pieces]
    pack = _pack(pieces)
    RP = pack.shape[0]
    tok_small, small_gathered = _allgather_split("gather_small", pack, me, dng1)
    tok_r3, scatter_fin3 = _scatter_split("scatter_ffn1_up", [dwg1, dwu1], tok_small)
    results = {}

    def sum_group(tag, names, fin, after_work):
        sent, landed = fin(after_work)
        plane = [_sum_plane("sum_" + n, g_, rb, chip_index) for n, g_, rb in zip(names, sent, landed)]
        tok_, swapped = _swap_split("swap_" + tag, plane, chip_index)
        return tok_, (names, swapped)

    def update_group(group, after_work):
        names, swapped = group
        mine, theirs = swapped(after_work)
        for n, m_, t_ in zip(names, mine, theirs):
            results[n] = _adamw("adamw_" + n, A[n], A['m_' + n], A['v_' + n], [m_, t_])

    tok_a, grp_ffn2 = sum_group("ffn2", big[3:6], scatter_fin1, tok_r3)
    tok_b, grp_mix = sum_group("mix", big[7:11], scatter_fin2a, tok_a)
    tok_c, grp_w_in = sum_group("w_in", big[6:7], scatter_fin2, tok_b)
    update_group(grp_ffn2, tok_c)
    tok_d, grp_down = sum_group("ffn1_down", big[2:3], early['fin'], results['w_ffn2_down'][0])
    update_group(grp_mix, tok_d)
    update_group(grp_w_in, results['w_out'][0])
    update_group(grp_down, results['w_in'][0])
    allp = small_gathered(results['w_ffn1_down'][0])
    head_rows = -(-18 * D // PACK_W)
    head = allp[:, :head_rows].reshape(N_DEV, head_rows * PACK_W)
    dmx_all = head[:, :9 * D]

    def sum_rows_fn(i, t):
        s_ = t[0:1]
        for k in range(1, N_DEV):
            s_ = s_ + t[k:k + 1]
        return [s_]

    dmc_sum = _rowk("sum_dmod_c", sum_rows_fn, 1, 1, [(head[:, 9 * D:18 * D], 'vec')], [((1, 9 * D), F32, 'row')])[0]
    cots = jnp.concatenate([dmx_all, dmc_sum, jnp.zeros((7, 9 * D), F32)], axis=0)
    cots_sh = lax.dynamic_slice(cots, (0, chip * NM), (16, NM))
    part = _mm("cctx_part", [(cots_sh[8:16], wm, NM)], 8, D, tm=8, tn=_div(D, 1024), nk=NM // _div(NM, 1152), tb=True,
               epi=ident, outs=[(F32, False)], a_pro=to_bf, b_pro=to_bf)[0]
    _, cctx_gathered = _allgather_split("gather_cctx", part, me, part)

    def sum_dev_fn(i, t):
        s_ = t[0]
        for k in range(1, N_DEV):
            s_ = s_ + t[k]
        return [s_]

    tot = _rowk("sum_small", sum_dev_fn, RP, 8, [(allp, 'row3')], [((RP, PACK_W), F32, 'row')])[0]
    (t_dmod_x, t_dmod_c, t_ng1, t_ng2, t_ng3, t_qg, t_kg, t_lr0, t_lr1, t_li0, t_li1, t_kr0, t_kr1, t_ki0, t_ki1,
     t_dbf0, t_dbf1, t_dcr0, t_dcr1, t_dci0, t_dci1, t_d, t_bglu) = _unpack(tot, shapes)
    b_grad = lambda t, lo: jnp.transpose(t[:, :, lo:lo + P].reshape(G, E, P), (0, 2, 1))
    c_grad = lambda t: jnp.transpose(t.reshape(nslab, P, SLAB_GROUPS, E), (0, 2, 3, 1)).reshape(G, E, P)
    cat2 = lambda u0, u1: jnp.concatenate([u0.reshape(G, P), u1.reshape(G, P)], axis=0)
    g_are, g_aim, g_ldt = _zoh_bwd(a_re2, a_im2, ldt2, [cat2(t_lr0, t_lr1), cat2(t_li0, t_li1),
                                                         cat2(t_kr0, t_kr1), cat2(t_ki0, t_ki1)])
    g_bmod = _rowk("bmod_grad", lambda i, u0, u1: [u0 + u1], 1, 1, [(t_dmod_x, 'row'), (t_dmod_c, 'row')],
                   [((1, 9 * D), F32, 'row')])[0]
    g_wmod = _outer_sum(acts, cots_sh)
    results['w_mod'] = _adamw("adamw_w_mod", w_mod, m_w_mod, v_w_mod, [g_wmod])
    done = sum(results[n][1].reshape(-1, results[n][1].shape[-1])[0:1, 0:1] for n in list(results)) + g_are[0:1, 0:1] \
        + g_bmod[0:1, 0:1]
    tok_e, grp_up = sum_group("ffn1_up", big[0:2], scatter_fin3, done)
    parts = cctx_gathered(tok_e).reshape(N_CHIPS, 2, 8, D)[:, 0, 0]

    def cctx_fn(i, pt, ct):
        ds_ = ((pt[0:1] + pt[1:2]) + pt[2:3]) + pt[3:4]
        _, vjp = jax.vjp(lambda v: v * _sigmoid(v), ct)
        return [vjp(ds_)[0]]

    g_cctx = _rowk("cctx_grad", cctx_fn, 1, 1, [(parts, 'vec'), (c_ctx[None], 'row')], [((1, D), F32, 'row')])[0]

    ng_full = jnp.concatenate([t_ng1, t_ng2, t_ng3], axis=0)
    gsmall = {
        'c_ctx': g_cctx, 'b_mod': g_bmod, 'norm_g': lax.dynamic_slice(ng_full, (0, chip * Dq), (3, Dq)),
        'q_norm_g': t_qg, 'k_norm_g': t_kg, 'ssm_a_re': g_are, 'ssm_a_im': g_aim, 'ssm_log_dt': g_ldt,
        'ssm_b_re': jnp.stack([b_grad(t_dbf0, 0), b_grad(t_dbf1, 0)]),
        'ssm_b_im': jnp.stack([b_grad(t_dbf0, P), b_grad(t_dbf1, P)]),
        'ssm_c_re': jnp.stack([c_grad(t_dcr0), c_grad(t_dcr1)]), 'ssm_c_im': jnp.stack([c_grad(t_dci0), c_grad(t_dci1)]),
        'ssm_d': t_d, 'b_glu': t_bglu}
    sshapes = [A[n].shape for n in small]
    sres = _adamw("adamw_small", packs_wmv[0], packs_wmv[1], packs_wmv[2], [_pack([gsmall[n] for n in small])])
    update_group(grp_up, sres[0])
    sres = [_unpack(b_, sshapes) for b_ in sres]
    for k, n in enumerate(small):
        results[n] = tuple(sres[q][k] for q in range(4))
    as2d = lambda v: v.reshape(1, -1) if v.ndim == 1 else v
    tres = _adamw_whole("adamw_tiny", [as2d(A[n]) for n in tiny], [as2d(A['m_' + n]) for n in tiny],
                        [as2d(A['v_' + n]) for n in tiny], [gsmall[n].reshape(as2d(A[n]).shape) for n in tiny])
    for n, res in zip(tiny, tres):
        results[n] = res

    order = ['c_ctx', 'w_mod', 'b_mod', 'norm_g', 'w_ffn1_gate', 'w_ffn1_up', 'w_ffn1_down', 'w_in', 'q_norm_g',
             'k_norm_g', 'ssm_a_re', 'ssm_a_im', 'ssm_log_dt', 'ssm_b_re', 'ssm_b_im', 'ssm_c_re', 'ssm_c_im',
             'ssm_d', 'w_glu', 'b_glu', 'w_br_attn', 'w_br_ssm', 'w_out', 'w_ffn2_gate', 'w_ffn2_up', 'w_ffn2_down']
    outs = [loss, grad_x]
    for q in range(4):
        outs += [results[n][q].reshape(A[n].shape) for n in order]
    return tuple(outs)
```

```python
import math

import jax
import jax.numpy as jnp
from jax import lax
from jax.experimental import pallas as pl
from jax.experimental.pallas import tpu as pltpu

F32 = jnp.float32
BF16 = jnp.bfloat16
MESH = pl.DeviceIdType.MESH

NORM_EPS = 1e-6
ROPE_THETA = 10000.0
GRID_W = 64
HEAD_DIM = 128
Q_PER_KV = 4
SSM_GROUP = 16
SSM_STATE = 64
ADAM_LR = 0.001
ADAM_B1 = 0.9
ADAM_B2 = 0.999
ADAM_EPS = 1e-08
ADAM_WD = 0.01
ADAM_STEP = 10

N_CHIPS = 4
N_DEV = 8
LANES = 128
SLAB_CH = 128
SLAB_GROUPS = SLAB_CH // SSM_GROUP
SLAB_ST = SLAB_GROUPS * SSM_STATE
VMEM_LIMIT_BYTES = 56 * 1024 * 1024
PACK_W = 1024


def _cparams(**kw):
    return pltpu.CompilerParams(vmem_limit_bytes=VMEM_LIMIT_BYTES, **kw)


def _div(n, pref, mult=LANES):
    t = (min(pref, n) // mult) * mult
    while t >= mult:
        if n % t == 0:
            return t
        t -= mult
    return n


def _sigmoid(x):
    return jax.nn.sigmoid(x)


def _gelu(x):
    return x * (0.5 * (1.0 + jnp.tanh(math.sqrt(2.0 / math.pi) * (x + 0.044715 * (x * x * x)))))


def _rowk(name, fn, nrows, tr, ins, outs, nc=0):
    nt = nrows // tr
    in_specs, arrays = [], []
    for arr, kind in ins:
        arrays.append(arr)
        if kind == 'row':
            in_specs.append(pl.BlockSpec((tr, arr.shape[1]), lambda i: (i, 0)))
        elif kind == 'xrow':
            in_specs.append(pl.BlockSpec((tr, arr.shape[1]), lambda i: (jnp.maximum(i - nc, 0), 0)))
        elif kind == 'orow':
            in_specs.append(pl.BlockSpec((tr, arr.shape[1]), lambda i: (i + nc, 0)))
        elif kind == 'vec':
            in_specs.append(pl.BlockSpec(arr.shape, lambda i, nd=arr.ndim: (0,) * nd))
        elif kind == 'row3':
            in_specs.append(pl.BlockSpec((arr.shape[0], tr, arr.shape[2]), lambda i: (0, i, 0)))
        elif kind == 'row1':
            in_specs.append(pl.BlockSpec((None, tr, arr.shape[2]), lambda i: (0, i, 0)))
        elif kind[0] == 'ocol':
            _, width, blk = kind
            in_specs.append(pl.BlockSpec((tr, width), lambda i, blk=blk: (i + nc, blk)))
        else:
            _, width, blk = kind
            in_specs.append(pl.BlockSpec((tr, width), lambda i, blk=blk: (i, blk)))
    out_shape, out_specs = [], []
    for shape, dtype, kind in outs:
        out_shape.append(jax.ShapeDtypeStruct(shape, dtype))
        if kind == 'row':
            out_specs.append(pl.BlockSpec((tr, shape[1]), lambda i: (i, 0)))
        elif kind == 'row1':
            out_specs.append(pl.BlockSpec((None, tr, shape[2]), lambda i: (0, i, 0)))
        else:
            out_specs.append(pl.BlockSpec(shape, lambda i, nd=len(shape): (0,) * nd))
    nin = len(ins)

    def body(*refs):
        i = pl.program_id(0)
        res = fn(i, *[r[...] for r in refs[:nin]])
        for (shape, dtype, kind), ref, val in zip(outs, refs[nin:], res):
            if kind in ('row', 'row1'):
                ref[...] = val.astype(dtype)
            else:
                @pl.when(i == 0)
                def _():
                    ref[...] = val.astype(dtype)

                @pl.when(i > 0)
                def _():
                    ref[...] += val.astype(dtype)

    return pl.pallas_call(body, name=name, grid=(nt,), in_specs=in_specs, out_specs=out_specs,
                          out_shape=out_shape, compiler_params=_cparams())(*arrays)


def _mm(name, pairs, M, N, *, tm, tn, nk=1, epi, outs, ta=False, tb=False, rows=(), vecs=(),
        a_pro=None, b_pro=None, n_outer=True, summed=False):
    nm, nn = M // tm, N // tn
    npair = len(pairs)

    def idx(f):
        if n_outer:
            return lambda j, i, k: f(i, j, k)
        return lambda i, j, k: f(i, j, k)

    in_specs, args = [], []
    for a, b, K in pairs:
        tk = K // nk
        if ta:
            in_specs.append(pl.BlockSpec((tk, tm), idx(lambda i, j, k: (k, i))))
        else:
            in_specs.append(pl.BlockSpec((tm, tk), idx(lambda i, j, k: (i, k))))
        args.append(a)
        if b.ndim == 3:
            if tb:
                per = b.shape[2] // tk
                in_specs.append(pl.BlockSpec((None, tn, tk), idx(lambda i, j, k, per=per: (k // per, j, k % per))))
            else:
                per = b.shape[2] // tn
                in_specs.append(pl.BlockSpec((None, tk, tn), idx(lambda i, j, k, per=per: (j // per, k, j % per))))
        elif tb:
            in_specs.append(pl.BlockSpec((tn, tk), idx(lambda i, j, k: (j, k))))
        else:
            in_specs.append(pl.BlockSpec((tk, tn), idx(lambda i, j, k: (k, j))))
        args.append(b)
    for arr, ro, co in rows:
        in_specs.append(pl.BlockSpec((tm, tn), idx(lambda i, j, k, ro=ro, co=co: (i + ro, j + co))))
        args.append(arr)
    for arr in vecs:
        in_specs.append(pl.BlockSpec((arr.shape[0], tn), idx(lambda i, j, k: (0, j))))
        args.append(arr)
    out_shape, out_specs = [], []
    for dtype, chunked in outs:
        if chunked:
            per = (N // N_CHIPS) // tn
            out_shape.append(jax.ShapeDtypeStruct((N_CHIPS, M, N // N_CHIPS), dtype))
            out_specs.append(pl.BlockSpec((None, tm, tn), idx(lambda i, j, k, per=per: (j // per, i, j % per))))
        else:
            out_shape.append(jax.ShapeDtypeStruct((M, N), dtype))
            out_specs.append(pl.BlockSpec((tm, tn), idx(lambda i, j, k: (i, j))))
    nacc = 1 if summed else npair
    scratch = [pltpu.VMEM((tm, tn), F32) for _ in range(nacc)] if nk > 1 else []
    nrow, nvec, nout = len(rows), len(vecs), len(outs)
    dims = (((0 if ta else 1,), (1 if tb else 0,)), ((), ()))

    def body(*refs):
        ab = refs[:2 * npair]
        row_refs = refs[2 * npair:2 * npair + nrow]
        vec_refs = refs[2 * npair + nrow:2 * npair + nrow + nvec]
        out_refs = refs[2 * npair + nrow + nvec:2 * npair + nrow + nvec + nout]
        acc_refs = refs[2 * npair + nrow + nvec + nout:]
        if n_outer:
            j, i, k = pl.program_id(0), pl.program_id(1), pl.program_id(2)
        else:
            i, j, k = pl.program_id(0), pl.program_id(1), pl.program_id(2)

        def part(p):
            av, bv = ab[2 * p][...], ab[2 * p + 1][...]
            if a_pro is not None:
                av = a_pro(av)
            if b_pro is not None:
                bv = b_pro(bv)
            return lax.dot_general(av, bv, dims, preferred_element_type=F32)

        def finish(accs):
            row_index = i * tm + lax.broadcasted_iota(jnp.int32, (tm, 1), 0)
            res = epi(accs, [r[...] for r in row_refs], [v[...] for v in vec_refs], row_index)
            for ref, val in zip(out_refs, res):
                ref[...] = val.astype(ref.dtype)

        parts = [part(p) for p in range(npair)]
        if summed:
            total = parts[0]
            for extra in parts[1:]:
                total = total + extra
            parts = [total]
        if nk == 1:
            finish(parts)
        else:
            @pl.when(k == 0)
            def _():
                for q in range(nacc):
                    acc_refs[q][...] = parts[q]

            @pl.when(jnp.logical_and(k > 0, k < nk - 1))
            def _():
                for q in range(nacc):
                    acc_refs[q][...] += parts[q]

            @pl.when(k == nk - 1)
            def _():
                finish([acc_refs[q][...] + parts[q] for q in range(nacc)])

    grid = (nn, nm, nk) if n_outer else (nm, nn, nk)
    return pl.pallas_call(body, name=name, grid=grid, in_specs=in_specs, out_specs=out_specs,
                          out_shape=out_shape, scratch_shapes=scratch, compiler_params=_cparams())(*args)


def _split3(v):
    v0 = v.astype(BF16)
    r1 = v - v0.astype(F32)
    v1 = r1.astype(BF16)
    v2 = (r1 - v1.astype(F32)).astype(BF16)
    return v0, v1, v2


def _mesh_pos():
    return lax.axis_index("x"), lax.axis_index("y"), lax.axis_index("c")


def _allgather_small(name, x):
    m, n = x.shape

    def body(x_ref, out_ref, send_sems, recv_sems, local_sem):
        xi, yi, ci = _mesh_pos()
        me, sibling = (xi, yi, ci), (xi, yi, 1 - ci)
        chips = [(1 - xi, yi), (xi, 1 - yi), (1 - xi, 1 - yi)]

        def rows(px, py, pc):
            return out_ref.at[pl.ds((4 * px + 2 * py + pc) * m, m), :]

        def copy(k, block, to, src=None):
            return pltpu.make_async_remote_copy(
                src_ref=rows(*block) if src is None else src, dst_ref=rows(*block),
                send_sem=send_sems.at[k], recv_sem=recv_sems.at[k], device_id=to, device_id_type=MESH)

        mine = pltpu.make_async_copy(x_ref, rows(*me), local_sem)
        mine.start()
        first = [copy(0, me, sibling, src=x_ref)]
        first += [copy(1 + j, me, (*chip, ci), src=x_ref) for j, chip in enumerate(chips)]
        for cp in first:
            cp.start()
        passed = [copy(4 + j, (*chip, ci), sibling) for j, chip in enumerate(chips)]
        for j, chip in enumerate(chips):
            copy(1 + j, (*chip, ci), me).wait_recv()
            passed[j].start()
        copy(0, sibling, me).wait_recv()
        for j, chip in enumerate(chips):
            copy(4 + j, (*chip, 1 - ci), me).wait_recv()
        for cp in first + passed:
            cp.wait_send()
        mine.wait()

    return pl.pallas_call(
        body, name=name, out_shape=jax.ShapeDtypeStruct((N_DEV * m, n), x.dtype),
        in_specs=[pl.BlockSpec(memory_space=pltpu.VMEM)], out_specs=pl.BlockSpec(memory_space=pltpu.VMEM),
        scratch_shapes=[pltpu.SemaphoreType.DMA((7,)), pltpu.SemaphoreType.DMA((7,)), pltpu.SemaphoreType.DMA],
        compiler_params=_cparams())(x)


_HBM = pl.BlockSpec(memory_space=pltpu.HBM)
_SEM = pl.BlockSpec(memory_space=pltpu.SEMAPHORE)
_ANY = pl.BlockSpec(memory_space=pl.ANY)
_EFFECT = pltpu.SideEffectType.DATAFLOW_SIDE_EFFECTING


def _in_hbm(v):
    return pltpu.with_memory_space_constraint(v, pltpu.HBM)


def _other_chips(xi, yi):
    return [(1 - xi, yi), (xi, 1 - yi), (1 - xi, 1 - yi)]


def _guarded(core, fn):
    if core is None:
        fn()
    else:
        pl.when(lax.axis_index("c") == core)(fn)


def _split_copies(name, srcs, lands, after, pairs, senders, receivers, ncopy):
    ns, nl = len(srcs), len(lands)
    dma = pltpu.SemaphoreType.DMA((ncopy,))
    thru = [pltpu.HBM(v.shape, v.dtype) for v in list(srcs) + list(lands)]

    def start_body(*refs):
        src_refs, land_refs = refs[:ns], refs[ns:ns + nl]
        descs = pairs(src_refs, land_refs, refs[ns + nl + 1], refs[ns + nl + 2])

        def go():
            for send, _ in descs:
                send.start()

        _guarded(senders, go)
        refs[-1][...] = jnp.zeros_like(refs[-1])

    res = pl.pallas_call(
        start_body, name=name + "_start",
        out_shape=(dma, dma, *thru, jax.ShapeDtypeStruct((8, LANES), F32)),
        in_specs=[_HBM] * (ns + nl) + [_ANY],
        out_specs=(_SEM, _SEM, *([_HBM] * (ns + nl)), pl.BlockSpec(memory_space=pltpu.VMEM)),
        input_output_aliases={k: 2 + k for k in range(ns + nl)},
        compiler_params=_cparams(has_side_effects=_EFFECT),
    )(*[_in_hbm(v) for v in srcs], *[_in_hbm(v) for v in lands], after)
    send_sems, recv_sems, token = res[0], res[1], res[-1]
    carried = res[2:2 + ns + nl]

    def finish(after_work):
        def wait_body(*refs):
            src_refs, land_refs = refs[:ns], refs[ns:ns + nl]
            descs = pairs(src_refs, land_refs, refs[ns + nl], refs[ns + nl + 1])

            def sent():
                for send, _ in descs:
                    send.wait_send()

            def landed():
                for _, recv in descs:
                    recv.wait_recv()

            _guarded(senders, sent)
            _guarded(receivers, landed)

        out = pl.pallas_call(
            wait_body, name=name + "_wait", out_shape=tuple(thru),
            in_specs=[_HBM] * (ns + nl) + [_SEM, _SEM, _ANY], out_specs=tuple([_HBM] * (ns + nl)),
            input_output_aliases={k: k for k in range(ns + nl)},
            compiler_params=_cparams(has_side_effects=_EFFECT),
        )(*carried, send_sems, recv_sems, after_work)
        return list(out[:ns]), list(out[ns:])

    return token, finish


def _cast_slot(name, w, chip_index, after):
    R, C = w.shape[1:]
    tr = _div(R, max(16, 524288 // C), mult=16)

    def body(chip_ref, w_ref, after_ref, o_ref):
        o_ref[...] = w_ref[...].astype(BF16)

    return pl.pallas_call(
        body, name=name, out_shape=jax.ShapeDtypeStruct((N_CHIPS, R, C), BF16),
        grid_spec=pltpu.PrefetchScalarGridSpec(
            num_scalar_prefetch=1, grid=(R // tr,),
            in_specs=[pl.BlockSpec((None, tr, C), lambda i, chip_ref: (0, i, 0)), _ANY],
            out_specs=pl.BlockSpec((None, tr, C), lambda i, chip_ref: (chip_ref[0], i, 0))),
        compiler_params=_cparams())(chip_index, w, after)


def _sum_plane(name, grads, landed, chip_index):
    R, C = grads.shape[1:]
    tr = _div(R, max(16, 1048576 // C), mult=16)

    def body(chip_ref, own_ref, land_ref, o_ref):
        o_ref[...] = ((own_ref[...].astype(F32) + land_ref[0].astype(F32)) + land_ref[1].astype(F32)) \
            + land_ref[2].astype(F32)

    return pl.pallas_call(
        body, name=name, out_shape=jax.ShapeDtypeStruct((R, C), F32),
        grid_spec=pltpu.PrefetchScalarGridSpec(
            num_scalar_prefetch=1, grid=(R // tr,),
            in_specs=[pl.BlockSpec((None, tr, C), lambda i, chip_ref: (chip_ref[0], i, 0)),
                      pl.BlockSpec((3, tr, C), lambda i, chip_ref: (0, i, 0))],
            out_specs=pl.BlockSpec((tr, C), lambda i, chip_ref: (i, 0))),
        compiler_params=_cparams())(chip_index, grads, landed)


def _gather_split(name, lands, after):
    def pairs(src_refs, land_refs, send_sems, recv_sems):
        xi, yi, _ = _mesh_pos()
        mine = 2 * xi + yi
        out = []
        for a in range(len(lands)):
            for j, (px, py) in enumerate(_other_chips(xi, yi)):
                def to_slot(slot, a=a, j=j, px=px, py=py):
                    return pltpu.make_async_remote_copy(
                        src_ref=land_refs[a].at[mine], dst_ref=land_refs[a].at[slot], send_sem=send_sems.at[3 * a + j],
                        recv_sem=recv_sems.at[3 * a + j], device_id=(px, py, 1), device_id_type=MESH)
                out.append((to_slot(mine), to_slot(2 * px + py)))
        return out

    return _split_copies(name, [], lands, after, pairs, senders=1, receivers=1, ncopy=3 * len(lands))


def _allgather_split(name, block, me, after):
    land = lax.dynamic_update_slice(lax.empty((N_DEV,) + block.shape, block.dtype), block[None], (me, 0, 0))

    def pairs(src_refs, land_refs, send_sems, recv_sems):
        xi, yi, ci = _mesh_pos()
        mine = 4 * xi + 2 * yi + ci
        out = []
        for k in range(1, N_DEV):
            kx, ky, kc = (k >> 2) & 1, (k >> 1) & 1, k & 1
            px = 1 - xi if kx else xi
            py = 1 - yi if ky else yi
            pc = 1 - ci if kc else ci

            def to_slot(slot, k=k, px=px, py=py, pc=pc):
                return pltpu.make_async_remote_copy(
                    src_ref=land_refs[0].at[mine], dst_ref=land_refs[0].at[slot], send_sem=send_sems.at[k - 1],
                    recv_sem=recv_sems.at[k - 1], device_id=(px, py, pc), device_id_type=MESH)
            out.append((to_slot(mine), to_slot(4 * px + 2 * py + pc)))
        return out

    tok, fin = _split_copies(name, [], [land], after, pairs, senders=None, receivers=None, ncopy=N_DEV - 1)
    return tok, lambda later: fin(later)[1][0]


def _swap_split(name, arrs, after):
    lands = [lax.empty(v.shape, v.dtype) for v in arrs]

    def pairs(src_refs, land_refs, send_sems, recv_sems):
        xi, yi, ci = _mesh_pos()
        out = []
        for a in range(len(arrs)):
            cp = pltpu.make_async_remote_copy(
                src_ref=src_refs[a], dst_ref=land_refs[a], send_sem=send_sems.at[a], recv_sem=recv_sems.at[a],
                device_id=(xi, yi, 1 - ci), device_id_type=MESH)
            out.append((cp, cp))
        return out

    return _split_copies(name, arrs, lands, after, pairs, senders=None, receivers=None, ncopy=len(arrs))


def _pass_split(name, lands, after):
    def pairs(src_refs, land_refs, send_sems, recv_sems):
        xi, yi, _ = _mesh_pos()
        out = []
        for a in range(len(lands)):
            for j, (px, py) in enumerate(_other_chips(xi, yi)):
                cp = pltpu.make_async_remote_copy(
                    src_ref=land_refs[a].at[2 * px + py], dst_ref=land_refs[a].at[2 * px + py],
                    send_sem=send_sems.at[3 * a + j], recv_sem=recv_sems.at[3 * a + j],
                    device_id=(xi, yi, 0), device_id_type=MESH)
                out.append((cp, cp))
        return out

    return _split_copies(name, [], lands, after, pairs, senders=1, receivers=0, ncopy=3 * len(lands))


def _scatter_split(name, grads, after):
    lands = [lax.empty((3,) + g.shape[1:], g.dtype) for g in grads]

    def pairs(src_refs, land_refs, send_sems, recv_sems):
        xi, yi, ci = _mesh_pos()
        out = []
        for a in range(len(grads)):
            for j, (px, py) in enumerate(_other_chips(xi, yi)):
                cp = pltpu.make_async_remote_copy(
                    src_ref=src_refs[a].at[2 * px + py], dst_ref=land_refs[a].at[j], send_sem=send_sems.at[3 * a + j],
                    recv_sem=recv_sems.at[3 * a + j], device_id=(px, py, ci), device_id_type=MESH)
                out.append((cp, cp))
        return out

    return _split_copies(name, grads, lands, after, pairs, senders=None, receivers=None, ncopy=3 * len(grads))


def _gather_finish(name, lands):
    na = len(lands)

    def body(*refs):
        outs = refs[na:2 * na]
        send_sems, recv_sems = refs[2 * na:]
        xi, yi, ci = _mesh_pos()
        passes = [pltpu.make_async_remote_copy(
            src_ref=outs[a].at[2 * px + py], dst_ref=outs[a].at[2 * px + py],
            send_sem=send_sems.at[a, j], recv_sem=recv_sems.at[a, j], device_id=(xi, yi, 0), device_id_type=MESH)
            for a in range(na) for j, (px, py) in enumerate(_other_chips(xi, yi))]

        @pl.when(ci == 1)
        def _():
            for cp in passes:
                cp.start()
            for cp in passes:
                cp.wait_send()

        @pl.when(ci == 0)
        def _():
            for cp in passes:
                cp.wait_recv()

    return pl.pallas_call(
        body, name=name, out_shape=[jax.ShapeDtypeStruct(v.shape, v.dtype) for v in lands],
        in_specs=[_ANY] * na, out_specs=[_ANY] * na,
        input_output_aliases={a: a for a in range(na)},
        scratch_shapes=[pltpu.SemaphoreType.DMA((na, 3)), pltpu.SemaphoreType.DMA((na, 3))],
        compiler_params=_cparams())(*lands)


ATTN_HEADS_PER_STEP = 2


def _attn_tiles(L, Lc, D, tq_pref=256):
    tq = min(tq_pref, Lc)
    return tq, L // tq, Lc // tq, D // HEAD_DIM // Q_PER_KV


def _attn_scores(q, k):
    return lax.dot_general(q, k, (((1,), (1,)), ((), ())), preferred_element_type=F32) * (HEAD_DIM ** -0.5)


def _softmax_rows(s):
    e = jnp.exp(s - jnp.max(s, axis=-1, keepdims=True))
    return e * (1.0 / jnp.sum(e, axis=-1, keepdims=True))


def _attn_probs(q, k):
    return _softmax_rows(_attn_scores(q, k))


def _attn_fwd(qr, kr, v, L, Lc, D):
    T = L + Lc
    tq, nq, qoff, nkv = _attn_tiles(L, Lc, D)
    hp = Q_PER_KV
    ng = Q_PER_KV // hp

    def body(q_ref, k_ref, v_ref, o_ref):
        k, vv = k_ref[...], v_ref[...]
        heads = [slice(r * HEAD_DIM, (r + 1) * HEAD_DIM) for r in range(hp)]
        scores = [_attn_scores(q_ref[:, cols], k) for cols in heads]
        probs = [_softmax_rows(s) for s in scores]
        for cols, p in zip(heads, probs):
            o_ref[:, cols] = jnp.dot(p.astype(BF16), vv, preferred_element_type=F32).astype(o_ref.dtype)

    kv_spec = pl.BlockSpec((T, HEAD_DIM), lambda h, r, q: (0, h))
    return pl.pallas_call(
        body, name="attn_fwd", grid=(nkv, ng, nq),
        in_specs=[pl.BlockSpec((tq, hp * HEAD_DIM), lambda h, r, q: (q + qoff, h * ng + r)), kv_spec, kv_spec],
        out_specs=pl.BlockSpec((tq, hp * HEAD_DIM), lambda h, r, q: (q, h * ng + r)),
        out_shape=jax.ShapeDtypeStruct((L, D), BF16), compiler_params=_cparams())(qr, kr, v)


def _attn_bwd(qr, kr, v, do, L, Lc, D):
    T = L + Lc
    tq, nq, qoff, nkv = _attn_tiles(L, Lc, D, 256)
    scale = HEAD_DIM ** -0.5
    hp = Q_PER_KV
    ng = Q_PER_KV // hp

    def body(q_ref, k_ref, v_ref, do_ref, dq_ref, dk_ref, dv_ref):
        first = jnp.logical_and(pl.program_id(1) == 0, pl.program_id(2) == 0)
        k, vv = k_ref[...], v_ref[...]
        nt_dims, tn_dims = (((1,), (1,)), ((), ())), (((0,), (0,)), ((), ()))
        heads = [slice(r * HEAD_DIM, (r + 1) * HEAD_DIM) for r in range(hp)]
        qs = [q_ref[:, cols] for cols in heads]
        douts = [do_ref[:, cols] for cols in heads]
        scores = [_attn_scores(q, k) for q in qs]
        dps = [lax.dot_general(dout, vv, nt_dims, preferred_element_type=F32) for dout in douts]
        probs = [_softmax_rows(s) for s in scores]
        dss = [(p * (dp - jnp.sum(p * dp, axis=-1, keepdims=True)) * scale).astype(BF16) for p, dp in zip(probs, dps)]
        for cols, ds in zip(heads, dss):
            dq_ref[:, cols] = jnp.dot(ds, k, preferred_element_type=F32)
        dk = dv = None
        for q, dout, p, ds in zip(qs, douts, probs, dss):
            dk_r = lax.dot_general(ds, q, tn_dims, preferred_element_type=F32)
            dv_r = lax.dot_general(p.astype(BF16), dout, tn_dims, preferred_element_type=F32)
            dk = dk_r if dk is None else dk + dk_r
            dv = dv_r if dv is None else dv + dv_r

        @pl.when(first)
        def _():
            dk_ref[...] = dk
            dv_ref[...] = dv

        @pl.when(jnp.logical_not(first))
        def _():
            dk_ref[...] += dk
            dv_ref[...] += dv

    kv_spec = pl.BlockSpec((T, HEAD_DIM), lambda h, r, q: (0, h))
    q_spec = pl.BlockSpec((tq, hp * HEAD_DIM), lambda h, r, q: (q + qoff, h * ng + r))
    o_spec = pl.BlockSpec((tq, hp * HEAD_DIM), lambda h, r, q: (q, h * ng + r))
    return pl.pallas_call(
        body, name="attn_bwd", grid=(nkv, ng, nq),
        in_specs=[q_spec, kv_spec, kv_spec, o_spec], out_specs=[o_spec, kv_spec, kv_spec],
        out_shape=[jax.ShapeDtypeStruct((L, D), F32), jax.ShapeDtypeStruct((T, D // Q_PER_KV), F32),
                   jax.ShapeDtypeStruct((T, D // Q_PER_KV), F32)],
        compiler_params=_cparams())(qr, kr, v, do)


SUB = 8


def _doubling(xr, xi, pw_re, pw_im, lanes, first_power, period, reverse):
    n = xr.shape[0]
    rows = lax.broadcasted_iota(jnp.int32, (n, 1), 0) & (period - 1)
    for k in range(period.bit_length() - 1):
        d = 1 << k
        keep = rows < period - d if reverse else rows >= d
        sr = jnp.where(keep, pltpu.roll(xr, n - d if reverse else d, 0), 0.0)
        si = jnp.where(keep, pltpu.roll(xi, n - d if reverse else d, 0), 0.0)
        pr, pi = pw_re[first_power + k:first_power + k + 1, lanes], pw_im[first_power + k:first_power + k + 1, lanes]
        xr, xi = xr + (pr * sr - pi * si), xi + (pr * si + pi * sr)
    return xr, xi


def _scan_tile(xr, xi, tb, lanes, reverse):
    pw_re, pw_im, w8_re, w8_im, wb_re, wb_im, carry_re, carry_im, sr, si = tb
    tt = xr.shape[0]
    nb = tt // SUB
    nq = sr.shape[0]
    cols = [slice(q * LANES, (q + 1) * LANES) for q in range(nq)]
    for q in range(nq):
        sr[q] = xr[:, cols[q]]
        si[q] = xi[:, cols[q]]
    order = list(range(SUB - 2, -1, -1)) if reverse else list(range(1, SUB))
    ends_r, ends_i = [], []
    for q in range(nq):
        ql = slice(lanes.start + q * LANES, lanes.start + (q + 1) * LANES)
        lr, li = pw_re[0:1, ql], pw_im[0:1, ql]
        first_row = pl.ds(SUB - 1 if reverse else 0, nb, stride=SUB)
        pr, pi = sr[q, first_row, :], si[q, first_row, :]
        for r in order:
            rows = pl.ds(r, nb, stride=SUB)
            pr, pi = sr[q, rows, :] + (lr * pr - li * pi), si[q, rows, :] + (lr * pi + li * pr)
            sr[q, rows, :] = pr
            si[q, rows, :] = pi
        ends_r.append(pr)
        ends_i.append(pi)
    er, ei = jnp.concatenate(ends_r, axis=1), jnp.concatenate(ends_i, axis=1)
    er, ei = _doubling(er, ei, pw_re, pw_im, lanes, 3, nb, reverse)
    car, cai = carry_re[:, lanes], carry_im[:, lanes]
    wbr, wbi = wb_re[:, lanes], wb_im[:, lanes]
    er = er + (wbr * car - wbi * cai)
    ei = ei + (wbr * cai + wbi * car)
    out_block = 0 if reverse else nb - 1
    carry_re[:, lanes] = er[out_block:out_block + 1, :]
    carry_im[:, lanes] = ei[out_block:out_block + 1, :]
    blocks = lax.broadcasted_iota(jnp.int32, (nb, 1), 0)
    first = blocks == (nb - 1 if reverse else 0)
    cr = jnp.where(first, car, pltpu.roll(er, nb - 1 if reverse else 1, 0))
    ci = jnp.where(first, cai, pltpu.roll(ei, nb - 1 if reverse else 1, 0))
    for r in range(SUB):
        wr, wi = w8_re[r:r + 1, lanes], w8_im[r:r + 1, lanes]
        add_r, add_i = wr * cr - wi * ci, wr * ci + wi * cr
        for q in range(nq):
            sr[q, pl.ds(r, nb, stride=SUB), :] += add_r[:, cols[q]]
            si[q, pl.ds(r, nb, stride=SUB), :] += add_i[:, cols[q]]
    hr = jnp.concatenate([sr[q] for q in range(nq)], axis=1)
    hi = jnp.concatenate([si[q] for q in range(nq)], axis=1)
    return hr, hi, car, cai


def _scan_scratch(tt, NS):
    nb = tt // SUB
    return [pltpu.VMEM((8, NS), F32), pltpu.VMEM((8, NS), F32), pltpu.VMEM((SUB, NS), F32), pltpu.VMEM((SUB, NS), F32),
            pltpu.VMEM((nb, NS), F32), pltpu.VMEM((nb, NS), F32), pltpu.VMEM((1, NS), F32), pltpu.VMEM((1, NS), F32),
            pltpu.VMEM((SLAB_ST // LANES, tt, LANES), F32), pltpu.VMEM((SLAB_ST // LANES, tt, LANES), F32)]


def _scan_init(lr, li, tb, reverse):
    pw_re, pw_im, w8_re, w8_im, wb_re, wb_im, carry_re, carry_im, sr, _ = tb
    nb = wb_re.shape[0]
    carry_re[...] = jnp.zeros_like(carry_re)
    carry_im[...] = jnp.zeros_like(carry_im)
    pr, pi = lr, li
    for k in range(3 + nb.bit_length() - 1):
        pw_re[k:k + 1, :] = pr
        pw_im[k:k + 1, :] = pi
        if k == 3:
            l8r, l8i = pr, pi
        pr, pi = pr * pr - pi * pi, 2.0 * pr * pi
    pr, pi = lr, li
    for r in range(SUB):
        row = SUB - 1 - r if reverse else r
        w8_re[row:row + 1, :] = pr
        w8_im[row:row + 1, :] = pi
        pr, pi = pr * lr - pi * li, pr * li + pi * lr
    pr, pi = l8r, l8i
    for b in range(nb):
        row = nb - 1 - b if reverse else b
        wb_re[row:row + 1, :] = pr
        wb_im[row:row + 1, :] = pi
        pr, pi = pr * l8r - pi * l8i, pr * l8i + pi * l8r


def _ssm_tiles(T, Lc):
    tt = min(128, Lc)
    return tt, T // tt, Lc // tt


def _ssm_fwd(name, u, bbd, cbd_re, cbd_im, lam_re, lam_im, coef_re, coef_im, Lc, reverse):
    T, W = u.shape
    nslab = W // SLAB_CH
    NS = nslab * SLAB_ST
    tt, nt, nc = _ssm_tiles(T, Lc)
    if reverse:
        tile = lambda s: jnp.where(s < nc, nc - 1 - s, nt - 1 - (s - nc))
    else:
        tile = lambda s: s

    def body(u_ref, b_ref, cr_ref, ci_ref, lr_ref, li_ref, kr_ref, ki_ref, hr_ref, hi_ref, y_ref, *tb):
        @pl.when(pl.program_id(0) == 0)
        def _():
            _scan_init(lr_ref[...], li_ref[...], tb, reverse)

        for j in range(nslab):
            lanes = slice(j * SLAB_ST, (j + 1) * SLAB_ST)
            bu = jnp.dot(u_ref[:, j * SLAB_CH:(j + 1) * SLAB_CH], b_ref[j], preferred_element_type=F32)
            br, bi = bu[:, :SLAB_ST], bu[:, SLAB_ST:]
            kr, ki = kr_ref[:, lanes], ki_ref[:, lanes]
            hr, hi, _, _ = _scan_tile(kr * br - ki * bi, kr * bi + ki * br, tb, lanes, reverse)
            hrb, hib = hr.astype(BF16), hi.astype(BF16)
            hr_ref[:, lanes] = hrb
            hi_ref[:, lanes] = hib
            y_ref[:, j * SLAB_CH:(j + 1) * SLAB_CH] = (
                jnp.dot(hrb, cr_ref[j], preferred_element_type=F32)
                - jnp.dot(hib, ci_ref[j], preferred_element_type=F32))

    whole3 = lambda arr: pl.BlockSpec(arr.shape, lambda s: (0, 0, 0))
    vec = pl.BlockSpec((1, NS), lambda s: (0, 0))
    return pl.pallas_call(
        body, name=name, grid=(nt,),
        in_specs=[pl.BlockSpec((tt, W), lambda s: (tile(s), 0)), whole3(bbd), whole3(cbd_re), whole3(cbd_im),
                  vec, vec, vec, vec],
        out_specs=[pl.BlockSpec((tt, NS), lambda s: (tile(s), 0)), pl.BlockSpec((tt, NS), lambda s: (tile(s), 0)),
                   pl.BlockSpec((tt, W), lambda s: (tile(s), 0))],
        out_shape=[jax.ShapeDtypeStruct((T, NS), BF16), jax.ShapeDtypeStruct((T, NS), BF16),
                   jax.ShapeDtypeStruct((T, W), F32)],
        scratch_shapes=_scan_scratch(tt, NS),
        compiler_params=_cparams())(u, bbd, cbd_re, cbd_im, lam_re, lam_im, coef_re, coef_im)


def _ssm_bwd(name, dy, h_re, h_im, u, bbd, bbdt_re, bbdt_im, cbdt_re, cbdt_im, lam_re, lam_im,
             coef_re, coef_im, Lc, reverse):
    T, W = u.shape
    nslab = W // SLAB_CH
    NS = nslab * SLAB_ST
    tt, nt, nc = _ssm_tiles(T, Lc)
    adj_reverse = not reverse
    if reverse:
        tile = lambda s: jnp.where(s < nt - nc, nc + s, s - (nt - nc))
    else:
        tile = lambda s: nt - 1 - s

    def body(dy_ref, hr_ref, hi_ref, u_ref, b_ref, btr_ref, bti_ref, ctr_ref, cti_ref, lr_ref, li_ref,
             kr_ref, ki_ref, du_ref, dlr_ref, dli_ref, dkr_ref, dki_ref, dbf_ref, dcrf_ref, dcif_ref,
             db_ref, dcr_ref, dci_ref, *tb):
        @pl.when(pl.program_id(0) == 0)
        def _():
            _scan_init(lr_ref[...], -li_ref[...], tb, adj_reverse)
            for ref in (dlr_ref, dli_ref, dkr_ref, dki_ref, db_ref, dcr_ref, dci_ref):
                ref[...] = jnp.zeros_like(ref)

        rows = lax.broadcasted_iota(jnp.int32, (tt, 1), 0)
        far_row = tt - 1 if adj_reverse else 0
        tn_dims = (((0,), (0,)), ((), ()))
        for j in range(nslab):
            lanes = slice(j * SLAB_ST, (j + 1) * SLAB_ST)
            chans = slice(j * SLAB_CH, (j + 1) * SLAB_CH)
            dys, us = dy_ref[:, chans], u_ref[:, chans]
            er = jnp.dot(dys, ctr_ref[j], preferred_element_type=F32)
            ei = -jnp.dot(dys, cti_ref[j], preferred_element_type=F32)
            ar, ai, car, cai = _scan_tile(er, ei, tb, lanes, adj_reverse)
            shift = tt - 1 if adj_reverse else 1
            nr = jnp.where(rows == far_row, car, pltpu.roll(ar, shift, 0))
            ni = jnp.where(rows == far_row, cai, pltpu.roll(ai, shift, 0))
            hrb, hib = hr_ref[:, lanes], hi_ref[:, lanes]
            hr, hi = hrb.astype(F32), hib.astype(F32)
            dlr_ref[:, lanes] += jnp.sum(nr * hr + ni * hi, axis=0, keepdims=True)
            dli_ref[:, lanes] += jnp.sum(ni * hr - nr * hi, axis=0, keepdims=True)
            bu = jnp.dot(us, b_ref[j], preferred_element_type=F32)
            br, bi = bu[:, :SLAB_ST], bu[:, SLAB_ST:]
            dkr_ref[:, lanes] += jnp.sum(ar * br + ai * bi, axis=0, keepdims=True)
            dki_ref[:, lanes] += jnp.sum(ai * br - ar * bi, axis=0, keepdims=True)
            kr, ki = kr_ref[:, lanes], ki_ref[:, lanes]
            dbr = (ar * kr + ai * ki).astype(BF16)
            dbi = (ai * kr - ar * ki).astype(BF16)
            du_ref[:, chans] = (jnp.dot(dbr, btr_ref[j], preferred_element_type=F32)
                                + jnp.dot(dbi, bti_ref[j], preferred_element_type=F32))
            db_ref[j, :, :SLAB_ST] += lax.dot_general(us, dbr, tn_dims, preferred_element_type=F32)
            db_ref[j, :, SLAB_ST:] += lax.dot_general(us, dbi, tn_dims, preferred_element_type=F32)
            dcr_ref[j] += lax.dot_general(hrb, dys, tn_dims, preferred_element_type=F32)
            dci_ref[j] -= lax.dot_general(hib, dys, tn_dims, preferred_element_type=F32)

        @pl.when(pl.program_id(0) == nt - 1)
        def _():
            def iota(shape, axis):
                return lax.broadcasted_iota(jnp.int32, shape, axis)

            sg, ss = SSM_GROUP.bit_length() - 1, SSM_STATE.bit_length() - 1
            b_mask = (iota((SLAB_CH, SLAB_ST), 0) >> sg) == (iota((SLAB_CH, SLAB_ST), 1) >> ss)
            c_mask = (iota((SLAB_ST, SLAB_CH), 0) >> ss) == (iota((SLAB_ST, SLAB_CH), 1) >> sg)
            fold = jnp.where((iota((SLAB_ST, SSM_STATE), 0) & (SSM_STATE - 1)) == iota((SLAB_ST, SSM_STATE), 1),
                             1.0, 0.0).astype(BF16)
            fold_t = jnp.where((iota((SSM_STATE, SLAB_ST), 1) & (SSM_STATE - 1)) == iota((SSM_STATE, SLAB_ST), 0),
                               1.0, 0.0).astype(BF16)

            def exact_dot(a, b, a_is_value):
                terms = _split3(a if a_is_value else b)
                acc = None
                for t in terms:
                    part = jnp.dot(t, b, preferred_element_type=F32) if a_is_value else jnp.dot(a, t, preferred_element_type=F32)
                    acc = part if acc is None else acc + part
                return acc

            for j in range(nslab):
                dbj = db_ref[j]
                dbf_ref[j, :, :SSM_STATE] = exact_dot(jnp.where(b_mask, dbj[:, :SLAB_ST], 0.0), fold, True)
                dbf_ref[j, :, SSM_STATE:] = exact_dot(jnp.where(b_mask, dbj[:, SLAB_ST:], 0.0), fold, True)
                dcrf_ref[j] = exact_dot(fold_t, jnp.where(c_mask, dcr_ref[j], 0.0), False)
                dcif_ref[j] = exact_dot(fold_t, jnp.where(c_mask, dci_ref[j], 0.0), False)

    whole3 = lambda arr: pl.BlockSpec(arr.shape, lambda s: (0, 0, 0))
    vec = pl.BlockSpec((1, NS), lambda s: (0, 0))
    row_w = pl.BlockSpec((tt, W), lambda s: (tile(s), 0))
    row_s = pl.BlockSpec((tt, NS), lambda s: (tile(s), 0))
    dbf = jax.ShapeDtypeStruct((nslab, SLAB_CH, 2 * SSM_STATE), F32)
    dcf = jax.ShapeDtypeStruct((nslab, SSM_STATE, SLAB_CH), F32)
    return pl.pallas_call(
        body, name=name, grid=(nt,),
        in_specs=[row_w, row_s, row_s, row_w, whole3(bbd), whole3(bbdt_re), whole3(bbdt_im), whole3(cbdt_re),
                  whole3(cbdt_im), vec, vec, vec, vec],
        out_specs=[row_w, vec, vec, vec, vec, whole3(dbf), whole3(dcf), whole3(dcf)],
        out_shape=[jax.ShapeDtypeStruct((T, W), F32)] + [jax.ShapeDtypeStruct((1, NS), F32)] * 4 + [dbf, dcf, dcf],
        scratch_shapes=[pltpu.VMEM(bbd.shape, F32), pltpu.VMEM(bbdt_re.shape, F32), pltpu.VMEM(bbdt_re.shape, F32)]
        + _scan_scratch(tt, NS),
        compiler_params=_cparams())(dy, h_re, h_im, u, bbd, bbdt_re, bbdt_im, cbdt_re, cbdt_im,
                                    lam_re, lam_im, coef_re, coef_im)


def _zoh_math(a_re, a_im, log_dt):
    dt = jnp.exp(log_dt)
    mag = jnp.exp(a_re * dt)
    lb_re = mag * jnp.cos(a_im * dt)
    lb_im = mag * jnp.sin(a_im * dt)
    den = a_re * a_re + a_im * a_im
    coef_re = ((lb_re - 1.0) * a_re + lb_im * a_im) / den
    coef_im = (lb_im * a_re - (lb_re - 1.0) * a_im) / den
    return lb_re, lb_im, coef_re, coef_im


def _zoh_fwd(a_re, a_im, log_dt):
    def body(ar, ai, ld, o0, o1, o2, o3):
        for ref, val in zip((o0, o1, o2, o3), _zoh_math(ar[...], ai[...], ld[...])):
            ref[...] = val

    return pl.pallas_call(body, name="zoh_fwd", out_shape=[jax.ShapeDtypeStruct(a_re.shape, F32)] * 4,
                          compiler_params=_cparams())(a_re, a_im, log_dt)


def _zoh_bwd(a_re, a_im, log_dt, cots):
    def body(ar, ai, ld, c0, c1, c2, c3, o0, o1, o2):
        _, vjp = jax.vjp(_zoh_math, ar[...], ai[...], ld[...])
        for ref, val in zip((o0, o1, o2), vjp((c0[...], c1[...], c2[...], c3[...]))):
            ref[...] = val

    return pl.pallas_call(
        body, name="zoh_bwd",
        out_shape=[jax.ShapeDtypeStruct(a_re.shape, F32), jax.ShapeDtypeStruct(a_re.shape, F32),
                   jax.ShapeDtypeStruct(log_dt.shape, F32)],
        compiler_params=_cparams())(a_re, a_im, log_dt, *cots)


def _adamw_outer(name, w, m, v, acts, cots):
    D, N = w.shape[1:]
    tm = LANES
    dims = (((0,), (0,)), ((), ()))

    def body(a_ref, b_ref, w_ref, m_ref, v_ref, g_ref, d_ref, nm_ref, nv_ref):
        a = a_ref[...]
        aa = _split3(a * _sigmoid(a))
        bb = _split3(b_ref[...])
        g = None
        for ia in range(3):
            for ib in range(3 - ia):
                t = lax.dot_general(aa[ia], bb[ib], dims, preferred_element_type=F32)
                g = t if g is None else g + t
        g_ref[...] = g
        d_ref[...], nm_ref[...], nv_ref[...] = _adamw_math(w_ref[...], g, m_ref[...], v_ref[...])

    tile = pl.BlockSpec((None, tm, N), lambda i: (0, i, 0))
    return pl.pallas_call(
        body, name=name, grid=(D // tm,),
        in_specs=[pl.BlockSpec((16, tm), lambda i: (0, i)), pl.BlockSpec((16, N), lambda i: (0, 0)), tile, tile, tile],
        out_specs=[tile] * 4, out_shape=[jax.ShapeDtypeStruct(w.shape, F32)] * 4,
        compiler_params=_cparams())(acts, cots, w, m, v)


def _adamw_math(w, g, m, v):
    m = ADAM_B1 * m + (1.0 - ADAM_B1) * g
    v = ADAM_B2 * v + (1.0 - ADAM_B2) * (g * g)
    m_hat = m / (1.0 - ADAM_B1 ** ADAM_STEP)
    v_hat = v / (1.0 - ADAM_B2 ** ADAM_STEP)
    delta = -ADAM_LR * (m_hat / (jnp.sqrt(v_hat) + ADAM_EPS) + ADAM_WD * w)
    return delta, m, v


def _adamw(name, w, m, v, gparts):
    R, C = w.shape[-2:]
    kind = 'row1' if w.ndim == 3 else 'row'
    tr = _div(R, max(8, 524288 // C), mult=8)

    def fn(i, wv, mv, vv, *gs):
        g = gs[0]
        for extra in gs[1:]:
            g = g + extra
        return (g,) + _adamw_math(wv, g, mv, vv)

    return _rowk(name, fn, R, tr, [(w, kind), (m, kind), (v, kind)] + [(g, 'row') for g in gparts],
                 [(w.shape, F32, kind)] * 4)


def _adamw_whole(name, ws, ms, vs, gs):
    n = len(ws)

    def body(*refs):
        for k in range(n):
            g = refs[3 * n + k][...]
            res = (g,) + _adamw_math(refs[k][...], g, refs[n + k][...], refs[2 * n + k][...])
            for q in range(4):
                refs[4 * n + 4 * k + q][...] = res[q]

    out = pl.pallas_call(
        body, name=name, out_shape=[jax.ShapeDtypeStruct(w.shape, F32) for w in ws for _ in range(4)],
        compiler_params=_cparams())(*ws, *ms, *vs, *gs)
    return [tuple(out[4 * k:4 * k + 4]) for k in range(n)]


def _pack(pieces, rows_mult=8):
    flat = jnp.concatenate([p.reshape(-1).astype(F32) for p in pieces])
    unit = rows_mult * PACK_W
    total = -(-flat.shape[0] // unit) * unit
    return jnp.pad(flat, (0, total - flat.shape[0])).reshape(total // PACK_W, PACK_W)


def _unpack(buf, shapes):
    flat = buf.reshape(-1)
    out, off = [], 0
    for s in shapes:
        n = math.prod(s)
        out.append(flat[off:off + n].reshape(s))
        off += n
    return out


def _bd_expand(t):
    S, g, a, b = t.shape
    eye = jnp.eye(g, dtype=t.dtype)
    return (t[:, :, :, None, :] * eye[None, :, None, :, None]).reshape(S, g * a, g * b)


def _rope_tables(L, Lc):
    rows = L // GRID_W
    row_ids = jnp.broadcast_to(jnp.arange(rows)[:, None], (rows, GRID_W)).reshape(-1).astype(F32)
    col_ids = jnp.broadcast_to(jnp.arange(GRID_W)[None, :], (rows, GRID_W)).reshape(-1).astype(F32)
    quarter = HEAD_DIM // 4
    inv_freq = ROPE_THETA ** (-jnp.arange(quarter, dtype=F32) / quarter)
    ang_r = row_ids[:, None] * inv_freq
    ang_c = col_ids[:, None] * inv_freq
    cos = jnp.concatenate([jnp.cos(ang_r), jnp.cos(ang_r), jnp.cos(ang_c), jnp.cos(ang_c)], axis=1)
    sin = jnp.concatenate([-jnp.sin(ang_r), jnp.sin(ang_r), -jnp.sin(ang_c), jnp.sin(ang_c)], axis=1)
    cos = jnp.concatenate([jnp.ones((Lc, HEAD_DIM), F32), cos], axis=0)
    sin = jnp.concatenate([jnp.zeros((Lc, HEAD_DIM), F32), sin], axis=0)
    return cos, sin


def _rot(v):
    lane = lax.broadcasted_iota(jnp.int32, (1, HEAD_DIM), 1)
    first = (lane % (HEAD_DIM // 2)) < (HEAD_DIM // 4)
    return jnp.where(first, pltpu.roll(v, HEAD_DIM - HEAD_DIM // 4, 1), pltpu.roll(v, HEAD_DIM // 4, 1))


def _head_norm(xh, g):
    return xh * lax.rsqrt(jnp.mean(xh * xh, axis=-1, keepdims=True) + NORM_EPS) * g


def _norm_mod(xv, g, sh, sc):
    r = lax.rsqrt(jnp.mean(xv * xv, axis=-1, keepdims=True) + NORM_EPS)
    return (xv * r) * g * (1.0 + sc) + sh


def kernel(x, c, ctx, c_ctx, w_mod, b_mod, norm_g, w_ffn1_gate, w_ffn1_up, w_ffn1_down, w_in, q_norm_g, k_norm_g, ssm_a_re, ssm_a_im, ssm_log_dt, ssm_b_re, ssm_b_im, ssm_c_re, ssm_c_im, ssm_d, w_glu, b_glu, w_br_attn, w_br_ssm, w_out, w_ffn2_gate, w_ffn2_up, w_ffn2_down, loss_target, m_c_ctx, m_w_mod, m_b_mod, m_norm_g, m_w_ffn1_gate, m_w_ffn1_up, m_w_ffn1_down, m_w_in, m_q_norm_g, m_k_norm_g, m_ssm_a_re, m_ssm_a_im, m_ssm_log_dt, m_ssm_b_re, m_ssm_b_im, m_ssm_c_re, m_ssm_c_im, m_ssm_d, m_w_glu, m_b_glu, m_w_br_attn, m_w_br_ssm, m_w_out, m_w_ffn2_gate, m_w_ffn2_up, m_w_ffn2_down, v_c_ctx, v_w_mod, v_b_mod, v_norm_g, v_w_ffn1_gate, v_w_ffn1_up, v_w_ffn1_down, v_w_in, v_q_norm_g, v_k_norm_g, v_ssm_a_re, v_ssm_a_im, v_ssm_log_dt, v_ssm_b_re, v_ssm_b_im, v_ssm_c_re, v_ssm_c_im, v_ssm_d, v_w_glu, v_b_glu, v_w_br_attn, v_w_br_ssm, v_w_out, v_w_ffn2_gate, v_w_ffn2_up, v_w_ffn2_down):
    A = dict(locals())
    xi, yi, ci = _mesh_pos()
    chip = 2 * xi + yi
    me = 4 * xi + 2 * yi + ci
    L, D = x.shape[1], x.shape[2]
    Lc = ctx.shape[1]
    T = L + Lc
    F4 = w_ffn1_gate.shape[2]
    F = N_CHIPS * F4
    W, KV, Dq = D // 2, D // 4, D // 4
    G = W // SSM_GROUP
    P, E = SSM_STATE, SSM_GROUP
    NS = G * P
    nslab = W // SLAB_CH
    tr = min(256, Lc)
    ncr = Lc // tr
    assert L % tr == 0 and Lc % tr == 0 and W % SLAB_CH == 0 and D % (4 * LANES) == 0

    def sel(i, v):
        return v if v.shape[0] == 1 else jnp.where(i < ncr, v[0:1], v[1:2])

    def put(i, v, nrow):
        if nrow == 1:
            return v
        which = (i >= ncr).astype(jnp.int32)
        r2 = lax.broadcasted_iota(jnp.int32, (nrow, 1), 0)
        return jnp.where(r2 == which, jnp.broadcast_to(v, (nrow, v.shape[1])), 0.0)

    ident = lambda accs, rows, vecs, ri: [accs[0]]

    NM = w_mod.shape[2]
    first = jnp.zeros((8, D), F32).at[0].set(c[0]).at[1:4, :Dq].set(norm_g[0])
    g0 = _allgather_small("gather_c", first).reshape(N_CHIPS, 2, 8, D)
    c_all = g0[:, :, 0].reshape(N_DEV, D)
    ng = jnp.transpose(g0[:, 0, 1:4, :Dq], (1, 0, 2)).reshape(3, D)
    acts = jnp.concatenate([c_all, c_ctx[None], jnp.zeros((7, D), F32)], axis=0)
    wm = w_mod[0]
    b_shard = lax.dynamic_slice(b_mod[0], (chip * NM,), (NM,))[None]
    silu_bf = lambda a: (a * _sigmoid(a)).astype(BF16)
    to_bf = lambda b: b.astype(BF16)
    mod_part = _mm("mod_fwd", [(acts, wm, D)], 16, NM, tm=16, tn=_div(NM, 1152),
                   epi=lambda accs, rows, vecs, ri: [accs[0] + vecs[0]], outs=[(F32, False)],
                   vecs=[b_shard], a_pro=silu_bf, b_pro=to_bf)[0]
    mg = _allgather_small("gather_mod", mod_part).reshape(N_CHIPS, 2, 16, NM)[:, 0]
    mod_all = jnp.transpose(mg, (1, 0, 2)).reshape(16, N_CHIPS * NM)
    mod_x = lax.dynamic_slice(mod_all, (me, 0), (1, 9 * D))
    mod_c = jnp.where(jnp.arange(9 * D)[None] < 5 * D, mod_all[8:9], 0.0)
    modv = jnp.concatenate([mod_c, mod_x], axis=0)
    mv = lambda k: modv[:, k * D:(k + 1) * D]
    sh1, sc1, g1, sh2, sc2 = mv(0), mv(1), mv(2), mv(3), mv(4)
    g2, sh3, sc3, g3 = mv(5)[1:2], mv(6)[1:2], mv(7)[1:2], mv(8)[1:2]

    big = ['w_ffn1_gate', 'w_ffn1_up', 'w_ffn1_down', 'w_ffn2_gate', 'w_ffn2_up', 'w_ffn2_down',
           'w_in', 'w_glu', 'w_br_attn', 'w_br_ssm', 'w_out']
    row_sharded = {'w_ffn1_down', 'w_ffn2_down', 'w_glu', 'w_br_attn', 'w_out'}
    groups = [big[0:2], big[2:3], big[6:7], big[7:11], big[3:6]]
    chip_index = jnp.reshape(chip, (1,)).astype(jnp.int32)
    tok, gather_finish = modv, []
    pin = c
    for gi, names in enumerate(groups):
        tok, fin = _gather_split("gather_w%d" % gi, [_cast_slot("cast_" + n, A[n], chip_index, pin) for n in names], tok)
        gather_finish.append(fin)
        pin = tok
    ng = ng + tok[0:1, 0:1]
    Wt = {}

    def register(names, full):
        for n, gw in zip(names, full):
            Wt[n] = gw.reshape(N_CHIPS * gw.shape[1], gw.shape[2]) if n in row_sharded else gw

    def weights_ready(gi, after_work):
        _, lands = gather_finish[gi](after_work)
        register(groups[gi], _gather_finish("gather_w%d_pass" % gi, lands))

    def weights_pass(gi, after_work):
        _, lands = gather_finish[gi](after_work)
        tok_, fin_ = _pass_split("gather_w%d_pass" % gi, lands, after_work)
        return tok_, lambda later: register(groups[gi], fin_(later)[1])

    a_re2, a_im2 = ssm_a_re[0].reshape(2 * G, P), ssm_a_im[0].reshape(2 * G, P)
    ldt2 = ssm_log_dt[0].reshape(2 * G, 1)
    zoh = _zoh_fwd(a_re2, a_im2, ldt2)
    lam_re, lam_im, coef_re, coef_im = [[z[d * G:(d + 1) * G].reshape(1, NS) for d in range(2)] for z in zoh]
    bd_b = lambda b: _bd_expand(jnp.transpose(b, (0, 2, 1)).reshape(nslab, SLAB_GROUPS, E, P))
    bd_c = lambda cc: _bd_expand(jnp.transpose(cc, (0, 2, 1)).reshape(nslab, SLAB_GROUPS, P, E))
    bbd, bbdt_re, bbdt_im, cbd_re, cbd_im, cbdt_re, cbdt_im = [], [], [], [], [], [], []
    for d in range(2):
        br_, bi_ = bd_b(ssm_b_re[0, d]).astype(BF16), bd_b(ssm_b_im[0, d]).astype(BF16)
        cr_, ci_ = bd_c(ssm_c_re[0, d]).astype(BF16), bd_c(ssm_c_im[0, d]).astype(BF16)
        bbd.append(jnp.concatenate([br_, bi_], axis=2))
        bbdt_re.append(jnp.transpose(br_, (0, 2, 1)))
        bbdt_im.append(jnp.transpose(bi_, (0, 2, 1)))
        cbd_re.append(cr_)
        cbd_im.append(ci_)
        cbdt_re.append(jnp.transpose(cr_, (0, 2, 1)))
        cbdt_im.append(jnp.transpose(ci_, (0, 2, 1)))
    cos_t, sin_t = _rope_tables(L, Lc)
    qg, kg = q_norm_g, k_norm_g
    tiny = ['c_ctx', 'b_mod', 'norm_g', 'q_norm_g', 'k_norm_g', 'ssm_a_re', 'ssm_a_im', 'ssm_log_dt', 'ssm_d', 'b_glu']
    small = ['ssm_b_re', 'ssm_b_im', 'ssm_c_re', 'ssm_c_im']
    packs_wmv = [_pack([A[pre + n] for n in small]) for pre in ('', 'm_', 'v_')]
    prepared = packs_wmv + [cos_t, sin_t, coef_im[0], coef_im[1]] + [
        t[d][0] for t in (bbd, bbdt_re, bbdt_im, cbd_re, cbd_im, cbdt_re, cbdt_im) for d in range(2)]
    weights_ready(0, tok + sum(t[0:1, 0:1].astype(F32) for t in prepared))

    def norm_mod(name, xv, g, sh, sc):
        rows = xv.shape[0]
        return _rowk(name, lambda i, xt, gt, sht, sct: [_norm_mod(xt, gt, sel(i, sht), sel(i, sct))],
                     rows, tr, [(xv, 'row'), (g, 'vec'), (sh, 'vec'), (sc, 'vec')], [((rows, D), BF16, 'row')])[0]

    def swiglu_epi(accs, rows, vecs, ri):
        a_, b_ = accs
        return [a_, b_, a_ * _sigmoid(a_) * b_]

    def res_epi(coef):
        def epi(accs, rows, vecs, ri):
            gate = vecs[0]
            if gate.shape[0] == 2:
                gate = jnp.where(ri < Lc, gate[0:1], gate[1:2])
            return [accs[0], rows[0] + (coef * gate) * accs[0]]
        return epi

    def ffn_fwd(tag, h, xres, gate, down_ready=None):
        rows = h.shape[0]
        a_, b_, s_ = _mm(tag + "_up", [(h, Wt['w_' + tag + '_gate'], D), (h, Wt['w_' + tag + '_up'], D)], rows, F,
                         tm=_div(rows, 512), tn=F4, epi=swiglu_epi, outs=[(BF16, False), (BF16, False), (BF16, False)])
        if down_ready is not None:
            down_ready(s_)
        f_, xo = _mm(tag + "_down", [(s_, Wt['w_' + tag + '_down'], F)], rows, D, tm=_div(rows, 768),
                     tn=_div(D, 512), epi=res_epi(0.5), outs=[(F32, False), (F32, False)],
                     rows=[(xres, 0, 0)], vecs=[gate])
        return a_, b_, s_, f_, xo

    xc = jnp.concatenate([ctx[0], x[0]], axis=0)
    h1 = norm_mod("norm1", xc, ng[0:1], sh1, sc1)
    a1, b1, s1, f1, x1 = ffn_fwd("ffn1", h1, xc, g1, down_ready=lambda s_: weights_ready(1, s_))
    weights_ready(2, x1)
    h2 = norm_mod("norm2", x1, ng[1:2], sh2, sc2)
    proj = _mm("in_proj", [(h2, Wt['w_in'], D)], T, 4 * D, tm=_div(T, 768), tn=_div(D, 1024), epi=ident,
               outs=[(F32, False)])[0]
    nh, nkvh = D // HEAD_DIM, KV // HEAD_DIM

    def prep_fn(i, kt, vt, ut, qt, qgt, kgt, ct, st):
        qs = [_head_norm(qt[:, h * HEAD_DIM:(h + 1) * HEAD_DIM], qgt) for h in range(nh)]
        ks = [_head_norm(kt[:, h * HEAD_DIM:(h + 1) * HEAD_DIM], kgt) for h in range(nkvh)]
        qs = [v * ct + _rot(v) * st for v in qs]
        ks = [v * ct + _rot(v) * st for v in ks]
        return [jnp.concatenate(qs, axis=1), jnp.concatenate(ks, axis=1), vt, ut]

    qr, kr, vb, ub = _rowk(
        "qk_prep", prep_fn, T, tr,
        [(proj, ('col', KV, 0)), (proj, ('col', KV, 1)), (proj, ('col', W, 1)), (proj, ('col', D, 1)),
         (qg, 'vec'), (kg, 'vec'), (cos_t, 'row'), (sin_t, 'row')],
        [((T, D), BF16, 'row'), ((T, KV), BF16, 'row'), ((T, KV), BF16, 'row'), ((T, W), BF16, 'row')])
    _, mixer_weights = weights_pass(3, qr)
    attn = _attn_fwd(qr, kr, vb, L, Lc, D)
    hs_re, hs_im, ys = [], [], []
    lam_in = lam_re[0]
    for d in range(2):
        hr_, hi_, y_ = _ssm_fwd("ssm_fwd%d" % d, ub, bbd[d], cbd_re[d], cbd_im[d], lam_in, lam_im[d],
                                coef_re[d], coef_im[d], Lc, reverse=bool(d))
        hs_re.append(hr_)
        hs_im.append(hi_)
        ys.append(y_)
        if d == 0:
            tok_p4, ffn2_weights = weights_pass(4, y_)
            lam_in = lam_re[1] + tok_p4[0:1, 0:1]
    mixer_weights(ys[1])

    def ssm_out_fn(i, y0, y1, ut, dt):
        pre = dt * ut + y0 + y1
        yg_ = _gelu(pre)
        return [pre, yg_, yg_]

    ssm_pre, yg, ygb = _rowk(
        "ssm_out", ssm_out_fn, L, tr,
        [(ys[0], 'orow'), (ys[1], 'orow'), (proj, ('ocol', W, 1)), (ssm_d, 'vec')],
        [((L, W), F32, 'row'), ((L, W), F32, 'row'), ((L, W), BF16, 'row')], nc=ncr)

    def glu_epi(accs, rows, vecs, ri):
        z_ = accs[0] + vecs[0]
        return [z_, rows[0] * _sigmoid(z_)]

    zglu, y2 = _mm("glu", [(ygb, Wt['w_glu'], W)], L, W, tm=_div(L, 512), tn=_div(W, 512), epi=glu_epi,
                   outs=[(F32, False), (BF16, False)], rows=[(yg, 0, 0)], vecs=[b_glu])
    tnm = _div(Dq, 512)

    def merge_epi(accs, rows, vecs, ri):
        ga, gs = _sigmoid(rows[0]), _sigmoid(rows[1])
        return [accs[0], accs[1], ga * accs[0] + gs * accs[1]]

    ba, bs, merged = _mm("merge", [(attn, Wt['w_br_attn'], D), (y2, Wt['w_br_ssm'], W)], L, D, tm=tr, tn=tnm,
                         epi=merge_epi, outs=[(F32, False), (F32, False), (BF16, False)],
                         rows=[(proj, ncr, 2 * D // tnm), (proj, ncr, 3 * D // tnm)])
    mix, x2 = _mm("out_proj", [(merged, Wt['w_out'], D)], L, D, tm=tr, tn=_div(D, 1024), epi=res_epi(1.0),
                  outs=[(F32, False), (F32, False)], rows=[(x1, ncr, 0)], vecs=[g2])
    ffn2_weights(x2)
    h3 = norm_mod("norm3", x2, ng[2:3], sh3, sc3)
    a3, b3, s3, f3, x3 = ffn_fwd("ffn2", h3, x2, g3)

    def loss_fn(i, yt, tt_, ft, gt):
        diff = yt - tt_
        dy_ = diff * (1.0 / D)
        return [dy_, jnp.sum(diff * diff, axis=0, keepdims=True), (0.5 * gt) * dy_,
                jnp.sum(dy_ * ft, axis=0, keepdims=True) * 0.5]

    dy, sq, df3, dg3 = _rowk("loss", loss_fn, L, tr, [(x3, 'row'), (loss_target[0], 'row'), (f3, 'row'), (g3, 'vec')],
                             [((L, D), F32, 'row'), ((1, D), F32, 'acc'), ((L, D), BF16, 'row'), ((1, D), F32, 'acc')])
    loss = lax.psum(0.5 * jnp.sum(sq) / D, ("x", "y", "c"))

    def swiglu_bwd_epi(accs, rows, vecs, ri):
        ds_, a_, b_ = accs[0], rows[0].astype(F32), rows[1].astype(F32)
        sg = _sigmoid(a_)
        return [ds_ * b_ * (sg * (1.0 + a_ * (1.0 - sg))), ds_ * (a_ * sg)]

    def norm_mod_bwd(name, xv, g, sh, sc, dh, dres, dres_kind, branch=None):
        rows, nrow = xv.shape[0], sh.shape[0]

        def fn(i, xt, gt, sht, sct, dht, rest, *more):
            _, vjp = jax.vjp(_norm_mod, xt, gt, sel(i, sht), sel(i, sct))
            dx_, dg_, dsh_, dsc_ = vjp(dht)
            dx_ = dx_ + (jnp.where(i >= ncr, rest, 0.0) if dres_kind == 'xrow' else rest)
            out = [dx_, dg_, put(i, dsh_, nrow), put(i, dsc_, nrow)]
            if branch is not None:
                ft, gatet = more
                out += [(branch[2] * sel(i, gatet)) * dx_,
                        put(i, jnp.sum(dx_ * ft, axis=0, keepdims=True) * branch[2], gatet.shape[0])]
            return out

        ins = [(xv, 'row'), (g, 'vec'), (sh, 'vec'), (sc, 'vec'), (dh, 'row'), (dres, dres_kind)]
        outs = [((rows, D), F32, 'row'), ((1, D), F32, 'acc'), ((nrow, D), F32, 'acc'), ((nrow, D), F32, 'acc')]
        if branch is not None:
            ins += [(branch[0], 'row'), (branch[1], 'vec')]
            outs += [((rows, D), BF16, 'row'), ((branch[1].shape[0], D), F32, 'acc')]
        return _rowk(name, fn, rows, tr, ins, outs, nc=ncr)

    def ffn_bwd(tag, df, h, a_, b_, s_, wg, wu, wd, on_dwd=None):
        rows = df.shape[0]
        dwd = _mm(tag + "_dwd", [(s_, df, rows)], F, D, tm=_div(F, 512), tn=_div(D, 1024), ta=True, epi=ident,
                  outs=[(BF16, False)])[0].reshape(N_CHIPS, F4, D)
        if on_dwd is not None:
            on_dwd(dwd)
        da, db = _mm(tag + "_dact", [(df, wd, D)], rows, F, tm=_div(rows, 512), tn=F4, tb=True, epi=swiglu_bwd_epi,
                     outs=[(BF16, False), (BF16, False)], rows=[(a_, 0, 0), (b_, 0, 0)])
        dwg = _mm(tag + "_dwg", [(h, da, rows)], D, F, tm=_div(D, 512), tn=F4, ta=True, epi=ident,
                  outs=[(BF16, True)])[0]
        dwu = _mm(tag + "_dwu", [(h, db, rows)], D, F, tm=_div(D, 512), tn=F4, ta=True, epi=ident,
                  outs=[(BF16, True)])[0]
        dh = _mm(tag + "_dh", [(da, wg, F), (db, wu, F)], rows, D, tm=_div(rows, 768), tn=_div(D, 1024), nk=N_CHIPS,
                 tb=True, epi=ident, outs=[(F32, False)], summed=True)[0]
        return dh, dwg, dwu, dwd

    dh3, dwg2, dwu2, dwd2 = ffn_bwd("ffn2", df3, h3, a3, b3, s3, Wt['w_ffn2_gate'], Wt['w_ffn2_up'], Wt['w_ffn2_down'])
    tok_r1, scatter_fin1 = _scatter_split("scatter_ffn2", [dwg2, dwu2, dwd2], dg3)
    dx2, dng3, dsh3, dsc3, dmix, dg2 = norm_mod_bwd("norm3_bwd", x2, ng[2:3], sh3, sc3, dh3, dy, 'row',
                                                    branch=(mix, g2 + tok_r1[0:1, 0:1], 1.0))

    def dmerge_epi(accs, rows, vecs, ri):
        dm_, ba_, bs_ = accs[0], rows[0], rows[1]
        ga, gs = _sigmoid(rows[2]), _sigmoid(rows[3])
        return [dm_ * ga, dm_ * gs, dm_ * ba_ * ga * (1.0 - ga), dm_ * bs_ * gs * (1.0 - gs)]

    tnd = _div(D, 1024)
    dba, dbs, dga, dgs = _mm("dmerge", [(dmix, Wt['w_out'], D)], L, D, tm=tr, tn=tnd, tb=True, epi=dmerge_epi,
                             outs=[(BF16, False)] * 4,
                             rows=[(ba, 0, 0), (bs, 0, 0), (proj, ncr, 2 * D // tnd), (proj, ncr, 3 * D // tnd)])
    dwout = _mm("dw_out", [(merged, dmix, L)], D, D, tm=_div(D, 512), tn=_div(D, 1024), ta=True, epi=ident,
                outs=[(BF16, False)])[0].reshape(N_CHIPS, Dq, D)
    dattn = _mm("dattn", [(dba, Wt['w_br_attn'], D)], L, D, tm=_div(L, 512), tn=_div(D, 1024), tb=True, epi=ident,
                outs=[(BF16, False)])[0]
    dwba = _mm("dw_br_attn", [(attn, dba, L)], D, D, tm=_div(D, 512), tn=_div(D, 1024), ta=True, epi=ident,
               outs=[(BF16, False)])[0].reshape(N_CHIPS, Dq, D)
    dy2 = _mm("dy2", [(dbs, Wt['w_br_ssm'], D)], L, W, tm=_div(L, 512), tn=_div(W, 1024), nk=N_CHIPS, tb=True,
              epi=ident, outs=[(F32, False)])[0]
    dwbs = _mm("dw_br_ssm", [(y2, dbs, L)], W, D, tm=_div(W, 512), tn=_div(Dq, 512), ta=True, epi=ident,
               outs=[(BF16, True)])[0]

    def glu_bwd_fn(i, d2, ygt, zt):
        sz = _sigmoid(zt)
        dz_ = d2 * ygt * sz * (1.0 - sz)
        return [dz_, d2 * sz, jnp.sum(dz_, axis=0, keepdims=True)]

    dz, dyd, dbglu = _rowk("glu_bwd", glu_bwd_fn, L, tr, [(dy2, 'row'), (yg, 'row'), (zglu, 'row')],
                           [((L, W), BF16, 'row'), ((L, W), F32, 'row'), ((1, W), F32, 'acc')])

    def dssm_epi(accs, rows, vecs, ri):
        _, vjp = jax.vjp(_gelu, rows[1])
        ds_ = vjp(accs[0] + rows[0])[0]
        return [ds_, ds_]

    dssm, dssm_b = _mm("dssm", [(dz, Wt['w_glu'], W)], L, W, tm=_div(L, 512), tn=_div(W, 512), tb=True, epi=dssm_epi,
                       outs=[(F32, False), (BF16, False)], rows=[(dyd, 0, 0), (ssm_pre, 0, 0)])
    dwglu = _mm("dw_glu", [(ygb, dz, L)], W, W, tm=_div(W, 512), tn=_div(W, 1024), ta=True, epi=ident,
                outs=[(BF16, False)])[0].reshape(N_CHIPS, W // N_CHIPS, W)
    tok_r2a, scatter_fin2a = _scatter_split("scatter_mix", [dwglu, dwba, dwbs, dwout], dbglu)
    dssm_full = jnp.concatenate([jnp.zeros((Lc, W), BF16), dssm_b], axis=0)
    dus, dlam_re, dlam_im, dcoef_re, dcoef_im, dbf, dcf_re, dcf_im = [], [], [], [], [], [], [], []
    for d in range(2):
        r = _ssm_bwd("ssm_bwd%d" % d, dssm_full, hs_re[d], hs_im[d], ub, bbd[d], bbdt_re[d], bbdt_im[d],
                     cbdt_re[d], cbdt_im[d], lam_re[d] + tok_r2a[0:1, 0:1], lam_im[d], coef_re[d], coef_im[d], Lc,
                     reverse=bool(d))
        for lst, val in zip((dus, dlam_re, dlam_im, dcoef_re, dcoef_im, dbf, dcf_re, dcf_im), r):
            lst.append(val)
    dqr, dkr, dvf = _attn_bwd(qr, kr, vb, dattn, L, Lc, D)

    def prep_bwd_fn(i, qt, kt, ut, dqt, dkt, dvt, du0, du1, dst, dgat, dgst, dt, qgt, kgt, ct, st):
        live = i >= ncr
        dqt = jnp.where(live, dqt, 0.0)
        dst = jnp.where(live, dst, 0.0)
        dgat = jnp.where(live, dgat, jnp.zeros_like(dgat))
        dgst = jnp.where(live, dgst, jnp.zeros_like(dgst))
        dqs, dks = [], []
        dqg_ = jnp.zeros((1, HEAD_DIM), F32)
        dkg_ = jnp.zeros((1, HEAD_DIM), F32)
        for h in range(nh):
            hl = slice(h * HEAD_DIM, (h + 1) * HEAD_DIM)
            dn = dqt[:, hl] * ct + _rot(dqt[:, hl] * st)
            _, vjp = jax.vjp(_head_norm, qt[:, hl], qgt)
            dxh, dgh = vjp(dn)
            dqs.append(dxh)
            dqg_ = dqg_ + dgh
        for h in range(nkvh):
            hl = slice(h * HEAD_DIM, (h + 1) * HEAD_DIM)
            dn = dkt[:, hl] * ct + _rot(dkt[:, hl] * st)
            _, vjp = jax.vjp(_head_norm, kt[:, hl], kgt)
            dxh, dgh = vjp(dn)
            dks.append(dxh)
            dkg_ = dkg_ + dgh
        du_ = du0 + du1 + dst * dt
        dproj_ = jnp.concatenate([c_.astype(BF16) for c_ in dks + [dvt, du_] + dqs + [dgat, dgst]], axis=1)
        return [dproj_, dqg_, dkg_, jnp.sum(dst * ut, axis=0, keepdims=True)]

    dproj, dqg, dkg, dssd = _rowk(
        "qk_prep_bwd", prep_bwd_fn, T, tr,
        [(proj, ('col', D, 1)), (proj, ('col', KV, 0)), (proj, ('col', W, 1)), (dqr, 'xrow'), (dkr, 'row'),
         (dvf, 'row'), (dus[0], 'row'), (dus[1], 'row'), (dssm, 'xrow'), (dga, 'xrow'), (dgs, 'xrow'), (ssm_d, 'vec'),
         (qg, 'vec'), (kg, 'vec'), (cos_t, 'row'), (sin_t, 'row')],
        [((T, 4 * D), BF16, 'row'), ((1, HEAD_DIM), F32, 'acc'), ((1, HEAD_DIM), F32, 'acc'), ((1, W), F32, 'acc')],
        nc=ncr)
    dh2 = _mm("in_proj_dx", [(dproj, Wt['w_in'], 4 * D)], T, D, tm=_div(T, 768), tn=_div(D, 1024), nk=N_CHIPS, tb=True,
              epi=ident, outs=[(F32, False)])[0]
    dwin = _mm("in_proj_dw", [(h2, dproj, T)], D, 4 * D, tm=_div(D, 512), tn=_div(D, 1024), ta=True, epi=ident,
               outs=[(BF16, True)])[0]
    tok_r2, scatter_fin2 = _scatter_split("scatter_w_in", [dwin], dqg)
    dx1, dng2, dsh2, dsc2, df1, dg1 = norm_mod_bwd("norm2_bwd", x1, ng[1:2] + tok_r2[0:1, 0:1], sh2, sc2, dh2, dx2,
                                                   'xrow', branch=(f1, g1, 0.5))
    early = {}

    def start_down(dwd):
        early['tok'], early['fin'] = _scatter_split("scatter_ffn1_down", [dwd], dg2)

    dh1, dwg1, dwu1, dwd1 = ffn_bwd("ffn1", df1, h1, a1, b1, s1, Wt['w_ffn1_gate'], Wt['w_ffn1_up'],
                                    Wt['w_ffn1_down'], on_dwd=start_down)
    dx0, dng1, dsh1, dsc1 = norm_mod_bwd("norm1_bwd", xc, ng[0:1] + early['tok'][0:1, 0:1], sh1, sc1, dh1, dx1, 'row')
    grad_x = dx0[Lc:][None]

    zD = jnp.zeros((1, D), F32)
    dmod_x = jnp.concatenate([dsh1[1:2], dsc1[1:2], dg1[1:2], dsh2[1:2], dsc2[1:2], dg2, dsh3, dsc3, dg3], axis=1)
    dmod_c = jnp.concatenate([dsh1[0:1], dsc1[0:1], dg1[0:1], dsh2[0:1], dsc2[0:1], zD, zD, zD, zD], axis=1)
    pieces = [dmod_x, dmod_c, dng1, dng2, dng3, dqg, dkg] + dlam_re + dlam_im + dcoef_re + dcoef_im \
        + dbf + dcf_re + dcf_im + [dssd, dbglu]
    shapes = [p_.shape for p_ in pieces]
    pack = _pack(pieces)
    RP = pack.shape[0]
    tok_small, small_gathered = _allgather_split("gather_small", pack, me, dng1)
    tok_r3, scatter_fin3 = _scatter_split("scatter_ffn1_up", [dwg1, dwu1], tok_small)
    results = {}

    def sum_group(tag, names, fin, after_work):
        sent, landed = fin(after_work)
        plane = [_sum_plane("sum_" + n, g_, rb, chip_index) for n, g_, rb in zip(names, sent, landed)]
        tok_, swapped = _swap_split("swap_" + tag, plane, chip_index)
        return tok_, (names, swapped)

    def update_group(group, after_work):
        names, swapped = group
        mine, theirs = swapped(after_work)
        for n, m_, t_ in zip(names, mine, theirs):
            results[n] = _adamw("adamw_" + n, A[n], A['m_' + n], A['v_' + n], [m_, t_])

    tok_a, grp_ffn2 = sum_group("ffn2", big[3:6], scatter_fin1, tok_r3)
    tok_b, grp_mix = sum_group("mix", big[7:11], scatter_fin2a, tok_a)
    tok_c, grp_w_in = sum_group("w_in", big[6:7], scatter_fin2, tok_b)
    update_group(grp_ffn2, tok_c)
    tok_d, grp_down = sum_group("ffn1_down", big[2:3], early['fin'], results['w_ffn2_down'][0])
    update_group(grp_mix, tok_d)
    update_group(grp_w_in, results['w_out'][0])
    update_group(grp_down, results['w_in'][0])
    allp = small_gathered(results['w_ffn1_down'][0])
    head_rows = -(-18 * D // PACK_W)
    head = allp[:, :head_rows].reshape(N_DEV, head_rows * PACK_W)
    dmx_all = head[:, :9 * D]

    def sum_rows_fn(i, t):
        s_ = t[0:1]
        for k in range(1, N_DEV):
            s_ = s_ + t[k:k + 1]
        return [s_]

    dmc_sum = _rowk("sum_dmod_c", sum_rows_fn, 1, 1, [(head[:, 9 * D:18 * D], 'vec')], [((1, 9 * D), F32, 'row')])[0]
    cots = jnp.concatenate([dmx_all, dmc_sum, jnp.zeros((7, 9 * D), F32)], axis=0)
    cots_sh = lax.dynamic_slice(cots, (0, chip * NM), (16, NM))
    part = _mm("cctx_part", [(cots_sh[8:16], wm, NM)], 8, D, tm=8, tn=_div(D, 1024), nk=NM // _div(NM, 1152), tb=True,
               epi=ident, outs=[(F32, False)], a_pro=to_bf, b_pro=to_bf)[0]
    _, cctx_gathered = _allgather_split("gather_cctx", part, me, part)

    def sum_dev_fn(i, t):
        s_ = t[0]
        for k in range(1, N_DEV):
            s_ = s_ + t[k]
        return [s_]

    tot = _rowk("sum_small", sum_dev_fn, RP, 8, [(allp, 'row3')], [((RP, PACK_W), F32, 'row')])[0]
    (t_dmod_x, t_dmod_c, t_ng1, t_ng2, t_ng3, t_qg, t_kg, t_lr0, t_lr1, t_li0, t_li1, t_kr0, t_kr1, t_ki0, t_ki1,
     t_dbf0, t_dbf1, t_dcr0, t_dcr1, t_dci0, t_dci1, t_d, t_bglu) = _unpack(tot, shapes)
    b_grad = lambda t, lo: jnp.transpose(t[:, :, lo:lo + P].reshape(G, E, P), (0, 2, 1))
    c_grad = lambda t: jnp.transpose(t.reshape(nslab, P, SLAB_GROUPS, E), (0, 2, 3, 1)).reshape(G, E, P)
    cat2 = lambda u0, u1: jnp.concatenate([u0.reshape(G, P), u1.reshape(G, P)], axis=0)
    g_are, g_aim, g_ldt = _zoh_bwd(a_re2, a_im2, ldt2, [cat2(t_lr0, t_lr1), cat2(t_li0, t_li1),
                                                         cat2(t_kr0, t_kr1), cat2(t_ki0, t_ki1)])
    g_bmod = _rowk("bmod_grad", lambda i, u0, u1: [u0 + u1], 1, 1, [(t_dmod_x, 'row'), (t_dmod_c, 'row')],
                   [((1, 9 * D), F32, 'row')])[0]
    results['w_mod'] = tuple(_adamw_outer("adamw_w_mod", w_mod, m_w_mod, v_w_mod, acts, cots_sh))
    done = sum(results[n][1].reshape(-1, results[n][1].shape[-1])[0:1, 0:1] for n in list(results)) + g_are[0:1, 0:1] \
        + g_bmod[0:1, 0:1]
    tok_e, grp_up = sum_group("ffn1_up", big[0:2], scatter_fin3, done)
    parts = cctx_gathered(tok_e).reshape(N_CHIPS, 2, 8, D)[:, 0, 0]

    def cctx_fn(i, pt, ct):
        ds_ = ((pt[0:1] + pt[1:2]) + pt[2:3]) + pt[3:4]
        _, vjp = jax.vjp(lambda v: v * _sigmoid(v), ct)
        return [vjp(ds_)[0]]

    g_cctx = _rowk("cctx_grad", cctx_fn, 1, 1, [(parts, 'vec'), (c_ctx[None], 'row')], [((1, D), F32, 'row')])[0]

    ng_full = jnp.concatenate([t_ng1, t_ng2, t_ng3], axis=0)
    gsmall = {
        'c_ctx': g_cctx, 'b_mod': g_bmod, 'norm_g': lax.dynamic_slice(ng_full, (0, chip * Dq), (3, Dq)),
        'q_norm_g': t_qg, 'k_norm_g': t_kg, 'ssm_a_re': g_are, 'ssm_a_im': g_aim, 'ssm_log_dt': g_ldt,
        'ssm_b_re': jnp.stack([b_grad(t_dbf0, 0), b_grad(t_dbf1, 0)]),
        'ssm_b_im': jnp.stack([b_grad(t_dbf0, P), b_grad(t_dbf1, P)]),
        'ssm_c_re': jnp.stack([c_grad(t_dcr0), c_grad(t_dcr1)]), 'ssm_c_im': jnp.stack([c_grad(t_dci0), c_grad(t_dci1)]),
        'ssm_d': t_d, 'b_glu': t_bglu}
    sshapes = [A[n].shape for n in small]
    sres = _adamw("adamw_small", packs_wmv[0], packs_wmv[1], packs_wmv[2], [_pack([gsmall[n] for n in small])])
    update_group(grp_up, sres[0])
    sres = [_unpack(b_, sshapes) for b_ in sres]
    for k, n in enumerate(small):
        results[n] = tuple(sres[q][k] for q in range(4))
    as2d = lambda v: v.reshape(1, -1) if v.ndim == 1 else v
    tres = _adamw_whole("adamw_tiny", [as2d(A[n]) for n in tiny], [as2d(A['m_' + n]) for n in tiny],
                        [as2d(A['v_' + n]) for n in tiny], [gsmall[n].reshape(as2d(A[n]).shape) for n in tiny])
    for n, res in zip(tiny, tres):
        results[n] = res

    order = ['c_ctx', 'w_mod', 'b_mod', 'norm_g', 'w_ffn1_gate', 'w_ffn1_up', 'w_ffn1_down', 'w_in', 'q_norm_g',
             'k_norm_g', 'ssm_a_re', 'ssm_a_im', 'ssm_log_dt', 'ssm_b_re', 'ssm_b_im', 'ssm_c_re', 'ssm_c_im',
             'ssm_d', 'w_glu', 'b_glu', 'w_br_attn', 'w_br_ssm', 'w_out', 'w_ffn2_gate', 'w_ffn2_up', 'w_ffn2_down']
    outs = [loss, grad_x]
    for q in range(4):
        outs += [results[n][q].reshape(A[n].shape) for n in order]
    return tuple(outs)
```

```python
import math

import jax
import jax.numpy as jnp
from jax import lax
from jax.experimental import pallas as pl
from jax.experimental.pallas import tpu as pltpu

F32 = jnp.float32
BF16 = jnp.bfloat16
MESH = pl.DeviceIdType.MESH

NORM_EPS = 1e-6
ROPE_THETA = 10000.0
GRID_W = 64
HEAD_DIM = 128
Q_PER_KV = 4
SSM_GROUP = 16
SSM_STATE = 64
ADAM_LR = 0.001
ADAM_B1 = 0.9
ADAM_B2 = 0.999
ADAM_EPS = 1e-08
ADAM_WD = 0.01
ADAM_STEP = 10

N_CHIPS = 4
N_DEV = 8
LANES = 128
SLAB_CH = 128
SLAB_GROUPS = SLAB_CH // SSM_GROUP
SLAB_ST = SLAB_GROUPS * SSM_STATE
VMEM_LIMIT_BYTES = 56 * 1024 * 1024
PACK_W = 1024


def _cparams(**kw):
    return pltpu.CompilerParams(vmem_limit_bytes=VMEM_LIMIT_BYTES, **kw)


def _div(n, pref, mult=LANES):
    t = (min(pref, n) // mult) * mult
    while t >= mult:
        if n % t == 0:
            return t
        t -= mult
    return n


def _sigmoid(x):
    return jax.nn.sigmoid(x)


def _gelu(x):
    return x * (0.5 * (1.0 + jnp.tanh(math.sqrt(2.0 / math.pi) * (x + 0.044715 * (x * x * x)))))


def _rowk(name, fn, nrows, tr, ins, outs, nc=0, after=()):
    nt = nrows // tr
    in_specs, arrays = [], []
    for arr, kind in ins:
        arrays.append(arr)
        if kind == 'row':
            in_specs.append(pl.BlockSpec((tr, arr.shape[1]), lambda i: (i, 0)))
        elif kind == 'xrow':
            in_specs.append(pl.BlockSpec((tr, arr.shape[1]), lambda i: (jnp.maximum(i - nc, 0), 0)))
        elif kind == 'orow':
            in_specs.append(pl.BlockSpec((tr, arr.shape[1]), lambda i: (i + nc, 0)))
        elif kind == 'vec':
            in_specs.append(pl.BlockSpec(arr.shape, lambda i, nd=arr.ndim: (0,) * nd))
        elif kind == 'row3':
            in_specs.append(pl.BlockSpec((arr.shape[0], tr, arr.shape[2]), lambda i: (0, i, 0)))
        elif kind == 'row1':
            in_specs.append(pl.BlockSpec((None, tr, arr.shape[2]), lambda i: (0, i, 0)))
        elif kind[0] == 'ocol':
            _, width, blk = kind
            in_specs.append(pl.BlockSpec((tr, width), lambda i, blk=blk: (i + nc, blk)))
        else:
            _, width, blk = kind
            in_specs.append(pl.BlockSpec((tr, width), lambda i, blk=blk: (i, blk)))
    out_shape, out_specs = [], []
    for shape, dtype, kind in outs:
        out_shape.append(jax.ShapeDtypeStruct(shape, dtype))
        if kind == 'row':
            out_specs.append(pl.BlockSpec((tr, shape[1]), lambda i: (i, 0)))
        elif kind == 'row1':
            out_specs.append(pl.BlockSpec((None, tr, shape[2]), lambda i: (0, i, 0)))
        else:
            out_specs.append(pl.BlockSpec(shape, lambda i, nd=len(shape): (0,) * nd))
    nin = len(ins)
    for arr in after:
        arrays.append(arr)
        in_specs.append(pl.BlockSpec(memory_space=pl.ANY))
    nafter = len(after)

    def body(*refs):
        i = pl.program_id(0)
        res = fn(i, *[r[...] for r in refs[:nin]])
        for (shape, dtype, kind), ref, val in zip(outs, refs[nin + nafter:], res):
            if kind in ('row', 'row1'):
                ref[...] = val.astype(dtype)
            else:
                @pl.when(i == 0)
                def _():
                    ref[...] = val.astype(dtype)

                @pl.when(i > 0)
                def _():
                    ref[...] += val.astype(dtype)

    return pl.pallas_call(body, name=name, grid=(nt,), in_specs=in_specs, out_specs=out_specs,
                          out_shape=out_shape, compiler_params=_cparams())(*arrays)


def _mm(name, pairs, M, N, *, tm, tn, nk=1, epi, outs, ta=False, tb=False, rows=(), vecs=(),
        a_pro=None, b_pro=None, n_outer=True, summed=False):
    nm, nn = M // tm, N // tn
    npair = len(pairs)

    def idx(f):
        if n_outer:
            return lambda j, i, k: f(i, j, k)
        return lambda i, j, k: f(i, j, k)

    in_specs, args = [], []
    for a, b, K in pairs:
        tk = K // nk
        if ta:
            in_specs.append(pl.BlockSpec((tk, tm), idx(lambda i, j, k: (k, i))))
        else:
            in_specs.append(pl.BlockSpec((tm, tk), idx(lambda i, j, k: (i, k))))
        args.append(a)
        if b.ndim == 3:
            if tb:
                per = b.shape[2] // tk
                in_specs.append(pl.BlockSpec((None, tn, tk), idx(lambda i, j, k, per=per: (k // per, j, k % per))))
            else:
                per = b.shape[2] // tn
                in_specs.append(pl.BlockSpec((None, tk, tn), idx(lambda i, j, k, per=per: (j // per, k, j % per))))
        elif tb:
            in_specs.append(pl.BlockSpec((tn, tk), idx(lambda i, j, k: (j, k))))
        else:
            in_specs.append(pl.BlockSpec((tk, tn), idx(lambda i, j, k: (k, j))))
        args.append(b)
    for arr, ro, co in rows:
        in_specs.append(pl.BlockSpec((tm, tn), idx(lambda i, j, k, ro=ro, co=co: (i + ro, j + co))))
        args.append(arr)
    for arr in vecs:
        in_specs.append(pl.BlockSpec((arr.shape[0], tn), idx(lambda i, j, k: (0, j))))
        args.append(arr)
    out_shape, out_specs = [], []
    for dtype, chunked in outs:
        if chunked:
            per = (N // N_CHIPS) // tn
            out_shape.append(jax.ShapeDtypeStruct((N_CHIPS, M, N // N_CHIPS), dtype))
            out_specs.append(pl.BlockSpec((None, tm, tn), idx(lambda i, j, k, per=per: (j // per, i, j % per))))
        else:
            out_shape.append(jax.ShapeDtypeStruct((M, N), dtype))
            out_specs.append(pl.BlockSpec((tm, tn), idx(lambda i, j, k: (i, j))))
    nacc = 1 if summed else npair
    scratch = [pltpu.VMEM((tm, tn), F32) for _ in range(nacc)] if nk > 1 else []
    nrow, nvec, nout = len(rows), len(vecs), len(outs)
    dims = (((0 if ta else 1,), (1 if tb else 0,)), ((), ()))

    def body(*refs):
        ab = refs[:2 * npair]
        row_refs = refs[2 * npair:2 * npair + nrow]
        vec_refs = refs[2 * npair + nrow:2 * npair + nrow + nvec]
        out_refs = refs[2 * npair + nrow + nvec:2 * npair + nrow + nvec + nout]
        acc_refs = refs[2 * npair + nrow + nvec + nout:]
        if n_outer:
            j, i, k = pl.program_id(0), pl.program_id(1), pl.program_id(2)
        else:
            i, j, k = pl.program_id(0), pl.program_id(1), pl.program_id(2)

        def part(p):
            av, bv = ab[2 * p][...], ab[2 * p + 1][...]
            if a_pro is not None:
                av = a_pro(av)
            if b_pro is not None:
                bv = b_pro(bv)
            return lax.dot_general(av, bv, dims, preferred_element_type=F32)

        def finish(accs):
            row_index = i * tm + lax.broadcasted_iota(jnp.int32, (tm, 1), 0)
            res = epi(accs, [r[...] for r in row_refs], [v[...] for v in vec_refs], row_index)
            for ref, val in zip(out_refs, res):
                ref[...] = val.astype(ref.dtype)

        parts = [part(p) for p in range(npair)]
        if summed:
            total = parts[0]
            for extra in parts[1:]:
                total = total + extra
            parts = [total]
        if nk == 1:
            finish(parts)
        else:
            @pl.when(k == 0)
            def _():
                for q in range(nacc):
                    acc_refs[q][...] = parts[q]

            @pl.when(jnp.logical_and(k > 0, k < nk - 1))
            def _():
                for q in range(nacc):
                    acc_refs[q][...] += parts[q]

            @pl.when(k == nk - 1)
            def _():
                finish([acc_refs[q][...] + parts[q] for q in range(nacc)])

    grid = (nn, nm, nk) if n_outer else (nm, nn, nk)
    return pl.pallas_call(body, name=name, grid=grid, in_specs=in_specs, out_specs=out_specs,
                          out_shape=out_shape, scratch_shapes=scratch, compiler_params=_cparams())(*args)


def _split3(v):
    v0 = v.astype(BF16)
    r1 = v - v0.astype(F32)
    v1 = r1.astype(BF16)
    v2 = (r1 - v1.astype(F32)).astype(BF16)
    return v0, v1, v2


def _mesh_pos():
    return lax.axis_index("x"), lax.axis_index("y"), lax.axis_index("c")


def _allgather_small(name, x):
    m, n = x.shape

    def body(x_ref, out_ref, send_sems, recv_sems, local_sem):
        xi, yi, ci = _mesh_pos()
        me, sibling = (xi, yi, ci), (xi, yi, 1 - ci)
        chips = [(1 - xi, yi), (xi, 1 - yi), (1 - xi, 1 - yi)]

        def rows(px, py, pc):
            return out_ref.at[pl.ds((4 * px + 2 * py + pc) * m, m), :]

        def copy(k, block, to, src=None):
            return pltpu.make_async_remote_copy(
                src_ref=rows(*block) if src is None else src, dst_ref=rows(*block),
                send_sem=send_sems.at[k], recv_sem=recv_sems.at[k], device_id=to, device_id_type=MESH)

        mine = pltpu.make_async_copy(x_ref, rows(*me), local_sem)
        mine.start()
        first = [copy(0, me, sibling, src=x_ref)]
        first += [copy(1 + j, me, (*chip, ci), src=x_ref) for j, chip in enumerate(chips)]
        for cp in first:
            cp.start()
        passed = [copy(4 + j, (*chip, ci), sibling) for j, chip in enumerate(chips)]
        for j, chip in enumerate(chips):
            copy(1 + j, (*chip, ci), me).wait_recv()
            passed[j].start()
        copy(0, sibling, me).wait_recv()
        for j, chip in enumerate(chips):
            copy(4 + j, (*chip, 1 - ci), me).wait_recv()
        for cp in first + passed:
            cp.wait_send()
        mine.wait()

    return pl.pallas_call(
        body, name=name, out_shape=jax.ShapeDtypeStruct((N_DEV * m, n), x.dtype),
        in_specs=[pl.BlockSpec(memory_space=pltpu.VMEM)], out_specs=pl.BlockSpec(memory_space=pltpu.VMEM),
        scratch_shapes=[pltpu.SemaphoreType.DMA((7,)), pltpu.SemaphoreType.DMA((7,)), pltpu.SemaphoreType.DMA],
        compiler_params=_cparams())(x)


_HBM = pl.BlockSpec(memory_space=pltpu.HBM)
_SEM = pl.BlockSpec(memory_space=pltpu.SEMAPHORE)
_ANY = pl.BlockSpec(memory_space=pl.ANY)
_EFFECT = pltpu.SideEffectType.DATAFLOW_SIDE_EFFECTING


def _in_hbm(v):
    return pltpu.with_memory_space_constraint(v, pltpu.HBM)


def _other_chips(xi, yi):
    return [(1 - xi, yi), (xi, 1 - yi), (1 - xi, 1 - yi)]


def _guarded(core, fn):
    if core is None:
        fn()
    else:
        pl.when(lax.axis_index("c") == core)(fn)


def _split_copies(name, srcs, lands, after, pairs, senders, receivers, ncopy):
    ns, nl = len(srcs), len(lands)
    dma = pltpu.SemaphoreType.DMA((ncopy,))
    thru = [pltpu.HBM(v.shape, v.dtype) for v in list(srcs) + list(lands)]

    def start_body(*refs):
        src_refs, land_refs = refs[:ns], refs[ns:ns + nl]
        descs = pairs(src_refs, land_refs, refs[ns + nl + 1], refs[ns + nl + 2])

        def go():
            for send, _ in descs:
                send.start()

        _guarded(senders, go)
        refs[-1][...] = jnp.zeros_like(refs[-1])

    res = pl.pallas_call(
        start_body, name=name + "_start",
        out_shape=(dma, dma, *thru, jax.ShapeDtypeStruct((8, LANES), F32)),
        in_specs=[_HBM] * (ns + nl) + [_ANY],
        out_specs=(_SEM, _SEM, *([_HBM] * (ns + nl)), pl.BlockSpec(memory_space=pltpu.VMEM)),
        input_output_aliases={k: 2 + k for k in range(ns + nl)},
        compiler_params=_cparams(has_side_effects=_EFFECT),
    )(*[_in_hbm(v) for v in srcs], *[_in_hbm(v) for v in lands], after)
    send_sems, recv_sems, token = res[0], res[1], res[-1]
    carried = res[2:2 + ns + nl]

    def finish(after_work):
        def wait_body(*refs):
            src_refs, land_refs = refs[:ns], refs[ns:ns + nl]
            descs = pairs(src_refs, land_refs, refs[ns + nl], refs[ns + nl + 1])

            def sent():
                for send, _ in descs:
                    send.wait_send()

            def landed():
                for _, recv in descs:
                    recv.wait_recv()

            _guarded(senders, sent)
            _guarded(receivers, landed)

        out = pl.pallas_call(
            wait_body, name=name + "_wait", out_shape=tuple(thru),
            in_specs=[_HBM] * (ns + nl) + [_SEM, _SEM, _ANY], out_specs=tuple([_HBM] * (ns + nl)),
            input_output_aliases={k: k for k in range(ns + nl)},
            compiler_params=_cparams(has_side_effects=_EFFECT),
        )(*carried, send_sems, recv_sems, after_work)
        return list(out[:ns]), list(out[ns:])

    return token, finish


def _cast_slot(name, w, chip_index, after):
    R, C = w.shape[1:]
    tr = _div(R, max(16, 524288 // C), mult=16)

    def body(chip_ref, w_ref, after_ref, o_ref):
        o_ref[...] = w_ref[...].astype(BF16)

    return pl.pallas_call(
        body, name=name, out_shape=jax.ShapeDtypeStruct((N_CHIPS, R, C), BF16),
        grid_spec=pltpu.PrefetchScalarGridSpec(
            num_scalar_prefetch=1, grid=(R // tr,),
            in_specs=[pl.BlockSpec((None, tr, C), lambda i, chip_ref: (0, i, 0)), _ANY],
            out_specs=pl.BlockSpec((None, tr, C), lambda i, chip_ref: (chip_ref[0], i, 0))),
        compiler_params=_cparams())(chip_index, w, after)


def _sum_plane(name, grads, landed, chip_index):
    R, C = grads.shape[1:]
    tr = _div(R, max(16, 1048576 // C), mult=16)

    def body(chip_ref, own_ref, land_ref, o_ref):
        o_ref[...] = ((own_ref[...].astype(F32) + land_ref[0].astype(F32)) + land_ref[1].astype(F32)) \
            + land_ref[2].astype(F32)

    return pl.pallas_call(
        body, name=name, out_shape=jax.ShapeDtypeStruct((R, C), F32),
        grid_spec=pltpu.PrefetchScalarGridSpec(
            num_scalar_prefetch=1, grid=(R // tr,),
            in_specs=[pl.BlockSpec((None, tr, C), lambda i, chip_ref: (chip_ref[0], i, 0)),
                      pl.BlockSpec((3, tr, C), lambda i, chip_ref: (0, i, 0))],
            out_specs=pl.BlockSpec((tr, C), lambda i, chip_ref: (i, 0))),
        compiler_params=_cparams())(chip_index, grads, landed)


def _gather_split(name, lands, after):
    def pairs(src_refs, land_refs, send_sems, recv_sems):
        xi, yi, _ = _mesh_pos()
        mine = 2 * xi + yi
        out = []
        for a in range(len(lands)):
            for j, (px, py) in enumerate(_other_chips(xi, yi)):
                def to_slot(slot, a=a, j=j, px=px, py=py):
                    return pltpu.make_async_remote_copy(
                        src_ref=land_refs[a].at[mine], dst_ref=land_refs[a].at[slot], send_sem=send_sems.at[3 * a + j],
                        recv_sem=recv_sems.at[3 * a + j], device_id=(px, py, 1), device_id_type=MESH)
                out.append((to_slot(mine), to_slot(2 * px + py)))
        return out

    return _split_copies(name, [], lands, after, pairs, senders=1, receivers=1, ncopy=3 * len(lands))


def _allgather_split(name, block, me, after):
    land = lax.dynamic_update_slice(lax.empty((N_DEV,) + block.shape, block.dtype), block[None], (me, 0, 0))

    def pairs(src_refs, land_refs, send_sems, recv_sems):
        xi, yi, ci = _mesh_pos()
        mine = 4 * xi + 2 * yi + ci
        out = []
        for k in range(1, N_DEV):
            kx, ky, kc = (k >> 2) & 1, (k >> 1) & 1, k & 1
            px = 1 - xi if kx else xi
            py = 1 - yi if ky else yi
            pc = 1 - ci if kc else ci

            def to_slot(slot, k=k, px=px, py=py, pc=pc):
                return pltpu.make_async_remote_copy(
                    src_ref=land_refs[0].at[mine], dst_ref=land_refs[0].at[slot], send_sem=send_sems.at[k - 1],
                    recv_sem=recv_sems.at[k - 1], device_id=(px, py, pc), device_id_type=MESH)
            out.append((to_slot(mine), to_slot(4 * px + 2 * py + pc)))
        return out

    tok, fin = _split_copies(name, [], [land], after, pairs, senders=None, receivers=None, ncopy=N_DEV - 1)
    return tok, lambda later: fin(later)[1][0]


def _swap_split(name, arrs, after):
    lands = [lax.empty(v.shape, v.dtype) for v in arrs]

    def pairs(src_refs, land_refs, send_sems, recv_sems):
        xi, yi, ci = _mesh_pos()
        out = []
        for a in range(len(arrs)):
            cp = pltpu.make_async_remote_copy(
                src_ref=src_refs[a], dst_ref=land_refs[a], send_sem=send_sems.at[a], recv_sem=recv_sems.at[a],
                device_id=(xi, yi, 1 - ci), device_id_type=MESH)
            out.append((cp, cp))
        return out

    return _split_copies(name, arrs, lands, after, pairs, senders=None, receivers=None, ncopy=len(arrs))


def _pass_split(name, lands, after):
    def pairs(src_refs, land_refs, send_sems, recv_sems):
        xi, yi, _ = _mesh_pos()
        out = []
        for a in range(len(lands)):
            for j, (px, py) in enumerate(_other_chips(xi, yi)):
                cp = pltpu.make_async_remote_copy(
                    src_ref=land_refs[a].at[2 * px + py], dst_ref=land_refs[a].at[2 * px + py],
                    send_sem=send_sems.at[3 * a + j], recv_sem=recv_sems.at[3 * a + j],
                    device_id=(xi, yi, 0), device_id_type=MESH)
                out.append((cp, cp))
        return out

    return _split_copies(name, [], lands, after, pairs, senders=1, receivers=0, ncopy=3 * len(lands))


def _scatter_split(name, grads, after):
    lands = [lax.empty((3,) + g.shape[1:], g.dtype) for g in grads]

    def pairs(src_refs, land_refs, send_sems, recv_sems):
        xi, yi, ci = _mesh_pos()
        out = []
        for a in range(len(grads)):
            for j, (px, py) in enumerate(_other_chips(xi, yi)):
                cp = pltpu.make_async_remote_copy(
                    src_ref=src_refs[a].at[2 * px + py], dst_ref=land_refs[a].at[j], send_sem=send_sems.at[3 * a + j],
                    recv_sem=recv_sems.at[3 * a + j], device_id=(px, py, ci), device_id_type=MESH)
                out.append((cp, cp))
        return out

    return _split_copies(name, grads, lands, after, pairs, senders=None, receivers=None, ncopy=3 * len(grads))


def _gather_finish(name, lands):
    na = len(lands)

    def body(*refs):
        outs = refs[na:2 * na]
        send_sems, recv_sems = refs[2 * na:]
        xi, yi, ci = _mesh_pos()
        passes = [pltpu.make_async_remote_copy(
            src_ref=outs[a].at[2 * px + py], dst_ref=outs[a].at[2 * px + py],
            send_sem=send_sems.at[a, j], recv_sem=recv_sems.at[a, j], device_id=(xi, yi, 0), device_id_type=MESH)
            for a in range(na) for j, (px, py) in enumerate(_other_chips(xi, yi))]

        @pl.when(ci == 1)
        def _():
            for cp in passes:
                cp.start()
            for cp in passes:
                cp.wait_send()

        @pl.when(ci == 0)
        def _():
            for cp in passes:
                cp.wait_recv()

    return pl.pallas_call(
        body, name=name, out_shape=[jax.ShapeDtypeStruct(v.shape, v.dtype) for v in lands],
        in_specs=[_ANY] * na, out_specs=[_ANY] * na,
        input_output_aliases={a: a for a in range(na)},
        scratch_shapes=[pltpu.SemaphoreType.DMA((na, 3)), pltpu.SemaphoreType.DMA((na, 3))],
        compiler_params=_cparams())(*lands)


ATTN_HEADS_PER_STEP = 2


def _attn_tiles(L, Lc, D, tq_pref=256):
    tq = min(tq_pref, Lc)
    return tq, L // tq, Lc // tq, D // HEAD_DIM // Q_PER_KV


def _attn_scores(q, k):
    return lax.dot_general(q, k, (((1,), (1,)), ((), ())), preferred_element_type=F32) * (HEAD_DIM ** -0.5)


def _softmax_rows(s):
    e = jnp.exp(s - jnp.max(s, axis=-1, keepdims=True))
    return e * (1.0 / jnp.sum(e, axis=-1, keepdims=True))


def _attn_probs(q, k):
    return _softmax_rows(_attn_scores(q, k))


def _attn_fwd(qr, kr, v, L, Lc, D):
    T = L + Lc
    tq, nq, qoff, nkv = _attn_tiles(L, Lc, D)
    hp = Q_PER_KV
    ng = Q_PER_KV // hp

    def body(q_ref, k_ref, v_ref, o_ref):
        k, vv = k_ref[...], v_ref[...]
        heads = [slice(r * HEAD_DIM, (r + 1) * HEAD_DIM) for r in range(hp)]
        scores = [_attn_scores(q_ref[:, cols], k) for cols in heads]
        probs = [_softmax_rows(s) for s in scores]
        for cols, p in zip(heads, probs):
            o_ref[:, cols] = jnp.dot(p.astype(BF16), vv, preferred_element_type=F32).astype(o_ref.dtype)

    kv_spec = pl.BlockSpec((T, HEAD_DIM), lambda h, r, q: (0, h))
    return pl.pallas_call(
        body, name="attn_fwd", grid=(nkv, ng, nq),
        in_specs=[pl.BlockSpec((tq, hp * HEAD_DIM), lambda h, r, q: (q + qoff, h * ng + r)), kv_spec, kv_spec],
        out_specs=pl.BlockSpec((tq, hp * HEAD_DIM), lambda h, r, q: (q, h * ng + r)),
        out_shape=jax.ShapeDtypeStruct((L, D), BF16), compiler_params=_cparams())(qr, kr, v)


def _attn_bwd(qr, kr, v, do, L, Lc, D):
    T = L + Lc
    tq, nq, qoff, nkv = _attn_tiles(L, Lc, D, 256)
    scale = HEAD_DIM ** -0.5
    hp = Q_PER_KV
    ng = Q_PER_KV // hp

    def body(q_ref, k_ref, v_ref, do_ref, dq_ref, dk_ref, dv_ref):
        first = jnp.logical_and(pl.program_id(1) == 0, pl.program_id(2) == 0)
        k, vv = k_ref[...], v_ref[...]
        nt_dims, tn_dims = (((1,), (1,)), ((), ())), (((0,), (0,)), ((), ()))
        heads = [slice(r * HEAD_DIM, (r + 1) * HEAD_DIM) for r in range(hp)]
        qs = [q_ref[:, cols] for cols in heads]
        douts = [do_ref[:, cols] for cols in heads]
        scores = [_attn_scores(q, k) for q in qs]
        dps = [lax.dot_general(dout, vv, nt_dims, preferred_element_type=F32) for dout in douts]
        probs = [_softmax_rows(s) for s in scores]
        dss = [(p * (dp - jnp.sum(p * dp, axis=-1, keepdims=True)) * scale).astype(BF16) for p, dp in zip(probs, dps)]
        for cols, ds in zip(heads, dss):
            dq_ref[:, cols] = jnp.dot(ds, k, preferred_element_type=F32)
        dk = dv = None
        for q, dout, p, ds in zip(qs, douts, probs, dss):
            dk_r = lax.dot_general(ds, q, tn_dims, preferred_element_type=F32)
            dv_r = lax.dot_general(p.astype(BF16), dout, tn_dims, preferred_element_type=F32)
            dk = dk_r if dk is None else dk + dk_r
            dv = dv_r if dv is None else dv + dv_r

        @pl.when(first)
        def _():
            dk_ref[...] = dk
            dv_ref[...] = dv

        @pl.when(jnp.logical_not(first))
        def _():
            dk_ref[...] += dk
            dv_ref[...] += dv

    kv_spec = pl.BlockSpec((T, HEAD_DIM), lambda h, r, q: (0, h))
    q_spec = pl.BlockSpec((tq, hp * HEAD_DIM), lambda h, r, q: (q + qoff, h * ng + r))
    o_spec = pl.BlockSpec((tq, hp * HEAD_DIM), lambda h, r, q: (q, h * ng + r))
    return pl.pallas_call(
        body, name="attn_bwd", grid=(nkv, ng, nq),
        in_specs=[q_spec, kv_spec, kv_spec, o_spec], out_specs=[o_spec, kv_spec, kv_spec],
        out_shape=[jax.ShapeDtypeStruct((L, D), F32), jax.ShapeDtypeStruct((T, D // Q_PER_KV), F32),
                   jax.ShapeDtypeStruct((T, D // Q_PER_KV), F32)],
        compiler_params=_cparams())(qr, kr, v, do)


SUB = 8


def _doubling(xr, xi, pw_re, pw_im, lanes, first_power, period, reverse):
    n = xr.shape[0]
    rows = lax.broadcasted_iota(jnp.int32, (n, 1), 0) & (period - 1)
    for k in range(period.bit_length() - 1):
        d = 1 << k
        keep = rows < period - d if reverse else rows >= d
        sr = jnp.where(keep, pltpu.roll(xr, n - d if reverse else d, 0), 0.0)
        si = jnp.where(keep, pltpu.roll(xi, n - d if reverse else d, 0), 0.0)
        pr, pi = pw_re[first_power + k:first_power + k + 1, lanes], pw_im[first_power + k:first_power + k + 1, lanes]
        xr, xi = xr + (pr * sr - pi * si), xi + (pr * si + pi * sr)
    return xr, xi


def _scan_tile(xr, xi, tb, lanes, reverse):
    pw_re, pw_im, w8_re, w8_im, wb_re, wb_im, carry_re, carry_im, sr, si = tb
    tt = xr.shape[0]
    nb = tt // SUB
    nq = sr.shape[0]
    cols = [slice(q * LANES, (q + 1) * LANES) for q in range(nq)]
    for q in range(nq):
        sr[q] = xr[:, cols[q]]
        si[q] = xi[:, cols[q]]
    order = list(range(SUB - 2, -1, -1)) if reverse else list(range(1, SUB))
    ends_r, ends_i = [], []
    for q in range(nq):
        ql = slice(lanes.start + q * LANES, lanes.start + (q + 1) * LANES)
        lr, li = pw_re[0:1, ql], pw_im[0:1, ql]
        first_row = pl.ds(SUB - 1 if reverse else 0, nb, stride=SUB)
        pr, pi = sr[q, first_row, :], si[q, first_row, :]
        for r in order:
            rows = pl.ds(r, nb, stride=SUB)
            pr, pi = sr[q, rows, :] + (lr * pr - li * pi), si[q, rows, :] + (lr * pi + li * pr)
            sr[q, rows, :] = pr
            si[q, rows, :] = pi
        ends_r.append(pr)
        ends_i.append(pi)
    er, ei = jnp.concatenate(ends_r, axis=1), jnp.concatenate(ends_i, axis=1)
    er, ei = _doubling(er, ei, pw_re, pw_im, lanes, 3, nb, reverse)
    car, cai = carry_re[:, lanes], carry_im[:, lanes]
    wbr, wbi = wb_re[:, lanes], wb_im[:, lanes]
    er = er + (wbr * car - wbi * cai)
    ei = ei + (wbr * cai + wbi * car)
    out_block = 0 if reverse else nb - 1
    carry_re[:, lanes] = er[out_block:out_block + 1, :]
    carry_im[:, lanes] = ei[out_block:out_block + 1, :]
    blocks = lax.broadcasted_iota(jnp.int32, (nb, 1), 0)
    first = blocks == (nb - 1 if reverse else 0)
    cr = jnp.where(first, car, pltpu.roll(er, nb - 1 if reverse else 1, 0))
    ci = jnp.where(first, cai, pltpu.roll(ei, nb - 1 if reverse else 1, 0))
    for r in range(SUB):
        wr, wi = w8_re[r:r + 1, lanes], w8_im[r:r + 1, lanes]
        add_r, add_i = wr * cr - wi * ci, wr * ci + wi * cr
        for q in range(nq):
            sr[q, pl.ds(r, nb, stride=SUB), :] += add_r[:, cols[q]]
            si[q, pl.ds(r, nb, stride=SUB), :] += add_i[:, cols[q]]
    hr = jnp.concatenate([sr[q] for q in range(nq)], axis=1)
    hi = jnp.concatenate([si[q] for q in range(nq)], axis=1)
    return hr, hi, car, cai


def _scan_scratch(tt, NS):
    nb = tt // SUB
    return [pltpu.VMEM((8, NS), F32), pltpu.VMEM((8, NS), F32), pltpu.VMEM((SUB, NS), F32), pltpu.VMEM((SUB, NS), F32),
            pltpu.VMEM((nb, NS), F32), pltpu.VMEM((nb, NS), F32), pltpu.VMEM((1, NS), F32), pltpu.VMEM((1, NS), F32),
            pltpu.VMEM((SLAB_ST // LANES, tt, LANES), F32), pltpu.VMEM((SLAB_ST // LANES, tt, LANES), F32)]


def _scan_init(lr, li, tb, reverse):
    pw_re, pw_im, w8_re, w8_im, wb_re, wb_im, carry_re, carry_im, sr, _ = tb
    nb = wb_re.shape[0]
    carry_re[...] = jnp.zeros_like(carry_re)
    carry_im[...] = jnp.zeros_like(carry_im)
    pr, pi = lr, li
    for k in range(3 + nb.bit_length() - 1):
        pw_re[k:k + 1, :] = pr
        pw_im[k:k + 1, :] = pi
        if k == 3:
            l8r, l8i = pr, pi
        pr, pi = pr * pr - pi * pi, 2.0 * pr * pi
    pr, pi = lr, li
    for r in range(SUB):
        row = SUB - 1 - r if reverse else r
        w8_re[row:row + 1, :] = pr
        w8_im[row:row + 1, :] = pi
        pr, pi = pr * lr - pi * li, pr * li + pi * lr
    pr, pi = l8r, l8i
    for b in range(nb):
        row = nb - 1 - b if reverse else b
        wb_re[row:row + 1, :] = pr
        wb_im[row:row + 1, :] = pi
        pr, pi = pr * l8r - pi * l8i, pr * l8i + pi * l8r


def _ssm_tiles(T, Lc):
    tt = min(128, Lc)
    return tt, T // tt, Lc // tt


def _ssm_fwd(name, u, bbd, cbd_re, cbd_im, lam_re, lam_im, coef_re, coef_im, Lc, reverse):
    T, W = u.shape
    nslab = W // SLAB_CH
    NS = nslab * SLAB_ST
    tt, nt, nc = _ssm_tiles(T, Lc)
    if reverse:
        tile = lambda s: jnp.where(s < nc, nc - 1 - s, nt - 1 - (s - nc))
    else:
        tile = lambda s: s

    def body(u_ref, b_ref, cr_ref, ci_ref, lr_ref, li_ref, kr_ref, ki_ref, hr_ref, hi_ref, y_ref, *tb):
        @pl.when(pl.program_id(0) == 0)
        def _():
            _scan_init(lr_ref[...], li_ref[...], tb, reverse)

        for j in range(nslab):
            lanes = slice(j * SLAB_ST, (j + 1) * SLAB_ST)
            bu = jnp.dot(u_ref[:, j * SLAB_CH:(j + 1) * SLAB_CH], b_ref[j], preferred_element_type=F32)
            br, bi = bu[:, :SLAB_ST], bu[:, SLAB_ST:]
            kr, ki = kr_ref[:, lanes], ki_ref[:, lanes]
            hr, hi, _, _ = _scan_tile(kr * br - ki * bi, kr * bi + ki * br, tb, lanes, reverse)
            hrb, hib = hr.astype(BF16), hi.astype(BF16)
            hr_ref[:, lanes] = hrb
            hi_ref[:, lanes] = hib
            y_ref[:, j * SLAB_CH:(j + 1) * SLAB_CH] = (
                jnp.dot(hrb, cr_ref[j], preferred_element_type=F32)
                - jnp.dot(hib, ci_ref[j], preferred_element_type=F32))

    whole3 = lambda arr: pl.BlockSpec(arr.shape, lambda s: (0, 0, 0))
    vec = pl.BlockSpec((1, NS), lambda s: (0, 0))
    return pl.pallas_call(
        body, name=name, grid=(nt,),
        in_specs=[pl.BlockSpec((tt, W), lambda s: (tile(s), 0)), whole3(bbd), whole3(cbd_re), whole3(cbd_im),
                  vec, vec, vec, vec],
        out_specs=[pl.BlockSpec((tt, NS), lambda s: (tile(s), 0)), pl.BlockSpec((tt, NS), lambda s: (tile(s), 0)),
                   pl.BlockSpec((tt, W), lambda s: (tile(s), 0))],
        out_shape=[jax.ShapeDtypeStruct((T, NS), BF16), jax.ShapeDtypeStruct((T, NS), BF16),
                   jax.ShapeDtypeStruct((T, W), F32)],
        scratch_shapes=_scan_scratch(tt, NS),
        compiler_params=_cparams())(u, bbd, cbd_re, cbd_im, lam_re, lam_im, coef_re, coef_im)


def _ssm_bwd(name, dy, h_re, h_im, u, bbd, bbdt_re, bbdt_im, cbdt_re, cbdt_im, lam_re, lam_im,
             coef_re, coef_im, Lc, reverse):
    T, W = u.shape
    nslab = W // SLAB_CH
    NS = nslab * SLAB_ST
    tt, nt, nc = _ssm_tiles(T, Lc)
    adj_reverse = not reverse
    if reverse:
        tile = lambda s: jnp.where(s < nt - nc, nc + s, s - (nt - nc))
    else:
        tile = lambda s: nt - 1 - s

    def body(dy_ref, hr_ref, hi_ref, u_ref, b_ref, btr_ref, bti_ref, ctr_ref, cti_ref, lr_ref, li_ref,
             kr_ref, ki_ref, du_ref, dlr_ref, dli_ref, dkr_ref, dki_ref, dbf_ref, dcrf_ref, dcif_ref,
             db_ref, dcr_ref, dci_ref, *tb):
        @pl.when(pl.program_id(0) == 0)
        def _():
            _scan_init(lr_ref[...], -li_ref[...], tb, adj_reverse)
            for ref in (dlr_ref, dli_ref, dkr_ref, dki_ref, db_ref, dcr_ref, dci_ref):
                ref[...] = jnp.zeros_like(ref)

        rows = lax.broadcasted_iota(jnp.int32, (tt, 1), 0)
        far_row = tt - 1 if adj_reverse else 0
        tn_dims = (((0,), (0,)), ((), ()))
        for j in range(nslab):
            lanes = slice(j * SLAB_ST, (j + 1) * SLAB_ST)
            chans = slice(j * SLAB_CH, (j + 1) * SLAB_CH)
            dys, us = dy_ref[:, chans], u_ref[:, chans]
            er = jnp.dot(dys, ctr_ref[j], preferred_element_type=F32)
            ei = -jnp.dot(dys, cti_ref[j], preferred_element_type=F32)
            ar, ai, car, cai = _scan_tile(er, ei, tb, lanes, adj_reverse)
            shift = tt - 1 if adj_reverse else 1
            nr = jnp.where(rows == far_row, car, pltpu.roll(ar, shift, 0))
            ni = jnp.where(rows == far_row, cai, pltpu.roll(ai, shift, 0))
            hrb, hib = hr_ref[:, lanes], hi_ref[:, lanes]
            hr, hi = hrb.astype(F32), hib.astype(F32)
            dlr_ref[:, lanes] += jnp.sum(nr * hr + ni * hi, axis=0, keepdims=True)
            dli_ref[:, lanes] += jnp.sum(ni * hr - nr * hi, axis=0, keepdims=True)
            bu = jnp.dot(us, b_ref[j], preferred_element_type=F32)
            br, bi = bu[:, :SLAB_ST], bu[:, SLAB_ST:]
            dkr_ref[:, lanes] += jnp.sum(ar * br + ai * bi, axis=0, keepdims=True)
            dki_ref[:, lanes] += jnp.sum(ai * br - ar * bi, axis=0, keepdims=True)
            kr, ki = kr_ref[:, lanes], ki_ref[:, lanes]
            dbr = (ar * kr + ai * ki).astype(BF16)
            dbi = (ai * kr - ar * ki).astype(BF16)
            du_ref[:, chans] = (jnp.dot(dbr, btr_ref[j], preferred_element_type=F32)
                                + jnp.dot(dbi, bti_ref[j], preferred_element_type=F32))
            db_ref[j, :, :SLAB_ST] += lax.dot_general(us, dbr, tn_dims, preferred_element_type=F32)
            db_ref[j, :, SLAB_ST:] += lax.dot_general(us, dbi, tn_dims, preferred_element_type=F32)
            dcr_ref[j] += lax.dot_general(hrb, dys, tn_dims, preferred_element_type=F32)
            dci_ref[j] -= lax.dot_general(hib, dys, tn_dims, preferred_element_type=F32)

        @pl.when(pl.program_id(0) == nt - 1)
        def _():
            def iota(shape, axis):
                return lax.broadcasted_iota(jnp.int32, shape, axis)

            sg, ss = SSM_GROUP.bit_length() - 1, SSM_STATE.bit_length() - 1
            b_mask = (iota((SLAB_CH, SLAB_ST), 0) >> sg) == (iota((SLAB_CH, SLAB_ST), 1) >> ss)
            c_mask = (iota((SLAB_ST, SLAB_CH), 0) >> ss) == (iota((SLAB_ST, SLAB_CH), 1) >> sg)
            fold = jnp.where((iota((SLAB_ST, SSM_STATE), 0) & (SSM_STATE - 1)) == iota((SLAB_ST, SSM_STATE), 1),
                             1.0, 0.0).astype(BF16)
            fold_t = jnp.where((iota((SSM_STATE, SLAB_ST), 1) & (SSM_STATE - 1)) == iota((SSM_STATE, SLAB_ST), 0),
                               1.0, 0.0).astype(BF16)

            def exact_dot(a, b, a_is_value):
                terms = _split3(a if a_is_value else b)
                acc = None
                for t in terms:
                    part = jnp.dot(t, b, preferred_element_type=F32) if a_is_value else jnp.dot(a, t, preferred_element_type=F32)
                    acc = part if acc is None else acc + part
                return acc

            for j in range(nslab):
                dbj = db_ref[j]
                dbf_ref[j, :, :SSM_STATE] = exact_dot(jnp.where(b_mask, dbj[:, :SLAB_ST], 0.0), fold, True)
                dbf_ref[j, :, SSM_STATE:] = exact_dot(jnp.where(b_mask, dbj[:, SLAB_ST:], 0.0), fold, True)
                dcrf_ref[j] = exact_dot(fold_t, jnp.where(c_mask, dcr_ref[j], 0.0), False)
                dcif_ref[j] = exact_dot(fold_t, jnp.where(c_mask, dci_ref[j], 0.0), False)

    whole3 = lambda arr: pl.BlockSpec(arr.shape, lambda s: (0, 0, 0))
    vec = pl.BlockSpec((1, NS), lambda s: (0, 0))
    row_w = pl.BlockSpec((tt, W), lambda s: (tile(s), 0))
    row_s = pl.BlockSpec((tt, NS), lambda s: (tile(s), 0))
    dbf = jax.ShapeDtypeStruct((nslab, SLAB_CH, 2 * SSM_STATE), F32)
    dcf = jax.ShapeDtypeStruct((nslab, SSM_STATE, SLAB_CH), F32)
    return pl.pallas_call(
        body, name=name, grid=(nt,),
        in_specs=[row_w, row_s, row_s, row_w, whole3(bbd), whole3(bbdt_re), whole3(bbdt_im), whole3(cbdt_re),
                  whole3(cbdt_im), vec, vec, vec, vec],
        out_specs=[row_w, vec, vec, vec, vec, whole3(dbf), whole3(dcf), whole3(dcf)],
        out_shape=[jax.ShapeDtypeStruct((T, W), F32)] + [jax.ShapeDtypeStruct((1, NS), F32)] * 4 + [dbf, dcf, dcf],
        scratch_shapes=[pltpu.VMEM(bbd.shape, F32), pltpu.VMEM(bbdt_re.shape, F32), pltpu.VMEM(bbdt_re.shape, F32)]
        + _scan_scratch(tt, NS),
        compiler_params=_cparams())(dy, h_re, h_im, u, bbd, bbdt_re, bbdt_im, cbdt_re, cbdt_im,
                                    lam_re, lam_im, coef_re, coef_im)


def _zoh_math(a_re, a_im, log_dt):
    dt = jnp.exp(log_dt)
    mag = jnp.exp(a_re * dt)
    lb_re = mag * jnp.cos(a_im * dt)
    lb_im = mag * jnp.sin(a_im * dt)
    den = a_re * a_re + a_im * a_im
    coef_re = ((lb_re - 1.0) * a_re + lb_im * a_im) / den
    coef_im = (lb_im * a_re - (lb_re - 1.0) * a_im) / den
    return lb_re, lb_im, coef_re, coef_im


def _zoh_fwd(a_re, a_im, log_dt):
    def body(ar, ai, ld, o0, o1, o2, o3):
        for ref, val in zip((o0, o1, o2, o3), _zoh_math(ar[...], ai[...], ld[...])):
            ref[...] = val

    return pl.pallas_call(body, name="zoh_fwd", out_shape=[jax.ShapeDtypeStruct(a_re.shape, F32)] * 4,
                          compiler_params=_cparams())(a_re, a_im, log_dt)


def _zoh_bwd(a_re, a_im, log_dt, cots):
    def body(ar, ai, ld, c0, c1, c2, c3, o0, o1, o2):
        _, vjp = jax.vjp(_zoh_math, ar[...], ai[...], ld[...])
        for ref, val in zip((o0, o1, o2), vjp((c0[...], c1[...], c2[...], c3[...]))):
            ref[...] = val

    return pl.pallas_call(
        body, name="zoh_bwd",
        out_shape=[jax.ShapeDtypeStruct(a_re.shape, F32), jax.ShapeDtypeStruct(a_re.shape, F32),
                   jax.ShapeDtypeStruct(log_dt.shape, F32)],
        compiler_params=_cparams())(a_re, a_im, log_dt, *cots)


def _adamw_outer(name, w, m, v, acts, cots):
    D, N = w.shape[1:]
    tm = LANES
    dims = (((0,), (0,)), ((), ()))

    def body(a_ref, b_ref, w_ref, m_ref, v_ref, g_ref, d_ref, nm_ref, nv_ref):
        a = a_ref[...]
        aa = _split3(a * _sigmoid(a))
        bb = _split3(b_ref[...])
        g = None
        for ia in range(3):
            for ib in range(3 - ia):
                t = lax.dot_general(aa[ia], bb[ib], dims, preferred_element_type=F32)
                g = t if g is None else g + t
        g_ref[...] = g
        d_ref[...], nm_ref[...], nv_ref[...] = _adamw_math(w_ref[...], g, m_ref[...], v_ref[...])

    tile = pl.BlockSpec((None, tm, N), lambda i: (0, i, 0))
    return pl.pallas_call(
        body, name=name, grid=(D // tm,),
        in_specs=[pl.BlockSpec((16, tm), lambda i: (0, i)), pl.BlockSpec((16, N), lambda i: (0, 0)), tile, tile, tile],
        out_specs=[tile] * 4, out_shape=[jax.ShapeDtypeStruct(w.shape, F32)] * 4,
        compiler_params=_cparams())(acts, cots, w, m, v)


def _adamw_math(w, g, m, v):
    m = ADAM_B1 * m + (1.0 - ADAM_B1) * g
    v = ADAM_B2 * v + (1.0 - ADAM_B2) * (g * g)
    m_hat = m / (1.0 - ADAM_B1 ** ADAM_STEP)
    v_hat = v / (1.0 - ADAM_B2 ** ADAM_STEP)
    delta = -ADAM_LR * (m_hat / (jnp.sqrt(v_hat) + ADAM_EPS) + ADAM_WD * w)
    return delta, m, v


def _adamw(name, w, m, v, gparts):
    R, C = w.shape[-2:]
    kind = 'row1' if w.ndim == 3 else 'row'
    tr = _div(R, max(8, 524288 // C), mult=8)

    def fn(i, wv, mv, vv, *gs):
        g = gs[0]
        for extra in gs[1:]:
            g = g + extra
        return (g,) + _adamw_math(wv, g, mv, vv)

    return _rowk(name, fn, R, tr, [(w, kind), (m, kind), (v, kind)] + [(g, 'row') for g in gparts],
                 [(w.shape, F32, kind)] * 4)


def _adamw_whole(name, ws, ms, vs, gs):
    n = len(ws)

    def body(*refs):
        for k in range(n):
            g = refs[3 * n + k][...]
            res = (g,) + _adamw_math(refs[k][...], g, refs[n + k][...], refs[2 * n + k][...])
            for q in range(4):
                refs[4 * n + 4 * k + q][...] = res[q]

    out = pl.pallas_call(
        body, name=name, out_shape=[jax.ShapeDtypeStruct(w.shape, F32) for w in ws for _ in range(4)],
        compiler_params=_cparams())(*ws, *ms, *vs, *gs)
    return [tuple(out[4 * k:4 * k + 4]) for k in range(n)]


def _pack(pieces, rows_mult=8):
    flat = jnp.concatenate([p.reshape(-1).astype(F32) for p in pieces])
    unit = rows_mult * PACK_W
    total = -(-flat.shape[0] // unit) * unit
    return jnp.pad(flat, (0, total - flat.shape[0])).reshape(total // PACK_W, PACK_W)


def _unpack(buf, shapes):
    flat = buf.reshape(-1)
    out, off = [], 0
    for s in shapes:
        n = math.prod(s)
        out.append(flat[off:off + n].reshape(s))
        off += n
    return out


def _bd_expand(t):
    S, g, a, b = t.shape
    eye = jnp.eye(g, dtype=t.dtype)
    return (t[:, :, :, None, :] * eye[None, :, None, :, None]).reshape(S, g * a, g * b)


def _rope_tables(L, Lc):
    rows = L // GRID_W
    row_ids = jnp.broadcast_to(jnp.arange(rows)[:, None], (rows, GRID_W)).reshape(-1).astype(F32)
    col_ids = jnp.broadcast_to(jnp.arange(GRID_W)[None, :], (rows, GRID_W)).reshape(-1).astype(F32)
    quarter = HEAD_DIM // 4
    inv_freq = ROPE_THETA ** (-jnp.arange(quarter, dtype=F32) / quarter)
    ang_r = row_ids[:, None] * inv_freq
    ang_c = col_ids[:, None] * inv_freq
    cos = jnp.concatenate([jnp.cos(ang_r), jnp.cos(ang_r), jnp.cos(ang_c), jnp.cos(ang_c)], axis=1)
    sin = jnp.concatenate([-jnp.sin(ang_r), jnp.sin(ang_r), -jnp.sin(ang_c), jnp.sin(ang_c)], axis=1)
    cos = jnp.concatenate([jnp.ones((Lc, HEAD_DIM), F32), cos], axis=0)
    sin = jnp.concatenate([jnp.zeros((Lc, HEAD_DIM), F32), sin], axis=0)
    return cos, sin


def _rot(v):
    lane = lax.broadcasted_iota(jnp.int32, (1, HEAD_DIM), 1)
    first = (lane % (HEAD_DIM // 2)) < (HEAD_DIM // 4)
    return jnp.where(first, pltpu.roll(v, HEAD_DIM - HEAD_DIM // 4, 1), pltpu.roll(v, HEAD_DIM // 4, 1))


def _head_norm(xh, g):
    return xh * lax.rsqrt(jnp.mean(xh * xh, axis=-1, keepdims=True) + NORM_EPS) * g


def _norm_mod(xv, g, sh, sc):
    r = lax.rsqrt(jnp.mean(xv * xv, axis=-1, keepdims=True) + NORM_EPS)
    return (xv * r) * g * (1.0 + sc) + sh


def kernel(x, c, ctx, c_ctx, w_mod, b_mod, norm_g, w_ffn1_gate, w_ffn1_up, w_ffn1_down, w_in, q_norm_g, k_norm_g, ssm_a_re, ssm_a_im, ssm_log_dt, ssm_b_re, ssm_b_im, ssm_c_re, ssm_c_im, ssm_d, w_glu, b_glu, w_br_attn, w_br_ssm, w_out, w_ffn2_gate, w_ffn2_up, w_ffn2_down, loss_target, m_c_ctx, m_w_mod, m_b_mod, m_norm_g, m_w_ffn1_gate, m_w_ffn1_up, m_w_ffn1_down, m_w_in, m_q_norm_g, m_k_norm_g, m_ssm_a_re, m_ssm_a_im, m_ssm_log_dt, m_ssm_b_re, m_ssm_b_im, m_ssm_c_re, m_ssm_c_im, m_ssm_d, m_w_glu, m_b_glu, m_w_br_attn, m_w_br_ssm, m_w_out, m_w_ffn2_gate, m_w_ffn2_up, m_w_ffn2_down, v_c_ctx, v_w_mod, v_b_mod, v_norm_g, v_w_ffn1_gate, v_w_ffn1_up, v_w_ffn1_down, v_w_in, v_q_norm_g, v_k_norm_g, v_ssm_a_re, v_ssm_a_im, v_ssm_log_dt, v_ssm_b_re, v_ssm_b_im, v_ssm_c_re, v_ssm_c_im, v_ssm_d, v_w_glu, v_b_glu, v_w_br_attn, v_w_br_ssm, v_w_out, v_w_ffn2_gate, v_w_ffn2_up, v_w_ffn2_down):
    A = dict(locals())
    xi, yi, ci = _mesh_pos()
    chip = 2 * xi + yi
    me = 4 * xi + 2 * yi + ci
    L, D = x.shape[1], x.shape[2]
    Lc = ctx.shape[1]
    T = L + Lc
    F4 = w_ffn1_gate.shape[2]
    F = N_CHIPS * F4
    W, KV, Dq = D // 2, D // 4, D // 4
    G = W // SSM_GROUP
    P, E = SSM_STATE, SSM_GROUP
    NS = G * P
    nslab = W // SLAB_CH
    tr = min(256, Lc)
    ncr = Lc // tr
    assert L % tr == 0 and Lc % tr == 0 and W % SLAB_CH == 0 and D % (4 * LANES) == 0

    def sel(i, v):
        return v if v.shape[0] == 1 else jnp.where(i < ncr, v[0:1], v[1:2])

    def put(i, v, nrow):
        if nrow == 1:
            return v
        which = (i >= ncr).astype(jnp.int32)
        r2 = lax.broadcasted_iota(jnp.int32, (nrow, 1), 0)
        return jnp.where(r2 == which, jnp.broadcast_to(v, (nrow, v.shape[1])), 0.0)

    ident = lambda accs, rows, vecs, ri: [accs[0]]

    NM = w_mod.shape[2]
    first = jnp.zeros((8, D), F32).at[0].set(c[0]).at[1:4, :Dq].set(norm_g[0])
    g0 = _allgather_small("gather_c", first).reshape(N_CHIPS, 2, 8, D)
    c_all = g0[:, :, 0].reshape(N_DEV, D)
    ng = jnp.transpose(g0[:, 0, 1:4, :Dq], (1, 0, 2)).reshape(3, D)
    acts = jnp.concatenate([c_all, c_ctx[None], jnp.zeros((7, D), F32)], axis=0)
    wm = w_mod[0]
    b_shard = lax.dynamic_slice(b_mod[0], (chip * NM,), (NM,))[None]
    silu_bf = lambda a: (a * _sigmoid(a)).astype(BF16)
    to_bf = lambda b: b.astype(BF16)
    mod_part = _mm("mod_fwd", [(acts, wm, D)], 16, NM, tm=16, tn=_div(NM, 1152),
                   epi=lambda accs, rows, vecs, ri: [accs[0] + vecs[0]], outs=[(F32, False)],
                   vecs=[b_shard], a_pro=silu_bf, b_pro=to_bf)[0]
    mg = _allgather_small("gather_mod", mod_part).reshape(N_CHIPS, 2, 16, NM)[:, 0]
    mod_all = jnp.transpose(mg, (1, 0, 2)).reshape(16, N_CHIPS * NM)
    mod_x = lax.dynamic_slice(mod_all, (me, 0), (1, 9 * D))
    mod_c = jnp.where(jnp.arange(9 * D)[None] < 5 * D, mod_all[8:9], 0.0)
    modv = jnp.concatenate([mod_c, mod_x], axis=0)
    mv = lambda k: modv[:, k * D:(k + 1) * D]
    sh1, sc1, g1, sh2, sc2 = mv(0), mv(1), mv(2), mv(3), mv(4)
    g2, sh3, sc3, g3 = mv(5)[1:2], mv(6)[1:2], mv(7)[1:2], mv(8)[1:2]

    big = ['w_ffn1_gate', 'w_ffn1_up', 'w_ffn1_down', 'w_ffn2_gate', 'w_ffn2_up', 'w_ffn2_down',
           'w_in', 'w_glu', 'w_br_attn', 'w_br_ssm', 'w_out']
    row_sharded = {'w_ffn1_down', 'w_ffn2_down', 'w_glu', 'w_br_attn', 'w_out'}
    groups = [big[0:2], big[2:3], big[6:7], big[7:11], big[3:6]]
    chip_index = jnp.reshape(chip, (1,)).astype(jnp.int32)
    tok, gather_finish = modv, []
    pin = c
    for gi, names in enumerate(groups):
        tok, fin = _gather_split("gather_w%d" % gi, [_cast_slot("cast_" + n, A[n], chip_index, pin) for n in names], tok)
        gather_finish.append(fin)
        pin = tok
    ng = ng + tok[0:1, 0:1]
    Wt = {}

    def register(names, full):
        for n, gw in zip(names, full):
            Wt[n] = gw.reshape(N_CHIPS * gw.shape[1], gw.shape[2]) if n in row_sharded else gw

    def weights_ready(gi, after_work):
        _, lands = gather_finish[gi](after_work)
        register(groups[gi], _gather_finish("gather_w%d_pass" % gi, lands))

    def weights_pass(gi, after_work):
        _, lands = gather_finish[gi](after_work)
        tok_, fin_ = _pass_split("gather_w%d_pass" % gi, lands, after_work)
        return tok_, lambda later: register(groups[gi], fin_(later)[1])

    a_re2, a_im2 = ssm_a_re[0].reshape(2 * G, P), ssm_a_im[0].reshape(2 * G, P)
    ldt2 = ssm_log_dt[0].reshape(2 * G, 1)
    zoh = _zoh_fwd(a_re2, a_im2, ldt2)
    lam_re, lam_im, coef_re, coef_im = [[z[d * G:(d + 1) * G].reshape(1, NS) for d in range(2)] for z in zoh]
    bd_b = lambda b: _bd_expand(jnp.transpose(b, (0, 2, 1)).reshape(nslab, SLAB_GROUPS, E, P))
    bd_c = lambda cc: _bd_expand(jnp.transpose(cc, (0, 2, 1)).reshape(nslab, SLAB_GROUPS, P, E))
    bbd, bbdt_re, bbdt_im, cbd_re, cbd_im, cbdt_re, cbdt_im = [], [], [], [], [], [], []
    for d in range(2):
        br_, bi_ = bd_b(ssm_b_re[0, d]).astype(BF16), bd_b(ssm_b_im[0, d]).astype(BF16)
        cr_, ci_ = bd_c(ssm_c_re[0, d]).astype(BF16), bd_c(ssm_c_im[0, d]).astype(BF16)
        bbd.append(jnp.concatenate([br_, bi_], axis=2))
        bbdt_re.append(jnp.transpose(br_, (0, 2, 1)))
        bbdt_im.append(jnp.transpose(bi_, (0, 2, 1)))
        cbd_re.append(cr_)
        cbd_im.append(ci_)
        cbdt_re.append(jnp.transpose(cr_, (0, 2, 1)))
        cbdt_im.append(jnp.transpose(ci_, (0, 2, 1)))
    cos_t, sin_t = _rope_tables(L, Lc)
    qg, kg = q_norm_g, k_norm_g
    tiny = ['c_ctx', 'b_mod', 'norm_g', 'q_norm_g', 'k_norm_g', 'ssm_a_re', 'ssm_a_im', 'ssm_log_dt', 'ssm_d', 'b_glu']
    small = ['ssm_b_re', 'ssm_b_im', 'ssm_c_re', 'ssm_c_im']
    packs_wmv = [_pack([A[pre + n] for n in small]) for pre in ('', 'm_', 'v_')]
    prepared = packs_wmv + [cos_t, sin_t, coef_im[0], coef_im[1]] + [
        t[d][0] for t in (bbd, bbdt_re, bbdt_im, cbd_re, cbd_im, cbdt_re, cbdt_im) for d in range(2)]
    weights_ready(0, tok + sum(t[0:1, 0:1].astype(F32) for t in prepared))

    def norm_mod(name, xv, g, sh, sc):
        rows = xv.shape[0]
        return _rowk(name, lambda i, xt, gt, sht, sct: [_norm_mod(xt, gt, sel(i, sht), sel(i, sct))],
                     rows, tr, [(xv, 'row'), (g, 'vec'), (sh, 'vec'), (sc, 'vec')], [((rows, D), BF16, 'row')])[0]

    def swiglu_epi(accs, rows, vecs, ri):
        a_, b_ = accs
        return [a_, b_, a_ * _sigmoid(a_) * b_]

    def res_epi(coef):
        def epi(accs, rows, vecs, ri):
            gate = vecs[0]
            if gate.shape[0] == 2:
                gate = jnp.where(ri < Lc, gate[0:1], gate[1:2])
            return [accs[0], rows[0] + (coef * gate) * accs[0]]
        return epi

    def ffn_fwd(tag, h, xres, gate, down_ready=None):
        rows = h.shape[0]
        a_, b_, s_ = _mm(tag + "_up", [(h, Wt['w_' + tag + '_gate'], D), (h, Wt['w_' + tag + '_up'], D)], rows, F,
                         tm=_div(rows, 512), tn=F4, epi=swiglu_epi, outs=[(BF16, False), (BF16, False), (BF16, False)])
        if down_ready is not None:
            down_ready(s_)
        f_, xo = _mm(tag + "_down", [(s_, Wt['w_' + tag + '_down'], F)], rows, D, tm=_div(rows, 768),
                     tn=_div(D, 512), epi=res_epi(0.5), outs=[(F32, False), (F32, False)],
                     rows=[(xres, 0, 0)], vecs=[gate])
        return a_, b_, s_, f_, xo

    xc = jnp.concatenate([ctx[0], x[0]], axis=0)
    h1 = norm_mod("norm1", xc, ng[0:1], sh1, sc1)
    a1, b1, s1, f1, x1 = ffn_fwd("ffn1", h1, xc, g1, down_ready=lambda s_: weights_ready(1, s_))
    weights_ready(2, x1)
    h2 = norm_mod("norm2", x1, ng[1:2], sh2, sc2)
    proj = _mm("in_proj", [(h2, Wt['w_in'], D)], T, 4 * D, tm=_div(T, 768), tn=_div(D, 1024), epi=ident,
               outs=[(F32, False)])[0]
    nh, nkvh = D // HEAD_DIM, KV // HEAD_DIM

    def prep_fn(i, kt, vt, ut, qt, qgt, kgt, ct, st):
        qs = [_head_norm(qt[:, h * HEAD_DIM:(h + 1) * HEAD_DIM], qgt) for h in range(nh)]
        ks = [_head_norm(kt[:, h * HEAD_DIM:(h + 1) * HEAD_DIM], kgt) for h in range(nkvh)]
        qs = [v * ct + _rot(v) * st for v in qs]
        ks = [v * ct + _rot(v) * st for v in ks]
        return [jnp.concatenate(qs, axis=1), jnp.concatenate(ks, axis=1), vt, ut]

    qr, kr, vb, ub = _rowk(
        "qk_prep", prep_fn, T, tr,
        [(proj, ('col', KV, 0)), (proj, ('col', KV, 1)), (proj, ('col', W, 1)), (proj, ('col', D, 1)),
         (qg, 'vec'), (kg, 'vec'), (cos_t, 'row'), (sin_t, 'row')],
        [((T, D), BF16, 'row'), ((T, KV), BF16, 'row'), ((T, KV), BF16, 'row'), ((T, W), BF16, 'row')])
    _, mixer_weights = weights_pass(3, qr)
    attn = _attn_fwd(qr, kr, vb, L, Lc, D)
    hs_re, hs_im, ys = [], [], []
    lam_in = lam_re[0]
    for d in range(2):
        hr_, hi_, y_ = _ssm_fwd("ssm_fwd%d" % d, ub, bbd[d], cbd_re[d], cbd_im[d], lam_in, lam_im[d],
                                coef_re[d], coef_im[d], Lc, reverse=bool(d))
        hs_re.append(hr_)
        hs_im.append(hi_)
        ys.append(y_)
        if d == 0:
            tok_p4, ffn2_weights = weights_pass(4, y_)
            lam_in = lam_re[1] + tok_p4[0:1, 0:1]
    mixer_weights(ys[1])

    def ssm_out_fn(i, y0, y1, ut, dt):
        pre = dt * ut + y0 + y1
        yg_ = _gelu(pre)
        return [pre, yg_, yg_]

    ssm_pre, yg, ygb = _rowk(
        "ssm_out", ssm_out_fn, L, tr,
        [(ys[0], 'orow'), (ys[1], 'orow'), (proj, ('ocol', W, 1)), (ssm_d, 'vec')],
        [((L, W), F32, 'row'), ((L, W), F32, 'row'), ((L, W), BF16, 'row')], nc=ncr)

    def glu_epi(accs, rows, vecs, ri):
        z_ = accs[0] + vecs[0]
        return [z_, rows[0] * _sigmoid(z_)]

    zglu, y2 = _mm("glu", [(ygb, Wt['w_glu'], W)], L, W, tm=_div(L, 512), tn=_div(W, 512), epi=glu_epi,
                   outs=[(F32, False), (BF16, False)], rows=[(yg, 0, 0)], vecs=[b_glu])
    tnm = _div(Dq, 512)

    def merge_epi(accs, rows, vecs, ri):
        ga, gs = _sigmoid(rows[0]), _sigmoid(rows[1])
        return [accs[0], accs[1], ga * accs[0] + gs * accs[1]]

    ba, bs, merged = _mm("merge", [(attn, Wt['w_br_attn'], D), (y2, Wt['w_br_ssm'], W)], L, D, tm=tr, tn=tnm,
                         epi=merge_epi, outs=[(F32, False), (F32, False), (BF16, False)],
                         rows=[(proj, ncr, 2 * D // tnm), (proj, ncr, 3 * D // tnm)])
    mix, x2 = _mm("out_proj", [(merged, Wt['w_out'], D)], L, D, tm=tr, tn=_div(D, 1024), epi=res_epi(1.0),
                  outs=[(F32, False), (F32, False)], rows=[(x1, ncr, 0)], vecs=[g2])
    ffn2_weights(x2)
    h3 = norm_mod("norm3", x2, ng[2:3], sh3, sc3)
    a3, b3, s3, f3, x3 = ffn_fwd("ffn2", h3, x2, g3)

    def loss_fn(i, yt, tt_, ft, gt):
        diff = yt - tt_
        dy_ = diff * (1.0 / D)
        return [dy_, jnp.sum(diff * diff, axis=0, keepdims=True), (0.5 * gt) * dy_,
                jnp.sum(dy_ * ft, axis=0, keepdims=True) * 0.5]

    dy, sq, df3, dg3 = _rowk("loss", loss_fn, L, tr, [(x3, 'row'), (loss_target[0], 'row'), (f3, 'row'), (g3, 'vec')],
                             [((L, D), F32, 'row'), ((1, D), F32, 'acc'), ((L, D), BF16, 'row'), ((1, D), F32, 'acc')])
    loss = lax.psum(0.5 * jnp.sum(sq) / D, ("x", "y", "c"))

    def swiglu_bwd_epi(accs, rows, vecs, ri):
        ds_, a_, b_ = accs[0], rows[0].astype(F32), rows[1].astype(F32)
        sg = _sigmoid(a_)
        return [ds_ * b_ * (sg * (1.0 + a_ * (1.0 - sg))), ds_ * (a_ * sg)]

    def norm_mod_bwd(name, xv, g, sh, sc, dh, dres, dres_kind, branch=None, after=()):
        rows, nrow = xv.shape[0], sh.shape[0]

        def fn(i, xt, gt, sht, sct, dht, rest, *more):
            _, vjp = jax.vjp(_norm_mod, xt, gt, sel(i, sht), sel(i, sct))
            dx_, dg_, dsh_, dsc_ = vjp(dht)
            dx_ = dx_ + (jnp.where(i >= ncr, rest, 0.0) if dres_kind == 'xrow' else rest)
            out = [dx_, dg_, put(i, dsh_, nrow), put(i, dsc_, nrow)]
            if branch is not None:
                ft, gatet = more
                out += [(branch[2] * sel(i, gatet)) * dx_,
                        put(i, jnp.sum(dx_ * ft, axis=0, keepdims=True) * branch[2], gatet.shape[0])]
            return out

        ins = [(xv, 'row'), (g, 'vec'), (sh, 'vec'), (sc, 'vec'), (dh, 'row'), (dres, dres_kind)]
        outs = [((rows, D), F32, 'row'), ((1, D), F32, 'acc'), ((nrow, D), F32, 'acc'), ((nrow, D), F32, 'acc')]
        if branch is not None:
            ins += [(branch[0], 'row'), (branch[1], 'vec')]
            outs += [((rows, D), BF16, 'row'), ((branch[1].shape[0], D), F32, 'acc')]
        return _rowk(name, fn, rows, tr, ins, outs, nc=ncr, after=after)

    def ffn_bwd(tag, df, h, a_, b_, s_, wg, wu, wd, on_dwd=None):
        rows = df.shape[0]
        dwd = _mm(tag + "_dwd", [(s_, df, rows)], F, D, tm=_div(F, 512), tn=_div(D, 1024), ta=True, epi=ident,
                  outs=[(BF16, False)])[0].reshape(N_CHIPS, F4, D)
        if on_dwd is not None:
            on_dwd(dwd)
        da, db = _mm(tag + "_dact", [(df, wd, D)], rows, F, tm=_div(rows, 512), tn=F4, tb=True, epi=swiglu_bwd_epi,
                     outs=[(BF16, False), (BF16, False)], rows=[(a_, 0, 0), (b_, 0, 0)])
        dwg = _mm(tag + "_dwg", [(h, da, rows)], D, F, tm=_div(D, 512), tn=F4, ta=True, epi=ident,
                  outs=[(BF16, True)])[0]
        dwu = _mm(tag + "_dwu", [(h, db, rows)], D, F, tm=_div(D, 512), tn=F4, ta=True, epi=ident,
                  outs=[(BF16, True)])[0]
        dh = _mm(tag + "_dh", [(da, wg, F), (db, wu, F)], rows, D, tm=_div(rows, 768), tn=_div(D, 1024), nk=N_CHIPS,
                 tb=True, epi=ident, outs=[(F32, False)], summed=True)[0]
        return dh, dwg, dwu, dwd

    dh3, dwg2, dwu2, dwd2 = ffn_bwd("ffn2", df3, h3, a3, b3, s3, Wt['w_ffn2_gate'], Wt['w_ffn2_up'], Wt['w_ffn2_down'])
    tok_r1, scatter_fin1 = _scatter_split("scatter_ffn2", [dwg2, dwu2, dwd2], dg3)
    dx2, dng3, dsh3, dsc3, dmix, dg2 = norm_mod_bwd("norm3_bwd", x2, ng[2:3], sh3, sc3, dh3, dy, 'row',
                                                    branch=(mix, g2 + tok_r1[0:1, 0:1], 1.0))

    def dmerge_epi(accs, rows, vecs, ri):
        dm_, ba_, bs_ = accs[0], rows[0], rows[1]
        ga, gs = _sigmoid(rows[2]), _sigmoid(rows[3])
        return [dm_ * ga, dm_ * gs, dm_ * ba_ * ga * (1.0 - ga), dm_ * bs_ * gs * (1.0 - gs)]

    tnd = _div(D, 1024)
    dba, dbs, dga, dgs = _mm("dmerge", [(dmix, Wt['w_out'], D)], L, D, tm=tr, tn=tnd, tb=True, epi=dmerge_epi,
                             outs=[(BF16, False)] * 4,
                             rows=[(ba, 0, 0), (bs, 0, 0), (proj, ncr, 2 * D // tnd), (proj, ncr, 3 * D // tnd)])
    dwout = _mm("dw_out", [(merged, dmix, L)], D, D, tm=_div(D, 512), tn=_div(D, 1024), ta=True, epi=ident,
                outs=[(BF16, False)])[0].reshape(N_CHIPS, Dq, D)
    dattn = _mm("dattn", [(dba, Wt['w_br_attn'], D)], L, D, tm=_div(L, 512), tn=_div(D, 1024), tb=True, epi=ident,
                outs=[(BF16, False)])[0]
    dwba = _mm("dw_br_attn", [(attn, dba, L)], D, D, tm=_div(D, 512), tn=_div(D, 1024), ta=True, epi=ident,
               outs=[(BF16, False)])[0].reshape(N_CHIPS, Dq, D)
    dy2 = _mm("dy2", [(dbs, Wt['w_br_ssm'], D)], L, W, tm=_div(L, 512), tn=_div(W, 1024), nk=N_CHIPS, tb=True,
              epi=ident, outs=[(F32, False)])[0]
    dwbs = _mm("dw_br_ssm", [(y2, dbs, L)], W, D, tm=_div(W, 512), tn=_div(Dq, 512), ta=True, epi=ident,
               outs=[(BF16, True)])[0]

    def glu_bwd_fn(i, d2, ygt, zt):
        sz = _sigmoid(zt)
        dz_ = d2 * ygt * sz * (1.0 - sz)
        return [dz_, d2 * sz, jnp.sum(dz_, axis=0, keepdims=True)]

    dz, dyd, dbglu = _rowk("glu_bwd", glu_bwd_fn, L, tr, [(dy2, 'row'), (yg, 'row'), (zglu, 'row')],
                           [((L, W), BF16, 'row'), ((L, W), F32, 'row'), ((1, W), F32, 'acc')])

    def dssm_epi(accs, rows, vecs, ri):
        _, vjp = jax.vjp(_gelu, rows[1])
        ds_ = vjp(accs[0] + rows[0])[0]
        return [ds_, ds_]

    dssm, dssm_b = _mm("dssm", [(dz, Wt['w_glu'], W)], L, W, tm=_div(L, 512), tn=_div(W, 512), tb=True, epi=dssm_epi,
                       outs=[(F32, False), (BF16, False)], rows=[(dyd, 0, 0), (ssm_pre, 0, 0)])
    dwglu = _mm("dw_glu", [(ygb, dz, L)], W, W, tm=_div(W, 512), tn=_div(W, 1024), ta=True, epi=ident,
                outs=[(BF16, False)])[0].reshape(N_CHIPS, W // N_CHIPS, W)
    tok_r2a, scatter_fin2a = _scatter_split("scatter_mix", [dwglu, dwba, dwbs, dwout], dbglu)
    dssm_full = jnp.concatenate([jnp.zeros((Lc, W), BF16), dssm_b], axis=0)
    dus, dlam_re, dlam_im, dcoef_re, dcoef_im, dbf, dcf_re, dcf_im = [], [], [], [], [], [], [], []
    for d in range(2):
        r = _ssm_bwd("ssm_bwd%d" % d, dssm_full, hs_re[d], hs_im[d], ub, bbd[d], bbdt_re[d], bbdt_im[d],
                     cbdt_re[d], cbdt_im[d], lam_re[d] + tok_r2a[0:1, 0:1], lam_im[d], coef_re[d], coef_im[d], Lc,
                     reverse=bool(d))
        for lst, val in zip((dus, dlam_re, dlam_im, dcoef_re, dcoef_im, dbf, dcf_re, dcf_im), r):
            lst.append(val)
    dqr, dkr, dvf = _attn_bwd(qr, kr, vb, dattn, L, Lc, D)

    def prep_bwd_fn(i, qt, kt, ut, dqt, dkt, dvt, du0, du1, dst, dgat, dgst, dt, qgt, kgt, ct, st):
        live = i >= ncr
        dqt = jnp.where(live, dqt, 0.0)
        dst = jnp.where(live, dst, 0.0)
        dgat = jnp.where(live, dgat, jnp.zeros_like(dgat))
        dgst = jnp.where(live, dgst, jnp.zeros_like(dgst))
        dqs, dks = [], []
        dqg_ = jnp.zeros((1, HEAD_DIM), F32)
        dkg_ = jnp.zeros((1, HEAD_DIM), F32)
        for h in range(nh):
            hl = slice(h * HEAD_DIM, (h + 1) * HEAD_DIM)
            dn = dqt[:, hl] * ct + _rot(dqt[:, hl] * st)
            _, vjp = jax.vjp(_head_norm, qt[:, hl], qgt)
            dxh, dgh = vjp(dn)
            dqs.append(dxh)
            dqg_ = dqg_ + dgh
        for h in range(nkvh):
            hl = slice(h * HEAD_DIM, (h + 1) * HEAD_DIM)
            dn = dkt[:, hl] * ct + _rot(dkt[:, hl] * st)
            _, vjp = jax.vjp(_head_norm, kt[:, hl], kgt)
            dxh, dgh = vjp(dn)
            dks.append(dxh)
            dkg_ = dkg_ + dgh
        du_ = du0 + du1 + dst * dt
        dproj_ = jnp.concatenate([c_.astype(BF16) for c_ in dks + [dvt, du_] + dqs + [dgat, dgst]], axis=1)
        return [dproj_, dqg_, dkg_, jnp.sum(dst * ut, axis=0, keepdims=True)]

    dproj, dqg, dkg, dssd = _rowk(
        "qk_prep_bwd", prep_bwd_fn, T, tr,
        [(proj, ('col', D, 1)), (proj, ('col', KV, 0)), (proj, ('col', W, 1)), (dqr, 'xrow'), (dkr, 'row'),
         (dvf, 'row'), (dus[0], 'row'), (dus[1], 'row'), (dssm, 'xrow'), (dga, 'xrow'), (dgs, 'xrow'), (ssm_d, 'vec'),
         (qg, 'vec'), (kg, 'vec'), (cos_t, 'row'), (sin_t, 'row')],
        [((T, 4 * D), BF16, 'row'), ((1, HEAD_DIM), F32, 'acc'), ((1, HEAD_DIM), F32, 'acc'), ((1, W), F32, 'acc')],
        nc=ncr)
    dh2 = _mm("in_proj_dx", [(dproj, Wt['w_in'], 4 * D)], T, D, tm=_div(T, 768), tn=_div(D, 1024), nk=N_CHIPS, tb=True,
              epi=ident, outs=[(F32, False)])[0]
    dwin = _mm("in_proj_dw", [(h2, dproj, T)], D, 4 * D, tm=_div(D, 512), tn=_div(D, 1024), ta=True, epi=ident,
               outs=[(BF16, True)])[0]
    tok_r2, scatter_fin2 = _scatter_split("scatter_w_in", [dwin], dqg)
    dx1, dng2, dsh2, dsc2, df1, dg1 = norm_mod_bwd("norm2_bwd", x1, ng[1:2] + tok_r2[0:1, 0:1], sh2, sc2, dh2, dx2,
                                                   'xrow', branch=(f1, g1, 0.5))
    early = {}

    def start_down(dwd):
        early['tok'], early['fin'] = _scatter_split("scatter_ffn1_down", [dwd], dg2)

    dh1, dwg1, dwu1, dwd1 = ffn_bwd("ffn1", df1, h1, a1, b1, s1, Wt['w_ffn1_gate'], Wt['w_ffn1_up'],
                                    Wt['w_ffn1_down'], on_dwd=start_down)
    dx0, dng1, dsh1, dsc1 = norm_mod_bwd("norm1_bwd", xc, ng[0:1] + early['tok'][0:1, 0:1], sh1, sc1, dh1, dx1, 'row',
                                         after=(dwg1, dwu1))
    grad_x = dx0[Lc:][None]

    zD = jnp.zeros((1, D), F32)
    dmod_x = jnp.concatenate([dsh1[1:2], dsc1[1:2], dg1[1:2], dsh2[1:2], dsc2[1:2], dg2, dsh3, dsc3, dg3], axis=1)
    dmod_c = jnp.concatenate([dsh1[0:1], dsc1[0:1], dg1[0:1], dsh2[0:1], dsc2[0:1], zD, zD, zD, zD], axis=1)
    pieces = [dmod_x, dmod_c, dng1, dng2, dng3, dqg, dkg] + dlam_re + dlam_im + dcoef_re + dcoef_im \
        + dbf + dcf_re + dcf_im + [dssd, dbglu]
    shapes = [p_.shape for p_ in pieces]
    pack = _pack(pieces)
    RP = pack.shape[0]
    tok_small, small_gathered = _allgather_split("gather_small", pack, me, dng1)
    tok_r3, scatter_fin3 = _scatter_split("scatter_ffn1_up", [dwg1, dwu1], tok_small)
    results = {}

    def sum_group(tag, names, fin, after_work):
        sent, landed = fin(after_work)
        plane = [_sum_plane("sum_" + n, g_, rb, chip_index) for n, g_, rb in zip(names, sent, landed)]
        tok_, swapped = _swap_split("swap_" + tag, plane, chip_index)
        return tok_, (names, swapped)

    def update_group(group, after_work):
        names, swapped = group
        mine, theirs = swapped(after_work)
        for n, m_, t_ in zip(names, mine, theirs):
            results[n] = _adamw("adamw_" + n, A[n], A['m_' + n], A['v_' + n], [m_, t_])

    tok_a, grp_ffn2 = sum_group("ffn2", big[3:6], scatter_fin1, tok_r3)
    tok_b, grp_mix = sum_group("mix", big[7:11], scatter_fin2a, tok_a)
    tok_c, grp_w_in = sum_group("w_in", big[6:7], scatter_fin2, tok_b)
    update_group(grp_ffn2, tok_c)
    tok_d, grp_down = sum_group("ffn1_down", big[2:3], early['fin'], results['w_ffn2_down'][0])
    update_group(grp_mix, tok_d)
    update_group(grp_w_in, results['w_out'][0])
    update_group(grp_down, results['w_in'][0])
    allp = small_gathered(results['w_ffn1_down'][0])
    head_rows = -(-18 * D // PACK_W)
    head = allp[:, :head_rows].reshape(N_DEV, head_rows * PACK_W)
    dmx_all = head[:, :9 * D]

    def sum_rows_fn(i, t):
        s_ = t[0:1]
        for k in range(1, N_DEV):
            s_ = s_ + t[k:k + 1]
        return [s_]

    dmc_sum = _rowk("sum_dmod_c", sum_rows_fn, 1, 1, [(head[:, 9 * D:18 * D], 'vec')], [((1, 9 * D), F32, 'row')])[0]
    cots = jnp.concatenate([dmx_all, dmc_sum, jnp.zeros((7, 9 * D), F32)], axis=0)
    cots_sh = lax.dynamic_slice(cots, (0, chip * NM), (16, NM))
    part = _mm("cctx_part", [(cots_sh[8:16], wm, NM)], 8, D, tm=8, tn=_div(D, 1024), nk=NM // _div(NM, 1152), tb=True,
               epi=ident, outs=[(F32, False)], a_pro=to_bf, b_pro=to_bf)[0]
    _, cctx_gathered = _allgather_split("gather_cctx", part, me, part)

    def sum_dev_fn(i, t):
        s_ = t[0]
        for k in range(1, N_DEV):
            s_ = s_ + t[k]
        return [s_]

    tot = _rowk("sum_small", sum_dev_fn, RP, 8, [(allp, 'row3')], [((RP, PACK_W), F32, 'row')])[0]
    (t_dmod_x, t_dmod_c, t_ng1, t_ng2, t_ng3, t_qg, t_kg, t_lr0, t_lr1, t_li0, t_li1, t_kr0, t_kr1, t_ki0, t_ki1,
     t_dbf0, t_dbf1, t_dcr0, t_dcr1, t_dci0, t_dci1, t_d, t_bglu) = _unpack(tot, shapes)
    b_grad = lambda t, lo: jnp.transpose(t[:, :, lo:lo + P].reshape(G, E, P), (0, 2, 1))
    c_grad = lambda t: jnp.transpose(t.reshape(nslab, P, SLAB_GROUPS, E), (0, 2, 3, 1)).reshape(G, E, P)
    cat2 = lambda u0, u1: jnp.concatenate([u0.reshape(G, P), u1.reshape(G, P)], axis=0)
    g_are, g_aim, g_ldt = _zoh_bwd(a_re2, a_im2, ldt2, [cat2(t_lr0, t_lr1), cat2(t_li0, t_li1),
                                                         cat2(t_kr0, t_kr1), cat2(t_ki0, t_ki1)])
    g_bmod = _rowk("bmod_grad", lambda i, u0, u1: [u0 + u1], 1, 1, [(t_dmod_x, 'row'), (t_dmod_c, 'row')],
                   [((1, 9 * D), F32, 'row')])[0]
    results['w_mod'] = tuple(_adamw_outer("adamw_w_mod", w_mod, m_w_mod, v_w_mod, acts, cots_sh))
    done = sum(results[n][1].reshape(-1, results[n][1].shape[-1])[0:1, 0:1] for n in list(results)) + g_are[0:1, 0:1] \
        + g_bmod[0:1, 0:1]
    tok_e, grp_up = sum_group("ffn1_up", big[0:2], scatter_fin3, done)
    parts = cctx_gathered(tok_e).reshape(N_CHIPS, 2, 8, D)[:, 0, 0]

    def cctx_fn(i, pt, ct):
        ds_ = ((pt[0:1] + pt[1:2]) + pt[2:3]) + pt[3:4]
        _, vjp = jax.vjp(lambda v: v * _sigmoid(v), ct)
        return [vjp(ds_)[0]]

    g_cctx = _rowk("cctx_grad", cctx_fn, 1, 1, [(parts, 'vec'), (c_ctx[None], 'row')], [((1, D), F32, 'row')])[0]

    ng_full = jnp.concatenate([t_ng1, t_ng2, t_ng3], axis=0)
    gsmall = {
        'c_ctx': g_cctx, 'b_mod': g_bmod, 'norm_g': lax.dynamic_slice(ng_full, (0, chip * Dq), (3, Dq)),
        'q_norm_g': t_qg, 'k_norm_g': t_kg, 'ssm_a_re': g_are, 'ssm_a_im': g_aim, 'ssm_log_dt': g_ldt,
        'ssm_b_re': jnp.stack([b_grad(t_dbf0, 0), b_grad(t_dbf1, 0)]),
        'ssm_b_im': jnp.stack([b_grad(t_dbf0, P), b_grad(t_dbf1, P)]),
        'ssm_c_re': jnp.stack([c_grad(t_dcr0), c_grad(t_dcr1)]), 'ssm_c_im': jnp.stack([c_grad(t_dci0), c_grad(t_dci1)]),
        'ssm_d': t_d, 'b_glu': t_bglu}
    sshapes = [A[n].shape for n in small]
    sres = _adamw("adamw_small", packs_wmv[0], packs_wmv[1], packs_wmv[2], [_pack([gsmall[n] for n in small])])
    update_group(grp_up, sres[0])
    sres = [_unpack(b_, sshapes) for b_ in sres]
    for k, n in enumerate(small):
        results[n] = tuple(sres[q][k] for q in range(4))
    as2d = lambda v: v.reshape(1, -1) if v.ndim == 1 else v
    tres = _adamw_whole("adamw_tiny", [as2d(A[n]) for n in tiny], [as2d(A['m_' + n]) for n in tiny],
                        [as2d(A['v_' + n]) for n in tiny], [gsmall[n].reshape(as2d(A[n]).shape) for n in tiny])
    for n, res in zip(tiny, tres):
        results[n] = res

    order = ['c_ctx', 'w_mod', 'b_mod', 'norm_g', 'w_ffn1_gate', 'w_ffn1_up', 'w_ffn1_down', 'w_in', 'q_norm_g',
             'k_norm_g', 'ssm_a_re', 'ssm_a_im', 'ssm_log_dt', 'ssm_b_re', 'ssm_b_im', 'ssm_c_re', 'ssm_c_im',
             'ssm_d', 'w_glu', 'b_glu', 'w_br_attn', 'w_br_ssm', 'w_out', 'w_ffn2_gate', 'w_ffn2_up', 'w_ffn2_down']
    outs = [loss, grad_x]
    for q in range(4):
        outs += [results[n][q].reshape(A[n].shape) for n in order]
    return tuple(outs)
```

```python
import math

import jax
import jax.numpy as jnp
from jax import lax
from jax.experimental import pallas as pl
from jax.experimental.pallas import tpu as pltpu

F32 = jnp.float32
BF16 = jnp.bfloat16
MESH = pl.DeviceIdType.MESH

NORM_EPS = 1e-6
ROPE_THETA = 10000.0
GRID_W = 64
HEAD_DIM = 128
Q_PER_KV = 4
SSM_GROUP = 16
SSM_STATE = 64
ADAM_LR = 0.001
ADAM_B1 = 0.9
ADAM_B2 = 0.999
ADAM_EPS = 1e-08
ADAM_WD = 0.01
ADAM_STEP = 10

N_CHIPS = 4
N_DEV = 8
LANES = 128
SLAB_CH = 128
SLAB_GROUPS = SLAB_CH // SSM_GROUP
SLAB_ST = SLAB_GROUPS * SSM_STATE
VMEM_LIMIT_BYTES = 56 * 1024 * 1024
PACK_W = 1024


def _cparams(**kw):
    return pltpu.CompilerParams(vmem_limit_bytes=VMEM_LIMIT_BYTES, **kw)


def _div(n, pref, mult=LANES):
    t = (min(pref, n) // mult) * mult
    while t >= mult:
        if n % t == 0:
            return t
        t -= mult
    return n


def _sigmoid(x):
    return jax.nn.sigmoid(x)


def _gelu(x):
    return x * (0.5 * (1.0 + jnp.tanh(math.sqrt(2.0 / math.pi) * (x + 0.044715 * (x * x * x)))))


def _rowk(name, fn, nrows, tr, ins, outs, nc=0, after=()):
    nt = nrows // tr
    in_specs, arrays = [], []
    for arr, kind in ins:
        arrays.append(arr)
        if kind == 'row':
            in_specs.append(pl.BlockSpec((tr, arr.shape[1]), lambda i: (i, 0)))
        elif kind == 'xrow':
            in_specs.append(pl.BlockSpec((tr, arr.shape[1]), lambda i: (jnp.maximum(i - nc, 0), 0)))
        elif kind == 'orow':
            in_specs.append(pl.BlockSpec((tr, arr.shape[1]), lambda i: (i + nc, 0)))
        elif kind == 'vec':
            in_specs.append(pl.BlockSpec(arr.shape, lambda i, nd=arr.ndim: (0,) * nd))
        elif kind == 'row3':
            in_specs.append(pl.BlockSpec((arr.shape[0], tr, arr.shape[2]), lambda i: (0, i, 0)))
        elif kind == 'row1':
            in_specs.append(pl.BlockSpec((None, tr, arr.shape[2]), lambda i: (0, i, 0)))
        elif kind[0] == 'ocol':
            _, width, blk = kind
            in_specs.append(pl.BlockSpec((tr, width), lambda i, blk=blk: (i + nc, blk)))
        else:
            _, width, blk = kind
            in_specs.append(pl.BlockSpec((tr, width), lambda i, blk=blk: (i, blk)))
    out_shape, out_specs = [], []
    for shape, dtype, kind in outs:
        out_shape.append(jax.ShapeDtypeStruct(shape, dtype))
        if kind == 'row':
            out_specs.append(pl.BlockSpec((tr, shape[1]), lambda i: (i, 0)))
        elif kind == 'row1':
            out_specs.append(pl.BlockSpec((None, tr, shape[2]), lambda i: (0, i, 0)))
        else:
            out_specs.append(pl.BlockSpec(shape, lambda i, nd=len(shape): (0,) * nd))
    nin = len(ins)
    for arr in after:
        arrays.append(arr)
        in_specs.append(pl.BlockSpec(memory_space=pl.ANY))
    nafter = len(after)

    def body(*refs):
        i = pl.program_id(0)
        res = fn(i, *[r[...] for r in refs[:nin]])
        for (shape, dtype, kind), ref, val in zip(outs, refs[nin + nafter:], res):
            if kind in ('row', 'row1'):
                ref[...] = val.astype(dtype)
            else:
                @pl.when(i == 0)
                def _():
                    ref[...] = val.astype(dtype)

                @pl.when(i > 0)
                def _():
                    ref[...] += val.astype(dtype)

    return pl.pallas_call(body, name=name, grid=(nt,), in_specs=in_specs, out_specs=out_specs,
                          out_shape=out_shape, compiler_params=_cparams())(*arrays)


def _mm(name, pairs, M, N, *, tm, tn, nk=1, epi, outs, ta=False, tb=False, rows=(), vecs=(),
        a_pro=None, b_pro=None, n_outer=True, summed=False):
    nm, nn = M // tm, N // tn
    npair = len(pairs)

    def idx(f):
        if n_outer:
            return lambda j, i, k: f(i, j, k)
        return lambda i, j, k: f(i, j, k)

    in_specs, args = [], []
    for a, b, K in pairs:
        tk = K // nk
        if ta:
            in_specs.append(pl.BlockSpec((tk, tm), idx(lambda i, j, k: (k, i))))
        else:
            in_specs.append(pl.BlockSpec((tm, tk), idx(lambda i, j, k: (i, k))))
        args.append(a)
        if b.ndim == 3:
            if tb:
                per = b.shape[2] // tk
                in_specs.append(pl.BlockSpec((None, tn, tk), idx(lambda i, j, k, per=per: (k // per, j, k % per))))
            else:
                per = b.shape[2] // tn
                in_specs.append(pl.BlockSpec((None, tk, tn), idx(lambda i, j, k, per=per: (j // per, k, j % per))))
        elif tb:
            in_specs.append(pl.BlockSpec((tn, tk), idx(lambda i, j, k: (j, k))))
        else:
            in_specs.append(pl.BlockSpec((tk, tn), idx(lambda i, j, k: (k, j))))
        args.append(b)
    for arr, ro, co in rows:
        in_specs.append(pl.BlockSpec((tm, tn), idx(lambda i, j, k, ro=ro, co=co: (i + ro, j + co))))
        args.append(arr)
    for arr in vecs:
        in_specs.append(pl.BlockSpec((arr.shape[0], tn), idx(lambda i, j, k: (0, j))))
        args.append(arr)
    out_shape, out_specs = [], []
    for dtype, chunked in outs:
        if chunked:
            per = (N // N_CHIPS) // tn
            out_shape.append(jax.ShapeDtypeStruct((N_CHIPS, M, N // N_CHIPS), dtype))
            out_specs.append(pl.BlockSpec((None, tm, tn), idx(lambda i, j, k, per=per: (j // per, i, j % per))))
        else:
            out_shape.append(jax.ShapeDtypeStruct((M, N), dtype))
            out_specs.append(pl.BlockSpec((tm, tn), idx(lambda i, j, k: (i, j))))
    nacc = 1 if summed else npair
    scratch = [pltpu.VMEM((tm, tn), F32) for _ in range(nacc)] if nk > 1 else []
    nrow, nvec, nout = len(rows), len(vecs), len(outs)
    dims = (((0 if ta else 1,), (1 if tb else 0,)), ((), ()))

    def body(*refs):
        ab = refs[:2 * npair]
        row_refs = refs[2 * npair:2 * npair + nrow]
        vec_refs = refs[2 * npair + nrow:2 * npair + nrow + nvec]
        out_refs = refs[2 * npair + nrow + nvec:2 * npair + nrow + nvec + nout]
        acc_refs = refs[2 * npair + nrow + nvec + nout:]
        if n_outer:
            j, i, k = pl.program_id(0), pl.program_id(1), pl.program_id(2)
        else:
            i, j, k = pl.program_id(0), pl.program_id(1), pl.program_id(2)

        def part(p):
            av, bv = ab[2 * p][...], ab[2 * p + 1][...]
            if a_pro is not None:
                av = a_pro(av)
            if b_pro is not None:
                bv = b_pro(bv)
            return lax.dot_general(av, bv, dims, preferred_element_type=F32)

        def finish(accs):
            row_index = i * tm + lax.broadcasted_iota(jnp.int32, (tm, 1), 0)
            res = epi(accs, [r[...] for r in row_refs], [v[...] for v in vec_refs], row_index)
            for ref, val in zip(out_refs, res):
                ref[...] = val.astype(ref.dtype)

        parts = [part(p) for p in range(npair)]
        if summed:
            total = parts[0]
            for extra in parts[1:]:
                total = total + extra
            parts = [total]
        if nk == 1:
            finish(parts)
        else:
            @pl.when(k == 0)
            def _():
                for q in range(nacc):
                    acc_refs[q][...] = parts[q]

            @pl.when(jnp.logical_and(k > 0, k < nk - 1))
            def _():
                for q in range(nacc):
                    acc_refs[q][...] += parts[q]

            @pl.when(k == nk - 1)
            def _():
                finish([acc_refs[q][...] + parts[q] for q in range(nacc)])

    grid = (nn, nm, nk) if n_outer else (nm, nn, nk)
    return pl.pallas_call(body, name=name, grid=grid, in_specs=in_specs, out_specs=out_specs,
                          out_shape=out_shape, scratch_shapes=scratch, compiler_params=_cparams())(*args)


def _split3(v):
    v0 = v.astype(BF16)
    r1 = v - v0.astype(F32)
    v1 = r1.astype(BF16)
    v2 = (r1 - v1.astype(F32)).astype(BF16)
    return v0, v1, v2


def _mesh_pos():
    return lax.axis_index("x"), lax.axis_index("y"), lax.axis_index("c")


def _allgather_small(name, x):
    m, n = x.shape

    def body(x_ref, out_ref, send_sems, recv_sems, local_sem):
        xi, yi, ci = _mesh_pos()
        me, sibling = (xi, yi, ci), (xi, yi, 1 - ci)
        chips = [(1 - xi, yi), (xi, 1 - yi), (1 - xi, 1 - yi)]

        def rows(px, py, pc):
            return out_ref.at[pl.ds((4 * px + 2 * py + pc) * m, m), :]

        def copy(k, block, to, src=None):
            return pltpu.make_async_remote_copy(
                src_ref=rows(*block) if src is None else src, dst_ref=rows(*block),
                send_sem=send_sems.at[k], recv_sem=recv_sems.at[k], device_id=to, device_id_type=MESH)

        mine = pltpu.make_async_copy(x_ref, rows(*me), local_sem)
        mine.start()
        first = [copy(0, me, sibling, src=x_ref)]
        first += [copy(1 + j, me, (*chip, ci), src=x_ref) for j, chip in enumerate(chips)]
        for cp in first:
            cp.start()
        passed = [copy(4 + j, (*chip, ci), sibling) for j, chip in enumerate(chips)]
        for j, chip in enumerate(chips):
            copy(1 + j, (*chip, ci), me).wait_recv()
            passed[j].start()
        copy(0, sibling, me).wait_recv()
        for j, chip in enumerate(chips):
            copy(4 + j, (*chip, 1 - ci), me).wait_recv()
        for cp in first + passed:
            cp.wait_send()
        mine.wait()

    return pl.pallas_call(
        body, name=name, out_shape=jax.ShapeDtypeStruct((N_DEV * m, n), x.dtype),
        in_specs=[pl.BlockSpec(memory_space=pltpu.VMEM)], out_specs=pl.BlockSpec(memory_space=pltpu.VMEM),
        scratch_shapes=[pltpu.SemaphoreType.DMA((7,)), pltpu.SemaphoreType.DMA((7,)), pltpu.SemaphoreType.DMA],
        compiler_params=_cparams())(x)


_HBM = pl.BlockSpec(memory_space=pltpu.HBM)
_SEM = pl.BlockSpec(memory_space=pltpu.SEMAPHORE)
_ANY = pl.BlockSpec(memory_space=pl.ANY)
_EFFECT = pltpu.SideEffectType.DATAFLOW_SIDE_EFFECTING


def _in_hbm(v):
    return pltpu.with_memory_space_constraint(v, pltpu.HBM)


def _other_chips(xi, yi):
    return [(1 - xi, yi), (xi, 1 - yi), (1 - xi, 1 - yi)]


def _guarded(core, fn):
    if core is None:
        fn()
    else:
        pl.when(lax.axis_index("c") == core)(fn)


def _split_copies(name, srcs, lands, after, pairs, senders, receivers, ncopy):
    ns, nl = len(srcs), len(lands)
    dma = pltpu.SemaphoreType.DMA((ncopy,))
    thru = [pltpu.HBM(v.shape, v.dtype) for v in list(srcs) + list(lands)]

    def start_body(*refs):
        src_refs, land_refs = refs[:ns], refs[ns:ns + nl]
        descs = pairs(src_refs, land_refs, refs[ns + nl + 1], refs[ns + nl + 2])

        def go():
            for send, _ in descs:
                send.start()

        _guarded(senders, go)
        refs[-1][...] = jnp.zeros_like(refs[-1])

    res = pl.pallas_call(
        start_body, name=name + "_start",
        out_shape=(dma, dma, *thru, jax.ShapeDtypeStruct((8, LANES), F32)),
        in_specs=[_HBM] * (ns + nl) + [_ANY],
        out_specs=(_SEM, _SEM, *([_HBM] * (ns + nl)), pl.BlockSpec(memory_space=pltpu.VMEM)),
        input_output_aliases={k: 2 + k for k in range(ns + nl)},
        compiler_params=_cparams(has_side_effects=_EFFECT),
    )(*[_in_hbm(v) for v in srcs], *[_in_hbm(v) for v in lands], after)
    send_sems, recv_sems, token = res[0], res[1], res[-1]
    carried = res[2:2 + ns + nl]

    def finish(after_work):
        def wait_body(*refs):
            src_refs, land_refs = refs[:ns], refs[ns:ns + nl]
            descs = pairs(src_refs, land_refs, refs[ns + nl], refs[ns + nl + 1])

            def sent():
                for send, _ in descs:
                    send.wait_send()

            def landed():
                for _, recv in descs:
                    recv.wait_recv()

            _guarded(senders, sent)
            _guarded(receivers, landed)

        out = pl.pallas_call(
            wait_body, name=name + "_wait", out_shape=tuple(thru),
            in_specs=[_HBM] * (ns + nl) + [_SEM, _SEM, _ANY], out_specs=tuple([_HBM] * (ns + nl)),
            input_output_aliases={k: k for k in range(ns + nl)},
            compiler_params=_cparams(has_side_effects=_EFFECT),
        )(*carried, send_sems, recv_sems, after_work)
        return list(out[:ns]), list(out[ns:])

    return token, finish


def _cast_slot(name, w, chip_index, after):
    R, C = w.shape[1:]
    tr = _div(R, max(16, 524288 // C), mult=16)

    def body(chip_ref, w_ref, after_ref, o_ref):
        o_ref[...] = w_ref[...].astype(BF16)

    return pl.pallas_call(
        body, name=name, out_shape=jax.ShapeDtypeStruct((N_CHIPS, R, C), BF16),
        grid_spec=pltpu.PrefetchScalarGridSpec(
            num_scalar_prefetch=1, grid=(R // tr,),
            in_specs=[pl.BlockSpec((None, tr, C), lambda i, chip_ref: (0, i, 0)), _ANY],
            out_specs=pl.BlockSpec((None, tr, C), lambda i, chip_ref: (chip_ref[0], i, 0))),
        compiler_params=_cparams())(chip_index, w, after)


def _sum_plane(name, grads, landed, chip_index):
    R, C = grads.shape[1:]
    tr = _div(R, max(16, 1048576 // C), mult=16)

    def body(chip_ref, own_ref, land_ref, o_ref):
        o_ref[...] = ((own_ref[...].astype(F32) + land_ref[0].astype(F32)) + land_ref[1].astype(F32)) \
            + land_ref[2].astype(F32)

    return pl.pallas_call(
        body, name=name, out_shape=jax.ShapeDtypeStruct((R, C), F32),
        grid_spec=pltpu.PrefetchScalarGridSpec(
            num_scalar_prefetch=1, grid=(R // tr,),
            in_specs=[pl.BlockSpec((None, tr, C), lambda i, chip_ref: (chip_ref[0], i, 0)),
                      pl.BlockSpec((3, tr, C), lambda i, chip_ref: (0, i, 0))],
            out_specs=pl.BlockSpec((tr, C), lambda i, chip_ref: (i, 0))),
        compiler_params=_cparams())(chip_index, grads, landed)


def _gather_split(name, lands, after):
    def pairs(src_refs, land_refs, send_sems, recv_sems):
        xi, yi, _ = _mesh_pos()
        mine = 2 * xi + yi
        out = []
        for a in range(len(lands)):
            for j, (px, py) in enumerate(_other_chips(xi, yi)):
                def to_slot(slot, a=a, j=j, px=px, py=py):
                    return pltpu.make_async_remote_copy(
                        src_ref=land_refs[a].at[mine], dst_ref=land_refs[a].at[slot], send_sem=send_sems.at[3 * a + j],
                        recv_sem=recv_sems.at[3 * a + j], device_id=(px, py, 1), device_id_type=MESH)
                out.append((to_slot(mine), to_slot(2 * px + py)))
        return out

    return _split_copies(name, [], lands, after, pairs, senders=1, receivers=1, ncopy=3 * len(lands))


def _allgather_split(name, block, me, after):
    land = lax.dynamic_update_slice(lax.empty((N_DEV,) + block.shape, block.dtype), block[None], (me, 0, 0))

    def pairs(src_refs, land_refs, send_sems, recv_sems):
        xi, yi, ci = _mesh_pos()
        mine = 4 * xi + 2 * yi + ci
        out = []
        for k in range(1, N_DEV):
            kx, ky, kc = (k >> 2) & 1, (k >> 1) & 1, k & 1
            px = 1 - xi if kx else xi
            py = 1 - yi if ky else yi
            pc = 1 - ci if kc else ci

            def to_slot(slot, k=k, px=px, py=py, pc=pc):
                return pltpu.make_async_remote_copy(
                    src_ref=land_refs[0].at[mine], dst_ref=land_refs[0].at[slot], send_sem=send_sems.at[k - 1],
                    recv_sem=recv_sems.at[k - 1], device_id=(px, py, pc), device_id_type=MESH)
            out.append((to_slot(mine), to_slot(4 * px + 2 * py + pc)))
        return out

    tok, fin = _split_copies(name, [], [land], after, pairs, senders=None, receivers=None, ncopy=N_DEV - 1)
    return tok, lambda later: fin(later)[1][0]


def _swap_split(name, arrs, after):
    lands = [lax.empty(v.shape, v.dtype) for v in arrs]

    def pairs(src_refs, land_refs, send_sems, recv_sems):
        xi, yi, ci = _mesh_pos()
        out = []
        for a in range(len(arrs)):
            cp = pltpu.make_async_remote_copy(
                src_ref=src_refs[a], dst_ref=land_refs[a], send_sem=send_sems.at[a], recv_sem=recv_sems.at[a],
                device_id=(xi, yi, 1 - ci), device_id_type=MESH)
            out.append((cp, cp))
        return out

    return _split_copies(name, arrs, lands, after, pairs, senders=None, receivers=None, ncopy=len(arrs))


def _pass_split(name, lands, after):
    def pairs(src_refs, land_refs, send_sems, recv_sems):
        xi, yi, _ = _mesh_pos()
        out = []
        for a in range(len(lands)):
            for j, (px, py) in enumerate(_other_chips(xi, yi)):
                cp = pltpu.make_async_remote_copy(
                    src_ref=land_refs[a].at[2 * px + py], dst_ref=land_refs[a].at[2 * px + py],
                    send_sem=send_sems.at[3 * a + j], recv_sem=recv_sems.at[3 * a + j],
                    device_id=(xi, yi, 0), device_id_type=MESH)
                out.append((cp, cp))
        return out

    return _split_copies(name, [], lands, after, pairs, senders=1, receivers=0, ncopy=3 * len(lands))


def _scatter_split(name, grads, after):
    lands = [lax.empty((3,) + g.shape[1:], g.dtype) for g in grads]

    def pairs(src_refs, land_refs, send_sems, recv_sems):
        xi, yi, ci = _mesh_pos()
        out = []
        for a in range(len(grads)):
            for j, (px, py) in enumerate(_other_chips(xi, yi)):
                cp = pltpu.make_async_remote_copy(
                    src_ref=src_refs[a].at[2 * px + py], dst_ref=land_refs[a].at[j], send_sem=send_sems.at[3 * a + j],
                    recv_sem=recv_sems.at[3 * a + j], device_id=(px, py, ci), device_id_type=MESH)
                out.append((cp, cp))
        return out

    return _split_copies(name, grads, lands, after, pairs, senders=None, receivers=None, ncopy=3 * len(grads))


def _gather_finish(name, lands):
    na = len(lands)

    def body(*refs):
        outs = refs[na:2 * na]
        send_sems, recv_sems = refs[2 * na:]
        xi, yi, ci = _mesh_pos()
        passes = [pltpu.make_async_remote_copy(
            src_ref=outs[a].at[2 * px + py], dst_ref=outs[a].at[2 * px + py],
            send_sem=send_sems.at[a, j], recv_sem=recv_sems.at[a, j], device_id=(xi, yi, 0), device_id_type=MESH)
            for a in range(na) for j, (px, py) in enumerate(_other_chips(xi, yi))]

        @pl.when(ci == 1)
        def _():
            for cp in passes:
                cp.start()
            for cp in passes:
                cp.wait_send()

        @pl.when(ci == 0)
        def _():
            for cp in passes:
                cp.wait_recv()

    return pl.pallas_call(
        body, name=name, out_shape=[jax.ShapeDtypeStruct(v.shape, v.dtype) for v in lands],
        in_specs=[_ANY] * na, out_specs=[_ANY] * na,
        input_output_aliases={a: a for a in range(na)},
        scratch_shapes=[pltpu.SemaphoreType.DMA((na, 3)), pltpu.SemaphoreType.DMA((na, 3))],
        compiler_params=_cparams())(*lands)


ATTN_HEADS_PER_STEP = 2


def _attn_tiles(L, Lc, D, tq_pref=256):
    tq = min(tq_pref, Lc)
    return tq, L // tq, Lc // tq, D // HEAD_DIM // Q_PER_KV


def _attn_scores(q, k):
    return lax.dot_general(q, k, (((1,), (1,)), ((), ())), preferred_element_type=F32) * (HEAD_DIM ** -0.5)


def _softmax_rows(s):
    e = jnp.exp(s - jnp.max(s, axis=-1, keepdims=True))
    return e * (1.0 / jnp.sum(e, axis=-1, keepdims=True))


def _attn_probs(q, k):
    return _softmax_rows(_attn_scores(q, k))


def _attn_fwd(qr, kr, v, L, Lc, D):
    T = L + Lc
    tq, nq, qoff, nkv = _attn_tiles(L, Lc, D)
    hp = Q_PER_KV
    ng = Q_PER_KV // hp

    def body(q_ref, k_ref, v_ref, o_ref):
        k, vv = k_ref[...], v_ref[...]
        heads = [slice(r * HEAD_DIM, (r + 1) * HEAD_DIM) for r in range(hp)]
        scores = [_attn_scores(q_ref[:, cols], k) for cols in heads]
        probs = [_softmax_rows(s) for s in scores]
        for cols, p in zip(heads, probs):
            o_ref[:, cols] = jnp.dot(p.astype(BF16), vv, preferred_element_type=F32).astype(o_ref.dtype)

    kv_spec = pl.BlockSpec((T, HEAD_DIM), lambda h, r, q: (0, h))
    return pl.pallas_call(
        body, name="attn_fwd", grid=(nkv, ng, nq),
        in_specs=[pl.BlockSpec((tq, hp * HEAD_DIM), lambda h, r, q: (q + qoff, h * ng + r)), kv_spec, kv_spec],
        out_specs=pl.BlockSpec((tq, hp * HEAD_DIM), lambda h, r, q: (q, h * ng + r)),
        out_shape=jax.ShapeDtypeStruct((L, D), BF16), compiler_params=_cparams())(qr, kr, v)


def _attn_bwd(qr, kr, v, do, L, Lc, D):
    T = L + Lc
    tq, nq, qoff, nkv = _attn_tiles(L, Lc, D, 256)
    scale = HEAD_DIM ** -0.5
    hp = Q_PER_KV
    ng = Q_PER_KV // hp

    def body(q_ref, k_ref, v_ref, do_ref, dq_ref, dk_ref, dv_ref):
        first = jnp.logical_and(pl.program_id(1) == 0, pl.program_id(2) == 0)
        k, vv = k_ref[...], v_ref[...]
        nt_dims, tn_dims = (((1,), (1,)), ((), ())), (((0,), (0,)), ((), ()))
        heads = [slice(r * HEAD_DIM, (r + 1) * HEAD_DIM) for r in range(hp)]
        qs = [q_ref[:, cols] for cols in heads]
        douts = [do_ref[:, cols] for cols in heads]
        scores = [_attn_scores(q, k) for q in qs]
        dps = [lax.dot_general(dout, vv, nt_dims, preferred_element_type=F32) for dout in douts]
        probs = [_softmax_rows(s) for s in scores]
        dss = [(p * (dp - jnp.sum(p * dp, axis=-1, keepdims=True)) * scale).astype(BF16) for p, dp in zip(probs, dps)]
        for cols, ds in zip(heads, dss):
            dq_ref[:, cols] = jnp.dot(ds, k, preferred_element_type=F32)
        dk = dv = None
        for q, dout, p, ds in zip(qs, douts, probs, dss):
            dk_r = lax.dot_general(ds, q, tn_dims, preferred_element_type=F32)
            dv_r = lax.dot_general(p.astype(BF16), dout, tn_dims, preferred_element_type=F32)
            dk = dk_r if dk is None else dk + dk_r
            dv = dv_r if dv is None else dv + dv_r

        @pl.when(first)
        def _():
            dk_ref[...] = dk
            dv_ref[...] = dv

        @pl.when(jnp.logical_not(first))
        def _():
            dk_ref[...] += dk
            dv_ref[...] += dv

    kv_spec = pl.BlockSpec((T, HEAD_DIM), lambda h, r, q: (0, h))
    q_spec = pl.BlockSpec((tq, hp * HEAD_DIM), lambda h, r, q: (q + qoff, h * ng + r))
    o_spec = pl.BlockSpec((tq, hp * HEAD_DIM), lambda h, r, q: (q, h * ng + r))
    return pl.pallas_call(
        body, name="attn_bwd", grid=(nkv, ng, nq),
        in_specs=[q_spec, kv_spec, kv_spec, o_spec], out_specs=[o_spec, kv_spec, kv_spec],
        out_shape=[jax.ShapeDtypeStruct((L, D), F32), jax.ShapeDtypeStruct((T, D // Q_PER_KV), F32),
                   jax.ShapeDtypeStruct((T, D // Q_PER_KV), F32)],
        compiler_params=_cparams())(qr, kr, v, do)


SUB = 8


def _doubling(xr, xi, pw_re, pw_im, lanes, first_power, period, reverse):
    n = xr.shape[0]
    rows = lax.broadcasted_iota(jnp.int32, (n, 1), 0) & (period - 1)
    for k in range(period.bit_length() - 1):
        d = 1 << k
        keep = rows < period - d if reverse else rows >= d
        sr = jnp.where(keep, pltpu.roll(xr, n - d if reverse else d, 0), 0.0)
        si = jnp.where(keep, pltpu.roll(xi, n - d if reverse else d, 0), 0.0)
        pr, pi = pw_re[first_power + k:first_power + k + 1, lanes], pw_im[first_power + k:first_power + k + 1, lanes]
        xr, xi = xr + (pr * sr - pi * si), xi + (pr * si + pi * sr)
    return xr, xi


def _scan_tile(xr, xi, tb, lanes, reverse):
    pw_re, pw_im, w8_re, w8_im, wb_re, wb_im, carry_re, carry_im, sr, si = tb
    tt = xr.shape[0]
    nb = tt // SUB
    nq = sr.shape[0]
    cols = [slice(q * LANES, (q + 1) * LANES) for q in range(nq)]
    for q in range(nq):
        sr[q] = xr[:, cols[q]]
        si[q] = xi[:, cols[q]]
    order = list(range(SUB - 2, -1, -1)) if reverse else list(range(1, SUB))
    ends_r, ends_i = [], []
    for q in range(nq):
        ql = slice(lanes.start + q * LANES, lanes.start + (q + 1) * LANES)
        lr, li = pw_re[0:1, ql], pw_im[0:1, ql]
        first_row = pl.ds(SUB - 1 if reverse else 0, nb, stride=SUB)
        pr, pi = sr[q, first_row, :], si[q, first_row, :]
        for r in order:
            rows = pl.ds(r, nb, stride=SUB)
            pr, pi = sr[q, rows, :] + (lr * pr - li * pi), si[q, rows, :] + (lr * pi + li * pr)
            sr[q, rows, :] = pr
            si[q, rows, :] = pi
        ends_r.append(pr)
        ends_i.append(pi)
    er, ei = jnp.concatenate(ends_r, axis=1), jnp.concatenate(ends_i, axis=1)
    er, ei = _doubling(er, ei, pw_re, pw_im, lanes, 3, nb, reverse)
    car, cai = carry_re[:, lanes], carry_im[:, lanes]
    wbr, wbi = wb_re[:, lanes], wb_im[:, lanes]
    er = er + (wbr * car - wbi * cai)
    ei = ei + (wbr * cai + wbi * car)
    out_block = 0 if reverse else nb - 1
    carry_re[:, lanes] = er[out_block:out_block + 1, :]
    carry_im[:, lanes] = ei[out_block:out_block + 1, :]
    blocks = lax.broadcasted_iota(jnp.int32, (nb, 1), 0)
    first = blocks == (nb - 1 if reverse else 0)
    cr = jnp.where(first, car, pltpu.roll(er, nb - 1 if reverse else 1, 0))
    ci = jnp.where(first, cai, pltpu.roll(ei, nb - 1 if reverse else 1, 0))
    for r in range(SUB):
        wr, wi = w8_re[r:r + 1, lanes], w8_im[r:r + 1, lanes]
        add_r, add_i = wr * cr - wi * ci, wr * ci + wi * cr
        for q in range(nq):
            sr[q, pl.ds(r, nb, stride=SUB), :] += add_r[:, cols[q]]
            si[q, pl.ds(r, nb, stride=SUB), :] += add_i[:, cols[q]]
    hr = jnp.concatenate([sr[q] for q in range(nq)], axis=1)
    hi = jnp.concatenate([si[q] for q in range(nq)], axis=1)
    return hr, hi, car, cai


def _scan_scratch(tt, NS):
    nb = tt // SUB
    return [pltpu.VMEM((8, NS), F32), pltpu.VMEM((8, NS), F32), pltpu.VMEM((SUB, NS), F32), pltpu.VMEM((SUB, NS), F32),
            pltpu.VMEM((nb, NS), F32), pltpu.VMEM((nb, NS), F32), pltpu.VMEM((1, NS), F32), pltpu.VMEM((1, NS), F32),
            pltpu.VMEM((SLAB_ST // LANES, tt, LANES), F32), pltpu.VMEM((SLAB_ST // LANES, tt, LANES), F32)]


def _scan_init(lr, li, tb, reverse):
    pw_re, pw_im, w8_re, w8_im, wb_re, wb_im, carry_re, carry_im, sr, _ = tb
    nb = wb_re.shape[0]
    carry_re[...] = jnp.zeros_like(carry_re)
    carry_im[...] = jnp.zeros_like(carry_im)
    pr, pi = lr, li
    for k in range(3 + nb.bit_length() - 1):
        pw_re[k:k + 1, :] = pr
        pw_im[k:k + 1, :] = pi
        if k == 3:
            l8r, l8i = pr, pi
        pr, pi = pr * pr - pi * pi, 2.0 * pr * pi
    pr, pi = lr, li
    for r in range(SUB):
        row = SUB - 1 - r if reverse else r
        w8_re[row:row + 1, :] = pr
        w8_im[row:row + 1, :] = pi
        pr, pi = pr * lr - pi * li, pr * li + pi * lr
    pr, pi = l8r, l8i
    for b in range(nb):
        row = nb - 1 - b if reverse else b
        wb_re[row:row + 1, :] = pr
        wb_im[row:row + 1, :] = pi
        pr, pi = pr * l8r - pi * l8i, pr * l8i + pi * l8r


def _ssm_tiles(T, Lc):
    tt = min(128, Lc)
    return tt, T // tt, Lc // tt


def _ssm_fwd(name, u, bbd, cbd_re, cbd_im, lam_re, lam_im, coef_re, coef_im, Lc, reverse):
    T, W = u.shape
    nslab = W // SLAB_CH
    NS = nslab * SLAB_ST
    tt, nt, nc = _ssm_tiles(T, Lc)
    if reverse:
        tile = lambda s: jnp.where(s < nc, nc - 1 - s, nt - 1 - (s - nc))
    else:
        tile = lambda s: s

    def body(u_ref, b_ref, cr_ref, ci_ref, lr_ref, li_ref, kr_ref, ki_ref, hr_ref, hi_ref, y_ref, *tb):
        @pl.when(pl.program_id(0) == 0)
        def _():
            _scan_init(lr_ref[...], li_ref[...], tb, reverse)

        for j in range(nslab):
            lanes = slice(j * SLAB_ST, (j + 1) * SLAB_ST)
            bu = jnp.dot(u_ref[:, j * SLAB_CH:(j + 1) * SLAB_CH], b_ref[j], preferred_element_type=F32)
            br, bi = bu[:, :SLAB_ST], bu[:, SLAB_ST:]
            kr, ki = kr_ref[:, lanes], ki_ref[:, lanes]
            hr, hi, _, _ = _scan_tile(kr * br - ki * bi, kr * bi + ki * br, tb, lanes, reverse)
            hrb, hib = hr.astype(BF16), hi.astype(BF16)
            hr_ref[:, lanes] = hrb
            hi_ref[:, lanes] = hib
            y_ref[:, j * SLAB_CH:(j + 1) * SLAB_CH] = (
                jnp.dot(hrb, cr_ref[j], preferred_element_type=F32)
                - jnp.dot(hib, ci_ref[j], preferred_element_type=F32))

    whole3 = lambda arr: pl.BlockSpec(arr.shape, lambda s: (0, 0, 0))
    vec = pl.BlockSpec((1, NS), lambda s: (0, 0))
    return pl.pallas_call(
        body, name=name, grid=(nt,),
        in_specs=[pl.BlockSpec((tt, W), lambda s: (tile(s), 0)), whole3(bbd), whole3(cbd_re), whole3(cbd_im),
                  vec, vec, vec, vec],
        out_specs=[pl.BlockSpec((tt, NS), lambda s: (tile(s), 0)), pl.BlockSpec((tt, NS), lambda s: (tile(s), 0)),
                   pl.BlockSpec((tt, W), lambda s: (tile(s), 0))],
        out_shape=[jax.ShapeDtypeStruct((T, NS), BF16), jax.ShapeDtypeStruct((T, NS), BF16),
                   jax.ShapeDtypeStruct((T, W), F32)],
        scratch_shapes=_scan_scratch(tt, NS),
        compiler_params=_cparams())(u, bbd, cbd_re, cbd_im, lam_re, lam_im, coef_re, coef_im)


def _ssm_bwd(name, dy, h_re, h_im, u, bbd, bbdt_re, bbdt_im, cbdt_re, cbdt_im, lam_re, lam_im,
             coef_re, coef_im, Lc, reverse):
    T, W = u.shape
    nslab = W // SLAB_CH
    NS = nslab * SLAB_ST
    tt, nt, nc = _ssm_tiles(T, Lc)
    adj_reverse = not reverse
    if reverse:
        tile = lambda s: jnp.where(s < nt - nc, nc + s, s - (nt - nc))
    else:
        tile = lambda s: nt - 1 - s

    def body(dy_ref, hr_ref, hi_ref, u_ref, b_ref, btr_ref, bti_ref, ctr_ref, cti_ref, lr_ref, li_ref,
             kr_ref, ki_ref, du_ref, dlr_ref, dli_ref, dkr_ref, dki_ref, dbf_ref, dcrf_ref, dcif_ref,
             db_ref, dcr_ref, dci_ref, *tb):
        @pl.when(pl.program_id(0) == 0)
        def _():
            _scan_init(lr_ref[...], -li_ref[...], tb, adj_reverse)
            for ref in (dlr_ref, dli_ref, dkr_ref, dki_ref, db_ref, dcr_ref, dci_ref):
                ref[...] = jnp.zeros_like(ref)

        rows = lax.broadcasted_iota(jnp.int32, (tt, 1), 0)
        far_row = tt - 1 if adj_reverse else 0
        tn_dims = (((0,), (0,)), ((), ()))
        for j in range(nslab):
            lanes = slice(j * SLAB_ST, (j + 1) * SLAB_ST)
            chans = slice(j * SLAB_CH, (j + 1) * SLAB_CH)
            dys, us = dy_ref[:, chans], u_ref[:, chans]
            er = jnp.dot(dys, ctr_ref[j], preferred_element_type=F32)
            ei = -jnp.dot(dys, cti_ref[j], preferred_element_type=F32)
            ar, ai, car, cai = _scan_tile(er, ei, tb, lanes, adj_reverse)
            shift = tt - 1 if adj_reverse else 1
            nr = jnp.where(rows == far_row, car, pltpu.roll(ar, shift, 0))
            ni = jnp.where(rows == far_row, cai, pltpu.roll(ai, shift, 0))
            hrb, hib = hr_ref[:, lanes], hi_ref[:, lanes]
            hr, hi = hrb.astype(F32), hib.astype(F32)
            dlr_ref[:, lanes] += jnp.sum(nr * hr + ni * hi, axis=0, keepdims=True)
            dli_ref[:, lanes] += jnp.sum(ni * hr - nr * hi, axis=0, keepdims=True)
            bu = jnp.dot(us, b_ref[j], preferred_element_type=F32)
            br, bi = bu[:, :SLAB_ST], bu[:, SLAB_ST:]
            dkr_ref[:, lanes] += jnp.sum(ar * br + ai * bi, axis=0, keepdims=True)
            dki_ref[:, lanes] += jnp.sum(ai * br - ar * bi, axis=0, keepdims=True)
            kr, ki = kr_ref[:, lanes], ki_ref[:, lanes]
            dbr = (ar * kr + ai * ki).astype(BF16)
            dbi = (ai * kr - ar * ki).astype(BF16)
            du_ref[:, chans] = (jnp.dot(dbr, btr_ref[j], preferred_element_type=F32)
                                + jnp.dot(dbi, bti_ref[j], preferred_element_type=F32))
            db_ref[j, :, :SLAB_ST] += lax.dot_general(us, dbr, tn_dims, preferred_element_type=F32)
            db_ref[j, :, SLAB_ST:] += lax.dot_general(us, dbi, tn_dims, preferred_element_type=F32)
            dcr_ref[j] += lax.dot_general(hrb, dys, tn_dims, preferred_element_type=F32)
            dci_ref[j] -= lax.dot_general(hib, dys, tn_dims, preferred_element_type=F32)

        @pl.when(pl.program_id(0) == nt - 1)
        def _():
            def iota(shape, axis):
                return lax.broadcasted_iota(jnp.int32, shape, axis)

            sg, ss = SSM_GROUP.bit_length() - 1, SSM_STATE.bit_length() - 1
            b_mask = (iota((SLAB_CH, SLAB_ST), 0) >> sg) == (iota((SLAB_CH, SLAB_ST), 1) >> ss)
            c_mask = (iota((SLAB_ST, SLAB_CH), 0) >> ss) == (iota((SLAB_ST, SLAB_CH), 1) >> sg)
            fold = jnp.where((iota((SLAB_ST, SSM_STATE), 0) & (SSM_STATE - 1)) == iota((SLAB_ST, SSM_STATE), 1),
                             1.0, 0.0).astype(BF16)
            fold_t = jnp.where((iota((SSM_STATE, SLAB_ST), 1) & (SSM_STATE - 1)) == iota((SSM_STATE, SLAB_ST), 0),
                               1.0, 0.0).astype(BF16)

            def exact_dot(a, b, a_is_value):
                terms = _split3(a if a_is_value else b)
                acc = None
                for t in terms:
                    part = jnp.dot(t, b, preferred_element_type=F32) if a_is_value else jnp.dot(a, t, preferred_element_type=F32)
                    acc = part if acc is None else acc + part
                return acc

            for j in range(nslab):
                dbj = db_ref[j]
                dbf_ref[j, :, :SSM_STATE] = exact_dot(jnp.where(b_mask, dbj[:, :SLAB_ST], 0.0), fold, True)
                dbf_ref[j, :, SSM_STATE:] = exact_dot(jnp.where(b_mask, dbj[:, SLAB_ST:], 0.0), fold, True)
                dcrf_ref[j] = exact_dot(fold_t, jnp.where(c_mask, dcr_ref[j], 0.0), False)
                dcif_ref[j] = exact_dot(fold_t, jnp.where(c_mask, dci_ref[j], 0.0), False)

    whole3 = lambda arr: pl.BlockSpec(arr.shape, lambda s: (0, 0, 0))
    vec = pl.BlockSpec((1, NS), lambda s: (0, 0))
    row_w = pl.BlockSpec((tt, W), lambda s: (tile(s), 0))
    row_s = pl.BlockSpec((tt, NS), lambda s: (tile(s), 0))
    dbf = jax.ShapeDtypeStruct((nslab, SLAB_CH, 2 * SSM_STATE), F32)
    dcf = jax.ShapeDtypeStruct((nslab, SSM_STATE, SLAB_CH), F32)
    return pl.pallas_call(
        body, name=name, grid=(nt,),
        in_specs=[row_w, row_s, row_s, row_w, whole3(bbd), whole3(bbdt_re), whole3(bbdt_im), whole3(cbdt_re),
                  whole3(cbdt_im), vec, vec, vec, vec],
        out_specs=[row_w, vec, vec, vec, vec, whole3(dbf), whole3(dcf), whole3(dcf)],
        out_shape=[jax.ShapeDtypeStruct((T, W), F32)] + [jax.ShapeDtypeStruct((1, NS), F32)] * 4 + [dbf, dcf, dcf],
        scratch_shapes=[pltpu.VMEM(bbd.shape, F32), pltpu.VMEM(bbdt_re.shape, F32), pltpu.VMEM(bbdt_re.shape, F32)]
        + _scan_scratch(tt, NS),
        compiler_params=_cparams())(dy, h_re, h_im, u, bbd, bbdt_re, bbdt_im, cbdt_re, cbdt_im,
                                    lam_re, lam_im, coef_re, coef_im)


def _zoh_math(a_re, a_im, log_dt):
    dt = jnp.exp(log_dt)
    mag = jnp.exp(a_re * dt)
    lb_re = mag * jnp.cos(a_im * dt)
    lb_im = mag * jnp.sin(a_im * dt)
    den = a_re * a_re + a_im * a_im
    coef_re = ((lb_re - 1.0) * a_re + lb_im * a_im) / den
    coef_im = (lb_im * a_re - (lb_re - 1.0) * a_im) / den
    return lb_re, lb_im, coef_re, coef_im


def _zoh_fwd(a_re, a_im, log_dt):
    def body(ar, ai, ld, o0, o1, o2, o3):
        for ref, val in zip((o0, o1, o2, o3), _zoh_math(ar[...], ai[...], ld[...])):
            ref[...] = val

    return pl.pallas_call(body, name="zoh_fwd", out_shape=[jax.ShapeDtypeStruct(a_re.shape, F32)] * 4,
                          compiler_params=_cparams())(a_re, a_im, log_dt)


def _zoh_bwd(a_re, a_im, log_dt, cots):
    def body(ar, ai, ld, c0, c1, c2, c3, o0, o1, o2):
        _, vjp = jax.vjp(_zoh_math, ar[...], ai[...], ld[...])
        for ref, val in zip((o0, o1, o2), vjp((c0[...], c1[...], c2[...], c3[...]))):
            ref[...] = val

    return pl.pallas_call(
        body, name="zoh_bwd",
        out_shape=[jax.ShapeDtypeStruct(a_re.shape, F32), jax.ShapeDtypeStruct(a_re.shape, F32),
                   jax.ShapeDtypeStruct(log_dt.shape, F32)],
        compiler_params=_cparams())(a_re, a_im, log_dt, *cots)


def _adamw_outer(name, w, m, v, acts, cots):
    D, N = w.shape[1:]
    tm = LANES
    dims = (((0,), (0,)), ((), ()))

    def body(a_ref, b_ref, w_ref, m_ref, v_ref, g_ref, d_ref, nm_ref, nv_ref):
        a = a_ref[...]
        aa = _split3(a * _sigmoid(a))
        bb = _split3(b_ref[...])
        g = None
        for ia in range(3):
            for ib in range(3 - ia):
                t = lax.dot_general(aa[ia], bb[ib], dims, preferred_element_type=F32)
                g = t if g is None else g + t
        g_ref[...] = g
        d_ref[...], nm_ref[...], nv_ref[...] = _adamw_math(w_ref[...], g, m_ref[...], v_ref[...])

    tile = pl.BlockSpec((None, tm, N), lambda i: (0, i, 0))
    return pl.pallas_call(
        body, name=name, grid=(D // tm,),
        in_specs=[pl.BlockSpec((16, tm), lambda i: (0, i)), pl.BlockSpec((16, N), lambda i: (0, 0)), tile, tile, tile],
        out_specs=[tile] * 4, out_shape=[jax.ShapeDtypeStruct(w.shape, F32)] * 4,
        compiler_params=_cparams())(acts, cots, w, m, v)


def _adamw_math(w, g, m, v):
    m = ADAM_B1 * m + (1.0 - ADAM_B1) * g
    v = ADAM_B2 * v + (1.0 - ADAM_B2) * (g * g)
    m_hat = m / (1.0 - ADAM_B1 ** ADAM_STEP)
    v_hat = v / (1.0 - ADAM_B2 ** ADAM_STEP)
    delta = -ADAM_LR * (m_hat / (jnp.sqrt(v_hat) + ADAM_EPS) + ADAM_WD * w)
    return delta, m, v


def _adamw(name, w, m, v, gparts):
    R, C = w.shape[-2:]
    kind = 'row1' if w.ndim == 3 else 'row'
    tr = _div(R, max(8, 524288 // C), mult=8)

    def fn(i, wv, mv, vv, *gs):
        g = gs[0]
        for extra in gs[1:]:
            g = g + extra
        return (g,) + _adamw_math(wv, g, mv, vv)

    return _rowk(name, fn, R, tr, [(w, kind), (m, kind), (v, kind)] + [(g, 'row') for g in gparts],
                 [(w.shape, F32, kind)] * 4)


def _adamw_whole(name, ws, ms, vs, gs):
    n = len(ws)

    def body(*refs):
        for k in range(n):
            g = refs[3 * n + k][...]
            res = (g,) + _adamw_math(refs[k][...], g, refs[n + k][...], refs[2 * n + k][...])
            for q in range(4):
                refs[4 * n + 4 * k + q][...] = res[q]

    out = pl.pallas_call(
        body, name=name, out_shape=[jax.ShapeDtypeStruct(w.shape, F32) for w in ws for _ in range(4)],
        compiler_params=_cparams())(*ws, *ms, *vs, *gs)
    return [tuple(out[4 * k:4 * k + 4]) for k in range(n)]


def _pack(pieces, rows_mult=8):
    flat = jnp.concatenate([p.reshape(-1).astype(F32) for p in pieces])
    unit = rows_mult * PACK_W
    total = -(-flat.shape[0] // unit) * unit
    return jnp.pad(flat, (0, total - flat.shape[0])).reshape(total // PACK_W, PACK_W)


def _unpack(buf, shapes):
    flat = buf.reshape(-1)
    out, off = [], 0
    for s in shapes:
        n = math.prod(s)
        out.append(flat[off:off + n].reshape(s))
        off += n
    return out


def _bd_expand(t):
    S, g, a, b = t.shape
    eye = jnp.eye(g, dtype=t.dtype)
    return (t[:, :, :, None, :] * eye[None, :, None, :, None]).reshape(S, g * a, g * b)


def _rope_tables(L, Lc):
    rows = L // GRID_W
    row_ids = jnp.broadcast_to(jnp.arange(rows)[:, None], (rows, GRID_W)).reshape(-1).astype(F32)
    col_ids = jnp.broadcast_to(jnp.arange(GRID_W)[None, :], (rows, GRID_W)).reshape(-1).astype(F32)
    quarter = HEAD_DIM // 4
    inv_freq = ROPE_THETA ** (-jnp.arange(quarter, dtype=F32) / quarter)
    ang_r = row_ids[:, None] * inv_freq
    ang_c = col_ids[:, None] * inv_freq
    cos = jnp.concatenate([jnp.cos(ang_r), jnp.cos(ang_r), jnp.cos(ang_c), jnp.cos(ang_c)], axis=1)
    sin = jnp.concatenate([-jnp.sin(ang_r), jnp.sin(ang_r), -jnp.sin(ang_c), jnp.sin(ang_c)], axis=1)
    cos = jnp.concatenate([jnp.ones((Lc, HEAD_DIM), F32), cos], axis=0)
    sin = jnp.concatenate([jnp.zeros((Lc, HEAD_DIM), F32), sin], axis=0)
    return cos, sin


def _rot(v):
    lane = lax.broadcasted_iota(jnp.int32, (1, HEAD_DIM), 1)
    first = (lane % (HEAD_DIM // 2)) < (HEAD_DIM // 4)
    return jnp.where(first, pltpu.roll(v, HEAD_DIM - HEAD_DIM // 4, 1), pltpu.roll(v, HEAD_DIM // 4, 1))


def _head_norm(xh, g):
    return xh * lax.rsqrt(jnp.mean(xh * xh, axis=-1, keepdims=True) + NORM_EPS) * g


def _norm_mod(xv, g, sh, sc):
    r = lax.rsqrt(jnp.mean(xv * xv, axis=-1, keepdims=True) + NORM_EPS)
    return (xv * r) * g * (1.0 + sc) + sh


def kernel(x, c, ctx, c_ctx, w_mod, b_mod, norm_g, w_ffn1_gate, w_ffn1_up, w_ffn1_down, w_in, q_norm_g, k_norm_g, ssm_a_re, ssm_a_im, ssm_log_dt, ssm_b_re, ssm_b_im, ssm_c_re, ssm_c_im, ssm_d, w_glu, b_glu, w_br_attn, w_br_ssm, w_out, w_ffn2_gate, w_ffn2_up, w_ffn2_down, loss_target, m_c_ctx, m_w_mod, m_b_mod, m_norm_g, m_w_ffn1_gate, m_w_ffn1_up, m_w_ffn1_down, m_w_in, m_q_norm_g, m_k_norm_g, m_ssm_a_re, m_ssm_a_im, m_ssm_log_dt, m_ssm_b_re, m_ssm_b_im, m_ssm_c_re, m_ssm_c_im, m_ssm_d, m_w_glu, m_b_glu, m_w_br_attn, m_w_br_ssm, m_w_out, m_w_ffn2_gate, m_w_ffn2_up, m_w_ffn2_down, v_c_ctx, v_w_mod, v_b_mod, v_norm_g, v_w_ffn1_gate, v_w_ffn1_up, v_w_ffn1_down, v_w_in, v_q_norm_g, v_k_norm_g, v_ssm_a_re, v_ssm_a_im, v_ssm_log_dt, v_ssm_b_re, v_ssm_b_im, v_ssm_c_re, v_ssm_c_im, v_ssm_d, v_w_glu, v_b_glu, v_w_br_attn, v_w_br_ssm, v_w_out, v_w_ffn2_gate, v_w_ffn2_up, v_w_ffn2_down):
    A = dict(locals())
    xi, yi, ci = _mesh_pos()
    chip = 2 * xi + yi
    me = 4 * xi + 2 * yi + ci
    L, D = x.shape[1], x.shape[2]
    Lc = ctx.shape[1]
    T = L + Lc
    F4 = w_ffn1_gate.shape[2]
    F = N_CHIPS * F4
    W, KV, Dq = D // 2, D // 4, D // 4
    G = W // SSM_GROUP
    P, E = SSM_STATE, SSM_GROUP
    NS = G * P
    nslab = W // SLAB_CH
    tr = min(256, Lc)
    ncr = Lc // tr
    assert L % tr == 0 and Lc % tr == 0 and W % SLAB_CH == 0 and D % (4 * LANES) == 0

    def sel(i, v):
        return v if v.shape[0] == 1 else jnp.where(i < ncr, v[0:1], v[1:2])

    def put(i, v, nrow):
        if nrow == 1:
            return v
        which = (i >= ncr).astype(jnp.int32)
        r2 = lax.broadcasted_iota(jnp.int32, (nrow, 1), 0)
        return jnp.where(r2 == which, jnp.broadcast_to(v, (nrow, v.shape[1])), 0.0)

    ident = lambda accs, rows, vecs, ri: [accs[0]]

    NM = w_mod.shape[2]
    first = jnp.zeros((8, D), F32).at[0].set(c[0]).at[1:4, :Dq].set(norm_g[0])
    g0 = _allgather_small("gather_c", first).reshape(N_CHIPS, 2, 8, D)
    c_all = g0[:, :, 0].reshape(N_DEV, D)
    ng = jnp.transpose(g0[:, 0, 1:4, :Dq], (1, 0, 2)).reshape(3, D)
    acts = jnp.concatenate([c_all, c_ctx[None], jnp.zeros((7, D), F32)], axis=0)
    wm = w_mod[0]
    b_shard = lax.dynamic_slice(b_mod[0], (chip * NM,), (NM,))[None]
    silu_bf = lambda a: (a * _sigmoid(a)).astype(BF16)
    to_bf = lambda b: b.astype(BF16)
    mod_part = _mm("mod_fwd", [(acts, wm, D)], 16, NM, tm=16, tn=_div(NM, 1152),
                   epi=lambda accs, rows, vecs, ri: [accs[0] + vecs[0]], outs=[(F32, False)],
                   vecs=[b_shard], a_pro=silu_bf, b_pro=to_bf)[0]
    mg = _allgather_small("gather_mod", mod_part).reshape(N_CHIPS, 2, 16, NM)[:, 0]
    mod_all = jnp.transpose(mg, (1, 0, 2)).reshape(16, N_CHIPS * NM)
    mod_x = lax.dynamic_slice(mod_all, (me, 0), (1, 9 * D))
    mod_c = jnp.where(jnp.arange(9 * D)[None] < 5 * D, mod_all[8:9], 0.0)
    modv = jnp.concatenate([mod_c, mod_x], axis=0)
    mv = lambda k: modv[:, k * D:(k + 1) * D]
    sh1, sc1, g1, sh2, sc2 = mv(0), mv(1), mv(2), mv(3), mv(4)
    g2, sh3, sc3, g3 = mv(5)[1:2], mv(6)[1:2], mv(7)[1:2], mv(8)[1:2]

    big = ['w_ffn1_gate', 'w_ffn1_up', 'w_ffn1_down', 'w_ffn2_gate', 'w_ffn2_up', 'w_ffn2_down',
           'w_in', 'w_glu', 'w_br_attn', 'w_br_ssm', 'w_out']
    row_sharded = {'w_ffn1_down', 'w_ffn2_down', 'w_glu', 'w_br_attn', 'w_out'}
    groups = [big[0:2], big[2:3], big[6:7], big[7:11], big[3:6]]
    chip_index = jnp.reshape(chip, (1,)).astype(jnp.int32)
    tok, gather_finish = modv, []
    pin = c
    for gi, names in enumerate(groups):
        tok, fin = _gather_split("gather_w%d" % gi, [_cast_slot("cast_" + n, A[n], chip_index, pin) for n in names], tok)
        gather_finish.append(fin)
        pin = tok
    ng = ng + tok[0:1, 0:1]
    Wt = {}

    def register(names, full):
        for n, gw in zip(names, full):
            Wt[n] = gw.reshape(N_CHIPS * gw.shape[1], gw.shape[2]) if n in row_sharded else gw

    def weights_ready(gi, after_work):
        _, lands = gather_finish[gi](after_work)
        register(groups[gi], _gather_finish("gather_w%d_pass" % gi, lands))

    def weights_pass(gi, after_work):
        _, lands = gather_finish[gi](after_work)
        tok_, fin_ = _pass_split("gather_w%d_pass" % gi, lands, after_work)
        return tok_, lambda later: register(groups[gi], fin_(later)[1])

    a_re2, a_im2 = ssm_a_re[0].reshape(2 * G, P), ssm_a_im[0].reshape(2 * G, P)
    ldt2 = ssm_log_dt[0].reshape(2 * G, 1)
    zoh = _zoh_fwd(a_re2, a_im2, ldt2)
    lam_re, lam_im, coef_re, coef_im = [[z[d * G:(d + 1) * G].reshape(1, NS) for d in range(2)] for z in zoh]
    bd_b = lambda b: _bd_expand(jnp.transpose(b, (0, 2, 1)).reshape(nslab, SLAB_GROUPS, E, P))
    bd_c = lambda cc: _bd_expand(jnp.transpose(cc, (0, 2, 1)).reshape(nslab, SLAB_GROUPS, P, E))
    bbd, bbdt_re, bbdt_im, cbd_re, cbd_im, cbdt_re, cbdt_im = [], [], [], [], [], [], []
    for d in range(2):
        br_, bi_ = bd_b(ssm_b_re[0, d]).astype(BF16), bd_b(ssm_b_im[0, d]).astype(BF16)
        cr_, ci_ = bd_c(ssm_c_re[0, d]).astype(BF16), bd_c(ssm_c_im[0, d]).astype(BF16)
        bbd.append(jnp.concatenate([br_, bi_], axis=2))
        bbdt_re.append(jnp.transpose(br_, (0, 2, 1)))
        bbdt_im.append(jnp.transpose(bi_, (0, 2, 1)))
        cbd_re.append(cr_)
        cbd_im.append(ci_)
        cbdt_re.append(jnp.transpose(cr_, (0, 2, 1)))
        cbdt_im.append(jnp.transpose(ci_, (0, 2, 1)))
    cos_t, sin_t = _rope_tables(L, Lc)
    qg, kg = q_norm_g, k_norm_g
    tiny = ['c_ctx', 'b_mod', 'norm_g', 'q_norm_g', 'k_norm_g', 'ssm_a_re', 'ssm_a_im', 'ssm_log_dt', 'ssm_d', 'b_glu']
    small = ['ssm_b_re', 'ssm_b_im', 'ssm_c_re', 'ssm_c_im']
    packs_wmv = [_pack([A[pre + n] for n in small]) for pre in ('', 'm_', 'v_')]
    prepared = packs_wmv + [cos_t, sin_t, coef_im[0], coef_im[1]] + [
        t[d][0] for t in (bbd, bbdt_re, bbdt_im, cbd_re, cbd_im, cbdt_re, cbdt_im) for d in range(2)]
    weights_ready(0, tok + sum(t[0:1, 0:1].astype(F32) for t in prepared))

    def norm_mod(name, xv, g, sh, sc):
        rows = xv.shape[0]
        return _rowk(name, lambda i, xt, gt, sht, sct: [_norm_mod(xt, gt, sel(i, sht), sel(i, sct))],
                     rows, tr, [(xv, 'row'), (g, 'vec'), (sh, 'vec'), (sc, 'vec')], [((rows, D), BF16, 'row')])[0]

    def swiglu_epi(accs, rows, vecs, ri):
        a_, b_ = accs
        return [a_, b_, a_ * _sigmoid(a_) * b_]

    def res_epi(coef):
        def epi(accs, rows, vecs, ri):
            gate = vecs[0]
            if gate.shape[0] == 2:
                gate = jnp.where(ri < Lc, gate[0:1], gate[1:2])
            return [accs[0], rows[0] + (coef * gate) * accs[0]]
        return epi

    def ffn_fwd(tag, h, xres, gate, down_ready=None):
        rows = h.shape[0]
        a_, b_, s_ = _mm(tag + "_up", [(h, Wt['w_' + tag + '_gate'], D), (h, Wt['w_' + tag + '_up'], D)], rows, F,
                         tm=_div(rows, 512), tn=F4, epi=swiglu_epi, outs=[(BF16, False), (BF16, False), (BF16, False)])
        if down_ready is not None:
            down_ready(s_)
        f_, xo = _mm(tag + "_down", [(s_, Wt['w_' + tag + '_down'], F)], rows, D, tm=_div(rows, 768),
                     tn=_div(D, 512), epi=res_epi(0.5), outs=[(F32, False), (F32, False)],
                     rows=[(xres, 0, 0)], vecs=[gate])
        return a_, b_, s_, f_, xo

    xc = jnp.concatenate([ctx[0], x[0]], axis=0)
    h1 = norm_mod("norm1", xc, ng[0:1], sh1, sc1)
    a1, b1, s1, f1, x1 = ffn_fwd("ffn1", h1, xc, g1, down_ready=lambda s_: weights_ready(1, s_))
    weights_ready(2, x1)
    h2 = norm_mod("norm2", x1, ng[1:2], sh2, sc2)
    proj = _mm("in_proj", [(h2, Wt['w_in'], D)], T, 4 * D, tm=_div(T, 768), tn=_div(D, 1024), epi=ident,
               outs=[(F32, False)])[0]
    nh, nkvh = D // HEAD_DIM, KV // HEAD_DIM

    def prep_fn(i, kt, vt, ut, qt, qgt, kgt, ct, st):
        qs = [_head_norm(qt[:, h * HEAD_DIM:(h + 1) * HEAD_DIM], qgt) for h in range(nh)]
        ks = [_head_norm(kt[:, h * HEAD_DIM:(h + 1) * HEAD_DIM], kgt) for h in range(nkvh)]
        qs = [v * ct + _rot(v) * st for v in qs]
        ks = [v * ct + _rot(v) * st for v in ks]
        return [jnp.concatenate(qs, axis=1), jnp.concatenate(ks, axis=1), vt, ut]

    qr, kr, vb, ub = _rowk(
        "qk_prep", prep_fn, T, tr,
        [(proj, ('col', KV, 0)), (proj, ('col', KV, 1)), (proj, ('col', W, 1)), (proj, ('col', D, 1)),
         (qg, 'vec'), (kg, 'vec'), (cos_t, 'row'), (sin_t, 'row')],
        [((T, D), BF16, 'row'), ((T, KV), BF16, 'row'), ((T, KV), BF16, 'row'), ((T, W), BF16, 'row')])
    _, mixer_weights = weights_pass(3, qr)
    attn = _attn_fwd(qr, kr, vb, L, Lc, D)
    hs_re, hs_im, ys = [], [], []
    lam_in = lam_re[0]
    for d in range(2):
        hr_, hi_, y_ = _ssm_fwd("ssm_fwd%d" % d, ub, bbd[d], cbd_re[d], cbd_im[d], lam_in, lam_im[d],
                                coef_re[d], coef_im[d], Lc, reverse=bool(d))
        hs_re.append(hr_)
        hs_im.append(hi_)
        ys.append(y_)
        if d == 0:
            tok_p4, ffn2_weights = weights_pass(4, y_)
            lam_in = lam_re[1] + tok_p4[0:1, 0:1]
    mixer_weights(ys[1])

    def ssm_out_fn(i, y0, y1, ut, dt):
        pre = dt * ut + y0 + y1
        yg_ = _gelu(pre)
        return [pre, yg_, yg_]

    ssm_pre, yg, ygb = _rowk(
        "ssm_out", ssm_out_fn, L, tr,
        [(ys[0], 'orow'), (ys[1], 'orow'), (proj, ('ocol', W, 1)), (ssm_d, 'vec')],
        [((L, W), F32, 'row'), ((L, W), F32, 'row'), ((L, W), BF16, 'row')], nc=ncr)

    def glu_epi(accs, rows, vecs, ri):
        z_ = accs[0] + vecs[0]
        return [z_, rows[0] * _sigmoid(z_)]

    zglu, y2 = _mm("glu", [(ygb, Wt['w_glu'], W)], L, W, tm=_div(L, 512), tn=_div(W, 512), epi=glu_epi,
                   outs=[(F32, False), (BF16, False)], rows=[(yg, 0, 0)], vecs=[b_glu])
    tnm = _div(Dq, 512)

    def merge_epi(accs, rows, vecs, ri):
        ga, gs = _sigmoid(rows[0]), _sigmoid(rows[1])
        return [accs[0], accs[1], ga * accs[0] + gs * accs[1]]

    ba, bs, merged = _mm("merge", [(attn, Wt['w_br_attn'], D), (y2, Wt['w_br_ssm'], W)], L, D, tm=tr, tn=tnm,
                         epi=merge_epi, outs=[(F32, False), (F32, False), (BF16, False)],
                         rows=[(proj, ncr, 2 * D // tnm), (proj, ncr, 3 * D // tnm)])
    mix, x2 = _mm("out_proj", [(merged, Wt['w_out'], D)], L, D, tm=tr, tn=_div(D, 1024), epi=res_epi(1.0),
                  outs=[(F32, False), (F32, False)], rows=[(x1, ncr, 0)], vecs=[g2])
    ffn2_weights(x2)
    h3 = norm_mod("norm3", x2, ng[2:3], sh3, sc3)
    a3, b3, s3, f3, x3 = ffn_fwd("ffn2", h3, x2, g3)

    def loss_fn(i, yt, tt_, ft, gt):
        diff = yt - tt_
        dy_ = diff * (1.0 / D)
        return [dy_, jnp.sum(diff * diff, axis=0, keepdims=True), (0.5 * gt) * dy_,
                jnp.sum(dy_ * ft, axis=0, keepdims=True) * 0.5]

    dy, sq, df3, dg3 = _rowk("loss", loss_fn, L, tr, [(x3, 'row'), (loss_target[0], 'row'), (f3, 'row'), (g3, 'vec')],
                             [((L, D), F32, 'row'), ((1, D), F32, 'acc'), ((L, D), BF16, 'row'), ((1, D), F32, 'acc')])
    loss = lax.psum(0.5 * jnp.sum(sq) / D, ("x", "y", "c"))

    def swiglu_bwd_epi(accs, rows, vecs, ri):
        ds_, a_, b_ = accs[0], rows[0].astype(F32), rows[1].astype(F32)
        sg = _sigmoid(a_)
        return [ds_ * b_ * (sg * (1.0 + a_ * (1.0 - sg))), ds_ * (a_ * sg)]

    def norm_mod_bwd(name, xv, g, sh, sc, dh, dres, dres_kind, branch=None, after=()):
        rows, nrow = xv.shape[0], sh.shape[0]

        def fn(i, xt, gt, sht, sct, dht, rest, *more):
            _, vjp = jax.vjp(_norm_mod, xt, gt, sel(i, sht), sel(i, sct))
            dx_, dg_, dsh_, dsc_ = vjp(dht)
            dx_ = dx_ + (jnp.where(i >= ncr, rest, 0.0) if dres_kind == 'xrow' else rest)
            out = [dx_, dg_, put(i, dsh_, nrow), put(i, dsc_, nrow)]
            if branch is not None:
                ft, gatet = more
                out += [(branch[2] * sel(i, gatet)) * dx_,
                        put(i, jnp.sum(dx_ * ft, axis=0, keepdims=True) * branch[2], gatet.shape[0])]
            return out

        ins = [(xv, 'row'), (g, 'vec'), (sh, 'vec'), (sc, 'vec'), (dh, 'row'), (dres, dres_kind)]
        outs = [((rows, D), F32, 'row'), ((1, D), F32, 'acc'), ((nrow, D), F32, 'acc'), ((nrow, D), F32, 'acc')]
        if branch is not None:
            ins += [(branch[0], 'row'), (branch[1], 'vec')]
            outs += [((rows, D), BF16, 'row'), ((branch[1].shape[0], D), F32, 'acc')]
        return _rowk(name, fn, rows, tr, ins, outs, nc=ncr, after=after)

    def ffn_bwd(tag, df, h, a_, b_, s_, wg, wu, wd, on_dwd=None):
        rows = df.shape[0]
        dwd = _mm(tag + "_dwd", [(s_, df, rows)], F, D, tm=_div(F, 512), tn=_div(D, 1024), ta=True, epi=ident,
                  outs=[(BF16, False)])[0].reshape(N_CHIPS, F4, D)
        if on_dwd is not None:
            on_dwd(dwd)
        da, db = _mm(tag + "_dact", [(df, wd, D)], rows, F, tm=_div(rows, 512), tn=F4, tb=True, epi=swiglu_bwd_epi,
                     outs=[(BF16, False), (BF16, False)], rows=[(a_, 0, 0), (b_, 0, 0)])
        dwg = _mm(tag + "_dwg", [(h, da, rows)], D, F, tm=_div(D, 512), tn=F4, ta=True, epi=ident,
                  outs=[(BF16, True)])[0]
        dwu = _mm(tag + "_dwu", [(h, db, rows)], D, F, tm=_div(D, 512), tn=F4, ta=True, epi=ident,
                  outs=[(BF16, True)])[0]
        dh = _mm(tag + "_dh", [(da, wg, F), (db, wu, F)], rows, D, tm=_div(rows, 768), tn=_div(D, 1024), nk=N_CHIPS,
                 tb=True, epi=ident, outs=[(F32, False)], summed=True)[0]
        return dh, dwg, dwu, dwd

    dh3, dwg2, dwu2, dwd2 = ffn_bwd("ffn2", df3, h3, a3, b3, s3, Wt['w_ffn2_gate'], Wt['w_ffn2_up'], Wt['w_ffn2_down'])
    tok_r1, scatter_fin1 = _scatter_split("scatter_ffn2", [dwg2, dwu2, dwd2], dg3)
    dx2, dng3, dsh3, dsc3, dmix, dg2 = norm_mod_bwd("norm3_bwd", x2, ng[2:3], sh3, sc3, dh3, dy, 'row',
                                                    branch=(mix, g2 + tok_r1[0:1, 0:1], 1.0))

    def dmerge_epi(accs, rows, vecs, ri):
        dm_, ba_, bs_ = accs[0], rows[0], rows[1]
        ga, gs = _sigmoid(rows[2]), _sigmoid(rows[3])
        return [dm_ * ga, dm_ * gs, dm_ * ba_ * ga * (1.0 - ga), dm_ * bs_ * gs * (1.0 - gs)]

    tnd = _div(D, 1024)
    dba, dbs, dga, dgs = _mm("dmerge", [(dmix, Wt['w_out'], D)], L, D, tm=tr, tn=tnd, tb=True, epi=dmerge_epi,
                             outs=[(BF16, False)] * 4,
                             rows=[(ba, 0, 0), (bs, 0, 0), (proj, ncr, 2 * D // tnd), (proj, ncr, 3 * D // tnd)])
    dwout = _mm("dw_out", [(merged, dmix, L)], D, D, tm=_div(D, 512), tn=_div(D, 1024), ta=True, epi=ident,
                outs=[(BF16, False)])[0].reshape(N_CHIPS, Dq, D)
    dattn = _mm("dattn", [(dba, Wt['w_br_attn'], D)], L, D, tm=_div(L, 512), tn=_div(D, 1024), tb=True, epi=ident,
                outs=[(BF16, False)])[0]
    dwba = _mm("dw_br_attn", [(attn, dba, L)], D, D, tm=_div(D, 512), tn=_div(D, 1024), ta=True, epi=ident,
               outs=[(BF16, False)])[0].reshape(N_CHIPS, Dq, D)
    dy2 = _mm("dy2", [(dbs, Wt['w_br_ssm'], D)], L, W, tm=_div(L, 512), tn=_div(W, 1024), nk=N_CHIPS, tb=True,
              epi=ident, outs=[(F32, False)])[0]
    dwbs = _mm("dw_br_ssm", [(y2, dbs, L)], W, D, tm=_div(W, 512), tn=_div(Dq, 512), ta=True, epi=ident,
               outs=[(BF16, True)])[0]

    def glu_bwd_fn(i, d2, ygt, zt):
        sz = _sigmoid(zt)
        dz_ = d2 * ygt * sz * (1.0 - sz)
        return [dz_, d2 * sz, jnp.sum(dz_, axis=0, keepdims=True)]

    dz, dyd, dbglu = _rowk("glu_bwd", glu_bwd_fn, L, tr, [(dy2, 'row'), (yg, 'row'), (zglu, 'row')],
                           [((L, W), BF16, 'row'), ((L, W), F32, 'row'), ((1, W), F32, 'acc')])

    def dssm_epi(accs, rows, vecs, ri):
        _, vjp = jax.vjp(_gelu, rows[1])
        ds_ = vjp(accs[0] + rows[0])[0]
        return [ds_, ds_]

    dssm, dssm_b = _mm("dssm", [(dz, Wt['w_glu'], W)], L, W, tm=_div(L, 512), tn=_div(W, 512), tb=True, epi=dssm_epi,
                       outs=[(F32, False), (BF16, False)], rows=[(dyd, 0, 0), (ssm_pre, 0, 0)])
    dwglu = _mm("dw_glu", [(ygb, dz, L)], W, W, tm=_div(W, 512), tn=_div(W, 1024), ta=True, epi=ident,
                outs=[(BF16, False)])[0].reshape(N_CHIPS, W // N_CHIPS, W)
    tok_r2a, scatter_fin2a = _scatter_split("scatter_mix", [dwglu, dwba, dwbs, dwout], dbglu)
    dssm_full = jnp.concatenate([jnp.zeros((Lc, W), BF16), dssm_b], axis=0)
    dus, dlam_re, dlam_im, dcoef_re, dcoef_im, dbf, dcf_re, dcf_im = [], [], [], [], [], [], [], []
    for d in range(2):
        r = _ssm_bwd("ssm_bwd%d" % d, dssm_full, hs_re[d], hs_im[d], ub, bbd[d], bbdt_re[d], bbdt_im[d],
                     cbdt_re[d], cbdt_im[d], lam_re[d] + tok_r2a[0:1, 0:1], lam_im[d], coef_re[d], coef_im[d], Lc,
                     reverse=bool(d))
        for lst, val in zip((dus, dlam_re, dlam_im, dcoef_re, dcoef_im, dbf, dcf_re, dcf_im), r):
            lst.append(val)
    dqr, dkr, dvf = _attn_bwd(qr, kr, vb, dattn, L, Lc, D)

    def prep_bwd_fn(i, qt, kt, ut, dqt, dkt, dvt, du0, du1, dst, dgat, dgst, dt, qgt, kgt, ct, st):
        live = i >= ncr
        dqt = jnp.where(live, dqt, 0.0)
        dst = jnp.where(live, dst, 0.0)
        dgat = jnp.where(live, dgat, jnp.zeros_like(dgat))
        dgst = jnp.where(live, dgst, jnp.zeros_like(dgst))
        dqs, dks = [], []
        dqg_ = jnp.zeros((1, HEAD_DIM), F32)
        dkg_ = jnp.zeros((1, HEAD_DIM), F32)
        for h in range(nh):
            hl = slice(h * HEAD_DIM, (h + 1) * HEAD_DIM)
            dn = dqt[:, hl] * ct + _rot(dqt[:, hl] * st)
            _, vjp = jax.vjp(_head_norm, qt[:, hl], qgt)
            dxh, dgh = vjp(dn)
            dqs.append(dxh)
            dqg_ = dqg_ + dgh
        for h in range(nkvh):
            hl = slice(h * HEAD_DIM, (h + 1) * HEAD_DIM)
            dn = dkt[:, hl] * ct + _rot(dkt[:, hl] * st)
            _, vjp = jax.vjp(_head_norm, kt[:, hl], kgt)
            dxh, dgh = vjp(dn)
            dks.append(dxh)
            dkg_ = dkg_ + dgh
        du_ = du0 + du1 + dst * dt
        dproj_ = jnp.concatenate([c_.astype(BF16) for c_ in dks + [dvt, du_] + dqs + [dgat, dgst]], axis=1)
        return [dproj_, dqg_, dkg_, jnp.sum(dst * ut, axis=0, keepdims=True)]

    dproj, dqg, dkg, dssd = _rowk(
        "qk_prep_bwd", prep_bwd_fn, T, tr,
        [(proj, ('col', D, 1)), (proj, ('col', KV, 0)), (proj, ('col', W, 1)), (dqr, 'xrow'), (dkr, 'row'),
         (dvf, 'row'), (dus[0], 'row'), (dus[1], 'row'), (dssm, 'xrow'), (dga, 'xrow'), (dgs, 'xrow'), (ssm_d, 'vec'),
         (qg, 'vec'), (kg, 'vec'), (cos_t, 'row'), (sin_t, 'row')],
        [((T, 4 * D), BF16, 'row'), ((1, HEAD_DIM), F32, 'acc'), ((1, HEAD_DIM), F32, 'acc'), ((1, W), F32, 'acc')],
        nc=ncr)
    dh2 = _mm("in_proj_dx", [(dproj, Wt['w_in'], 4 * D)], T, D, tm=_div(T, 768), tn=_div(D, 1024), nk=N_CHIPS, tb=True,
              epi=ident, outs=[(F32, False)])[0]
    dwin = _mm("in_proj_dw", [(h2, dproj, T)], D, 4 * D, tm=_div(D, 512), tn=_div(D, 1024), ta=True, epi=ident,
               outs=[(BF16, True)])[0]
    tok_r2, scatter_fin2 = _scatter_split("scatter_w_in", [dwin], dqg)
    dx1, dng2, dsh2, dsc2, df1, dg1 = norm_mod_bwd("norm2_bwd", x1, ng[1:2] + tok_r2[0:1, 0:1], sh2, sc2, dh2, dx2,
                                                   'xrow', branch=(f1, g1, 0.5))
    early = {}

    def start_down(dwd):
        early['tok'], early['fin'] = _scatter_split("scatter_ffn1_down", [dwd], dg2)

    dh1, dwg1, dwu1, dwd1 = ffn_bwd("ffn1", df1, h1, a1, b1, s1, Wt['w_ffn1_gate'], Wt['w_ffn1_up'],
                                    Wt['w_ffn1_down'], on_dwd=start_down)
    dx0, dng1, dsh1, dsc1 = norm_mod_bwd("norm1_bwd", xc, ng[0:1] + early['tok'][0:1, 0:1], sh1, sc1, dh1, dx1, 'row')
    grad_x = dx0[Lc:][None]

    zD = jnp.zeros((1, D), F32)
    dmod_x = jnp.concatenate([dsh1[1:2], dsc1[1:2], dg1[1:2], dsh2[1:2], dsc2[1:2], dg2, dsh3, dsc3, dg3], axis=1)
    dmod_c = jnp.concatenate([dsh1[0:1], dsc1[0:1], dg1[0:1], dsh2[0:1], dsc2[0:1], zD, zD, zD, zD], axis=1)
    pieces = [dmod_x, dmod_c, dng1, dng2, dng3, dqg, dkg] + dlam_re + dlam_im + dcoef_re + dcoef_im + [dssd, dbglu]
    shapes = [p_.shape for p_ in pieces]
    pack = _pack(pieces)
    RP = pack.shape[0]
    pieces_b = dbf + dcf_re + dcf_im
    shapes_b = [p_.shape for p_ in pieces_b]
    pack_b = _pack(pieces_b, rows_mult=16).astype(BF16)
    RB = pack_b.shape[0]
    tok_small, small_gathered = _allgather_split("gather_small", pack, me, dng1)
    tok_small, small_gathered_b = _allgather_split("gather_small_b", pack_b, me, tok_small)
    tok_r3, scatter_fin3 = _scatter_split("scatter_ffn1_up", [dwg1, dwu1], tok_small)
    results = {}

    def sum_group(tag, names, fin, after_work):
        sent, landed = fin(after_work)
        plane = [_sum_plane("sum_" + n, g_, rb, chip_index) for n, g_, rb in zip(names, sent, landed)]
        tok_, swapped = _swap_split("swap_" + tag, plane, chip_index)
        return tok_, (names, swapped)

    def update_group(group, after_work):
        names, swapped = group
        mine, theirs = swapped(after_work)
        for n, m_, t_ in zip(names, mine, theirs):
            results[n] = _adamw("adamw_" + n, A[n], A['m_' + n], A['v_' + n], [m_, t_])

    tok_a, grp_ffn2 = sum_group("ffn2", big[3:6], scatter_fin1, tok_r3)
    tok_b, grp_mix = sum_group("mix", big[7:11], scatter_fin2a, tok_a)
    tok_c, grp_w_in = sum_group("w_in", big[6:7], scatter_fin2, tok_b)
    update_group(grp_ffn2, tok_c)
    tok_d, grp_down = sum_group("ffn1_down", big[2:3], early['fin'], results['w_ffn2_down'][0])
    update_group(grp_mix, tok_d)
    update_group(grp_w_in, results['w_out'][0])
    update_group(grp_down, results['w_in'][0])
    allp = small_gathered(results['w_ffn1_down'][0])
    head_rows = -(-18 * D // PACK_W)
    head = allp[:, :head_rows].reshape(N_DEV, head_rows * PACK_W)
    dmx_all = head[:, :9 * D]

    def sum_rows_fn(i, t):
        s_ = t[0:1]
        for k in range(1, N_DEV):
            s_ = s_ + t[k:k + 1]
        return [s_]

    dmc_sum = _rowk("sum_dmod_c", sum_rows_fn, 1, 1, [(head[:, 9 * D:18 * D], 'vec')], [((1, 9 * D), F32, 'row')])[0]
    cots = jnp.concatenate([dmx_all, dmc_sum, jnp.zeros((7, 9 * D), F32)], axis=0)
    cots_sh = lax.dynamic_slice(cots, (0, chip * NM), (16, NM))
    part = _mm("cctx_part", [(cots_sh[8:16], wm, NM)], 8, D, tm=8, tn=_div(D, 1024), nk=NM // _div(NM, 1152), tb=True,
               epi=ident, outs=[(F32, False)], a_pro=to_bf, b_pro=to_bf)[0]
    _, cctx_gathered = _allgather_split("gather_cctx", part, me, part)

    def sum_dev_fn(i, t):
        s_ = t[0].astype(F32)
        for k in range(1, N_DEV):
            s_ = s_ + t[k].astype(F32)
        return [s_]

    tot = _rowk("sum_small", sum_dev_fn, RP, 8, [(allp, 'row3')], [((RP, PACK_W), F32, 'row')])[0]
    (t_dmod_x, t_dmod_c, t_ng1, t_ng2, t_ng3, t_qg, t_kg, t_lr0, t_lr1, t_li0, t_li1, t_kr0, t_kr1, t_ki0, t_ki1,
     t_d, t_bglu) = _unpack(tot, shapes)
    allb = small_gathered_b(tot)
    tot_b = _rowk("sum_small_b", sum_dev_fn, RB, 16, [(allb, 'row3')], [((RB, PACK_W), F32, 'row')])[0]
    t_dbf0, t_dbf1, t_dcr0, t_dcr1, t_dci0, t_dci1 = _unpack(tot_b, shapes_b)
    b_grad = lambda t, lo: jnp.transpose(t[:, :, lo:lo + P].reshape(G, E, P), (0, 2, 1))
    c_grad = lambda t: jnp.transpose(t.reshape(nslab, P, SLAB_GROUPS, E), (0, 2, 3, 1)).reshape(G, E, P)
    cat2 = lambda u0, u1: jnp.concatenate([u0.reshape(G, P), u1.reshape(G, P)], axis=0)
    g_are, g_aim, g_ldt = _zoh_bwd(a_re2, a_im2, ldt2, [cat2(t_lr0, t_lr1), cat2(t_li0, t_li1),
                                                         cat2(t_kr0, t_kr1), cat2(t_ki0, t_ki1)])
    g_bmod = _rowk("bmod_grad", lambda i, u0, u1: [u0 + u1], 1, 1, [(t_dmod_x, 'row'), (t_dmod_c, 'row')],
                   [((1, 9 * D), F32, 'row')])[0]
    results['w_mod'] = tuple(_adamw_outer("adamw_w_mod", w_mod, m_w_mod, v_w_mod, acts, cots_sh))
    done = sum(results[n][1].reshape(-1, results[n][1].shape[-1])[0:1, 0:1] for n in list(results)) + g_are[0:1, 0:1] \
        + g_bmod[0:1, 0:1]
    tok_e, grp_up = sum_group("ffn1_up", big[0:2], scatter_fin3, done)
    parts = cctx_gathered(tok_e).reshape(N_CHIPS, 2, 8, D)[:, 0, 0]

    def cctx_fn(i, pt, ct):
        ds_ = ((pt[0:1] + pt[1:2]) + pt[2:3]) + pt[3:4]
        _, vjp = jax.vjp(lambda v: v * _sigmoid(v), ct)
        return [vjp(ds_)[0]]

    g_cctx = _rowk("cctx_grad", cctx_fn, 1, 1, [(parts, 'vec'), (c_ctx[None], 'row')], [((1, D), F32, 'row')])[0]

    ng_full = jnp.concatenate([t_ng1, t_ng2, t_ng3], axis=0)
    gsmall = {
        'c_ctx': g_cctx, 'b_mod': g_bmod, 'norm_g': lax.dynamic_slice(ng_full, (0, chip * Dq), (3, Dq)),
        'q_norm_g': t_qg, 'k_norm_g': t_kg, 'ssm_a_re': g_are, 'ssm_a_im': g_aim, 'ssm_log_dt': g_ldt,
        'ssm_b_re': jnp.stack([b_grad(t_dbf0, 0), b_grad(t_dbf1, 0)]),
        'ssm_b_im': jnp.stack([b_grad(t_dbf0, P), b_grad(t_dbf1, P)]),
        'ssm_c_re': jnp.stack([c_grad(t_dcr0), c_grad(t_dcr1)]), 'ssm_c_im': jnp.stack([c_grad(t_dci0), c_grad(t_dci1)]),
        'ssm_d': t_d, 'b_glu': t_bglu}
    sshapes = [A[n].shape for n in small]
    sres = _adamw("adamw_small", packs_wmv[0], packs_wmv[1], packs_wmv[2], [_pack([gsmall[n] for n in small])])
    update_group(grp_up, sres[0])
    sres = [_unpack(b_, sshapes) for b_ in sres]
    for k, n in enumerate(small):
        results[n] = tuple(sres[q][k] for q in range(4))
    as2d = lambda v: v.reshape(1, -1) if v.ndim == 1 else v
    tres = _adamw_whole("adamw_tiny", [as2d(A[n]) for n in tiny], [as2d(A['m_' + n]) for n in tiny],
                        [as2d(A['v_' + n]) for n in tiny], [gsmall[n].reshape(as2d(A[n]).shape) for n in tiny])
    for n, res in zip(tiny, tres):
        results[n] = res

    order = ['c_ctx', 'w_mod', 'b_mod', 'norm_g', 'w_ffn1_gate', 'w_ffn1_up', 'w_ffn1_down', 'w_in', 'q_norm_g',
             'k_norm_g', 'ssm_a_re', 'ssm_a_im', 'ssm_log_dt', 'ssm_b_re', 'ssm_b_im', 'ssm_c_re', 'ssm_c_im',
             'ssm_d', 'w_glu', 'b_glu', 'w_br_attn', 'w_br_ssm', 'w_out', 'w_ffn2_gate', 'w_ffn2_up', 'w_ffn2_down']
    outs = [loss, grad_x]
    for q in range(4):
        outs += [results[n][q].reshape(A[n].shape) for n in order]
    return tuple(outs)
```

```python
import math

import jax
import jax.numpy as jnp
from jax import lax
from jax.experimental import pallas as pl
from jax.experimental.pallas import tpu as pltpu

F32 = jnp.float32
BF16 = jnp.bfloat16
MESH = pl.DeviceIdType.MESH

NORM_EPS = 1e-6
ROPE_THETA = 10000.0
GRID_W = 64
HEAD_DIM = 128
Q_PER_KV = 4
SSM_GROUP = 16
SSM_STATE = 64
ADAM_LR = 0.001
ADAM_B1 = 0.9
ADAM_B2 = 0.999
ADAM_EPS = 1e-08
ADAM_WD = 0.01
ADAM_STEP = 10

N_CHIPS = 4
N_DEV = 8
LANES = 128
SLAB_CH = 128
SLAB_GROUPS = SLAB_CH // SSM_GROUP
SLAB_ST = SLAB_GROUPS * SSM_STATE
VMEM_LIMIT_BYTES = 56 * 1024 * 1024
PACK_W = 1024


def _cparams(**kw):
    return pltpu.CompilerParams(vmem_limit_bytes=VMEM_LIMIT_BYTES, **kw)


def _div(n, pref, mult=LANES):
    t = (min(pref, n) // mult) * mult
    while t >= mult:
        if n % t == 0:
            return t
        t -= mult
    return n


def _sigmoid(x):
    return jax.nn.sigmoid(x)


def _gelu(x):
    return x * (0.5 * (1.0 + jnp.tanh(math.sqrt(2.0 / math.pi) * (x + 0.044715 * (x * x * x)))))


def _rowk(name, fn, nrows, tr, ins, outs, nc=0, after=()):
    nt = nrows // tr
    in_specs, arrays = [], []
    for arr, kind in ins:
        arrays.append(arr)
        if kind == 'row':
            in_specs.append(pl.BlockSpec((tr, arr.shape[1]), lambda i: (i, 0)))
        elif kind == 'xrow':
            in_specs.append(pl.BlockSpec((tr, arr.shape[1]), lambda i: (jnp.maximum(i - nc, 0), 0)))
        elif kind == 'orow':
            in_specs.append(pl.BlockSpec((tr, arr.shape[1]), lambda i: (i + nc, 0)))
        elif kind == 'vec':
            in_specs.append(pl.BlockSpec(arr.shape, lambda i, nd=arr.ndim: (0,) * nd))
        elif kind == 'row3':
            in_specs.append(pl.BlockSpec((arr.shape[0], tr, arr.shape[2]), lambda i: (0, i, 0)))
        elif kind == 'row1':
            in_specs.append(pl.BlockSpec((None, tr, arr.shape[2]), lambda i: (0, i, 0)))
        elif kind[0] == 'ocol':
            _, width, blk = kind
            in_specs.append(pl.BlockSpec((tr, width), lambda i, blk=blk: (i + nc, blk)))
        else:
            _, width, blk = kind
            in_specs.append(pl.BlockSpec((tr, width), lambda i, blk=blk: (i, blk)))
    out_shape, out_specs = [], []
    for shape, dtype, kind in outs:
        out_shape.append(jax.ShapeDtypeStruct(shape, dtype))
        if kind == 'row':
            out_specs.append(pl.BlockSpec((tr, shape[1]), lambda i: (i, 0)))
        elif kind == 'row1':
            out_specs.append(pl.BlockSpec((None, tr, shape[2]), lambda i: (0, i, 0)))
        else:
            out_specs.append(pl.BlockSpec(shape, lambda i, nd=len(shape): (0,) * nd))
    nin = len(ins)
    for arr in after:
        arrays.append(arr)
        in_specs.append(pl.BlockSpec(memory_space=pl.ANY))
    nafter = len(after)

    def body(*refs):
        i = pl.program_id(0)
        res = fn(i, *[r[...] for r in refs[:nin]])
        for (shape, dtype, kind), ref, val in zip(outs, refs[nin + nafter:], res):
            if kind in ('row', 'row1'):
                ref[...] = val.astype(dtype)
            else:
                @pl.when(i == 0)
                def _():
                    ref[...] = val.astype(dtype)

                @pl.when(i > 0)
                def _():
                    ref[...] += val.astype(dtype)

    return pl.pallas_call(body, name=name, grid=(nt,), in_specs=in_specs, out_specs=out_specs,
                          out_shape=out_shape, compiler_params=_cparams())(*arrays)


def _mm(name, pairs, M, N, *, tm, tn, nk=1, epi, outs, ta=False, tb=False, rows=(), vecs=(),
        a_pro=None, b_pro=None, n_outer=True, summed=False):
    nm, nn = M // tm, N // tn
    npair = len(pairs)

    def idx(f):
        if n_outer:
            return lambda j, i, k: f(i, j, k)
        return lambda i, j, k: f(i, j, k)

    in_specs, args = [], []
    for a, b, K in pairs:
        tk = K // nk
        if ta:
            in_specs.append(pl.BlockSpec((tk, tm), idx(lambda i, j, k: (k, i))))
        else:
            in_specs.append(pl.BlockSpec((tm, tk), idx(lambda i, j, k: (i, k))))
        args.append(a)
        if b.ndim == 3:
            if tb:
                per = b.shape[2] // tk
                in_specs.append(pl.BlockSpec((None, tn, tk), idx(lambda i, j, k, per=per: (k // per, j, k % per))))
            else:
                per = b.shape[2] // tn
                in_specs.append(pl.BlockSpec((None, tk, tn), idx(lambda i, j, k, per=per: (j // per, k, j % per))))
        elif tb:
            in_specs.append(pl.BlockSpec((tn, tk), idx(lambda i, j, k: (j, k))))
        else:
            in_specs.append(pl.BlockSpec((tk, tn), idx(lambda i, j, k: (k, j))))
        args.append(b)
    for arr, ro, co in rows:
        in_specs.append(pl.BlockSpec((tm, tn), idx(lambda i, j, k, ro=ro, co=co: (i + ro, j + co))))
        args.append(arr)
    for arr in vecs:
        in_specs.append(pl.BlockSpec((arr.shape[0], tn), idx(lambda i, j, k: (0, j))))
        args.append(arr)
    out_shape, out_specs = [], []
    for dtype, chunked in outs:
        if chunked:
            per = (N // N_CHIPS) // tn
            out_shape.append(jax.ShapeDtypeStruct((N_CHIPS, M, N // N_CHIPS), dtype))
            out_specs.append(pl.BlockSpec((None, tm, tn), idx(lambda i, j, k, per=per: (j // per, i, j % per))))
        else:
            out_shape.append(jax.ShapeDtypeStruct((M, N), dtype))
            out_specs.append(pl.BlockSpec((tm, tn), idx(lambda i, j, k: (i, j))))
    nacc = 1 if summed else npair
    scratch = [pltpu.VMEM((tm, tn), F32) for _ in range(nacc)] if nk > 1 else []
    nrow, nvec, nout = len(rows), len(vecs), len(outs)
    dims = (((0 if ta else 1,), (1 if tb else 0,)), ((), ()))

    def body(*refs):
        ab = refs[:2 * npair]
        row_refs = refs[2 * npair:2 * npair + nrow]
        vec_refs = refs[2 * npair + nrow:2 * npair + nrow + nvec]
        out_refs = refs[2 * npair + nrow + nvec:2 * npair + nrow + nvec + nout]
        acc_refs = refs[2 * npair + nrow + nvec + nout:]
        if n_outer:
            j, i, k = pl.program_id(0), pl.program_id(1), pl.program_id(2)
        else:
            i, j, k = pl.program_id(0), pl.program_id(1), pl.program_id(2)

        def part(p):
            av, bv = ab[2 * p][...], ab[2 * p + 1][...]
            if a_pro is not None:
                av = a_pro(av)
            if b_pro is not None:
                bv = b_pro(bv)
            return lax.dot_general(av, bv, dims, preferred_element_type=F32)

        def finish(accs):
            row_index = i * tm + lax.broadcasted_iota(jnp.int32, (tm, 1), 0)
            res = epi(accs, [r[...] for r in row_refs], [v[...] for v in vec_refs], row_index)
            for ref, val in zip(out_refs, res):
                ref[...] = val.astype(ref.dtype)

        parts = [part(p) for p in range(npair)]
        if summed:
            total = parts[0]
            for extra in parts[1:]:
                total = total + extra
            parts = [total]
        if nk == 1:
            finish(parts)
        else:
            @pl.when(k == 0)
            def _():
                for q in range(nacc):
                    acc_refs[q][...] = parts[q]

            @pl.when(jnp.logical_and(k > 0, k < nk - 1))
            def _():
                for q in range(nacc):
                    acc_refs[q][...] += parts[q]

            @pl.when(k == nk - 1)
            def _():
                finish([acc_refs[q][...] + parts[q] for q in range(nacc)])

    grid = (nn, nm, nk) if n_outer else (nm, nn, nk)
    return pl.pallas_call(body, name=name, grid=grid, in_specs=in_specs, out_specs=out_specs,
                          out_shape=out_shape, scratch_shapes=scratch, compiler_params=_cparams())(*args)


def _split3(v):
    v0 = v.astype(BF16)
    r1 = v - v0.astype(F32)
    v1 = r1.astype(BF16)
    v2 = (r1 - v1.astype(F32)).astype(BF16)
    return v0, v1, v2


def _mesh_pos():
    return lax.axis_index("x"), lax.axis_index("y"), lax.axis_index("c")


def _allgather_small(name, x):
    m, n = x.shape

    def body(x_ref, out_ref, send_sems, recv_sems, local_sem):
        xi, yi, ci = _mesh_pos()
        me, sibling = (xi, yi, ci), (xi, yi, 1 - ci)
        chips = [(1 - xi, yi), (xi, 1 - yi), (1 - xi, 1 - yi)]

        def rows(px, py, pc):
            return out_ref.at[pl.ds((4 * px + 2 * py + pc) * m, m), :]

        def copy(k, block, to, src=None):
            return pltpu.make_async_remote_copy(
                src_ref=rows(*block) if src is None else src, dst_ref=rows(*block),
                send_sem=send_sems.at[k], recv_sem=recv_sems.at[k], device_id=to, device_id_type=MESH)

        mine = pltpu.make_async_copy(x_ref, rows(*me), local_sem)
        mine.start()
        first = [copy(0, me, sibling, src=x_ref)]
        first += [copy(1 + j, me, (*chip, ci), src=x_ref) for j, chip in enumerate(chips)]
        for cp in first:
            cp.start()
        passed = [copy(4 + j, (*chip, ci), sibling) for j, chip in enumerate(chips)]
        for j, chip in enumerate(chips):
            copy(1 + j, (*chip, ci), me).wait_recv()
            passed[j].start()
        copy(0, sibling, me).wait_recv()
        for j, chip in enumerate(chips):
            copy(4 + j, (*chip, 1 - ci), me).wait_recv()
        for cp in first + passed:
            cp.wait_send()
        mine.wait()

    return pl.pallas_call(
        body, name=name, out_shape=jax.ShapeDtypeStruct((N_DEV * m, n), x.dtype),
        in_specs=[pl.BlockSpec(memory_space=pltpu.VMEM)], out_specs=pl.BlockSpec(memory_space=pltpu.VMEM),
        scratch_shapes=[pltpu.SemaphoreType.DMA((7,)), pltpu.SemaphoreType.DMA((7,)), pltpu.SemaphoreType.DMA],
        compiler_params=_cparams())(x)


_HBM = pl.BlockSpec(memory_space=pltpu.HBM)
_SEM = pl.BlockSpec(memory_space=pltpu.SEMAPHORE)
_ANY = pl.BlockSpec(memory_space=pl.ANY)
_EFFECT = pltpu.SideEffectType.DATAFLOW_SIDE_EFFECTING


def _in_hbm(v):
    return pltpu.with_memory_space_constraint(v, pltpu.HBM)


def _other_chips(xi, yi):
    return [(1 - xi, yi), (xi, 1 - yi), (1 - xi, 1 - yi)]


def _guarded(core, fn):
    if core is None:
        fn()
    else:
        pl.when(lax.axis_index("c") == core)(fn)


def _split_copies(name, srcs, lands, after, pairs, senders, receivers, ncopy):
    ns, nl = len(srcs), len(lands)
    dma = pltpu.SemaphoreType.DMA((ncopy,))
    thru = [pltpu.HBM(v.shape, v.dtype) for v in list(srcs) + list(lands)]

    def start_body(*refs):
        src_refs, land_refs = refs[:ns], refs[ns:ns + nl]
        descs = pairs(src_refs, land_refs, refs[ns + nl + 1], refs[ns + nl + 2])

        def go():
            for send, _ in descs:
                send.start()

        _guarded(senders, go)
        refs[-1][...] = jnp.zeros_like(refs[-1])

    res = pl.pallas_call(
        start_body, name=name + "_start",
        out_shape=(dma, dma, *thru, jax.ShapeDtypeStruct((8, LANES), F32)),
        in_specs=[_HBM] * (ns + nl) + [_ANY],
        out_specs=(_SEM, _SEM, *([_HBM] * (ns + nl)), pl.BlockSpec(memory_space=pltpu.VMEM)),
        input_output_aliases={k: 2 + k for k in range(ns + nl)},
        compiler_params=_cparams(has_side_effects=_EFFECT),
    )(*[_in_hbm(v) for v in srcs], *[_in_hbm(v) for v in lands], after)
    send_sems, recv_sems, token = res[0], res[1], res[-1]
    carried = res[2:2 + ns + nl]

    def finish(after_work):
        def wait_body(*refs):
            src_refs, land_refs = refs[:ns], refs[ns:ns + nl]
            descs = pairs(src_refs, land_refs, refs[ns + nl], refs[ns + nl + 1])

            def sent():
                for send, _ in descs:
                    send.wait_send()

            def landed():
                for _, recv in descs:
                    recv.wait_recv()

            _guarded(senders, sent)
            _guarded(receivers, landed)

        out = pl.pallas_call(
            wait_body, name=name + "_wait", out_shape=tuple(thru),
            in_specs=[_HBM] * (ns + nl) + [_SEM, _SEM, _ANY], out_specs=tuple([_HBM] * (ns + nl)),
            input_output_aliases={k: k for k in range(ns + nl)},
            compiler_params=_cparams(has_side_effects=_EFFECT),
        )(*carried, send_sems, recv_sems, after_work)
        return list(out[:ns]), list(out[ns:])

    return token, finish


def _cast_slot(name, w, chip_index, after):
    R, C = w.shape[1:]
    tr = _div(R, max(16, 524288 // C), mult=16)

    def body(chip_ref, w_ref, after_ref, o_ref):
        o_ref[...] = w_ref[...].astype(BF16)

    return pl.pallas_call(
        body, name=name, out_shape=jax.ShapeDtypeStruct((N_CHIPS, R, C), BF16),
        grid_spec=pltpu.PrefetchScalarGridSpec(
            num_scalar_prefetch=1, grid=(R // tr,),
            in_specs=[pl.BlockSpec((None, tr, C), lambda i, chip_ref: (0, i, 0)), _ANY],
            out_specs=pl.BlockSpec((None, tr, C), lambda i, chip_ref: (chip_ref[0], i, 0))),
        compiler_params=_cparams())(chip_index, w, after)


def _sum_plane(name, grads, landed, chip_index):
    R, C = grads.shape[1:]
    tr = _div(R, max(16, 1048576 // C), mult=16)

    def body(chip_ref, own_ref, land_ref, o_ref):
        o_ref[...] = ((own_ref[...].astype(F32) + land_ref[0].astype(F32)) + land_ref[1].astype(F32)) \
            + land_ref[2].astype(F32)

    return pl.pallas_call(
        body, name=name, out_shape=jax.ShapeDtypeStruct((R, C), F32),
        grid_spec=pltpu.PrefetchScalarGridSpec(
            num_scalar_prefetch=1, grid=(R // tr,),
            in_specs=[pl.BlockSpec((None, tr, C), lambda i, chip_ref: (chip_ref[0], i, 0)),
                      pl.BlockSpec((3, tr, C), lambda i, chip_ref: (0, i, 0))],
            out_specs=pl.BlockSpec((tr, C), lambda i, chip_ref: (i, 0))),
        compiler_params=_cparams())(chip_index, grads, landed)


def _gather_split(name, lands, after):
    def pairs(src_refs, land_refs, send_sems, recv_sems):
        xi, yi, _ = _mesh_pos()
        mine = 2 * xi + yi
        out = []
        for a in range(len(lands)):
            for j, (px, py) in enumerate(_other_chips(xi, yi)):
                def to_slot(slot, a=a, j=j, px=px, py=py):
                    return pltpu.make_async_remote_copy(
                        src_ref=land_refs[a].at[mine], dst_ref=land_refs[a].at[slot], send_sem=send_sems.at[3 * a + j],
                        recv_sem=recv_sems.at[3 * a + j], device_id=(px, py, 1), device_id_type=MESH)
                out.append((to_slot(mine), to_slot(2 * px + py)))
        return out

    return _split_copies(name, [], lands, after, pairs, senders=1, receivers=1, ncopy=3 * len(lands))


def _allgather_split(name, block, me, after):
    land = lax.dynamic_update_slice(lax.empty((N_DEV,) + block.shape, block.dtype), block[None], (me, 0, 0))

    def pairs(src_refs, land_refs, send_sems, recv_sems):
        xi, yi, ci = _mesh_pos()
        mine = 4 * xi + 2 * yi + ci
        out = []
        for k in range(1, N_DEV):
            kx, ky, kc = (k >> 2) & 1, (k >> 1) & 1, k & 1
            px = 1 - xi if kx else xi
            py = 1 - yi if ky else yi
            pc = 1 - ci if kc else ci

            def to_slot(slot, k=k, px=px, py=py, pc=pc):
                return pltpu.make_async_remote_copy(
                    src_ref=land_refs[0].at[mine], dst_ref=land_refs[0].at[slot], send_sem=send_sems.at[k - 1],
                    recv_sem=recv_sems.at[k - 1], device_id=(px, py, pc), device_id_type=MESH)
            out.append((to_slot(mine), to_slot(4 * px + 2 * py + pc)))
        return out

    tok, fin = _split_copies(name, [], [land], after, pairs, senders=None, receivers=None, ncopy=N_DEV - 1)
    return tok, lambda later: fin(later)[1][0]


def _swap_split(name, arrs, after):
    lands = [lax.empty(v.shape, v.dtype) for v in arrs]

    def pairs(src_refs, land_refs, send_sems, recv_sems):
        xi, yi, ci = _mesh_pos()
        out = []
        for a in range(len(arrs)):
            cp = pltpu.make_async_remote_copy(
                src_ref=src_refs[a], dst_ref=land_refs[a], send_sem=send_sems.at[a], recv_sem=recv_sems.at[a],
                device_id=(xi, yi, 1 - ci), device_id_type=MESH)
            out.append((cp, cp))
        return out

    return _split_copies(name, arrs, lands, after, pairs, senders=None, receivers=None, ncopy=len(arrs))


def _pass_split(name, lands, after):
    def pairs(src_refs, land_refs, send_sems, recv_sems):
        xi, yi, _ = _mesh_pos()
        out = []
        for a in range(len(lands)):
            for j, (px, py) in enumerate(_other_chips(xi, yi)):
                cp = pltpu.make_async_remote_copy(
                    src_ref=land_refs[a].at[2 * px + py], dst_ref=land_refs[a].at[2 * px + py],
                    send_sem=send_sems.at[3 * a + j], recv_sem=recv_sems.at[3 * a + j],
                    device_id=(xi, yi, 0), device_id_type=MESH)
                out.append((cp, cp))
        return out

    return _split_copies(name, [], lands, after, pairs, senders=1, receivers=0, ncopy=3 * len(lands))


def _scatter_split(name, grads, after):
    lands = [lax.empty((3,) + g.shape[1:], g.dtype) for g in grads]

    def pairs(src_refs, land_refs, send_sems, recv_sems):
        xi, yi, ci = _mesh_pos()
        out = []
        for a in range(len(grads)):
            for j, (px, py) in enumerate(_other_chips(xi, yi)):
                cp = pltpu.make_async_remote_copy(
                    src_ref=src_refs[a].at[2 * px + py], dst_ref=land_refs[a].at[j], send_sem=send_sems.at[3 * a + j],
                    recv_sem=recv_sems.at[3 * a + j], device_id=(px, py, ci), device_id_type=MESH)
                out.append((cp, cp))
        return out

    return _split_copies(name, grads, lands, after, pairs, senders=None, receivers=None, ncopy=3 * len(grads))


def _gather_finish(name, lands):
    na = len(lands)

    def body(*refs):
        outs = refs[na:2 * na]
        send_sems, recv_sems = refs[2 * na:]
        xi, yi, ci = _mesh_pos()
        passes = [pltpu.make_async_remote_copy(
            src_ref=outs[a].at[2 * px + py], dst_ref=outs[a].at[2 * px + py],
            send_sem=send_sems.at[a, j], recv_sem=recv_sems.at[a, j], device_id=(xi, yi, 0), device_id_type=MESH)
            for a in range(na) for j, (px, py) in enumerate(_other_chips(xi, yi))]

        @pl.when(ci == 1)
        def _():
            for cp in passes:
                cp.start()
            for cp in passes:
                cp.wait_send()

        @pl.when(ci == 0)
        def _():
            for cp in passes:
                cp.wait_recv()

    return pl.pallas_call(
        body, name=name, out_shape=[jax.ShapeDtypeStruct(v.shape, v.dtype) for v in lands],
        in_specs=[_ANY] * na, out_specs=[_ANY] * na,
        input_output_aliases={a: a for a in range(na)},
        scratch_shapes=[pltpu.SemaphoreType.DMA((na, 3)), pltpu.SemaphoreType.DMA((na, 3))],
        compiler_params=_cparams())(*lands)


ATTN_HEADS_PER_STEP = 2


def _attn_tiles(L, Lc, D, tq_pref=256):
    tq = min(tq_pref, Lc)
    return tq, L // tq, Lc // tq, D // HEAD_DIM // Q_PER_KV


def _attn_scores(q, k):
    return lax.dot_general(q, k, (((1,), (1,)), ((), ())), preferred_element_type=F32) * (HEAD_DIM ** -0.5)


def _softmax_rows(s):
    e = jnp.exp(s - jnp.max(s, axis=-1, keepdims=True))
    return e * (1.0 / jnp.sum(e, axis=-1, keepdims=True))


def _attn_probs(q, k):
    return _softmax_rows(_attn_scores(q, k))


def _attn_fwd(qr, kr, v, L, Lc, D):
    T = L + Lc
    tq, nq, qoff, nkv = _attn_tiles(L, Lc, D)
    hp = Q_PER_KV
    ng = Q_PER_KV // hp

    def body(q_ref, k_ref, v_ref, o_ref):
        k, vv = k_ref[...], v_ref[...]
        heads = [slice(r * HEAD_DIM, (r + 1) * HEAD_DIM) for r in range(hp)]
        scores = [_attn_scores(q_ref[:, cols], k) for cols in heads]
        probs = [_softmax_rows(s) for s in scores]
        for cols, p in zip(heads, probs):
            o_ref[:, cols] = jnp.dot(p.astype(BF16), vv, preferred_element_type=F32).astype(o_ref.dtype)

    kv_spec = pl.BlockSpec((T, HEAD_DIM), lambda h, r, q: (0, h))
    return pl.pallas_call(
        body, name="attn_fwd", grid=(nkv, ng, nq),
        in_specs=[pl.BlockSpec((tq, hp * HEAD_DIM), lambda h, r, q: (q + qoff, h * ng + r)), kv_spec, kv_spec],
        out_specs=pl.BlockSpec((tq, hp * HEAD_DIM), lambda h, r, q: (q, h * ng + r)),
        out_shape=jax.ShapeDtypeStruct((L, D), BF16), compiler_params=_cparams())(qr, kr, v)


def _attn_bwd(qr, kr, v, do, L, Lc, D):
    T = L + Lc
    tq, nq, qoff, nkv = _attn_tiles(L, Lc, D, 256)
    scale = HEAD_DIM ** -0.5
    hp = Q_PER_KV
    ng = Q_PER_KV // hp

    def body(q_ref, k_ref, v_ref, do_ref, dq_ref, dk_ref, dv_ref):
        first = jnp.logical_and(pl.program_id(1) == 0, pl.program_id(2) == 0)
        k, vv = k_ref[...], v_ref[...]
        nt_dims, tn_dims = (((1,), (1,)), ((), ())), (((0,), (0,)), ((), ()))
        heads = [slice(r * HEAD_DIM, (r + 1) * HEAD_DIM) for r in range(hp)]
        qs = [q_ref[:, cols] for cols in heads]
        douts = [do_ref[:, cols] for cols in heads]
        scores = [_attn_scores(q, k) for q in qs]
        dps = [lax.dot_general(dout, vv, nt_dims, preferred_element_type=F32) for dout in douts]
        probs = [_softmax_rows(s) for s in scores]
        dss = [(p * (dp - jnp.sum(p * dp, axis=-1, keepdims=True)) * scale).astype(BF16) for p, dp in zip(probs, dps)]
        for cols, ds in zip(heads, dss):
            dq_ref[:, cols] = jnp.dot(ds, k, preferred_element_type=F32)
        dk = dv = None
        for q, dout, p, ds in zip(qs, douts, probs, dss):
            dk_r = lax.dot_general(ds, q, tn_dims, preferred_element_type=F32)
            dv_r = lax.dot_general(p.astype(BF16), dout, tn_dims, preferred_element_type=F32)
            dk = dk_r if dk is None else dk + dk_r
            dv = dv_r if dv is None else dv + dv_r

        @pl.when(first)
        def _():
            dk_ref[...] = dk
            dv_ref[...] = dv

        @pl.when(jnp.logical_not(first))
        def _():
            dk_ref[...] += dk
            dv_ref[...] += dv

    kv_spec = pl.BlockSpec((T, HEAD_DIM), lambda h, r, q: (0, h))
    q_spec = pl.BlockSpec((tq, hp * HEAD_DIM), lambda h, r, q: (q + qoff, h * ng + r))
    o_spec = pl.BlockSpec((tq, hp * HEAD_DIM), lambda h, r, q: (q, h * ng + r))
    return pl.pallas_call(
        body, name="attn_bwd", grid=(nkv, ng, nq),
        in_specs=[q_spec, kv_spec, kv_spec, o_spec], out_specs=[o_spec, kv_spec, kv_spec],
        out_shape=[jax.ShapeDtypeStruct((L, D), F32), jax.ShapeDtypeStruct((T, D // Q_PER_KV), F32),
                   jax.ShapeDtypeStruct((T, D // Q_PER_KV), F32)],
        compiler_params=_cparams())(qr, kr, v, do)


SUB = 8


def _doubling(xr, xi, pw_re, pw_im, lanes, first_power, period, reverse):
    n = xr.shape[0]
    rows = lax.broadcasted_iota(jnp.int32, (n, 1), 0) & (period - 1)
    for k in range(period.bit_length() - 1):
        d = 1 << k
        keep = rows < period - d if reverse else rows >= d
        sr = jnp.where(keep, pltpu.roll(xr, n - d if reverse else d, 0), 0.0)
        si = jnp.where(keep, pltpu.roll(xi, n - d if reverse else d, 0), 0.0)
        pr, pi = pw_re[first_power + k:first_power + k + 1, lanes], pw_im[first_power + k:first_power + k + 1, lanes]
        xr, xi = xr + (pr * sr - pi * si), xi + (pr * si + pi * sr)
    return xr, xi


def _scan_tile(xr, xi, tb, lanes, reverse):
    pw_re, pw_im, w8_re, w8_im, wb_re, wb_im, carry_re, carry_im, sr, si = tb
    tt = xr.shape[0]
    nb = tt // SUB
    nq = sr.shape[0]
    cols = [slice(q * LANES, (q + 1) * LANES) for q in range(nq)]
    for q in range(nq):
        sr[q] = xr[:, cols[q]]
        si[q] = xi[:, cols[q]]
    order = list(range(SUB - 2, -1, -1)) if reverse else list(range(1, SUB))
    ends_r, ends_i = [], []
    for q in range(nq):
        ql = slice(lanes.start + q * LANES, lanes.start + (q + 1) * LANES)
        lr, li = pw_re[0:1, ql], pw_im[0:1, ql]
        first_row = pl.ds(SUB - 1 if reverse else 0, nb, stride=SUB)
        pr, pi = sr[q, first_row, :], si[q, first_row, :]
        for r in order:
            rows = pl.ds(r, nb, stride=SUB)
            pr, pi = sr[q, rows, :] + (lr * pr - li * pi), si[q, rows, :] + (lr * pi + li * pr)
            sr[q, rows, :] = pr
            si[q, rows, :] = pi
        ends_r.append(pr)
        ends_i.append(pi)
    er, ei = jnp.concatenate(ends_r, axis=1), jnp.concatenate(ends_i, axis=1)
    er, ei = _doubling(er, ei, pw_re, pw_im, lanes, 3, nb, reverse)
    car, cai = carry_re[:, lanes], carry_im[:, lanes]
    wbr, wbi = wb_re[:, lanes], wb_im[:, lanes]
    er = er + (wbr * car - wbi * cai)
    ei = ei + (wbr * cai + wbi * car)
    out_block = 0 if reverse else nb - 1
    carry_re[:, lanes] = er[out_block:out_block + 1, :]
    carry_im[:, lanes] = ei[out_block:out_block + 1, :]
    blocks = lax.broadcasted_iota(jnp.int32, (nb, 1), 0)
    first = blocks == (nb - 1 if reverse else 0)
    cr = jnp.where(first, car, pltpu.roll(er, nb - 1 if reverse else 1, 0))
    ci = jnp.where(first, cai, pltpu.roll(ei, nb - 1 if reverse else 1, 0))
    for r in range(SUB):
        wr, wi = w8_re[r:r + 1, lanes], w8_im[r:r + 1, lanes]
        add_r, add_i = wr * cr - wi * ci, wr * ci + wi * cr
        for q in range(nq):
            sr[q, pl.ds(r, nb, stride=SUB), :] += add_r[:, cols[q]]
            si[q, pl.ds(r, nb, stride=SUB), :] += add_i[:, cols[q]]
    hr = jnp.concatenate([sr[q] for q in range(nq)], axis=1)
    hi = jnp.concatenate([si[q] for q in range(nq)], axis=1)
    return hr, hi, car, cai


def _scan_scratch(tt, NS):
    nb = tt // SUB
    return [pltpu.VMEM((8, NS), F32), pltpu.VMEM((8, NS), F32), pltpu.VMEM((SUB, NS), F32), pltpu.VMEM((SUB, NS), F32),
            pltpu.VMEM((nb, NS), F32), pltpu.VMEM((nb, NS), F32), pltpu.VMEM((1, NS), F32), pltpu.VMEM((1, NS), F32),
            pltpu.VMEM((SLAB_ST // LANES, tt, LANES), F32), pltpu.VMEM((SLAB_ST // LANES, tt, LANES), F32)]


def _scan_init(lr, li, tb, reverse):
    pw_re, pw_im, w8_re, w8_im, wb_re, wb_im, carry_re, carry_im, sr, _ = tb
    nb = wb_re.shape[0]
    carry_re[...] = jnp.zeros_like(carry_re)
    carry_im[...] = jnp.zeros_like(carry_im)
    pr, pi = lr, li
    for k in range(3 + nb.bit_length() - 1):
        pw_re[k:k + 1, :] = pr
        pw_im[k:k + 1, :] = pi
        if k == 3:
            l8r, l8i = pr, pi
        pr, pi = pr * pr - pi * pi, 2.0 * pr * pi
    pr, pi = lr, li
    for r in range(SUB):
        row = SUB - 1 - r if reverse else r
        w8_re[row:row + 1, :] = pr
        w8_im[row:row + 1, :] = pi
        pr, pi = pr * lr - pi * li, pr * li + pi * lr
    pr, pi = l8r, l8i
    for b in range(nb):
        row = nb - 1 - b if reverse else b
        wb_re[row:row + 1, :] = pr
        wb_im[row:row + 1, :] = pi
        pr, pi = pr * l8r - pi * l8i, pr * l8i + pi * l8r


def _ssm_tiles(T, Lc):
    tt = min(128, Lc)
    return tt, T // tt, Lc // tt


def _ssm_fwd(name, u, bbd, cbd_re, cbd_im, lam_re, lam_im, coef_re, coef_im, Lc, reverse):
    T, W = u.shape
    nslab = W // SLAB_CH
    NS = nslab * SLAB_ST
    tt, nt, nc = _ssm_tiles(T, Lc)
    if reverse:
        tile = lambda s: jnp.where(s < nc, nc - 1 - s, nt - 1 - (s - nc))
    else:
        tile = lambda s: s

    def body(u_ref, b_ref, cr_ref, ci_ref, lr_ref, li_ref, kr_ref, ki_ref, hr_ref, hi_ref, y_ref, *tb):
        @pl.when(pl.program_id(0) == 0)
        def _():
            _scan_init(lr_ref[...], li_ref[...], tb, reverse)

        for j in range(nslab):
            lanes = slice(j * SLAB_ST, (j + 1) * SLAB_ST)
            bu = jnp.dot(u_ref[:, j * SLAB_CH:(j + 1) * SLAB_CH], b_ref[j], preferred_element_type=F32)
            br, bi = bu[:, :SLAB_ST], bu[:, SLAB_ST:]
            kr, ki = kr_ref[:, lanes], ki_ref[:, lanes]
            hr, hi, _, _ = _scan_tile(kr * br - ki * bi, kr * bi + ki * br, tb, lanes, reverse)
            hrb, hib = hr.astype(BF16), hi.astype(BF16)
            hr_ref[:, lanes] = hrb
            hi_ref[:, lanes] = hib
            y_ref[:, j * SLAB_CH:(j + 1) * SLAB_CH] = (
                jnp.dot(hrb, cr_ref[j], preferred_element_type=F32)
                - jnp.dot(hib, ci_ref[j], preferred_element_type=F32))

    whole3 = lambda arr: pl.BlockSpec(arr.shape, lambda s: (0, 0, 0))
    vec = pl.BlockSpec((1, NS), lambda s: (0, 0))
    return pl.pallas_call(
        body, name=name, grid=(nt,),
        in_specs=[pl.BlockSpec((tt, W), lambda s: (tile(s), 0)), whole3(bbd), whole3(cbd_re), whole3(cbd_im),
                  vec, vec, vec, vec],
        out_specs=[pl.BlockSpec((tt, NS), lambda s: (tile(s), 0)), pl.BlockSpec((tt, NS), lambda s: (tile(s), 0)),
                   pl.BlockSpec((tt, W), lambda s: (tile(s), 0))],
        out_shape=[jax.ShapeDtypeStruct((T, NS), BF16), jax.ShapeDtypeStruct((T, NS), BF16),
                   jax.ShapeDtypeStruct((T, W), F32)],
        scratch_shapes=_scan_scratch(tt, NS),
        compiler_params=_cparams())(u, bbd, cbd_re, cbd_im, lam_re, lam_im, coef_re, coef_im)


def _ssm_bwd(name, dy, h_re, h_im, u, bbd, bbdt_re, bbdt_im, cbdt_re, cbdt_im, lam_re, lam_im,
             coef_re, coef_im, Lc, reverse):
    T, W = u.shape
    nslab = W // SLAB_CH
    NS = nslab * SLAB_ST
    tt, nt, nc = _ssm_tiles(T, Lc)
    adj_reverse = not reverse
    if reverse:
        tile = lambda s: jnp.where(s < nt - nc, nc + s, s - (nt - nc))
    else:
        tile = lambda s: nt - 1 - s

    def body(dy_ref, hr_ref, hi_ref, u_ref, b_ref, btr_ref, bti_ref, ctr_ref, cti_ref, lr_ref, li_ref,
             kr_ref, ki_ref, du_ref, dlr_ref, dli_ref, dkr_ref, dki_ref, dbf_ref, dcrf_ref, dcif_ref,
             db_ref, dcr_ref, dci_ref, *tb):
        @pl.when(pl.program_id(0) == 0)
        def _():
            _scan_init(lr_ref[...], -li_ref[...], tb, adj_reverse)
            for ref in (dlr_ref, dli_ref, dkr_ref, dki_ref, db_ref, dcr_ref, dci_ref):
                ref[...] = jnp.zeros_like(ref)

        rows = lax.broadcasted_iota(jnp.int32, (tt, 1), 0)
        far_row = tt - 1 if adj_reverse else 0
        tn_dims = (((0,), (0,)), ((), ()))
        for j in range(nslab):
            lanes = slice(j * SLAB_ST, (j + 1) * SLAB_ST)
            chans = slice(j * SLAB_CH, (j + 1) * SLAB_CH)
            dys, us = dy_ref[:, chans], u_ref[:, chans]
            er = jnp.dot(dys, ctr_ref[j], preferred_element_type=F32)
            ei = -jnp.dot(dys, cti_ref[j], preferred_element_type=F32)
            ar, ai, car, cai = _scan_tile(er, ei, tb, lanes, adj_reverse)
            shift = tt - 1 if adj_reverse else 1
            nr = jnp.where(rows == far_row, car, pltpu.roll(ar, shift, 0))
            ni = jnp.where(rows == far_row, cai, pltpu.roll(ai, shift, 0))
            hrb, hib = hr_ref[:, lanes], hi_ref[:, lanes]
            hr, hi = hrb.astype(F32), hib.astype(F32)
            dlr_ref[:, lanes] += jnp.sum(nr * hr + ni * hi, axis=0, keepdims=True)
            dli_ref[:, lanes] += jnp.sum(ni * hr - nr * hi, axis=0, keepdims=True)
            bu = jnp.dot(us, b_ref[j], preferred_element_type=F32)
            br, bi = bu[:, :SLAB_ST], bu[:, SLAB_ST:]
            dkr_ref[:, lanes] += jnp.sum(ar * br + ai * bi, axis=0, keepdims=True)
            dki_ref[:, lanes] += jnp.sum(ai * br - ar * bi, axis=0, keepdims=True)
            kr, ki = kr_ref[:, lanes], ki_ref[:, lanes]
            dbr = (ar * kr + ai * ki).astype(BF16)
            dbi = (ai * kr - ar * ki).astype(BF16)
            du_ref[:, chans] = (jnp.dot(dbr, btr_ref[j], preferred_element_type=F32)
                                + jnp.dot(dbi, bti_ref[j], preferred_element_type=F32))
            db_ref[j, :, :SLAB_ST] += lax.dot_general(us, dbr, tn_dims, preferred_element_type=F32)
            db_ref[j, :, SLAB_ST:] += lax.dot_general(us, dbi, tn_dims, preferred_element_type=F32)
            dcr_ref[j] += lax.dot_general(hrb, dys, tn_dims, preferred_element_type=F32)
            dci_ref[j] -= lax.dot_general(hib, dys, tn_dims, preferred_element_type=F32)

        @pl.when(pl.program_id(0) == nt - 1)
        def _():
            def iota(shape, axis):
                return lax.broadcasted_iota(jnp.int32, shape, axis)

            sg, ss = SSM_GROUP.bit_length() - 1, SSM_STATE.bit_length() - 1
            b_mask = (iota((SLAB_CH, SLAB_ST), 0) >> sg) == (iota((SLAB_CH, SLAB_ST), 1) >> ss)
            c_mask = (iota((SLAB_ST, SLAB_CH), 0) >> ss) == (iota((SLAB_ST, SLAB_CH), 1) >> sg)
            fold = jnp.where((iota((SLAB_ST, SSM_STATE), 0) & (SSM_STATE - 1)) == iota((SLAB_ST, SSM_STATE), 1),
                             1.0, 0.0).astype(BF16)
            fold_t = jnp.where((iota((SSM_STATE, SLAB_ST), 1) & (SSM_STATE - 1)) == iota((SSM_STATE, SLAB_ST), 0),
                               1.0, 0.0).astype(BF16)

            def exact_dot(a, b, a_is_value):
                terms = _split3(a if a_is_value else b)
                acc = None
                for t in terms:
                    part = jnp.dot(t, b, preferred_element_type=F32) if a_is_value else jnp.dot(a, t, preferred_element_type=F32)
                    acc = part if acc is None else acc + part
                return acc

            for j in range(nslab):
                dbj = db_ref[j]
                dbf_ref[j, :, :SSM_STATE] = exact_dot(jnp.where(b_mask, dbj[:, :SLAB_ST], 0.0), fold, True)
                dbf_ref[j, :, SSM_STATE:] = exact_dot(jnp.where(b_mask, dbj[:, SLAB_ST:], 0.0), fold, True)
                dcrf_ref[j] = exact_dot(fold_t, jnp.where(c_mask, dcr_ref[j], 0.0), False)
                dcif_ref[j] = exact_dot(fold_t, jnp.where(c_mask, dci_ref[j], 0.0), False)

    whole3 = lambda arr: pl.BlockSpec(arr.shape, lambda s: (0, 0, 0))
    vec = pl.BlockSpec((1, NS), lambda s: (0, 0))
    row_w = pl.BlockSpec((tt, W), lambda s: (tile(s), 0))
    row_s = pl.BlockSpec((tt, NS), lambda s: (tile(s), 0))
    dbf = jax.ShapeDtypeStruct((nslab, SLAB_CH, 2 * SSM_STATE), F32)
    dcf = jax.ShapeDtypeStruct((nslab, SSM_STATE, SLAB_CH), F32)
    return pl.pallas_call(
        body, name=name, grid=(nt,),
        in_specs=[row_w, row_s, row_s, row_w, whole3(bbd), whole3(bbdt_re), whole3(bbdt_im), whole3(cbdt_re),
                  whole3(cbdt_im), vec, vec, vec, vec],
        out_specs=[row_w, vec, vec, vec, vec, whole3(dbf), whole3(dcf), whole3(dcf)],
        out_shape=[jax.ShapeDtypeStruct((T, W), F32)] + [jax.ShapeDtypeStruct((1, NS), F32)] * 4 + [dbf, dcf, dcf],
        scratch_shapes=[pltpu.VMEM(bbd.shape, F32), pltpu.VMEM(bbdt_re.shape, F32), pltpu.VMEM(bbdt_re.shape, F32)]
        + _scan_scratch(tt, NS),
        compiler_params=_cparams())(dy, h_re, h_im, u, bbd, bbdt_re, bbdt_im, cbdt_re, cbdt_im,
                                    lam_re, lam_im, coef_re, coef_im)


def _zoh_math(a_re, a_im, log_dt):
    dt = jnp.exp(log_dt)
    mag = jnp.exp(a_re * dt)
    lb_re = mag * jnp.cos(a_im * dt)
    lb_im = mag * jnp.sin(a_im * dt)
    den = a_re * a_re + a_im * a_im
    coef_re = ((lb_re - 1.0) * a_re + lb_im * a_im) / den
    coef_im = (lb_im * a_re - (lb_re - 1.0) * a_im) / den
    return lb_re, lb_im, coef_re, coef_im


def _zoh_fwd(a_re, a_im, log_dt):
    def body(ar, ai, ld, o0, o1, o2, o3):
        for ref, val in zip((o0, o1, o2, o3), _zoh_math(ar[...], ai[...], ld[...])):
            ref[...] = val

    return pl.pallas_call(body, name="zoh_fwd", out_shape=[jax.ShapeDtypeStruct(a_re.shape, F32)] * 4,
                          compiler_params=_cparams())(a_re, a_im, log_dt)


def _zoh_bwd(a_re, a_im, log_dt, cots):
    def body(ar, ai, ld, c0, c1, c2, c3, o0, o1, o2):
        _, vjp = jax.vjp(_zoh_math, ar[...], ai[...], ld[...])
        for ref, val in zip((o0, o1, o2), vjp((c0[...], c1[...], c2[...], c3[...]))):
            ref[...] = val

    return pl.pallas_call(
        body, name="zoh_bwd",
        out_shape=[jax.ShapeDtypeStruct(a_re.shape, F32), jax.ShapeDtypeStruct(a_re.shape, F32),
                   jax.ShapeDtypeStruct(log_dt.shape, F32)],
        compiler_params=_cparams())(a_re, a_im, log_dt, *cots)


def _adamw_outer(name, w, m, v, acts, cots):
    D, N = w.shape[1:]
    tm = LANES
    dims = (((0,), (0,)), ((), ()))

    def body(a_ref, b_ref, w_ref, m_ref, v_ref, g_ref, d_ref, nm_ref, nv_ref):
        a = a_ref[...]
        aa = _split3(a * _sigmoid(a))
        bb = _split3(b_ref[...])
        g = None
        for ia in range(3):
            for ib in range(3 - ia):
                t = lax.dot_general(aa[ia], bb[ib], dims, preferred_element_type=F32)
                g = t if g is None else g + t
        g_ref[...] = g
        d_ref[...], nm_ref[...], nv_ref[...] = _adamw_math(w_ref[...], g, m_ref[...], v_ref[...])

    tile = pl.BlockSpec((None, tm, N), lambda i: (0, i, 0))
    return pl.pallas_call(
        body, name=name, grid=(D // tm,),
        in_specs=[pl.BlockSpec((16, tm), lambda i: (0, i)), pl.BlockSpec((16, N), lambda i: (0, 0)), tile, tile, tile],
        out_specs=[tile] * 4, out_shape=[jax.ShapeDtypeStruct(w.shape, F32)] * 4,
        compiler_params=_cparams())(acts, cots, w, m, v)


def _adamw_math(w, g, m, v):
    m = ADAM_B1 * m + (1.0 - ADAM_B1) * g
    v = ADAM_B2 * v + (1.0 - ADAM_B2) * (g * g)
    m_hat = m / (1.0 - ADAM_B1 ** ADAM_STEP)
    v_hat = v / (1.0 - ADAM_B2 ** ADAM_STEP)
    delta = -ADAM_LR * (m_hat / (jnp.sqrt(v_hat) + ADAM_EPS) + ADAM_WD * w)
    return delta, m, v


def _adamw(name, w, m, v, gparts):
    R, C = w.shape[-2:]
    kind = 'row1' if w.ndim == 3 else 'row'
    tr = _div(R, max(8, 524288 // C), mult=8)

    def fn(i, wv, mv, vv, *gs):
        g = gs[0]
        for extra in gs[1:]:
            g = g + extra
        return (g,) + _adamw_math(wv, g, mv, vv)

    return _rowk(name, fn, R, tr, [(w, kind), (m, kind), (v, kind)] + [(g, 'row') for g in gparts],
                 [(w.shape, F32, kind)] * 4)


def _adamw_whole(name, ws, ms, vs, gs):
    n = len(ws)

    def body(*refs):
        for k in range(n):
            g = refs[3 * n + k][...]
            res = (g,) + _adamw_math(refs[k][...], g, refs[n + k][...], refs[2 * n + k][...])
            for q in range(4):
                refs[4 * n + 4 * k + q][...] = res[q]

    out = pl.pallas_call(
        body, name=name, out_shape=[jax.ShapeDtypeStruct(w.shape, F32) for w in ws for _ in range(4)],
        compiler_params=_cparams())(*ws, *ms, *vs, *gs)
    return [tuple(out[4 * k:4 * k + 4]) for k in range(n)]


def _pack(pieces, rows_mult=8):
    flat = jnp.concatenate([p.reshape(-1).astype(F32) for p in pieces])
    unit = rows_mult * PACK_W
    total = -(-flat.shape[0] // unit) * unit
    return jnp.pad(flat, (0, total - flat.shape[0])).reshape(total // PACK_W, PACK_W)


def _unpack(buf, shapes):
    flat = buf.reshape(-1)
    out, off = [], 0
    for s in shapes:
        n = math.prod(s)
        out.append(flat[off:off + n].reshape(s))
        off += n
    return out


def _bd_expand(t):
    S, g, a, b = t.shape
    eye = jnp.eye(g, dtype=t.dtype)
    return (t[:, :, :, None, :] * eye[None, :, None, :, None]).reshape(S, g * a, g * b)


def _rope_tables(L, Lc):
    rows = L // GRID_W
    row_ids = jnp.broadcast_to(jnp.arange(rows)[:, None], (rows, GRID_W)).reshape(-1).astype(F32)
    col_ids = jnp.broadcast_to(jnp.arange(GRID_W)[None, :], (rows, GRID_W)).reshape(-1).astype(F32)
    quarter = HEAD_DIM // 4
    inv_freq = ROPE_THETA ** (-jnp.arange(quarter, dtype=F32) / quarter)
    ang_r = row_ids[:, None] * inv_freq
    ang_c = col_ids[:, None] * inv_freq
    cos = jnp.concatenate([jnp.cos(ang_r), jnp.cos(ang_r), jnp.cos(ang_c), jnp.cos(ang_c)], axis=1)
    sin = jnp.concatenate([-jnp.sin(ang_r), jnp.sin(ang_r), -jnp.sin(ang_c), jnp.sin(ang_c)], axis=1)
    cos = jnp.concatenate([jnp.ones((Lc, HEAD_DIM), F32), cos], axis=0)
    sin = jnp.concatenate([jnp.zeros((Lc, HEAD_DIM), F32), sin], axis=0)
    return cos, sin


def _rot(v):
    lane = lax.broadcasted_iota(jnp.int32, (1, HEAD_DIM), 1)
    first = (lane % (HEAD_DIM // 2)) < (HEAD_DIM // 4)
    return jnp.where(first, pltpu.roll(v, HEAD_DIM - HEAD_DIM // 4, 1), pltpu.roll(v, HEAD_DIM // 4, 1))


def _head_norm(xh, g):
    return xh * lax.rsqrt(jnp.mean(xh * xh, axis=-1, keepdims=True) + NORM_EPS) * g


def _norm_mod(xv, g, sh, sc):
    r = lax.rsqrt(jnp.mean(xv * xv, axis=-1, keepdims=True) + NORM_EPS)
    return (xv * r) * g * (1.0 + sc) + sh


def kernel(x, c, ctx, c_ctx, w_mod, b_mod, norm_g, w_ffn1_gate, w_ffn1_up, w_ffn1_down, w_in, q_norm_g, k_norm_g, ssm_a_re, ssm_a_im, ssm_log_dt, ssm_b_re, ssm_b_im, ssm_c_re, ssm_c_im, ssm_d, w_glu, b_glu, w_br_attn, w_br_ssm, w_out, w_ffn2_gate, w_ffn2_up, w_ffn2_down, loss_target, m_c_ctx, m_w_mod, m_b_mod, m_norm_g, m_w_ffn1_gate, m_w_ffn1_up, m_w_ffn1_down, m_w_in, m_q_norm_g, m_k_norm_g, m_ssm_a_re, m_ssm_a_im, m_ssm_log_dt, m_ssm_b_re, m_ssm_b_im, m_ssm_c_re, m_ssm_c_im, m_ssm_d, m_w_glu, m_b_glu, m_w_br_attn, m_w_br_ssm, m_w_out, m_w_ffn2_gate, m_w_ffn2_up, m_w_ffn2_down, v_c_ctx, v_w_mod, v_b_mod, v_norm_g, v_w_ffn1_gate, v_w_ffn1_up, v_w_ffn1_down, v_w_in, v_q_norm_g, v_k_norm_g, v_ssm_a_re, v_ssm_a_im, v_ssm_log_dt, v_ssm_b_re, v_ssm_b_im, v_ssm_c_re, v_ssm_c_im, v_ssm_d, v_w_glu, v_b_glu, v_w_br_attn, v_w_br_ssm, v_w_out, v_w_ffn2_gate, v_w_ffn2_up, v_w_ffn2_down):
    A = dict(locals())
    xi, yi, ci = _mesh_pos()
    chip = 2 * xi + yi
    me = 4 * xi + 2 * yi + ci
    L, D = x.shape[1], x.shape[2]
    Lc = ctx.shape[1]
    T = L + Lc
    F4 = w_ffn1_gate.shape[2]
    F = N_CHIPS * F4
    W, KV, Dq = D // 2, D // 4, D // 4
    G = W // SSM_GROUP
    P, E = SSM_STATE, SSM_GROUP
    NS = G * P
    nslab = W // SLAB_CH
    tr = min(256, Lc)
    ncr = Lc // tr
    assert L % tr == 0 and Lc % tr == 0 and W % SLAB_CH == 0 and D % (4 * LANES) == 0

    def sel(i, v):
        return v if v.shape[0] == 1 else jnp.where(i < ncr, v[0:1], v[1:2])

    def put(i, v, nrow):
        if nrow == 1:
            return v
        which = (i >= ncr).astype(jnp.int32)
        r2 = lax.broadcasted_iota(jnp.int32, (nrow, 1), 0)
        return jnp.where(r2 == which, jnp.broadcast_to(v, (nrow, v.shape[1])), 0.0)

    ident = lambda accs, rows, vecs, ri: [accs[0]]

    NM = w_mod.shape[2]
    first = jnp.zeros((8, D), F32).at[0].set(c[0]).at[1:4, :Dq].set(norm_g[0])
    g0 = _allgather_small("gather_c", first).reshape(N_CHIPS, 2, 8, D)
    c_all = g0[:, :, 0].reshape(N_DEV, D)
    ng = jnp.transpose(g0[:, 0, 1:4, :Dq], (1, 0, 2)).reshape(3, D)
    acts = jnp.concatenate([c_all, c_ctx[None], jnp.zeros((7, D), F32)], axis=0)
    wm = w_mod[0]
    b_shard = lax.dynamic_slice(b_mod[0], (chip * NM,), (NM,))[None]
    silu_bf = lambda a: (a * _sigmoid(a)).astype(BF16)
    to_bf = lambda b: b.astype(BF16)
    mod_part = _mm("mod_fwd", [(acts, wm, D)], 16, NM, tm=16, tn=_div(NM, 1152),
                   epi=lambda accs, rows, vecs, ri: [accs[0] + vecs[0]], outs=[(F32, False)],
                   vecs=[b_shard], a_pro=silu_bf, b_pro=to_bf)[0]
    mg = _allgather_small("gather_mod", mod_part).reshape(N_CHIPS, 2, 16, NM)[:, 0]
    mod_all = jnp.transpose(mg, (1, 0, 2)).reshape(16, N_CHIPS * NM)
    mod_x = lax.dynamic_slice(mod_all, (me, 0), (1, 9 * D))
    mod_c = jnp.where(jnp.arange(9 * D)[None] < 5 * D, mod_all[8:9], 0.0)
    modv = jnp.concatenate([mod_c, mod_x], axis=0)
    mv = lambda k: modv[:, k * D:(k + 1) * D]
    sh1, sc1, g1, sh2, sc2 = mv(0), mv(1), mv(2), mv(3), mv(4)
    g2, sh3, sc3, g3 = mv(5)[1:2], mv(6)[1:2], mv(7)[1:2], mv(8)[1:2]

    big = ['w_ffn1_gate', 'w_ffn1_up', 'w_ffn1_down', 'w_ffn2_gate', 'w_ffn2_up', 'w_ffn2_down',
           'w_in', 'w_glu', 'w_br_attn', 'w_br_ssm', 'w_out']
    row_sharded = {'w_ffn1_down', 'w_ffn2_down', 'w_glu', 'w_br_attn', 'w_out'}
    groups = [big[0:2], big[2:3], big[6:7], big[7:11], big[3:6]]
    chip_index = jnp.reshape(chip, (1,)).astype(jnp.int32)
    tok, gather_finish = modv, []
    pin = c
    for gi, names in enumerate(groups):
        tok, fin = _gather_split("gather_w%d" % gi, [_cast_slot("cast_" + n, A[n], chip_index, pin) for n in names], tok)
        gather_finish.append(fin)
        pin = tok
    ng = ng + tok[0:1, 0:1]
    Wt = {}

    def register(names, full):
        for n, gw in zip(names, full):
            Wt[n] = gw.reshape(N_CHIPS * gw.shape[1], gw.shape[2]) if n in row_sharded else gw

    def weights_ready(gi, after_work):
        _, lands = gather_finish[gi](after_work)
        register(groups[gi], _gather_finish("gather_w%d_pass" % gi, lands))

    def weights_pass(gi, after_work):
        _, lands = gather_finish[gi](after_work)
        tok_, fin_ = _pass_split("gather_w%d_pass" % gi, lands, after_work)
        return tok_, lambda later: register(groups[gi], fin_(later)[1])

    a_re2, a_im2 = ssm_a_re[0].reshape(2 * G, P), ssm_a_im[0].reshape(2 * G, P)
    ldt2 = ssm_log_dt[0].reshape(2 * G, 1)
    zoh = _zoh_fwd(a_re2, a_im2, ldt2)
    lam_re, lam_im, coef_re, coef_im = [[z[d * G:(d + 1) * G].reshape(1, NS) for d in range(2)] for z in zoh]
    bd_b = lambda b: _bd_expand(jnp.transpose(b, (0, 2, 1)).reshape(nslab, SLAB_GROUPS, E, P))
    bd_c = lambda cc: _bd_expand(jnp.transpose(cc, (0, 2, 1)).reshape(nslab, SLAB_GROUPS, P, E))
    bbd, bbdt_re, bbdt_im, cbd_re, cbd_im, cbdt_re, cbdt_im = [], [], [], [], [], [], []
    for d in range(2):
        br_, bi_ = bd_b(ssm_b_re[0, d]).astype(BF16), bd_b(ssm_b_im[0, d]).astype(BF16)
        cr_, ci_ = bd_c(ssm_c_re[0, d]).astype(BF16), bd_c(ssm_c_im[0, d]).astype(BF16)
        bbd.append(jnp.concatenate([br_, bi_], axis=2))
        bbdt_re.append(jnp.transpose(br_, (0, 2, 1)))
        bbdt_im.append(jnp.transpose(bi_, (0, 2, 1)))
        cbd_re.append(cr_)
        cbd_im.append(ci_)
        cbdt_re.append(jnp.transpose(cr_, (0, 2, 1)))
        cbdt_im.append(jnp.transpose(ci_, (0, 2, 1)))
    cos_t, sin_t = _rope_tables(L, Lc)
    qg, kg = q_norm_g, k_norm_g
    tiny = ['c_ctx', 'b_mod', 'norm_g', 'q_norm_g', 'k_norm_g', 'ssm_a_re', 'ssm_a_im', 'ssm_log_dt', 'ssm_d', 'b_glu']
    small = ['ssm_b_re', 'ssm_b_im', 'ssm_c_re', 'ssm_c_im']
    packs_wmv = [_pack([A[pre + n] for n in small]) for pre in ('', 'm_', 'v_')]
    prepared = packs_wmv + [cos_t, sin_t, coef_im[0], coef_im[1]] + [
        t[d][0] for t in (bbd, bbdt_re, bbdt_im, cbd_re, cbd_im, cbdt_re, cbdt_im) for d in range(2)]
    weights_ready(0, tok + sum(t[0:1, 0:1].astype(F32) for t in prepared))

    def norm_mod(name, xv, g, sh, sc):
        rows = xv.shape[0]
        return _rowk(name, lambda i, xt, gt, sht, sct: [_norm_mod(xt, gt, sel(i, sht), sel(i, sct))],
                     rows, tr, [(xv, 'row'), (g, 'vec'), (sh, 'vec'), (sc, 'vec')], [((rows, D), BF16, 'row')])[0]

    def swiglu_epi(accs, rows, vecs, ri):
        a_, b_ = accs
        return [a_, b_, a_ * _sigmoid(a_) * b_]

    def res_epi(coef):
        def epi(accs, rows, vecs, ri):
            gate = vecs[0]
            if gate.shape[0] == 2:
                gate = jnp.where(ri < Lc, gate[0:1], gate[1:2])
            return [accs[0], rows[0] + (coef * gate) * accs[0]]
        return epi

    def ffn_fwd(tag, h, xres, gate, down_ready=None):
        rows = h.shape[0]
        a_, b_, s_ = _mm(tag + "_up", [(h, Wt['w_' + tag + '_gate'], D), (h, Wt['w_' + tag + '_up'], D)], rows, F,
                         tm=_div(rows, 512), tn=F4, epi=swiglu_epi, outs=[(BF16, False), (BF16, False), (BF16, False)])
        if down_ready is not None:
            down_ready(s_)
        f_, xo = _mm(tag + "_down", [(s_, Wt['w_' + tag + '_down'], F)], rows, D, tm=_div(rows, 768),
                     tn=_div(D, 512), epi=res_epi(0.5), outs=[(F32, False), (F32, False)],
                     rows=[(xres, 0, 0)], vecs=[gate])
        return a_, b_, s_, f_, xo

    xc = jnp.concatenate([ctx[0], x[0]], axis=0)
    h1 = norm_mod("norm1", xc, ng[0:1], sh1, sc1)
    a1, b1, s1, f1, x1 = ffn_fwd("ffn1", h1, xc, g1, down_ready=lambda s_: weights_ready(1, s_))
    weights_ready(2, x1)
    h2 = norm_mod("norm2", x1, ng[1:2], sh2, sc2)
    proj = _mm("in_proj", [(h2, Wt['w_in'], D)], T, 4 * D, tm=_div(T, 768), tn=_div(D, 1024), epi=ident,
               outs=[(F32, False)])[0]
    nh, nkvh = D // HEAD_DIM, KV // HEAD_DIM

    def prep_fn(i, kt, vt, ut, qt, qgt, kgt, ct, st):
        qs = [_head_norm(qt[:, h * HEAD_DIM:(h + 1) * HEAD_DIM], qgt) for h in range(nh)]
        ks = [_head_norm(kt[:, h * HEAD_DIM:(h + 1) * HEAD_DIM], kgt) for h in range(nkvh)]
        qs = [v * ct + _rot(v) * st for v in qs]
        ks = [v * ct + _rot(v) * st for v in ks]
        return [jnp.concatenate(qs, axis=1), jnp.concatenate(ks, axis=1), vt, ut]

    qr, kr, vb, ub = _rowk(
        "qk_prep", prep_fn, T, tr,
        [(proj, ('col', KV, 0)), (proj, ('col', KV, 1)), (proj, ('col', W, 1)), (proj, ('col', D, 1)),
         (qg, 'vec'), (kg, 'vec'), (cos_t, 'row'), (sin_t, 'row')],
        [((T, D), BF16, 'row'), ((T, KV), BF16, 'row'), ((T, KV), BF16, 'row'), ((T, W), BF16, 'row')])
    _, mixer_weights = weights_pass(3, qr)
    attn = _attn_fwd(qr, kr, vb, L, Lc, D)
    hs_re, hs_im, ys = [], [], []
    lam_in = lam_re[0]
    for d in range(2):
        hr_, hi_, y_ = _ssm_fwd("ssm_fwd%d" % d, ub, bbd[d], cbd_re[d], cbd_im[d], lam_in, lam_im[d],
                                coef_re[d], coef_im[d], Lc, reverse=bool(d))
        hs_re.append(hr_)
        hs_im.append(hi_)
        ys.append(y_)
        if d == 0:
            tok_p4, ffn2_weights = weights_pass(4, y_)
            lam_in = lam_re[1] + tok_p4[0:1, 0:1]
    mixer_weights(ys[1])

    def ssm_out_fn(i, y0, y1, ut, dt):
        pre = dt * ut + y0 + y1
        yg_ = _gelu(pre)
        return [pre, yg_, yg_]

    ssm_pre, yg, ygb = _rowk(
        "ssm_out", ssm_out_fn, L, tr,
        [(ys[0], 'orow'), (ys[1], 'orow'), (proj, ('ocol', W, 1)), (ssm_d, 'vec')],
        [((L, W), F32, 'row'), ((L, W), F32, 'row'), ((L, W), BF16, 'row')], nc=ncr)

    def glu_epi(accs, rows, vecs, ri):
        z_ = accs[0] + vecs[0]
        return [z_, rows[0] * _sigmoid(z_)]

    zglu, y2 = _mm("glu", [(ygb, Wt['w_glu'], W)], L, W, tm=_div(L, 512), tn=_div(W, 512), epi=glu_epi,
                   outs=[(F32, False), (BF16, False)], rows=[(yg, 0, 0)], vecs=[b_glu])
    tnm = _div(Dq, 512)

    def merge_epi(accs, rows, vecs, ri):
        ga, gs = _sigmoid(rows[0]), _sigmoid(rows[1])
        return [accs[0], accs[1], ga * accs[0] + gs * accs[1]]

    ba, bs, merged = _mm("merge", [(attn, Wt['w_br_attn'], D), (y2, Wt['w_br_ssm'], W)], L, D, tm=tr, tn=tnm,
                         epi=merge_epi, outs=[(F32, False), (F32, False), (BF16, False)],
                         rows=[(proj, ncr, 2 * D // tnm), (proj, ncr, 3 * D // tnm)])
    mix, x2 = _mm("out_proj", [(merged, Wt['w_out'], D)], L, D, tm=tr, tn=_div(D, 1024), epi=res_epi(1.0),
                  outs=[(F32, False), (F32, False)], rows=[(x1, ncr, 0)], vecs=[g2])
    ffn2_weights(x2)
    h3 = norm_mod("norm3", x2, ng[2:3], sh3, sc3)
    a3, b3, s3, f3, x3 = ffn_fwd("ffn2", h3, x2, g3)

    def loss_fn(i, yt, tt_, ft, gt):
        diff = yt - tt_
        dy_ = diff * (1.0 / D)
        return [dy_, jnp.sum(diff * diff, axis=0, keepdims=True), (0.5 * gt) * dy_,
                jnp.sum(dy_ * ft, axis=0, keepdims=True) * 0.5]

    dy, sq, df3, dg3 = _rowk("loss", loss_fn, L, tr, [(x3, 'row'), (loss_target[0], 'row'), (f3, 'row'), (g3, 'vec')],
                             [((L, D), F32, 'row'), ((1, D), F32, 'acc'), ((L, D), BF16, 'row'), ((1, D), F32, 'acc')])
    loss = lax.psum(0.5 * jnp.sum(sq) / D, ("x", "y", "c"))

    def swiglu_bwd_epi(accs, rows, vecs, ri):
        ds_, a_, b_ = accs[0], rows[0].astype(F32), rows[1].astype(F32)
        sg = _sigmoid(a_)
        return [ds_ * b_ * (sg * (1.0 + a_ * (1.0 - sg))), ds_ * (a_ * sg)]

    def norm_mod_bwd(name, xv, g, sh, sc, dh, dres, dres_kind, branch=None, after=()):
        rows, nrow = xv.shape[0], sh.shape[0]

        def fn(i, xt, gt, sht, sct, dht, rest, *more):
            _, vjp = jax.vjp(_norm_mod, xt, gt, sel(i, sht), sel(i, sct))
            dx_, dg_, dsh_, dsc_ = vjp(dht)
            dx_ = dx_ + (jnp.where(i >= ncr, rest, 0.0) if dres_kind == 'xrow' else rest)
            out = [dx_, dg_, put(i, dsh_, nrow), put(i, dsc_, nrow)]
            if branch is not None:
                ft, gatet = more
                out += [(branch[2] * sel(i, gatet)) * dx_,
                        put(i, jnp.sum(dx_ * ft, axis=0, keepdims=True) * branch[2], gatet.shape[0])]
            return out

        ins = [(xv, 'row'), (g, 'vec'), (sh, 'vec'), (sc, 'vec'), (dh, 'row'), (dres, dres_kind)]
        outs = [((rows, D), F32, 'row'), ((1, D), F32, 'acc'), ((nrow, D), F32, 'acc'), ((nrow, D), F32, 'acc')]
        if branch is not None:
            ins += [(branch[0], 'row'), (branch[1], 'vec')]
            outs += [((rows, D), BF16, 'row'), ((branch[1].shape[0], D), F32, 'acc')]
        return _rowk(name, fn, rows, tr, ins, outs, nc=ncr, after=after)

    def ffn_bwd(tag, df, h, a_, b_, s_, wg, wu, wd, on_dwd=None):
        rows = df.shape[0]
        dwd = _mm(tag + "_dwd", [(s_, df, rows)], F, D, tm=_div(F, 512), tn=_div(D, 1024), ta=True, epi=ident,
                  outs=[(BF16, False)])[0].reshape(N_CHIPS, F4, D)
        if on_dwd is not None:
            on_dwd(dwd)
        da, db = _mm(tag + "_dact", [(df, wd, D)], rows, F, tm=_div(rows, 512), tn=F4, tb=True, epi=swiglu_bwd_epi,
                     outs=[(BF16, False), (BF16, False)], rows=[(a_, 0, 0), (b_, 0, 0)])
        dwg = _mm(tag + "_dwg", [(h, da, rows)], D, F, tm=_div(D, 512), tn=F4, ta=True, epi=ident,
                  outs=[(BF16, True)])[0]
        dwu = _mm(tag + "_dwu", [(h, db, rows)], D, F, tm=_div(D, 512), tn=F4, ta=True, epi=ident,
                  outs=[(BF16, True)])[0]
        dh = _mm(tag + "_dh", [(da, wg, F), (db, wu, F)], rows, D, tm=_div(rows, 768), tn=_div(D, 1024), nk=N_CHIPS,
                 tb=True, epi=ident, outs=[(F32, False)], summed=True)[0]
        return dh, dwg, dwu, dwd

    dh3, dwg2, dwu2, dwd2 = ffn_bwd("ffn2", df3, h3, a3, b3, s3, Wt['w_ffn2_gate'], Wt['w_ffn2_up'], Wt['w_ffn2_down'])
    tok_r1, scatter_fin1 = _scatter_split("scatter_ffn2", [dwg2, dwu2, dwd2], dg3)
    dx2, dng3, dsh3, dsc3, dmix, dg2 = norm_mod_bwd("norm3_bwd", x2, ng[2:3], sh3, sc3, dh3, dy, 'row',
                                                    branch=(mix, g2 + tok_r1[0:1, 0:1], 1.0))

    def dmerge_epi(accs, rows, vecs, ri):
        dm_, ba_, bs_ = accs[0], rows[0], rows[1]
        ga, gs = _sigmoid(rows[2]), _sigmoid(rows[3])
        return [dm_ * ga, dm_ * gs, dm_ * ba_ * ga * (1.0 - ga), dm_ * bs_ * gs * (1.0 - gs)]

    tnd = _div(D, 1024)
    dba, dbs, dga, dgs = _mm("dmerge", [(dmix, Wt['w_out'], D)], L, D, tm=tr, tn=tnd, tb=True, epi=dmerge_epi,
                             outs=[(BF16, False)] * 4,
                             rows=[(ba, 0, 0), (bs, 0, 0), (proj, ncr, 2 * D // tnd), (proj, ncr, 3 * D // tnd)])
    dwout = _mm("dw_out", [(merged, dmix, L)], D, D, tm=_div(D, 512), tn=_div(D, 1024), ta=True, epi=ident,
                outs=[(BF16, False)])[0].reshape(N_CHIPS, Dq, D)
    dattn = _mm("dattn", [(dba, Wt['w_br_attn'], D)], L, D, tm=_div(L, 512), tn=_div(D, 1024), tb=True, epi=ident,
                outs=[(BF16, False)])[0]
    dwba = _mm("dw_br_attn", [(attn, dba, L)], D, D, tm=_div(D, 512), tn=_div(D, 1024), ta=True, epi=ident,
               outs=[(BF16, False)])[0].reshape(N_CHIPS, Dq, D)
    dy2 = _mm("dy2", [(dbs, Wt['w_br_ssm'], D)], L, W, tm=_div(L, 512), tn=_div(W, 1024), nk=N_CHIPS, tb=True,
              epi=ident, outs=[(F32, False)])[0]
    dwbs = _mm("dw_br_ssm", [(y2, dbs, L)], W, D, tm=_div(W, 512), tn=_div(Dq, 512), ta=True, epi=ident,
               outs=[(BF16, True)])[0]

    def glu_bwd_fn(i, d2, ygt, zt):
        sz = _sigmoid(zt)
        dz_ = d2 * ygt * sz * (1.0 - sz)
        return [dz_, d2 * sz, jnp.sum(dz_, axis=0, keepdims=True)]

    dz, dyd, dbglu = _rowk("glu_bwd", glu_bwd_fn, L, tr, [(dy2, 'row'), (yg, 'row'), (zglu, 'row')],
                           [((L, W), BF16, 'row'), ((L, W), F32, 'row'), ((1, W), F32, 'acc')])

    def dssm_epi(accs, rows, vecs, ri):
        _, vjp = jax.vjp(_gelu, rows[1])
        ds_ = vjp(accs[0] + rows[0])[0]
        return [ds_, ds_]

    dssm, dssm_b = _mm("dssm", [(dz, Wt['w_glu'], W)], L, W, tm=_div(L, 512), tn=_div(W, 512), tb=True, epi=dssm_epi,
                       outs=[(F32, False), (BF16, False)], rows=[(dyd, 0, 0), (ssm_pre, 0, 0)])
    dwglu = _mm("dw_glu", [(ygb, dz, L)], W, W, tm=_div(W, 512), tn=_div(W, 1024), ta=True, epi=ident,
                outs=[(BF16, False)])[0].reshape(N_CHIPS, W // N_CHIPS, W)
    tok_r2a, scatter_fin2a = _scatter_split("scatter_mix", [dwglu, dwba, dwbs, dwout], dbglu)
    dssm_full = jnp.concatenate([jnp.zeros((Lc, W), BF16), dssm_b], axis=0)
    dus, dlam_re, dlam_im, dcoef_re, dcoef_im, dbf, dcf_re, dcf_im = [], [], [], [], [], [], [], []
    for d in range(2):
        r = _ssm_bwd("ssm_bwd%d" % d, dssm_full, hs_re[d], hs_im[d], ub, bbd[d], bbdt_re[d], bbdt_im[d],
                     cbdt_re[d], cbdt_im[d], lam_re[d] + tok_r2a[0:1, 0:1], lam_im[d], coef_re[d], coef_im[d], Lc,
                     reverse=bool(d))
        for lst, val in zip((dus, dlam_re, dlam_im, dcoef_re, dcoef_im, dbf, dcf_re, dcf_im), r):
            lst.append(val)
    dqr, dkr, dvf = _attn_bwd(qr, kr, vb, dattn, L, Lc, D)

    def prep_bwd_fn(i, qt, kt, ut, dqt, dkt, dvt, du0, du1, dst, dgat, dgst, dt, qgt, kgt, ct, st):
        live = i >= ncr
        dqt = jnp.where(live, dqt, 0.0)
        dst = jnp.where(live, dst, 0.0)
        dgat = jnp.where(live, dgat, jnp.zeros_like(dgat))
        dgst = jnp.where(live, dgst, jnp.zeros_like(dgst))
        dqs, dks = [], []
        dqg_ = jnp.zeros((1, HEAD_DIM), F32)
        dkg_ = jnp.zeros((1, HEAD_DIM), F32)
        for h in range(nh):
            hl = slice(h * HEAD_DIM, (h + 1) * HEAD_DIM)
            dn = dqt[:, hl] * ct + _rot(dqt[:, hl] * st)
            _, vjp = jax.vjp(_head_norm, qt[:, hl], qgt)
            dxh, dgh = vjp(dn)
            dqs.append(dxh)
            dqg_ = dqg_ + dgh
        for h in range(nkvh):
            hl = slice(h * HEAD_DIM, (h + 1) * HEAD_DIM)
            dn = dkt[:, hl] * ct + _rot(dkt[:, hl] * st)
            _, vjp = jax.vjp(_head_norm, kt[:, hl], kgt)
            dxh, dgh = vjp(dn)
            dks.append(dxh)
            dkg_ = dkg_ + dgh
        du_ = du0 + du1 + dst * dt
        dproj_ = jnp.concatenate([c_.astype(BF16) for c_ in dks + [dvt, du_] + dqs + [dgat, dgst]], axis=1)
        return [dproj_, dqg_, dkg_, jnp.sum(dst * ut, axis=0, keepdims=True)]

    dproj, dqg, dkg, dssd = _rowk(
        "qk_prep_bwd", prep_bwd_fn, T, tr,
        [(proj, ('col', D, 1)), (proj, ('col', KV, 0)), (proj, ('col', W, 1)), (dqr, 'xrow'), (dkr, 'row'),
         (dvf, 'row'), (dus[0], 'row'), (dus[1], 'row'), (dssm, 'xrow'), (dga, 'xrow'), (dgs, 'xrow'), (ssm_d, 'vec'),
         (qg, 'vec'), (kg, 'vec'), (cos_t, 'row'), (sin_t, 'row')],
        [((T, 4 * D), BF16, 'row'), ((1, HEAD_DIM), F32, 'acc'), ((1, HEAD_DIM), F32, 'acc'), ((1, W), F32, 'acc')],
        nc=ncr)
    dh2 = _mm("in_proj_dx", [(dproj, Wt['w_in'], 4 * D)], T, D, tm=_div(T, 768), tn=_div(D, 1024), nk=N_CHIPS, tb=True,
              epi=ident, outs=[(F32, False)])[0]
    dwin = _mm("in_proj_dw", [(h2, dproj, T)], D, 4 * D, tm=_div(D, 512), tn=_div(D, 1024), ta=True, epi=ident,
               outs=[(BF16, True)])[0]
    tok_r2, scatter_fin2 = _scatter_split("scatter_w_in", [dwin], dqg)
    dx1, dng2, dsh2, dsc2, df1, dg1 = norm_mod_bwd("norm2_bwd", x1, ng[1:2] + tok_r2[0:1, 0:1], sh2, sc2, dh2, dx2,
                                                   'xrow', branch=(f1, g1, 0.5))
    dh1, dwg1, dwu1, dwd1 = ffn_bwd("ffn1", df1, h1, a1, b1, s1, Wt['w_ffn1_gate'], Wt['w_ffn1_up'],
                                    Wt['w_ffn1_down'])
    dx0, dng1, dsh1, dsc1 = norm_mod_bwd("norm1_bwd", xc, ng[0:1], sh1, sc1, dh1, dx1, 'row')
    grad_x = dx0[Lc:][None]

    zD = jnp.zeros((1, D), F32)
    dmod_x = jnp.concatenate([dsh1[1:2], dsc1[1:2], dg1[1:2], dsh2[1:2], dsc2[1:2], dg2, dsh3, dsc3, dg3], axis=1)
    dmod_c = jnp.concatenate([dsh1[0:1], dsc1[0:1], dg1[0:1], dsh2[0:1], dsc2[0:1], zD, zD, zD, zD], axis=1)
    pieces = [dmod_x, dmod_c, dng1, dng2, dng3, dqg, dkg] + dlam_re + dlam_im + dcoef_re + dcoef_im + [dssd, dbglu]
    shapes = [p_.shape for p_ in pieces]
    pack = _pack(pieces)
    RP = pack.shape[0]
    pieces_b = dbf + dcf_re + dcf_im
    shapes_b = [p_.shape for p_ in pieces_b]
    pack_b = _pack(pieces_b, rows_mult=16).astype(BF16)
    RB = pack_b.shape[0]
    tok_small, small_gathered = _allgather_split("gather_small", pack, me, dng1)
    tok_small, small_gathered_b = _allgather_split("gather_small_b", pack_b, me, tok_small)
    tok_r3, scatter_fin3 = _scatter_split("scatter_ffn1_up", [dwg1, dwu1], tok_small)
    tok_r3, scatter_fin_dn = _scatter_split("scatter_ffn1_down", [dwd1], tok_r3)
    results = {}

    def sum_group(tag, names, fin, after_work):
        sent, landed = fin(after_work)
        plane = [_sum_plane("sum_" + n, g_, rb, chip_index) for n, g_, rb in zip(names, sent, landed)]
        tok_, swapped = _swap_split("swap_" + tag, plane, chip_index)
        return tok_, (names, swapped)

    def update_group(group, after_work):
        names, swapped = group
        mine, theirs = swapped(after_work)
        for n, m_, t_ in zip(names, mine, theirs):
            results[n] = _adamw("adamw_" + n, A[n], A['m_' + n], A['v_' + n], [m_, t_])

    tok_a, grp_ffn2 = sum_group("ffn2", big[3:6], scatter_fin1, tok_r3)
    tok_b, grp_mix = sum_group("mix", big[7:11], scatter_fin2a, tok_a)
    tok_c, grp_w_in = sum_group("w_in", big[6:7], scatter_fin2, tok_b)
    update_group(grp_ffn2, tok_c)
    update_group(grp_mix, results['w_ffn2_down'][0])
    update_group(grp_w_in, results['w_out'][0])
    allp = small_gathered(results['w_in'][0])
    head_rows = -(-18 * D // PACK_W)
    head = allp[:, :head_rows].reshape(N_DEV, head_rows * PACK_W)
    dmx_all = head[:, :9 * D]

    def sum_rows_fn(i, t):
        s_ = t[0:1]
        for k in range(1, N_DEV):
            s_ = s_ + t[k:k + 1]
        return [s_]

    dmc_sum = _rowk("sum_dmod_c", sum_rows_fn, 1, 1, [(head[:, 9 * D:18 * D], 'vec')], [((1, 9 * D), F32, 'row')])[0]
    cots = jnp.concatenate([dmx_all, dmc_sum, jnp.zeros((7, 9 * D), F32)], axis=0)
    cots_sh = lax.dynamic_slice(cots, (0, chip * NM), (16, NM))
    part = _mm("cctx_part", [(cots_sh[8:16], wm, NM)], 8, D, tm=8, tn=_div(D, 1024), nk=NM // _div(NM, 1152), tb=True,
               epi=ident, outs=[(F32, False)], a_pro=to_bf, b_pro=to_bf)[0]
    _, cctx_gathered = _allgather_split("gather_cctx", part, me, part)

    def sum_dev_fn(i, t):
        s_ = t[0].astype(F32)
        for k in range(1, N_DEV):
            s_ = s_ + t[k].astype(F32)
        return [s_]

    tot = _rowk("sum_small", sum_dev_fn, RP, 8, [(allp, 'row3')], [((RP, PACK_W), F32, 'row')])[0]
    (t_dmod_x, t_dmod_c, t_ng1, t_ng2, t_ng3, t_qg, t_kg, t_lr0, t_lr1, t_li0, t_li1, t_kr0, t_kr1, t_ki0, t_ki1,
     t_d, t_bglu) = _unpack(tot, shapes)
    allb = small_gathered_b(tot)
    tot_b = _rowk("sum_small_b", sum_dev_fn, RB, 16, [(allb, 'row3')], [((RB, PACK_W), F32, 'row')])[0]
    t_dbf0, t_dbf1, t_dcr0, t_dcr1, t_dci0, t_dci1 = _unpack(tot_b, shapes_b)
    b_grad = lambda t, lo: jnp.transpose(t[:, :, lo:lo + P].reshape(G, E, P), (0, 2, 1))
    c_grad = lambda t: jnp.transpose(t.reshape(nslab, P, SLAB_GROUPS, E), (0, 2, 3, 1)).reshape(G, E, P)
    cat2 = lambda u0, u1: jnp.concatenate([u0.reshape(G, P), u1.reshape(G, P)], axis=0)
    g_are, g_aim, g_ldt = _zoh_bwd(a_re2, a_im2, ldt2, [cat2(t_lr0, t_lr1), cat2(t_li0, t_li1),
                                                         cat2(t_kr0, t_kr1), cat2(t_ki0, t_ki1)])
    g_bmod = _rowk("bmod_grad", lambda i, u0, u1: [u0 + u1], 1, 1, [(t_dmod_x, 'row'), (t_dmod_c, 'row')],
                   [((1, 9 * D), F32, 'row')])[0]
    results['w_mod'] = tuple(_adamw_outer("adamw_w_mod", w_mod, m_w_mod, v_w_mod, acts, cots_sh))
    done = sum(results[n][1].reshape(-1, results[n][1].shape[-1])[0:1, 0:1] for n in list(results)) + g_are[0:1, 0:1] \
        + g_bmod[0:1, 0:1]
    tok_e, grp_up = sum_group("ffn1_up", big[0:2], scatter_fin3, done)
    parts = cctx_gathered(tok_e).reshape(N_CHIPS, 2, 8, D)[:, 0, 0]

    def cctx_fn(i, pt, ct):
        ds_ = ((pt[0:1] + pt[1:2]) + pt[2:3]) + pt[3:4]
        _, vjp = jax.vjp(lambda v: v * _sigmoid(v), ct)
        return [vjp(ds_)[0]]

    g_cctx = _rowk("cctx_grad", cctx_fn, 1, 1, [(parts, 'vec'), (c_ctx[None], 'row')], [((1, D), F32, 'row')])[0]

    ng_full = jnp.concatenate([t_ng1, t_ng2, t_ng3], axis=0)
    gsmall = {
        'c_ctx': g_cctx, 'b_mod': g_bmod, 'norm_g': lax.dynamic_slice(ng_full, (0, chip * Dq), (3, Dq)),
        'q_norm_g': t_qg, 'k_norm_g': t_kg, 'ssm_a_re': g_are, 'ssm_a_im': g_aim, 'ssm_log_dt': g_ldt,
        'ssm_b_re': jnp.stack([b_grad(t_dbf0, 0), b_grad(t_dbf1, 0)]),
        'ssm_b_im': jnp.stack([b_grad(t_dbf0, P), b_grad(t_dbf1, P)]),
        'ssm_c_re': jnp.stack([c_grad(t_dcr0), c_grad(t_dcr1)]), 'ssm_c_im': jnp.stack([c_grad(t_dci0), c_grad(t_dci1)]),
        'ssm_d': t_d, 'b_glu': t_bglu}
    sshapes = [A[n].shape for n in small]
    sres = _adamw("adamw_small", packs_wmv[0], packs_wmv[1], packs_wmv[2], [_pack([gsmall[n] for n in small])])
    update_group(grp_up, sres[0])
    tok_d, grp_down = sum_group("ffn1_down", big[2:3], scatter_fin_dn, results['w_ffn1_up'][0])
    update_group(grp_down, tok_d)
    sres = [_unpack(b_, sshapes) for b_ in sres]
    for k, n in enumerate(small):
        results[n] = tuple(sres[q][k] for q in range(4))
    as2d = lambda v: v.reshape(1, -1) if v.ndim == 1 else v
    tres = _adamw_whole("adamw_tiny", [as2d(A[n]) for n in tiny], [as2d(A['m_' + n]) for n in tiny],
                        [as2d(A['v_' + n]) for n in tiny], [gsmall[n].reshape(as2d(A[n]).shape) for n in tiny])
    for n, res in zip(tiny, tres):
        results[n] = res

    order = ['c_ctx', 'w_mod', 'b_mod', 'norm_g', 'w_ffn1_gate', 'w_ffn1_up', 'w_ffn1_down', 'w_in', 'q_norm_g',
             'k_norm_g', 'ssm_a_re', 'ssm_a_im', 'ssm_log_dt', 'ssm_b_re', 'ssm_b_im', 'ssm_c_re', 'ssm_c_im',
             'ssm_d', 'w_glu', 'b_glu', 'w_br_attn', 'w_br_ssm', 'w_out', 'w_ffn2_gate', 'w_ffn2_up', 'w_ffn2_down']
    outs = [loss, grad_x]
    for q in range(4):
        outs += [results[n][q].reshape(A[n].shape) for n in order]
    return tuple(outs)
```

```python
import math

import jax
import jax.numpy as jnp
from jax import lax
from jax.experimental import pallas as pl
from jax.experimental.pallas import tpu as pltpu

F32 = jnp.float32
BF16 = jnp.bfloat16
MESH = pl.DeviceIdType.MESH

NORM_EPS = 1e-6
ROPE_THETA = 10000.0
GRID_W = 64
HEAD_DIM = 128
Q_PER_KV = 4
SSM_GROUP = 16
SSM_STATE = 64
ADAM_LR = 0.001
ADAM_B1 = 0.9
ADAM_B2 = 0.999
ADAM_EPS = 1e-08
ADAM_WD = 0.01
ADAM_STEP = 10

N_CHIPS = 4
N_DEV = 8
LANES = 128
SLAB_CH = 128
SLAB_GROUPS = SLAB_CH // SSM_GROUP
SLAB_ST = SLAB_GROUPS * SSM_STATE
VMEM_LIMIT_BYTES = 56 * 1024 * 1024
PACK_W = 1024


def _cparams(**kw):
    return pltpu.CompilerParams(vmem_limit_bytes=VMEM_LIMIT_BYTES, **kw)


def _div(n, pref, mult=LANES):
    t = (min(pref, n) // mult) * mult
    while t >= mult:
        if n % t == 0:
            return t
        t -= mult
    return n


def _sigmoid(x):
    return jax.nn.sigmoid(x)


def _gelu(x):
    return x * (0.5 * (1.0 + jnp.tanh(math.sqrt(2.0 / math.pi) * (x + 0.044715 * (x * x * x)))))


def _rowk(name, fn, nrows, tr, ins, outs, nc=0, after=()):
    nt = nrows // tr
    in_specs, arrays = [], []
    for arr, kind in ins:
        arrays.append(arr)
        if kind == 'row':
            in_specs.append(pl.BlockSpec((tr, arr.shape[1]), lambda i: (i, 0)))
        elif kind == 'xrow':
            in_specs.append(pl.BlockSpec((tr, arr.shape[1]), lambda i: (jnp.maximum(i - nc, 0), 0)))
        elif kind == 'orow':
            in_specs.append(pl.BlockSpec((tr, arr.shape[1]), lambda i: (i + nc, 0)))
        elif kind == 'vec':
            in_specs.append(pl.BlockSpec(arr.shape, lambda i, nd=arr.ndim: (0,) * nd))
        elif kind == 'row3':
            in_specs.append(pl.BlockSpec((arr.shape[0], tr, arr.shape[2]), lambda i: (0, i, 0)))
        elif kind == 'row1':
            in_specs.append(pl.BlockSpec((None, tr, arr.shape[2]), lambda i: (0, i, 0)))
        elif kind[0] == 'ocol':
            _, width, blk = kind
            in_specs.append(pl.BlockSpec((tr, width), lambda i, blk=blk: (i + nc, blk)))
        else:
            _, width, blk = kind
            in_specs.append(pl.BlockSpec((tr, width), lambda i, blk=blk: (i, blk)))
    out_shape, out_specs = [], []
    for shape, dtype, kind in outs:
        out_shape.append(jax.ShapeDtypeStruct(shape, dtype))
        if kind == 'row':
            out_specs.append(pl.BlockSpec((tr, shape[1]), lambda i: (i, 0)))
        elif kind == 'row1':
            out_specs.append(pl.BlockSpec((None, tr, shape[2]), lambda i: (0, i, 0)))
        else:
            out_specs.append(pl.BlockSpec(shape, lambda i, nd=len(shape): (0,) * nd))
    nin = len(ins)
    for arr in after:
        arrays.append(arr)
        in_specs.append(pl.BlockSpec(memory_space=pl.ANY))
    nafter = len(after)

    def body(*refs):
        i = pl.program_id(0)
        res = fn(i, *[r[...] for r in refs[:nin]])
        for (shape, dtype, kind), ref, val in zip(outs, refs[nin + nafter:], res):
            if kind in ('row', 'row1'):
                ref[...] = val.astype(dtype)
            else:
                @pl.when(i == 0)
                def _():
                    ref[...] = val.astype(dtype)

                @pl.when(i > 0)
                def _():
                    ref[...] += val.astype(dtype)

    return pl.pallas_call(body, name=name, grid=(nt,), in_specs=in_specs, out_specs=out_specs,
                          out_shape=out_shape, compiler_params=_cparams())(*arrays)


def _mm(name, pairs, M, N, *, tm, tn, nk=1, epi, outs, ta=False, tb=False, rows=(), vecs=(),
        a_pro=None, b_pro=None, n_outer=True, summed=False):
    nm, nn = M // tm, N // tn
    npair = len(pairs)

    def idx(f):
        if n_outer:
            return lambda j, i, k: f(i, j, k)
        return lambda i, j, k: f(i, j, k)

    in_specs, args = [], []
    for a, b, K in pairs:
        tk = K // nk
        if ta:
            in_specs.append(pl.BlockSpec((tk, tm), idx(lambda i, j, k: (k, i))))
        else:
            in_specs.append(pl.BlockSpec((tm, tk), idx(lambda i, j, k: (i, k))))
        args.append(a)
        if b.ndim == 3:
            if tb:
                per = b.shape[2] // tk
                in_specs.append(pl.BlockSpec((None, tn, tk), idx(lambda i, j, k, per=per: (k // per, j, k % per))))
            else:
                per = b.shape[2] // tn
                in_specs.append(pl.BlockSpec((None, tk, tn), idx(lambda i, j, k, per=per: (j // per, k, j % per))))
        elif tb:
            in_specs.append(pl.BlockSpec((tn, tk), idx(lambda i, j, k: (j, k))))
        else:
            in_specs.append(pl.BlockSpec((tk, tn), idx(lambda i, j, k: (k, j))))
        args.append(b)
    for arr, ro, co in rows:
        in_specs.append(pl.BlockSpec((tm, tn), idx(lambda i, j, k, ro=ro, co=co: (i + ro, j + co))))
        args.append(arr)
    for arr in vecs:
        in_specs.append(pl.BlockSpec((arr.shape[0], tn), idx(lambda i, j, k: (0, j))))
        args.append(arr)
    out_shape, out_specs = [], []
    for dtype, chunked in outs:
        if chunked:
            per = (N // N_CHIPS) // tn
            out_shape.append(jax.ShapeDtypeStruct((N_CHIPS, M, N // N_CHIPS), dtype))
            out_specs.append(pl.BlockSpec((None, tm, tn), idx(lambda i, j, k, per=per: (j // per, i, j % per))))
        else:
            out_shape.append(jax.ShapeDtypeStruct((M, N), dtype))
            out_specs.append(pl.BlockSpec((tm, tn), idx(lambda i, j, k: (i, j))))
    nacc = 1 if summed else npair
    scratch = [pltpu.VMEM((tm, tn), F32) for _ in range(nacc)] if nk > 1 else []
    nrow, nvec, nout = len(rows), len(vecs), len(outs)
    dims = (((0 if ta else 1,), (1 if tb else 0,)), ((), ()))

    def body(*refs):
        ab = refs[:2 * npair]
        row_refs = refs[2 * npair:2 * npair + nrow]
        vec_refs = refs[2 * npair + nrow:2 * npair + nrow + nvec]
        out_refs = refs[2 * npair + nrow + nvec:2 * npair + nrow + nvec + nout]
        acc_refs = refs[2 * npair + nrow + nvec + nout:]
        if n_outer:
            j, i, k = pl.program_id(0), pl.program_id(1), pl.program_id(2)
        else:
            i, j, k = pl.program_id(0), pl.program_id(1), pl.program_id(2)

        def part(p):
            av, bv = ab[2 * p][...], ab[2 * p + 1][...]
            if a_pro is not None:
                av = a_pro(av)
            if b_pro is not None:
                bv = b_pro(bv)
            return lax.dot_general(av, bv, dims, preferred_element_type=F32)

        def finish(accs):
            row_index = i * tm + lax.broadcasted_iota(jnp.int32, (tm, 1), 0)
            res = epi(accs, [r[...] for r in row_refs], [v[...] for v in vec_refs], row_index)
            for ref, val in zip(out_refs, res):
                ref[...] = val.astype(ref.dtype)

        parts = [part(p) for p in range(npair)]
        if summed:
            total = parts[0]
            for extra in parts[1:]:
                total = total + extra
            parts = [total]
        if nk == 1:
            finish(parts)
        else:
            @pl.when(k == 0)
            def _():
                for q in range(nacc):
                    acc_refs[q][...] = parts[q]

            @pl.when(jnp.logical_and(k > 0, k < nk - 1))
            def _():
                for q in range(nacc):
                    acc_refs[q][...] += parts[q]

            @pl.when(k == nk - 1)
            def _():
                finish([acc_refs[q][...] + parts[q] for q in range(nacc)])

    grid = (nn, nm, nk) if n_outer else (nm, nn, nk)
    return pl.pallas_call(body, name=name, grid=grid, in_specs=in_specs, out_specs=out_specs,
                          out_shape=out_shape, scratch_shapes=scratch, compiler_params=_cparams())(*args)


def _split3(v):
    v0 = v.astype(BF16)
    r1 = v - v0.astype(F32)
    v1 = r1.astype(BF16)
    v2 = (r1 - v1.astype(F32)).astype(BF16)
    return v0, v1, v2


def _mesh_pos():
    return lax.axis_index("x"), lax.axis_index("y"), lax.axis_index("c")


def _allgather_small(name, x):
    m, n = x.shape

    def body(x_ref, out_ref, send_sems, recv_sems, local_sem):
        xi, yi, ci = _mesh_pos()
        me, sibling = (xi, yi, ci), (xi, yi, 1 - ci)
        chips = [(1 - xi, yi), (xi, 1 - yi), (1 - xi, 1 - yi)]

        def rows(px, py, pc):
            return out_ref.at[pl.ds((4 * px + 2 * py + pc) * m, m), :]

        def copy(k, block, to, src=None):
            return pltpu.make_async_remote_copy(
                src_ref=rows(*block) if src is None else src, dst_ref=rows(*block),
                send_sem=send_sems.at[k], recv_sem=recv_sems.at[k], device_id=to, device_id_type=MESH)

        mine = pltpu.make_async_copy(x_ref, rows(*me), local_sem)
        mine.start()
        first = [copy(0, me, sibling, src=x_ref)]
        first += [copy(1 + j, me, (*chip, ci), src=x_ref) for j, chip in enumerate(chips)]
        for cp in first:
            cp.start()
        passed = [copy(4 + j, (*chip, ci), sibling) for j, chip in enumerate(chips)]
        for j, chip in enumerate(chips):
            copy(1 + j, (*chip, ci), me).wait_recv()
            passed[j].start()
        copy(0, sibling, me).wait_recv()
        for j, chip in enumerate(chips):
            copy(4 + j, (*chip, 1 - ci), me).wait_recv()
        for cp in first + passed:
            cp.wait_send()
        mine.wait()

    return pl.pallas_call(
        body, name=name, out_shape=jax.ShapeDtypeStruct((N_DEV * m, n), x.dtype),
        in_specs=[pl.BlockSpec(memory_space=pltpu.VMEM)], out_specs=pl.BlockSpec(memory_space=pltpu.VMEM),
        scratch_shapes=[pltpu.SemaphoreType.DMA((7,)), pltpu.SemaphoreType.DMA((7,)), pltpu.SemaphoreType.DMA],
        compiler_params=_cparams())(x)


_HBM = pl.BlockSpec(memory_space=pltpu.HBM)
_SEM = pl.BlockSpec(memory_space=pltpu.SEMAPHORE)
_ANY = pl.BlockSpec(memory_space=pl.ANY)
_EFFECT = pltpu.SideEffectType.DATAFLOW_SIDE_EFFECTING


def _in_hbm(v):
    return pltpu.with_memory_space_constraint(v, pltpu.HBM)


def _other_chips(xi, yi):
    return [(1 - xi, yi), (xi, 1 - yi), (1 - xi, 1 - yi)]


def _guarded(core, fn):
    if core is None:
        fn()
    else:
        pl.when(lax.axis_index("c") == core)(fn)


def _split_copies(name, srcs, lands, after, pairs, senders, receivers, ncopy):
    ns, nl = len(srcs), len(lands)
    dma = pltpu.SemaphoreType.DMA((ncopy,))
    thru = [pltpu.HBM(v.shape, v.dtype) for v in list(srcs) + list(lands)]

    def start_body(*refs):
        src_refs, land_refs = refs[:ns], refs[ns:ns + nl]
        descs = pairs(src_refs, land_refs, refs[ns + nl + 1], refs[ns + nl + 2])

        def go():
            for send, _ in descs:
                send.start()

        _guarded(senders, go)
        refs[-1][...] = jnp.zeros_like(refs[-1])

    res = pl.pallas_call(
        start_body, name=name + "_start",
        out_shape=(dma, dma, *thru, jax.ShapeDtypeStruct((8, LANES), F32)),
        in_specs=[_HBM] * (ns + nl) + [_ANY],
        out_specs=(_SEM, _SEM, *([_HBM] * (ns + nl)), pl.BlockSpec(memory_space=pltpu.VMEM)),
        input_output_aliases={k: 2 + k for k in range(ns + nl)},
        compiler_params=_cparams(has_side_effects=_EFFECT),
    )(*[_in_hbm(v) for v in srcs], *[_in_hbm(v) for v in lands], after)
    send_sems, recv_sems, token = res[0], res[1], res[-1]
    carried = res[2:2 + ns + nl]

    def finish(after_work):
        def wait_body(*refs):
            src_refs, land_refs = refs[:ns], refs[ns:ns + nl]
            descs = pairs(src_refs, land_refs, refs[ns + nl], refs[ns + nl + 1])

            def sent():
                for send, _ in descs:
                    send.wait_send()

            def landed():
                for _, recv in descs:
                    recv.wait_recv()

            _guarded(senders, sent)
            _guarded(receivers, landed)

        out = pl.pallas_call(
            wait_body, name=name + "_wait", out_shape=tuple(thru),
            in_specs=[_HBM] * (ns + nl) + [_SEM, _SEM, _ANY], out_specs=tuple([_HBM] * (ns + nl)),
            input_output_aliases={k: k for k in range(ns + nl)},
            compiler_params=_cparams(has_side_effects=_EFFECT),
        )(*carried, send_sems, recv_sems, after_work)
        return list(out[:ns]), list(out[ns:])

    return token, finish


def _cast_slot(name, w, chip_index, after):
    R, C = w.shape[1:]
    tr = _div(R, max(16, 524288 // C), mult=16)

    def body(chip_ref, w_ref, after_ref, o_ref):
        o_ref[...] = w_ref[...].astype(BF16)

    return pl.pallas_call(
        body, name=name, out_shape=jax.ShapeDtypeStruct((N_CHIPS, R, C), BF16),
        grid_spec=pltpu.PrefetchScalarGridSpec(
            num_scalar_prefetch=1, grid=(R // tr,),
            in_specs=[pl.BlockSpec((None, tr, C), lambda i, chip_ref: (0, i, 0)), _ANY],
            out_specs=pl.BlockSpec((None, tr, C), lambda i, chip_ref: (chip_ref[0], i, 0))),
        compiler_params=_cparams())(chip_index, w, after)


def _sum_plane(name, grads, landed, chip_index):
    R, C = grads.shape[1:]
    tr = _div(R, max(16, 1048576 // C), mult=16)

    def body(chip_ref, own_ref, land_ref, o_ref):
        o_ref[...] = ((own_ref[...].astype(F32) + land_ref[0].astype(F32)) + land_ref[1].astype(F32)) \
            + land_ref[2].astype(F32)

    return pl.pallas_call(
        body, name=name, out_shape=jax.ShapeDtypeStruct((R, C), F32),
        grid_spec=pltpu.PrefetchScalarGridSpec(
            num_scalar_prefetch=1, grid=(R // tr,),
            in_specs=[pl.BlockSpec((None, tr, C), lambda i, chip_ref: (chip_ref[0], i, 0)),
                      pl.BlockSpec((3, tr, C), lambda i, chip_ref: (0, i, 0))],
            out_specs=pl.BlockSpec((tr, C), lambda i, chip_ref: (i, 0))),
        compiler_params=_cparams())(chip_index, grads, landed)


def _gather_split(name, lands, after):
    def pairs(src_refs, land_refs, send_sems, recv_sems):
        xi, yi, _ = _mesh_pos()
        mine = 2 * xi + yi
        out = []
        for a in range(len(lands)):
            for j, (px, py) in enumerate(_other_chips(xi, yi)):
                def to_slot(slot, a=a, j=j, px=px, py=py):
                    return pltpu.make_async_remote_copy(
                        src_ref=land_refs[a].at[mine], dst_ref=land_refs[a].at[slot], send_sem=send_sems.at[3 * a + j],
                        recv_sem=recv_sems.at[3 * a + j], device_id=(px, py, 1), device_id_type=MESH)
                out.append((to_slot(mine), to_slot(2 * px + py)))
        return out

    return _split_copies(name, [], lands, after, pairs, senders=1, receivers=1, ncopy=3 * len(lands))


def _allgather_split(name, block, me, after):
    land = lax.dynamic_update_slice(lax.empty((N_DEV,) + block.shape, block.dtype), block[None], (me, 0, 0))

    def pairs(src_refs, land_refs, send_sems, recv_sems):
        xi, yi, ci = _mesh_pos()
        mine = 4 * xi + 2 * yi + ci
        out = []
        for k in range(1, N_DEV):
            kx, ky, kc = (k >> 2) & 1, (k >> 1) & 1, k & 1
            px = 1 - xi if kx else xi
            py = 1 - yi if ky else yi
            pc = 1 - ci if kc else ci

            def to_slot(slot, k=k, px=px, py=py, pc=pc):
                return pltpu.make_async_remote_copy(
                    src_ref=land_refs[0].at[mine], dst_ref=land_refs[0].at[slot], send_sem=send_sems.at[k - 1],
                    recv_sem=recv_sems.at[k - 1], device_id=(px, py, pc), device_id_type=MESH)
            out.append((to_slot(mine), to_slot(4 * px + 2 * py + pc)))
        return out

    tok, fin = _split_copies(name, [], [land], after, pairs, senders=None, receivers=None, ncopy=N_DEV - 1)
    return tok, lambda later: fin(later)[1][0]


def _swap_split(name, arrs, after):
    lands = [lax.empty(v.shape, v.dtype) for v in arrs]

    def pairs(src_refs, land_refs, send_sems, recv_sems):
        xi, yi, ci = _mesh_pos()
        out = []
        for a in range(len(arrs)):
            cp = pltpu.make_async_remote_copy(
                src_ref=src_refs[a], dst_ref=land_refs[a], send_sem=send_sems.at[a], recv_sem=recv_sems.at[a],
                device_id=(xi, yi, 1 - ci), device_id_type=MESH)
            out.append((cp, cp))
        return out

    return _split_copies(name, arrs, lands, after, pairs, senders=None, receivers=None, ncopy=len(arrs))


def _pass_split(name, lands, after):
    def pairs(src_refs, land_refs, send_sems, recv_sems):
        xi, yi, _ = _mesh_pos()
        out = []
        for a in range(len(lands)):
            for j, (px, py) in enumerate(_other_chips(xi, yi)):
                cp = pltpu.make_async_remote_copy(
                    src_ref=land_refs[a].at[2 * px + py], dst_ref=land_refs[a].at[2 * px + py],
                    send_sem=send_sems.at[3 * a + j], recv_sem=recv_sems.at[3 * a + j],
                    device_id=(xi, yi, 0), device_id_type=MESH)
                out.append((cp, cp))
        return out

    return _split_copies(name, [], lands, after, pairs, senders=1, receivers=0, ncopy=3 * len(lands))


def _scatter_split(name, grads, after):
    lands = [lax.empty((3,) + g.shape[1:], g.dtype) for g in grads]

    def pairs(src_refs, land_refs, send_sems, recv_sems):
        xi, yi, ci = _mesh_pos()
        out = []
        for a in range(len(grads)):
            for j, (px, py) in enumerate(_other_chips(xi, yi)):
                cp = pltpu.make_async_remote_copy(
                    src_ref=src_refs[a].at[2 * px + py], dst_ref=land_refs[a].at[j], send_sem=send_sems.at[3 * a + j],
                    recv_sem=recv_sems.at[3 * a + j], device_id=(px, py, ci), device_id_type=MESH)
                out.append((cp, cp))
        return out

    return _split_copies(name, grads, lands, after, pairs, senders=None, receivers=None, ncopy=3 * len(grads))


def _gather_finish(name, lands):
    na = len(lands)

    def body(*refs):
        outs = refs[na:2 * na]
        send_sems, recv_sems = refs[2 * na:]
        xi, yi, ci = _mesh_pos()
        passes = [pltpu.make_async_remote_copy(
            src_ref=outs[a].at[2 * px + py], dst_ref=outs[a].at[2 * px + py],
            send_sem=send_sems.at[a, j], recv_sem=recv_sems.at[a, j], device_id=(xi, yi, 0), device_id_type=MESH)
            for a in range(na) for j, (px, py) in enumerate(_other_chips(xi, yi))]

        @pl.when(ci == 1)
        def _():
            for cp in passes:
                cp.start()
            for cp in passes:
                cp.wait_send()

        @pl.when(ci == 0)
        def _():
            for cp in passes:
                cp.wait_recv()

    return pl.pallas_call(
        body, name=name, out_shape=[jax.ShapeDtypeStruct(v.shape, v.dtype) for v in lands],
        in_specs=[_ANY] * na, out_specs=[_ANY] * na,
        input_output_aliases={a: a for a in range(na)},
        scratch_shapes=[pltpu.SemaphoreType.DMA((na, 3)), pltpu.SemaphoreType.DMA((na, 3))],
        compiler_params=_cparams())(*lands)


ATTN_HEADS_PER_STEP = 2


def _attn_tiles(L, Lc, D, tq_pref=256):
    tq = min(tq_pref, Lc)
    return tq, L // tq, Lc // tq, D // HEAD_DIM // Q_PER_KV


def _attn_scores(q, k):
    return lax.dot_general(q, k, (((1,), (1,)), ((), ())), preferred_element_type=F32) * (HEAD_DIM ** -0.5)


def _softmax_rows(s):
    e = jnp.exp(s - jnp.max(s, axis=-1, keepdims=True))
    return e * (1.0 / jnp.sum(e, axis=-1, keepdims=True))


def _attn_probs(q, k):
    return _softmax_rows(_attn_scores(q, k))


def _attn_fwd(qr, kr, v, L, Lc, D):
    T = L + Lc
    tq, nq, qoff, nkv = _attn_tiles(L, Lc, D)
    hp = Q_PER_KV
    ng = Q_PER_KV // hp

    def body(q_ref, k_ref, v_ref, o_ref):
        k, vv = k_ref[...], v_ref[...]
        heads = [slice(r * HEAD_DIM, (r + 1) * HEAD_DIM) for r in range(hp)]
        scores = [_attn_scores(q_ref[:, cols], k) for cols in heads]
        probs = [_softmax_rows(s) for s in scores]
        for cols, p in zip(heads, probs):
            o_ref[:, cols] = jnp.dot(p.astype(BF16), vv, preferred_element_type=F32).astype(o_ref.dtype)

    kv_spec = pl.BlockSpec((T, HEAD_DIM), lambda h, r, q: (0, h))
    return pl.pallas_call(
        body, name="attn_fwd", grid=(nkv, ng, nq),
        in_specs=[pl.BlockSpec((tq, hp * HEAD_DIM), lambda h, r, q: (q + qoff, h * ng + r)), kv_spec, kv_spec],
        out_specs=pl.BlockSpec((tq, hp * HEAD_DIM), lambda h, r, q: (q, h * ng + r)),
        out_shape=jax.ShapeDtypeStruct((L, D), BF16), compiler_params=_cparams())(qr, kr, v)


def _attn_bwd(qr, kr, v, do, L, Lc, D):
    T = L + Lc
    tq, nq, qoff, nkv = _attn_tiles(L, Lc, D, 256)
    scale = HEAD_DIM ** -0.5
    hp = Q_PER_KV
    ng = Q_PER_KV // hp

    def body(q_ref, k_ref, v_ref, do_ref, dq_ref, dk_ref, dv_ref):
        first = jnp.logical_and(pl.program_id(1) == 0, pl.program_id(2) == 0)
        k, vv = k_ref[...], v_ref[...]
        nt_dims, tn_dims = (((1,), (1,)), ((), ())), (((0,), (0,)), ((), ()))
        heads = [slice(r * HEAD_DIM, (r + 1) * HEAD_DIM) for r in range(hp)]
        qs = [q_ref[:, cols] for cols in heads]
        douts = [do_ref[:, cols] for cols in heads]
        scores = [_attn_scores(q, k) for q in qs]
        dps = [lax.dot_general(dout, vv, nt_dims, preferred_element_type=F32) for dout in douts]
        probs = [_softmax_rows(s) for s in scores]
        dss = [(p * (dp - jnp.sum(p * dp, axis=-1, keepdims=True)) * scale).astype(BF16) for p, dp in zip(probs, dps)]
        for cols, ds in zip(heads, dss):
            dq_ref[:, cols] = jnp.dot(ds, k, preferred_element_type=F32)
        dk = dv = None
        for q, dout, p, ds in zip(qs, douts, probs, dss):
            dk_r = lax.dot_general(ds, q, tn_dims, preferred_element_type=F32)
            dv_r = lax.dot_general(p.astype(BF16), dout, tn_dims, preferred_element_type=F32)
            dk = dk_r if dk is None else dk + dk_r
            dv = dv_r if dv is None else dv + dv_r

        @pl.when(first)
        def _():
            dk_ref[...] = dk
            dv_ref[...] = dv

        @pl.when(jnp.logical_not(first))
        def _():
            dk_ref[...] += dk
            dv_ref[...] += dv

    kv_spec = pl.BlockSpec((T, HEAD_DIM), lambda h, r, q: (0, h))
    q_spec = pl.BlockSpec((tq, hp * HEAD_DIM), lambda h, r, q: (q + qoff, h * ng + r))
    o_spec = pl.BlockSpec((tq, hp * HEAD_DIM), lambda h, r, q: (q, h * ng + r))
    return pl.pallas_call(
        body, name="attn_bwd", grid=(nkv, ng, nq),
        in_specs=[q_spec, kv_spec, kv_spec, o_spec], out_specs=[o_spec, kv_spec, kv_spec],
        out_shape=[jax.ShapeDtypeStruct((L, D), F32), jax.ShapeDtypeStruct((T, D // Q_PER_KV), F32),
                   jax.ShapeDtypeStruct((T, D // Q_PER_KV), F32)],
        compiler_params=_cparams())(qr, kr, v, do)


SUB = 8


def _doubling(xr, xi, pw_re, pw_im, lanes, first_power, period, reverse):
    n = xr.shape[0]
    rows = lax.broadcasted_iota(jnp.int32, (n, 1), 0) & (period - 1)
    for k in range(period.bit_length() - 1):
        d = 1 << k
        keep = rows < period - d if reverse else rows >= d
        sr = jnp.where(keep, pltpu.roll(xr, n - d if reverse else d, 0), 0.0)
        si = jnp.where(keep, pltpu.roll(xi, n - d if reverse else d, 0), 0.0)
        pr, pi = pw_re[first_power + k:first_power + k + 1, lanes], pw_im[first_power + k:first_power + k + 1, lanes]
        xr, xi = xr + (pr * sr - pi * si), xi + (pr * si + pi * sr)
    return xr, xi


def _scan_tile(xr, xi, tb, lanes, reverse):
    pw_re, pw_im, w8_re, w8_im, wb_re, wb_im, carry_re, carry_im, sr, si = tb
    tt = xr.shape[0]
    nb = tt // SUB
    nq = sr.shape[0]
    cols = [slice(q * LANES, (q + 1) * LANES) for q in range(nq)]
    for q in range(nq):
        sr[q] = xr[:, cols[q]]
        si[q] = xi[:, cols[q]]
    order = list(range(SUB - 2, -1, -1)) if reverse else list(range(1, SUB))
    ends_r, ends_i = [], []
    for q in range(nq):
        ql = slice(lanes.start + q * LANES, lanes.start + (q + 1) * LANES)
        lr, li = pw_re[0:1, ql], pw_im[0:1, ql]
        first_row = pl.ds(SUB - 1 if reverse else 0, nb, stride=SUB)
        pr, pi = sr[q, first_row, :], si[q, first_row, :]
        for r in order:
            rows = pl.ds(r, nb, stride=SUB)
            pr, pi = sr[q, rows, :] + (lr * pr - li * pi), si[q, rows, :] + (lr * pi + li * pr)
            sr[q, rows, :] = pr
            si[q, rows, :] = pi
        ends_r.append(pr)
        ends_i.append(pi)
    er, ei = jnp.concatenate(ends_r, axis=1), jnp.concatenate(ends_i, axis=1)
    er, ei = _doubling(er, ei, pw_re, pw_im, lanes, 3, nb, reverse)
    car, cai = carry_re[:, lanes], carry_im[:, lanes]
    wbr, wbi = wb_re[:, lanes], wb_im[:, lanes]
    er = er + (wbr * car - wbi * cai)
    ei = ei + (wbr * cai + wbi * car)
    out_block = 0 if reverse else nb - 1
    carry_re[:, lanes] = er[out_block:out_block + 1, :]
    carry_im[:, lanes] = ei[out_block:out_block + 1, :]
    blocks = lax.broadcasted_iota(jnp.int32, (nb, 1), 0)
    first = blocks == (nb - 1 if reverse else 0)
    cr = jnp.where(first, car, pltpu.roll(er, nb - 1 if reverse else 1, 0))
    ci = jnp.where(first, cai, pltpu.roll(ei, nb - 1 if reverse else 1, 0))
    for r in range(SUB):
        wr, wi = w8_re[r:r + 1, lanes], w8_im[r:r + 1, lanes]
        add_r, add_i = wr * cr - wi * ci, wr * ci + wi * cr
        for q in range(nq):
            sr[q, pl.ds(r, nb, stride=SUB), :] += add_r[:, cols[q]]
            si[q, pl.ds(r, nb, stride=SUB), :] += add_i[:, cols[q]]
    hr = jnp.concatenate([sr[q] for q in range(nq)], axis=1)
    hi = jnp.concatenate([si[q] for q in range(nq)], axis=1)
    return hr, hi, car, cai


def _scan_scratch(tt, NS):
    nb = tt // SUB
    return [pltpu.VMEM((8, NS), F32), pltpu.VMEM((8, NS), F32), pltpu.VMEM((SUB, NS), F32), pltpu.VMEM((SUB, NS), F32),
            pltpu.VMEM((nb, NS), F32), pltpu.VMEM((nb, NS), F32), pltpu.VMEM((1, NS), F32), pltpu.VMEM((1, NS), F32),
            pltpu.VMEM((SLAB_ST // LANES, tt, LANES), F32), pltpu.VMEM((SLAB_ST // LANES, tt, LANES), F32)]


def _scan_init(lr, li, tb, reverse):
    pw_re, pw_im, w8_re, w8_im, wb_re, wb_im, carry_re, carry_im, sr, _ = tb
    nb = wb_re.shape[0]
    carry_re[...] = jnp.zeros_like(carry_re)
    carry_im[...] = jnp.zeros_like(carry_im)
    pr, pi = lr, li
    for k in range(3 + nb.bit_length() - 1):
        pw_re[k:k + 1, :] = pr
        pw_im[k:k + 1, :] = pi
        if k == 3:
            l8r, l8i = pr, pi
        pr, pi = pr * pr - pi * pi, 2.0 * pr * pi
    pr, pi = lr, li
    for r in range(SUB):
        row = SUB - 1 - r if reverse else r
        w8_re[row:row + 1, :] = pr
        w8_im[row:row + 1, :] = pi
        pr, pi = pr * lr - pi * li, pr * li + pi * lr
    pr, pi = l8r, l8i
    for b in range(nb):
        row = nb - 1 - b if reverse else b
        wb_re[row:row + 1, :] = pr
        wb_im[row:row + 1, :] = pi
        pr, pi = pr * l8r - pi * l8i, pr * l8i + pi * l8r


def _ssm_tiles(T, Lc):
    tt = min(128, Lc)
    return tt, T // tt, Lc // tt


def _ssm_fwd(name, u, bbd, cbd_re, cbd_im, lam_re, lam_im, coef_re, coef_im, Lc, reverse):
    T, W = u.shape
    nslab = W // SLAB_CH
    NS = nslab * SLAB_ST
    tt, nt, nc = _ssm_tiles(T, Lc)
    if reverse:
        tile = lambda s: jnp.where(s < nc, nc - 1 - s, nt - 1 - (s - nc))
    else:
        tile = lambda s: s

    def body(u_ref, b_ref, cr_ref, ci_ref, lr_ref, li_ref, kr_ref, ki_ref, hr_ref, hi_ref, y_ref, *tb):
        @pl.when(pl.program_id(0) == 0)
        def _():
            _scan_init(lr_ref[...], li_ref[...], tb, reverse)

        for j in range(nslab):
            lanes = slice(j * SLAB_ST, (j + 1) * SLAB_ST)
            bu = jnp.dot(u_ref[:, j * SLAB_CH:(j + 1) * SLAB_CH], b_ref[j], preferred_element_type=F32)
            br, bi = bu[:, :SLAB_ST], bu[:, SLAB_ST:]
            kr, ki = kr_ref[:, lanes], ki_ref[:, lanes]
            hr, hi, _, _ = _scan_tile(kr * br - ki * bi, kr * bi + ki * br, tb, lanes, reverse)
            hrb, hib = hr.astype(BF16), hi.astype(BF16)
            hr_ref[:, lanes] = hrb
            hi_ref[:, lanes] = hib
            y_ref[:, j * SLAB_CH:(j + 1) * SLAB_CH] = (
                jnp.dot(hrb, cr_ref[j], preferred_element_type=F32)
                - jnp.dot(hib, ci_ref[j], preferred_element_type=F32))

    whole3 = lambda arr: pl.BlockSpec(arr.shape, lambda s: (0, 0, 0))
    vec = pl.BlockSpec((1, NS), lambda s: (0, 0))
    return pl.pallas_call(
        body, name=name, grid=(nt,),
        in_specs=[pl.BlockSpec((tt, W), lambda s: (tile(s), 0)), whole3(bbd), whole3(cbd_re), whole3(cbd_im),
                  vec, vec, vec, vec],
        out_specs=[pl.BlockSpec((tt, NS), lambda s: (tile(s), 0)), pl.BlockSpec((tt, NS), lambda s: (tile(s), 0)),
                   pl.BlockSpec((tt, W), lambda s: (tile(s), 0))],
        out_shape=[jax.ShapeDtypeStruct((T, NS), BF16), jax.ShapeDtypeStruct((T, NS), BF16),
                   jax.ShapeDtypeStruct((T, W), F32)],
        scratch_shapes=_scan_scratch(tt, NS),
        compiler_params=_cparams())(u, bbd, cbd_re, cbd_im, lam_re, lam_im, coef_re, coef_im)


def _ssm_bwd(name, dy, h_re, h_im, u, bbd, bbdt_re, bbdt_im, cbdt_re, cbdt_im, lam_re, lam_im,
             coef_re, coef_im, Lc, reverse):
    T, W = u.shape
    nslab = W // SLAB_CH
    NS = nslab * SLAB_ST
    tt, nt, nc = _ssm_tiles(T, Lc)
    adj_reverse = not reverse
    if reverse:
        tile = lambda s: jnp.where(s < nt - nc, nc + s, s - (nt - nc))
    else:
        tile = lambda s: nt - 1 - s

    def body(dy_ref, hr_ref, hi_ref, u_ref, b_ref, btr_ref, bti_ref, ctr_ref, cti_ref, lr_ref, li_ref,
             kr_ref, ki_ref, du_ref, dlr_ref, dli_ref, dkr_ref, dki_ref, dbf_ref, dcrf_ref, dcif_ref,
             db_ref, dcr_ref, dci_ref, *tb):
        @pl.when(pl.program_id(0) == 0)
        def _():
            _scan_init(lr_ref[...], -li_ref[...], tb, adj_reverse)
            for ref in (dlr_ref, dli_ref, dkr_ref, dki_ref, db_ref, dcr_ref, dci_ref):
                ref[...] = jnp.zeros_like(ref)

        rows = lax.broadcasted_iota(jnp.int32, (tt, 1), 0)
        far_row = tt - 1 if adj_reverse else 0
        tn_dims = (((0,), (0,)), ((), ()))
        for j in range(nslab):
            lanes = slice(j * SLAB_ST, (j + 1) * SLAB_ST)
            chans = slice(j * SLAB_CH, (j + 1) * SLAB_CH)
            dys, us = dy_ref[:, chans], u_ref[:, chans]
            er = jnp.dot(dys, ctr_ref[j], preferred_element_type=F32)
            ei = -jnp.dot(dys, cti_ref[j], preferred_element_type=F32)
            ar, ai, car, cai = _scan_tile(er, ei, tb, lanes, adj_reverse)
            shift = tt - 1 if adj_reverse else 1
            nr = jnp.where(rows == far_row, car, pltpu.roll(ar, shift, 0))
            ni = jnp.where(rows == far_row, cai, pltpu.roll(ai, shift, 0))
            hrb, hib = hr_ref[:, lanes], hi_ref[:, lanes]
            hr, hi = hrb.astype(F32), hib.astype(F32)
            dlr_ref[:, lanes] += jnp.sum(nr * hr + ni * hi, axis=0, keepdims=True)
            dli_ref[:, lanes] += jnp.sum(ni * hr - nr * hi, axis=0, keepdims=True)
            bu = jnp.dot(us, b_ref[j], preferred_element_type=F32)
            br, bi = bu[:, :SLAB_ST], bu[:, SLAB_ST:]
            dkr_ref[:, lanes] += jnp.sum(ar * br + ai * bi, axis=0, keepdims=True)
            dki_ref[:, lanes] += jnp.sum(ai * br - ar * bi, axis=0, keepdims=True)
            kr, ki = kr_ref[:, lanes], ki_ref[:, lanes]
            dbr = (ar * kr + ai * ki).astype(BF16)
            dbi = (ai * kr - ar * ki).astype(BF16)
            du_ref[:, chans] = (jnp.dot(dbr, btr_ref[j], preferred_element_type=F32)
                                + jnp.dot(dbi, bti_ref[j], preferred_element_type=F32))
            db_ref[j, :, :SLAB_ST] += lax.dot_general(us, dbr, tn_dims, preferred_element_type=F32)
            db_ref[j, :, SLAB_ST:] += lax.dot_general(us, dbi, tn_dims, preferred_element_type=F32)
            dcr_ref[j] += lax.dot_general(hrb, dys, tn_dims, preferred_element_type=F32)
            dci_ref[j] -= lax.dot_general(hib, dys, tn_dims, preferred_element_type=F32)

        @pl.when(pl.program_id(0) == nt - 1)
        def _():
            def iota(shape, axis):
                return lax.broadcasted_iota(jnp.int32, shape, axis)

            sg, ss = SSM_GROUP.bit_length() - 1, SSM_STATE.bit_length() - 1
            b_mask = (iota((SLAB_CH, SLAB_ST), 0) >> sg) == (iota((SLAB_CH, SLAB_ST), 1) >> ss)
            c_mask = (iota((SLAB_ST, SLAB_CH), 0) >> ss) == (iota((SLAB_ST, SLAB_CH), 1) >> sg)
            fold = jnp.where((iota((SLAB_ST, SSM_STATE), 0) & (SSM_STATE - 1)) == iota((SLAB_ST, SSM_STATE), 1),
                             1.0, 0.0).astype(BF16)
            fold_t = jnp.where((iota((SSM_STATE, SLAB_ST), 1) & (SSM_STATE - 1)) == iota((SSM_STATE, SLAB_ST), 0),
                               1.0, 0.0).astype(BF16)

            def exact_dot(a, b, a_is_value):
                terms = _split3(a if a_is_value else b)
                acc = None
                for t in terms:
                    part = jnp.dot(t, b, preferred_element_type=F32) if a_is_value else jnp.dot(a, t, preferred_element_type=F32)
                    acc = part if acc is None else acc + part
                return acc

            for j in range(nslab):
                dbj = db_ref[j]
                dbf_ref[j, :, :SSM_STATE] = exact_dot(jnp.where(b_mask, dbj[:, :SLAB_ST], 0.0), fold, True)
                dbf_ref[j, :, SSM_STATE:] = exact_dot(jnp.where(b_mask, dbj[:, SLAB_ST:], 0.0), fold, True)
                dcrf_ref[j] = exact_dot(fold_t, jnp.where(c_mask, dcr_ref[j], 0.0), False)
                dcif_ref[j] = exact_dot(fold_t, jnp.where(c_mask, dci_ref[j], 0.0), False)

    whole3 = lambda arr: pl.BlockSpec(arr.shape, lambda s: (0, 0, 0))
    vec = pl.BlockSpec((1, NS), lambda s: (0, 0))
    row_w = pl.BlockSpec((tt, W), lambda s: (tile(s), 0))
    row_s = pl.BlockSpec((tt, NS), lambda s: (tile(s), 0))
    dbf = jax.ShapeDtypeStruct((nslab, SLAB_CH, 2 * SSM_STATE), F32)
    dcf = jax.ShapeDtypeStruct((nslab, SSM_STATE, SLAB_CH), F32)
    return pl.pallas_call(
        body, name=name, grid=(nt,),
        in_specs=[row_w, row_s, row_s, row_w, whole3(bbd), whole3(bbdt_re), whole3(bbdt_im), whole3(cbdt_re),
                  whole3(cbdt_im), vec, vec, vec, vec],
        out_specs=[row_w, vec, vec, vec, vec, whole3(dbf), whole3(dcf), whole3(dcf)],
        out_shape=[jax.ShapeDtypeStruct((T, W), F32)] + [jax.ShapeDtypeStruct((1, NS), F32)] * 4 + [dbf, dcf, dcf],
        scratch_shapes=[pltpu.VMEM(bbd.shape, F32), pltpu.VMEM(bbdt_re.shape, F32), pltpu.VMEM(bbdt_re.shape, F32)]
        + _scan_scratch(tt, NS),
        compiler_params=_cparams())(dy, h_re, h_im, u, bbd, bbdt_re, bbdt_im, cbdt_re, cbdt_im,
                                    lam_re, lam_im, coef_re, coef_im)


def _zoh_math(a_re, a_im, log_dt):
    dt = jnp.exp(log_dt)
    mag = jnp.exp(a_re * dt)
    lb_re = mag * jnp.cos(a_im * dt)
    lb_im = mag * jnp.sin(a_im * dt)
    den = a_re * a_re + a_im * a_im
    coef_re = ((lb_re - 1.0) * a_re + lb_im * a_im) / den
    coef_im = (lb_im * a_re - (lb_re - 1.0) * a_im) / den
    return lb_re, lb_im, coef_re, coef_im


def _zoh_fwd(a_re, a_im, log_dt):
    def body(ar, ai, ld, o0, o1, o2, o3):
        for ref, val in zip((o0, o1, o2, o3), _zoh_math(ar[...], ai[...], ld[...])):
            ref[...] = val

    return pl.pallas_call(body, name="zoh_fwd", out_shape=[jax.ShapeDtypeStruct(a_re.shape, F32)] * 4,
                          compiler_params=_cparams())(a_re, a_im, log_dt)


def _zoh_bwd(a_re, a_im, log_dt, cots):
    def body(ar, ai, ld, c0, c1, c2, c3, o0, o1, o2):
        _, vjp = jax.vjp(_zoh_math, ar[...], ai[...], ld[...])
        for ref, val in zip((o0, o1, o2), vjp((c0[...], c1[...], c2[...], c3[...]))):
            ref[...] = val

    return pl.pallas_call(
        body, name="zoh_bwd",
        out_shape=[jax.ShapeDtypeStruct(a_re.shape, F32), jax.ShapeDtypeStruct(a_re.shape, F32),
                   jax.ShapeDtypeStruct(log_dt.shape, F32)],
        compiler_params=_cparams())(a_re, a_im, log_dt, *cots)


def _adamw_outer(name, w, m, v, acts, cots):
    D, N = w.shape[1:]
    tm = LANES
    dims = (((0,), (0,)), ((), ()))

    def body(a_ref, b_ref, w_ref, m_ref, v_ref, g_ref, d_ref, nm_ref, nv_ref):
        a = a_ref[...]
        aa = _split3(a * _sigmoid(a))
        bb = _split3(b_ref[...])
        g = None
        for ia in range(3):
            for ib in range(3 - ia):
                t = lax.dot_general(aa[ia], bb[ib], dims, preferred_element_type=F32)
                g = t if g is None else g + t
        g_ref[...] = g
        d_ref[...], nm_ref[...], nv_ref[...] = _adamw_math(w_ref[...], g, m_ref[...], v_ref[...])

    tile = pl.BlockSpec((None, tm, N), lambda i: (0, i, 0))
    return pl.pallas_call(
        body, name=name, grid=(D // tm,),
        in_specs=[pl.BlockSpec((16, tm), lambda i: (0, i)), pl.BlockSpec((16, N), lambda i: (0, 0)), tile, tile, tile],
        out_specs=[tile] * 4, out_shape=[jax.ShapeDtypeStruct(w.shape, F32)] * 4,
        compiler_params=_cparams())(acts, cots, w, m, v)


def _adamw_math(w, g, m, v):
    m = ADAM_B1 * m + (1.0 - ADAM_B1) * g
    v = ADAM_B2 * v + (1.0 - ADAM_B2) * (g * g)
    m_hat = m / (1.0 - ADAM_B1 ** ADAM_STEP)
    v_hat = v / (1.0 - ADAM_B2 ** ADAM_STEP)
    delta = -ADAM_LR * (m_hat / (jnp.sqrt(v_hat) + ADAM_EPS) + ADAM_WD * w)
    return delta, m, v


def _adamw(name, w, m, v, gparts):
    R, C = w.shape[-2:]
    kind = 'row1' if w.ndim == 3 else 'row'
    tr = _div(R, max(8, 524288 // C), mult=8)

    def fn(i, wv, mv, vv, *gs):
        g = gs[0]
        for extra in gs[1:]:
            g = g + extra
        return (g,) + _adamw_math(wv, g, mv, vv)

    return _rowk(name, fn, R, tr, [(w, kind), (m, kind), (v, kind)] + [(g, 'row') for g in gparts],
                 [(w.shape, F32, kind)] * 4)


def _adamw_whole(name, ws, ms, vs, gs):
    n = len(ws)

    def body(*refs):
        for k in range(n):
            g = refs[3 * n + k][...]
            res = (g,) + _adamw_math(refs[k][...], g, refs[n + k][...], refs[2 * n + k][...])
            for q in range(4):
                refs[4 * n + 4 * k + q][...] = res[q]

    out = pl.pallas_call(
        body, name=name, out_shape=[jax.ShapeDtypeStruct(w.shape, F32) for w in ws for _ in range(4)],
        compiler_params=_cparams())(*ws, *ms, *vs, *gs)
    return [tuple(out[4 * k:4 * k + 4]) for k in range(n)]


def _pack(pieces, rows_mult=8):
    flat = jnp.concatenate([p.reshape(-1).astype(F32) for p in pieces])
    unit = rows_mult * PACK_W
    total = -(-flat.shape[0] // unit) * unit
    return jnp.pad(flat, (0, total - flat.shape[0])).reshape(total // PACK_W, PACK_W)


def _unpack(buf, shapes):
    flat = buf.reshape(-1)
    out, off = [], 0
    for s in shapes:
        n = math.prod(s)
        out.append(flat[off:off + n].reshape(s))
        off += n
    return out


def _bd_expand(t):
    S, g, a, b = t.shape
    eye = jnp.eye(g, dtype=t.dtype)
    return (t[:, :, :, None, :] * eye[None, :, None, :, None]).reshape(S, g * a, g * b)


def _rope_tables(L, Lc):
    rows = L // GRID_W
    row_ids = jnp.broadcast_to(jnp.arange(rows)[:, None], (rows, GRID_W)).reshape(-1).astype(F32)
    col_ids = jnp.broadcast_to(jnp.arange(GRID_W)[None, :], (rows, GRID_W)).reshape(-1).astype(F32)
    quarter = HEAD_DIM // 4
    inv_freq = ROPE_THETA ** (-jnp.arange(quarter, dtype=F32) / quarter)
    ang_r = row_ids[:, None] * inv_freq
    ang_c = col_ids[:, None] * inv_freq
    cos = jnp.concatenate([jnp.cos(ang_r), jnp.cos(ang_r), jnp.cos(ang_c), jnp.cos(ang_c)], axis=1)
    sin = jnp.concatenate([-jnp.sin(ang_r), jnp.sin(ang_r), -jnp.sin(ang_c), jnp.sin(ang_c)], axis=1)
    cos = jnp.concatenate([jnp.ones((Lc, HEAD_DIM), F32), cos], axis=0)
    sin = jnp.concatenate([jnp.zeros((Lc, HEAD_DIM), F32), sin], axis=0)
    return cos, sin


def _rot(v):
    lane = lax.broadcasted_iota(jnp.int32, (1, HEAD_DIM), 1)
    first = (lane % (HEAD_DIM // 2)) < (HEAD_DIM // 4)
    return jnp.where(first, pltpu.roll(v, HEAD_DIM - HEAD_DIM // 4, 1), pltpu.roll(v, HEAD_DIM // 4, 1))


def _head_norm(xh, g):
    return xh * lax.rsqrt(jnp.mean(xh * xh, axis=-1, keepdims=True) + NORM_EPS) * g


def _norm_mod(xv, g, sh, sc):
    r = lax.rsqrt(jnp.mean(xv * xv, axis=-1, keepdims=True) + NORM_EPS)
    return (xv * r) * g * (1.0 + sc) + sh


def kernel(x, c, ctx, c_ctx, w_mod, b_mod, norm_g, w_ffn1_gate, w_ffn1_up, w_ffn1_down, w_in, q_norm_g, k_norm_g, ssm_a_re, ssm_a_im, ssm_log_dt, ssm_b_re, ssm_b_im, ssm_c_re, ssm_c_im, ssm_d, w_glu, b_glu, w_br_attn, w_br_ssm, w_out, w_ffn2_gate, w_ffn2_up, w_ffn2_down, loss_target, m_c_ctx, m_w_mod, m_b_mod, m_norm_g, m_w_ffn1_gate, m_w_ffn1_up, m_w_ffn1_down, m_w_in, m_q_norm_g, m_k_norm_g, m_ssm_a_re, m_ssm_a_im, m_ssm_log_dt, m_ssm_b_re, m_ssm_b_im, m_ssm_c_re, m_ssm_c_im, m_ssm_d, m_w_glu, m_b_glu, m_w_br_attn, m_w_br_ssm, m_w_out, m_w_ffn2_gate, m_w_ffn2_up, m_w_ffn2_down, v_c_ctx, v_w_mod, v_b_mod, v_norm_g, v_w_ffn1_gate, v_w_ffn1_up, v_w_ffn1_down, v_w_in, v_q_norm_g, v_k_norm_g, v_ssm_a_re, v_ssm_a_im, v_ssm_log_dt, v_ssm_b_re, v_ssm_b_im, v_ssm_c_re, v_ssm_c_im, v_ssm_d, v_w_glu, v_b_glu, v_w_br_attn, v_w_br_ssm, v_w_out, v_w_ffn2_gate, v_w_ffn2_up, v_w_ffn2_down):
    A = dict(locals())
    xi, yi, ci = _mesh_pos()
    chip = 2 * xi + yi
    me = 4 * xi + 2 * yi + ci
    L, D = x.shape[1], x.shape[2]
    Lc = ctx.shape[1]
    T = L + Lc
    F4 = w_ffn1_gate.shape[2]
    F = N_CHIPS * F4
    W, KV, Dq = D // 2, D // 4, D // 4
    G = W // SSM_GROUP
    P, E = SSM_STATE, SSM_GROUP
    NS = G * P
    nslab = W // SLAB_CH
    tr = min(256, Lc)
    ncr = Lc // tr
    assert L % tr == 0 and Lc % tr == 0 and W % SLAB_CH == 0 and D % (4 * LANES) == 0

    def sel(i, v):
        return v if v.shape[0] == 1 else jnp.where(i < ncr, v[0:1], v[1:2])

    def put(i, v, nrow):
        if nrow == 1:
            return v
        which = (i >= ncr).astype(jnp.int32)
        r2 = lax.broadcasted_iota(jnp.int32, (nrow, 1), 0)
        return jnp.where(r2 == which, jnp.broadcast_to(v, (nrow, v.shape[1])), 0.0)

    ident = lambda accs, rows, vecs, ri: [accs[0]]

    NM = w_mod.shape[2]
    first = jnp.zeros((8, D), F32).at[0].set(c[0]).at[1:4, :Dq].set(norm_g[0])
    g0 = _allgather_small("gather_c", first).reshape(N_CHIPS, 2, 8, D)
    c_all = g0[:, :, 0].reshape(N_DEV, D)
    ng = jnp.transpose(g0[:, 0, 1:4, :Dq], (1, 0, 2)).reshape(3, D)
    acts = jnp.concatenate([c_all, c_ctx[None], jnp.zeros((7, D), F32)], axis=0)
    wm = w_mod[0]
    b_shard = lax.dynamic_slice(b_mod[0], (chip * NM,), (NM,))[None]
    silu_bf = lambda a: (a * _sigmoid(a)).astype(BF16)
    to_bf = lambda b: b.astype(BF16)
    mod_part = _mm("mod_fwd", [(acts, wm, D)], 16, NM, tm=16, tn=_div(NM, 1152),
                   epi=lambda accs, rows, vecs, ri: [accs[0] + vecs[0]], outs=[(F32, False)],
                   vecs=[b_shard], a_pro=silu_bf, b_pro=to_bf)[0]
    mg = _allgather_small("gather_mod", mod_part).reshape(N_CHIPS, 2, 16, NM)[:, 0]
    mod_all = jnp.transpose(mg, (1, 0, 2)).reshape(16, N_CHIPS * NM)
    mod_x = lax.dynamic_slice(mod_all, (me, 0), (1, 9 * D))
    mod_c = jnp.where(jnp.arange(9 * D)[None] < 5 * D, mod_all[8:9], 0.0)
    modv = jnp.concatenate([mod_c, mod_x], axis=0)
    mv = lambda k: modv[:, k * D:(k + 1) * D]
    sh1, sc1, g1, sh2, sc2 = mv(0), mv(1), mv(2), mv(3), mv(4)
    g2, sh3, sc3, g3 = mv(5)[1:2], mv(6)[1:2], mv(7)[1:2], mv(8)[1:2]

    big = ['w_ffn1_gate', 'w_ffn1_up', 'w_ffn1_down', 'w_ffn2_gate', 'w_ffn2_up', 'w_ffn2_down',
           'w_in', 'w_glu', 'w_br_attn', 'w_br_ssm', 'w_out']
    row_sharded = {'w_ffn1_down', 'w_ffn2_down', 'w_glu', 'w_br_attn', 'w_out'}
    groups = [big[0:2], big[2:3], big[6:7], big[7:11], big[3:6]]
    chip_index = jnp.reshape(chip, (1,)).astype(jnp.int32)
    tok, gather_finish = modv, []
    pin = c
    for gi, names in enumerate(groups):
        tok, fin = _gather_split("gather_w%d" % gi, [_cast_slot("cast_" + n, A[n], chip_index, pin) for n in names], tok)
        gather_finish.append(fin)
        pin = tok
    ng = ng + tok[0:1, 0:1]
    Wt = {}

    def register(names, full):
        for n, gw in zip(names, full):
            Wt[n] = gw.reshape(N_CHIPS * gw.shape[1], gw.shape[2]) if n in row_sharded else gw

    def weights_ready(gi, after_work):
        _, lands = gather_finish[gi](after_work)
        register(groups[gi], _gather_finish("gather_w%d_pass" % gi, lands))

    def weights_pass(gi, after_work):
        _, lands = gather_finish[gi](after_work)
        tok_, fin_ = _pass_split("gather_w%d_pass" % gi, lands, after_work)
        return tok_, lambda later: register(groups[gi], fin_(later)[1])

    a_re2, a_im2 = ssm_a_re[0].reshape(2 * G, P), ssm_a_im[0].reshape(2 * G, P)
    ldt2 = ssm_log_dt[0].reshape(2 * G, 1)
    zoh = _zoh_fwd(a_re2, a_im2, ldt2)
    lam_re, lam_im, coef_re, coef_im = [[z[d * G:(d + 1) * G].reshape(1, NS) for d in range(2)] for z in zoh]
    bd_b = lambda b: _bd_expand(jnp.transpose(b, (0, 2, 1)).reshape(nslab, SLAB_GROUPS, E, P))
    bd_c = lambda cc: _bd_expand(jnp.transpose(cc, (0, 2, 1)).reshape(nslab, SLAB_GROUPS, P, E))
    bbd, bbdt_re, bbdt_im, cbd_re, cbd_im, cbdt_re, cbdt_im = [], [], [], [], [], [], []
    for d in range(2):
        br_, bi_ = bd_b(ssm_b_re[0, d]).astype(BF16), bd_b(ssm_b_im[0, d]).astype(BF16)
        cr_, ci_ = bd_c(ssm_c_re[0, d]).astype(BF16), bd_c(ssm_c_im[0, d]).astype(BF16)
        bbd.append(jnp.concatenate([br_, bi_], axis=2))
        bbdt_re.append(jnp.transpose(br_, (0, 2, 1)))
        bbdt_im.append(jnp.transpose(bi_, (0, 2, 1)))
        cbd_re.append(cr_)
        cbd_im.append(ci_)
        cbdt_re.append(jnp.transpose(cr_, (0, 2, 1)))
        cbdt_im.append(jnp.transpose(ci_, (0, 2, 1)))
    cos_t, sin_t = _rope_tables(L, Lc)
    qg, kg = q_norm_g, k_norm_g
    tiny = ['c_ctx', 'b_mod', 'norm_g', 'q_norm_g', 'k_norm_g', 'ssm_a_re', 'ssm_a_im', 'ssm_log_dt', 'ssm_d', 'b_glu']
    small = ['ssm_b_re', 'ssm_b_im', 'ssm_c_re', 'ssm_c_im']
    packs_wmv = [_pack([A[pre + n] for n in small]) for pre in ('', 'm_', 'v_')]
    prepared = packs_wmv + [cos_t, sin_t, coef_im[0], coef_im[1]] + [
        t[d][0] for t in (bbd, bbdt_re, bbdt_im, cbd_re, cbd_im, cbdt_re, cbdt_im) for d in range(2)]
    weights_ready(0, tok + sum(t[0:1, 0:1].astype(F32) for t in prepared))

    def norm_mod(name, xv, g, sh, sc):
        rows = xv.shape[0]
        return _rowk(name, lambda i, xt, gt, sht, sct: [_norm_mod(xt, gt, sel(i, sht), sel(i, sct))],
                     rows, tr, [(xv, 'row'), (g, 'vec'), (sh, 'vec'), (sc, 'vec')], [((rows, D), BF16, 'row')])[0]

    def swiglu_epi(accs, rows, vecs, ri):
        a_, b_ = accs
        return [a_, b_, a_ * _sigmoid(a_) * b_]

    def res_epi(coef):
        def epi(accs, rows, vecs, ri):
            gate = vecs[0]
            if gate.shape[0] == 2:
                gate = jnp.where(ri < Lc, gate[0:1], gate[1:2])
            return [accs[0], rows[0] + (coef * gate) * accs[0]]
        return epi

    def ffn_fwd(tag, h, xres, gate, down_ready=None):
        rows = h.shape[0]
        a_, b_, s_ = _mm(tag + "_up", [(h, Wt['w_' + tag + '_gate'], D), (h, Wt['w_' + tag + '_up'], D)], rows, F,
                         tm=_div(rows, 512), tn=F4, epi=swiglu_epi, outs=[(BF16, False), (BF16, False), (BF16, False)])
        if down_ready is not None:
            down_ready(s_)
        f_, xo = _mm(tag + "_down", [(s_, Wt['w_' + tag + '_down'], F)], rows, D, tm=_div(rows, 768),
                     tn=_div(D, 512), epi=res_epi(0.5), outs=[(F32, False), (F32, False)],
                     rows=[(xres, 0, 0)], vecs=[gate])
        return a_, b_, s_, f_, xo

    xc = jnp.concatenate([ctx[0], x[0]], axis=0)
    h1 = norm_mod("norm1", xc, ng[0:1], sh1, sc1)
    a1, b1, s1, f1, x1 = ffn_fwd("ffn1", h1, xc, g1, down_ready=lambda s_: weights_ready(1, s_))
    weights_ready(2, x1)
    h2 = norm_mod("norm2", x1, ng[1:2], sh2, sc2)
    proj = _mm("in_proj", [(h2, Wt['w_in'], D)], T, 4 * D, tm=_div(T, 768), tn=_div(D, 1024), epi=ident,
               outs=[(F32, False)])[0]
    nh, nkvh = D // HEAD_DIM, KV // HEAD_DIM

    def prep_fn(i, kt, vt, ut, qt, qgt, kgt, ct, st):
        qs = [_head_norm(qt[:, h * HEAD_DIM:(h + 1) * HEAD_DIM], qgt) for h in range(nh)]
        ks = [_head_norm(kt[:, h * HEAD_DIM:(h + 1) * HEAD_DIM], kgt) for h in range(nkvh)]
        qs = [v * ct + _rot(v) * st for v in qs]
        ks = [v * ct + _rot(v) * st for v in ks]
        return [jnp.concatenate(qs, axis=1), jnp.concatenate(ks, axis=1), vt, ut]

    qr, kr, vb, ub = _rowk(
        "qk_prep", prep_fn, T, tr,
        [(proj, ('col', KV, 0)), (proj, ('col', KV, 1)), (proj, ('col', W, 1)), (proj, ('col', D, 1)),
         (qg, 'vec'), (kg, 'vec'), (cos_t, 'row'), (sin_t, 'row')],
        [((T, D), BF16, 'row'), ((T, KV), BF16, 'row'), ((T, KV), BF16, 'row'), ((T, W), BF16, 'row')])
    _, mixer_weights = weights_pass(3, qr)
    attn = _attn_fwd(qr, kr, vb, L, Lc, D)
    hs_re, hs_im, ys = [], [], []
    lam_in = lam_re[0]
    for d in range(2):
        hr_, hi_, y_ = _ssm_fwd("ssm_fwd%d" % d, ub, bbd[d], cbd_re[d], cbd_im[d], lam_in, lam_im[d],
                                coef_re[d], coef_im[d], Lc, reverse=bool(d))
        hs_re.append(hr_)
        hs_im.append(hi_)
        ys.append(y_)
        if d == 0:
            tok_p4, ffn2_weights = weights_pass(4, y_)
            lam_in = lam_re[1] + tok_p4[0:1, 0:1]
    mixer_weights(ys[1])

    def ssm_out_fn(i, y0, y1, ut, dt):
        pre = dt * ut + y0 + y1
        yg_ = _gelu(pre)
        return [pre, yg_, yg_]

    ssm_pre, yg, ygb = _rowk(
        "ssm_out", ssm_out_fn, L, tr,
        [(ys[0], 'orow'), (ys[1], 'orow'), (proj, ('ocol', W, 1)), (ssm_d, 'vec')],
        [((L, W), F32, 'row'), ((L, W), F32, 'row'), ((L, W), BF16, 'row')], nc=ncr)

    def glu_epi(accs, rows, vecs, ri):
        z_ = accs[0] + vecs[0]
        return [z_, rows[0] * _sigmoid(z_)]

    zglu, y2 = _mm("glu", [(ygb, Wt['w_glu'], W)], L, W, tm=_div(L, 512), tn=_div(W, 512), epi=glu_epi,
                   outs=[(F32, False), (BF16, False)], rows=[(yg, 0, 0)], vecs=[b_glu])
    tnm = _div(Dq, 512)

    def merge_epi(accs, rows, vecs, ri):
        ga, gs = _sigmoid(rows[0]), _sigmoid(rows[1])
        return [accs[0], accs[1], ga * accs[0] + gs * accs[1]]

    ba, bs, merged = _mm("merge", [(attn, Wt['w_br_attn'], D), (y2, Wt['w_br_ssm'], W)], L, D, tm=tr, tn=tnm,
                         epi=merge_epi, outs=[(F32, False), (F32, False), (BF16, False)],
                         rows=[(proj, ncr, 2 * D // tnm), (proj, ncr, 3 * D // tnm)])
    mix, x2 = _mm("out_proj", [(merged, Wt['w_out'], D)], L, D, tm=tr, tn=_div(D, 1024), epi=res_epi(1.0),
                  outs=[(F32, False), (F32, False)], rows=[(x1, ncr, 0)], vecs=[g2])
    ffn2_weights(x2)
    h3 = norm_mod("norm3", x2, ng[2:3], sh3, sc3)
    a3, b3, s3, f3, x3 = ffn_fwd("ffn2", h3, x2, g3)

    def loss_fn(i, yt, tt_, ft, gt):
        diff = yt - tt_
        dy_ = diff * (1.0 / D)
        return [dy_, jnp.sum(diff * diff, axis=0, keepdims=True), (0.5 * gt) * dy_,
                jnp.sum(dy_ * ft, axis=0, keepdims=True) * 0.5]

    dy, sq, df3, dg3 = _rowk("loss", loss_fn, L, tr, [(x3, 'row'), (loss_target[0], 'row'), (f3, 'row'), (g3, 'vec')],
                             [((L, D), F32, 'row'), ((1, D), F32, 'acc'), ((L, D), BF16, 'row'), ((1, D), F32, 'acc')])
    loss = lax.psum(0.5 * jnp.sum(sq) / D, ("x", "y", "c"))

    def swiglu_bwd_epi(accs, rows, vecs, ri):
        ds_, a_, b_ = accs[0], rows[0].astype(F32), rows[1].astype(F32)
        sg = _sigmoid(a_)
        return [ds_ * b_ * (sg * (1.0 + a_ * (1.0 - sg))), ds_ * (a_ * sg)]

    def norm_mod_bwd(name, xv, g, sh, sc, dh, dres, dres_kind, branch=None, after=()):
        rows, nrow = xv.shape[0], sh.shape[0]

        def fn(i, xt, gt, sht, sct, dht, rest, *more):
            _, vjp = jax.vjp(_norm_mod, xt, gt, sel(i, sht), sel(i, sct))
            dx_, dg_, dsh_, dsc_ = vjp(dht)
            dx_ = dx_ + (jnp.where(i >= ncr, rest, 0.0) if dres_kind == 'xrow' else rest)
            out = [dx_, dg_, put(i, dsh_, nrow), put(i, dsc_, nrow)]
            if branch is not None:
                ft, gatet = more
                out += [(branch[2] * sel(i, gatet)) * dx_,
                        put(i, jnp.sum(dx_ * ft, axis=0, keepdims=True) * branch[2], gatet.shape[0])]
            return out

        ins = [(xv, 'row'), (g, 'vec'), (sh, 'vec'), (sc, 'vec'), (dh, 'row'), (dres, dres_kind)]
        outs = [((rows, D), F32, 'row'), ((1, D), F32, 'acc'), ((nrow, D), F32, 'acc'), ((nrow, D), F32, 'acc')]
        if branch is not None:
            ins += [(branch[0], 'row'), (branch[1], 'vec')]
            outs += [((rows, D), BF16, 'row'), ((branch[1].shape[0], D), F32, 'acc')]
        return _rowk(name, fn, rows, tr, ins, outs, nc=ncr, after=after)

    def ffn_bwd(tag, df, h, a_, b_, s_, wg, wu, wd, on_dwd=None):
        rows = df.shape[0]
        dwd = _mm(tag + "_dwd", [(s_, df, rows)], F, D, tm=_div(F, 512), tn=_div(D, 1024), ta=True, epi=ident,
                  outs=[(BF16, False)])[0].reshape(N_CHIPS, F4, D)
        if on_dwd is not None:
            on_dwd(dwd)
        da, db = _mm(tag + "_dact", [(df, wd, D)], rows, F, tm=_div(rows, 512), tn=F4, tb=True, epi=swiglu_bwd_epi,
                     outs=[(BF16, False), (BF16, False)], rows=[(a_, 0, 0), (b_, 0, 0)])
        dwg = _mm(tag + "_dwg", [(h, da, rows)], D, F, tm=_div(D, 512), tn=F4, ta=True, epi=ident,
                  outs=[(BF16, True)])[0]
        dwu = _mm(tag + "_dwu", [(h, db, rows)], D, F, tm=_div(D, 512), tn=F4, ta=True, epi=ident,
                  outs=[(BF16, True)])[0]
        dh = _mm(tag + "_dh", [(da, wg, F), (db, wu, F)], rows, D, tm=_div(rows, 768), tn=_div(D, 1024), nk=N_CHIPS,
                 tb=True, epi=ident, outs=[(F32, False)], summed=True)[0]
        return dh, dwg, dwu, dwd

    dh3, dwg2, dwu2, dwd2 = ffn_bwd("ffn2", df3, h3, a3, b3, s3, Wt['w_ffn2_gate'], Wt['w_ffn2_up'], Wt['w_ffn2_down'])
    tok_r1, scatter_fin1 = _scatter_split("scatter_ffn2", [dwg2, dwu2, dwd2], dg3)
    dx2, dng3, dsh3, dsc3, dmix, dg2 = norm_mod_bwd("norm3_bwd", x2, ng[2:3], sh3, sc3, dh3, dy, 'row',
                                                    branch=(mix, g2 + tok_r1[0:1, 0:1], 1.0))

    def dmerge_epi(accs, rows, vecs, ri):
        dm_, ba_, bs_ = accs[0], rows[0], rows[1]
        ga, gs = _sigmoid(rows[2]), _sigmoid(rows[3])
        return [dm_ * ga, dm_ * gs, dm_ * ba_ * ga * (1.0 - ga), dm_ * bs_ * gs * (1.0 - gs)]

    tnd = _div(D, 1024)
    dba, dbs, dga, dgs = _mm("dmerge", [(dmix, Wt['w_out'], D)], L, D, tm=tr, tn=tnd, tb=True, epi=dmerge_epi,
                             outs=[(BF16, False)] * 4,
                             rows=[(ba, 0, 0), (bs, 0, 0), (proj, ncr, 2 * D // tnd), (proj, ncr, 3 * D // tnd)])
    dwout = _mm("dw_out", [(merged, dmix, L)], D, D, tm=_div(D, 512), tn=_div(D, 1024), ta=True, epi=ident,
                outs=[(BF16, False)])[0].reshape(N_CHIPS, Dq, D)
    dattn = _mm("dattn", [(dba, Wt['w_br_attn'], D)], L, D, tm=_div(L, 512), tn=_div(D, 1024), tb=True, epi=ident,
                outs=[(BF16, False)])[0]
    dwba = _mm("dw_br_attn", [(attn, dba, L)], D, D, tm=_div(D, 512), tn=_div(D, 1024), ta=True, epi=ident,
               outs=[(BF16, False)])[0].reshape(N_CHIPS, Dq, D)
    dy2 = _mm("dy2", [(dbs, Wt['w_br_ssm'], D)], L, W, tm=_div(L, 512), tn=_div(W, 1024), nk=N_CHIPS, tb=True,
              epi=ident, outs=[(F32, False)])[0]
    dwbs = _mm("dw_br_ssm", [(y2, dbs, L)], W, D, tm=_div(W, 512), tn=_div(Dq, 512), ta=True, epi=ident,
               outs=[(BF16, True)])[0]

    def glu_bwd_fn(i, d2, ygt, zt):
        sz = _sigmoid(zt)
        dz_ = d2 * ygt * sz * (1.0 - sz)
        return [dz_, d2 * sz, jnp.sum(dz_, axis=0, keepdims=True)]

    dz, dyd, dbglu = _rowk("glu_bwd", glu_bwd_fn, L, tr, [(dy2, 'row'), (yg, 'row'), (zglu, 'row')],
                           [((L, W), BF16, 'row'), ((L, W), F32, 'row'), ((1, W), F32, 'acc')])

    def dssm_epi(accs, rows, vecs, ri):
        _, vjp = jax.vjp(_gelu, rows[1])
        ds_ = vjp(accs[0] + rows[0])[0]
        return [ds_, ds_]

    dssm, dssm_b = _mm("dssm", [(dz, Wt['w_glu'], W)], L, W, tm=_div(L, 512), tn=_div(W, 512), tb=True, epi=dssm_epi,
                       outs=[(F32, False), (BF16, False)], rows=[(dyd, 0, 0), (ssm_pre, 0, 0)])
    dwglu = _mm("dw_glu", [(ygb, dz, L)], W, W, tm=_div(W, 512), tn=_div(W, 1024), ta=True, epi=ident,
                outs=[(BF16, False)])[0].reshape(N_CHIPS, W // N_CHIPS, W)
    tok_r2a, scatter_fin2a = _scatter_split("scatter_mix", [dwglu, dwba, dwbs, dwout], dbglu)
    dssm_full = jnp.concatenate([jnp.zeros((Lc, W), BF16), dssm_b], axis=0)
    dus, dlam_re, dlam_im, dcoef_re, dcoef_im, dbf, dcf_re, dcf_im = [], [], [], [], [], [], [], []
    for d in range(2):
        r = _ssm_bwd("ssm_bwd%d" % d, dssm_full, hs_re[d], hs_im[d], ub, bbd[d], bbdt_re[d], bbdt_im[d],
                     cbdt_re[d], cbdt_im[d], lam_re[d] + tok_r2a[0:1, 0:1], lam_im[d], coef_re[d], coef_im[d], Lc,
                     reverse=bool(d))
        for lst, val in zip((dus, dlam_re, dlam_im, dcoef_re, dcoef_im, dbf, dcf_re, dcf_im), r):
            lst.append(val)
    dqr, dkr, dvf = _attn_bwd(qr, kr, vb, dattn, L, Lc, D)

    def prep_bwd_fn(i, qt, kt, ut, dqt, dkt, dvt, du0, du1, dst, dgat, dgst, dt, qgt, kgt, ct, st):
        live = i >= ncr
        dqt = jnp.where(live, dqt, 0.0)
        dst = jnp.where(live, dst, 0.0)
        dgat = jnp.where(live, dgat, jnp.zeros_like(dgat))
        dgst = jnp.where(live, dgst, jnp.zeros_like(dgst))
        dqs, dks = [], []
        dqg_ = jnp.zeros((1, HEAD_DIM), F32)
        dkg_ = jnp.zeros((1, HEAD_DIM), F32)
        for h in range(nh):
            hl = slice(h * HEAD_DIM, (h + 1) * HEAD_DIM)
            dn = dqt[:, hl] * ct + _rot(dqt[:, hl] * st)
            _, vjp = jax.vjp(_head_norm, qt[:, hl], qgt)
            dxh, dgh = vjp(dn)
            dqs.append(dxh)
            dqg_ = dqg_ + dgh
        for h in range(nkvh):
            hl = slice(h * HEAD_DIM, (h + 1) * HEAD_DIM)
            dn = dkt[:, hl] * ct + _rot(dkt[:, hl] * st)
            _, vjp = jax.vjp(_head_norm, kt[:, hl], kgt)
            dxh, dgh = vjp(dn)
            dks.append(dxh)
            dkg_ = dkg_ + dgh
        du_ = du0 + du1 + dst * dt
        dproj_ = jnp.concatenate([c_.astype(BF16) for c_ in dks + [dvt, du_] + dqs + [dgat, dgst]], axis=1)
        return [dproj_, dqg_, dkg_, jnp.sum(dst * ut, axis=0, keepdims=True)]

    dproj, dqg, dkg, dssd = _rowk(
        "qk_prep_bwd", prep_bwd_fn, T, tr,
        [(proj, ('col', D, 1)), (proj, ('col', KV, 0)), (proj, ('col', W, 1)), (dqr, 'xrow'), (dkr, 'row'),
         (dvf, 'row'), (dus[0], 'row'), (dus[1], 'row'), (dssm, 'xrow'), (dga, 'xrow'), (dgs, 'xrow'), (ssm_d, 'vec'),
         (qg, 'vec'), (kg, 'vec'), (cos_t, 'row'), (sin_t, 'row')],
        [((T, 4 * D), BF16, 'row'), ((1, HEAD_DIM), F32, 'acc'), ((1, HEAD_DIM), F32, 'acc'), ((1, W), F32, 'acc')],
        nc=ncr)
    dh2 = _mm("in_proj_dx", [(dproj, Wt['w_in'], 4 * D)], T, D, tm=_div(T, 768), tn=_div(D, 1024), nk=N_CHIPS, tb=True,
              epi=ident, outs=[(F32, False)])[0]
    dwin = _mm("in_proj_dw", [(h2, dproj, T)], D, 4 * D, tm=_div(D, 512), tn=_div(D, 1024), ta=True, epi=ident,
               outs=[(BF16, True)])[0]
    tok_r2, scatter_fin2 = _scatter_split("scatter_w_in", [dwin], dqg)
    dx1, dng2, dsh2, dsc2, df1, dg1 = norm_mod_bwd("norm2_bwd", x1, ng[1:2] + tok_r2[0:1, 0:1], sh2, sc2, dh2, dx2,
                                                   'xrow', branch=(f1, g1, 0.5))
    early = {}

    def start_down(dwd):
        early['tok'], early['fin'] = _scatter_split("scatter_ffn1_down", [dwd], dg2)

    dh1, dwg1, dwu1, dwd1 = ffn_bwd("ffn1", df1, h1, a1, b1, s1, Wt['w_ffn1_gate'], Wt['w_ffn1_up'],
                                    Wt['w_ffn1_down'], on_dwd=start_down)
    dx0, dng1, dsh1, dsc1 = norm_mod_bwd("norm1_bwd", xc, ng[0:1] + early['tok'][0:1, 0:1], sh1, sc1, dh1, dx1, 'row')
    grad_x = dx0[Lc:][None]

    zD = jnp.zeros((1, D), F32)
    dmod_x = jnp.concatenate([dsh1[1:2], dsc1[1:2], dg1[1:2], dsh2[1:2], dsc2[1:2], dg2, dsh3, dsc3, dg3], axis=1)
    dmod_c = jnp.concatenate([dsh1[0:1], dsc1[0:1], dg1[0:1], dsh2[0:1], dsc2[0:1], zD, zD, zD, zD], axis=1)
    pieces = [dmod_x, dmod_c, dng1, dng2, dng3, dqg, dkg] + dlam_re + dlam_im + dcoef_re + dcoef_im + [dssd, dbglu]
    shapes = [p_.shape for p_ in pieces]
    pack = _pack(pieces)
    RP = pack.shape[0]
    pieces_b = dbf + dcf_re + dcf_im
    shapes_b = [p_.shape for p_ in pieces_b]
    pack_b = _pack(pieces_b, rows_mult=16).astype(BF16)
    RB = pack_b.shape[0]
    tok_small, small_gathered = _allgather_split("gather_small", pack, me, dng1)
    tok_small, small_gathered_b = _allgather_split("gather_small_b", pack_b, me, tok_small)
    Rh = D // 2
    rows_half = lambda g_, h: lax.dynamic_slice(g_, (0, h * Rh, 0), (N_CHIPS, Rh, F4))
    tok_h, half_swapped = _swap_split("swap_ffn1_up_half", [rows_half(g_, 1 - ci) for g_ in (dwg1, dwu1)], tok_small)
    _, from_sibling = half_swapped(tok_h)
    chip_part = [
        _rowk("add_" + n, lambda i, p_, q_: [p_.astype(F32) + q_.astype(F32)], N_CHIPS * Rh, min(512, Rh),
              [(rows_half(g_, ci).reshape(N_CHIPS * Rh, F4), 'row'), (t_.reshape(N_CHIPS * Rh, F4), 'row')],
              [((N_CHIPS * Rh, F4), BF16, 'row')])[0].reshape(N_CHIPS, Rh, F4)
        for n, g_, t_ in zip(big[0:2], (dwg1, dwu1), from_sibling)]
    tok_r3, scatter_fin3 = _scatter_split("scatter_ffn1_up", chip_part, tok_h)
    results = {}

    def sum_group(tag, names, fin, after_work):
        sent, landed = fin(after_work)
        plane = [_sum_plane("sum_" + n, g_, rb, chip_index) for n, g_, rb in zip(names, sent, landed)]
        tok_, swapped = _swap_split("swap_" + tag, plane, chip_index)
        return tok_, (names, swapped)

    def update_group(group, after_work):
        names, swapped = group
        mine, theirs = swapped(after_work)
        for n, m_, t_ in zip(names, mine, theirs):
            results[n] = _adamw("adamw_" + n, A[n], A['m_' + n], A['v_' + n], [m_, t_])

    tok_a, grp_ffn2 = sum_group("ffn2", big[3:6], scatter_fin1, tok_r3)
    tok_b, grp_mix = sum_group("mix", big[7:11], scatter_fin2a, tok_a)
    tok_c, grp_w_in = sum_group("w_in", big[6:7], scatter_fin2, tok_b)
    update_group(grp_ffn2, tok_c)
    tok_d, grp_down = sum_group("ffn1_down", big[2:3], early['fin'], results['w_ffn2_down'][0])
    update_group(grp_mix, tok_d)
    update_group(grp_w_in, results['w_out'][0])
    update_group(grp_down, results['w_in'][0])
    allp = small_gathered(results['w_ffn1_down'][0])
    head_rows = -(-18 * D // PACK_W)
    head = allp[:, :head_rows].reshape(N_DEV, head_rows * PACK_W)
    dmx_all = head[:, :9 * D]

    def sum_rows_fn(i, t):
        s_ = t[0:1]
        for k in range(1, N_DEV):
            s_ = s_ + t[k:k + 1]
        return [s_]

    dmc_sum = _rowk("sum_dmod_c", sum_rows_fn, 1, 1, [(head[:, 9 * D:18 * D], 'vec')], [((1, 9 * D), F32, 'row')])[0]
    cots = jnp.concatenate([dmx_all, dmc_sum, jnp.zeros((7, 9 * D), F32)], axis=0)
    cots_sh = lax.dynamic_slice(cots, (0, chip * NM), (16, NM))
    part = _mm("cctx_part", [(cots_sh[8:16], wm, NM)], 8, D, tm=8, tn=_div(D, 1024), nk=NM // _div(NM, 1152), tb=True,
               epi=ident, outs=[(F32, False)], a_pro=to_bf, b_pro=to_bf)[0]
    _, cctx_gathered = _allgather_split("gather_cctx", part, me, part)

    def sum_dev_fn(i, t):
        s_ = t[0].astype(F32)
        for k in range(1, N_DEV):
            s_ = s_ + t[k].astype(F32)
        return [s_]

    tot = _rowk("sum_small", sum_dev_fn, RP, 8, [(allp, 'row3')], [((RP, PACK_W), F32, 'row')])[0]
    (t_dmod_x, t_dmod_c, t_ng1, t_ng2, t_ng3, t_qg, t_kg, t_lr0, t_lr1, t_li0, t_li1, t_kr0, t_kr1, t_ki0, t_ki1,
     t_d, t_bglu) = _unpack(tot, shapes)
    allb = small_gathered_b(tot)
    tot_b = _rowk("sum_small_b", sum_dev_fn, RB, 16, [(allb, 'row3')], [((RB, PACK_W), F32, 'row')])[0]
    t_dbf0, t_dbf1, t_dcr0, t_dcr1, t_dci0, t_dci1 = _unpack(tot_b, shapes_b)
    b_grad = lambda t, lo: jnp.transpose(t[:, :, lo:lo + P].reshape(G, E, P), (0, 2, 1))
    c_grad = lambda t: jnp.transpose(t.reshape(nslab, P, SLAB_GROUPS, E), (0, 2, 3, 1)).reshape(G, E, P)
    cat2 = lambda u0, u1: jnp.concatenate([u0.reshape(G, P), u1.reshape(G, P)], axis=0)
    g_are, g_aim, g_ldt = _zoh_bwd(a_re2, a_im2, ldt2, [cat2(t_lr0, t_lr1), cat2(t_li0, t_li1),
                                                         cat2(t_kr0, t_kr1), cat2(t_ki0, t_ki1)])
    g_bmod = _rowk("bmod_grad", lambda i, u0, u1: [u0 + u1], 1, 1, [(t_dmod_x, 'row'), (t_dmod_c, 'row')],
                   [((1, 9 * D), F32, 'row')])[0]
    results['w_mod'] = tuple(_adamw_outer("adamw_w_mod", w_mod, m_w_mod, v_w_mod, acts, cots_sh))
    done = sum(results[n][1].reshape(-1, results[n][1].shape[-1])[0:1, 0:1] for n in list(results)) + g_are[0:1, 0:1] \
        + g_bmod[0:1, 0:1]
    sent_up, landed_up = scatter_fin3(done)
    half_sums = [_sum_plane("sum_" + n, g_, rb, chip_index) for n, g_, rb in zip(big[0:2], sent_up, landed_up)]
    tok_e, halves_swapped = _swap_split("swap_ffn1_up", half_sums, chip_index)
    parts = cctx_gathered(tok_e).reshape(N_CHIPS, 2, 8, D)[:, 0, 0]

    def cctx_fn(i, pt, ct):
        ds_ = ((pt[0:1] + pt[1:2]) + pt[2:3]) + pt[3:4]
        _, vjp = jax.vjp(lambda v: v * _sigmoid(v), ct)
        return [vjp(ds_)[0]]

    g_cctx = _rowk("cctx_grad", cctx_fn, 1, 1, [(parts, 'vec'), (c_ctx[None], 'row')], [((1, D), F32, 'row')])[0]

    ng_full = jnp.concatenate([t_ng1, t_ng2, t_ng3], axis=0)
    gsmall = {
        'c_ctx': g_cctx, 'b_mod': g_bmod, 'norm_g': lax.dynamic_slice(ng_full, (0, chip * Dq), (3, Dq)),
        'q_norm_g': t_qg, 'k_norm_g': t_kg, 'ssm_a_re': g_are, 'ssm_a_im': g_aim, 'ssm_log_dt': g_ldt,
        'ssm_b_re': jnp.stack([b_grad(t_dbf0, 0), b_grad(t_dbf1, 0)]),
        'ssm_b_im': jnp.stack([b_grad(t_dbf0, P), b_grad(t_dbf1, P)]),
        'ssm_c_re': jnp.stack([c_grad(t_dcr0), c_grad(t_dcr1)]), 'ssm_c_im': jnp.stack([c_grad(t_dci0), c_grad(t_dci1)]),
        'ssm_d': t_d, 'b_glu': t_bglu}
    sshapes = [A[n].shape for n in small]
    sres = _adamw("adamw_small", packs_wmv[0], packs_wmv[1], packs_wmv[2], [_pack([gsmall[n] for n in small])])
    my_half, other_half = halves_swapped(sres[0])
    for n, m_, t_ in zip(big[0:2], my_half, other_half):
        full = lax.dynamic_update_slice(lax.dynamic_update_slice(lax.empty((D, F4), F32), m_, (ci * Rh, 0)),
                                        t_, ((1 - ci) * Rh, 0))
        results[n] = _adamw("adamw_" + n, A[n], A['m_' + n], A['v_' + n], [full])
    sres = [_unpack(b_, sshapes) for b_ in sres]
    for k, n in enumerate(small):
        results[n] = tuple(sres[q][k] for q in range(4))
    as2d = lambda v: v.reshape(1, -1) if v.ndim == 1 else v
    tres = _adamw_whole("adamw_tiny", [as2d(A[n]) for n in tiny], [as2d(A['m_' + n]) for n in tiny],
                        [as2d(A['v_' + n]) for n in tiny], [gsmall[n].reshape(as2d(A[n]).shape) for n in tiny])
    for n, res in zip(tiny, tres):
        results[n] = res

    order = ['c_ctx', 'w_mod', 'b_mod', 'norm_g', 'w_ffn1_gate', 'w_ffn1_up', 'w_ffn1_down', 'w_in', 'q_norm_g',
             'k_norm_g', 'ssm_a_re', 'ssm_a_im', 'ssm_log_dt', 'ssm_b_re', 'ssm_b_im', 'ssm_c_re', 'ssm_c_im',
             'ssm_d', 'w_glu', 'b_glu', 'w_br_attn', 'w_br_ssm', 'w_out', 'w_ffn2_gate', 'w_ffn2_up', 'w_ffn2_down']
    outs = [loss, grad_x]
    for q in range(4):
        outs += [results[n][q].reshape(A[n].shape) for n in order]
    return tuple(outs)
```
